```python
import math
import jax
import jax.numpy as jnp
from jax import lax
import numpy as np

D_MODEL = 1024
BATCH = 2
SEQ = 8192
DEPTH = 4
DEC_BATCH = 128
DEC_SEQ = 4
PAST_LEN = 2048
PAGE_SIZE = 128

F32 = jnp.float32
EPS = 1e-6
N_MIXERS = 3
N_LAYERS_A = (DEPTH + 2) // 3
N_LAYERS_B = (DEPTH + 1) // 3
N_LAYERS_C = DEPTH // 3
HEAD_DIM = 64
ROPE_THETA = 10000.0
Q_BLOCK = 128
A_HEADS = D_MODEL // HEAD_DIM
A_KV_HEADS = 4
IDX_HEADS = 8
IDX_DIM = 64
TOPK_MAX = 256
B_D_INNER = 2 * D_MODEL
B_HEADDIM = 64
B_HEADS = B_D_INNER // B_HEADDIM
B_GROUPS = 4
B_STATE = 128
B_CONV = 4
B_CONV_DIM = B_D_INNER + 2 * B_GROUPS * B_STATE
B_CHUNK = 128
C_HEADS = D_MODEL // HEAD_DIM
C_KV_HEADS = 2
CMP_BLOCK = 32
CMP_STRIDE = 16
SLC_BLOCK = 64
N_SLC = 16
WINDOW = 512
MEM_LEN = 256
X_HEADS = 4
X_HEAD_DIM = 128
D_FF = 2816

kernel_name = 'hybrid_dsa_ssd_nsa_macaron_step'


def rms_norm(x, g):
    xf = x.astype(F32)
    y = xf * lax.rsqrt(jnp.mean(xf * xf, axis=-1, keepdims=True) + EPS)
    return (y * g.astype(F32)).astype(x.dtype)


def rope(x, pos):
    half = x.shape[-1] // 2
    inv = ROPE_THETA ** (-jnp.arange(half, dtype=F32) / half)
    ang = pos.astype(F32)[:, None] * inv[None, :]
    cos, sin = jnp.cos(ang)[:, None, :], jnp.sin(ang)[:, None, :]
    xf = x.astype(F32)
    x1, x2 = xf[..., :half], xf[..., half:]
    return jnp.concatenate([x1 * cos - x2 * sin, x2 * cos + x1 * sin], axis=-1).astype(x.dtype)


def masked_softmax(s, mask):
    s = jnp.where(mask, s, -jnp.inf)
    m = jnp.max(s, axis=-1, keepdims=True)
    e = jnp.exp(s - jnp.where(jnp.isfinite(m), m, 0.0))
    den = jnp.sum(e, axis=-1, keepdims=True)
    return e / jnp.where(den > 0, den, 1.0)


def last_rows(a, n):
    t = a.shape[1]
    if t >= n:
        return a[:, t - n:]
    return jnp.pad(a, ((0, 0), (n - t, 0)) + ((0, 0),) * (a.ndim - 2))


def gather_paged_rows(pool, page_table, new, idx, head=None):
    page = pool.shape[1]
    past = page_table.shape[1] * page
    flat = pool.reshape((-1,) + pool.shape[2:])
    bi = jnp.arange(idx.shape[0]).reshape((-1,) + (1,) * (idx.ndim - 1))
    old = jnp.minimum(idx, past - 1)
    phys = page_table[bi, old // page] * page + old % page
    fresh = jnp.clip(idx - past, 0, new.shape[1] - 1)
    if head is None:
        rows_old, rows_new = flat[phys], new[bi, fresh]
    else:
        rows_old, rows_new = flat[phys, head], new[bi, fresh, head]
    sel = (idx >= past).reshape(idx.shape + (1,) * (rows_old.ndim - idx.ndim))
    return jnp.where(sel, rows_new, rows_old)


def swiglu(x, wi, wo):
    gate, up = jnp.split(x @ wi, 2, axis=-1)
    return (jax.nn.silu(gate) * up) @ wo


def mem_kv(mem, g, w_kv):
    b, m, _ = mem.shape
    k, v = jnp.split(rms_norm(mem, g) @ w_kv, 2, axis=-1)
    return k.reshape(b, m, X_HEADS, X_HEAD_DIM), v.reshape(b, m, X_HEADS, X_HEAD_DIM)


def cross_attn(h, mk, mv, w_q, w_o):
    b, t, _ = h.shape
    q = (h @ w_q).reshape(b, t, X_HEADS, X_HEAD_DIM)
    s = jnp.einsum('bthd,bmhd->bhtm', q, mk).astype(F32) * (X_HEAD_DIM ** -0.5)
    p = jax.nn.softmax(s, axis=-1).astype(h.dtype)
    return jnp.einsum('bhtm,bmhd->bthd', p, mv).reshape(b, t, -1) @ w_o


def dsa_project(h, w_in, pos):
    b, t, _ = h.shape
    widths = (A_HEADS * HEAD_DIM, A_KV_HEADS * HEAD_DIM, A_KV_HEADS * HEAD_DIM, IDX_HEADS * IDX_DIM, IDX_DIM)
    q, k, v, iq, ik, iw = jnp.split(h @ w_in, np.cumsum(widths).tolist(), axis=-1)
    q = rope(q.reshape(b, t, A_HEADS, HEAD_DIM), pos)
    k = rope(k.reshape(b, t, A_KV_HEADS, HEAD_DIM), pos)
    v = v.reshape(b, t, A_KV_HEADS, HEAD_DIM)
    iq = rope(iq.reshape(b, t, IDX_HEADS, IDX_DIM), pos)
    ik = rope(ik.reshape(b, t, 1, IDX_DIM), pos)[:, :, 0]
    iw = iw * (IDX_HEADS ** -0.5 * IDX_DIM ** -0.5)
    return q, k, v, iq, ik, iw


def indexer_select(iq, iw, ik_all, tpos, topk):
    r = jax.nn.relu(jnp.einsum('bthd,bsd->bths', iq, ik_all).astype(F32))
    score = jnp.einsum('bths,bth->bts', r, iw.astype(F32))
    causal = jnp.arange(ik_all.shape[1])[None, :] <= tpos[:, None]
    score = jnp.where(causal[None], score, -jnp.inf)
    idx = lax.top_k(score, topk)[1]
    return idx, idx <= tpos[None, :, None]


def gathered_attn(q, k_sel, v_sel, valid):
    b, t, h, d = q.shape
    g = k_sel.shape[3]
    qg = q.reshape(b, t, g, h // g, d)
    s = jnp.einsum('btgrd,btkgd->btgrk', qg, k_sel).astype(F32) * (d ** -0.5)
    p = masked_softmax(s, valid[:, :, None, None, :]).astype(q.dtype)
    return jnp.einsum('btgrk,btkgd->btgrd', p, v_sel).reshape(b, t, h * d)


def dsa_prompt(h, w_in, w_out):
    b, s, _ = h.shape
    q, k, v, iq, ik, iw = dsa_project(h, w_in, jnp.arange(s))
    topk = min(TOPK_MAX, s // 4)
    bi = jnp.arange(b)[:, None, None]

    def block(i):
        t0 = i * Q_BLOCK
        tpos = t0 + jnp.arange(Q_BLOCK)
        cut = lambda a: lax.dynamic_slice_in_dim(a, t0, Q_BLOCK, axis=1)
        idx, valid = indexer_select(cut(iq), cut(iw), ik, tpos, topk)
        return gathered_attn(cut(q), k[bi, idx], v[bi, idx], valid)

    o = lax.map(block, jnp.arange(s // Q_BLOCK))
    o = jnp.moveaxis(o, 0, 1).reshape(b, s, -1)
    return o @ w_out, (k, v, ik)


def dsa_sample(h, w_in, w_out, pool_k, pool_v, pool_idx, page_table):
    b, t, _ = h.shape
    past = page_table.shape[1] * pool_k.shape[1]
    pos = past + jnp.arange(t)
    q, k, v, iq, ik, iw = dsa_project(h, w_in, pos)
    ik_all = jnp.concatenate([pool_idx[page_table].reshape(b, past, IDX_DIM), ik], axis=1)
    idx, valid = indexer_select(iq, iw, ik_all, pos, min(TOPK_MAX, (past + t) // 4))
    k_sel = gather_paged_rows(pool_k, page_table, k, idx)
    v_sel = gather_paged_rows(pool_v, page_table, v, idx)
    return gathered_attn(q, k_sel, v_sel, valid) @ w_out, (k, v, ik)


def ssd_chunked(x, dt, a, bm, cm, h0):
    bsz, t, nh, p = x.shape
    g, n = bm.shape[2], bm.shape[3]
    r = nh // g
    l = min(B_CHUNK, t)
    nc = t // l
    xr = (x * dt[..., None]).reshape(bsz, nc, l, g, r, p)
    acs = jnp.cumsum((dt * a).reshape(bsz, nc, l, g, r), axis=2)
    br = bm.reshape(bsz, nc, l, g, n)
    cr = cm.reshape(bsz, nc, l, g, n)
    seg = acs[:, :, :, None] - acs[:, :, None, :]
    causal = jnp.tril(jnp.ones((l, l), bool))[:, :, None, None]
    decay = jnp.where(causal, jnp.exp(jnp.where(causal, seg, 0.0)), 0.0)
    cb = jnp.einsum('bcign,bcjgn->bcijg', cr, br)
    y_intra = jnp.einsum('bcijg,bcijgr,bcjgrp->bcigrp', cb, decay, xr)
    to_end = jnp.exp(acs[:, :, -1:] - acs)
    s_chunk = jnp.einsum('bclgn,bclgr,bclgrp->bcgrpn', br, to_end, xr)
    d_chunk = jnp.exp(acs[:, :, -1])

    def step(hc, inp):
        s_c, d_c = inp
        return hc * d_c[..., None, None] + s_c, hc

    h_fin, h_in = lax.scan(step, h0.reshape(bsz, g, r, p, n),
                           (jnp.moveaxis(s_chunk, 1, 0), jnp.moveaxis(d_chunk, 1, 0)))
    h_in = jnp.moveaxis(h_in, 0, 1)
    y_inter = jnp.einsum('bcign,bcigr,bcgrpn->bcigrp', cr, jnp.exp(acs), h_in)
    return (y_intra + y_inter).reshape(bsz, t, nh, p), h_fin.reshape(bsz, nh, p, n)


def mamba_mixer(h, w_in, conv_w, conv_b, dt_bias, a_log, d_skip, g_norm, w_out, conv_buf, ssm0):
    b, t, _ = h.shape
    proj = h @ w_in
    z = proj[..., :B_D_INNER]
    xbc = proj[..., B_D_INNER:B_D_INNER + B_CONV_DIM]
    dt = proj[..., B_D_INNER + B_CONV_DIM:]
    xpad = jnp.concatenate([conv_buf.astype(xbc.dtype), xbc], axis=1)
    conv = conv_b + sum(xpad[:, j:j + t] * conv_w[j] for j in range(B_CONV))
    xbc = jax.nn.silu(conv)
    new_buf = xpad[:, t:]
    gn = B_GROUPS * B_STATE
    xs = xbc[..., :B_D_INNER].reshape(b, t, B_HEADS, B_HEADDIM).astype(F32)
    bm = xbc[..., B_D_INNER:B_D_INNER + gn].reshape(b, t, B_GROUPS, B_STATE).astype(F32)
    cm = xbc[..., B_D_INNER + gn:].reshape(b, t, B_GROUPS, B_STATE).astype(F32)
    dt = jax.nn.softplus(dt.astype(F32) + dt_bias.astype(F32))
    a = -jnp.exp(a_log.astype(F32))
    y, ssm = ssd_chunked(xs, dt, a, bm, cm, ssm0.astype(F32))
    y = (y + xs * d_skip.astype(F32)[:, None]).reshape(b, t, B_D_INNER).astype(h.dtype)
    y = rms_norm(y * jax.nn.silu(z), g_norm)
    return y @ w_out, new_buf, ssm.astype(ssm0.dtype)


def nsa_project(h, w_in, pos):
    b, t, _ = h.shape
    qw, kw_ = C_HEADS * HEAD_DIM, C_KV_HEADS * HEAD_DIM
    proj = h @ w_in
    q = rope(proj[..., :qw].reshape(b, t, C_HEADS, HEAD_DIM), pos)
    kv = proj[..., qw:qw + 6 * kw_].reshape(b, t, 6, C_KV_HEADS, HEAD_DIM)
    gates = jax.nn.sigmoid(proj[..., qw + 6 * kw_:]).reshape(b, t, 3, C_KV_HEADS, C_HEADS // C_KV_HEADS)
    kc, vc, ks, vs, kw, vw = (kv[:, :, j] for j in range(6))
    return q, gates, rope(kc, pos), vc, rope(ks, pos), vs, rope(kw, pos), vw


def compress(rows, pe, w1, w2):
    b, l, g, d = rows.shape
    ch = rows.reshape(b, l // CMP_STRIDE, CMP_STRIDE, g, d)
    per = CMP_BLOCK // CMP_STRIDE
    nblk = l // CMP_STRIDE - per + 1
    blk = jnp.concatenate([ch[:, j:j + nblk] for j in range(per)], axis=2) + pe[:, None, :]
    flat = jnp.swapaxes(blk, 2, 3).reshape(b, nblk, g, CMP_BLOCK * d)
    return jax.nn.silu(flat @ w1) @ w2


def nsa_attend(q, gates, kcmp, vcmp, gather_sel, kwin, vwin, win_pos, tpos, n_keys):
    b, t, h, d = q.shape
    g = kcmp.shape[2]
    qg = q.reshape(b, t, g, h // g, d)
    scale = d ** -0.5
    n_cmp = kcmp.shape[1]
    cmp_end = jnp.arange(n_cmp) * CMP_STRIDE + (CMP_BLOCK - 1)
    cmask = (cmp_end[None, :] <= tpos[:, None])[None, :, None, None, :]
    p_cmp = masked_softmax(jnp.einsum('btgrd,bcgd->btgrc', qg, kcmp).astype(F32) * scale, cmask)
    o_cmp = jnp.einsum('btgrc,bcgd->btgrd', p_cmp.astype(q.dtype), vcmp)
    n_blk = -(-n_keys // SLC_BLOCK)
    n_sel = min(N_SLC, n_blk)
    c0 = jnp.arange(n_cmp)[:, None] * CMP_STRIDE
    s0 = jnp.arange(n_blk)[None, :] * SLC_BLOCK
    cover = ((c0 < s0 + SLC_BLOCK) & (c0 + CMP_BLOCK > s0)).astype(F32)
    imp = jnp.einsum('btgrc,cj->btgj', p_cmp, cover)
    cur = tpos // SLC_BLOCK
    jb = jnp.arange(n_blk)[None, :]
    forced = (jb == 0) | (jb == cur[:, None]) | (jb == cur[:, None] - 1)
    imp = jnp.where(forced[None, :, None, :], jnp.inf, imp)
    imp = jnp.where((jb <= cur[:, None])[None, :, None, :], imp, -jnp.inf)
    blk = lax.top_k(imp, n_sel)[1]
    rows = (blk[..., None] * SLC_BLOCK + jnp.arange(SLC_BLOCK)).reshape(b, t, g, n_sel * SLC_BLOCK)
    k_sel, v_sel = gather_sel(rows)
    row_ok = (rows <= tpos[None, :, None, None])[:, :, :, None, :]
    p_slc = masked_softmax(jnp.einsum('btgrd,btgkd->btgrk', qg, k_sel).astype(F32) * scale, row_ok)
    o_slc = jnp.einsum('btgrk,btgkd->btgrd', p_slc.astype(q.dtype), v_sel)
    dpos = tpos[:, None] - win_pos[None, :]
    wmask = ((dpos >= 0) & (dpos < WINDOW) & (win_pos[None, :] >= 0))[None, :, None, None, :]
    p_win = masked_softmax(jnp.einsum('btgrd,bwgd->btgrw', qg, kwin).astype(F32) * scale, wmask)
    o_win = jnp.einsum('btgrw,bwgd->btgrd', p_win.astype(q.dtype), vwin)
    gt = gates[..., None]
    o = gt[:, :, 0] * o_cmp + gt[:, :, 1] * o_slc + gt[:, :, 2] * o_win
    return o.reshape(b, t, h * d)


def nsa_prompt(h, w_in, w_out, cmp_k, cmp_v, w_buf):
    b, s, _ = h.shape
    q, gates, kc, vc, ks, vs, kw, vw = nsa_project(h, w_in, jnp.arange(s))
    kcmp, vcmp = compress(kc, *cmp_k), compress(vc, *cmp_v)
    pad = ((0, 0), (WINDOW, 0), (0, 0), (0, 0))
    kw_pad, vw_pad = jnp.pad(kw, pad), jnp.pad(vw, pad)
    bi = jnp.arange(b)[:, None, None, None]
    gi = jnp.arange(C_KV_HEADS)[None, None, :, None]

    def gather_sel(rows):
        r = jnp.minimum(rows, s - 1)
        return ks[bi, r, gi], vs[bi, r, gi]

    def block(i):
        t0 = i * Q_BLOCK
        cut = lambda a, n: lax.dynamic_slice_in_dim(a, t0, n, axis=1)
        win_pos = t0 - WINDOW + jnp.arange(WINDOW + Q_BLOCK)
        return nsa_attend(cut(q, Q_BLOCK), cut(gates, Q_BLOCK), kcmp, vcmp, gather_sel,
                          cut(kw_pad, WINDOW + Q_BLOCK), cut(vw_pad, WINDOW + Q_BLOCK),
                          win_pos, t0 + jnp.arange(Q_BLOCK), s)

    o = lax.map(block, jnp.arange(s // Q_BLOCK))
    o = jnp.moveaxis(o, 0, 1).reshape(b, s, -1)
    return o @ w_out, (kc, vc, ks, vs, last_rows(kw, w_buf), last_rows(vw, w_buf))


def nsa_sample(h, w_in, w_out, cmp_k, cmp_v, pool_kc, pool_vc, pool_ks, pool_vs, win_k, win_v, page_table):
    b, t, _ = h.shape
    past = page_table.shape[1] * pool_kc.shape[1]
    n_keys = past + t
    pos = past + jnp.arange(t)
    q, gates, kc, vc, ks, vs, kw, vw = nsa_project(h, w_in, pos)

    def full_rows(pool, new):
        old = pool[page_table].reshape((b, past) + pool.shape[2:])
        return jnp.concatenate([old, new], axis=1)[:, :(n_keys // CMP_STRIDE) * CMP_STRIDE]

    kcmp, vcmp = compress(full_rows(pool_kc, kc), *cmp_k), compress(full_rows(pool_vc, vc), *cmp_v)
    head = jnp.arange(C_KV_HEADS)[None, None, :, None]

    def gather_sel(rows):
        return (gather_paged_rows(pool_ks, page_table, ks, rows, head),
                gather_paged_rows(pool_vs, page_table, vs, rows, head))

    w_buf = win_k.shape[1]
    kwin = jnp.concatenate([win_k, kw], axis=1)
    vwin = jnp.concatenate([win_v, vw], axis=1)
    win_pos = past - w_buf + jnp.arange(w_buf + t)
    o = nsa_attend(q, gates, kcmp, vcmp, gather_sel, kwin, vwin, win_pos, pos, n_keys)
    return o @ w_out, (kc, vc, ks, vs, last_rows(kwin, w_buf), last_rows(vwin, w_buf))


def setup_inputs(seed: int = 0) -> dict:
    key = jax.random.key(seed)
    keys = iter(jax.random.split(key, 96))

    def nrm(shape, scale=1.0):
        return jax.random.normal(next(keys), shape, F32) * scale

    def gain(*shape):
        return 1.0 + 0.05 * jax.random.normal(next(keys), shape, F32)

    d = D_MODEL
    n_pages = PAST_LEN // PAGE_SIZE
    n_phys = (DEC_BATCH * n_pages * 5) // 4
    w_buf = min(WINDOW, PAST_LEN)
    page_table = jax.random.permutation(next(keys), n_phys)[:DEC_BATCH * n_pages]
    page_table = page_table.reshape(DEC_BATCH, n_pages).astype(jnp.int32)
    a_in = A_HEADS * HEAD_DIM + 2 * A_KV_HEADS * HEAD_DIM + IDX_HEADS * IDX_DIM + IDX_DIM + IDX_HEADS
    b_in = 2 * B_D_INNER + 2 * B_GROUPS * B_STATE + B_HEADS
    c_in = C_HEADS * HEAD_DIM + 6 * C_KV_HEADS * HEAD_DIM + 3 * C_HEADS
    xw = X_HEADS * X_HEAD_DIM
    dt0 = jnp.exp(jax.random.uniform(next(keys), (N_LAYERS_B, B_HEADS), F32, math.log(1e-3), math.log(1e-1)))
    a0 = jax.random.uniform(next(keys), (N_LAYERS_B, B_HEADS), F32, 1.0, 16.0)
    return {
        'x_prompt': nrm((BATCH, SEQ, d)),
        'x_sample': nrm((DEC_BATCH, DEC_SEQ, d)),
        'cache_a_k': nrm((N_LAYERS_A, n_phys, PAGE_SIZE, A_KV_HEADS, HEAD_DIM)),
        'cache_a_v': nrm((N_LAYERS_A, n_phys, PAGE_SIZE, A_KV_HEADS, HEAD_DIM)),
        'cache_a_idx': nrm((N_LAYERS_A, n_phys, PAGE_SIZE, IDX_DIM)),
        'state_b_ssm': nrm((N_LAYERS_B, DEC_BATCH, B_HEADS, B_HEADDIM, B_STATE), 0.5),
        'state_b_conv': nrm((N_LAYERS_B, DEC_BATCH, B_CONV - 1, B_CONV_DIM)),
        'cache_c_cmp_k': nrm((N_LAYERS_C, n_phys, PAGE_SIZE, C_KV_HEADS, HEAD_DIM)),
        'cache_c_cmp_v': nrm((N_LAYERS_C, n_phys, PAGE_SIZE, C_KV_HEADS, HEAD_DIM)),
        'cache_c_slc_k': nrm((N_LAYERS_C, n_phys, PAGE_SIZE, C_KV_HEADS, HEAD_DIM)),
        'cache_c_slc_v': nrm((N_LAYERS_C, n_phys, PAGE_SIZE, C_KV_HEADS, HEAD_DIM)),
        'cache_c_win_k': nrm((N_LAYERS_C, DEC_BATCH, w_buf, C_KV_HEADS, HEAD_DIM)),
        'cache_c_win_v': nrm((N_LAYERS_C, DEC_BATCH, w_buf, C_KV_HEADS, HEAD_DIM)),
        'cache_mem_k': nrm((DEPTH, DEC_BATCH, MEM_LEN, X_HEADS, X_HEAD_DIM)),
        'cache_mem_v': nrm((DEPTH, DEC_BATCH, MEM_LEN, X_HEADS, X_HEAD_DIM)),
        'page_table': page_table,
        'mem_prompt': nrm((BATCH, MEM_LEN, d)),
        'g_ffn1': gain(DEPTH, d),
        'ffn1_wi': nrm((DEPTH, d, 2 * D_FF), d ** -0.5),
        'ffn1_wo': nrm((DEPTH, D_FF, d), D_FF ** -0.5),
        'g_mix': gain(DEPTH, d),
        'g_xattn': gain(DEPTH, d),
        'g_mem': gain(DEPTH, d),
        'x_w_q': nrm((DEPTH, d, xw), d ** -0.5),
        'x_w_kv': nrm((DEPTH, d, 2 * xw), d ** -0.5),
        'x_w_o': nrm((DEPTH, xw, d), xw ** -0.5),
        'g_ffn2': gain(DEPTH, d),
        'ffn2_wi': nrm((DEPTH, d, 2 * D_FF), d ** -0.5),
        'ffn2_wo': nrm((DEPTH, D_FF, d), D_FF ** -0.5),
        'g_final': gain(d),
        'a_w_in': nrm((N_LAYERS_A, d, a_in), d ** -0.5),
        'a_w_out': nrm((N_LAYERS_A, A_HEADS * HEAD_DIM, d), (A_HEADS * HEAD_DIM) ** -0.5),
        'b_w_in': nrm((N_LAYERS_B, d, b_in), d ** -0.5),
        'b_conv_w': nrm((N_LAYERS_B, B_CONV, B_CONV_DIM), B_CONV ** -0.5),
        'b_conv_b': nrm((N_LAYERS_B, B_CONV_DIM), 0.01),
        'b_dt_bias': dt0 + jnp.log(-jnp.expm1(-dt0)),
        'b_a_log': jnp.log(a0),
        'b_d_skip': 1.0 + nrm((N_LAYERS_B, B_HEADS), 0.1),
        'b_g_norm': gain(N_LAYERS_B, B_D_INNER),
        'b_w_out': nrm((N_LAYERS_B, B_D_INNER, d), B_D_INNER ** -0.5),
        'c_w_in': nrm((N_LAYERS_C, d, c_in), d ** -0.5),
        'c_w_out': nrm((N_LAYERS_C, C_HEADS * HEAD_DIM, d), (C_HEADS * HEAD_DIM) ** -0.5),
        'c_pe_k': nrm((N_LAYERS_C, CMP_BLOCK, HEAD_DIM), 0.1),
        'c_w1_k': nrm((N_LAYERS_C, CMP_BLOCK * HEAD_DIM, HEAD_DIM), (CMP_BLOCK * HEAD_DIM) ** -0.5),
        'c_w2_k': nrm((N_LAYERS_C, HEAD_DIM, HEAD_DIM), HEAD_DIM ** -0.5),
        'c_pe_v': nrm((N_LAYERS_C, CMP_BLOCK, HEAD_DIM), 0.1),
        'c_w1_v': nrm((N_LAYERS_C, CMP_BLOCK * HEAD_DIM, HEAD_DIM), (CMP_BLOCK * HEAD_DIM) ** -0.5),
        'c_w2_v': nrm((N_LAYERS_C, HEAD_DIM, HEAD_DIM), HEAD_DIM ** -0.5),
    }


def reference(x_prompt, x_sample, cache_a_k, cache_a_v, cache_a_idx, state_b_ssm, state_b_conv,
              cache_c_cmp_k, cache_c_cmp_v, cache_c_slc_k, cache_c_slc_v, cache_c_win_k, cache_c_win_v,
              cache_mem_k, cache_mem_v, page_table, mem_prompt,
              g_ffn1, ffn1_wi, ffn1_wo, g_mix, g_xattn, g_mem, x_w_q, x_w_kv, x_w_o,
              g_ffn2, ffn2_wi, ffn2_wo, g_final, a_w_in, a_w_out,
              b_w_in, b_conv_w, b_conv_b, b_dt_bias, b_a_log, b_d_skip, b_g_norm, b_w_out,
              c_w_in, c_w_out, c_pe_k, c_w1_k, c_w2_k, c_pe_v, c_w1_v, c_w2_v):
    xp, xs = x_prompt, x_sample
    w_buf = cache_c_win_k.shape[2]
    ak_p, av_p, ai_p, bs_p, bc_p = [], [], [], [], []
    cck_p, ccv_p, csk_p, csv_p, cwk_p, cwv_p = [], [], [], [], [], []
    mk_p, mv_p = [], []
    ak_s, av_s, ai_s, bs_s, bc_s = [], [], [], [], []
    cck_s, ccv_s, csk_s, csv_s, cwk_s, cwv_s = [], [], [], [], [], []
    ia = ib = ic = 0
    for i in range(DEPTH):
        xp = xp + 0.5 * swiglu(rms_norm(xp, g_ffn1[i]), ffn1_wi[i], ffn1_wo[i])
        xs = xs + 0.5 * swiglu(rms_norm(xs, g_ffn1[i]), ffn1_wi[i], ffn1_wo[i])
        hp, hs = rms_norm(xp, g_mix[i]), rms_norm(xs, g_mix[i])
        kind = i % N_MIXERS
        if kind == 0:
            op, (k1, v1, i1) = dsa_prompt(hp, a_w_in[ia], a_w_out[ia])
            os_, (k2, v2, i2) = dsa_sample(hs, a_w_in[ia], a_w_out[ia], cache_a_k[ia], cache_a_v[ia],
                                           cache_a_idx[ia], page_table)
            ak_p.append(k1); av_p.append(v1); ai_p.append(i1)
            ak_s.append(k2); av_s.append(v2); ai_s.append(i2)
            ia += 1
        elif kind == 1:
            conv0 = jnp.zeros((hp.shape[0], B_CONV - 1, B_CONV_DIM), hp.dtype)
            ssm0 = jnp.zeros((hp.shape[0], B_HEADS, B_HEADDIM, B_STATE), F32)
            wts = (b_w_in[ib], b_conv_w[ib], b_conv_b[ib], b_dt_bias[ib], b_a_log[ib], b_d_skip[ib],
                   b_g_norm[ib], b_w_out[ib])
            op, c1, s1 = mamba_mixer(hp, *wts, conv0, ssm0)
            os_, c2, s2 = mamba_mixer(hs, *wts, state_b_conv[ib], state_b_ssm[ib])
            bc_p.append(c1); bs_p.append(s1); bc_s.append(c2); bs_s.append(s2)
            ib += 1
        else:
            cmp_k = (c_pe_k[ic], c_w1_k[ic], c_w2_k[ic])
            cmp_v = (c_pe_v[ic], c_w1_v[ic], c_w2_v[ic])
            op, st1 = nsa_prompt(hp, c_w_in[ic], c_w_out[ic], cmp_k, cmp_v, w_buf)
            os_, st2 = nsa_sample(hs, c_w_in[ic], c_w_out[ic], cmp_k, cmp_v, cache_c_cmp_k[ic], cache_c_cmp_v[ic],
                                  cache_c_slc_k[ic], cache_c_slc_v[ic], cache_c_win_k[ic], cache_c_win_v[ic],
                                  page_table)
            for lst, arr in zip((cck_p, ccv_p, csk_p, csv_p, cwk_p, cwv_p), st1):
                lst.append(arr)
            for lst, arr in zip((cck_s, ccv_s, csk_s, csv_s, cwk_s, cwv_s), st2):
                lst.append(arr)
            ic += 1
        xp = xp + op
        xs = xs + os_
        mk, mv = mem_kv(mem_prompt, g_mem[i], x_w_kv[i])
        mk_p.append(mk); mv_p.append(mv)
        xp = xp + cross_attn(rms_norm(xp, g_xattn[i]), mk, mv, x_w_q[i], x_w_o[i])
        xs = xs + cross_attn(rms_norm(xs, g_xattn[i]), cache_mem_k[i], cache_mem_v[i], x_w_q[i], x_w_o[i])
        xp = xp + 0.5 * swiglu(rms_norm(xp, g_ffn2[i]), ffn2_wi[i], ffn2_wo[i])
        xs = xs + 0.5 * swiglu(rms_norm(xs, g_ffn2[i]), ffn2_wi[i], ffn2_wo[i])
    y_prompt = rms_norm(xp, g_final)
    y_sample = rms_norm(xs, g_final)
    st = jnp.stack
    return (y_prompt, y_sample,
            st(ak_p), st(av_p), st(ai_p), st(bs_p), st(bc_p),
            st(cck_p), st(ccv_p), st(csk_p), st(csv_p), st(cwk_p), st(cwv_p),
            st(mk_p), st(mv_p),
            st(ak_s), st(av_s), st(ai_s), st(bs_s), st(bc_s),
            st(cck_s), st(ccv_s), st(csk_s), st(csv_s), st(cwk_s), st(cwv_s))
```

```python
import functools
import math

import jax
import jax.numpy as jnp
import numpy as np
from jax import lax
from jax.experimental import pallas as pl
from jax.experimental.pallas import tpu as pltpu

F32 = jnp.float32
BF16 = jnp.bfloat16
EPS = 1e-6

D_MODEL = 1024
DEPTH = 4
N_MIXERS = 3
HEAD_DIM = 64
ROPE_THETA = 10000.0
Q_BLOCK = 128
A_HEADS = D_MODEL // HEAD_DIM
A_KV_HEADS = 4
IDX_HEADS = 8
IDX_DIM = 64
TOPK_MAX = 256
B_D_INNER = 2 * D_MODEL
B_HEADDIM = 64
B_HEADS = B_D_INNER // B_HEADDIM
B_GROUPS = 4
B_STATE = 128
B_CONV = 4
B_CONV_DIM = B_D_INNER + 2 * B_GROUPS * B_STATE
B_CHUNK = 128
C_HEADS = D_MODEL // HEAD_DIM
C_KV_HEADS = 2
CMP_BLOCK = 32
CMP_STRIDE = 16
SLC_BLOCK = 64
N_SLC = 16
WINDOW = 512
X_HEADS = 4
X_HEAD_DIM = 128
D_FF = 2816

LANE = 128
VMEM_LIMIT = 48 * 1024 * 1024


def _nmm_kernel(x_ref, g_ref, w_ref, o_ref, xn_ref, *, norm):
    @pl.when(pl.program_id(1) == 0)
    def _():
        x = x_ref[...]
        if norm:
            ms = jnp.mean(x * x, axis=-1, keepdims=True)
            x = x * lax.rsqrt(ms + EPS) * g_ref[...]
        xn_ref[...] = x.astype(BF16)

    o_ref[...] = jnp.dot(xn_ref[...], w_ref[...], preferred_element_type=F32)


def _pick_tile(n, candidates):
    for c in candidates:
        if n % c == 0:
            return c
    return n


def norm_matmul(x, w, g=None):
    m, k = x.shape
    n = w.shape[1]
    n_pad = -(-n // LANE) * LANE
    wb = w.astype(BF16)
    if n_pad != n:
        wb = jnp.pad(wb, ((0, 0), (0, n_pad - n)))
    m_orig = m
    if m % 8 != 0:
        m = -(-m // 256) * 256
        x = jnp.pad(x, ((0, m - m_orig), (0, 0)))
    tm = _pick_tile(m, (512, 256, 128, 64, 32, 16, 8))
    tn = _pick_tile(n_pad, (512, 256, 128))
    norm = g is not None
    gg = (g if norm else jnp.ones((k,), F32)).reshape(1, k).astype(F32)
    out = pl.pallas_call(
        functools.partial(_nmm_kernel, norm=norm),
        grid=(m // tm, n_pad // tn),
        in_specs=[
            pl.BlockSpec((tm, k), lambda i, j: (i, 0)),
            pl.BlockSpec((1, k), lambda i, j: (0, 0)),
            pl.BlockSpec((k, tn), lambda i, j: (0, j)),
        ],
        out_specs=pl.BlockSpec((tm, tn), lambda i, j: (i, j)),
        out_shape=jax.ShapeDtypeStruct((m, n_pad), F32),
        scratch_shapes=[pltpu.VMEM((tm, k), BF16)],
        compiler_params=pltpu.CompilerParams(
            dimension_semantics=("parallel", "arbitrary"),
            vmem_limit_bytes=VMEM_LIMIT),
    )(x, gg, wb)
    if n_pad != n or m != m_orig:
        out = out[:m_orig, :n]
    return out


def mm(x, w, g=None):
    lead = x.shape[:-1]
    return norm_matmul(x.reshape(-1, x.shape[-1]), w, g).reshape(lead + (w.shape[1],))


def rms_norm(x, g):
    xf = x.astype(F32)
    y = xf * lax.rsqrt(jnp.mean(xf * xf, axis=-1, keepdims=True) + EPS)
    return (y * g.astype(F32)).astype(x.dtype)


def rope(x, pos):
    half = x.shape[-1] // 2
    inv = ROPE_THETA ** (-jnp.arange(half, dtype=F32) / half)
    ang = pos.astype(F32)[:, None] * inv[None, :]
    cos, sin = jnp.cos(ang)[:, None, :], jnp.sin(ang)[:, None, :]
    xf = x.astype(F32)
    x1, x2 = xf[..., :half], xf[..., half:]
    return jnp.concatenate([x1 * cos - x2 * sin, x2 * cos + x1 * sin], axis=-1).astype(x.dtype)


def masked_softmax(s, mask):
    s = jnp.where(mask, s, -jnp.inf)
    m = jnp.max(s, axis=-1, keepdims=True)
    e = jnp.exp(s - jnp.where(jnp.isfinite(m), m, 0.0))
    den = jnp.sum(e, axis=-1, keepdims=True)
    return e / jnp.where(den > 0, den, 1.0)


def last_rows(a, n):
    t = a.shape[1]
    if t >= n:
        return a[:, t - n:]
    return jnp.pad(a, ((0, 0), (n - t, 0)) + ((0, 0),) * (a.ndim - 2))


def gather_paged_rows(pool, page_table, new, idx, head=None):
    page = pool.shape[1]
    past = page_table.shape[1] * page
    flat = pool.reshape((-1,) + pool.shape[2:])
    bi = jnp.arange(idx.shape[0]).reshape((-1,) + (1,) * (idx.ndim - 1))
    old = jnp.minimum(idx, past - 1)
    phys = page_table[bi, old // page] * page + old % page
    fresh = jnp.clip(idx - past, 0, new.shape[1] - 1)
    if head is None:
        rows_old, rows_new = flat[phys], new[bi, fresh]
    else:
        rows_old, rows_new = flat[phys, head], new[bi, fresh, head]
    sel = (idx >= past).reshape(idx.shape + (1,) * (rows_old.ndim - idx.ndim))
    return jnp.where(sel, rows_new, rows_old)


def swiglu(x, g, wi, wo):
    gate, up = jnp.split(mm(x, wi, g), 2, axis=-1)
    return mm(jax.nn.silu(gate) * up, wo)


def mem_kv(mem, g, w_kv):
    b, m, _ = mem.shape
    k, v = jnp.split(mm(mem, w_kv, g), 2, axis=-1)
    return k.reshape(b, m, X_HEADS, X_HEAD_DIM), v.reshape(b, m, X_HEADS, X_HEAD_DIM)


def cross_attn(x, g, mk, mv, w_q, w_o):
    b, t, _ = x.shape
    q = mm(x, w_q, g).reshape(b, t, X_HEADS, X_HEAD_DIM)
    s = jnp.einsum('bthd,bmhd->bhtm', q, mk).astype(F32) * (X_HEAD_DIM ** -0.5)
    p = jax.nn.softmax(s, axis=-1).astype(x.dtype)
    return mm(jnp.einsum('bhtm,bmhd->bthd', p, mv).reshape(b, t, -1), w_o)


def dsa_project(x, g, w_in, pos):
    b, t, _ = x.shape
    widths = (A_HEADS * HEAD_DIM, A_KV_HEADS * HEAD_DIM, A_KV_HEADS * HEAD_DIM, IDX_HEADS * IDX_DIM, IDX_DIM)
    q, k, v, iq, ik, iw = jnp.split(mm(x, w_in, g), np.cumsum(widths).tolist(), axis=-1)
    q = rope(q.reshape(b, t, A_HEADS, HEAD_DIM), pos)
    k = rope(k.reshape(b, t, A_KV_HEADS, HEAD_DIM), pos)
    v = v.reshape(b, t, A_KV_HEADS, HEAD_DIM)
    iq = rope(iq.reshape(b, t, IDX_HEADS, IDX_DIM), pos)
    ik = rope(ik.reshape(b, t, 1, IDX_DIM), pos)[:, :, 0]
    iw = iw * (IDX_HEADS ** -0.5 * IDX_DIM ** -0.5)
    return q, k, v, iq, ik, iw


def indexer_select(iq, iw, ik_all, tpos, topk):
    r = jax.nn.relu(jnp.einsum('bthd,bsd->bths', iq, ik_all).astype(F32))
    score = jnp.einsum('bths,bth->bts', r, iw.astype(F32))
    causal = jnp.arange(ik_all.shape[1])[None, :] <= tpos[:, None]
    score = jnp.where(causal[None], score, -jnp.inf)
    idx = lax.top_k(score, topk)[1]
    return idx, idx <= tpos[None, :, None]


def gathered_attn(q, k_sel, v_sel, valid):
    b, t, h, d = q.shape
    g = k_sel.shape[3]
    qg = q.reshape(b, t, g, h // g, d)
    s = jnp.einsum('btgrd,btkgd->btgrk', qg, k_sel).astype(F32) * (d ** -0.5)
    p = masked_softmax(s, valid[:, :, None, None, :]).astype(q.dtype)
    return jnp.einsum('btgrk,btkgd->btgrd', p, v_sel).reshape(b, t, h * d)


def dsa_prompt(x, g, w_in, w_out):
    b, s, _ = x.shape
    q, k, v, iq, ik, iw = dsa_project(x, g, w_in, jnp.arange(s))
    topk = min(TOPK_MAX, s // 4)
    bi = jnp.arange(b)[:, None, None]

    def block(i):
        t0 = i * Q_BLOCK
        tpos = t0 + jnp.arange(Q_BLOCK)
        cut = lambda a: lax.dynamic_slice_in_dim(a, t0, Q_BLOCK, axis=1)
        idx, valid = indexer_select(cut(iq), cut(iw), ik, tpos, topk)
        return gathered_attn(cut(q), k[bi, idx], v[bi, idx], valid)

    o = lax.map(block, jnp.arange(s // Q_BLOCK))
    o = jnp.moveaxis(o, 0, 1).reshape(b, s, -1)
    return mm(o, w_out), (k, v, ik)


def dsa_sample(x, g, w_in, w_out, pool_k, pool_v, pool_idx, page_table):
    b, t, _ = x.shape
    past = page_table.shape[1] * pool_k.shape[1]
    pos = past + jnp.arange(t)
    q, k, v, iq, ik, iw = dsa_project(x, g, w_in, pos)
    ik_all = jnp.concatenate([pool_idx[page_table].reshape(b, past, IDX_DIM), ik], axis=1)
    idx, valid = indexer_select(iq, iw, ik_all, pos, min(TOPK_MAX, (past + t) // 4))
    k_sel = gather_paged_rows(pool_k, page_table, k, idx)
    v_sel = gather_paged_rows(pool_v, page_table, v, idx)
    return mm(gathered_attn(q, k_sel, v_sel, valid), w_out), (k, v, ik)


def ssd_chunked(x, dt, a, bm, cm, h0):
    bsz, t, nh, p = x.shape
    g, n = bm.shape[2], bm.shape[3]
    r = nh // g
    l = min(B_CHUNK, t)
    nc = t // l
    xr = (x * dt[..., None]).reshape(bsz, nc, l, g, r, p)
    acs = jnp.cumsum((dt * a).reshape(bsz, nc, l, g, r), axis=2)
    br = bm.reshape(bsz, nc, l, g, n)
    cr = cm.reshape(bsz, nc, l, g, n)
    seg = acs[:, :, :, None] - acs[:, :, None, :]
    causal = jnp.tril(jnp.ones((l, l), bool))[:, :, None, None]
    decay = jnp.where(causal, jnp.exp(jnp.where(causal, seg, 0.0)), 0.0)
    cb = jnp.einsum('bcign,bcjgn->bcijg', cr, br)
    y_intra = jnp.einsum('bcijg,bcijgr,bcjgrp->bcigrp', cb, decay, xr)
    to_end = jnp.exp(acs[:, :, -1:] - acs)
    s_chunk = jnp.einsum('bclgn,bclgr,bclgrp->bcgrpn', br, to_end, xr)
    d_chunk = jnp.exp(acs[:, :, -1])

    def step(hc, inp):
        s_c, d_c = inp
        return hc * d_c[..., None, None] + s_c, hc

    h_fin, h_in = lax.scan(step, h0.reshape(bsz, g, r, p, n),
                           (jnp.moveaxis(s_chunk, 1, 0), jnp.moveaxis(d_chunk, 1, 0)))
    h_in = jnp.moveaxis(h_in, 0, 1)
    y_inter = jnp.einsum('bcign,bcigr,bcgrpn->bcigrp', cr, jnp.exp(acs), h_in)
    return (y_intra + y_inter).reshape(bsz, t, nh, p), h_fin.reshape(bsz, nh, p, n)


def mamba_mixer(x, g, w_in, conv_w, conv_b, dt_bias, a_log, d_skip, g_norm, w_out, conv_buf, ssm0):
    b, t, _ = x.shape
    proj = mm(x, w_in, g)
    z = proj[..., :B_D_INNER]
    xbc = proj[..., B_D_INNER:B_D_INNER + B_CONV_DIM]
    dt = proj[..., B_D_INNER + B_CONV_DIM:]
    xpad = jnp.concatenate([conv_buf.astype(xbc.dtype), xbc], axis=1)
    conv = conv_b + sum(xpad[:, j:j + t] * conv_w[j] for j in range(B_CONV))
    xbc = jax.nn.silu(conv)
    new_buf = xpad[:, t:]
    gn = B_GROUPS * B_STATE
    xs = xbc[..., :B_D_INNER].reshape(b, t, B_HEADS, B_HEADDIM).astype(F32)
    bm = xbc[..., B_D_INNER:B_D_INNER + gn].reshape(b, t, B_GROUPS, B_STATE).astype(F32)
    cm = xbc[..., B_D_INNER + gn:].reshape(b, t, B_GROUPS, B_STATE).astype(F32)
    dt = jax.nn.softplus(dt.astype(F32) + dt_bias.astype(F32))
    a = -jnp.exp(a_log.astype(F32))
    y, ssm = ssd_chunked(xs, dt, a, bm, cm, ssm0.astype(F32))
    y = (y + xs * d_skip.astype(F32)[:, None]).reshape(b, t, B_D_INNER).astype(x.dtype)
    y = y * jax.nn.silu(z)
    return mm(y, w_out, g_norm), new_buf, ssm.astype(ssm0.dtype)


def nsa_project(x, g, w_in, pos):
    b, t, _ = x.shape
    qw, kw_ = C_HEADS * HEAD_DIM, C_KV_HEADS * HEAD_DIM
    proj = mm(x, w_in, g)
    q = rope(proj[..., :qw].reshape(b, t, C_HEADS, HEAD_DIM), pos)
    kv = proj[..., qw:qw + 6 * kw_].reshape(b, t, 6, C_KV_HEADS, HEAD_DIM)
    gates = jax.nn.sigmoid(proj[..., qw + 6 * kw_:]).reshape(b, t, 3, C_KV_HEADS, C_HEADS // C_KV_HEADS)
    kc, vc, ks, vs, kw, vw = (kv[:, :, j] for j in range(6))
    return q, gates, rope(kc, pos), vc, rope(ks, pos), vs, rope(kw, pos), vw


def compress(rows, pe, w1, w2):
    b, l, g, d = rows.shape
    ch = rows.reshape(b, l // CMP_STRIDE, CMP_STRIDE, g, d)
    per = CMP_BLOCK // CMP_STRIDE
    nblk = l // CMP_STRIDE - per + 1
    blk = jnp.concatenate([ch[:, j:j + nblk] for j in range(per)], axis=2) + pe[:, None, :]
    flat = jnp.swapaxes(blk, 2, 3).reshape(b, nblk, g, CMP_BLOCK * d)
    return mm(jax.nn.silu(mm(flat, w1)), w2)


def nsa_attend(q, gates, kcmp, vcmp, gather_sel, kwin, vwin, win_pos, tpos, n_keys):
    b, t, h, d = q.shape
    g = kcmp.shape[2]
    qg = q.reshape(b, t, g, h // g, d)
    scale = d ** -0.5
    n_cmp = kcmp.shape[1]
    cmp_end = jnp.arange(n_cmp) * CMP_STRIDE + (CMP_BLOCK - 1)
    cmask = (cmp_end[None, :] <= tpos[:, None])[None, :, None, None, :]
    p_cmp = masked_softmax(jnp.einsum('btgrd,bcgd->btgrc', qg, kcmp).astype(F32) * scale, cmask)
    o_cmp = jnp.einsum('btgrc,bcgd->btgrd', p_cmp.astype(q.dtype), vcmp)
    n_blk = -(-n_keys // SLC_BLOCK)
    n_sel = min(N_SLC, n_blk)
    c0 = jnp.arange(n_cmp)[:, None] * CMP_STRIDE
    s0 = jnp.arange(n_blk)[None, :] * SLC_BLOCK
    cover = ((c0 < s0 + SLC_BLOCK) & (c0 + CMP_BLOCK > s0)).astype(F32)
    imp = jnp.einsum('btgrc,cj->btgj', p_cmp, cover)
    cur = tpos // SLC_BLOCK
    jb = jnp.arange(n_blk)[None, :]
    forced = (jb == 0) | (jb == cur[:, None]) | (jb == cur[:, None] - 1)
    imp = jnp.where(forced[None, :, None, :], jnp.inf, imp)
    imp = jnp.where((jb <= cur[:, None])[None, :, None, :], imp, -jnp.inf)
    blk = lax.top_k(imp, n_sel)[1]
    rows = (blk[..., None] * SLC_BLOCK + jnp.arange(SLC_BLOCK)).reshape(b, t, g, n_sel * SLC_BLOCK)
    k_sel, v_sel = gather_sel(rows)
    row_ok = (rows <= tpos[None, :, None, None])[:, :, :, None, :]
    p_slc = masked_softmax(jnp.einsum('btgrd,btgkd->btgrk', qg, k_sel).astype(F32) * scale, row_ok)
    o_slc = jnp.einsum('btgrk,btgkd->btgrd', p_slc.astype(q.dtype), v_sel)
    dpos = tpos[:, None] - win_pos[None, :]
    wmask = ((dpos >= 0) & (dpos < WINDOW) & (win_pos[None, :] >= 0))[None, :, None, None, :]
    p_win = masked_softmax(jnp.einsum('btgrd,bwgd->btgrw', qg, kwin).astype(F32) * scale, wmask)
    o_win = jnp.einsum('btgrw,bwgd->btgrd', p_win.astype(q.dtype), vwin)
    gt = gates[..., None]
    o = gt[:, :, 0] * o_cmp + gt[:, :, 1] * o_slc + gt[:, :, 2] * o_win
    return o.reshape(b, t, h * d)


def nsa_prompt(x, g, w_in, w_out, cmp_k, cmp_v, w_buf):
    b, s, _ = x.shape
    q, gates, kc, vc, ks, vs, kw, vw = nsa_project(x, g, w_in, jnp.arange(s))
    kcmp, vcmp = compress(kc, *cmp_k), compress(vc, *cmp_v)
    pad = ((0, 0), (WINDOW, 0), (0, 0), (0, 0))
    kw_pad, vw_pad = jnp.pad(kw, pad), jnp.pad(vw, pad)
    bi = jnp.arange(b)[:, None, None, None]
    gi = jnp.arange(C_KV_HEADS)[None, None, :, None]

    def gather_sel(rows):
        r = jnp.minimum(rows, s - 1)
        return ks[bi, r, gi], vs[bi, r, gi]

    def block(i):
        t0 = i * Q_BLOCK
        cut = lambda a, n: lax.dynamic_slice_in_dim(a, t0, n, axis=1)
        win_pos = t0 - WINDOW + jnp.arange(WINDOW + Q_BLOCK)
        return nsa_attend(cut(q, Q_BLOCK), cut(gates, Q_BLOCK), kcmp, vcmp, gather_sel,
                          cut(kw_pad, WINDOW + Q_BLOCK), cut(vw_pad, WINDOW + Q_BLOCK),
                          win_pos, t0 + jnp.arange(Q_BLOCK), s)

    o = lax.map(block, jnp.arange(s // Q_BLOCK))
    o = jnp.moveaxis(o, 0, 1).reshape(b, s, -1)
    return mm(o, w_out), (kc, vc, ks, vs, last_rows(kw, w_buf), last_rows(vw, w_buf))


def nsa_sample(x, g, w_in, w_out, cmp_k, cmp_v, pool_kc, pool_vc, pool_ks, pool_vs, win_k, win_v, page_table):
    b, t, _ = x.shape
    past = page_table.shape[1] * pool_kc.shape[1]
    n_keys = past + t
    pos = past + jnp.arange(t)
    q, gates, kc, vc, ks, vs, kw, vw = nsa_project(x, g, w_in, pos)

    def full_rows(pool, new):
        old = pool[page_table].reshape((b, past) + pool.shape[2:])
        return jnp.concatenate([old, new], axis=1)[:, :(n_keys // CMP_STRIDE) * CMP_STRIDE]

    kcmp, vcmp = compress(full_rows(pool_kc, kc), *cmp_k), compress(full_rows(pool_vc, vc), *cmp_v)
    head = jnp.arange(C_KV_HEADS)[None, None, :, None]

    def gather_sel(rows):
        return (gather_paged_rows(pool_ks, page_table, ks, rows, head),
                gather_paged_rows(pool_vs, page_table, vs, rows, head))

    w_buf = win_k.shape[1]
    kwin = jnp.concatenate([win_k, kw], axis=1)
    vwin = jnp.concatenate([win_v, vw], axis=1)
    win_pos = past - w_buf + jnp.arange(w_buf + t)
    o = nsa_attend(q, gates, kcmp, vcmp, gather_sel, kwin, vwin, win_pos, pos, n_keys)
    return mm(o, w_out), (kc, vc, ks, vs, last_rows(kwin, w_buf), last_rows(vwin, w_buf))


def kernel(x_prompt, x_sample, cache_a_k, cache_a_v, cache_a_idx, state_b_ssm, state_b_conv, cache_c_cmp_k, cache_c_cmp_v, cache_c_slc_k, cache_c_slc_v, cache_c_win_k, cache_c_win_v, cache_mem_k, cache_mem_v, page_table, mem_prompt, g_ffn1, ffn1_wi, ffn1_wo, g_mix, g_xattn, g_mem, x_w_q, x_w_kv, x_w_o, g_ffn2, ffn2_wi, ffn2_wo, g_final, a_w_in, a_w_out, b_w_in, b_conv_w, b_conv_b, b_dt_bias, b_a_log, b_d_skip, b_g_norm, b_w_out, c_w_in, c_w_out, c_pe_k, c_w1_k, c_w2_k, c_pe_v, c_w1_v, c_w2_v):
    xp, xs = x_prompt, x_sample
    w_buf = cache_c_win_k.shape[2]
    ak_p, av_p, ai_p, bs_p, bc_p = [], [], [], [], []
    cck_p, ccv_p, csk_p, csv_p, cwk_p, cwv_p = [], [], [], [], [], []
    mk_p, mv_p = [], []
    ak_s, av_s, ai_s, bs_s, bc_s = [], [], [], [], []
    cck_s, ccv_s, csk_s, csv_s, cwk_s, cwv_s = [], [], [], [], [], []
    ia = ib = ic = 0
    for i in range(DEPTH):
        xp = xp + 0.5 * swiglu(xp, g_ffn1[i], ffn1_wi[i], ffn1_wo[i])
        xs = xs + 0.5 * swiglu(xs, g_ffn1[i], ffn1_wi[i], ffn1_wo[i])
        kind = i % N_MIXERS
        if kind == 0:
            op, (k1, v1, i1) = dsa_prompt(xp, g_mix[i], a_w_in[ia], a_w_out[ia])
            os_, (k2, v2, i2) = dsa_sample(xs, g_mix[i], a_w_in[ia], a_w_out[ia], cache_a_k[ia], cache_a_v[ia],
                                           cache_a_idx[ia], page_table)
            ak_p.append(k1); av_p.append(v1); ai_p.append(i1)
            ak_s.append(k2); av_s.append(v2); ai_s.append(i2)
            ia += 1
        elif kind == 1:
            conv0 = jnp.zeros((xp.shape[0], B_CONV - 1, B_CONV_DIM), xp.dtype)
            ssm0 = jnp.zeros((xp.shape[0], B_HEADS, B_HEADDIM, B_STATE), F32)
            wts = (b_w_in[ib], b_conv_w[ib], b_conv_b[ib], b_dt_bias[ib], b_a_log[ib], b_d_skip[ib],
                   b_g_norm[ib], b_w_out[ib])
            op, c1, s1 = mamba_mixer(xp, g_mix[i], *wts, conv0, ssm0)
            os_, c2, s2 = mamba_mixer(xs, g_mix[i], *wts, state_b_conv[ib], state_b_ssm[ib])
            bc_p.append(c1); bs_p.append(s1); bc_s.append(c2); bs_s.append(s2)
            ib += 1
        else:
            cmp_k = (c_pe_k[ic], c_w1_k[ic], c_w2_k[ic])
            cmp_v = (c_pe_v[ic], c_w1_v[ic], c_w2_v[ic])
            op, st1 = nsa_prompt(xp, g_mix[i], c_w_in[ic], c_w_out[ic], cmp_k, cmp_v, w_buf)
            os_, st2 = nsa_sample(xs, g_mix[i], c_w_in[ic], c_w_out[ic], cmp_k, cmp_v, cache_c_cmp_k[ic],
                                  cache_c_cmp_v[ic], cache_c_slc_k[ic], cache_c_slc_v[ic], cache_c_win_k[ic],
                                  cache_c_win_v[ic], page_table)
            for lst, arr in zip((cck_p, ccv_p, csk_p, csv_p, cwk_p, cwv_p), st1):
                lst.append(arr)
            for lst, arr in zip((cck_s, ccv_s, csk_s, csv_s, cwk_s, cwv_s), st2):
                lst.append(arr)
            ic += 1
        xp = xp + op
        xs = xs + os_
        mk, mv = mem_kv(mem_prompt, g_mem[i], x_w_kv[i])
        mk_p.append(mk); mv_p.append(mv)
        xp = xp + cross_attn(xp, g_xattn[i], mk, mv, x_w_q[i], x_w_o[i])
        xs = xs + cross_attn(xs, g_xattn[i], cache_mem_k[i], cache_mem_v[i], x_w_q[i], x_w_o[i])
        xp = xp + 0.5 * swiglu(xp, g_ffn2[i], ffn2_wi[i], ffn2_wo[i])
        xs = xs + 0.5 * swiglu(xs, g_ffn2[i], ffn2_wi[i], ffn2_wo[i])
    y_prompt = rms_norm(xp, g_final)
    y_sample = rms_norm(xs, g_final)
    st = jnp.stack
    return (y_prompt, y_sample,
            st(ak_p), st(av_p), st(ai_p), st(bs_p), st(bc_p),
            st(cck_p), st(ccv_p), st(csk_p), st(csv_p), st(cwk_p), st(cwv_p),
            st(mk_p), st(mv_p),
            st(ak_s), st(av_s), st(ai_s), st(bs_s), st(bc_s),
            st(cck_s), st(ccv_s), st(csk_s), st(csv_s), st(cwk_s), st(cwv_s))
```

```python
import functools
import math

import jax
import jax.numpy as jnp
import numpy as np
from jax import lax
from jax.experimental import pallas as pl
from jax.experimental.pallas import tpu as pltpu

F32 = jnp.float32
BF16 = jnp.bfloat16
I32 = jnp.int32
INT_MIN = -2 ** 31
EPS = 1e-6

D_MODEL = 1024
DEPTH = 4
N_MIXERS = 3
HEAD_DIM = 64
ROPE_THETA = 10000.0
Q_BLOCK = 128
A_HEADS = D_MODEL // HEAD_DIM
A_KV_HEADS = 4
IDX_HEADS = 8
IDX_DIM = 64
TOPK_MAX = 256
B_D_INNER = 2 * D_MODEL
B_HEADDIM = 64
B_HEADS = B_D_INNER // B_HEADDIM
B_GROUPS = 4
B_STATE = 128
B_CONV = 4
B_CONV_DIM = B_D_INNER + 2 * B_GROUPS * B_STATE
B_CHUNK = 128
C_HEADS = D_MODEL // HEAD_DIM
C_KV_HEADS = 2
CMP_BLOCK = 32
CMP_STRIDE = 16
SLC_BLOCK = 64
N_SLC = 16
WINDOW = 512
X_HEADS = 4
X_HEAD_DIM = 128
D_FF = 2816

LANE = 128
VMEM_LIMIT = 48 * 1024 * 1024


def _nmm_kernel(x_ref, g_ref, w_ref, o_ref, xn_ref, *, norm):
    @pl.when(pl.program_id(1) == 0)
    def _():
        x = x_ref[...]
        if norm:
            ms = jnp.mean(x * x, axis=-1, keepdims=True)
            x = x * lax.rsqrt(ms + EPS) * g_ref[...]
        xn_ref[...] = x.astype(BF16)

    o_ref[...] = jnp.dot(xn_ref[...], w_ref[...], preferred_element_type=F32)


def _pick_tile(n, candidates):
    for c in candidates:
        if n % c == 0:
            return c
    return n


def norm_matmul(x, w, g=None):
    m, k = x.shape
    n = w.shape[1]
    n_pad = -(-n // LANE) * LANE
    wb = w.astype(BF16)
    if n_pad != n:
        wb = jnp.pad(wb, ((0, 0), (0, n_pad - n)))
    m_orig = m
    if m % 8 != 0:
        m = -(-m // 256) * 256
        x = jnp.pad(x, ((0, m - m_orig), (0, 0)))
    tm = _pick_tile(m, (512, 256, 128, 64, 32, 16, 8))
    tn = _pick_tile(n_pad, (512, 256, 128))
    norm = g is not None
    gg = (g if norm else jnp.ones((k,), F32)).reshape(1, k).astype(F32)
    out = pl.pallas_call(
        functools.partial(_nmm_kernel, norm=norm),
        grid=(m // tm, n_pad // tn),
        in_specs=[
            pl.BlockSpec((tm, k), lambda i, j: (i, 0)),
            pl.BlockSpec((1, k), lambda i, j: (0, 0)),
            pl.BlockSpec((k, tn), lambda i, j: (0, j)),
        ],
        out_specs=pl.BlockSpec((tm, tn), lambda i, j: (i, j)),
        out_shape=jax.ShapeDtypeStruct((m, n_pad), F32),
        scratch_shapes=[pltpu.VMEM((tm, k), BF16)],
        compiler_params=pltpu.CompilerParams(
            dimension_semantics=("parallel", "arbitrary"),
            vmem_limit_bytes=VMEM_LIMIT),
    )(x, gg, wb)
    if n_pad != n or m != m_orig:
        out = out[:m_orig, :n]
    return out


def mm(x, w, g=None):
    lead = x.shape[:-1]
    return norm_matmul(x.reshape(-1, x.shape[-1]), w, g).reshape(lead + (w.shape[1],))


def _softmax_rows(s):
    m = jnp.max(s, axis=-1, keepdims=True)
    e = jnp.exp(s - jnp.where(m == -jnp.inf, 0.0, m))
    den = jnp.sum(e, axis=-1, keepdims=True)
    return e / jnp.where(den > 0, den, 1.0)


def _flash_step(carry, q, kt_tile, v_tile, sel, rep):
    m, l, acc = carry
    rows, tk = q.shape[0], kt_tile.shape[1]
    s = jnp.dot(q, kt_tile, preferred_element_type=F32)
    s = jnp.where(sel, s.reshape(rep, Q_BLOCK, tk), -jnp.inf).reshape(rows, tk)
    m_new = jnp.maximum(m, jnp.max(s, axis=1, keepdims=True))
    m_safe = jnp.where(m_new == -jnp.inf, 0.0, m_new)
    alpha = jnp.exp(m - m_safe)
    p = jnp.exp(s - m_safe)
    l = l * alpha + jnp.sum(p, axis=1, keepdims=True)
    pv = jnp.dot(p.astype(BF16), v_tile, preferred_element_type=F32)
    return m_new, l, acc * alpha + pv


def _flash_init(rows, hd):
    return (jnp.full((rows, 1), -jnp.inf, F32), jnp.zeros((rows, 1), F32), jnp.zeros((rows, hd), F32))


def _flash_finish(carry):
    _, l, acc = carry
    return acc / jnp.where(l > 0, l, 1.0)


def _sortable_key(score):
    bits = pltpu.bitcast(score, I32)
    key = jnp.where(bits < 0, bits ^ jnp.int32(0x7FFFFFFF), bits)
    return jnp.where(score == 0.0, 0, key)


def _count_lanes(pred_fn, key_ref, nkt, tk):
    rows = key_ref.shape[0]

    def body(kt, cnt):
        keys = key_ref[:, pl.ds(pl.multiple_of(kt * tk, tk), tk)]
        for c in range(tk // LANE):
            hit = pred_fn(keys[:, c * LANE:(c + 1) * LANE], kt * tk + c * LANE)
            cnt = cnt + jnp.where(hit, 1, 0)
        return cnt

    cnt = lax.fori_loop(0, nkt, body, jnp.zeros((rows, LANE), I32))
    return jnp.sum(cnt, axis=1, keepdims=True)


def _dsa_prompt_kernel(iq_ref, iw_ref, ikt_ref, q_ref, kt_ref, v_ref, o_ref, key_ref, *, topk, tk,
                       n_idx_heads, n_groups, rep):
    i = pl.program_id(1)
    t0 = i * Q_BLOCK
    nkt = (t0 + Q_BLOCK + tk - 1) // tk
    row = t0 + lax.broadcasted_iota(I32, (Q_BLOCK, 1), 0)
    lane_iota = lax.broadcasted_iota(I32, (Q_BLOCK, LANE), 1)

    iw = iw_ref[0]

    def score_body(kt, _):
        off = pl.multiple_of(kt * tk, tk)
        acc = jnp.zeros((Q_BLOCK, tk), F32)
        for h in range(n_idx_heads):
            r = jnp.dot(iq_ref[0, h], ikt_ref[0, :, pl.ds(off, tk)], preferred_element_type=F32)
            acc = acc + jnp.maximum(r, 0.0) * iw[:, h:h + 1]
        col = off + lax.broadcasted_iota(I32, (Q_BLOCK, tk), 1)
        key_ref[:, pl.ds(off, tk)] = jnp.where(col <= row, _sortable_key(acc), INT_MIN)
        return 0

    lax.fori_loop(0, nkt, score_body, 0)

    def bit_body(it, thr):
        cand = thr + jnp.left_shift(jnp.int32(1), 31 - it)
        cnt = _count_lanes(lambda keys, c0: keys >= cand, key_ref, nkt, tk)
        return jnp.where(cnt >= topk, cand, thr)

    thr = lax.fori_loop(0, 32, bit_body, jnp.full((Q_BLOCK, 1), INT_MIN, I32))
    many = row + 1 > topk
    thr = jnp.where(many, thr, INT_MIN + 1)

    c_gt = _count_lanes(lambda keys, c0: keys > thr, key_ref, nkt, tk)
    c_eq = _count_lanes(lambda keys, c0: keys == thr, key_ref, nkt, tk)
    need = topk - c_gt
    excess = many & (c_eq > need)
    n_col_bits = max(1, (key_ref.shape[1] - 1).bit_length())

    def tie_search():
        def jbody(it, last):
            cand = last + jnp.left_shift(jnp.int32(1), n_col_bits - 1 - it)
            cnt = _count_lanes(lambda keys, c0: (keys == thr) & (c0 + lane_iota < cand), key_ref, nkt, tk)
            return jnp.where(cnt <= need - 1, cand, last)

        return lax.fori_loop(0, n_col_bits, jbody, jnp.zeros((Q_BLOCK, 1), I32))

    any_excess = jnp.max(jnp.where(excess, 1, 0)) > 0
    last_tie = lax.cond(any_excess, tie_search, lambda: jnp.zeros((Q_BLOCK, 1), I32))
    last_tie = jnp.where(excess, last_tie, jnp.int32(2 ** 30))

    hd = q_ref.shape[-1]
    rows_g = rep * Q_BLOCK

    def attn_body(kt, carry):
        off = pl.multiple_of(kt * tk, tk)
        keys = key_ref[:, pl.ds(off, tk)]
        col = off + lax.broadcasted_iota(I32, (Q_BLOCK, tk), 1)
        sel = (keys > thr) | ((keys == thr) & (col <= last_tie))
        sel = sel[None]
        v_t = v_ref[0, pl.ds(off, tk), :]
        return tuple(
            _flash_step(carry[g], q_ref[0, g * rep:(g + 1) * rep].reshape(rows_g, hd),
                        kt_ref[0, g, :, pl.ds(off, tk)], v_t[:, g * hd:(g + 1) * hd], sel, rep)
            for g in range(n_groups))

    fin = lax.fori_loop(0, nkt, attn_body, tuple(_flash_init(rows_g, hd) for _ in range(n_groups)))
    for g in range(n_groups):
        o_ref[0, g * rep:(g + 1) * rep] = _flash_finish(fin[g]).reshape(rep, Q_BLOCK, hd)


def dsa_prompt_attention(q, k, v, iq, ik, iw, topk, tk=512):
    b, s, h, d = q.shape
    g = k.shape[2]
    hi, di = iq.shape[2], iq.shape[3]
    rep = h // g
    tk = min(tk, s)
    nq = s // Q_BLOCK
    qh = (q * (d ** -0.5)).astype(BF16).transpose(0, 2, 1, 3)
    kt = k.astype(BF16).transpose(0, 2, 3, 1)
    vv = v.astype(BF16).reshape(b, s, g * d)
    iqh = iq.astype(BF16).transpose(0, 2, 1, 3)
    ikt = ik.astype(BF16).transpose(0, 2, 1)
    kern = functools.partial(_dsa_prompt_kernel, topk=topk, tk=tk, n_idx_heads=hi, n_groups=g, rep=rep)
    o = pl.pallas_call(
        kern,
        grid=(b, nq),
        in_specs=[
            pl.BlockSpec((1, hi, Q_BLOCK, di), lambda bb, i: (bb, 0, i, 0)),
            pl.BlockSpec((1, Q_BLOCK, hi), lambda bb, i: (bb, i, 0)),
            pl.BlockSpec((1, di, s), lambda bb, i: (bb, 0, 0)),
            pl.BlockSpec((1, h, Q_BLOCK, d), lambda bb, i: (bb, 0, i, 0)),
            pl.BlockSpec((1, g, d, s), lambda bb, i: (bb, 0, 0, 0)),
            pl.BlockSpec((1, s, g * d), lambda bb, i: (bb, 0, 0)),
        ],
        out_specs=pl.BlockSpec((1, h, Q_BLOCK, d), lambda bb, i: (bb, 0, i, 0)),
        out_shape=jax.ShapeDtypeStruct((b, h, s, d), F32),
        scratch_shapes=[pltpu.VMEM((Q_BLOCK, s), I32)],
        compiler_params=pltpu.CompilerParams(
            dimension_semantics=("parallel", "arbitrary"), vmem_limit_bytes=VMEM_LIMIT),
        name="dsa_prompt_attention",
    )(iqh, iw.astype(F32), ikt, qh, kt, vv)
    return o.transpose(0, 2, 1, 3).reshape(b, s, h * d)


def _top_blocks(imp_t, n_sel):
    nb = imp_t.shape[0]
    j_iota = lax.broadcasted_iota(I32, imp_t.shape, 0)
    rank = jnp.zeros(imp_t.shape, F32)
    for k in range(nb):
        row_k = imp_t[k:k + 1, :]
        earlier = jnp.where(j_iota > k, 1.0, 0.0)
        rank = rank + jnp.where(row_k > imp_t, 1.0, jnp.where(row_k == imp_t, earlier, 0.0))
    return jnp.where(rank < n_sel, 1.0, 0.0)


def _nsa_prompt_kernel(q_ref, gate_ref, kct_ref, vc_ref, kst_ref, vs_ref, kwt_ref, vw_ref, o_ref, *,
                       tk, n_cmp, n_blk, n_groups, rep, win_len):
    i = pl.program_id(1)
    t0 = i * Q_BLOCK
    nkt = (t0 + Q_BLOCK + tk - 1) // tk
    hd = q_ref.shape[-1]
    rows = rep * Q_BLOCK
    nc_pad = kct_ref.shape[-1]
    row = t0 + lax.broadcasted_iota(I32, (Q_BLOCK, 1), 0)
    gates = jax.nn.sigmoid(gate_ref[0])
    n_heads = n_groups * rep

    c0 = lax.broadcasted_iota(I32, (nc_pad, LANE), 0) * CMP_STRIDE
    s0 = lax.broadcasted_iota(I32, (nc_pad, LANE), 1) * SLC_BLOCK
    cover = jnp.where(c0 < s0 + SLC_BLOCK, jnp.where(c0 + CMP_BLOCK > s0, 1.0, 0.0), 0.0).astype(BF16)
    c_idx = lax.broadcasted_iota(I32, (Q_BLOCK, nc_pad), 1)
    c_ok = ((c_idx < n_cmp) & (c_idx * CMP_STRIDE + (CMP_BLOCK - 1) <= row))[None]
    jb = lax.broadcasted_iota(I32, (Q_BLOCK, LANE), 1)
    cur = row // SLC_BLOCK
    forced = (jb == 0) | (jb == cur) | (jb == cur - 1)
    w_start = pl.multiple_of(jnp.maximum(t0 - WINDOW, 0), Q_BLOCK)
    w_col = w_start + lax.broadcasted_iota(I32, (Q_BLOCK, win_len), 1)
    w_d = row - w_col
    w_ok = ((w_d >= 0) & (w_d < WINDOW))[None]

    for g in range(n_groups):
        q = q_ref[0, g * rep:(g + 1) * rep].reshape(rows, hd)
        lo, hi = g * hd, (g + 1) * hd

        s = jnp.dot(q, kct_ref[0, g], preferred_element_type=F32)
        s = jnp.where(c_ok, s.reshape(rep, Q_BLOCK, nc_pad), -jnp.inf)
        p = _softmax_rows(s)
        o_cmp = jnp.dot(p.reshape(rows, nc_pad).astype(BF16), vc_ref[0, :, lo:hi], preferred_element_type=F32)

        p_sum = jnp.sum(p, axis=0)
        p_hi = p_sum.astype(BF16)
        p_lo = (p_sum - p_hi.astype(F32)).astype(BF16)
        imp = (jnp.dot(p_hi, cover, preferred_element_type=F32) + jnp.dot(p_lo, cover, preferred_element_type=F32))
        imp = jnp.where(forced, jnp.inf, imp)
        imp = jnp.where(jb <= cur, imp, -jnp.inf)
        sel_blocks = _top_blocks(imp.T[:n_blk], min(N_SLC, n_blk))
        if n_blk < LANE:
            sel_blocks = jnp.concatenate([sel_blocks, jnp.zeros((LANE - n_blk, Q_BLOCK), F32)], axis=0)
        sel_blocks = sel_blocks.T.astype(BF16)

        def slc_body(kt, carry):
            off = pl.multiple_of(kt * tk, tk)
            col = off + lax.broadcasted_iota(I32, (Q_BLOCK, tk), 1)
            blk_of_col = (off + lax.broadcasted_iota(I32, (LANE, tk), 1)) // SLC_BLOCK
            expand = jnp.where(lax.broadcasted_iota(I32, (LANE, tk), 0) == blk_of_col, 1.0, 0.0).astype(BF16)
            hit = jnp.dot(sel_blocks, expand, preferred_element_type=F32)
            sel = (jnp.where(col <= row, hit, 0.0) > 0.5)[None]
            return _flash_step(carry, q, kst_ref[0, g, :, pl.ds(off, tk)], vs_ref[0, pl.ds(off, tk), lo:hi], sel, rep)

        o_slc = _flash_finish(lax.fori_loop(0, nkt, slc_body, _flash_init(rows, hd)))

        s = jnp.dot(q, kwt_ref[0, g, :, pl.ds(w_start, win_len)], preferred_element_type=F32)
        s = jnp.where(w_ok, s.reshape(rep, Q_BLOCK, win_len), -jnp.inf)
        p = _softmax_rows(s).reshape(rows, win_len)
        o_win = jnp.dot(p.astype(BF16), vw_ref[0, pl.ds(w_start, win_len), lo:hi], preferred_element_type=F32)

        for r in range(rep):
            h = g * rep + r
            rs = slice(r * Q_BLOCK, (r + 1) * Q_BLOCK)
            o_ref[0, h] = (gates[:, h:h + 1] * o_cmp[rs] + gates[:, n_heads + h:n_heads + h + 1] * o_slc[rs]
                           + gates[:, 2 * n_heads + h:2 * n_heads + h + 1] * o_win[rs])


def nsa_prompt_attention(q, gate_logits, kcmp, vcmp, ks, vs, kw, vw, tk=512):
    b, s, h, d = q.shape
    g = ks.shape[2]
    rep = h // g
    n_cmp = kcmp.shape[1]
    nc_pad = -(-n_cmp // LANE) * LANE
    n_blk = -(-s // SLC_BLOCK)
    assert n_blk <= LANE and s % Q_BLOCK == 0
    tk = min(tk, s)
    win_len = min(WINDOW + Q_BLOCK, s)
    nq = s // Q_BLOCK
    pad_c = ((0, 0), (0, nc_pad - n_cmp), (0, 0), (0, 0))
    qh = (q * (d ** -0.5)).astype(BF16).transpose(0, 2, 1, 3)
    kct = jnp.pad(kcmp.astype(BF16), pad_c).transpose(0, 2, 3, 1)
    vcc = jnp.pad(vcmp.astype(BF16), pad_c).reshape(b, nc_pad, g * d)
    kst = ks.astype(BF16).transpose(0, 2, 3, 1)
    kwt = kw.astype(BF16).transpose(0, 2, 3, 1)
    vss = vs.astype(BF16).reshape(b, s, g * d)
    vww = vw.astype(BF16).reshape(b, s, g * d)
    kern = functools.partial(_nsa_prompt_kernel, tk=tk, n_cmp=n_cmp, n_blk=n_blk, n_groups=g, rep=rep,
                             win_len=win_len)
    full3 = lambda bb, i: (bb, 0, 0)
    full4 = lambda bb, i: (bb, 0, 0, 0)
    o = pl.pallas_call(
        kern,
        grid=(b, nq),
        in_specs=[
            pl.BlockSpec((1, h, Q_BLOCK, d), lambda bb, i: (bb, 0, i, 0)),
            pl.BlockSpec((1, Q_BLOCK, 3 * h), lambda bb, i: (bb, i, 0)),
            pl.BlockSpec((1, g, d, nc_pad), full4),
            pl.BlockSpec((1, nc_pad, g * d), full3),
            pl.BlockSpec((1, g, d, s), full4),
            pl.BlockSpec((1, s, g * d), full3),
            pl.BlockSpec((1, g, d, s), full4),
            pl.BlockSpec((1, s, g * d), full3),
        ],
        out_specs=pl.BlockSpec((1, h, Q_BLOCK, d), lambda bb, i: (bb, 0, i, 0)),
        out_shape=jax.ShapeDtypeStruct((b, h, s, d), F32),
        compiler_params=pltpu.CompilerParams(
            dimension_semantics=("parallel", "arbitrary"), vmem_limit_bytes=VMEM_LIMIT),
        name="nsa_prompt_attention",
    )(qh, gate_logits.astype(F32), kct, vcc, kst, vss, kwt, vww)
    return o.transpose(0, 2, 1, 3).reshape(b, s, h * d)


def rms_norm(x, g):
    xf = x.astype(F32)
    y = xf * lax.rsqrt(jnp.mean(xf * xf, axis=-1, keepdims=True) + EPS)
    return (y * g.astype(F32)).astype(x.dtype)


def rope(x, pos):
    half = x.shape[-1] // 2
    inv = ROPE_THETA ** (-jnp.arange(half, dtype=F32) / half)
    ang = pos.astype(F32)[:, None] * inv[None, :]
    cos, sin = jnp.cos(ang)[:, None, :], jnp.sin(ang)[:, None, :]
    xf = x.astype(F32)
    x1, x2 = xf[..., :half], xf[..., half:]
    return jnp.concatenate([x1 * cos - x2 * sin, x2 * cos + x1 * sin], axis=-1).astype(x.dtype)


def masked_softmax(s, mask):
    s = jnp.where(mask, s, -jnp.inf)
    m = jnp.max(s, axis=-1, keepdims=True)
    e = jnp.exp(s - jnp.where(jnp.isfinite(m), m, 0.0))
    den = jnp.sum(e, axis=-1, keepdims=True)
    return e / jnp.where(den > 0, den, 1.0)


def last_rows(a, n):
    t = a.shape[1]
    if t >= n:
        return a[:, t - n:]
    return jnp.pad(a, ((0, 0), (n - t, 0)) + ((0, 0),) * (a.ndim - 2))


def gather_paged_rows(pool, page_table, new, idx, head=None):
    page = pool.shape[1]
    past = page_table.shape[1] * page
    flat = pool.reshape((-1,) + pool.shape[2:])
    bi = jnp.arange(idx.shape[0]).reshape((-1,) + (1,) * (idx.ndim - 1))
    old = jnp.minimum(idx, past - 1)
    phys = page_table[bi, old // page] * page + old % page
    fresh = jnp.clip(idx - past, 0, new.shape[1] - 1)
    if head is None:
        rows_old, rows_new = flat[phys], new[bi, fresh]
    else:
        rows_old, rows_new = flat[phys, head], new[bi, fresh, head]
    sel = (idx >= past).reshape(idx.shape + (1,) * (rows_old.ndim - idx.ndim))
    return jnp.where(sel, rows_new, rows_old)


def swiglu(x, g, wi, wo):
    gate, up = jnp.split(mm(x, wi, g), 2, axis=-1)
    return mm(jax.nn.silu(gate) * up, wo)


def mem_kv(mem, g, w_kv):
    b, m, _ = mem.shape
    k, v = jnp.split(mm(mem, w_kv, g), 2, axis=-1)
    return k.reshape(b, m, X_HEADS, X_HEAD_DIM), v.reshape(b, m, X_HEADS, X_HEAD_DIM)


def cross_attn(x, g, mk, mv, w_q, w_o):
    b, t, _ = x.shape
    q = mm(x, w_q, g).reshape(b, t, X_HEADS, X_HEAD_DIM)
    s = jnp.einsum('bthd,bmhd->bhtm', q, mk).astype(F32) * (X_HEAD_DIM ** -0.5)
    p = jax.nn.softmax(s, axis=-1).astype(x.dtype)
    return mm(jnp.einsum('bhtm,bmhd->bthd', p, mv).reshape(b, t, -1), w_o)


def dsa_project(x, g, w_in, pos):
    b, t, _ = x.shape
    widths = (A_HEADS * HEAD_DIM, A_KV_HEADS * HEAD_DIM, A_KV_HEADS * HEAD_DIM, IDX_HEADS * IDX_DIM, IDX_DIM)
    q, k, v, iq, ik, iw = jnp.split(mm(x, w_in, g), np.cumsum(widths).tolist(), axis=-1)
    q = rope(q.reshape(b, t, A_HEADS, HEAD_DIM), pos)
    k = rope(k.reshape(b, t, A_KV_HEADS, HEAD_DIM), pos)
    v = v.reshape(b, t, A_KV_HEADS, HEAD_DIM)
    iq = rope(iq.reshape(b, t, IDX_HEADS, IDX_DIM), pos)
    ik = rope(ik.reshape(b, t, 1, IDX_DIM), pos)[:, :, 0]
    iw = iw * (IDX_HEADS ** -0.5 * IDX_DIM ** -0.5)
    return q, k, v, iq, ik, iw


def indexer_select(iq, iw, ik_all, tpos, topk):
    r = jax.nn.relu(jnp.einsum('bthd,bsd->bths', iq, ik_all).astype(F32))
    score = jnp.einsum('bths,bth->bts', r, iw.astype(F32))
    causal = jnp.arange(ik_all.shape[1])[None, :] <= tpos[:, None]
    score = jnp.where(causal[None], score, -jnp.inf)
    idx = lax.top_k(score, topk)[1]
    return idx, idx <= tpos[None, :, None]


def gathered_attn(q, k_sel, v_sel, valid):
    b, t, h, d = q.shape
    g = k_sel.shape[3]
    qg = q.reshape(b, t, g, h // g, d)
    s = jnp.einsum('btgrd,btkgd->btgrk', qg, k_sel).astype(F32) * (d ** -0.5)
    p = masked_softmax(s, valid[:, :, None, None, :]).astype(q.dtype)
    return jnp.einsum('btgrk,btkgd->btgrd', p, v_sel).reshape(b, t, h * d)


def dsa_prompt(x, g, w_in, w_out):
    s = x.shape[1]
    q, k, v, iq, ik, iw = dsa_project(x, g, w_in, jnp.arange(s))
    o = dsa_prompt_attention(q, k, v, iq, ik, iw, min(TOPK_MAX, s // 4))
    return mm(o, w_out), (k, v, ik)


def dsa_sample(x, g, w_in, w_out, pool_k, pool_v, pool_idx, page_table):
    b, t, _ = x.shape
    past = page_table.shape[1] * pool_k.shape[1]
    pos = past + jnp.arange(t)
    q, k, v, iq, ik, iw = dsa_project(x, g, w_in, pos)
    ik_all = jnp.concatenate([pool_idx[page_table].reshape(b, past, IDX_DIM), ik], axis=1)
    idx, valid = indexer_select(iq, iw, ik_all, pos, min(TOPK_MAX, (past + t) // 4))
    k_sel = gather_paged_rows(pool_k, page_table, k, idx)
    v_sel = gather_paged_rows(pool_v, page_table, v, idx)
    return mm(gathered_attn(q, k_sel, v_sel, valid), w_out), (k, v, ik)


def ssd_chunked(x, dt, a, bm, cm, h0):
    bsz, t, nh, p = x.shape
    g, n = bm.shape[2], bm.shape[3]
    r = nh // g
    l = min(B_CHUNK, t)
    nc = t // l
    xr = (x * dt[..., None]).reshape(bsz, nc, l, g, r, p)
    acs = jnp.cumsum((dt * a).reshape(bsz, nc, l, g, r), axis=2)
    br = bm.reshape(bsz, nc, l, g, n)
    cr = cm.reshape(bsz, nc, l, g, n)
    seg = acs[:, :, :, None] - acs[:, :, None, :]
    causal = jnp.tril(jnp.ones((l, l), bool))[:, :, None, None]
    decay = jnp.where(causal, jnp.exp(jnp.where(causal, seg, 0.0)), 0.0)
    cb = jnp.einsum('bcign,bcjgn->bcijg', cr, br)
    y_intra = jnp.einsum('bcijg,bcijgr,bcjgrp->bcigrp', cb, decay, xr)
    to_end = jnp.exp(acs[:, :, -1:] - acs)
    s_chunk = jnp.einsum('bclgn,bclgr,bclgrp->bcgrpn', br, to_end, xr)
    d_chunk = jnp.exp(acs[:, :, -1])

    def step(hc, inp):
        s_c, d_c = inp
        return hc * d_c[..., None, None] + s_c, hc

    h_fin, h_in = lax.scan(step, h0.reshape(bsz, g, r, p, n),
                           (jnp.moveaxis(s_chunk, 1, 0), jnp.moveaxis(d_chunk, 1, 0)))
    h_in = jnp.moveaxis(h_in, 0, 1)
    y_inter = jnp.einsum('bcign,bcigr,bcgrpn->bcigrp', cr, jnp.exp(acs), h_in)
    return (y_intra + y_inter).reshape(bsz, t, nh, p), h_fin.reshape(bsz, nh, p, n)


def mamba_mixer(x, g, w_in, conv_w, conv_b, dt_bias, a_log, d_skip, g_norm, w_out, conv_buf, ssm0):
    b, t, _ = x.shape
    proj = mm(x, w_in, g)
    z = proj[..., :B_D_INNER]
    xbc = proj[..., B_D_INNER:B_D_INNER + B_CONV_DIM]
    dt = proj[..., B_D_INNER + B_CONV_DIM:]
    xpad = jnp.concatenate([conv_buf.astype(xbc.dtype), xbc], axis=1)
    conv = conv_b + sum(xpad[:, j:j + t] * conv_w[j] for j in range(B_CONV))
    xbc = jax.nn.silu(conv)
    new_buf = xpad[:, t:]
    gn = B_GROUPS * B_STATE
    xs = xbc[..., :B_D_INNER].reshape(b, t, B_HEADS, B_HEADDIM).astype(F32)
    bm = xbc[..., B_D_INNER:B_D_INNER + gn].reshape(b, t, B_GROUPS, B_STATE).astype(F32)
    cm = xbc[..., B_D_INNER + gn:].reshape(b, t, B_GROUPS, B_STATE).astype(F32)
    dt = jax.nn.softplus(dt.astype(F32) + dt_bias.astype(F32))
    a = -jnp.exp(a_log.astype(F32))
    y, ssm = ssd_chunked(xs, dt, a, bm, cm, ssm0.astype(F32))
    y = (y + xs * d_skip.astype(F32)[:, None]).reshape(b, t, B_D_INNER).astype(x.dtype)
    y = y * jax.nn.silu(z)
    return mm(y, w_out, g_norm), new_buf, ssm.astype(ssm0.dtype)


def nsa_project(x, g, w_in, pos):
    b, t, _ = x.shape
    qw, kw_ = C_HEADS * HEAD_DIM, C_KV_HEADS * HEAD_DIM
    proj = mm(x, w_in, g)
    q = rope(proj[..., :qw].reshape(b, t, C_HEADS, HEAD_DIM), pos)
    kv = proj[..., qw:qw + 6 * kw_].reshape(b, t, 6, C_KV_HEADS, HEAD_DIM)
    gates = proj[..., qw + 6 * kw_:]
    kc, vc, ks, vs, kw, vw = (kv[:, :, j] for j in range(6))
    return q, gates, rope(kc, pos), vc, rope(ks, pos), vs, rope(kw, pos), vw


def compress(rows, pe, w1, w2):
    b, l, g, d = rows.shape
    ch = rows.reshape(b, l // CMP_STRIDE, CMP_STRIDE, g, d)
    per = CMP_BLOCK // CMP_STRIDE
    nblk = l // CMP_STRIDE - per + 1
    blk = jnp.concatenate([ch[:, j:j + nblk] for j in range(per)], axis=2) + pe[:, None, :]
    flat = jnp.swapaxes(blk, 2, 3).reshape(b, nblk, g, CMP_BLOCK * d)
    return mm(jax.nn.silu(mm(flat, w1)), w2)


def nsa_attend(q, gates, kcmp, vcmp, gather_sel, kwin, vwin, win_pos, tpos, n_keys):
    b, t, h, d = q.shape
    g = kcmp.shape[2]
    qg = q.reshape(b, t, g, h // g, d)
    scale = d ** -0.5
    n_cmp = kcmp.shape[1]
    cmp_end = jnp.arange(n_cmp) * CMP_STRIDE + (CMP_BLOCK - 1)
    cmask = (cmp_end[None, :] <= tpos[:, None])[None, :, None, None, :]
    p_cmp = masked_softmax(jnp.einsum('btgrd,bcgd->btgrc', qg, kcmp).astype(F32) * scale, cmask)
    o_cmp = jnp.einsum('btgrc,bcgd->btgrd', p_cmp.astype(q.dtype), vcmp)
    n_blk = -(-n_keys // SLC_BLOCK)
    n_sel = min(N_SLC, n_blk)
    c0 = jnp.arange(n_cmp)[:, None] * CMP_STRIDE
    s0 = jnp.arange(n_blk)[None, :] * SLC_BLOCK
    cover = ((c0 < s0 + SLC_BLOCK) & (c0 + CMP_BLOCK > s0)).astype(F32)
    imp = jnp.einsum('btgrc,cj->btgj', p_cmp, cover)
    cur = tpos // SLC_BLOCK
    jb = jnp.arange(n_blk)[None, :]
    forced = (jb == 0) | (jb == cur[:, None]) | (jb == cur[:, None] - 1)
    imp = jnp.where(forced[None, :, None, :], jnp.inf, imp)
    imp = jnp.where((jb <= cur[:, None])[None, :, None, :], imp, -jnp.inf)
    blk = lax.top_k(imp, n_sel)[1]
    rows = (blk[..., None] * SLC_BLOCK + jnp.arange(SLC_BLOCK)).reshape(b, t, g, n_sel * SLC_BLOCK)
    k_sel, v_sel = gather_sel(rows)
    row_ok = (rows <= tpos[None, :, None, None])[:, :, :, None, :]
    p_slc = masked_softmax(jnp.einsum('btgrd,btgkd->btgrk', qg, k_sel).astype(F32) * scale, row_ok)
    o_slc = jnp.einsum('btgrk,btgkd->btgrd', p_slc.astype(q.dtype), v_sel)
    dpos = tpos[:, None] - win_pos[None, :]
    wmask = ((dpos >= 0) & (dpos < WINDOW) & (win_pos[None, :] >= 0))[None, :, None, None, :]
    p_win = masked_softmax(jnp.einsum('btgrd,bwgd->btgrw', qg, kwin).astype(F32) * scale, wmask)
    o_win = jnp.einsum('btgrw,bwgd->btgrd', p_win.astype(q.dtype), vwin)
    gt = gates[..., None]
    o = gt[:, :, 0] * o_cmp + gt[:, :, 1] * o_slc + gt[:, :, 2] * o_win
    return o.reshape(b, t, h * d)


def nsa_prompt(x, g, w_in, w_out, cmp_k, cmp_v, w_buf):
    s = x.shape[1]
    q, gate_logits, kc, vc, ks, vs, kw, vw = nsa_project(x, g, w_in, jnp.arange(s))
    kcmp, vcmp = compress(kc, *cmp_k), compress(vc, *cmp_v)
    o = nsa_prompt_attention(q, gate_logits, kcmp, vcmp, ks, vs, kw, vw)
    return mm(o, w_out), (kc, vc, ks, vs, last_rows(kw, w_buf), last_rows(vw, w_buf))


def nsa_sample(x, g, w_in, w_out, cmp_k, cmp_v, pool_kc, pool_vc, pool_ks, pool_vs, win_k, win_v, page_table):
    b, t, _ = x.shape
    past = page_table.shape[1] * pool_kc.shape[1]
    n_keys = past + t
    pos = past + jnp.arange(t)
    q, gate_logits, kc, vc, ks, vs, kw, vw = nsa_project(x, g, w_in, pos)
    gates = jax.nn.sigmoid(gate_logits).reshape(b, t, 3, C_KV_HEADS, C_HEADS // C_KV_HEADS)

    def full_rows(pool, new):
        old = pool[page_table].reshape((b, past) + pool.shape[2:])
        return jnp.concatenate([old, new], axis=1)[:, :(n_keys // CMP_STRIDE) * CMP_STRIDE]

    kcmp, vcmp = compress(full_rows(pool_kc, kc), *cmp_k), compress(full_rows(pool_vc, vc), *cmp_v)
    head = jnp.arange(C_KV_HEADS)[None, None, :, None]

    def gather_sel(rows):
        return (gather_paged_rows(pool_ks, page_table, ks, rows, head),
                gather_paged_rows(pool_vs, page_table, vs, rows, head))

    w_buf = win_k.shape[1]
    kwin = jnp.concatenate([win_k, kw], axis=1)
    vwin = jnp.concatenate([win_v, vw], axis=1)
    win_pos = past - w_buf + jnp.arange(w_buf + t)
    o = nsa_attend(q, gates, kcmp, vcmp, gather_sel, kwin, vwin, win_pos, pos, n_keys)
    return mm(o, w_out), (kc, vc, ks, vs, last_rows(kwin, w_buf), last_rows(vwin, w_buf))


def kernel(x_prompt, x_sample, cache_a_k, cache_a_v, cache_a_idx, state_b_ssm, state_b_conv, cache_c_cmp_k, cache_c_cmp_v, cache_c_slc_k, cache_c_slc_v, cache_c_win_k, cache_c_win_v, cache_mem_k, cache_mem_v, page_table, mem_prompt, g_ffn1, ffn1_wi, ffn1_wo, g_mix, g_xattn, g_mem, x_w_q, x_w_kv, x_w_o, g_ffn2, ffn2_wi, ffn2_wo, g_final, a_w_in, a_w_out, b_w_in, b_conv_w, b_conv_b, b_dt_bias, b_a_log, b_d_skip, b_g_norm, b_w_out, c_w_in, c_w_out, c_pe_k, c_w1_k, c_w2_k, c_pe_v, c_w1_v, c_w2_v):
    xp, xs = x_prompt, x_sample
    w_buf = cache_c_win_k.shape[2]
    ak_p, av_p, ai_p, bs_p, bc_p = [], [], [], [], []
    cck_p, ccv_p, csk_p, csv_p, cwk_p, cwv_p = [], [], [], [], [], []
    mk_p, mv_p = [], []
    ak_s, av_s, ai_s, bs_s, bc_s = [], [], [], [], []
    cck_s, ccv_s, csk_s, csv_s, cwk_s, cwv_s = [], [], [], [], [], []
    ia = ib = ic = 0
    for i in range(DEPTH):
        xp = xp + 0.5 * swiglu(xp, g_ffn1[i], ffn1_wi[i], ffn1_wo[i])
        xs = xs + 0.5 * swiglu(xs, g_ffn1[i], ffn1_wi[i], ffn1_wo[i])
        kind = i % N_MIXERS
        if kind == 0:
            op, (k1, v1, i1) = dsa_prompt(xp, g_mix[i], a_w_in[ia], a_w_out[ia])
            os_, (k2, v2, i2) = dsa_sample(xs, g_mix[i], a_w_in[ia], a_w_out[ia], cache_a_k[ia], cache_a_v[ia],
                                           cache_a_idx[ia], page_table)
            ak_p.append(k1); av_p.append(v1); ai_p.append(i1)
            ak_s.append(k2); av_s.append(v2); ai_s.append(i2)
            ia += 1
        elif kind == 1:
            conv0 = jnp.zeros((xp.shape[0], B_CONV - 1, B_CONV_DIM), xp.dtype)
            ssm0 = jnp.zeros((xp.shape[0], B_HEADS, B_HEADDIM, B_STATE), F32)
            wts = (b_w_in[ib], b_conv_w[ib], b_conv_b[ib], b_dt_bias[ib], b_a_log[ib], b_d_skip[ib],
                   b_g_norm[ib], b_w_out[ib])
            op, c1, s1 = mamba_mixer(xp, g_mix[i], *wts, conv0, ssm0)
            os_, c2, s2 = mamba_mixer(xs, g_mix[i], *wts, state_b_conv[ib], state_b_ssm[ib])
            bc_p.append(c1); bs_p.append(s1); bc_s.append(c2); bs_s.append(s2)
            ib += 1
        else:
            cmp_k = (c_pe_k[ic], c_w1_k[ic], c_w2_k[ic])
            cmp_v = (c_pe_v[ic], c_w1_v[ic], c_w2_v[ic])
            op, st1 = nsa_prompt(xp, g_mix[i], c_w_in[ic], c_w_out[ic], cmp_k, cmp_v, w_buf)
            os_, st2 = nsa_sample(xs, g_mix[i], c_w_in[ic], c_w_out[ic], cmp_k, cmp_v, cache_c_cmp_k[ic],
                                  cache_c_cmp_v[ic], cache_c_slc_k[ic], cache_c_slc_v[ic], cache_c_win_k[ic],
                                  cache_c_win_v[ic], page_table)
            for lst, arr in zip((cck_p, ccv_p, csk_p, csv_p, cwk_p, cwv_p), st1):
                lst.append(arr)
            for lst, arr in zip((cck_s, ccv_s, csk_s, csv_s, cwk_s, cwv_s), st2):
                lst.append(arr)
            ic += 1
        xp = xp + op
        xs = xs + os_
        mk, mv = mem_kv(mem_prompt, g_mem[i], x_w_kv[i])
        mk_p.append(mk); mv_p.append(mv)
        xp = xp + cross_attn(xp, g_xattn[i], mk, mv, x_w_q[i], x_w_o[i])
        xs = xs + cross_attn(xs, g_xattn[i], cache_mem_k[i], cache_mem_v[i], x_w_q[i], x_w_o[i])
        xp = xp + 0.5 * swiglu(xp, g_ffn2[i], ffn2_wi[i], ffn2_wo[i])
        xs = xs + 0.5 * swiglu(xs, g_ffn2[i], ffn2_wi[i], ffn2_wo[i])
    y_prompt = rms_norm(xp, g_final)
    y_sample = rms_norm(xs, g_final)
    st = jnp.stack
    return (y_prompt, y_sample,
            st(ak_p), st(av_p), st(ai_p), st(bs_p), st(bc_p),
            st(cck_p), st(ccv_p), st(csk_p), st(csv_p), st(cwk_p), st(cwv_p),
            st(mk_p), st(mv_p),
            st(ak_s), st(av_s), st(ai_s), st(bs_s), st(bc_s),
            st(cck_s), st(ccv_s), st(csk_s), st(csv_s), st(cwk_s), st(cwv_s))
```

```python
import functools

import jax
import jax.numpy as jnp
import numpy as np
from jax import lax
from jax.experimental import pallas as pl
from jax.experimental.pallas import tpu as pltpu

F32 = jnp.float32
BF16 = jnp.bfloat16
I32 = jnp.int32
INT_MIN = -2 ** 31
EPS = 1e-6

D_MODEL = 1024
DEPTH = 4
N_MIXERS = 3
HEAD_DIM = 64
ROPE_THETA = 10000.0
Q_BLOCK = 128
A_HEADS = D_MODEL // HEAD_DIM
A_KV_HEADS = 4
IDX_HEADS = 8
IDX_DIM = 64
TOPK_MAX = 256
B_D_INNER = 2 * D_MODEL
B_HEADDIM = 64
B_HEADS = B_D_INNER // B_HEADDIM
B_GROUPS = 4
B_STATE = 128
B_CONV = 4
B_CONV_DIM = B_D_INNER + 2 * B_GROUPS * B_STATE
B_CHUNK = 128
C_HEADS = D_MODEL // HEAD_DIM
C_KV_HEADS = 2
CMP_BLOCK = 32
CMP_STRIDE = 16
SLC_BLOCK = 64
N_SLC = 16
WINDOW = 512
X_HEADS = 4
X_HEAD_DIM = 128
D_FF = 2816

LANE = 128
SUBLANE = 8
VMEM_LIMIT = 48 * 1024 * 1024
MM_COLS = 512
FF_CHUNK = 256
TQ = SUBLANE
PAGES_PER_STEP = 8


def _pick_tile(n, candidates):
    for c in candidates:
        if n % c == 0:
            return c
    return n


def _resident(shape):
    return pl.BlockSpec(shape, lambda *_: (0,) * len(shape), pipeline_mode=pl.Buffered(1))


def _rms(x, g):
    return x * lax.rsqrt(jnp.mean(x * x, axis=-1, keepdims=True) + EPS) * g


def _nmm_kernel(x_ref, g_ref, w_ref, o_ref, *, norm):
    x = x_ref[...]
    if norm:
        x = _rms(x, g_ref[...])
    xb = x.astype(BF16)
    n = w_ref.shape[1]
    for c0 in range(0, n, MM_COLS):
        c1 = min(c0 + MM_COLS, n)
        o_ref[:, c0:c1] = jnp.dot(xb, w_ref[:, c0:c1], preferred_element_type=F32)


def norm_matmul(x, w, g=None):
    m, k = x.shape
    n = w.shape[1]
    n_pad = -(-n // LANE) * LANE
    wb = w.astype(BF16)
    if n_pad != n:
        wb = jnp.pad(wb, ((0, 0), (0, n_pad - n)))
    m_orig = m
    if m % SUBLANE != 0:
        m = -(-m // 256) * 256
        x = jnp.pad(x, ((0, m - m_orig), (0, 0)))
    tm = _pick_tile(m, (256, 128, 64, 32, 16, 8))
    norm = g is not None
    gg = (g if norm else jnp.ones((k,), F32)).reshape(1, k).astype(F32)
    out = pl.pallas_call(
        functools.partial(_nmm_kernel, norm=norm),
        grid=(m // tm,),
        in_specs=[pl.BlockSpec((tm, k), lambda i: (i, 0)), _resident((1, k)), _resident((k, n_pad))],
        out_specs=pl.BlockSpec((tm, n_pad), lambda i: (i, 0)),
        out_shape=jax.ShapeDtypeStruct((m, n_pad), F32),
        compiler_params=pltpu.CompilerParams(dimension_semantics=("parallel",), vmem_limit_bytes=VMEM_LIMIT),
        name="norm_matmul",
    )(x, gg, wb)
    if n_pad != n or m != m_orig:
        out = out[:m_orig, :n]
    return out


def mm(x, w, g=None):
    lead = x.shape[:-1]
    return norm_matmul(x.reshape(-1, x.shape[-1]), w, g).reshape(lead + (w.shape[1],))


def _ffn_kernel(x_ref, g_ref, wg_ref, wu_ref, wo_ref, o_ref, *, n_chunks):
    x = x_ref[...]
    xn = _rms(x, g_ref[...]).astype(BF16)

    def body(c, acc):
        off = pl.multiple_of(c * FF_CHUNK, FF_CHUNK)
        hg = jnp.dot(xn, wg_ref[:, pl.ds(off, FF_CHUNK)], preferred_element_type=F32)
        hu = jnp.dot(xn, wu_ref[:, pl.ds(off, FF_CHUNK)], preferred_element_type=F32)
        act = (jax.nn.silu(hg) * hu).astype(BF16)
        return acc + jnp.dot(act, wo_ref[pl.ds(off, FF_CHUNK), :], preferred_element_type=F32)

    acc = lax.fori_loop(0, n_chunks, body, jnp.zeros(x.shape, F32))
    o_ref[...] = x + 0.5 * acc


def ffn_residual(x, g, wi, wo):
    lead, d = x.shape[:-1], x.shape[-1]
    x2 = x.reshape(-1, d)
    m = x2.shape[0]
    f = wo.shape[0]
    assert f % FF_CHUNK == 0
    tm = _pick_tile(m, (512, 256, 128, 64, 32, 16, 8))
    wb = wi.astype(BF16)
    out = pl.pallas_call(
        functools.partial(_ffn_kernel, n_chunks=f // FF_CHUNK),
        grid=(m // tm,),
        in_specs=[pl.BlockSpec((tm, d), lambda i: (i, 0)), _resident((1, d)), _resident((d, f)), _resident((d, f)),
                  _resident((f, d))],
        out_specs=pl.BlockSpec((tm, d), lambda i: (i, 0)),
        out_shape=jax.ShapeDtypeStruct((m, d), F32),
        compiler_params=pltpu.CompilerParams(dimension_semantics=("parallel",), vmem_limit_bytes=VMEM_LIMIT),
        name="ffn_residual",
    )(x2, g.reshape(1, d).astype(F32), wb[:, :f], wb[:, f:], wo.astype(BF16))
    return out.reshape(lead + (d,))


def _softmax_rows(s):
    m = jnp.max(s, axis=-1, keepdims=True)
    e = jnp.exp(s - jnp.where(m == -jnp.inf, 0.0, m))
    den = jnp.sum(e, axis=-1, keepdims=True)
    return e / jnp.where(den > 0, den, 1.0)


def _flash_step(carry, q, kt_tile, v_tile, sel, rep):
    m, l, acc = carry
    rows, tk = q.shape[0], kt_tile.shape[1]
    s = jnp.dot(q, kt_tile, preferred_element_type=F32)
    s = jnp.where(sel, s.reshape(rep, Q_BLOCK, tk), -jnp.inf).reshape(rows, tk)
    m_new = jnp.maximum(m, jnp.max(s, axis=1, keepdims=True))
    m_safe = jnp.where(m_new == -jnp.inf, 0.0, m_new)
    alpha = jnp.exp(m - m_safe)
    p = jnp.exp(s - m_safe)
    l = l * alpha + jnp.sum(p, axis=1, keepdims=True)
    pv = jnp.dot(p.astype(BF16), v_tile, preferred_element_type=F32)
    return m_new, l, acc * alpha + pv


def _flash_init(rows, hd):
    return (jnp.full((rows, 1), -jnp.inf, F32), jnp.zeros((rows, 1), F32), jnp.zeros((rows, hd), F32))


def _flash_finish(carry):
    _, l, acc = carry
    return acc / jnp.where(l > 0, l, 1.0)


def _stack_heads(x, h0, rep, hd):
    return jnp.concatenate([x[:, (h0 + r) * hd:(h0 + r + 1) * hd] for r in range(rep)], axis=0)


def _sortable_key(score):
    bits = pltpu.bitcast(score, I32)
    key = jnp.where(bits < 0, bits ^ jnp.int32(0x7FFFFFFF), bits)
    return jnp.where(score == 0.0, 0, key)


def _count_lanes(pred_fn, key_ref, nkt, tk):
    rows = key_ref.shape[0]

    def body(kt, cnt):
        keys = key_ref[:, pl.ds(pl.multiple_of(kt * tk, tk), tk)]
        for c in range(tk // LANE):
            hit = pred_fn(keys[:, c * LANE:(c + 1) * LANE], kt * tk + c * LANE)
            cnt = cnt + jnp.where(hit, 1, 0)
        return cnt

    cnt = lax.fori_loop(0, nkt, body, jnp.zeros((rows, LANE), I32))
    return jnp.sum(cnt, axis=1, keepdims=True)


def _top_k_mask_params(key_ref, nkt, tk, topk, n_valid):
    rows = key_ref.shape[0]
    lane_iota = lax.broadcasted_iota(I32, (rows, LANE), 1)

    def bit_body(it, thr):
        cand = thr + jnp.left_shift(jnp.int32(1), 31 - it)
        cnt = _count_lanes(lambda keys, c0: keys >= cand, key_ref, nkt, tk)
        return jnp.where(cnt >= topk, cand, thr)

    thr = lax.fori_loop(0, 32, bit_body, jnp.full((rows, 1), INT_MIN, I32))
    many = n_valid > topk
    thr = jnp.where(many, thr, INT_MIN + 1)
    c_gt = _count_lanes(lambda keys, c0: keys > thr, key_ref, nkt, tk)
    c_eq = _count_lanes(lambda keys, c0: keys == thr, key_ref, nkt, tk)
    need = topk - c_gt
    excess = many & (c_eq > need)
    n_col_bits = max(1, (key_ref.shape[1] - 1).bit_length())

    def tie_search():
        def jbody(it, last):
            cand = last + jnp.left_shift(jnp.int32(1), n_col_bits - 1 - it)
            cnt = _count_lanes(lambda keys, c0: (keys == thr) & (c0 + lane_iota < cand), key_ref, nkt, tk)
            return jnp.where(cnt <= need - 1, cand, last)

        return lax.fori_loop(0, n_col_bits, jbody, jnp.zeros((rows, 1), I32))

    any_excess = jnp.max(jnp.where(excess, 1, 0)) > 0
    last_tie = lax.cond(any_excess, tie_search, lambda: jnp.zeros((rows, 1), I32))
    return thr, jnp.where(excess, last_tie, jnp.int32(2 ** 30))


def _block_cover(n_cmp_rows):
    c0 = lax.broadcasted_iota(I32, (n_cmp_rows, LANE), 0) * CMP_STRIDE
    s0 = lax.broadcasted_iota(I32, (n_cmp_rows, LANE), 1) * SLC_BLOCK
    return jnp.where(c0 < s0 + SLC_BLOCK, jnp.where(c0 + CMP_BLOCK > s0, 1.0, 0.0), 0.0).astype(BF16)


def _block_importance(p_sum, cover, row_pos):
    p_hi = p_sum.astype(BF16)
    p_lo = (p_sum - p_hi.astype(F32)).astype(BF16)
    imp = jnp.dot(p_hi, cover, preferred_element_type=F32) + jnp.dot(p_lo, cover, preferred_element_type=F32)
    jb = lax.broadcasted_iota(I32, imp.shape, 1)
    cur = row_pos // SLC_BLOCK
    forced = (jb == 0) | (jb == cur) | (jb == cur - 1)
    imp = jnp.where(forced, jnp.inf, imp)
    return jnp.where(jb <= cur, imp, -jnp.inf)


def _dsa_prompt_kernel(iq_ref, iw_ref, ikt_ref, q_ref, kt_ref, v_ref, o_ref, key_ref, *, topk, tk,
                       n_idx_heads, n_groups, rep):
    i = pl.program_id(1)
    t0 = i * Q_BLOCK
    nkt = (t0 + Q_BLOCK + tk - 1) // tk
    row = t0 + lax.broadcasted_iota(I32, (Q_BLOCK, 1), 0)
    hd = kt_ref.shape[2]
    di = ikt_ref.shape[1]

    iw = iw_ref[0]
    iq = iq_ref[0]
    iq_heads = [iq[:, h * di:(h + 1) * di] for h in range(n_idx_heads)]

    def score_body(kt, _):
        off = pl.multiple_of(kt * tk, tk)
        acc = jnp.zeros((Q_BLOCK, tk), F32)
        for h in range(n_idx_heads):
            r = jnp.dot(iq_heads[h], ikt_ref[0, :, pl.ds(off, tk)], preferred_element_type=F32)
            acc = acc + jnp.maximum(r, 0.0) * iw[:, h:h + 1]
        col = off + lax.broadcasted_iota(I32, (Q_BLOCK, tk), 1)
        key_ref[:, pl.ds(off, tk)] = jnp.where(col <= row, _sortable_key(acc), INT_MIN)
        return 0

    lax.fori_loop(0, nkt, score_body, 0)

    thr, last_tie = _top_k_mask_params(key_ref, nkt, tk, topk, row + 1)

    q = q_ref[0]
    q_groups = [_stack_heads(q, g * rep, rep, hd) for g in range(n_groups)]
    rows_g = rep * Q_BLOCK

    def attn_body(kt, carry):
        off = pl.multiple_of(kt * tk, tk)
        keys = key_ref[:, pl.ds(off, tk)]
        col = off + lax.broadcasted_iota(I32, (Q_BLOCK, tk), 1)
        sel = ((keys > thr) | ((keys == thr) & (col <= last_tie)))[None]
        v_t = v_ref[0, pl.ds(off, tk), :]
        return tuple(
            _flash_step(carry[g], q_groups[g], kt_ref[0, g, :, pl.ds(off, tk)], v_t[:, g * hd:(g + 1) * hd], sel, rep)
            for g in range(n_groups))

    fin = lax.fori_loop(0, nkt, attn_body, tuple(_flash_init(rows_g, hd) for _ in range(n_groups)))
    for g in range(n_groups):
        o = _flash_finish(fin[g])
        for r in range(rep):
            h = g * rep + r
            o_ref[0, :, h * hd:(h + 1) * hd] = o[r * Q_BLOCK:(r + 1) * Q_BLOCK]


def dsa_prompt_attention(q, k, v, iq, ik, iw, topk, tk=512):
    b, s, h, d = q.shape
    g = k.shape[2]
    hi, di = iq.shape[2], iq.shape[3]
    tk = min(tk, s)
    qh = (q * (d ** -0.5)).astype(BF16).reshape(b, s, h * d)
    kt = k.astype(BF16).transpose(0, 2, 3, 1)
    vv = v.astype(BF16).reshape(b, s, g * d)
    iqh = iq.astype(BF16).reshape(b, s, hi * di)
    ikt = ik.astype(BF16).transpose(0, 2, 1)
    kern = functools.partial(_dsa_prompt_kernel, topk=topk, tk=tk, n_idx_heads=hi, n_groups=g, rep=h // g)
    q_block = lambda width: pl.BlockSpec((1, Q_BLOCK, width), lambda bb, i: (bb, i, 0))
    return pl.pallas_call(
        kern,
        grid=(b, s // Q_BLOCK),
        in_specs=[
            q_block(hi * di), q_block(hi),
            pl.BlockSpec((1, di, s), lambda bb, i: (bb, 0, 0)),
            q_block(h * d),
            pl.BlockSpec((1, g, d, s), lambda bb, i: (bb, 0, 0, 0)),
            pl.BlockSpec((1, s, g * d), lambda bb, i: (bb, 0, 0)),
        ],
        out_specs=q_block(h * d),
        out_shape=jax.ShapeDtypeStruct((b, s, h * d), F32),
        scratch_shapes=[pltpu.VMEM((Q_BLOCK, s), I32)],
        compiler_params=pltpu.CompilerParams(
            dimension_semantics=("parallel", "arbitrary"), vmem_limit_bytes=VMEM_LIMIT),
        name="dsa_prompt_attention",
    )(iqh, iw.astype(F32), ikt, qh, kt, vv)


def _top_blocks(imp_t, n_sel):
    nb = imp_t.shape[0]
    j_iota = lax.broadcasted_iota(I32, imp_t.shape, 0)
    rank = jnp.zeros(imp_t.shape, F32)
    for k in range(nb):
        row_k = imp_t[k:k + 1, :]
        earlier = jnp.where(j_iota > k, 1.0, 0.0)
        rank = rank + jnp.where(row_k > imp_t, 1.0, jnp.where(row_k == imp_t, earlier, 0.0))
    return jnp.where(rank < n_sel, 1.0, 0.0)


def _nsa_prompt_kernel(q_ref, gate_ref, kct_ref, vc_ref, kst_ref, vs_ref, kwt_ref, vw_ref, o_ref, *,
                       tk, n_cmp, n_blk, n_groups, rep, win_len):
    i = pl.program_id(1)
    t0 = i * Q_BLOCK
    nkt = (t0 + Q_BLOCK + tk - 1) // tk
    hd = kst_ref.shape[2]
    rows = rep * Q_BLOCK
    nc_pad = kct_ref.shape[-1]
    row = t0 + lax.broadcasted_iota(I32, (Q_BLOCK, 1), 0)
    gates = jax.nn.sigmoid(gate_ref[0])
    n_heads = n_groups * rep
    q_all = q_ref[0]

    cover = _block_cover(nc_pad)
    c_idx = lax.broadcasted_iota(I32, (Q_BLOCK, nc_pad), 1)
    c_ok = ((c_idx < n_cmp) & (c_idx * CMP_STRIDE + (CMP_BLOCK - 1) <= row))[None]
    w_start = pl.multiple_of(jnp.maximum(t0 - WINDOW, 0), Q_BLOCK)
    w_col = w_start + lax.broadcasted_iota(I32, (Q_BLOCK, win_len), 1)
    w_d = row - w_col
    w_ok = ((w_d >= 0) & (w_d < WINDOW))[None]

    for g in range(n_groups):
        q = _stack_heads(q_all, g * rep, rep, hd)
        lo, hi = g * hd, (g + 1) * hd

        s = jnp.dot(q, kct_ref[0, g], preferred_element_type=F32)
        s = jnp.where(c_ok, s.reshape(rep, Q_BLOCK, nc_pad), -jnp.inf)
        p = _softmax_rows(s)
        o_cmp = jnp.dot(p.reshape(rows, nc_pad).astype(BF16), vc_ref[0, :, lo:hi], preferred_element_type=F32)

        imp = _block_importance(jnp.sum(p, axis=0), cover, row)
        sel_blocks = _top_blocks(imp.T[:n_blk], min(N_SLC, n_blk))
        if n_blk < LANE:
            sel_blocks = jnp.concatenate([sel_blocks, jnp.zeros((LANE - n_blk, Q_BLOCK), F32)], axis=0)
        sel_blocks = sel_blocks.T.astype(BF16)

        def slc_body(kt, carry):
            off = pl.multiple_of(kt * tk, tk)
            col = off + lax.broadcasted_iota(I32, (Q_BLOCK, tk), 1)
            blk_of_col = (off + lax.broadcasted_iota(I32, (LANE, tk), 1)) // SLC_BLOCK
            expand = jnp.where(lax.broadcasted_iota(I32, (LANE, tk), 0) == blk_of_col, 1.0, 0.0).astype(BF16)
            hit = jnp.dot(sel_blocks, expand, preferred_element_type=F32)
            sel = (jnp.where(col <= row, hit, 0.0) > 0.5)[None]
            return _flash_step(carry, q, kst_ref[0, g, :, pl.ds(off, tk)], vs_ref[0, pl.ds(off, tk), lo:hi], sel, rep)

        o_slc = _flash_finish(lax.fori_loop(0, nkt, slc_body, _flash_init(rows, hd)))

        s = jnp.dot(q, kwt_ref[0, g, :, pl.ds(w_start, win_len)], preferred_element_type=F32)
        s = jnp.where(w_ok, s.reshape(rep, Q_BLOCK, win_len), -jnp.inf)
        p = _softmax_rows(s).reshape(rows, win_len)
        o_win = jnp.dot(p.astype(BF16), vw_ref[0, pl.ds(w_start, win_len), lo:hi], preferred_element_type=F32)

        for r in range(rep):
            h = g * rep + r
            rs = slice(r * Q_BLOCK, (r + 1) * Q_BLOCK)
            o_ref[0, :, h * hd:(h + 1) * hd] = (
                gates[:, h:h + 1] * o_cmp[rs] + gates[:, n_heads + h:n_heads + h + 1] * o_slc[rs]
                + gates[:, 2 * n_heads + h:2 * n_heads + h + 1] * o_win[rs])


def nsa_prompt_attention(q, gate_logits, kcmp, vcmp, ks, vs, kw, vw, tk=512):
    b, s, h, d = q.shape
    g = ks.shape[2]
    n_cmp = kcmp.shape[1]
    nc_pad = -(-n_cmp // LANE) * LANE
    n_blk = -(-s // SLC_BLOCK)
    assert n_blk <= LANE and s % Q_BLOCK == 0
    tk = min(tk, s)
    win_len = min(WINDOW + Q_BLOCK, s)
    pad_c = ((0, 0), (0, nc_pad - n_cmp), (0, 0), (0, 0))
    qh = (q * (d ** -0.5)).astype(BF16).reshape(b, s, h * d)
    kct = jnp.pad(kcmp.astype(BF16), pad_c).transpose(0, 2, 3, 1)
    vcc = jnp.pad(vcmp.astype(BF16), pad_c).reshape(b, nc_pad, g * d)
    kst = ks.astype(BF16).transpose(0, 2, 3, 1)
    kwt = kw.astype(BF16).transpose(0, 2, 3, 1)
    vss = vs.astype(BF16).reshape(b, s, g * d)
    vww = vw.astype(BF16).reshape(b, s, g * d)
    kern = functools.partial(_nsa_prompt_kernel, tk=tk, n_cmp=n_cmp, n_blk=n_blk, n_groups=g, rep=h // g,
                             win_len=win_len)
    full3 = lambda bb, i: (bb, 0, 0)
    full4 = lambda bb, i: (bb, 0, 0, 0)
    q_block = lambda width: pl.BlockSpec((1, Q_BLOCK, width), lambda bb, i: (bb, i, 0))
    return pl.pallas_call(
        kern,
        grid=(b, s // Q_BLOCK),
        in_specs=[
            q_block(h * d), q_block(3 * h),
            pl.BlockSpec((1, g, d, nc_pad), full4),
            pl.BlockSpec((1, nc_pad, g * d), full3),
            pl.BlockSpec((1, g, d, s), full4),
            pl.BlockSpec((1, s, g * d), full3),
            pl.BlockSpec((1, g, d, s), full4),
            pl.BlockSpec((1, s, g * d), full3),
        ],
        out_specs=q_block(h * d),
        out_shape=jax.ShapeDtypeStruct((b, s, h * d), F32),
        compiler_params=pltpu.CompilerParams(
            dimension_semantics=("parallel", "arbitrary"), vmem_limit_bytes=VMEM_LIMIT),
        name="nsa_prompt_attention",
    )(qh, gate_logits.astype(F32), kct, vcc, kst, vss, kwt, vww)


def _attend_rows(q, k_rows, v_rows, sel, rep):
    n = k_rows.shape[0]
    s = lax.dot_general(q, k_rows, (((1,), (1,)), ((), ())), preferred_element_type=F32)
    s = jnp.where(sel, s.reshape(rep, TQ, n), -jnp.inf)
    p = _softmax_rows(s).reshape(rep * TQ, n)
    return p, jnp.dot(p.astype(BF16), v_rows, preferred_element_type=F32)


def _pad_rows(x, n):
    return jnp.concatenate([x, jnp.zeros((n - x.shape[0],) + x.shape[1:], x.dtype)], axis=0)


def _head_rows(x, scale=None):
    b, t, h, d = x.shape
    if scale is not None:
        x = x * scale
    x = jnp.pad(x.astype(BF16), ((0, 0), (0, TQ - t), (0, 0), (0, 0)))
    return x.transpose(0, 2, 1, 3).reshape(b, h * TQ, d)


def _new_rows(x):
    b, t = x.shape[:2]
    return jnp.pad(x.reshape(b, t, -1).astype(BF16), ((0, 0), (0, TQ - t), (0, 0)))


def _page_specs(block, n_pages, pp):
    def spec(j):
        return pl.BlockSpec(block, lambda bb, p, pt: (pt[bb * n_pages + p * pp + j],) + (0,) * (len(block) - 1))
    return [spec(j) for j in range(pp)]


def _per_batch(*blk):
    return pl.BlockSpec((1,) + blk, lambda bb, p, pt: (bb,) + (0,) * len(blk))


def _whole(*blk):
    return pl.BlockSpec(blk, lambda bb, p, pt: (0,) * len(blk))


def _dsa_sample_kernel(pt_ref, iq_ref, iw_ref, q_ref, ikn_ref, kn_ref, vn_ref, *rest, topk, past, n_new,
                       n_idx_heads, n_groups, rep, pp):
    idx_pages = rest[:pp]
    k_pages = rest[pp:2 * pp]
    v_pages = rest[2 * pp:3 * pp]
    o_ref, key_ref, ksc, vsc = rest[3 * pp:]
    p_step = pl.program_id(1)
    hd = q_ref.shape[-1]
    page = idx_pages[0].shape[1]
    n_tiles = past // page + 1
    iw = iw_ref[0]
    tpos = past + jnp.minimum(lax.broadcasted_iota(I32, (TQ, 1), 0), n_new - 1)

    def scores(ik_rows):
        r = lax.dot_general(iq_ref[0], ik_rows, (((1,), (1,)), ((), ())), preferred_element_type=F32)
        r = jnp.maximum(r, 0.0).reshape(n_idx_heads, TQ, ik_rows.shape[0]) * iw
        return jnp.sum(r, axis=0)

    for j in range(pp):
        off = pl.multiple_of((p_step * pp + j) * page, page)
        key_ref[:, pl.ds(off, page)] = _sortable_key(scores(idx_pages[j][0].astype(BF16)))
        ksc[pl.ds(off, page), :] = k_pages[j][0].astype(BF16)
        vsc[pl.ds(off, page), :] = v_pages[j][0].astype(BF16)

    @pl.when(p_step == pl.num_programs(1) - 1)
    def _():
        c = lax.broadcasted_iota(I32, (TQ, page), 1)
        fresh_ok = (c < n_new) & (past + c <= tpos)
        key_ref[:, past:past + page] = jnp.where(fresh_ok, _sortable_key(scores(_pad_rows(ikn_ref[0], page))), INT_MIN)
        ksc[past:past + page, :] = _pad_rows(kn_ref[0], page)
        vsc[past:past + page, :] = _pad_rows(vn_ref[0], page)
        thr, last_tie = _top_k_mask_params(key_ref, n_tiles, page, topk, tpos + 1)
        keys = key_ref[...]
        col = lax.broadcasted_iota(I32, keys.shape, 1)
        sel = ((keys > thr) | ((keys == thr) & (col <= last_tie)))[None]
        for g in range(n_groups):
            _, o = _attend_rows(q_ref[0, g * rep * TQ:(g + 1) * rep * TQ], ksc[:, g * hd:(g + 1) * hd],
                                vsc[:, g * hd:(g + 1) * hd], sel, rep)
            for r in range(rep):
                h = g * rep + r
                o_ref[0, :, h * hd:(h + 1) * hd] = o[r * TQ:(r + 1) * TQ]


def dsa_sample_attention(q, k, v, iq, ik, iw, pool_k, pool_v, pool_idx, page_table, topk):
    b, t, h, d = q.shape
    g = k.shape[2]
    hi, di = iq.shape[2], iq.shape[3]
    n_pages = page_table.shape[1]
    page = pool_k.shape[1]
    past = n_pages * page
    pp = min(PAGES_PER_STEP, n_pages)
    assert n_pages % pp == 0 and t <= TQ
    n_phys = pool_k.shape[0]
    iw_h = jnp.pad(iw.astype(F32), ((0, 0), (0, TQ - t), (0, 0))).transpose(0, 2, 1)[..., None]
    kern = functools.partial(_dsa_sample_kernel, topk=topk, past=past, n_new=t, n_idx_heads=hi, n_groups=g,
                             rep=h // g, pp=pp)
    o = pl.pallas_call(
        kern,
        grid_spec=pltpu.PrefetchScalarGridSpec(
            num_scalar_prefetch=1,
            grid=(b, n_pages // pp),
            in_specs=[_per_batch(hi * TQ, di), _per_batch(hi, TQ, 1), _per_batch(h * TQ, d), _per_batch(TQ, di),
                      _per_batch(TQ, g * d), _per_batch(TQ, g * d)]
                     + _page_specs((1, page, di), n_pages, pp)
                     + _page_specs((1, page, g * d), n_pages, pp)
                     + _page_specs((1, page, g * d), n_pages, pp),
            out_specs=_per_batch(TQ, h * d),
            scratch_shapes=[pltpu.VMEM((TQ, past + page), I32), pltpu.VMEM((past + page, g * d), BF16),
                            pltpu.VMEM((past + page, g * d), BF16)],
        ),
        out_shape=jax.ShapeDtypeStruct((b, TQ, h * d), F32),
        compiler_params=pltpu.CompilerParams(
            dimension_semantics=("parallel", "arbitrary"), vmem_limit_bytes=VMEM_LIMIT),
        name="dsa_sample_attention",
    )(page_table.reshape(-1).astype(I32), _head_rows(iq), iw_h, _head_rows(q, d ** -0.5), _new_rows(ik),
      _new_rows(k), _new_rows(v),
      *([pool_idx] * pp), *([pool_k.reshape(n_phys, page, g * d)] * pp), *([pool_v.reshape(n_phys, page, g * d)] * pp))
    return o[:, :t]


def _top_blocks_rows(imp, n_blk, n_sel):
    j_iota = lax.broadcasted_iota(I32, imp.shape, 1)
    rank = jnp.zeros(imp.shape, F32)
    for k in range(n_blk):
        col_k = imp[:, k:k + 1]
        earlier = jnp.where(j_iota > k, 1.0, 0.0)
        rank = rank + jnp.where(col_k > imp, 1.0, jnp.where(col_k == imp, earlier, 0.0))
    return jnp.where((rank < n_sel) & (j_iota < n_blk), 1.0, 0.0)


def _nsa_sample_kernel(pt_ref, q_ref, gate_ref, ksn_ref, vsn_ref, kwn_ref, vwn_ref, wink_ref, winv_ref,
                       wck_ref, wcv_ref, pewk_ref, pewv_ref, w2k_ref, w2v_ref, *rest, past, n_new, n_groups, rep, pp):
    kc_pages = rest[:pp]
    vc_pages = rest[pp:2 * pp]
    ks_pages = rest[2 * pp:3 * pp]
    vs_pages = rest[3 * pp:4 * pp]
    o_ref, kcs, vcs, kss, vss, kws, vws = rest[4 * pp:]
    p_step = pl.program_id(1)
    hd = q_ref.shape[-1]
    page = ks_pages[0].shape[1]
    cpp = kc_pages[0].shape[1]
    n_chunks = kcs.shape[0]
    n_cmp = n_chunks - CMP_BLOCK // CMP_STRIDE + 1
    n_blk = -(-(past + n_new) // SLC_BLOCK)
    w_buf = wink_ref.shape[1]
    n_heads = n_groups * rep
    rows = rep * TQ

    for j in range(pp):
        pg = p_step * pp + j
        coff = pl.multiple_of(pg * cpp, cpp)
        kcs[pl.ds(coff, cpp), :] = kc_pages[j][0]
        vcs[pl.ds(coff, cpp), :] = vc_pages[j][0]
        off = pl.multiple_of(pg * page, page)
        kss[pl.ds(off, page), :] = ks_pages[j][0].astype(BF16)
        vss[pl.ds(off, page), :] = vs_pages[j][0].astype(BF16)

    @pl.when(p_step == pl.num_programs(1) - 1)
    def _():
        kss[past:past + page, :] = _pad_rows(ksn_ref[0], page)
        vss[past:past + page, :] = _pad_rows(vsn_ref[0], page)
        kws[0:w_buf, :] = wink_ref[0].astype(BF16)
        vws[0:w_buf, :] = winv_ref[0].astype(BF16)
        kws[w_buf:w_buf + LANE, :] = _pad_rows(kwn_ref[0], LANE)
        vws[w_buf:w_buf + LANE, :] = _pad_rows(vwn_ref[0], LANE)

        tpos = past + jnp.minimum(lax.broadcasted_iota(I32, (TQ, 1), 0), n_new - 1)
        gates = jax.nn.sigmoid(gate_ref[0])

        def compress(chunks_ref, wc_ref, pew_ref, w2_ref):
            a = jnp.dot(chunks_ref[...].astype(BF16), wc_ref[...], preferred_element_type=F32)
            half = n_groups * hd
            first, second = a[:, :half], a[:, half:]
            second = jnp.concatenate([second[1:], jnp.zeros((1, half), F32)], axis=0)
            out = []
            for g in range(n_groups):
                hcol = first[:, g * hd:(g + 1) * hd] + second[:, g * hd:(g + 1) * hd] + pew_ref[...]
                out.append(jnp.dot(jax.nn.silu(hcol).astype(BF16), w2_ref[...],
                                   preferred_element_type=F32).astype(BF16))
            return out

        kcmp = compress(kcs, wck_ref, pewk_ref, w2k_ref)
        vcmp = compress(vcs, wcv_ref, pewv_ref, w2v_ref)

        c_idx = lax.broadcasted_iota(I32, (TQ, n_chunks), 1)
        c_ok = ((c_idx < n_cmp) & (c_idx * CMP_STRIDE + (CMP_BLOCK - 1) <= tpos))[None]
        cover = _block_cover(n_chunks)
        n_slc = kss.shape[0]
        col = lax.broadcasted_iota(I32, (TQ, n_slc), 1)
        blk_of_col = lax.broadcasted_iota(I32, (LANE, n_slc), 1) // SLC_BLOCK
        expand = jnp.where(lax.broadcasted_iota(I32, (LANE, n_slc), 0) == blk_of_col, 1.0, 0.0).astype(BF16)
        n_win = kws.shape[0]
        w_c = lax.broadcasted_iota(I32, (TQ, n_win), 1)
        w_pos = past - w_buf + w_c
        w_d = tpos - w_pos
        w_ok = ((w_c < w_buf + n_new) & (w_pos >= 0) & (w_d >= 0) & (w_d < WINDOW))[None]

        for g in range(n_groups):
            q = q_ref[0, g * rows:(g + 1) * rows]
            lo, hi = g * hd, (g + 1) * hd
            p_cmp, o_cmp = _attend_rows(q, kcmp[g], vcmp[g], c_ok, rep)
            imp = _block_importance(jnp.sum(p_cmp.reshape(rep, TQ, n_chunks), axis=0), cover, tpos)
            sel_blocks = _top_blocks_rows(imp, n_blk, min(N_SLC, n_blk)).astype(BF16)
            hit = jnp.dot(sel_blocks, expand, preferred_element_type=F32)
            sel = (jnp.where(col <= tpos, hit, 0.0) > 0.5)[None]
            _, o_slc = _attend_rows(q, kss[:, lo:hi], vss[:, lo:hi], sel, rep)
            _, o_win = _attend_rows(q, kws[:, lo:hi], vws[:, lo:hi], w_ok, rep)
            for r in range(rep):
                h = g * rep + r
                rs = slice(r * TQ, (r + 1) * TQ)
                o_ref[0, :, h * hd:(h + 1) * hd] = (
                    gates[:, h:h + 1] * o_cmp[rs] + gates[:, n_heads + h:n_heads + h + 1] * o_slc[rs]
                    + gates[:, 2 * n_heads + h:2 * n_heads + h + 1] * o_win[rs])


def _compress_chunk_weights(w1, n_groups, hd):
    wr = w1.reshape(2, CMP_STRIDE, hd, hd)
    eye = jnp.eye(n_groups, dtype=w1.dtype)
    wc = jnp.einsum('hrdo,gk->rgdhko', wr, eye)
    return wc.reshape(CMP_STRIDE * n_groups * hd, 2 * n_groups * hd)


def nsa_sample_attention(q, gate_logits, ks, vs, kw, vw, cmp_k, cmp_v, pool_kc, pool_vc, pool_ks, pool_vs,
                         win_k, win_v, page_table):
    b, t, h, d = q.shape
    g = ks.shape[2]
    n_pages = page_table.shape[1]
    page = pool_ks.shape[1]
    past = n_pages * page
    pp = min(PAGES_PER_STEP, n_pages)
    assert n_pages % pp == 0 and t <= TQ and (past + t) // CMP_STRIDE * CMP_STRIDE == past
    n_phys = pool_ks.shape[0]
    cpp = page // CMP_STRIDE
    n_chunks = past // CMP_STRIDE
    cw = CMP_STRIDE * g * d
    w_buf = win_k.shape[1]

    def cmp_weights(pe, w1, w2):
        return (_compress_chunk_weights(w1, g, d).astype(BF16), mm(pe.reshape(1, -1), w1), w2.astype(BF16))

    wck, pewk, w2k = cmp_weights(*cmp_k)
    wcv, pewv, w2v = cmp_weights(*cmp_v)
    kern = functools.partial(_nsa_sample_kernel, past=past, n_new=t, n_groups=g, rep=h // g, pp=pp)
    o = pl.pallas_call(
        kern,
        grid_spec=pltpu.PrefetchScalarGridSpec(
            num_scalar_prefetch=1,
            grid=(b, n_pages // pp),
            in_specs=[_per_batch(h * TQ, d), _per_batch(TQ, 3 * h), _per_batch(TQ, g * d), _per_batch(TQ, g * d),
                      _per_batch(TQ, g * d), _per_batch(TQ, g * d), _per_batch(w_buf, g * d), _per_batch(w_buf, g * d),
                      _whole(cw, 2 * g * d), _whole(cw, 2 * g * d), _whole(1, d), _whole(1, d), _whole(d, d),
                      _whole(d, d)]
                     + _page_specs((1, cpp, cw), n_pages, pp) + _page_specs((1, cpp, cw), n_pages, pp)
                     + _page_specs((1, page, g * d), n_pages, pp) + _page_specs((1, page, g * d), n_pages, pp),
            out_specs=_per_batch(TQ, h * d),
            scratch_shapes=[pltpu.VMEM((n_chunks, cw), F32), pltpu.VMEM((n_chunks, cw), F32),
                            pltpu.VMEM((past + page, g * d), BF16), pltpu.VMEM((past + page, g * d), BF16),
                            pltpu.VMEM((w_buf + LANE, g * d), BF16), pltpu.VMEM((w_buf + LANE, g * d), BF16)],
        ),
        out_shape=jax.ShapeDtypeStruct((b, TQ, h * d), F32),
        compiler_params=pltpu.CompilerParams(
            dimension_semantics=("parallel", "arbitrary"), vmem_limit_bytes=VMEM_LIMIT),
        name="nsa_sample_attention",
    )(page_table.reshape(-1).astype(I32), _head_rows(q, d ** -0.5),
      jnp.pad(gate_logits.astype(F32), ((0, 0), (0, TQ - t), (0, 0))),
      _new_rows(ks), _new_rows(vs), _new_rows(kw), _new_rows(vw),
      win_k.reshape(b, w_buf, g * d), win_v.reshape(b, w_buf, g * d),
      wck, wcv, pewk, pewv, w2k, w2v,
      *([pool_kc.reshape(n_phys, cpp, cw)] * pp), *([pool_vc.reshape(n_phys, cpp, cw)] * pp),
      *([pool_ks.reshape(n_phys, page, g * d)] * pp), *([pool_vs.reshape(n_phys, page, g * d)] * pp))
    return o[:, :t]


def rms_norm(x, g):
    xf = x.astype(F32)
    y = xf * lax.rsqrt(jnp.mean(xf * xf, axis=-1, keepdims=True) + EPS)
    return (y * g.astype(F32)).astype(x.dtype)


def rope(x, pos):
    half = x.shape[-1] // 2
    inv = ROPE_THETA ** (-jnp.arange(half, dtype=F32) / half)
    ang = pos.astype(F32)[:, None] * inv[None, :]
    cos, sin = jnp.cos(ang)[:, None, :], jnp.sin(ang)[:, None, :]
    xf = x.astype(F32)
    x1, x2 = xf[..., :half], xf[..., half:]
    return jnp.concatenate([x1 * cos - x2 * sin, x2 * cos + x1 * sin], axis=-1).astype(x.dtype)


def last_rows(a, n):
    t = a.shape[1]
    if t >= n:
        return a[:, t - n:]
    return jnp.pad(a, ((0, 0), (n - t, 0)) + ((0, 0),) * (a.ndim - 2))


def mem_kv(mem, g, w_kv):
    b, m, _ = mem.shape
    k, v = jnp.split(mm(mem, w_kv, g), 2, axis=-1)
    return k.reshape(b, m, X_HEADS, X_HEAD_DIM), v.reshape(b, m, X_HEADS, X_HEAD_DIM)


def cross_attn(x, g, mk, mv, w_q, w_o):
    b, t, _ = x.shape
    q = mm(x, w_q, g).reshape(b, t, X_HEADS, X_HEAD_DIM)
    s = jnp.einsum('bthd,bmhd->bhtm', q, mk).astype(F32) * (X_HEAD_DIM ** -0.5)
    p = jax.nn.softmax(s, axis=-1).astype(x.dtype)
    return mm(jnp.einsum('bhtm,bmhd->bthd', p, mv).reshape(b, t, -1), w_o)


def dsa_project(x, g, w_in, pos):
    b, t, _ = x.shape
    widths = (A_HEADS * HEAD_DIM, A_KV_HEADS * HEAD_DIM, A_KV_HEADS * HEAD_DIM, IDX_HEADS * IDX_DIM, IDX_DIM)
    q, k, v, iq, ik, iw = jnp.split(mm(x, w_in, g), np.cumsum(widths).tolist(), axis=-1)
    q = rope(q.reshape(b, t, A_HEADS, HEAD_DIM), pos)
    k = rope(k.reshape(b, t, A_KV_HEADS, HEAD_DIM), pos)
    v = v.reshape(b, t, A_KV_HEADS, HEAD_DIM)
    iq = rope(iq.reshape(b, t, IDX_HEADS, IDX_DIM), pos)
    ik = rope(ik.reshape(b, t, 1, IDX_DIM), pos)[:, :, 0]
    iw = iw * (IDX_HEADS ** -0.5 * IDX_DIM ** -0.5)
    return q, k, v, iq, ik, iw


def dsa_prompt(x, g, w_in, w_out):
    s = x.shape[1]
    q, k, v, iq, ik, iw = dsa_project(x, g, w_in, jnp.arange(s))
    o = dsa_prompt_attention(q, k, v, iq, ik, iw, min(TOPK_MAX, s // 4))
    return mm(o, w_out), (k, v, ik)


def dsa_sample(x, g, w_in, w_out, pool_k, pool_v, pool_idx, page_table):
    t = x.shape[1]
    past = page_table.shape[1] * pool_k.shape[1]
    q, k, v, iq, ik, iw = dsa_project(x, g, w_in, past + jnp.arange(t))
    o = dsa_sample_attention(q, k, v, iq, ik, iw, pool_k, pool_v, pool_idx, page_table,
                             min(TOPK_MAX, (past + t) // 4))
    return mm(o, w_out), (k, v, ik)


def ssd_chunked(x, dt, a, bm, cm, h0):
    bsz, t, nh, p = x.shape
    g, n = bm.shape[2], bm.shape[3]
    r = nh // g
    l = min(B_CHUNK, t)
    nc = t // l
    xr = (x * dt[..., None]).reshape(bsz, nc, l, g, r, p)
    acs = jnp.cumsum((dt * a).reshape(bsz, nc, l, g, r), axis=2)
    br = bm.reshape(bsz, nc, l, g, n)
    cr = cm.reshape(bsz, nc, l, g, n)
    seg = acs[:, :, :, None] - acs[:, :, None, :]
    causal = jnp.tril(jnp.ones((l, l), bool))[:, :, None, None]
    decay = jnp.where(causal, jnp.exp(jnp.where(causal, seg, 0.0)), 0.0)
    cb = jnp.einsum('bcign,bcjgn->bcijg', cr, br)
    y_intra = jnp.einsum('bcijg,bcijgr,bcjgrp->bcigrp', cb, decay, xr)
    to_end = jnp.exp(acs[:, :, -1:] - acs)
    s_chunk = jnp.einsum('bclgn,bclgr,bclgrp->bcgrpn', br, to_end, xr)
    d_chunk = jnp.exp(acs[:, :, -1])

    def step(hc, inp):
        s_c, d_c = inp
        return hc * d_c[..., None, None] + s_c, hc

    h_fin, h_in = lax.scan(step, h0.reshape(bsz, g, r, p, n),
                           (jnp.moveaxis(s_chunk, 1, 0), jnp.moveaxis(d_chunk, 1, 0)))
    h_in = jnp.moveaxis(h_in, 0, 1)
    y_inter = jnp.einsum('bcign,bcigr,bcgrpn->bcigrp', cr, jnp.exp(acs), h_in)
    return (y_intra + y_inter).reshape(bsz, t, nh, p), h_fin.reshape(bsz, nh, p, n)


def mamba_mixer(x, g, w_in, conv_w, conv_b, dt_bias, a_log, d_skip, g_norm, w_out, conv_buf, ssm0):
    b, t, _ = x.shape
    proj = mm(x, w_in, g)
    z = proj[..., :B_D_INNER]
    xbc = proj[..., B_D_INNER:B_D_INNER + B_CONV_DIM]
    dt = proj[..., B_D_INNER + B_CONV_DIM:]
    xpad = jnp.concatenate([conv_buf.astype(xbc.dtype), xbc], axis=1)
    conv = conv_b + sum(xpad[:, j:j + t] * conv_w[j] for j in range(B_CONV))
    xbc = jax.nn.silu(conv)
    new_buf = xpad[:, t:]
    gn = B_GROUPS * B_STATE
    xs = xbc[..., :B_D_INNER].reshape(b, t, B_HEADS, B_HEADDIM).astype(F32)
    bm = xbc[..., B_D_INNER:B_D_INNER + gn].reshape(b, t, B_GROUPS, B_STATE).astype(F32)
    cm = xbc[..., B_D_INNER + gn:].reshape(b, t, B_GROUPS, B_STATE).astype(F32)
    dt = jax.nn.softplus(dt.astype(F32) + dt_bias.astype(F32))
    a = -jnp.exp(a_log.astype(F32))
    y, ssm = ssd_chunked(xs, dt, a, bm, cm, ssm0.astype(F32))
    y = (y + xs * d_skip.astype(F32)[:, None]).reshape(b, t, B_D_INNER).astype(x.dtype)
    y = y * jax.nn.silu(z)
    return mm(y, w_out, g_norm), new_buf, ssm.astype(ssm0.dtype)


def nsa_project(x, g, w_in, pos):
    b, t, _ = x.shape
    qw, kw_ = C_HEADS * HEAD_DIM, C_KV_HEADS * HEAD_DIM
    proj = mm(x, w_in, g)
    q = rope(proj[..., :qw].reshape(b, t, C_HEADS, HEAD_DIM), pos)
    kv = proj[..., qw:qw + 6 * kw_].reshape(b, t, 6, C_KV_HEADS, HEAD_DIM)
    gate_logits = proj[..., qw + 6 * kw_:]
    kc, vc, ks, vs, kw, vw = (kv[:, :, j] for j in range(6))
    return q, gate_logits, rope(kc, pos), vc, rope(ks, pos), vs, rope(kw, pos), vw


def compress(rows, pe, w1, w2):
    b, l, g, d = rows.shape
    ch = rows.reshape(b, l // CMP_STRIDE, CMP_STRIDE, g, d)
    per = CMP_BLOCK // CMP_STRIDE
    nblk = l // CMP_STRIDE - per + 1
    blk = jnp.concatenate([ch[:, j:j + nblk] for j in range(per)], axis=2) + pe[:, None, :]
    flat = jnp.swapaxes(blk, 2, 3).reshape(b, nblk, g, CMP_BLOCK * d)
    return mm(jax.nn.silu(mm(flat, w1)), w2)


def nsa_prompt(x, g, w_in, w_out, cmp_k, cmp_v, w_buf):
    s = x.shape[1]
    q, gate_logits, kc, vc, ks, vs, kw, vw = nsa_project(x, g, w_in, jnp.arange(s))
    kcmp, vcmp = compress(kc, *cmp_k), compress(vc, *cmp_v)
    o = nsa_prompt_attention(q, gate_logits, kcmp, vcmp, ks, vs, kw, vw)
    return mm(o, w_out), (kc, vc, ks, vs, last_rows(kw, w_buf), last_rows(vw, w_buf))


def nsa_sample(x, g, w_in, w_out, cmp_k, cmp_v, pool_kc, pool_vc, pool_ks, pool_vs, win_k, win_v, page_table):
    t = x.shape[1]
    past = page_table.shape[1] * pool_kc.shape[1]
    q, gate_logits, kc, vc, ks, vs, kw, vw = nsa_project(x, g, w_in, past + jnp.arange(t))
    o = nsa_sample_attention(q, gate_logits, ks, vs, kw, vw, cmp_k, cmp_v, pool_kc, pool_vc, pool_ks, pool_vs,
                             win_k, win_v, page_table)
    w_buf = win_k.shape[1]
    kwin = jnp.concatenate([win_k, kw], axis=1)
    vwin = jnp.concatenate([win_v, vw], axis=1)
    return mm(o, w_out), (kc, vc, ks, vs, last_rows(kwin, w_buf), last_rows(vwin, w_buf))


def kernel(x_prompt, x_sample, cache_a_k, cache_a_v, cache_a_idx, state_b_ssm, state_b_conv, cache_c_cmp_k, cache_c_cmp_v, cache_c_slc_k, cache_c_slc_v, cache_c_win_k, cache_c_win_v, cache_mem_k, cache_mem_v, page_table, mem_prompt, g_ffn1, ffn1_wi, ffn1_wo, g_mix, g_xattn, g_mem, x_w_q, x_w_kv, x_w_o, g_ffn2, ffn2_wi, ffn2_wo, g_final, a_w_in, a_w_out, b_w_in, b_conv_w, b_conv_b, b_dt_bias, b_a_log, b_d_skip, b_g_norm, b_w_out, c_w_in, c_w_out, c_pe_k, c_w1_k, c_w2_k, c_pe_v, c_w1_v, c_w2_v):
    xp, xs = x_prompt, x_sample
    w_buf = cache_c_win_k.shape[2]
    ak_p, av_p, ai_p, bs_p, bc_p = [], [], [], [], []
    cck_p, ccv_p, csk_p, csv_p, cwk_p, cwv_p = [], [], [], [], [], []
    mk_p, mv_p = [], []
    ak_s, av_s, ai_s, bs_s, bc_s = [], [], [], [], []
    cck_s, ccv_s, csk_s, csv_s, cwk_s, cwv_s = [], [], [], [], [], []
    ia = ib = ic = 0
    for i in range(DEPTH):
        xp = ffn_residual(xp, g_ffn1[i], ffn1_wi[i], ffn1_wo[i])
        xs = ffn_residual(xs, g_ffn1[i], ffn1_wi[i], ffn1_wo[i])
        kind = i % N_MIXERS
        if kind == 0:
            op, (k1, v1, i1) = dsa_prompt(xp, g_mix[i], a_w_in[ia], a_w_out[ia])
            os_, (k2, v2, i2) = dsa_sample(xs, g_mix[i], a_w_in[ia], a_w_out[ia], cache_a_k[ia], cache_a_v[ia],
                                           cache_a_idx[ia], page_table)
            ak_p.append(k1); av_p.append(v1); ai_p.append(i1)
            ak_s.append(k2); av_s.append(v2); ai_s.append(i2)
            ia += 1
        elif kind == 1:
            conv0 = jnp.zeros((xp.shape[0], B_CONV - 1, B_CONV_DIM), xp.dtype)
            ssm0 = jnp.zeros((xp.shape[0], B_HEADS, B_HEADDIM, B_STATE), F32)
            wts = (b_w_in[ib], b_conv_w[ib], b_conv_b[ib], b_dt_bias[ib], b_a_log[ib], b_d_skip[ib],
                   b_g_norm[ib], b_w_out[ib])
            op, c1, s1 = mamba_mixer(xp, g_mix[i], *wts, conv0, ssm0)
            os_, c2, s2 = mamba_mixer(xs, g_mix[i], *wts, state_b_conv[ib], state_b_ssm[ib])
            bc_p.append(c1); bs_p.append(s1); bc_s.append(c2); bs_s.append(s2)
            ib += 1
        else:
            cmp_k = (c_pe_k[ic], c_w1_k[ic], c_w2_k[ic])
            cmp_v = (c_pe_v[ic], c_w1_v[ic], c_w2_v[ic])
            op, st1 = nsa_prompt(xp, g_mix[i], c_w_in[ic], c_w_out[ic], cmp_k, cmp_v, w_buf)
            os_, st2 = nsa_sample(xs, g_mix[i], c_w_in[ic], c_w_out[ic], cmp_k, cmp_v, cache_c_cmp_k[ic],
                                  cache_c_cmp_v[ic], cache_c_slc_k[ic], cache_c_slc_v[ic], cache_c_win_k[ic],
                                  cache_c_win_v[ic], page_table)
            for lst, arr in zip((cck_p, ccv_p, csk_p, csv_p, cwk_p, cwv_p), st1):
                lst.append(arr)
            for lst, arr in zip((cck_s, ccv_s, csk_s, csv_s, cwk_s, cwv_s), st2):
                lst.append(arr)
            ic += 1
        xp = xp + op
        xs = xs + os_
        mk, mv = mem_kv(mem_prompt, g_mem[i], x_w_kv[i])
        mk_p.append(mk); mv_p.append(mv)
        xp = xp + cross_attn(xp, g_xattn[i], mk, mv, x_w_q[i], x_w_o[i])
        xs = xs + cross_attn(xs, g_xattn[i], cache_mem_k[i], cache_mem_v[i], x_w_q[i], x_w_o[i])
        xp = ffn_residual(xp, g_ffn2[i], ffn2_wi[i], ffn2_wo[i])
        xs = ffn_residual(xs, g_ffn2[i], ffn2_wi[i], ffn2_wo[i])
    y_prompt = rms_norm(xp, g_final)
    y_sample = rms_norm(xs, g_final)
    st = jnp.stack
    return (y_prompt, y_sample,
            st(ak_p), st(av_p), st(ai_p), st(bs_p), st(bc_p),
            st(cck_p), st(ccv_p), st(csk_p), st(csv_p), st(cwk_p), st(cwv_p),
            st(mk_p), st(mv_p),
            st(ak_s), st(av_s), st(ai_s), st(bs_s), st(bc_s),
            st(cck_s), st(ccv_s), st(csk_s), st(csv_s), st(cwk_s), st(cwv_s))
```

```python
import functools

import jax
import jax.numpy as jnp
import numpy as np
from jax import lax
from jax.experimental import pallas as pl
from jax.experimental.pallas import tpu as pltpu

F32 = jnp.float32
BF16 = jnp.bfloat16
I32 = jnp.int32
INT_MIN = -2 ** 31
EPS = 1e-6

D_MODEL = 1024
DEPTH = 4
N_MIXERS = 3
HEAD_DIM = 64
ROPE_THETA = 10000.0
Q_BLOCK = 128
A_HEADS = D_MODEL // HEAD_DIM
A_KV_HEADS = 4
IDX_HEADS = 8
IDX_DIM = 64
TOPK_MAX = 256
B_D_INNER = 2 * D_MODEL
B_HEADDIM = 64
B_HEADS = B_D_INNER // B_HEADDIM
B_GROUPS = 4
B_STATE = 128
B_CONV = 4
B_CONV_DIM = B_D_INNER + 2 * B_GROUPS * B_STATE
B_CHUNK = 128
C_HEADS = D_MODEL // HEAD_DIM
C_KV_HEADS = 2
CMP_BLOCK = 32
CMP_STRIDE = 16
SLC_BLOCK = 64
N_SLC = 16
WINDOW = 512
X_HEADS = 4
X_HEAD_DIM = 128
D_FF = 2816

LANE = 128
SUBLANE = 8
VMEM_LIMIT = 48 * 1024 * 1024
MM_COLS = 512
FF_CHUNK = 256
TQ = SUBLANE
PAGES_PER_STEP = 8


def _pick_tile(n, candidates):
    for c in candidates:
        if n % c == 0:
            return c
    return n


def _resident(shape):
    return pl.BlockSpec(shape, lambda *_: (0,) * len(shape), pipeline_mode=pl.Buffered(1))


def _rms(x, g):
    return x * lax.rsqrt(jnp.mean(x * x, axis=-1, keepdims=True) + EPS) * g


def _nmm_kernel(x_ref, g_ref, w_ref, *rest, norm, residual):
    o_ref = rest[-1]
    x = x_ref[...]
    if norm:
        x = _rms(x, g_ref[...])
    xb = x.astype(BF16)
    n = w_ref.shape[1]
    for c0 in range(0, n, MM_COLS):
        c1 = min(c0 + MM_COLS, n)
        y = jnp.dot(xb, w_ref[:, c0:c1], preferred_element_type=F32)
        o_ref[:, c0:c1] = rest[0][:, c0:c1] + y if residual else y


def norm_matmul(x, w, g=None, res=None):
    m, k = x.shape
    n = w.shape[1]
    assert res is None or (n % LANE == 0 and m % SUBLANE == 0)
    n_pad = -(-n // LANE) * LANE
    wb = w.astype(BF16)
    if n_pad != n:
        wb = jnp.pad(wb, ((0, 0), (0, n_pad - n)))
    m_orig = m
    if m % SUBLANE != 0:
        m = -(-m // 256) * 256
        x = jnp.pad(x, ((0, m - m_orig), (0, 0)))
    tm = _pick_tile(m, (256, 128, 64, 32, 16, 8))
    norm = g is not None
    gg = (g if norm else jnp.ones((k,), F32)).reshape(1, k).astype(F32)
    row_block = lambda width: pl.BlockSpec((tm, width), lambda i: (i, 0))
    extra = [] if res is None else [res]
    out = pl.pallas_call(
        functools.partial(_nmm_kernel, norm=norm, residual=res is not None),
        grid=(m // tm,),
        in_specs=[row_block(k), _resident((1, k)), _resident((k, n_pad))] + [row_block(n_pad) for _ in extra],
        out_specs=row_block(n_pad),
        out_shape=jax.ShapeDtypeStruct((m, n_pad), F32),
        compiler_params=pltpu.CompilerParams(dimension_semantics=("parallel",), vmem_limit_bytes=VMEM_LIMIT),
        name="norm_matmul",
    )(x, gg, wb, *extra)
    if n_pad != n or m != m_orig:
        out = out[:m_orig, :n]
    return out


def mm(x, w, g=None, res=None):
    lead = x.shape[:-1]
    r2 = None if res is None else res.reshape(-1, res.shape[-1])
    return norm_matmul(x.reshape(-1, x.shape[-1]), w, g, r2).reshape(lead + (w.shape[1],))


def _ffn_kernel(x_ref, g_ref, wg_ref, wu_ref, wo_ref, o_ref, act_ref, *, n_chunks):
    xn = _rms(x_ref[...], g_ref[...]).astype(BF16)

    def body(c, _):
        off = pl.multiple_of(c * FF_CHUNK, FF_CHUNK)
        hg = jnp.dot(xn, wg_ref[:, pl.ds(off, FF_CHUNK)], preferred_element_type=F32)
        hu = jnp.dot(xn, wu_ref[:, pl.ds(off, FF_CHUNK)], preferred_element_type=F32)
        act_ref[:, pl.ds(off, FF_CHUNK)] = (jax.nn.silu(hg) * hu).astype(BF16)
        return 0

    lax.fori_loop(0, n_chunks, body, 0)
    for c0 in range(0, o_ref.shape[1], MM_COLS):
        cs = slice(c0, c0 + MM_COLS)
        o_ref[:, cs] = x_ref[:, cs] + 0.5 * jnp.dot(act_ref[...], wo_ref[:, cs], preferred_element_type=F32)


def ffn_residual(x, g, wi, wo):
    lead, d = x.shape[:-1], x.shape[-1]
    x2 = x.reshape(-1, d)
    m = x2.shape[0]
    f = wo.shape[0]
    assert f % FF_CHUNK == 0 and d % MM_COLS == 0
    tm = _pick_tile(m, (512, 256, 128, 64, 32, 16, 8))
    wb = wi.astype(BF16)
    out = pl.pallas_call(
        functools.partial(_ffn_kernel, n_chunks=f // FF_CHUNK),
        grid=(m // tm,),
        in_specs=[pl.BlockSpec((tm, d), lambda i: (i, 0)), _resident((1, d)), _resident((d, f)), _resident((d, f)),
                  _resident((f, d))],
        out_specs=pl.BlockSpec((tm, d), lambda i: (i, 0)),
        out_shape=jax.ShapeDtypeStruct((m, d), F32),
        scratch_shapes=[pltpu.VMEM((tm, f), BF16)],
        compiler_params=pltpu.CompilerParams(dimension_semantics=("parallel",), vmem_limit_bytes=VMEM_LIMIT),
        name="ffn_residual",
    )(x2, g.reshape(1, d).astype(F32), wb[:, :f], wb[:, f:], wo.astype(BF16))
    return out.reshape(lead + (d,))


def _rmsnorm_kernel(x_ref, g_ref, o_ref):
    o_ref[...] = _rms(x_ref[...], g_ref[...])


def rms_norm_rows(x, g):
    lead, d = x.shape[:-1], x.shape[-1]
    x2 = x.reshape(-1, d)
    m = x2.shape[0]
    tm = _pick_tile(m, (512, 256, 128, 64, 32, 16, 8))
    row_block = pl.BlockSpec((tm, d), lambda i: (i, 0))
    out = pl.pallas_call(
        _rmsnorm_kernel, grid=(m // tm,), in_specs=[row_block, _resident((1, d))], out_specs=row_block,
        out_shape=jax.ShapeDtypeStruct((m, d), F32),
        compiler_params=pltpu.CompilerParams(dimension_semantics=("parallel",)), name="rms_norm_rows",
    )(x2, g.reshape(1, d).astype(F32))
    return out.reshape(lead + (d,))


def rope_tables(pos):
    half = HEAD_DIM // 2
    inv = ROPE_THETA ** (-jnp.arange(half, dtype=F32) / half)
    ang = pos.astype(F32)[:, None] * inv[None, :]
    cos, sin = jnp.cos(ang), jnp.sin(ang)
    reps = LANE // HEAD_DIM
    return (jnp.tile(jnp.concatenate([cos, cos], axis=1), (1, reps)),
            jnp.tile(jnp.concatenate([-sin, sin], axis=1), (1, reps)))


def _rope_cols(y, cos, sin):
    w = y.shape[1]
    half = HEAD_DIM // 2
    lane = lax.broadcasted_iota(I32, y.shape, 1)
    partner = jnp.where(lane % HEAD_DIM < half, pltpu.roll(y, w - half, axis=1), pltpu.roll(y, half, axis=1))
    reps = w // LANE
    return y * jnp.tile(cos, (1, reps)) + partner * jnp.tile(sin, (1, reps))


def _proj_kernel(x_ref, g_ref, cos_ref, sin_ref, w_ref, *out_refs, segments):
    xb = _rms(x_ref[...], g_ref[...]).astype(BF16)
    cos, sin = cos_ref[...], sin_ref[...]
    k = 0
    for start, width, rope, scale, outs in segments:
        y = jnp.dot(xb, w_ref[:, start:start + width], preferred_element_type=F32)
        if rope:
            y = _rope_cols(y, cos, sin)
        if scale is not None:
            y = y * scale
        for used, dtype, transposed in outs:
            v = y[:, :used]
            out_refs[k][...] = (v.T if transposed else v).astype(dtype)
            k += 1


def project(x, g, w_cols, segments, pos_rows):
    m, kdim = x.shape
    packed, segs, out_shapes, out_specs = [], [], [], []
    tm = _pick_tile(m, (256, 128, 64, 32, 16, 8))
    start = 0
    for wc, (rope, scale, outs) in zip(w_cols, segments):
        n = wc.shape[1]
        n_pad = -(-n // LANE) * LANE
        packed.append(jnp.pad(wc.astype(BF16), ((0, 0), (0, n_pad - n))))
        segs.append((start, n_pad, rope, scale, tuple((n, dt, tr) for dt, tr in outs)))
        for dt, tr in outs:
            if tr:
                out_shapes.append(jax.ShapeDtypeStruct((n, m), dt))
                out_specs.append(pl.BlockSpec((n, tm), lambda i: (0, i)))
            else:
                out_shapes.append(jax.ShapeDtypeStruct((m, n), dt))
                out_specs.append(pl.BlockSpec((tm, n), lambda i: (i, 0)))
        start += n_pad
    wb = jnp.concatenate(packed, axis=1)
    cos, sin = rope_tables(pos_rows)
    row_block = lambda width: pl.BlockSpec((tm, width), lambda i: (i, 0))
    return pl.pallas_call(
        functools.partial(_proj_kernel, segments=tuple(segs)),
        grid=(m // tm,),
        in_specs=[row_block(kdim), _resident((1, kdim)), row_block(LANE), row_block(LANE), _resident(wb.shape)],
        out_specs=out_specs,
        out_shape=out_shapes,
        compiler_params=pltpu.CompilerParams(dimension_semantics=("parallel",), vmem_limit_bytes=VMEM_LIMIT),
        name="project",
    )(x, g.reshape(1, kdim).astype(F32), cos, sin, wb)


def _split_cols(w, widths):
    cols = np.cumsum((0,) + tuple(widths))
    return [w[:, cols[j]:cols[j + 1]] for j in range(len(widths))]


def _softmax_rows(s):
    m = jnp.max(s, axis=-1, keepdims=True)
    e = jnp.exp(s - jnp.where(m == -jnp.inf, 0.0, m))
    den = jnp.sum(e, axis=-1, keepdims=True)
    return e / jnp.where(den > 0, den, 1.0)


def _flash_step(carry, q, kt_tile, v_tile, sel, rep):
    m, l, acc = carry
    rows, tk = q.shape[0], kt_tile.shape[1]
    s = jnp.dot(q, kt_tile, preferred_element_type=F32)
    s = jnp.where(sel, s.reshape(rep, Q_BLOCK, tk), -jnp.inf).reshape(rows, tk)
    m_new = jnp.maximum(m, jnp.max(s, axis=1, keepdims=True))
    m_safe = jnp.where(m_new == -jnp.inf, 0.0, m_new)
    alpha = jnp.exp(m - m_safe)
    p = jnp.exp(s - m_safe)
    l = l * alpha + jnp.sum(p, axis=1, keepdims=True)
    pv = jnp.dot(p.astype(BF16), v_tile, preferred_element_type=F32)
    return m_new, l, acc * alpha + pv


def _flash_init(rows, hd):
    return (jnp.full((rows, 1), -jnp.inf, F32), jnp.zeros((rows, 1), F32), jnp.zeros((rows, hd), F32))


def _flash_finish(carry):
    _, l, acc = carry
    return acc / jnp.where(l > 0, l, 1.0)


def _stack_heads(x, h0, rep, hd):
    return jnp.concatenate([x[:, (h0 + r) * hd:(h0 + r + 1) * hd] for r in range(rep)], axis=0)


def _sortable_key(score):
    bits = pltpu.bitcast(score, I32)
    key = jnp.where(bits < 0, bits ^ jnp.int32(0x7FFFFFFF), bits)
    return jnp.where(score == 0.0, 0, key)


def _count_lanes(pred_fn, key_ref, nkt, tk):
    rows = key_ref.shape[0]

    def body(kt, cnt):
        keys = key_ref[:, pl.ds(pl.multiple_of(kt * tk, tk), tk)]
        for c in range(tk // LANE):
            hit = pred_fn(keys[:, c * LANE:(c + 1) * LANE], kt * tk + c * LANE)
            cnt = cnt + jnp.where(hit, 1, 0)
        return cnt

    cnt = lax.fori_loop(0, nkt, body, jnp.zeros((rows, LANE), I32))
    return jnp.sum(cnt, axis=1, keepdims=True)


def _top_k_mask_params(key_ref, nkt, tk, topk, n_valid):
    rows = key_ref.shape[0]
    lane_iota = lax.broadcasted_iota(I32, (rows, LANE), 1)

    def bit_body(it, thr):
        cand = thr + jnp.left_shift(jnp.int32(1), 31 - it)
        cnt = _count_lanes(lambda keys, c0: keys >= cand, key_ref, nkt, tk)
        return jnp.where(cnt >= topk, cand, thr)

    thr = lax.fori_loop(0, 32, bit_body, jnp.full((rows, 1), INT_MIN, I32))
    many = n_valid > topk
    thr = jnp.where(many, thr, INT_MIN + 1)
    c_gt = _count_lanes(lambda keys, c0: keys > thr, key_ref, nkt, tk)
    c_eq = _count_lanes(lambda keys, c0: keys == thr, key_ref, nkt, tk)
    need = topk - c_gt
    excess = many & (c_eq > need)
    n_col_bits = max(1, (key_ref.shape[1] - 1).bit_length())

    def tie_search():
        def jbody(it, last):
            cand = last + jnp.left_shift(jnp.int32(1), n_col_bits - 1 - it)
            cnt = _count_lanes(lambda keys, c0: (keys == thr) & (c0 + lane_iota < cand), key_ref, nkt, tk)
            return jnp.where(cnt <= need - 1, cand, last)

        return lax.fori_loop(0, n_col_bits, jbody, jnp.zeros((rows, 1), I32))

    any_excess = jnp.max(jnp.where(excess, 1, 0)) > 0
    last_tie = lax.cond(any_excess, tie_search, lambda: jnp.zeros((rows, 1), I32))
    return thr, jnp.where(excess, last_tie, jnp.int32(2 ** 30))


def _block_cover(n_cmp_rows):
    c0 = lax.broadcasted_iota(I32, (n_cmp_rows, LANE), 0) * CMP_STRIDE
    s0 = lax.broadcasted_iota(I32, (n_cmp_rows, LANE), 1) * SLC_BLOCK
    return jnp.where(c0 < s0 + SLC_BLOCK, jnp.where(c0 + CMP_BLOCK > s0, 1.0, 0.0), 0.0).astype(BF16)


def _block_importance(p_sum, cover, row_pos):
    p_hi = p_sum.astype(BF16)
    p_lo = (p_sum - p_hi.astype(F32)).astype(BF16)
    imp = jnp.dot(p_hi, cover, preferred_element_type=F32) + jnp.dot(p_lo, cover, preferred_element_type=F32)
    jb = lax.broadcasted_iota(I32, imp.shape, 1)
    cur = row_pos // SLC_BLOCK
    forced = (jb == 0) | (jb == cur) | (jb == cur - 1)
    imp = jnp.where(forced, jnp.inf, imp)
    return jnp.where(jb <= cur, imp, -jnp.inf)


def _dsa_prompt_kernel(iq_ref, iw_ref, ikt_ref, q_ref, kt_ref, v_ref, o_ref, key_ref, *, topk, tk,
                       n_idx_heads, n_groups, rep):
    i = pl.program_id(1)
    t0 = i * Q_BLOCK
    nkt = (t0 + Q_BLOCK + tk - 1) // tk
    row = t0 + lax.broadcasted_iota(I32, (Q_BLOCK, 1), 0)
    hd = kt_ref.shape[0] // n_groups
    di = ikt_ref.shape[0]

    iw = iw_ref[0]
    iq = iq_ref[0]
    iq_heads = [iq[:, h * di:(h + 1) * di] for h in range(n_idx_heads)]

    def score_body(kt, _):
        off = pl.multiple_of(kt * tk, tk)
        acc = jnp.zeros((Q_BLOCK, tk), F32)
        for h in range(n_idx_heads):
            r = jnp.dot(iq_heads[h], ikt_ref[:, pl.ds(off, tk)], preferred_element_type=F32)
            acc = acc + jnp.maximum(r, 0.0) * iw[:, h:h + 1]
        col = off + lax.broadcasted_iota(I32, (Q_BLOCK, tk), 1)
        key_ref[:, pl.ds(off, tk)] = jnp.where(col <= row, _sortable_key(acc), INT_MIN)
        return 0

    lax.fori_loop(0, nkt, score_body, 0)

    thr, last_tie = _top_k_mask_params(key_ref, nkt, tk, topk, row + 1)

    q = q_ref[0]
    q_groups = [_stack_heads(q, g * rep, rep, hd) for g in range(n_groups)]
    rows_g = rep * Q_BLOCK

    def attn_body(kt, carry):
        off = pl.multiple_of(kt * tk, tk)
        keys = key_ref[:, pl.ds(off, tk)]
        col = off + lax.broadcasted_iota(I32, (Q_BLOCK, tk), 1)
        sel = ((keys > thr) | ((keys == thr) & (col <= last_tie)))[None]
        v_t = v_ref[0, pl.ds(off, tk), :]
        return tuple(
            _flash_step(carry[g], q_groups[g], kt_ref[g * hd:(g + 1) * hd, pl.ds(off, tk)],
                        v_t[:, g * hd:(g + 1) * hd], sel, rep)
            for g in range(n_groups))

    fin = lax.fori_loop(0, nkt, attn_body, tuple(_flash_init(rows_g, hd) for _ in range(n_groups)))
    for g in range(n_groups):
        o = _flash_finish(fin[g])
        for r in range(rep):
            h = g * rep + r
            o_ref[0, :, h * hd:(h + 1) * hd] = o[r * Q_BLOCK:(r + 1) * Q_BLOCK]


def dsa_prompt_attention(q, kt, v, iq, ikt, iw, topk, n_groups, n_idx_heads, tk=512):
    b, s, hd_all = q.shape
    gd, di = kt.shape[0], ikt.shape[0]
    d = gd // n_groups
    tk = min(tk, s)
    kern = functools.partial(_dsa_prompt_kernel, topk=topk, tk=tk, n_idx_heads=n_idx_heads, n_groups=n_groups,
                             rep=hd_all // gd)
    q_block = lambda width: pl.BlockSpec((1, Q_BLOCK, width), lambda bb, i: (bb, i, 0))
    return pl.pallas_call(
        kern,
        grid=(b, s // Q_BLOCK),
        in_specs=[
            q_block(n_idx_heads * di), q_block(n_idx_heads),
            pl.BlockSpec((di, s), lambda bb, i: (0, bb)),
            q_block(hd_all),
            pl.BlockSpec((gd, s), lambda bb, i: (0, bb)),
            pl.BlockSpec((1, s, gd), lambda bb, i: (bb, 0, 0)),
        ],
        out_specs=q_block(hd_all),
        out_shape=jax.ShapeDtypeStruct((b, s, hd_all), F32),
        scratch_shapes=[pltpu.VMEM((Q_BLOCK, s), I32)],
        compiler_params=pltpu.CompilerParams(
            dimension_semantics=("parallel", "arbitrary"), vmem_limit_bytes=VMEM_LIMIT),
        name="dsa_prompt_attention",
    )(iq, iw, ikt, q, kt, v)


def _top_blocks(imp_t, n_sel):
    nb = imp_t.shape[0]
    j_iota = lax.broadcasted_iota(I32, imp_t.shape, 0)
    rank = jnp.zeros(imp_t.shape, F32)
    for k in range(nb):
        row_k = imp_t[k:k + 1, :]
        earlier = jnp.where(j_iota > k, 1.0, 0.0)
        rank = rank + jnp.where(row_k > imp_t, 1.0, jnp.where(row_k == imp_t, earlier, 0.0))
    return jnp.where(rank < n_sel, 1.0, 0.0)


def _nsa_prompt_kernel(q_ref, gate_ref, kct_ref, vc_ref, kst_ref, vs_ref, kwt_ref, vw_ref, o_ref, *,
                       tk, n_cmp, n_blk, n_groups, rep, win_len):
    i = pl.program_id(1)
    t0 = i * Q_BLOCK
    nkt = (t0 + Q_BLOCK + tk - 1) // tk
    hd = kst_ref.shape[0] // n_groups
    rows = rep * Q_BLOCK
    nc_pad = kct_ref.shape[-1]
    row = t0 + lax.broadcasted_iota(I32, (Q_BLOCK, 1), 0)
    gates = jax.nn.sigmoid(gate_ref[0])
    n_heads = n_groups * rep
    q_all = q_ref[0]

    cover = _block_cover(nc_pad)
    c_idx = lax.broadcasted_iota(I32, (Q_BLOCK, nc_pad), 1)
    c_ok = ((c_idx < n_cmp) & (c_idx * CMP_STRIDE + (CMP_BLOCK - 1) <= row))[None]
    w_start = pl.multiple_of(jnp.maximum(t0 - WINDOW, 0), Q_BLOCK)
    w_col = w_start + lax.broadcasted_iota(I32, (Q_BLOCK, win_len), 1)
    w_d = row - w_col
    w_ok = ((w_d >= 0) & (w_d < WINDOW))[None]

    for g in range(n_groups):
        q = _stack_heads(q_all, g * rep, rep, hd)
        lo, hi = g * hd, (g + 1) * hd

        s = jnp.dot(q, kct_ref[0, lo:hi, :], preferred_element_type=F32)
        s = jnp.where(c_ok, s.reshape(rep, Q_BLOCK, nc_pad), -jnp.inf)
        p = _softmax_rows(s)
        o_cmp = jnp.dot(p.reshape(rows, nc_pad).astype(BF16), vc_ref[0, :, lo:hi], preferred_element_type=F32)

        imp = _block_importance(jnp.sum(p, axis=0), cover, row)
        sel_blocks = _top_blocks(imp.T[:n_blk], min(N_SLC, n_blk))
        if n_blk < LANE:
            sel_blocks = jnp.concatenate([sel_blocks, jnp.zeros((LANE - n_blk, Q_BLOCK), F32)], axis=0)
        sel_blocks = sel_blocks.T.astype(BF16)

        def slc_body(kt, carry):
            off = pl.multiple_of(kt * tk, tk)
            col = off + lax.broadcasted_iota(I32, (Q_BLOCK, tk), 1)
            blk_of_col = (off + lax.broadcasted_iota(I32, (LANE, tk), 1)) // SLC_BLOCK
            expand = jnp.where(lax.broadcasted_iota(I32, (LANE, tk), 0) == blk_of_col, 1.0, 0.0).astype(BF16)
            hit = jnp.dot(sel_blocks, expand, preferred_element_type=F32)
            sel = (jnp.where(col <= row, hit, 0.0) > 0.5)[None]
            return _flash_step(carry, q, kst_ref[lo:hi, pl.ds(off, tk)], vs_ref[0, pl.ds(off, tk), lo:hi], sel, rep)

        o_slc = _flash_finish(lax.fori_loop(0, nkt, slc_body, _flash_init(rows, hd)))

        s = jnp.dot(q, kwt_ref[lo:hi, pl.ds(w_start, win_len)], preferred_element_type=F32)
        s = jnp.where(w_ok, s.reshape(rep, Q_BLOCK, win_len), -jnp.inf)
        p = _softmax_rows(s).reshape(rows, win_len)
        o_win = jnp.dot(p.astype(BF16), vw_ref[0, pl.ds(w_start, win_len), lo:hi], preferred_element_type=F32)

        for r in range(rep):
            h = g * rep + r
            rs = slice(r * Q_BLOCK, (r + 1) * Q_BLOCK)
            o_ref[0, :, h * hd:(h + 1) * hd] = (
                gates[:, h:h + 1] * o_cmp[rs] + gates[:, n_heads + h:n_heads + h + 1] * o_slc[rs]
                + gates[:, 2 * n_heads + h:2 * n_heads + h + 1] * o_win[rs])


def nsa_prompt_attention(q, gate_logits, kct, vc, n_cmp, kst, vs, kwt, vw, n_groups, tk=512):
    b, s, hd_all = q.shape
    gd, nc_pad = kct.shape[1], kct.shape[2]
    n_blk = -(-s // SLC_BLOCK)
    assert n_blk <= LANE and s % Q_BLOCK == 0
    tk = min(tk, s)
    win_len = min(WINDOW + Q_BLOCK, s)
    kern = functools.partial(_nsa_prompt_kernel, tk=tk, n_cmp=n_cmp, n_blk=n_blk, n_groups=n_groups,
                             rep=hd_all // gd, win_len=win_len)
    per_b = lambda *blk: pl.BlockSpec((1,) + blk, lambda bb, i: (bb,) + (0,) * len(blk))
    keys_t = pl.BlockSpec((gd, s), lambda bb, i: (0, bb))
    q_block = lambda width: pl.BlockSpec((1, Q_BLOCK, width), lambda bb, i: (bb, i, 0))
    return pl.pallas_call(
        kern,
        grid=(b, s // Q_BLOCK),
        in_specs=[q_block(hd_all), q_block(gate_logits.shape[-1]), per_b(gd, nc_pad), per_b(nc_pad, gd),
                  keys_t, per_b(s, gd), keys_t, per_b(s, gd)],
        out_specs=q_block(hd_all),
        out_shape=jax.ShapeDtypeStruct((b, s, hd_all), F32),
        compiler_params=pltpu.CompilerParams(
            dimension_semantics=("parallel", "arbitrary"), vmem_limit_bytes=VMEM_LIMIT),
        name="nsa_prompt_attention",
    )(q, gate_logits, kct, vc, kst, vs, kwt, vw)


def _compress_core(chunks, wc_ref, pew_ref, w2_ref, n_groups, hd):
    a = jnp.dot(chunks, wc_ref[...], preferred_element_type=F32)
    half = n_groups * hd
    first, second = a[:, :half], a[:, half:]
    second = jnp.concatenate([second[1:], jnp.zeros((1, half), F32)], axis=0)
    out = []
    for g in range(n_groups):
        hcol = first[:, g * hd:(g + 1) * hd] + second[:, g * hd:(g + 1) * hd] + pew_ref[...]
        out.append(jnp.dot(jax.nn.silu(hcol).astype(BF16), w2_ref[...], preferred_element_type=F32))
    return out


def _compress_kernel(c_ref, wc_ref, pew_ref, w2_ref, o_ref, *, n_groups, hd, transposed):
    out = _compress_core(c_ref[0].astype(BF16), wc_ref, pew_ref, w2_ref, n_groups, hd)
    res = jnp.concatenate(out, axis=1)
    o_ref[0] = (res.T if transposed else res).astype(BF16)


def _compress_chunk_weights(w1, n_groups, hd):
    wr = w1.reshape(2, CMP_STRIDE, hd, hd)
    eye = jnp.eye(n_groups, dtype=w1.dtype)
    wc = jnp.einsum('hrdo,gk->rgdhko', wr, eye)
    return wc.reshape(CMP_STRIDE * n_groups * hd, 2 * n_groups * hd)


def _compress_weights(pe, w1, w2, n_groups, hd):
    return _compress_chunk_weights(w1, n_groups, hd).astype(BF16), mm(pe.reshape(1, -1), w1), w2.astype(BF16)


def compress_rows(rows, cmp_w, n_groups, transposed):
    b, l, gd = rows.shape
    hd = gd // n_groups
    n_chunks = l // CMP_STRIDE
    cw = CMP_STRIDE * gd
    wc, pew, w2 = _compress_weights(*cmp_w, n_groups, hd)
    out_blk = (gd, n_chunks) if transposed else (n_chunks, gd)
    return pl.pallas_call(
        functools.partial(_compress_kernel, n_groups=n_groups, hd=hd, transposed=transposed),
        grid=(b,),
        in_specs=[pl.BlockSpec((1, n_chunks, cw), lambda bb: (bb, 0, 0)), _resident(wc.shape), _resident(pew.shape),
                  _resident(w2.shape)],
        out_specs=pl.BlockSpec((1,) + out_blk, lambda bb: (bb, 0, 0)),
        out_shape=jax.ShapeDtypeStruct((b,) + out_blk, BF16),
        compiler_params=pltpu.CompilerParams(dimension_semantics=("parallel",), vmem_limit_bytes=VMEM_LIMIT),
        name="compress_rows",
    )(rows.reshape(b, n_chunks, cw), wc, pew, w2)


def _attend_rows(q, k_rows, v_rows, sel, rep):
    n = k_rows.shape[0]
    s = lax.dot_general(q, k_rows, (((1,), (1,)), ((), ())), preferred_element_type=F32)
    s = jnp.where(sel, s.reshape(rep, TQ, n), -jnp.inf)
    p = _softmax_rows(s).reshape(rep * TQ, n)
    return p, jnp.dot(p.astype(BF16), v_rows, preferred_element_type=F32)


def _pad_rows(x, n):
    return jnp.concatenate([x, jnp.zeros((n - x.shape[0],) + x.shape[1:], x.dtype)], axis=0)


def _head_rows(x):
    b, t, h, d = x.shape
    x = jnp.pad(x.astype(BF16), ((0, 0), (0, TQ - t), (0, 0), (0, 0)))
    return x.transpose(0, 2, 1, 3).reshape(b, h * TQ, d)


def _new_rows(x):
    b, t = x.shape[:2]
    return jnp.pad(x.reshape(b, t, -1).astype(BF16), ((0, 0), (0, TQ - t), (0, 0)))


def _page_specs(block, n_pages, pp):
    def spec(j):
        return pl.BlockSpec(block, lambda bb, p, pt: (pt[bb * n_pages + p * pp + j],) + (0,) * (len(block) - 1))
    return [spec(j) for j in range(pp)]


def _per_batch(*blk):
    return pl.BlockSpec((1,) + blk, lambda bb, p, pt: (bb,) + (0,) * len(blk))


def _whole(*blk):
    return pl.BlockSpec(blk, lambda bb, p, pt: (0,) * len(blk))


def _dsa_sample_kernel(pt_ref, iq_ref, iw_ref, q_ref, ikn_ref, kn_ref, vn_ref, *rest, topk, past, n_new,
                       n_idx_heads, n_groups, rep, pp):
    idx_pages = rest[:pp]
    k_pages = rest[pp:2 * pp]
    v_pages = rest[2 * pp:3 * pp]
    o_ref, key_ref, ksc, vsc = rest[3 * pp:]
    p_step = pl.program_id(1)
    hd = q_ref.shape[-1]
    page = idx_pages[0].shape[1]
    n_tiles = past // page + 1
    iw = iw_ref[0]
    tpos = past + jnp.minimum(lax.broadcasted_iota(I32, (TQ, 1), 0), n_new - 1)

    def scores(ik_rows):
        r = lax.dot_general(iq_ref[0], ik_rows, (((1,), (1,)), ((), ())), preferred_element_type=F32)
        r = jnp.maximum(r, 0.0).reshape(n_idx_heads, TQ, ik_rows.shape[0]) * iw
        return jnp.sum(r, axis=0)

    for j in range(pp):
        off = pl.multiple_of((p_step * pp + j) * page, page)
        key_ref[:, pl.ds(off, page)] = _sortable_key(scores(idx_pages[j][0].astype(BF16)))
        ksc[pl.ds(off, page), :] = k_pages[j][0].astype(BF16)
        vsc[pl.ds(off, page), :] = v_pages[j][0].astype(BF16)

    @pl.when(p_step == pl.num_programs(1) - 1)
    def _():
        c = lax.broadcasted_iota(I32, (TQ, page), 1)
        fresh_ok = (c < n_new) & (past + c <= tpos)
        key_ref[:, past:past + page] = jnp.where(fresh_ok, _sortable_key(scores(_pad_rows(ikn_ref[0], page))), INT_MIN)
        ksc[past:past + page, :] = _pad_rows(kn_ref[0], page)
        vsc[past:past + page, :] = _pad_rows(vn_ref[0], page)
        thr, last_tie = _top_k_mask_params(key_ref, n_tiles, page, topk, tpos + 1)
        keys = key_ref[...]
        col = lax.broadcasted_iota(I32, keys.shape, 1)
        sel = ((keys > thr) | ((keys == thr) & (col <= last_tie)))[None]
        for g in range(n_groups):
            _, o = _attend_rows(q_ref[0, g * rep * TQ:(g + 1) * rep * TQ], ksc[:, g * hd:(g + 1) * hd],
                                vsc[:, g * hd:(g + 1) * hd], sel, rep)
            for r in range(rep):
                h = g * rep + r
                o_ref[0, :, h * hd:(h + 1) * hd] = o[r * TQ:(r + 1) * TQ]


def dsa_sample_attention(q, k, v, iq, ik, iw, pool_k, pool_v, pool_idx, page_table, topk):
    b, t, h, d = q.shape
    g = k.shape[2]
    hi, di = iq.shape[2], iq.shape[3]
    n_pages = page_table.shape[1]
    page = pool_k.shape[1]
    past = n_pages * page
    pp = min(PAGES_PER_STEP, n_pages)
    assert n_pages % pp == 0 and t <= TQ
    n_phys = pool_k.shape[0]
    iw_h = jnp.pad(iw.astype(F32), ((0, 0), (0, TQ - t), (0, 0))).transpose(0, 2, 1)[..., None]
    kern = functools.partial(_dsa_sample_kernel, topk=topk, past=past, n_new=t, n_idx_heads=hi, n_groups=g,
                             rep=h // g, pp=pp)
    o = pl.pallas_call(
        kern,
        grid_spec=pltpu.PrefetchScalarGridSpec(
            num_scalar_prefetch=1,
            grid=(b, n_pages // pp),
            in_specs=[_per_batch(hi * TQ, di), _per_batch(hi, TQ, 1), _per_batch(h * TQ, d), _per_batch(TQ, di),
                      _per_batch(TQ, g * d), _per_batch(TQ, g * d)]
                     + _page_specs((1, page, di), n_pages, pp)
                     + _page_specs((1, page, g * d), n_pages, pp)
                     + _page_specs((1, page, g * d), n_pages, pp),
            out_specs=_per_batch(TQ, h * d),
            scratch_shapes=[pltpu.VMEM((TQ, past + page), I32), pltpu.VMEM((past + page, g * d), BF16),
                            pltpu.VMEM((past + page, g * d), BF16)],
        ),
        out_shape=jax.ShapeDtypeStruct((b, TQ, h * d), F32),
        compiler_params=pltpu.CompilerParams(
            dimension_semantics=("parallel", "arbitrary"), vmem_limit_bytes=VMEM_LIMIT),
        name="dsa_sample_attention",
    )(page_table.reshape(-1).astype(I32), _head_rows(iq), iw_h, _head_rows(q), _new_rows(ik),
      _new_rows(k), _new_rows(v),
      *([pool_idx] * pp), *([pool_k.reshape(n_phys, page, g * d)] * pp), *([pool_v.reshape(n_phys, page, g * d)] * pp))
    return o[:, :t]


def _top_blocks_rows(imp, n_blk, n_sel):
    j_iota = lax.broadcasted_iota(I32, imp.shape, 1)
    rank = jnp.zeros(imp.shape, F32)
    for k in range(n_blk):
        col_k = imp[:, k:k + 1]
        earlier = jnp.where(j_iota > k, 1.0, 0.0)
        rank = rank + jnp.where(col_k > imp, 1.0, jnp.where(col_k == imp, earlier, 0.0))
    return jnp.where((rank < n_sel) & (j_iota < n_blk), 1.0, 0.0)


def _nsa_sample_kernel(pt_ref, q_ref, gate_ref, ksn_ref, vsn_ref, kwn_ref, vwn_ref, wink_ref, winv_ref,
                       wck_ref, wcv_ref, pewk_ref, pewv_ref, w2k_ref, w2v_ref, *rest, past, n_new, n_groups, rep, pp):
    kc_pages = rest[:pp]
    vc_pages = rest[pp:2 * pp]
    ks_pages = rest[2 * pp:3 * pp]
    vs_pages = rest[3 * pp:4 * pp]
    o_ref, kcs, vcs, kss, vss, kws, vws = rest[4 * pp:]
    p_step = pl.program_id(1)
    hd = q_ref.shape[-1]
    page = ks_pages[0].shape[1]
    cpp = kc_pages[0].shape[1]
    n_chunks = kcs.shape[0]
    n_cmp = n_chunks - CMP_BLOCK // CMP_STRIDE + 1
    n_blk = -(-(past + n_new) // SLC_BLOCK)
    w_buf = wink_ref.shape[1]
    n_heads = n_groups * rep
    rows = rep * TQ

    for j in range(pp):
        pg = p_step * pp + j
        coff = pl.multiple_of(pg * cpp, cpp)
        kcs[pl.ds(coff, cpp), :] = kc_pages[j][0]
        vcs[pl.ds(coff, cpp), :] = vc_pages[j][0]
        off = pl.multiple_of(pg * page, page)
        kss[pl.ds(off, page), :] = ks_pages[j][0].astype(BF16)
        vss[pl.ds(off, page), :] = vs_pages[j][0].astype(BF16)

    @pl.when(p_step == pl.num_programs(1) - 1)
    def _():
        kss[past:past + page, :] = _pad_rows(ksn_ref[0], page)
        vss[past:past + page, :] = _pad_rows(vsn_ref[0], page)
        kws[0:w_buf, :] = wink_ref[0].astype(BF16)
        vws[0:w_buf, :] = winv_ref[0].astype(BF16)
        kws[w_buf:w_buf + LANE, :] = _pad_rows(kwn_ref[0], LANE)
        vws[w_buf:w_buf + LANE, :] = _pad_rows(vwn_ref[0], LANE)

        tpos = past + jnp.minimum(lax.broadcasted_iota(I32, (TQ, 1), 0), n_new - 1)
        gates = jax.nn.sigmoid(gate_ref[0])

        def compress(chunks_ref, wc_ref, pew_ref, w2_ref):
            out = _compress_core(chunks_ref[...].astype(BF16), wc_ref, pew_ref, w2_ref, n_groups, hd)
            return [o.astype(BF16) for o in out]

        kcmp = compress(kcs, wck_ref, pewk_ref, w2k_ref)
        vcmp = compress(vcs, wcv_ref, pewv_ref, w2v_ref)

        c_idx = lax.broadcasted_iota(I32, (TQ, n_chunks), 1)
        c_ok = ((c_idx < n_cmp) & (c_idx * CMP_STRIDE + (CMP_BLOCK - 1) <= tpos))[None]
        cover = _block_cover(n_chunks)
        n_slc = kss.shape[0]
        col = lax.broadcasted_iota(I32, (TQ, n_slc), 1)
        blk_of_col = lax.broadcasted_iota(I32, (LANE, n_slc), 1) // SLC_BLOCK
        expand = jnp.where(lax.broadcasted_iota(I32, (LANE, n_slc), 0) == blk_of_col, 1.0, 0.0).astype(BF16)
        n_win = kws.shape[0]
        w_c = lax.broadcasted_iota(I32, (TQ, n_win), 1)
        w_pos = past - w_buf + w_c
        w_d = tpos - w_pos
        w_ok = ((w_c < w_buf + n_new) & (w_pos >= 0) & (w_d >= 0) & (w_d < WINDOW))[None]

        for g in range(n_groups):
            q = q_ref[0, g * rows:(g + 1) * rows]
            lo, hi = g * hd, (g + 1) * hd
            p_cmp, o_cmp = _attend_rows(q, kcmp[g], vcmp[g], c_ok, rep)
            imp = _block_importance(jnp.sum(p_cmp.reshape(rep, TQ, n_chunks), axis=0), cover, tpos)
            sel_blocks = _top_blocks_rows(imp, n_blk, min(N_SLC, n_blk)).astype(BF16)
            hit = jnp.dot(sel_blocks, expand, preferred_element_type=F32)
            sel = (jnp.where(col <= tpos, hit, 0.0) > 0.5)[None]
            _, o_slc = _attend_rows(q, kss[:, lo:hi], vss[:, lo:hi], sel, rep)
            _, o_win = _attend_rows(q, kws[:, lo:hi], vws[:, lo:hi], w_ok, rep)
            for r in range(rep):
                h = g * rep + r
                rs = slice(r * TQ, (r + 1) * TQ)
                o_ref[0, :, h * hd:(h + 1) * hd] = (
                    gates[:, h:h + 1] * o_cmp[rs] + gates[:, n_heads + h:n_heads + h + 1] * o_slc[rs]
                    + gates[:, 2 * n_heads + h:2 * n_heads + h + 1] * o_win[rs])


def nsa_sample_attention(q, gate_logits, ks, vs, kw, vw, cmp_k, cmp_v, pool_kc, pool_vc, pool_ks, pool_vs,
                         win_k, win_v, page_table):
    b, t, h, d = q.shape
    g = ks.shape[2]
    n_pages = page_table.shape[1]
    page = pool_ks.shape[1]
    past = n_pages * page
    pp = min(PAGES_PER_STEP, n_pages)
    assert n_pages % pp == 0 and t <= TQ and (past + t) // CMP_STRIDE * CMP_STRIDE == past
    n_phys = pool_ks.shape[0]
    cpp = page // CMP_STRIDE
    n_chunks = past // CMP_STRIDE
    cw = CMP_STRIDE * g * d
    w_buf = win_k.shape[1]

    wck, pewk, w2k = _compress_weights(*cmp_k, g, d)
    wcv, pewv, w2v = _compress_weights(*cmp_v, g, d)
    kern = functools.partial(_nsa_sample_kernel, past=past, n_new=t, n_groups=g, rep=h // g, pp=pp)
    o = pl.pallas_call(
        kern,
        grid_spec=pltpu.PrefetchScalarGridSpec(
            num_scalar_prefetch=1,
            grid=(b, n_pages // pp),
            in_specs=[_per_batch(h * TQ, d), _per_batch(TQ, 3 * h), _per_batch(TQ, g * d), _per_batch(TQ, g * d),
                      _per_batch(TQ, g * d), _per_batch(TQ, g * d), _per_batch(w_buf, g * d), _per_batch(w_buf, g * d),
                      _whole(cw, 2 * g * d), _whole(cw, 2 * g * d), _whole(1, d), _whole(1, d), _whole(d, d),
                      _whole(d, d)]
                     + _page_specs((1, cpp, cw), n_pages, pp) + _page_specs((1, cpp, cw), n_pages, pp)
                     + _page_specs((1, page, g * d), n_pages, pp) + _page_specs((1, page, g * d), n_pages, pp),
            out_specs=_per_batch(TQ, h * d),
            scratch_shapes=[pltpu.VMEM((n_chunks, cw), F32), pltpu.VMEM((n_chunks, cw), F32),
                            pltpu.VMEM((past + page, g * d), BF16), pltpu.VMEM((past + page, g * d), BF16),
                            pltpu.VMEM((w_buf + LANE, g * d), BF16), pltpu.VMEM((w_buf + LANE, g * d), BF16)],
        ),
        out_shape=jax.ShapeDtypeStruct((b, TQ, h * d), F32),
        compiler_params=pltpu.CompilerParams(
            dimension_semantics=("parallel", "arbitrary"), vmem_limit_bytes=VMEM_LIMIT),
        name="nsa_sample_attention",
    )(page_table.reshape(-1).astype(I32), _head_rows(q),
      jnp.pad(gate_logits.astype(F32), ((0, 0), (0, TQ - t), (0, 0))),
      _new_rows(ks), _new_rows(vs), _new_rows(kw), _new_rows(vw),
      win_k.reshape(b, w_buf, g * d), win_v.reshape(b, w_buf, g * d),
      wck, wcv, pewk, pewv, w2k, w2v,
      *([pool_kc.reshape(n_phys, cpp, cw)] * pp), *([pool_vc.reshape(n_phys, cpp, cw)] * pp),
      *([pool_ks.reshape(n_phys, page, g * d)] * pp), *([pool_vs.reshape(n_phys, page, g * d)] * pp))
    return o[:, :t]


def last_rows(a, n):
    t = a.shape[1]
    if t >= n:
        return a[:, t - n:]
    return jnp.pad(a, ((0, 0), (n - t, 0)) + ((0, 0),) * (a.ndim - 2))


def mem_kv(mem, g, w_kv):
    b, m, _ = mem.shape
    k, v = jnp.split(mm(mem, w_kv, g), 2, axis=-1)
    return k.reshape(b, m, X_HEADS, X_HEAD_DIM), v.reshape(b, m, X_HEADS, X_HEAD_DIM)


def cross_attn(x, g, mk, mv, w_q, w_o):
    b, t, _ = x.shape
    q = mm(x, w_q, g).reshape(b, t, X_HEADS, X_HEAD_DIM)
    s = jnp.einsum('bthd,bmhd->bhtm', q, mk).astype(F32) * (X_HEAD_DIM ** -0.5)
    p = jax.nn.softmax(s, axis=-1).astype(x.dtype)
    return mm(jnp.einsum('bhtm,bmhd->bthd', p, mv).reshape(b, t, -1), w_o, res=x)


A_WIDTHS = (A_HEADS * HEAD_DIM, A_KV_HEADS * HEAD_DIM, A_KV_HEADS * HEAD_DIM, IDX_HEADS * IDX_DIM, IDX_DIM, IDX_HEADS)


def dsa_project(x, g, w_in, pos, keys_transposed):
    b, t, d = x.shape
    keys = [(F32, False), (BF16, keys_transposed)]
    segments = [(True, HEAD_DIM ** -0.5, [(BF16, False)]), (True, None, keys), (False, None, [(F32, False), (BF16, False)]),
                (True, None, [(BF16, False)]), (True, None, keys),
                (False, IDX_HEADS ** -0.5 * IDX_DIM ** -0.5, [(F32, False)])]
    return project(x.reshape(b * t, d), g, _split_cols(w_in, A_WIDTHS), segments, jnp.tile(pos, b))


def dsa_prompt(x, g, w_in, w_out):
    b, s, _ = x.shape
    q, k, kt, v, vb, iq, ik, ikt, iw = dsa_project(x, g, w_in, jnp.arange(s), True)
    o = dsa_prompt_attention(q.reshape(b, s, -1), kt, vb.reshape(b, s, -1), iq.reshape(b, s, -1), ikt,
                             iw.reshape(b, s, -1), min(TOPK_MAX, s // 4), A_KV_HEADS, IDX_HEADS)
    state = (k.reshape(b, s, A_KV_HEADS, HEAD_DIM), v.reshape(b, s, A_KV_HEADS, HEAD_DIM), ik.reshape(b, s, IDX_DIM))
    return mm(o, w_out, res=x), state


def dsa_sample(x, g, w_in, w_out, pool_k, pool_v, pool_idx, page_table):
    b, t, _ = x.shape
    past = page_table.shape[1] * pool_k.shape[1]
    q, k, kb, v, vb, iq, ik, ikb, iw = dsa_project(x, g, w_in, past + jnp.arange(t), False)
    o = dsa_sample_attention(q.reshape(b, t, A_HEADS, HEAD_DIM), kb.reshape(b, t, A_KV_HEADS, HEAD_DIM),
                             vb.reshape(b, t, A_KV_HEADS, HEAD_DIM), iq.reshape(b, t, IDX_HEADS, IDX_DIM),
                             ikb.reshape(b, t, IDX_DIM), iw.reshape(b, t, IDX_HEADS), pool_k, pool_v, pool_idx,
                             page_table, min(TOPK_MAX, (past + t) // 4))
    state = (k.reshape(b, t, A_KV_HEADS, HEAD_DIM), v.reshape(b, t, A_KV_HEADS, HEAD_DIM), ik.reshape(b, t, IDX_DIM))
    return mm(o, w_out, res=x), state


def ssd_chunked(x, dt, a, bm, cm, h0):
    bsz, t, nh, p = x.shape
    g, n = bm.shape[2], bm.shape[3]
    r = nh // g
    l = min(B_CHUNK, t)
    nc = t // l
    xr = (x * dt[..., None]).reshape(bsz, nc, l, g, r, p)
    acs = jnp.cumsum((dt * a).reshape(bsz, nc, l, g, r), axis=2)
    br = bm.reshape(bsz, nc, l, g, n)
    cr = cm.reshape(bsz, nc, l, g, n)
    seg = acs[:, :, :, None] - acs[:, :, None, :]
    causal = jnp.tril(jnp.ones((l, l), bool))[:, :, None, None]
    decay = jnp.where(causal, jnp.exp(jnp.where(causal, seg, 0.0)), 0.0)
    cb = jnp.einsum('bcign,bcjgn->bcijg', cr, br)
    y_intra = jnp.einsum('bcijg,bcijgr,bcjgrp->bcigrp', cb, decay, xr)
    to_end = jnp.exp(acs[:, :, -1:] - acs)
    s_chunk = jnp.einsum('bclgn,bclgr,bclgrp->bcgrpn', br, to_end, xr)
    d_chunk = jnp.exp(acs[:, :, -1])

    def step(hc, inp):
        s_c, d_c = inp
        return hc * d_c[..., None, None] + s_c, hc

    h_fin, h_in = lax.scan(step, h0.reshape(bsz, g, r, p, n),
                           (jnp.moveaxis(s_chunk, 1, 0), jnp.moveaxis(d_chunk, 1, 0)))
    h_in = jnp.moveaxis(h_in, 0, 1)
    y_inter = jnp.einsum('bcign,bcigr,bcgrpn->bcigrp', cr, jnp.exp(acs), h_in)
    return (y_intra + y_inter).reshape(bsz, t, nh, p), h_fin.reshape(bsz, nh, p, n)


def mamba_mixer(x, g, w_in, conv_w, conv_b, dt_bias, a_log, d_skip, g_norm, w_out, conv_buf, ssm0):
    b, t, _ = x.shape
    proj = mm(x, w_in, g)
    z = proj[..., :B_D_INNER]
    xbc = proj[..., B_D_INNER:B_D_INNER + B_CONV_DIM]
    dt = proj[..., B_D_INNER + B_CONV_DIM:]
    xpad = jnp.concatenate([conv_buf.astype(xbc.dtype), xbc], axis=1)
    conv = conv_b + sum(xpad[:, j:j + t] * conv_w[j] for j in range(B_CONV))
    xbc = jax.nn.silu(conv)
    new_buf = xpad[:, t:]
    gn = B_GROUPS * B_STATE
    xs = xbc[..., :B_D_INNER].reshape(b, t, B_HEADS, B_HEADDIM).astype(F32)
    bm = xbc[..., B_D_INNER:B_D_INNER + gn].reshape(b, t, B_GROUPS, B_STATE).astype(F32)
    cm = xbc[..., B_D_INNER + gn:].reshape(b, t, B_GROUPS, B_STATE).astype(F32)
    dt = jax.nn.softplus(dt.astype(F32) + dt_bias.astype(F32))
    a = -jnp.exp(a_log.astype(F32))
    y, ssm = ssd_chunked(xs, dt, a, bm, cm, ssm0.astype(F32))
    y = (y + xs * d_skip.astype(F32)[:, None]).reshape(b, t, B_D_INNER).astype(x.dtype)
    y = y * jax.nn.silu(z)
    return mm(y, w_out, g_norm, res=x), new_buf, ssm.astype(ssm0.dtype)


C_KV_WIDTH = C_KV_HEADS * HEAD_DIM
C_WIDTHS = (C_HEADS * HEAD_DIM,) + (C_KV_WIDTH,) * 6 + (3 * C_HEADS,)


def nsa_project(x, g, w_in, pos, keys_transposed):
    b, t, d = x.shape
    keys = (True, None, [(F32, False), (BF16, keys_transposed)])
    vals = (False, None, [(F32, False), (BF16, False)])
    segments = [(True, HEAD_DIM ** -0.5, [(BF16, False)]), (True, None, [(F32, False)]), (False, None, [(F32, False)]),
                keys, vals, keys, vals, (False, None, [(F32, False)])]
    return project(x.reshape(b * t, d), g, _split_cols(w_in, C_WIDTHS), segments, jnp.tile(pos, b))


def nsa_prompt(x, g, w_in, w_out, cmp_k, cmp_v, w_buf):
    b, s, _ = x.shape
    q, kc, vc, ks, kst, vs, vsb, kw, kwt, vw, vwb, gates = nsa_project(x, g, w_in, jnp.arange(s), True)
    seq = lambda a: a.reshape(b, s, -1)
    kct = compress_rows(seq(kc), cmp_k, C_KV_HEADS, True)
    vcc = compress_rows(seq(vc), cmp_v, C_KV_HEADS, False)
    n_cmp = s // CMP_STRIDE - CMP_BLOCK // CMP_STRIDE + 1
    o = nsa_prompt_attention(seq(q), seq(gates), kct, vcc, n_cmp, kst, seq(vsb), kwt, seq(vwb), C_KV_HEADS)
    heads = lambda a: a.reshape(b, s, C_KV_HEADS, HEAD_DIM)
    state = (heads(kc), heads(vc), heads(ks), heads(vs), last_rows(heads(kw), w_buf), last_rows(heads(vw), w_buf))
    return mm(o, w_out, res=x), state


def nsa_sample(x, g, w_in, w_out, cmp_k, cmp_v, pool_kc, pool_vc, pool_ks, pool_vs, win_k, win_v, page_table):
    b, t, _ = x.shape
    past = page_table.shape[1] * pool_kc.shape[1]
    q, kc, vc, ks, ksb, vs, vsb, kw, kwb, vw, vwb, gates = nsa_project(x, g, w_in, past + jnp.arange(t), False)
    heads = lambda a: a.reshape(b, t, C_KV_HEADS, HEAD_DIM)
    o = nsa_sample_attention(q.reshape(b, t, C_HEADS, HEAD_DIM), gates.reshape(b, t, -1), heads(ksb), heads(vsb),
                             heads(kwb), heads(vwb), cmp_k, cmp_v, pool_kc, pool_vc, pool_ks, pool_vs, win_k, win_v,
                             page_table)
    w_buf = win_k.shape[1]
    kwin = jnp.concatenate([win_k, heads(kw)], axis=1)
    vwin = jnp.concatenate([win_v, heads(vw)], axis=1)
    state = (heads(kc), heads(vc), heads(ks), heads(vs), last_rows(kwin, w_buf), last_rows(vwin, w_buf))
    return mm(o, w_out, res=x), state


def kernel(x_prompt, x_sample, cache_a_k, cache_a_v, cache_a_idx, state_b_ssm, state_b_conv, cache_c_cmp_k, cache_c_cmp_v, cache_c_slc_k, cache_c_slc_v, cache_c_win_k, cache_c_win_v, cache_mem_k, cache_mem_v, page_table, mem_prompt, g_ffn1, ffn1_wi, ffn1_wo, g_mix, g_xattn, g_mem, x_w_q, x_w_kv, x_w_o, g_ffn2, ffn2_wi, ffn2_wo, g_final, a_w_in, a_w_out, b_w_in, b_conv_w, b_conv_b, b_dt_bias, b_a_log, b_d_skip, b_g_norm, b_w_out, c_w_in, c_w_out, c_pe_k, c_w1_k, c_w2_k, c_pe_v, c_w1_v, c_w2_v):
    xp, xs = x_prompt, x_sample
    w_buf = cache_c_win_k.shape[2]
    ak_p, av_p, ai_p, bs_p, bc_p = [], [], [], [], []
    cck_p, ccv_p, csk_p, csv_p, cwk_p, cwv_p = [], [], [], [], [], []
    mk_p, mv_p = [], []
    ak_s, av_s, ai_s, bs_s, bc_s = [], [], [], [], []
    cck_s, ccv_s, csk_s, csv_s, cwk_s, cwv_s = [], [], [], [], [], []
    ia = ib = ic = 0
    for i in range(DEPTH):
        xp = ffn_residual(xp, g_ffn1[i], ffn1_wi[i], ffn1_wo[i])
        xs = ffn_residual(xs, g_ffn1[i], ffn1_wi[i], ffn1_wo[i])
        kind = i % N_MIXERS
        if kind == 0:
            op, (k1, v1, i1) = dsa_prompt(xp, g_mix[i], a_w_in[ia], a_w_out[ia])
            os_, (k2, v2, i2) = dsa_sample(xs, g_mix[i], a_w_in[ia], a_w_out[ia], cache_a_k[ia], cache_a_v[ia],
                                           cache_a_idx[ia], page_table)
            ak_p.append(k1); av_p.append(v1); ai_p.append(i1)
            ak_s.append(k2); av_s.append(v2); ai_s.append(i2)
            ia += 1
        elif kind == 1:
            conv0 = jnp.zeros((xp.shape[0], B_CONV - 1, B_CONV_DIM), xp.dtype)
            ssm0 = jnp.zeros((xp.shape[0], B_HEADS, B_HEADDIM, B_STATE), F32)
            wts = (b_w_in[ib], b_conv_w[ib], b_conv_b[ib], b_dt_bias[ib], b_a_log[ib], b_d_skip[ib],
                   b_g_norm[ib], b_w_out[ib])
            op, c1, s1 = mamba_mixer(xp, g_mix[i], *wts, conv0, ssm0)
            os_, c2, s2 = mamba_mixer(xs, g_mix[i], *wts, state_b_conv[ib], state_b_ssm[ib])
            bc_p.append(c1); bs_p.append(s1); bc_s.append(c2); bs_s.append(s2)
            ib += 1
        else:
            cmp_k = (c_pe_k[ic], c_w1_k[ic], c_w2_k[ic])
            cmp_v = (c_pe_v[ic], c_w1_v[ic], c_w2_v[ic])
            op, st1 = nsa_prompt(xp, g_mix[i], c_w_in[ic], c_w_out[ic], cmp_k, cmp_v, w_buf)
            os_, st2 = nsa_sample(xs, g_mix[i], c_w_in[ic], c_w_out[ic], cmp_k, cmp_v, cache_c_cmp_k[ic],
                                  cache_c_cmp_v[ic], cache_c_slc_k[ic], cache_c_slc_v[ic], cache_c_win_k[ic],
                                  cache_c_win_v[ic], page_table)
            for lst, arr in zip((cck_p, ccv_p, csk_p, csv_p, cwk_p, cwv_p), st1):
                lst.append(arr)
            for lst, arr in zip((cck_s, ccv_s, csk_s, csv_s, cwk_s, cwv_s), st2):
                lst.append(arr)
            ic += 1
        xp, xs = op, os_
        mk, mv = mem_kv(mem_prompt, g_mem[i], x_w_kv[i])
        mk_p.append(mk); mv_p.append(mv)
        xp = cross_attn(xp, g_xattn[i], mk, mv, x_w_q[i], x_w_o[i])
        xs = cross_attn(xs, g_xattn[i], cache_mem_k[i], cache_mem_v[i], x_w_q[i], x_w_o[i])
        xp = ffn_residual(xp, g_ffn2[i], ffn2_wi[i], ffn2_wo[i])
        xs = ffn_residual(xs, g_ffn2[i], ffn2_wi[i], ffn2_wo[i])
    y_prompt = rms_norm_rows(xp, g_final)
    y_sample = rms_norm_rows(xs, g_final)
    st = jnp.stack
    return (y_prompt, y_sample,
            st(ak_p), st(av_p), st(ai_p), st(bs_p), st(bc_p),
            st(cck_p), st(ccv_p), st(csk_p), st(csv_p), st(cwk_p), st(cwv_p),
            st(mk_p), st(mv_p),
            st(ak_s), st(av_s), st(ai_s), st(bs_s), st(bc_s),
            st(cck_s), st(ccv_s), st(csk_s), st(csv_s), st(cwk_s), st(cwv_s))
```

```python
import functools

import jax
import jax.numpy as jnp
import numpy as np
from jax import lax
from jax.experimental import pallas as pl
from jax.experimental.pallas import tpu as pltpu

F32 = jnp.float32
BF16 = jnp.bfloat16
I32 = jnp.int32
INT_MIN = -2 ** 31
EPS = 1e-6

D_MODEL = 1024
DEPTH = 4
N_MIXERS = 3
HEAD_DIM = 64
ROPE_THETA = 10000.0
Q_BLOCK = 128
A_HEADS = D_MODEL // HEAD_DIM
A_KV_HEADS = 4
IDX_HEADS = 8
IDX_DIM = 64
TOPK_MAX = 256
B_D_INNER = 2 * D_MODEL
B_HEADDIM = 64
B_HEADS = B_D_INNER // B_HEADDIM
B_GROUPS = 4
B_STATE = 128
B_CONV = 4
B_CONV_DIM = B_D_INNER + 2 * B_GROUPS * B_STATE
B_CHUNK = 128
C_HEADS = D_MODEL // HEAD_DIM
C_KV_HEADS = 2
CMP_BLOCK = 32
CMP_STRIDE = 16
SLC_BLOCK = 64
N_SLC = 16
WINDOW = 512
X_HEADS = 4
X_HEAD_DIM = 128
D_FF = 2816

LANE = 128
SUBLANE = 8
VMEM_LIMIT = 48 * 1024 * 1024
MM_COLS = 512
FF_CHUNK = 256
TQ = SUBLANE
PAGES_PER_STEP = 8


def _pick_tile(n, candidates):
    for c in candidates:
        if n % c == 0:
            return c
    return n


def _resident(shape):
    return pl.BlockSpec(shape, lambda *_: (0,) * len(shape), pipeline_mode=pl.Buffered(1))


def _rms(x, g):
    return x * lax.rsqrt(jnp.mean(x * x, axis=-1, keepdims=True) + EPS) * g


def _nmm_kernel(x_ref, g_ref, w_ref, *rest, norm, residual):
    o_ref = rest[-1]
    x = x_ref[...]
    if norm:
        x = _rms(x, g_ref[...])
    xb = x.astype(BF16)
    n = w_ref.shape[1]
    for c0 in range(0, n, MM_COLS):
        c1 = min(c0 + MM_COLS, n)
        y = jnp.dot(xb, w_ref[:, c0:c1], preferred_element_type=F32)
        o_ref[:, c0:c1] = rest[0][:, c0:c1] + y if residual else y


def norm_matmul(x, w, g=None, res=None):
    m, k = x.shape
    n = w.shape[1]
    assert res is None or (n % LANE == 0 and m % SUBLANE == 0)
    n_pad = -(-n // LANE) * LANE
    wb = w.astype(BF16)
    if n_pad != n:
        wb = jnp.pad(wb, ((0, 0), (0, n_pad - n)))
    m_orig = m
    if m % SUBLANE != 0:
        m = -(-m // 256) * 256
        x = jnp.pad(x, ((0, m - m_orig), (0, 0)))
    tm = _pick_tile(m, (256, 128, 64, 32, 16, 8))
    norm = g is not None
    gg = (g if norm else jnp.ones((k,), F32)).reshape(1, k).astype(F32)
    row_block = lambda width: pl.BlockSpec((tm, width), lambda i: (i, 0))
    extra = [] if res is None else [res]
    out = pl.pallas_call(
        functools.partial(_nmm_kernel, norm=norm, residual=res is not None),
        grid=(m // tm,),
        in_specs=[row_block(k), _resident((1, k)), _resident((k, n_pad))] + [row_block(n_pad) for _ in extra],
        out_specs=row_block(n_pad),
        out_shape=jax.ShapeDtypeStruct((m, n_pad), F32),
        compiler_params=pltpu.CompilerParams(dimension_semantics=("parallel",), vmem_limit_bytes=VMEM_LIMIT),
        name="norm_matmul",
    )(x, gg, wb, *extra)
    if n_pad != n or m != m_orig:
        out = out[:m_orig, :n]
    return out


def mm(x, w, g=None, res=None):
    lead = x.shape[:-1]
    r2 = None if res is None else res.reshape(-1, res.shape[-1])
    return norm_matmul(x.reshape(-1, x.shape[-1]), w, g, r2).reshape(lead + (w.shape[1],))


def _ffn_kernel(x_ref, g_ref, wg_ref, wu_ref, wo_ref, o_ref, act_ref, *, n_chunks):
    xn = _rms(x_ref[...], g_ref[...]).astype(BF16)

    def body(c, _):
        off = pl.multiple_of(c * FF_CHUNK, FF_CHUNK)
        hg = jnp.dot(xn, wg_ref[:, pl.ds(off, FF_CHUNK)], preferred_element_type=F32)
        hu = jnp.dot(xn, wu_ref[:, pl.ds(off, FF_CHUNK)], preferred_element_type=F32)
        act_ref[:, pl.ds(off, FF_CHUNK)] = (jax.nn.silu(hg) * hu).astype(BF16)
        return 0

    lax.fori_loop(0, n_chunks, body, 0)
    for c0 in range(0, o_ref.shape[1], MM_COLS):
        cs = slice(c0, c0 + MM_COLS)
        o_ref[:, cs] = x_ref[:, cs] + 0.5 * jnp.dot(act_ref[...], wo_ref[:, cs], preferred_element_type=F32)


def ffn_residual(x, g, wi, wo):
    lead, d = x.shape[:-1], x.shape[-1]
    x2 = x.reshape(-1, d)
    m = x2.shape[0]
    f = wo.shape[0]
    assert f % FF_CHUNK == 0 and d % MM_COLS == 0
    tm = _pick_tile(m, (512, 256, 128, 64, 32, 16, 8))
    wb = wi.astype(BF16)
    out = pl.pallas_call(
        functools.partial(_ffn_kernel, n_chunks=f // FF_CHUNK),
        grid=(m // tm,),
        in_specs=[pl.BlockSpec((tm, d), lambda i: (i, 0)), _resident((1, d)), _resident((d, f)), _resident((d, f)),
                  _resident((f, d))],
        out_specs=pl.BlockSpec((tm, d), lambda i: (i, 0)),
        out_shape=jax.ShapeDtypeStruct((m, d), F32),
        scratch_shapes=[pltpu.VMEM((tm, f), BF16)],
        compiler_params=pltpu.CompilerParams(dimension_semantics=("parallel",), vmem_limit_bytes=VMEM_LIMIT),
        name="ffn_residual",
    )(x2, g.reshape(1, d).astype(F32), wb[:, :f], wb[:, f:], wo.astype(BF16))
    return out.reshape(lead + (d,))


def _rmsnorm_kernel(x_ref, g_ref, o_ref):
    o_ref[...] = _rms(x_ref[...], g_ref[...])


def rms_norm_rows(x, g):
    lead, d = x.shape[:-1], x.shape[-1]
    x2 = x.reshape(-1, d)
    m = x2.shape[0]
    tm = _pick_tile(m, (512, 256, 128, 64, 32, 16, 8))
    row_block = pl.BlockSpec((tm, d), lambda i: (i, 0))
    out = pl.pallas_call(
        _rmsnorm_kernel, grid=(m // tm,), in_specs=[row_block, _resident((1, d))], out_specs=row_block,
        out_shape=jax.ShapeDtypeStruct((m, d), F32),
        compiler_params=pltpu.CompilerParams(dimension_semantics=("parallel",)), name="rms_norm_rows",
    )(x2, g.reshape(1, d).astype(F32))
    return out.reshape(lead + (d,))


def rope_tables(pos):
    half = HEAD_DIM // 2
    inv = ROPE_THETA ** (-jnp.arange(half, dtype=F32) / half)
    ang = pos.astype(F32)[:, None] * inv[None, :]
    cos, sin = jnp.cos(ang), jnp.sin(ang)
    reps = LANE // HEAD_DIM
    return (jnp.tile(jnp.concatenate([cos, cos], axis=1), (1, reps)),
            jnp.tile(jnp.concatenate([-sin, sin], axis=1), (1, reps)))


def _rope_cols(y, cos, sin):
    w = y.shape[1]
    half = HEAD_DIM // 2
    lane = lax.broadcasted_iota(I32, y.shape, 1)
    partner = jnp.where(lane % HEAD_DIM < half, pltpu.roll(y, w - half, axis=1), pltpu.roll(y, half, axis=1))
    reps = w // LANE
    return y * jnp.tile(cos, (1, reps)) + partner * jnp.tile(sin, (1, reps))


def _proj_kernel(x_ref, g_ref, cos_ref, sin_ref, w_ref, *out_refs, segments):
    xb = _rms(x_ref[...], g_ref[...]).astype(BF16)
    cos, sin = cos_ref[...], sin_ref[...]
    k = 0
    for start, width, rope, scale, outs in segments:
        y = jnp.dot(xb, w_ref[:, start:start + width], preferred_element_type=F32)
        if rope:
            y = _rope_cols(y, cos, sin)
        if scale is not None:
            y = y * scale
        for used, dtype, transposed in outs:
            v = y[:, :used]
            out_refs[k][...] = (v.T if transposed else v).astype(dtype)
            k += 1


def project(x, g, w_cols, segments, pos_rows):
    m, kdim = x.shape
    packed, segs, out_shapes, out_specs = [], [], [], []
    tm = _pick_tile(m, (256, 128, 64, 32, 16, 8))
    start = 0
    for wc, (rope, scale, outs) in zip(w_cols, segments):
        n = wc.shape[1]
        n_pad = -(-n // LANE) * LANE
        packed.append(jnp.pad(wc.astype(BF16), ((0, 0), (0, n_pad - n))))
        segs.append((start, n_pad, rope, scale, tuple((n, dt, tr) for dt, tr in outs)))
        for dt, tr in outs:
            if tr:
                out_shapes.append(jax.ShapeDtypeStruct((n, m), dt))
                out_specs.append(pl.BlockSpec((n, tm), lambda i: (0, i)))
            else:
                out_shapes.append(jax.ShapeDtypeStruct((m, n), dt))
                out_specs.append(pl.BlockSpec((tm, n), lambda i: (i, 0)))
        start += n_pad
    wb = jnp.concatenate(packed, axis=1)
    cos, sin = rope_tables(pos_rows)
    row_block = lambda width: pl.BlockSpec((tm, width), lambda i: (i, 0))
    return pl.pallas_call(
        functools.partial(_proj_kernel, segments=tuple(segs)),
        grid=(m // tm,),
        in_specs=[row_block(kdim), _resident((1, kdim)), row_block(LANE), row_block(LANE), _resident(wb.shape)],
        out_specs=out_specs,
        out_shape=out_shapes,
        compiler_params=pltpu.CompilerParams(dimension_semantics=("parallel",), vmem_limit_bytes=VMEM_LIMIT),
        name="project",
    )(x, g.reshape(1, kdim).astype(F32), cos, sin, wb)


def _split_cols(w, widths):
    cols = np.cumsum((0,) + tuple(widths))
    return [w[:, cols[j]:cols[j + 1]] for j in range(len(widths))]


def _softmax_rows(s):
    m = jnp.max(s, axis=-1, keepdims=True)
    e = jnp.exp(s - jnp.where(m == -jnp.inf, 0.0, m))
    den = jnp.sum(e, axis=-1, keepdims=True)
    return e / jnp.where(den > 0, den, 1.0)


def _flash_step(carry, q, kt_tile, v_tile, sel, rep):
    m, l, acc = carry
    rows, tk = q.shape[0], kt_tile.shape[1]
    s = jnp.dot(q, kt_tile, preferred_element_type=F32)
    s = jnp.where(sel, s.reshape(rep, Q_BLOCK, tk), -jnp.inf).reshape(rows, tk)
    m_new = jnp.maximum(m, jnp.max(s, axis=1, keepdims=True))
    m_safe = jnp.where(m_new == -jnp.inf, 0.0, m_new)
    alpha = jnp.exp(m - m_safe)
    p = jnp.exp(s - m_safe)
    l = l * alpha + jnp.sum(p, axis=1, keepdims=True)
    pv = jnp.dot(p.astype(BF16), v_tile, preferred_element_type=F32)
    return m_new, l, acc * alpha + pv


def _flash_init(rows, hd):
    return (jnp.full((rows, 1), -jnp.inf, F32), jnp.zeros((rows, 1), F32), jnp.zeros((rows, hd), F32))


def _flash_finish(carry):
    _, l, acc = carry
    return acc / jnp.where(l > 0, l, 1.0)


def _stack_heads(x, h0, rep, hd):
    return jnp.concatenate([x[:, (h0 + r) * hd:(h0 + r + 1) * hd] for r in range(rep)], axis=0)


def _sortable_key(score):
    bits = pltpu.bitcast(score, I32)
    key = jnp.where(bits < 0, bits ^ jnp.int32(0x7FFFFFFF), bits)
    return jnp.where(score == 0.0, 0, key)


def _count_lanes(pred_fn, key_ref, nkt, tk):
    rows = key_ref.shape[0]

    def body(kt, cnt):
        keys = key_ref[:, pl.ds(pl.multiple_of(kt * tk, tk), tk)]
        for c in range(tk // LANE):
            hit = pred_fn(keys[:, c * LANE:(c + 1) * LANE], kt * tk + c * LANE)
            cnt = cnt + jnp.where(hit, 1, 0)
        return cnt

    cnt = lax.fori_loop(0, nkt, body, jnp.zeros((rows, LANE), I32))
    return jnp.sum(cnt, axis=1, keepdims=True)


def _top_k_mask_params(key_ref, nkt, tk, topk, n_valid):
    rows = key_ref.shape[0]
    lane_iota = lax.broadcasted_iota(I32, (rows, LANE), 1)

    def bit_body(it, thr):
        cand = thr + jnp.left_shift(jnp.int32(1), 31 - it)
        cnt = _count_lanes(lambda keys, c0: keys >= cand, key_ref, nkt, tk)
        return jnp.where(cnt >= topk, cand, thr)

    thr = lax.fori_loop(0, 32, bit_body, jnp.full((rows, 1), INT_MIN, I32))
    many = n_valid > topk
    thr = jnp.where(many, thr, INT_MIN + 1)
    c_gt = _count_lanes(lambda keys, c0: keys > thr, key_ref, nkt, tk)
    c_eq = _count_lanes(lambda keys, c0: keys == thr, key_ref, nkt, tk)
    need = topk - c_gt
    excess = many & (c_eq > need)
    n_col_bits = max(1, (key_ref.shape[1] - 1).bit_length())

    def tie_search():
        def jbody(it, last):
            cand = last + jnp.left_shift(jnp.int32(1), n_col_bits - 1 - it)
            cnt = _count_lanes(lambda keys, c0: (keys == thr) & (c0 + lane_iota < cand), key_ref, nkt, tk)
            return jnp.where(cnt <= need - 1, cand, last)

        return lax.fori_loop(0, n_col_bits, jbody, jnp.zeros((rows, 1), I32))

    any_excess = jnp.max(jnp.where(excess, 1, 0)) > 0
    last_tie = lax.cond(any_excess, tie_search, lambda: jnp.zeros((rows, 1), I32))
    return thr, jnp.where(excess, last_tie, jnp.int32(2 ** 30))


def _block_cover(n_cmp_rows):
    c0 = lax.broadcasted_iota(I32, (n_cmp_rows, LANE), 0) * CMP_STRIDE
    s0 = lax.broadcasted_iota(I32, (n_cmp_rows, LANE), 1) * SLC_BLOCK
    return jnp.where(c0 < s0 + SLC_BLOCK, jnp.where(c0 + CMP_BLOCK > s0, 1.0, 0.0), 0.0).astype(BF16)


def _block_importance(p_sum, cover, row_pos):
    p_hi = p_sum.astype(BF16)
    p_lo = (p_sum - p_hi.astype(F32)).astype(BF16)
    imp = jnp.dot(p_hi, cover, preferred_element_type=F32) + jnp.dot(p_lo, cover, preferred_element_type=F32)
    jb = lax.broadcasted_iota(I32, imp.shape, 1)
    cur = row_pos // SLC_BLOCK
    forced = (jb == 0) | (jb == cur) | (jb == cur - 1)
    imp = jnp.where(forced, jnp.inf, imp)
    return jnp.where(jb <= cur, imp, -jnp.inf)


def _dsa_prompt_kernel(iq_ref, iw_ref, ikt_ref, q_ref, kt_ref, v_ref, o_ref, key_ref, *, topk, tk,
                       n_idx_heads, n_groups, rep):
    i = pl.program_id(1)
    t0 = i * Q_BLOCK
    nkt = (t0 + Q_BLOCK + tk - 1) // tk
    row = t0 + lax.broadcasted_iota(I32, (Q_BLOCK, 1), 0)
    hd = kt_ref.shape[0] // n_groups
    di = ikt_ref.shape[0]

    iw = iw_ref[0]
    iq = iq_ref[0]
    iq_heads = [iq[:, h * di:(h + 1) * di] for h in range(n_idx_heads)]

    def score_body(kt, _):
        off = pl.multiple_of(kt * tk, tk)
        acc = jnp.zeros((Q_BLOCK, tk), F32)
        for h in range(n_idx_heads):
            r = jnp.dot(iq_heads[h], ikt_ref[:, pl.ds(off, tk)], preferred_element_type=F32)
            acc = acc + jnp.maximum(r, 0.0) * iw[:, h:h + 1]
        col = off + lax.broadcasted_iota(I32, (Q_BLOCK, tk), 1)
        key_ref[:, pl.ds(off, tk)] = jnp.where(col <= row, _sortable_key(acc), INT_MIN)
        return 0

    lax.fori_loop(0, nkt, score_body, 0)

    thr, last_tie = _top_k_mask_params(key_ref, nkt, tk, topk, row + 1)

    q = q_ref[0]
    q_groups = [_stack_heads(q, g * rep, rep, hd) for g in range(n_groups)]
    rows_g = rep * Q_BLOCK

    def attn_body(kt, carry):
        off = pl.multiple_of(kt * tk, tk)
        keys = key_ref[:, pl.ds(off, tk)]
        col = off + lax.broadcasted_iota(I32, (Q_BLOCK, tk), 1)
        sel = ((keys > thr) | ((keys == thr) & (col <= last_tie)))[None]
        v_t = v_ref[0, pl.ds(off, tk), :]
        return tuple(
            _flash_step(carry[g], q_groups[g], kt_ref[g * hd:(g + 1) * hd, pl.ds(off, tk)],
                        v_t[:, g * hd:(g + 1) * hd], sel, rep)
            for g in range(n_groups))

    fin = lax.fori_loop(0, nkt, attn_body, tuple(_flash_init(rows_g, hd) for _ in range(n_groups)))
    for g in range(n_groups):
        o = _flash_finish(fin[g])
        for r in range(rep):
            h = g * rep + r
            o_ref[0, :, h * hd:(h + 1) * hd] = o[r * Q_BLOCK:(r + 1) * Q_BLOCK]


def dsa_prompt_attention(q, kt, v, iq, ikt, iw, topk, n_groups, n_idx_heads, tk=512):
    b, s, hd_all = q.shape
    gd, di = kt.shape[0], ikt.shape[0]
    d = gd // n_groups
    tk = min(tk, s)
    kern = functools.partial(_dsa_prompt_kernel, topk=topk, tk=tk, n_idx_heads=n_idx_heads, n_groups=n_groups,
                             rep=hd_all // gd)
    q_block = lambda width: pl.BlockSpec((1, Q_BLOCK, width), lambda bb, i: (bb, i, 0))
    return pl.pallas_call(
        kern,
        grid=(b, s // Q_BLOCK),
        in_specs=[
            q_block(n_idx_heads * di), q_block(n_idx_heads),
            pl.BlockSpec((di, s), lambda bb, i: (0, bb)),
            q_block(hd_all),
            pl.BlockSpec((gd, s), lambda bb, i: (0, bb)),
            pl.BlockSpec((1, s, gd), lambda bb, i: (bb, 0, 0)),
        ],
        out_specs=q_block(hd_all),
        out_shape=jax.ShapeDtypeStruct((b, s, hd_all), F32),
        scratch_shapes=[pltpu.VMEM((Q_BLOCK, s), I32)],
        compiler_params=pltpu.CompilerParams(
            dimension_semantics=("parallel", "arbitrary"), vmem_limit_bytes=VMEM_LIMIT),
        name="dsa_prompt_attention",
    )(iq, iw, ikt, q, kt, v)


def _top_blocks(imp_t, n_sel):
    nb = imp_t.shape[0]
    j_iota = lax.broadcasted_iota(I32, imp_t.shape, 0)
    rank = jnp.zeros(imp_t.shape, F32)
    for k in range(nb):
        row_k = imp_t[k:k + 1, :]
        earlier = jnp.where(j_iota > k, 1.0, 0.0)
        rank = rank + jnp.where(row_k > imp_t, 1.0, jnp.where(row_k == imp_t, earlier, 0.0))
    return jnp.where(rank < n_sel, 1.0, 0.0)


def _nsa_prompt_kernel(q_ref, gate_ref, kct_ref, vc_ref, kst_ref, vs_ref, kwt_ref, vw_ref, o_ref, *,
                       tk, n_cmp, n_blk, n_groups, rep, win_len):
    i = pl.program_id(1)
    t0 = i * Q_BLOCK
    nkt = (t0 + Q_BLOCK + tk - 1) // tk
    hd = kst_ref.shape[0] // n_groups
    rows = rep * Q_BLOCK
    nc_pad = kct_ref.shape[-1]
    row = t0 + lax.broadcasted_iota(I32, (Q_BLOCK, 1), 0)
    gates = jax.nn.sigmoid(gate_ref[0])
    n_heads = n_groups * rep
    q_all = q_ref[0]

    cover = _block_cover(nc_pad)
    c_idx = lax.broadcasted_iota(I32, (Q_BLOCK, nc_pad), 1)
    c_ok = ((c_idx < n_cmp) & (c_idx * CMP_STRIDE + (CMP_BLOCK - 1) <= row))[None]
    w_start = pl.multiple_of(jnp.maximum(t0 - WINDOW, 0), Q_BLOCK)
    w_col = w_start + lax.broadcasted_iota(I32, (Q_BLOCK, win_len), 1)
    w_d = row - w_col
    w_ok = ((w_d >= 0) & (w_d < WINDOW))[None]

    for g in range(n_groups):
        q = _stack_heads(q_all, g * rep, rep, hd)
        lo, hi = g * hd, (g + 1) * hd

        s = jnp.dot(q, kct_ref[0, lo:hi, :], preferred_element_type=F32)
        s = jnp.where(c_ok, s.reshape(rep, Q_BLOCK, nc_pad), -jnp.inf)
        p = _softmax_rows(s)
        o_cmp = jnp.dot(p.reshape(rows, nc_pad).astype(BF16), vc_ref[0, :, lo:hi], preferred_element_type=F32)

        imp = _block_importance(jnp.sum(p, axis=0), cover, row)
        sel_blocks = _top_blocks(imp.T[:n_blk], min(N_SLC, n_blk))
        if n_blk < LANE:
            sel_blocks = jnp.concatenate([sel_blocks, jnp.zeros((LANE - n_blk, Q_BLOCK), F32)], axis=0)
        sel_blocks = sel_blocks.T.astype(BF16)

        def slc_body(kt, carry):
            off = pl.multiple_of(kt * tk, tk)
            col = off + lax.broadcasted_iota(I32, (Q_BLOCK, tk), 1)
            blk_of_col = (off + lax.broadcasted_iota(I32, (LANE, tk), 1)) // SLC_BLOCK
            expand = jnp.where(lax.broadcasted_iota(I32, (LANE, tk), 0) == blk_of_col, 1.0, 0.0).astype(BF16)
            hit = jnp.dot(sel_blocks, expand, preferred_element_type=F32)
            sel = (jnp.where(col <= row, hit, 0.0) > 0.5)[None]
            return _flash_step(carry, q, kst_ref[lo:hi, pl.ds(off, tk)], vs_ref[0, pl.ds(off, tk), lo:hi], sel, rep)

        o_slc = _flash_finish(lax.fori_loop(0, nkt, slc_body, _flash_init(rows, hd)))

        s = jnp.dot(q, kwt_ref[lo:hi, pl.ds(w_start, win_len)], preferred_element_type=F32)
        s = jnp.where(w_ok, s.reshape(rep, Q_BLOCK, win_len), -jnp.inf)
        p = _softmax_rows(s).reshape(rows, win_len)
        o_win = jnp.dot(p.astype(BF16), vw_ref[0, pl.ds(w_start, win_len), lo:hi], preferred_element_type=F32)

        for r in range(rep):
            h = g * rep + r
            rs = slice(r * Q_BLOCK, (r + 1) * Q_BLOCK)
            o_ref[0, :, h * hd:(h + 1) * hd] = (
                gates[:, h:h + 1] * o_cmp[rs] + gates[:, n_heads + h:n_heads + h + 1] * o_slc[rs]
                + gates[:, 2 * n_heads + h:2 * n_heads + h + 1] * o_win[rs])


def nsa_prompt_attention(q, gate_logits, kct, vc, n_cmp, kst, vs, kwt, vw, n_groups, tk=512):
    b, s, hd_all = q.shape
    gd, nc_pad = kct.shape[1], kct.shape[2]
    n_blk = -(-s // SLC_BLOCK)
    assert n_blk <= LANE and s % Q_BLOCK == 0
    tk = min(tk, s)
    win_len = min(WINDOW + Q_BLOCK, s)
    kern = functools.partial(_nsa_prompt_kernel, tk=tk, n_cmp=n_cmp, n_blk=n_blk, n_groups=n_groups,
                             rep=hd_all // gd, win_len=win_len)
    per_b = lambda *blk: pl.BlockSpec((1,) + blk, lambda bb, i: (bb,) + (0,) * len(blk))
    keys_t = pl.BlockSpec((gd, s), lambda bb, i: (0, bb))
    q_block = lambda width: pl.BlockSpec((1, Q_BLOCK, width), lambda bb, i: (bb, i, 0))
    return pl.pallas_call(
        kern,
        grid=(b, s // Q_BLOCK),
        in_specs=[q_block(hd_all), q_block(gate_logits.shape[-1]), per_b(gd, nc_pad), per_b(nc_pad, gd),
                  keys_t, per_b(s, gd), keys_t, per_b(s, gd)],
        out_specs=q_block(hd_all),
        out_shape=jax.ShapeDtypeStruct((b, s, hd_all), F32),
        compiler_params=pltpu.CompilerParams(
            dimension_semantics=("parallel", "arbitrary"), vmem_limit_bytes=VMEM_LIMIT),
        name="nsa_prompt_attention",
    )(q, gate_logits, kct, vc, kst, vs, kwt, vw)


def _compress_finish(a, pew_ref, w2_ref, n_groups, hd):
    half = n_groups * hd
    first, second = a[:, :half], a[:, half:]
    second = jnp.concatenate([second[1:], jnp.zeros((1, half), F32)], axis=0)
    out = []
    for g in range(n_groups):
        hcol = first[:, g * hd:(g + 1) * hd] + second[:, g * hd:(g + 1) * hd] + pew_ref[...]
        out.append(jnp.dot(jax.nn.silu(hcol).astype(BF16), w2_ref[...], preferred_element_type=F32))
    return out


def _compress_kernel(c_ref, wc_ref, pew_ref, w2_ref, o_ref, *, n_groups, hd, transposed):
    a = jnp.dot(c_ref[0].astype(BF16), wc_ref[...], preferred_element_type=F32)
    res = jnp.concatenate(_compress_finish(a, pew_ref, w2_ref, n_groups, hd), axis=1)
    o_ref[0] = (res.T if transposed else res).astype(BF16)


def _compress_chunk_weights(w1, n_groups, hd):
    wr = w1.reshape(2, CMP_STRIDE, hd, hd)
    eye = jnp.eye(n_groups, dtype=w1.dtype)
    wc = jnp.einsum('hrdo,gk->rgdhko', wr, eye)
    return wc.reshape(CMP_STRIDE * n_groups * hd, 2 * n_groups * hd)


def _compress_weights(pe, w1, w2, n_groups, hd):
    return _compress_chunk_weights(w1, n_groups, hd).astype(BF16), mm(pe.reshape(1, -1), w1), w2.astype(BF16)


def compress_rows(rows, cmp_w, n_groups, transposed):
    b, l, gd = rows.shape
    hd = gd // n_groups
    n_chunks = l // CMP_STRIDE
    cw = CMP_STRIDE * gd
    wc, pew, w2 = _compress_weights(*cmp_w, n_groups, hd)
    out_blk = (gd, n_chunks) if transposed else (n_chunks, gd)
    return pl.pallas_call(
        functools.partial(_compress_kernel, n_groups=n_groups, hd=hd, transposed=transposed),
        grid=(b,),
        in_specs=[pl.BlockSpec((1, n_chunks, cw), lambda bb: (bb, 0, 0)), _resident(wc.shape), _resident(pew.shape),
                  _resident(w2.shape)],
        out_specs=pl.BlockSpec((1,) + out_blk, lambda bb: (bb, 0, 0)),
        out_shape=jax.ShapeDtypeStruct((b,) + out_blk, BF16),
        compiler_params=pltpu.CompilerParams(dimension_semantics=("parallel",), vmem_limit_bytes=VMEM_LIMIT),
        name="compress_rows",
    )(rows.reshape(b, n_chunks, cw), wc, pew, w2)


def _attend_cols(q, kt, vt, sel, rep):
    n = kt.shape[1]
    s = jnp.dot(q, kt, preferred_element_type=F32)
    s = jnp.where(sel, s.reshape(rep, TQ, n), -jnp.inf)
    p = _softmax_rows(s).reshape(rep * TQ, n)
    return p, lax.dot_general(p.astype(BF16), vt, (((1,), (1,)), ((), ())), preferred_element_type=F32)


def _head_rows(x):
    b, t, h, d = x.shape
    x = jnp.pad(x.astype(BF16), ((0, 0), (0, TQ - t), (0, 0), (0, 0)))
    return x.transpose(0, 2, 1, 3).reshape(b, h * TQ, d)


def _new_cols(x, width):
    b, t = x.shape[:2]
    return jnp.pad(x.reshape(b, t, -1).astype(BF16).transpose(0, 2, 1), ((0, 0), (0, 0), (0, width - t)))


def _cols_view(pool):
    lead, rows = pool.shape[:2], pool.shape[2]
    perm = (0, 1) + tuple(range(3, pool.ndim)) + (2,)
    return pool.transpose(perm).reshape(lead + (-1, rows))


def _page_specs(block, layer, n_pages, pp):
    def spec(j):
        return pl.BlockSpec((1, 1) + block,
                            lambda bb, p, pt: (layer, pt[bb * n_pages + p * pp + j]) + (0,) * len(block))
    return [spec(j) for j in range(pp)]


def _per_batch(*blk):
    return pl.BlockSpec((1,) + blk, lambda bb, p, pt: (bb,) + (0,) * len(blk))


def _whole(*blk):
    return pl.BlockSpec(blk, lambda bb, p, pt: (0,) * len(blk))


def _dsa_sample_kernel(pt_ref, iq_ref, iw_ref, q_ref, ikn_ref, kn_ref, vn_ref, *rest, topk, past, n_new,
                       n_idx_heads, n_groups, rep, pp):
    idx_pages = rest[:pp]
    k_pages = rest[pp:2 * pp]
    v_pages = rest[2 * pp:3 * pp]
    o_ref, key_ref, kst, vst = rest[3 * pp:]
    p_step = pl.program_id(1)
    hd = q_ref.shape[-1]
    page = idx_pages[0].shape[-1]
    n_tiles = past // page + 1
    iw = iw_ref[0]
    tpos = past + jnp.minimum(lax.broadcasted_iota(I32, (TQ, 1), 0), n_new - 1)

    def scores(ikt):
        r = jnp.dot(iq_ref[0], ikt, preferred_element_type=F32)
        r = jnp.maximum(r, 0.0).reshape(n_idx_heads, TQ, ikt.shape[1]) * iw
        return jnp.sum(r, axis=0)

    for j in range(pp):
        off = pl.multiple_of((p_step * pp + j) * page, page)
        key_ref[:, pl.ds(off, page)] = _sortable_key(scores(idx_pages[j][0, 0].astype(BF16)))
        kst[:, pl.ds(off, page)] = k_pages[j][0, 0].astype(BF16)
        vst[:, pl.ds(off, page)] = v_pages[j][0, 0].astype(BF16)

    @pl.when(p_step == pl.num_programs(1) - 1)
    def _():
        c = lax.broadcasted_iota(I32, (TQ, page), 1)
        fresh_ok = (c < n_new) & (past + c <= tpos)
        key_ref[:, past:past + page] = jnp.where(fresh_ok, _sortable_key(scores(ikn_ref[0])), INT_MIN)
        kst[:, past:past + page] = kn_ref[0]
        vst[:, past:past + page] = vn_ref[0]
        thr, last_tie = _top_k_mask_params(key_ref, n_tiles, page, topk, tpos + 1)
        keys = key_ref[...]
        col = lax.broadcasted_iota(I32, keys.shape, 1)
        sel = ((keys > thr) | ((keys == thr) & (col <= last_tie)))[None]
        for g in range(n_groups):
            gs = slice(g * hd, (g + 1) * hd)
            _, o = _attend_cols(q_ref[0, g * rep * TQ:(g + 1) * rep * TQ], kst[gs, :], vst[gs, :], sel, rep)
            for r in range(rep):
                h = g * rep + r
                o_ref[0, :, h * hd:(h + 1) * hd] = o[r * TQ:(r + 1) * TQ]


def dsa_sample_attention(q, k, v, iq, ik, iw, pools_k, pools_v, pools_idx, layer, page_table, topk):
    b, t, h, d = q.shape
    g = k.shape[2]
    hi, di = iq.shape[2], iq.shape[3]
    n_pages = page_table.shape[1]
    page = pools_k.shape[2]
    past = n_pages * page
    pp = min(PAGES_PER_STEP, n_pages)
    assert n_pages % pp == 0 and t <= TQ
    iw_h = jnp.pad(iw.astype(F32), ((0, 0), (0, TQ - t), (0, 0))).transpose(0, 2, 1)[..., None]
    kern = functools.partial(_dsa_sample_kernel, topk=topk, past=past, n_new=t, n_idx_heads=hi, n_groups=g,
                             rep=h // g, pp=pp)
    o = pl.pallas_call(
        kern,
        grid_spec=pltpu.PrefetchScalarGridSpec(
            num_scalar_prefetch=1,
            grid=(b, n_pages // pp),
            in_specs=[_per_batch(hi * TQ, di), _per_batch(hi, TQ, 1), _per_batch(h * TQ, d), _per_batch(di, page),
                      _per_batch(g * d, page), _per_batch(g * d, page)]
                     + _page_specs((di, page), layer, n_pages, pp)
                     + _page_specs((g * d, page), layer, n_pages, pp)
                     + _page_specs((g * d, page), layer, n_pages, pp),
            out_specs=_per_batch(TQ, h * d),
            scratch_shapes=[pltpu.VMEM((TQ, past + page), I32), pltpu.VMEM((g * d, past + page), BF16),
                            pltpu.VMEM((g * d, past + page), BF16)],
        ),
        out_shape=jax.ShapeDtypeStruct((b, TQ, h * d), F32),
        compiler_params=pltpu.CompilerParams(
            dimension_semantics=("parallel", "arbitrary"), vmem_limit_bytes=VMEM_LIMIT),
        name="dsa_sample_attention",
    )(page_table.reshape(-1).astype(I32), _head_rows(iq), iw_h, _head_rows(q), _new_cols(ik, page),
      _new_cols(k, page), _new_cols(v, page),
      *([_cols_view(pools_idx)] * pp), *([_cols_view(pools_k)] * pp), *([_cols_view(pools_v)] * pp))
    return o[:, :t]


def _top_blocks_rows(imp, n_blk, n_sel):
    j_iota = lax.broadcasted_iota(I32, imp.shape, 1)
    rank = jnp.zeros(imp.shape, F32)
    for k in range(n_blk):
        col_k = imp[:, k:k + 1]
        earlier = jnp.where(j_iota > k, 1.0, 0.0)
        rank = rank + jnp.where(col_k > imp, 1.0, jnp.where(col_k == imp, earlier, 0.0))
    return jnp.where((rank < n_sel) & (j_iota < n_blk), 1.0, 0.0)


def _nsa_sample_kernel(pt_ref, q_ref, gate_ref, ksn_ref, vsn_ref, kwn_ref, vwn_ref, wink_ref, winv_ref,
                       wck_ref, wcv_ref, pewk_ref, pewv_ref, w2k_ref, w2v_ref, *rest, past, n_new, n_groups, rep, pp):
    kc_pages = rest[:pp]
    vc_pages = rest[pp:2 * pp]
    ks_pages = rest[2 * pp:3 * pp]
    vs_pages = rest[3 * pp:4 * pp]
    o_ref, kcs, vcs, kst, vst, kwt, vwt = rest[4 * pp:]
    p_step = pl.program_id(1)
    hd = q_ref.shape[-1]
    gd = n_groups * hd
    page = ks_pages[0].shape[-1]
    n_chunks = kcs.shape[0] // CMP_STRIDE
    n_cmp = n_chunks - CMP_BLOCK // CMP_STRIDE + 1
    n_blk = -(-(past + n_new) // SLC_BLOCK)
    w_buf = wink_ref.shape[-1]
    n_heads = n_groups * rep
    rows = rep * TQ

    for j in range(pp):
        off = pl.multiple_of((p_step * pp + j) * page, page)
        kcs[pl.ds(off, page), :] = kc_pages[j][0, 0].T
        vcs[pl.ds(off, page), :] = vc_pages[j][0, 0].T
        kst[:, pl.ds(off, page)] = ks_pages[j][0, 0].astype(BF16)
        vst[:, pl.ds(off, page)] = vs_pages[j][0, 0].astype(BF16)

    @pl.when(p_step == pl.num_programs(1) - 1)
    def _():
        kst[:, past:past + page] = ksn_ref[0]
        vst[:, past:past + page] = vsn_ref[0]
        kwt[:, 0:w_buf] = wink_ref[0, 0].astype(BF16)
        vwt[:, 0:w_buf] = winv_ref[0, 0].astype(BF16)
        kwt[:, w_buf:w_buf + LANE] = kwn_ref[0]
        vwt[:, w_buf:w_buf + LANE] = vwn_ref[0]

        tpos = past + jnp.minimum(lax.broadcasted_iota(I32, (TQ, 1), 0), n_new - 1)
        gates = jax.nn.sigmoid(gate_ref[0])

        def compress(rows_ref, wc_ref, pew_ref, w2_ref):
            a = jnp.zeros((n_chunks, 2 * gd), F32)
            for r in range(CMP_STRIDE):
                xr = rows_ref[pl.ds(r, n_chunks, stride=CMP_STRIDE), :].astype(BF16)
                a = a + jnp.dot(xr, wc_ref[r * gd:(r + 1) * gd, :], preferred_element_type=F32)
            out = _compress_finish(a, pew_ref, w2_ref, n_groups, hd)
            return [o.astype(BF16) for o in out]

        kcmp = compress(kcs, wck_ref, pewk_ref, w2k_ref)
        vcmp = compress(vcs, wcv_ref, pewv_ref, w2v_ref)

        c_idx = lax.broadcasted_iota(I32, (TQ, n_chunks), 1)
        c_ok = ((c_idx < n_cmp) & (c_idx * CMP_STRIDE + (CMP_BLOCK - 1) <= tpos))[None]
        cover = _block_cover(n_chunks)
        n_slc = kst.shape[1]
        col = lax.broadcasted_iota(I32, (TQ, n_slc), 1)
        blk_of_col = lax.broadcasted_iota(I32, (LANE, n_slc), 1) // SLC_BLOCK
        expand = jnp.where(lax.broadcasted_iota(I32, (LANE, n_slc), 0) == blk_of_col, 1.0, 0.0).astype(BF16)
        n_win = kwt.shape[1]
        w_c = lax.broadcasted_iota(I32, (TQ, n_win), 1)
        w_pos = past - w_buf + w_c
        w_d = tpos - w_pos
        w_ok = ((w_c < w_buf + n_new) & (w_pos >= 0) & (w_d >= 0) & (w_d < WINDOW))[None]

        for g in range(n_groups):
            q = q_ref[0, g * rows:(g + 1) * rows]
            gs = slice(g * hd, (g + 1) * hd)
            s = lax.dot_general(q, kcmp[g], (((1,), (1,)), ((), ())), preferred_element_type=F32)
            p_cmp = _softmax_rows(jnp.where(c_ok, s.reshape(rep, TQ, n_chunks), -jnp.inf))
            o_cmp = jnp.dot(p_cmp.reshape(rows, n_chunks).astype(BF16), vcmp[g], preferred_element_type=F32)
            imp = _block_importance(jnp.sum(p_cmp, axis=0), cover, tpos)
            sel_blocks = _top_blocks_rows(imp, n_blk, min(N_SLC, n_blk)).astype(BF16)
            hit = jnp.dot(sel_blocks, expand, preferred_element_type=F32)
            sel = (jnp.where(col <= tpos, hit, 0.0) > 0.5)[None]
            _, o_slc = _attend_cols(q, kst[gs, :], vst[gs, :], sel, rep)
            _, o_win = _attend_cols(q, kwt[gs, :], vwt[gs, :], w_ok, rep)
            for r in range(rep):
                h = g * rep + r
                rs = slice(r * TQ, (r + 1) * TQ)
                o_ref[0, :, h * hd:(h + 1) * hd] = (
                    gates[:, h:h + 1] * o_cmp[rs] + gates[:, n_heads + h:n_heads + h + 1] * o_slc[rs]
                    + gates[:, 2 * n_heads + h:2 * n_heads + h + 1] * o_win[rs])


def nsa_sample_attention(q, gate_logits, ks, vs, kw, vw, cmp_k, cmp_v, pools_kc, pools_vc, pools_ks, pools_vs,
                         wins_k, wins_v, layer, page_table):
    b, t, h, d = q.shape
    g = ks.shape[2]
    gd = g * d
    n_pages = page_table.shape[1]
    page = pools_ks.shape[2]
    past = n_pages * page
    pp = min(PAGES_PER_STEP, n_pages)
    assert n_pages % pp == 0 and t <= TQ and (past + t) // CMP_STRIDE * CMP_STRIDE == past
    cw = CMP_STRIDE * gd
    w_buf = wins_k.shape[2]

    wck, pewk, w2k = _compress_weights(*cmp_k, g, d)
    wcv, pewv, w2v = _compress_weights(*cmp_v, g, d)
    kern = functools.partial(_nsa_sample_kernel, past=past, n_new=t, n_groups=g, rep=h // g, pp=pp)
    win_spec = pl.BlockSpec((1, 1, gd, w_buf), lambda bb, p, pt: (layer, bb, 0, 0))
    o = pl.pallas_call(
        kern,
        grid_spec=pltpu.PrefetchScalarGridSpec(
            num_scalar_prefetch=1,
            grid=(b, n_pages // pp),
            in_specs=[_per_batch(h * TQ, d), _per_batch(TQ, 3 * h), _per_batch(gd, page), _per_batch(gd, page),
                      _per_batch(gd, LANE), _per_batch(gd, LANE), win_spec, win_spec,
                      _whole(cw, 2 * gd), _whole(cw, 2 * gd), _whole(1, d), _whole(1, d), _whole(d, d), _whole(d, d)]
                     + _page_specs((gd, page), layer, n_pages, pp) + _page_specs((gd, page), layer, n_pages, pp)
                     + _page_specs((gd, page), layer, n_pages, pp) + _page_specs((gd, page), layer, n_pages, pp),
            out_specs=_per_batch(TQ, h * d),
            scratch_shapes=[pltpu.VMEM((past, gd), F32), pltpu.VMEM((past, gd), F32),
                            pltpu.VMEM((gd, past + page), BF16), pltpu.VMEM((gd, past + page), BF16),
                            pltpu.VMEM((gd, w_buf + LANE), BF16), pltpu.VMEM((gd, w_buf + LANE), BF16)],
        ),
        out_shape=jax.ShapeDtypeStruct((b, TQ, h * d), F32),
        compiler_params=pltpu.CompilerParams(
            dimension_semantics=("parallel", "arbitrary"), vmem_limit_bytes=VMEM_LIMIT),
        name="nsa_sample_attention",
    )(page_table.reshape(-1).astype(I32), _head_rows(q),
      jnp.pad(gate_logits.astype(F32), ((0, 0), (0, TQ - t), (0, 0))),
      _new_cols(ks, page), _new_cols(vs, page), _new_cols(kw, LANE), _new_cols(vw, LANE),
      _cols_view(wins_k), _cols_view(wins_v), wck, wcv, pewk, pewv, w2k, w2v,
      *([_cols_view(pools_kc)] * pp), *([_cols_view(pools_vc)] * pp),
      *([_cols_view(pools_ks)] * pp), *([_cols_view(pools_vs)] * pp))
    return o[:, :t]


def last_rows(a, n):
    t = a.shape[1]
    if t >= n:
        return a[:, t - n:]
    return jnp.pad(a, ((0, 0), (n - t, 0)) + ((0, 0),) * (a.ndim - 2))


def mem_kv(mem, g, w_kv):
    b, m, _ = mem.shape
    k, v = jnp.split(mm(mem, w_kv, g), 2, axis=-1)
    return k.reshape(b, m, X_HEADS, X_HEAD_DIM), v.reshape(b, m, X_HEADS, X_HEAD_DIM)


def cross_attn(x, g, mk, mv, w_q, w_o):
    b, t, _ = x.shape
    q = mm(x, w_q, g).reshape(b, t, X_HEADS, X_HEAD_DIM)
    s = jnp.einsum('bthd,bmhd->bhtm', q, mk).astype(F32) * (X_HEAD_DIM ** -0.5)
    p = jax.nn.softmax(s, axis=-1).astype(x.dtype)
    return mm(jnp.einsum('bhtm,bmhd->bthd', p, mv).reshape(b, t, -1), w_o, res=x)


A_WIDTHS = (A_HEADS * HEAD_DIM, A_KV_HEADS * HEAD_DIM, A_KV_HEADS * HEAD_DIM, IDX_HEADS * IDX_DIM, IDX_DIM, IDX_HEADS)


def dsa_project(x, g, w_in, pos, keys_transposed):
    b, t, d = x.shape
    keys = [(F32, False), (BF16, keys_transposed)]
    segments = [(True, HEAD_DIM ** -0.5, [(BF16, False)]), (True, None, keys), (False, None, [(F32, False), (BF16, False)]),
                (True, None, [(BF16, False)]), (True, None, keys),
                (False, IDX_HEADS ** -0.5 * IDX_DIM ** -0.5, [(F32, False)])]
    return project(x.reshape(b * t, d), g, _split_cols(w_in, A_WIDTHS), segments, jnp.tile(pos, b))


def dsa_prompt(x, g, w_in, w_out):
    b, s, _ = x.shape
    q, k, kt, v, vb, iq, ik, ikt, iw = dsa_project(x, g, w_in, jnp.arange(s), True)
    o = dsa_prompt_attention(q.reshape(b, s, -1), kt, vb.reshape(b, s, -1), iq.reshape(b, s, -1), ikt,
                             iw.reshape(b, s, -1), min(TOPK_MAX, s // 4), A_KV_HEADS, IDX_HEADS)
    state = (k.reshape(b, s, A_KV_HEADS, HEAD_DIM), v.reshape(b, s, A_KV_HEADS, HEAD_DIM), ik.reshape(b, s, IDX_DIM))
    return mm(o, w_out, res=x), state


def dsa_sample(x, g, w_in, w_out, pools_k, pools_v, pools_idx, layer, page_table):
    b, t, _ = x.shape
    past = page_table.shape[1] * pools_k.shape[2]
    q, k, kb, v, vb, iq, ik, ikb, iw = dsa_project(x, g, w_in, past + jnp.arange(t), False)
    o = dsa_sample_attention(q.reshape(b, t, A_HEADS, HEAD_DIM), kb.reshape(b, t, A_KV_HEADS, HEAD_DIM),
                             vb.reshape(b, t, A_KV_HEADS, HEAD_DIM), iq.reshape(b, t, IDX_HEADS, IDX_DIM),
                             ikb.reshape(b, t, IDX_DIM), iw.reshape(b, t, IDX_HEADS), pools_k, pools_v, pools_idx,
                             layer, page_table, min(TOPK_MAX, (past + t) // 4))
    state = (k.reshape(b, t, A_KV_HEADS, HEAD_DIM), v.reshape(b, t, A_KV_HEADS, HEAD_DIM), ik.reshape(b, t, IDX_DIM))
    return mm(o, w_out, res=x), state


def _shift_rows(x, prev, k):
    return jnp.concatenate([prev[prev.shape[0] - k:], x[:x.shape[0] - k]], axis=0)


def _tile_transpose(m, rows_out, cols_out):
    r, c = m.shape
    tile = m
    if c < LANE:
        tile = jnp.concatenate([tile, jnp.zeros((r, LANE - c), m.dtype)], axis=1)
    if r < LANE:
        tile = jnp.concatenate([tile, jnp.zeros((LANE - r, LANE), m.dtype)], axis=0)
    return tile.T[:rows_out, :cols_out]


def _ssd_kernel(z_ref, x_ref, dt_ref, cw_ref, cb_ref, dtb_ref, alog_ref, dskip_ref, buf_ref, h0_ref,
                y_ref, hout_ref, carry_ref, ht_ref, *, n_heads, n_groups, hd, n_state):
    c = pl.program_id(1)
    l = x_ref.shape[0]
    d_inner = n_heads * hd
    rep = n_heads // n_groups

    @pl.when(c == 0)
    def _():
        carry_ref[...] = buf_ref[0]
        for h in range(n_heads):
            ht_ref[h] = _tile_transpose(h0_ref[0, h], n_state, hd)

    x = x_ref[...]
    prev = carry_ref[...]
    conv = cb_ref[...] + x * cw_ref[B_CONV - 1:B_CONV, :]
    for k in range(1, B_CONV):
        conv = conv + _shift_rows(x, prev, k) * cw_ref[B_CONV - 1 - k:B_CONV - k, :]
    carry_ref[...] = x[l - SUBLANE:]
    act = jax.nn.silu(conv)
    gn = n_groups * n_state
    xs, bm, cm = act[:, :d_inner], act[:, d_inner:d_inner + gn], act[:, d_inner + gn:]

    dt = jax.nn.softplus(dt_ref[...] + dtb_ref[...])
    a = -jnp.exp(alog_ref[...])
    acs = dt * a
    k = 1
    while k < l:
        acs = acs + jnp.concatenate([jnp.zeros((k, n_heads), F32), acs[:l - k]], axis=0)
        k *= 2
    acs_t = jnp.concatenate([acs, jnp.zeros((l, LANE - n_heads), F32)], axis=1).T
    last = acs[l - 1:l, :]
    to_end = jnp.exp(last - acs)
    grow = jnp.exp(acs)
    chunk_decay = jnp.exp(last)
    causal = lax.broadcasted_iota(I32, (l, l), 0) >= lax.broadcasted_iota(I32, (l, l), 1)
    z = z_ref[...]
    dskip = dskip_ref[...]

    for g in range(n_groups):
        bm_g = bm[:, g * n_state:(g + 1) * n_state]
        cm_g = cm[:, g * n_state:(g + 1) * n_state].astype(BF16)
        cb = lax.dot_general(cm_g, bm_g.astype(BF16), (((1,), (1,)), ((), ())), preferred_element_type=F32)
        bm_t = bm_g.T.astype(BF16)
        for r in range(rep):
            h = g * rep + r
            hs = slice(h * hd, (h + 1) * hd)
            xs_h = xs[:, hs]
            xr = xs_h * dt[:, h:h + 1]
            seg = acs[:, h:h + 1] - acs_t[h:h + 1, :]
            decay = jnp.where(causal, jnp.exp(jnp.where(causal, seg, 0.0)), 0.0)
            y = jnp.dot((cb * decay).astype(BF16), xr.astype(BF16), preferred_element_type=F32)
            ht = ht_ref[h]
            y = y + jnp.dot(cm_g, ht.astype(BF16), preferred_element_type=F32) * grow[:, h:h + 1]
            upd = jnp.dot(bm_t, (xr * to_end[:, h:h + 1]).astype(BF16), preferred_element_type=F32)
            ht_ref[h] = ht * chunk_decay[:, h:h + 1] + upd
            y = y + xs_h * dskip[:, hs]
            y_ref[:, hs] = y * jax.nn.silu(z[:, hs])

    @pl.when(c == pl.num_programs(1) - 1)
    def _():
        for h in range(n_heads):
            hout_ref[0, h] = _tile_transpose(ht_ref[h], hd, n_state)


def ssd_prompt(z, xbc, dt, conv_w, conv_b, dt_bias, a_log, d_skip, conv_buf, ssm0, n_groups):
    b, n_heads, hd, n_state = ssm0.shape
    m, w = xbc.shape
    s = m // b
    l = min(B_CHUNK, s)
    nc = s // l
    d_inner = n_heads * hd
    buf = jnp.pad(conv_buf.astype(F32), ((0, 0), (SUBLANE - (B_CONV - 1), 0), (0, 0)))
    dskip_full = jnp.repeat(d_skip.astype(F32), hd).reshape(1, d_inner)
    rows = lambda width: pl.BlockSpec((l, width), lambda bb, c: (bb * nc + c, 0))
    state = pl.BlockSpec((1, n_heads, hd, n_state), lambda bb, c: (bb, 0, 0, 0))
    kern = functools.partial(_ssd_kernel, n_heads=n_heads, n_groups=n_groups, hd=hd, n_state=n_state)
    return pl.pallas_call(
        kern,
        grid=(b, nc),
        in_specs=[rows(d_inner), rows(w), rows(n_heads), _resident((B_CONV, w)), _resident((1, w)),
                  _resident((1, n_heads)), _resident((1, n_heads)), _resident((1, d_inner)),
                  pl.BlockSpec((1, SUBLANE, w), lambda bb, c: (bb, 0, 0)), state],
        out_specs=[rows(d_inner), state],
        out_shape=[jax.ShapeDtypeStruct((m, d_inner), F32), jax.ShapeDtypeStruct(ssm0.shape, F32)],
        scratch_shapes=[pltpu.VMEM((SUBLANE, w), F32), pltpu.VMEM((n_heads, n_state, hd), F32)],
        compiler_params=pltpu.CompilerParams(
            dimension_semantics=("parallel", "arbitrary"), vmem_limit_bytes=VMEM_LIMIT),
        name="ssd_prompt",
    )(z, xbc, dt, conv_w.astype(F32), conv_b.reshape(1, w).astype(F32), dt_bias.reshape(1, n_heads).astype(F32),
      a_log.reshape(1, n_heads).astype(F32), dskip_full, buf, ssm0.astype(F32))


def mamba_prompt(x, g, w_in, conv_w, conv_b, dt_bias, a_log, d_skip, g_norm, w_out, conv_buf, ssm0):
    b, s, d = x.shape
    plain = (False, None, [(F32, False)])
    z, xbc, dt = project(x.reshape(b * s, d), g, _split_cols(w_in, (B_D_INNER, B_CONV_DIM, B_HEADS)),
                         [plain, plain, plain], jnp.zeros((b * s,), I32))
    y, ssm = ssd_prompt(z, xbc, dt, conv_w, conv_b, dt_bias, a_log, d_skip, conv_buf, ssm0, B_GROUPS)
    xpad = jnp.concatenate([conv_buf.astype(F32), xbc.reshape(b, s, -1)[:, s - (B_CONV - 1):]], axis=1)
    new_buf = xpad[:, xpad.shape[1] - (B_CONV - 1):]
    return mm(y.reshape(b, s, -1), w_out, g_norm, res=x), new_buf, ssm.astype(ssm0.dtype)


def ssd_chunked(x, dt, a, bm, cm, h0):
    bsz, t, nh, p = x.shape
    g, n = bm.shape[2], bm.shape[3]
    r = nh // g
    l = min(B_CHUNK, t)
    nc = t // l
    xr = (x * dt[..., None]).reshape(bsz, nc, l, g, r, p)
    acs = jnp.cumsum((dt * a).reshape(bsz, nc, l, g, r), axis=2)
    br = bm.reshape(bsz, nc, l, g, n)
    cr = cm.reshape(bsz, nc, l, g, n)
    seg = acs[:, :, :, None] - acs[:, :, None, :]
    causal = jnp.tril(jnp.ones((l, l), bool))[:, :, None, None]
    decay = jnp.where(causal, jnp.exp(jnp.where(causal, seg, 0.0)), 0.0)
    cb = jnp.einsum('bcign,bcjgn->bcijg', cr, br)
    y_intra = jnp.einsum('bcijg,bcijgr,bcjgrp->bcigrp', cb, decay, xr)
    to_end = jnp.exp(acs[:, :, -1:] - acs)
    s_chunk = jnp.einsum('bclgn,bclgr,bclgrp->bcgrpn', br, to_end, xr)
    d_chunk = jnp.exp(acs[:, :, -1])

    def step(hc, inp):
        s_c, d_c = inp
        return hc * d_c[..., None, None] + s_c, hc

    h_fin, h_in = lax.scan(step, h0.reshape(bsz, g, r, p, n),
                           (jnp.moveaxis(s_chunk, 1, 0), jnp.moveaxis(d_chunk, 1, 0)))
    h_in = jnp.moveaxis(h_in, 0, 1)
    y_inter = jnp.einsum('bcign,bcigr,bcgrpn->bcigrp', cr, jnp.exp(acs), h_in)
    return (y_intra + y_inter).reshape(bsz, t, nh, p), h_fin.reshape(bsz, nh, p, n)


def mamba_mixer(x, g, w_in, conv_w, conv_b, dt_bias, a_log, d_skip, g_norm, w_out, conv_buf, ssm0):
    b, t, _ = x.shape
    proj = mm(x, w_in, g)
    z = proj[..., :B_D_INNER]
    xbc = proj[..., B_D_INNER:B_D_INNER + B_CONV_DIM]
    dt = proj[..., B_D_INNER + B_CONV_DIM:]
    xpad = jnp.concatenate([conv_buf.astype(xbc.dtype), xbc], axis=1)
    conv = conv_b + sum(xpad[:, j:j + t] * conv_w[j] for j in range(B_CONV))
    xbc = jax.nn.silu(conv)
    new_buf = xpad[:, t:]
    gn = B_GROUPS * B_STATE
    xs = xbc[..., :B_D_INNER].reshape(b, t, B_HEADS, B_HEADDIM).astype(F32)
    bm = xbc[..., B_D_INNER:B_D_INNER + gn].reshape(b, t, B_GROUPS, B_STATE).astype(F32)
    cm = xbc[..., B_D_INNER + gn:].reshape(b, t, B_GROUPS, B_STATE).astype(F32)
    dt = jax.nn.softplus(dt.astype(F32) + dt_bias.astype(F32))
    a = -jnp.exp(a_log.astype(F32))
    y, ssm = ssd_chunked(xs, dt, a, bm, cm, ssm0.astype(F32))
    y = (y + xs * d_skip.astype(F32)[:, None]).reshape(b, t, B_D_INNER).astype(x.dtype)
    y = y * jax.nn.silu(z)
    return mm(y, w_out, g_norm, res=x), new_buf, ssm.astype(ssm0.dtype)


C_KV_WIDTH = C_KV_HEADS * HEAD_DIM
C_WIDTHS = (C_HEADS * HEAD_DIM,) + (C_KV_WIDTH,) * 6 + (3 * C_HEADS,)


def nsa_project(x, g, w_in, pos, keys_transposed):
    b, t, d = x.shape
    keys = (True, None, [(F32, False), (BF16, keys_transposed)])
    vals = (False, None, [(F32, False), (BF16, False)])
    segments = [(True, HEAD_DIM ** -0.5, [(BF16, False)]), (True, None, [(F32, False)]), (False, None, [(F32, False)]),
                keys, vals, keys, vals, (False, None, [(F32, False)])]
    return project(x.reshape(b * t, d), g, _split_cols(w_in, C_WIDTHS), segments, jnp.tile(pos, b))


def nsa_prompt(x, g, w_in, w_out, cmp_k, cmp_v, w_buf):
    b, s, _ = x.shape
    q, kc, vc, ks, kst, vs, vsb, kw, kwt, vw, vwb, gates = nsa_project(x, g, w_in, jnp.arange(s), True)
    seq = lambda a: a.reshape(b, s, -1)
    kct = compress_rows(seq(kc), cmp_k, C_KV_HEADS, True)
    vcc = compress_rows(seq(vc), cmp_v, C_KV_HEADS, False)
    n_cmp = s // CMP_STRIDE - CMP_BLOCK // CMP_STRIDE + 1
    o = nsa_prompt_attention(seq(q), seq(gates), kct, vcc, n_cmp, kst, seq(vsb), kwt, seq(vwb), C_KV_HEADS)
    heads = lambda a: a.reshape(b, s, C_KV_HEADS, HEAD_DIM)
    state = (heads(kc), heads(vc), heads(ks), heads(vs), last_rows(heads(kw), w_buf), last_rows(heads(vw), w_buf))
    return mm(o, w_out, res=x), state


def nsa_sample(x, g, w_in, w_out, cmp_k, cmp_v, pools_kc, pools_vc, pools_ks, pools_vs, wins_k, wins_v, layer,
               page_table):
    b, t, _ = x.shape
    past = page_table.shape[1] * pools_kc.shape[2]
    q, kc, vc, ks, ksb, vs, vsb, kw, kwb, vw, vwb, gates = nsa_project(x, g, w_in, past + jnp.arange(t), False)
    heads = lambda a: a.reshape(b, t, C_KV_HEADS, HEAD_DIM)
    o = nsa_sample_attention(q.reshape(b, t, C_HEADS, HEAD_DIM), gates.reshape(b, t, -1), heads(ksb), heads(vsb),
                             heads(kwb), heads(vwb), cmp_k, cmp_v, pools_kc, pools_vc, pools_ks, pools_vs, wins_k,
                             wins_v, layer, page_table)
    win_k, win_v = wins_k[layer], wins_v[layer]
    w_buf = win_k.shape[1]
    kwin = jnp.concatenate([win_k, heads(kw)], axis=1)
    vwin = jnp.concatenate([win_v, heads(vw)], axis=1)
    state = (heads(kc), heads(vc), heads(ks), heads(vs), last_rows(kwin, w_buf), last_rows(vwin, w_buf))
    return mm(o, w_out, res=x), state


def kernel(x_prompt, x_sample, cache_a_k, cache_a_v, cache_a_idx, state_b_ssm, state_b_conv, cache_c_cmp_k, cache_c_cmp_v, cache_c_slc_k, cache_c_slc_v, cache_c_win_k, cache_c_win_v, cache_mem_k, cache_mem_v, page_table, mem_prompt, g_ffn1, ffn1_wi, ffn1_wo, g_mix, g_xattn, g_mem, x_w_q, x_w_kv, x_w_o, g_ffn2, ffn2_wi, ffn2_wo, g_final, a_w_in, a_w_out, b_w_in, b_conv_w, b_conv_b, b_dt_bias, b_a_log, b_d_skip, b_g_norm, b_w_out, c_w_in, c_w_out, c_pe_k, c_w1_k, c_w2_k, c_pe_v, c_w1_v, c_w2_v):
    xp, xs = x_prompt, x_sample
    w_buf = cache_c_win_k.shape[2]
    ak_p, av_p, ai_p, bs_p, bc_p = [], [], [], [], []
    cck_p, ccv_p, csk_p, csv_p, cwk_p, cwv_p = [], [], [], [], [], []
    mk_p, mv_p = [], []
    ak_s, av_s, ai_s, bs_s, bc_s = [], [], [], [], []
    cck_s, ccv_s, csk_s, csv_s, cwk_s, cwv_s = [], [], [], [], [], []
    ia = ib = ic = 0
    for i in range(DEPTH):
        xp = ffn_residual(xp, g_ffn1[i], ffn1_wi[i], ffn1_wo[i])
        xs = ffn_residual(xs, g_ffn1[i], ffn1_wi[i], ffn1_wo[i])
        kind = i % N_MIXERS
        if kind == 0:
            op, (k1, v1, i1) = dsa_prompt(xp, g_mix[i], a_w_in[ia], a_w_out[ia])
            os_, (k2, v2, i2) = dsa_sample(xs, g_mix[i], a_w_in[ia], a_w_out[ia], cache_a_k, cache_a_v, cache_a_idx,
                                           ia, page_table)
            ak_p.append(k1); av_p.append(v1); ai_p.append(i1)
            ak_s.append(k2); av_s.append(v2); ai_s.append(i2)
            ia += 1
        elif kind == 1:
            conv0 = jnp.zeros((xp.shape[0], B_CONV - 1, B_CONV_DIM), xp.dtype)
            ssm0 = jnp.zeros((xp.shape[0], B_HEADS, B_HEADDIM, B_STATE), F32)
            wts = (b_w_in[ib], b_conv_w[ib], b_conv_b[ib], b_dt_bias[ib], b_a_log[ib], b_d_skip[ib],
                   b_g_norm[ib], b_w_out[ib])
            op, c1, s1 = mamba_prompt(xp, g_mix[i], *wts, conv0, ssm0)
            os_, c2, s2 = mamba_mixer(xs, g_mix[i], *wts, state_b_conv[ib], state_b_ssm[ib])
            bc_p.append(c1); bs_p.append(s1); bc_s.append(c2); bs_s.append(s2)
            ib += 1
        else:
            cmp_k = (c_pe_k[ic], c_w1_k[ic], c_w2_k[ic])
            cmp_v = (c_pe_v[ic], c_w1_v[ic], c_w2_v[ic])
            op, st1 = nsa_prompt(xp, g_mix[i], c_w_in[ic], c_w_out[ic], cmp_k, cmp_v, w_buf)
            os_, st2 = nsa_sample(xs, g_mix[i], c_w_in[ic], c_w_out[ic], cmp_k, cmp_v, cache_c_cmp_k, cache_c_cmp_v,
                                  cache_c_slc_k, cache_c_slc_v, cache_c_win_k, cache_c_win_v, ic, page_table)
            for lst, arr in zip((cck_p, ccv_p, csk_p, csv_p, cwk_p, cwv_p), st1):
                lst.append(arr)
            for lst, arr in zip((cck_s, ccv_s, csk_s, csv_s, cwk_s, cwv_s), st2):
                lst.append(arr)
            ic += 1
        xp, xs = op, os_
        mk, mv = mem_kv(mem_prompt, g_mem[i], x_w_kv[i])
        mk_p.append(mk); mv_p.append(mv)
        xp = cross_attn(xp, g_xattn[i], mk, mv, x_w_q[i], x_w_o[i])
        xs = cross_attn(xs, g_xattn[i], cache_mem_k[i], cache_mem_v[i], x_w_q[i], x_w_o[i])
        xp = ffn_residual(xp, g_ffn2[i], ffn2_wi[i], ffn2_wo[i])
        xs = ffn_residual(xs, g_ffn2[i], ffn2_wi[i], ffn2_wo[i])
    y_prompt = rms_norm_rows(xp, g_final)
    y_sample = rms_norm_rows(xs, g_final)
    st = jnp.stack
    return (y_prompt, y_sample,
            st(ak_p), st(av_p), st(ai_p), st(bs_p), st(bc_p),
            st(cck_p), st(ccv_p), st(csk_p), st(csv_p), st(cwk_p), st(cwv_p),
            st(mk_p), st(mv_p),
            st(ak_s), st(av_s), st(ai_s), st(bs_s), st(bc_s),
            st(cck_s), st(ccv_s), st(csk_s), st(csv_s), st(cwk_s), st(cwv_s))
```

```python
import functools

import jax
import jax.numpy as jnp
import numpy as np
from jax import lax
from jax.experimental import pallas as pl
from jax.experimental.pallas import tpu as pltpu

F32 = jnp.float32
BF16 = jnp.bfloat16
I32 = jnp.int32
INT_MIN = -2 ** 31
EPS = 1e-6

D_MODEL = 1024
DEPTH = 4
N_MIXERS = 3
HEAD_DIM = 64
ROPE_THETA = 10000.0
Q_BLOCK = 128
A_HEADS = D_MODEL // HEAD_DIM
A_KV_HEADS = 4
IDX_HEADS = 8
IDX_DIM = 64
TOPK_MAX = 256
B_D_INNER = 2 * D_MODEL
B_HEADDIM = 64
B_HEADS = B_D_INNER // B_HEADDIM
B_GROUPS = 4
B_STATE = 128
B_CONV = 4
B_CONV_DIM = B_D_INNER + 2 * B_GROUPS * B_STATE
B_CHUNK = 128
C_HEADS = D_MODEL // HEAD_DIM
C_KV_HEADS = 2
CMP_BLOCK = 32
CMP_STRIDE = 16
SLC_BLOCK = 64
N_SLC = 16
WINDOW = 512
X_HEADS = 4
X_HEAD_DIM = 128
D_FF = 2816

LANE = 128
SUBLANE = 8
VMEM_LIMIT = 48 * 1024 * 1024
MM_COLS = 512
FF_CHUNK = 256
ATTN_COLS = 1024
TQ = SUBLANE
PAGES_PER_STEP = 16


def _pick_tile(n, candidates):
    for c in candidates:
        if n % c == 0:
            return c
    return n


def _resident(shape):
    return pl.BlockSpec(shape, lambda *_: (0,) * len(shape), pipeline_mode=pl.Buffered(1))


def _rms(x, g):
    return x * lax.rsqrt(jnp.mean(x * x, axis=-1, keepdims=True) + EPS) * g


def _nmm_kernel(x_ref, g_ref, w_ref, *rest, norm, residual):
    o_ref = rest[-1]
    x = x_ref[...]
    if norm:
        x = _rms(x, g_ref[...])
    xb = x.astype(BF16)
    n = w_ref.shape[1]
    for c0 in range(0, n, MM_COLS):
        c1 = min(c0 + MM_COLS, n)
        y = jnp.dot(xb, w_ref[:, c0:c1], preferred_element_type=F32)
        o_ref[:, c0:c1] = rest[0][:, c0:c1] + y if residual else y


def norm_matmul(x, w, g=None, res=None):
    m, k = x.shape
    n = w.shape[1]
    assert res is None or (n % LANE == 0 and m % SUBLANE == 0)
    n_pad = -(-n // LANE) * LANE
    wb = w.astype(BF16)
    if n_pad != n:
        wb = jnp.pad(wb, ((0, 0), (0, n_pad - n)))
    m_orig = m
    if m % SUBLANE != 0:
        m = -(-m // 256) * 256
        x = jnp.pad(x, ((0, m - m_orig), (0, 0)))
    tm = _pick_tile(m, (256, 128, 64, 32, 16, 8))
    norm = g is not None
    gg = (g if norm else jnp.ones((k,), F32)).reshape(1, k).astype(F32)
    row_block = lambda width: pl.BlockSpec((tm, width), lambda i: (i, 0))
    extra = [] if res is None else [res]
    out = pl.pallas_call(
        functools.partial(_nmm_kernel, norm=norm, residual=res is not None),
        grid=(m // tm,),
        in_specs=[row_block(k), _resident((1, k)), _resident((k, n_pad))] + [row_block(n_pad) for _ in extra],
        out_specs=row_block(n_pad),
        out_shape=jax.ShapeDtypeStruct((m, n_pad), F32),
        compiler_params=pltpu.CompilerParams(dimension_semantics=("parallel",), vmem_limit_bytes=VMEM_LIMIT),
        name="norm_matmul",
    )(x, gg, wb, *extra)
    if n_pad != n or m != m_orig:
        out = out[:m_orig, :n]
    return out


def mm(x, w, g=None, res=None):
    lead = x.shape[:-1]
    r2 = None if res is None else res.reshape(-1, res.shape[-1])
    return norm_matmul(x.reshape(-1, x.shape[-1]), w, g, r2).reshape(lead + (w.shape[1],))


def _ffn_kernel(x_ref, g_ref, wg_ref, wu_ref, wo_ref, o_ref, act_ref, *, n_chunks):
    xn = _rms(x_ref[...], g_ref[...]).astype(BF16)

    def body(c, _):
        off = pl.multiple_of(c * FF_CHUNK, FF_CHUNK)
        hg = jnp.dot(xn, wg_ref[:, pl.ds(off, FF_CHUNK)], preferred_element_type=F32)
        hu = jnp.dot(xn, wu_ref[:, pl.ds(off, FF_CHUNK)], preferred_element_type=F32)
        act_ref[:, pl.ds(off, FF_CHUNK)] = (jax.nn.silu(hg) * hu).astype(BF16)
        return 0

    lax.fori_loop(0, n_chunks, body, 0)
    for c0 in range(0, o_ref.shape[1], MM_COLS):
        cs = slice(c0, c0 + MM_COLS)
        o_ref[:, cs] = x_ref[:, cs] + 0.5 * jnp.dot(act_ref[...], wo_ref[:, cs], preferred_element_type=F32)


def ffn_residual(x, g, wi, wo):
    lead, d = x.shape[:-1], x.shape[-1]
    x2 = x.reshape(-1, d)
    m = x2.shape[0]
    f = wo.shape[0]
    assert f % FF_CHUNK == 0 and d % MM_COLS == 0
    tm = _pick_tile(m, (512, 256, 128, 64, 32, 16, 8))
    wb = wi.astype(BF16)
    out = pl.pallas_call(
        functools.partial(_ffn_kernel, n_chunks=f // FF_CHUNK),
        grid=(m // tm,),
        in_specs=[pl.BlockSpec((tm, d), lambda i: (i, 0)), _resident((1, d)), _resident((d, f)), _resident((d, f)),
                  _resident((f, d))],
        out_specs=pl.BlockSpec((tm, d), lambda i: (i, 0)),
        out_shape=jax.ShapeDtypeStruct((m, d), F32),
        scratch_shapes=[pltpu.VMEM((tm, f), BF16)],
        compiler_params=pltpu.CompilerParams(dimension_semantics=("parallel",), vmem_limit_bytes=VMEM_LIMIT),
        name="ffn_residual",
    )(x2, g.reshape(1, d).astype(F32), wb[:, :f], wb[:, f:], wo.astype(BF16))
    return out.reshape(lead + (d,))


def _rmsnorm_kernel(x_ref, g_ref, o_ref):
    o_ref[...] = _rms(x_ref[...], g_ref[...])


def rms_norm_rows(x, g):
    lead, d = x.shape[:-1], x.shape[-1]
    x2 = x.reshape(-1, d)
    m = x2.shape[0]
    tm = _pick_tile(m, (512, 256, 128, 64, 32, 16, 8))
    row_block = pl.BlockSpec((tm, d), lambda i: (i, 0))
    out = pl.pallas_call(
        _rmsnorm_kernel, grid=(m // tm,), in_specs=[row_block, _resident((1, d))], out_specs=row_block,
        out_shape=jax.ShapeDtypeStruct((m, d), F32),
        compiler_params=pltpu.CompilerParams(dimension_semantics=("parallel",)), name="rms_norm_rows",
    )(x2, g.reshape(1, d).astype(F32))
    return out.reshape(lead + (d,))


def rope_tables(pos):
    half = HEAD_DIM // 2
    inv = ROPE_THETA ** (-jnp.arange(half, dtype=F32) / half)
    ang = pos.astype(F32)[:, None] * inv[None, :]
    cos, sin = jnp.cos(ang), jnp.sin(ang)
    reps = LANE // HEAD_DIM
    return (jnp.tile(jnp.concatenate([cos, cos], axis=1), (1, reps)),
            jnp.tile(jnp.concatenate([-sin, sin], axis=1), (1, reps)))


def _rope_cols(y, cos, sin):
    w = y.shape[1]
    half = HEAD_DIM // 2
    lane = lax.broadcasted_iota(I32, y.shape, 1)
    partner = jnp.where(lane % HEAD_DIM < half, pltpu.roll(y, w - half, axis=1), pltpu.roll(y, half, axis=1))
    reps = w // LANE
    return y * jnp.tile(cos, (1, reps)) + partner * jnp.tile(sin, (1, reps))


def _proj_kernel(x_ref, g_ref, cos_ref, sin_ref, w_ref, *out_refs, segments):
    xb = _rms(x_ref[...], g_ref[...]).astype(BF16)
    cos, sin = cos_ref[...], sin_ref[...]
    k = 0
    for start, width, rope, scale, outs in segments:
        y = jnp.dot(xb, w_ref[:, start:start + width], preferred_element_type=F32)
        if rope:
            y = _rope_cols(y, cos, sin)
        if scale is not None:
            y = y * scale
        for used, dtype, transposed in outs:
            v = y[:, :used]
            out_refs[k][...] = (v.T if transposed else v).astype(dtype)
            k += 1


def project(x, g, w_cols, segments, pos_rows):
    m, kdim = x.shape
    packed, segs, out_shapes, out_specs = [], [], [], []
    tm = _pick_tile(m, (256, 128, 64, 32, 16, 8))
    start = 0
    for wc, (rope, scale, outs) in zip(w_cols, segments):
        n = wc.shape[1]
        n_pad = -(-n // LANE) * LANE
        packed.append(jnp.pad(wc.astype(BF16), ((0, 0), (0, n_pad - n))))
        segs.append((start, n_pad, rope, scale, tuple((n, dt, tr) for dt, tr in outs)))
        for dt, tr in outs:
            if tr:
                out_shapes.append(jax.ShapeDtypeStruct((n, m), dt))
                out_specs.append(pl.BlockSpec((n, tm), lambda i: (0, i)))
            else:
                out_shapes.append(jax.ShapeDtypeStruct((m, n), dt))
                out_specs.append(pl.BlockSpec((tm, n), lambda i: (i, 0)))
        start += n_pad
    wb = jnp.concatenate(packed, axis=1)
    cos, sin = rope_tables(pos_rows)
    row_block = lambda width: pl.BlockSpec((tm, width), lambda i: (i, 0))
    return pl.pallas_call(
        functools.partial(_proj_kernel, segments=tuple(segs)),
        grid=(m // tm,),
        in_specs=[row_block(kdim), _resident((1, kdim)), row_block(LANE), row_block(LANE), _resident(wb.shape)],
        out_specs=out_specs,
        out_shape=out_shapes,
        compiler_params=pltpu.CompilerParams(dimension_semantics=("parallel",), vmem_limit_bytes=VMEM_LIMIT),
        name="project",
    )(x, g.reshape(1, kdim).astype(F32), cos, sin, wb)


def _split_cols(w, widths):
    cols = np.cumsum((0,) + tuple(widths))
    return [w[:, cols[j]:cols[j + 1]] for j in range(len(widths))]


def _softmax_rows(s):
    m = jnp.max(s, axis=-1, keepdims=True)
    e = jnp.exp(s - jnp.where(m == -jnp.inf, 0.0, m))
    den = jnp.sum(e, axis=-1, keepdims=True)
    return e / jnp.where(den > 0, den, 1.0)


def _flash_step(carry, q, kt_tile, v_tile, sel, rep):
    m, l, acc = carry
    rows, tk = q.shape[0], kt_tile.shape[1]
    s = jnp.dot(q, kt_tile, preferred_element_type=F32)
    s = jnp.where(sel, s.reshape(rep, Q_BLOCK, tk), -jnp.inf).reshape(rows, tk)
    m_new = jnp.maximum(m, jnp.max(s, axis=1, keepdims=True))
    m_safe = jnp.where(m_new == -jnp.inf, 0.0, m_new)
    alpha = jnp.exp(m - m_safe)
    p = jnp.exp(s - m_safe)
    l = l * alpha + jnp.sum(p, axis=1, keepdims=True)
    pv = jnp.dot(p.astype(BF16), v_tile, preferred_element_type=F32)
    return m_new, l, acc * alpha + pv


def _flash_init(rows, hd):
    return (jnp.full((rows, 1), -jnp.inf, F32), jnp.zeros((rows, 1), F32), jnp.zeros((rows, hd), F32))


def _flash_finish(carry):
    _, l, acc = carry
    return acc / jnp.where(l > 0, l, 1.0)


def _stack_heads(x, h0, rep, hd):
    return jnp.concatenate([x[:, (h0 + r) * hd:(h0 + r + 1) * hd] for r in range(rep)], axis=0)


def _sortable_key(score):
    bits = pltpu.bitcast(score, I32)
    key = jnp.where(bits < 0, bits ^ jnp.int32(0x7FFFFFFF), bits)
    return jnp.where(score == 0.0, 0, key)


def _count_lanes(pred_fn, key_ref, nkt, tk):
    rows = key_ref.shape[0]

    def body(kt, cnt):
        keys = key_ref[:, pl.ds(pl.multiple_of(kt * tk, tk), tk)]
        for c in range(tk // LANE):
            hit = pred_fn(keys[:, c * LANE:(c + 1) * LANE], kt * tk + c * LANE)
            cnt = cnt + jnp.where(hit, 1, 0)
        return cnt

    cnt = lax.fori_loop(0, nkt, body, jnp.zeros((rows, LANE), I32))
    return jnp.sum(cnt, axis=1, keepdims=True)


def _top_k_mask_params(key_ref, nkt, tk, topk, n_valid):
    rows = key_ref.shape[0]
    lane_iota = lax.broadcasted_iota(I32, (rows, LANE), 1)

    def bit_body(it, thr):
        cand = thr + jnp.left_shift(jnp.int32(1), 31 - it)
        cnt = _count_lanes(lambda keys, c0: keys >= cand, key_ref, nkt, tk)
        return jnp.where(cnt >= topk, cand, thr)

    thr = lax.fori_loop(0, 32, bit_body, jnp.full((rows, 1), INT_MIN, I32))
    many = n_valid > topk
    thr = jnp.where(many, thr, INT_MIN + 1)
    c_gt = _count_lanes(lambda keys, c0: keys > thr, key_ref, nkt, tk)
    c_eq = _count_lanes(lambda keys, c0: keys == thr, key_ref, nkt, tk)
    need = topk - c_gt
    excess = many & (c_eq > need)
    n_col_bits = max(1, (key_ref.shape[1] - 1).bit_length())

    def tie_search():
        def jbody(it, last):
            cand = last + jnp.left_shift(jnp.int32(1), n_col_bits - 1 - it)
            cnt = _count_lanes(lambda keys, c0: (keys == thr) & (c0 + lane_iota < cand), key_ref, nkt, tk)
            return jnp.where(cnt <= need - 1, cand, last)

        return lax.fori_loop(0, n_col_bits, jbody, jnp.zeros((rows, 1), I32))

    any_excess = jnp.max(jnp.where(excess, 1, 0)) > 0
    last_tie = lax.cond(any_excess, tie_search, lambda: jnp.zeros((rows, 1), I32))
    return thr, jnp.where(excess, last_tie, jnp.int32(2 ** 30))


def _block_cover(n_cmp_rows):
    c0 = lax.broadcasted_iota(I32, (n_cmp_rows, LANE), 0) * CMP_STRIDE
    s0 = lax.broadcasted_iota(I32, (n_cmp_rows, LANE), 1) * SLC_BLOCK
    return jnp.where(c0 < s0 + SLC_BLOCK, jnp.where(c0 + CMP_BLOCK > s0, 1.0, 0.0), 0.0).astype(BF16)


def _block_importance(p_sum, cover, row_pos):
    p_hi = p_sum.astype(BF16)
    p_lo = (p_sum - p_hi.astype(F32)).astype(BF16)
    imp = jnp.dot(p_hi, cover, preferred_element_type=F32) + jnp.dot(p_lo, cover, preferred_element_type=F32)
    jb = lax.broadcasted_iota(I32, imp.shape, 1)
    cur = row_pos // SLC_BLOCK
    forced = (jb == 0) | (jb == cur) | (jb == cur - 1)
    imp = jnp.where(forced, jnp.inf, imp)
    return jnp.where(jb <= cur, imp, -jnp.inf)


def _dsa_prompt_kernel(iq_ref, iw_ref, ikt_ref, q_ref, kt_ref, v_ref, o_ref, key_ref, *, topk, tk,
                       n_idx_heads, n_groups, rep):
    i = pl.program_id(1)
    t0 = i * Q_BLOCK
    nkt = (t0 + Q_BLOCK + tk - 1) // tk
    row = t0 + lax.broadcasted_iota(I32, (Q_BLOCK, 1), 0)
    hd = kt_ref.shape[0] // n_groups
    di = ikt_ref.shape[0]

    iw = iw_ref[0]
    iq = iq_ref[0]
    iq_heads = [iq[:, h * di:(h + 1) * di] for h in range(n_idx_heads)]

    def score_body(kt, _):
        off = pl.multiple_of(kt * tk, tk)
        acc = jnp.zeros((Q_BLOCK, tk), F32)
        for h in range(n_idx_heads):
            r = jnp.dot(iq_heads[h], ikt_ref[:, pl.ds(off, tk)], preferred_element_type=F32)
            acc = acc + jnp.maximum(r, 0.0) * iw[:, h:h + 1]
        col = off + lax.broadcasted_iota(I32, (Q_BLOCK, tk), 1)
        key_ref[:, pl.ds(off, tk)] = jnp.where(col <= row, _sortable_key(acc), INT_MIN)
        return 0

    lax.fori_loop(0, nkt, score_body, 0)

    thr, last_tie = _top_k_mask_params(key_ref, nkt, tk, topk, row + 1)

    q = q_ref[0]
    q_groups = [_stack_heads(q, g * rep, rep, hd) for g in range(n_groups)]
    rows_g = rep * Q_BLOCK

    def attn_body(kt, carry):
        off = pl.multiple_of(kt * tk, tk)
        keys = key_ref[:, pl.ds(off, tk)]
        col = off + lax.broadcasted_iota(I32, (Q_BLOCK, tk), 1)
        sel = ((keys > thr) | ((keys == thr) & (col <= last_tie)))[None]
        v_t = v_ref[0, pl.ds(off, tk), :]
        return tuple(
            _flash_step(carry[g], q_groups[g], kt_ref[g * hd:(g + 1) * hd, pl.ds(off, tk)],
                        v_t[:, g * hd:(g + 1) * hd], sel, rep)
            for g in range(n_groups))

    fin = lax.fori_loop(0, nkt, attn_body, tuple(_flash_init(rows_g, hd) for _ in range(n_groups)))
    for g in range(n_groups):
        o = _flash_finish(fin[g])
        for r in range(rep):
            h = g * rep + r
            o_ref[0, :, h * hd:(h + 1) * hd] = o[r * Q_BLOCK:(r + 1) * Q_BLOCK]


def dsa_prompt_attention(q, kt, v, iq, ikt, iw, topk, n_groups, n_idx_heads, tk=ATTN_COLS):
    b, s, hd_all = q.shape
    gd, di = kt.shape[0], ikt.shape[0]
    d = gd // n_groups
    tk = min(tk, s)
    kern = functools.partial(_dsa_prompt_kernel, topk=topk, tk=tk, n_idx_heads=n_idx_heads, n_groups=n_groups,
                             rep=hd_all // gd)
    q_block = lambda width: pl.BlockSpec((1, Q_BLOCK, width), lambda bb, i: (bb, i, 0))
    return pl.pallas_call(
        kern,
        grid=(b, s // Q_BLOCK),
        in_specs=[
            q_block(n_idx_heads * di), q_block(n_idx_heads),
            pl.BlockSpec((di, s), lambda bb, i: (0, bb)),
            q_block(hd_all),
            pl.BlockSpec((gd, s), lambda bb, i: (0, bb)),
            pl.BlockSpec((1, s, gd), lambda bb, i: (bb, 0, 0)),
        ],
        out_specs=q_block(hd_all),
        out_shape=jax.ShapeDtypeStruct((b, s, hd_all), F32),
        scratch_shapes=[pltpu.VMEM((Q_BLOCK, s), I32)],
        compiler_params=pltpu.CompilerParams(
            dimension_semantics=("parallel", "arbitrary"), vmem_limit_bytes=VMEM_LIMIT),
        name="dsa_prompt_attention",
    )(iq, iw, ikt, q, kt, v)


def _top_blocks(imp_t, n_sel):
    nb = imp_t.shape[0]
    j_iota = lax.broadcasted_iota(I32, imp_t.shape, 0)
    rank = jnp.zeros(imp_t.shape, F32)
    for k in range(nb):
        row_k = imp_t[k:k + 1, :]
        earlier = jnp.where(j_iota > k, 1.0, 0.0)
        rank = rank + jnp.where(row_k > imp_t, 1.0, jnp.where(row_k == imp_t, earlier, 0.0))
    return jnp.where(rank < n_sel, 1.0, 0.0)


def _nsa_prompt_kernel(q_ref, gate_ref, kct_ref, vc_ref, kst_ref, vs_ref, kwt_ref, vw_ref, o_ref, *,
                       tk, n_cmp, n_blk, n_groups, rep, win_len):
    i = pl.program_id(1)
    t0 = i * Q_BLOCK
    nkt = (t0 + Q_BLOCK + tk - 1) // tk
    hd = kst_ref.shape[0] // n_groups
    rows = rep * Q_BLOCK
    nc_pad = kct_ref.shape[-1]
    row = t0 + lax.broadcasted_iota(I32, (Q_BLOCK, 1), 0)
    gates = jax.nn.sigmoid(gate_ref[0])
    n_heads = n_groups * rep
    q_all = q_ref[0]

    cover = _block_cover(nc_pad)
    c_idx = lax.broadcasted_iota(I32, (Q_BLOCK, nc_pad), 1)
    c_ok = ((c_idx < n_cmp) & (c_idx * CMP_STRIDE + (CMP_BLOCK - 1) <= row))[None]
    w_start = pl.multiple_of(jnp.maximum(t0 - WINDOW, 0), Q_BLOCK)
    w_col = w_start + lax.broadcasted_iota(I32, (Q_BLOCK, win_len), 1)
    w_d = row - w_col
    w_ok = ((w_d >= 0) & (w_d < WINDOW))[None]

    for g in range(n_groups):
        q = _stack_heads(q_all, g * rep, rep, hd)
        lo, hi = g * hd, (g + 1) * hd

        s = jnp.dot(q, kct_ref[0, lo:hi, :], preferred_element_type=F32)
        s = jnp.where(c_ok, s.reshape(rep, Q_BLOCK, nc_pad), -jnp.inf)
        p = _softmax_rows(s)
        o_cmp = jnp.dot(p.reshape(rows, nc_pad).astype(BF16), vc_ref[0, :, lo:hi], preferred_element_type=F32)

        imp = _block_importance(jnp.sum(p, axis=0), cover, row)
        sel_blocks = _top_blocks(imp.T[:n_blk], min(N_SLC, n_blk))
        if n_blk < LANE:
            sel_blocks = jnp.concatenate([sel_blocks, jnp.zeros((LANE - n_blk, Q_BLOCK), F32)], axis=0)
        sel_blocks = sel_blocks.T.astype(BF16)

        def slc_body(kt, carry):
            off = pl.multiple_of(kt * tk, tk)
            col = off + lax.broadcasted_iota(I32, (Q_BLOCK, tk), 1)
            blk_of_col = (off + lax.broadcasted_iota(I32, (LANE, tk), 1)) // SLC_BLOCK
            expand = jnp.where(lax.broadcasted_iota(I32, (LANE, tk), 0) == blk_of_col, 1.0, 0.0).astype(BF16)
            hit = jnp.dot(sel_blocks, expand, preferred_element_type=F32)
            sel = (jnp.where(col <= row, hit, 0.0) > 0.5)[None]
            return _flash_step(carry, q, kst_ref[lo:hi, pl.ds(off, tk)], vs_ref[0, pl.ds(off, tk), lo:hi], sel, rep)

        o_slc = _flash_finish(lax.fori_loop(0, nkt, slc_body, _flash_init(rows, hd)))

        s = jnp.dot(q, kwt_ref[lo:hi, pl.ds(w_start, win_len)], preferred_element_type=F32)
        s = jnp.where(w_ok, s.reshape(rep, Q_BLOCK, win_len), -jnp.inf)
        p = _softmax_rows(s).reshape(rows, win_len)
        o_win = jnp.dot(p.astype(BF16), vw_ref[0, pl.ds(w_start, win_len), lo:hi], preferred_element_type=F32)

        for r in range(rep):
            h = g * rep + r
            rs = slice(r * Q_BLOCK, (r + 1) * Q_BLOCK)
            o_ref[0, :, h * hd:(h + 1) * hd] = (
                gates[:, h:h + 1] * o_cmp[rs] + gates[:, n_heads + h:n_heads + h + 1] * o_slc[rs]
                + gates[:, 2 * n_heads + h:2 * n_heads + h + 1] * o_win[rs])


def nsa_prompt_attention(q, gate_logits, kct, vc, n_cmp, kst, vs, kwt, vw, n_groups, tk=ATTN_COLS):
    b, s, hd_all = q.shape
    gd, nc_pad = kct.shape[1], kct.shape[2]
    n_blk = -(-s // SLC_BLOCK)
    assert n_blk <= LANE and s % Q_BLOCK == 0
    tk = min(tk, s)
    win_len = min(WINDOW + Q_BLOCK, s)
    kern = functools.partial(_nsa_prompt_kernel, tk=tk, n_cmp=n_cmp, n_blk=n_blk, n_groups=n_groups,
                             rep=hd_all // gd, win_len=win_len)
    per_b = lambda *blk: pl.BlockSpec((1,) + blk, lambda bb, i: (bb,) + (0,) * len(blk))
    keys_t = pl.BlockSpec((gd, s), lambda bb, i: (0, bb))
    q_block = lambda width: pl.BlockSpec((1, Q_BLOCK, width), lambda bb, i: (bb, i, 0))
    return pl.pallas_call(
        kern,
        grid=(b, s // Q_BLOCK),
        in_specs=[q_block(hd_all), q_block(gate_logits.shape[-1]), per_b(gd, nc_pad), per_b(nc_pad, gd),
                  keys_t, per_b(s, gd), keys_t, per_b(s, gd)],
        out_specs=q_block(hd_all),
        out_shape=jax.ShapeDtypeStruct((b, s, hd_all), F32),
        compiler_params=pltpu.CompilerParams(
            dimension_semantics=("parallel", "arbitrary"), vmem_limit_bytes=VMEM_LIMIT),
        name="nsa_prompt_attention",
    )(q, gate_logits, kct, vc, kst, vs, kwt, vw)


def _compress_finish(a, pew_ref, w2_ref, n_groups, hd):
    half = n_groups * hd
    first, second = a[:, :half], a[:, half:]
    second = jnp.concatenate([second[1:], jnp.zeros((1, half), F32)], axis=0)
    out = []
    for g in range(n_groups):
        hcol = first[:, g * hd:(g + 1) * hd] + second[:, g * hd:(g + 1) * hd] + pew_ref[...]
        out.append(jnp.dot(jax.nn.silu(hcol).astype(BF16), w2_ref[...], preferred_element_type=F32))
    return out


def _compress_kernel(c_ref, wc_ref, pew_ref, w2_ref, o_ref, *, n_groups, hd, transposed):
    a = jnp.dot(c_ref[0].astype(BF16), wc_ref[...], preferred_element_type=F32)
    res = jnp.concatenate(_compress_finish(a, pew_ref, w2_ref, n_groups, hd), axis=1)
    o_ref[0] = (res.T if transposed else res).astype(BF16)


def _compress_chunk_weights(w1, n_groups, hd):
    wr = w1.reshape(2, CMP_STRIDE, hd, hd)
    eye = jnp.eye(n_groups, dtype=w1.dtype)
    wc = jnp.einsum('hrdo,gk->rgdhko', wr, eye)
    return wc.reshape(CMP_STRIDE * n_groups * hd, 2 * n_groups * hd)


def _compress_weights(pe, w1, w2, n_groups, hd):
    return _compress_chunk_weights(w1, n_groups, hd).astype(BF16), mm(pe.reshape(1, -1), w1), w2.astype(BF16)


def compress_rows(rows, cmp_w, n_groups, transposed):
    b, l, gd = rows.shape
    hd = gd // n_groups
    n_chunks = l // CMP_STRIDE
    cw = CMP_STRIDE * gd
    wc, pew, w2 = _compress_weights(*cmp_w, n_groups, hd)
    out_blk = (gd, n_chunks) if transposed else (n_chunks, gd)
    return pl.pallas_call(
        functools.partial(_compress_kernel, n_groups=n_groups, hd=hd, transposed=transposed),
        grid=(b,),
        in_specs=[pl.BlockSpec((1, n_chunks, cw), lambda bb: (bb, 0, 0)), _resident(wc.shape), _resident(pew.shape),
                  _resident(w2.shape)],
        out_specs=pl.BlockSpec((1,) + out_blk, lambda bb: (bb, 0, 0)),
        out_shape=jax.ShapeDtypeStruct((b,) + out_blk, BF16),
        compiler_params=pltpu.CompilerParams(dimension_semantics=("parallel",), vmem_limit_bytes=VMEM_LIMIT),
        name="compress_rows",
    )(rows.reshape(b, n_chunks, cw), wc, pew, w2)


def _attend_cols(q, kt, vt, sel, rep):
    n = kt.shape[1]
    s = jnp.dot(q, kt, preferred_element_type=F32)
    s = jnp.where(sel, s.reshape(rep, TQ, n), -jnp.inf)
    p = _softmax_rows(s).reshape(rep * TQ, n)
    return p, lax.dot_general(p.astype(BF16), vt, (((1,), (1,)), ((), ())), preferred_element_type=F32)


def _head_rows(x):
    b, t, h, d = x.shape
    x = jnp.pad(x.astype(BF16), ((0, 0), (0, TQ - t), (0, 0), (0, 0)))
    return x.transpose(0, 2, 1, 3).reshape(b, h * TQ, d)


def _new_cols(x, width):
    b, t = x.shape[:2]
    return jnp.pad(x.reshape(b, t, -1).astype(BF16).transpose(0, 2, 1), ((0, 0), (0, 0), (0, width - t)))


def _cols_view(pool):
    lead, rows = pool.shape[:2], pool.shape[2]
    perm = (0, 1) + tuple(range(3, pool.ndim)) + (2,)
    return pool.transpose(perm).reshape(lead + (-1, rows))


def _page_specs(block, layer, n_pages, pp):
    def spec(j):
        return pl.BlockSpec((1, 1) + block,
                            lambda bb, p, pt: (layer, pt[bb * n_pages + p * pp + j]) + (0,) * len(block))
    return [spec(j) for j in range(pp)]


def _per_batch(*blk):
    return pl.BlockSpec((1,) + blk, lambda bb, p, pt: (bb,) + (0,) * len(blk))


def _whole(*blk):
    return pl.BlockSpec(blk, lambda bb, p, pt: (0,) * len(blk))


def _dsa_sample_kernel(pt_ref, iq_ref, iw_ref, q_ref, ikn_ref, kn_ref, vn_ref, *rest, topk, past, n_new,
                       n_idx_heads, n_groups, rep, pp):
    idx_pages = rest[:pp]
    k_pages = rest[pp:2 * pp]
    v_pages = rest[2 * pp:3 * pp]
    o_ref, key_ref, kst, vst = rest[3 * pp:]
    p_step = pl.program_id(1)
    hd = q_ref.shape[-1]
    page = idx_pages[0].shape[-1]
    n_tiles = past // page + 1
    iw = iw_ref[0]
    tpos = past + jnp.minimum(lax.broadcasted_iota(I32, (TQ, 1), 0), n_new - 1)

    def scores(ikt):
        r = jnp.dot(iq_ref[0], ikt, preferred_element_type=F32)
        r = jnp.maximum(r, 0.0).reshape(n_idx_heads, TQ, ikt.shape[1]) * iw
        return jnp.sum(r, axis=0)

    for j in range(pp):
        off = pl.multiple_of((p_step * pp + j) * page, page)
        key_ref[:, pl.ds(off, page)] = _sortable_key(scores(idx_pages[j][0, 0].astype(BF16)))
        kst[:, pl.ds(off, page)] = k_pages[j][0, 0].astype(BF16)
        vst[:, pl.ds(off, page)] = v_pages[j][0, 0].astype(BF16)

    @pl.when(p_step == pl.num_programs(1) - 1)
    def _():
        c = lax.broadcasted_iota(I32, (TQ, page), 1)
        fresh_ok = (c < n_new) & (past + c <= tpos)
        key_ref[:, past:past + page] = jnp.where(fresh_ok, _sortable_key(scores(ikn_ref[0])), INT_MIN)
        kst[:, past:past + page] = kn_ref[0]
        vst[:, past:past + page] = vn_ref[0]
        thr, last_tie = _top_k_mask_params(key_ref, n_tiles, page, topk, tpos + 1)
        keys = key_ref[...]
        col = lax.broadcasted_iota(I32, keys.shape, 1)
        sel = ((keys > thr) | ((keys == thr) & (col <= last_tie)))[None]
        for g in range(n_groups):
            gs = slice(g * hd, (g + 1) * hd)
            _, o = _attend_cols(q_ref[0, g * rep * TQ:(g + 1) * rep * TQ], kst[gs, :], vst[gs, :], sel, rep)
            for r in range(rep):
                h = g * rep + r
                o_ref[0, :, h * hd:(h + 1) * hd] = o[r * TQ:(r + 1) * TQ]


def dsa_sample_attention(q, k, v, iq, ik, iw, pools_k, pools_v, pools_idx, layer, page_table, topk):
    b, t, h, d = q.shape
    g = k.shape[2]
    hi, di = iq.shape[2], iq.shape[3]
    n_pages = page_table.shape[1]
    page = pools_k.shape[2]
    past = n_pages * page
    pp = min(PAGES_PER_STEP, n_pages)
    assert n_pages % pp == 0 and t <= TQ
    iw_h = jnp.pad(iw.astype(F32), ((0, 0), (0, TQ - t), (0, 0))).transpose(0, 2, 1)[..., None]
    kern = functools.partial(_dsa_sample_kernel, topk=topk, past=past, n_new=t, n_idx_heads=hi, n_groups=g,
                             rep=h // g, pp=pp)
    o = pl.pallas_call(
        kern,
        grid_spec=pltpu.PrefetchScalarGridSpec(
            num_scalar_prefetch=1,
            grid=(b, n_pages // pp),
            in_specs=[_per_batch(hi * TQ, di), _per_batch(hi, TQ, 1), _per_batch(h * TQ, d), _per_batch(di, page),
                      _per_batch(g * d, page), _per_batch(g * d, page)]
                     + _page_specs((di, page), layer, n_pages, pp)
                     + _page_specs((g * d, page), layer, n_pages, pp)
                     + _page_specs((g * d, page), layer, n_pages, pp),
            out_specs=_per_batch(TQ, h * d),
            scratch_shapes=[pltpu.VMEM((TQ, past + page), I32), pltpu.VMEM((g * d, past + page), BF16),
                            pltpu.VMEM((g * d, past + page), BF16)],
        ),
        out_shape=jax.ShapeDtypeStruct((b, TQ, h * d), F32),
        compiler_params=pltpu.CompilerParams(
            dimension_semantics=("parallel", "arbitrary"), vmem_limit_bytes=VMEM_LIMIT),
        name="dsa_sample_attention",
    )(page_table.reshape(-1).astype(I32), _head_rows(iq), iw_h, _head_rows(q), _new_cols(ik, page),
      _new_cols(k, page), _new_cols(v, page),
      *([_cols_view(pools_idx)] * pp), *([_cols_view(pools_k)] * pp), *([_cols_view(pools_v)] * pp))
    return o[:, :t]


def _top_blocks_rows(imp, n_blk, n_sel):
    j_iota = lax.broadcasted_iota(I32, imp.shape, 1)
    rank = jnp.zeros(imp.shape, F32)
    for k in range(n_blk):
        col_k = imp[:, k:k + 1]
        earlier = jnp.where(j_iota > k, 1.0, 0.0)
        rank = rank + jnp.where(col_k > imp, 1.0, jnp.where(col_k == imp, earlier, 0.0))
    return jnp.where((rank < n_sel) & (j_iota < n_blk), 1.0, 0.0)


def _nsa_sample_kernel(pt_ref, q_ref, gate_ref, ksn_ref, vsn_ref, kwn_ref, vwn_ref, wink_ref, winv_ref,
                       wck_ref, wcv_ref, pewk_ref, pewv_ref, w2k_ref, w2v_ref, *rest, past, n_new, n_groups, rep, pp):
    kc_pages = rest[:pp]
    vc_pages = rest[pp:2 * pp]
    ks_pages = rest[2 * pp:3 * pp]
    vs_pages = rest[3 * pp:4 * pp]
    o_ref, kcs, vcs, kst, vst, kwt, vwt = rest[4 * pp:]
    p_step = pl.program_id(1)
    hd = q_ref.shape[-1]
    gd = n_groups * hd
    page = ks_pages[0].shape[-1]
    n_chunks = kcs.shape[0] // CMP_STRIDE
    n_cmp = n_chunks - CMP_BLOCK // CMP_STRIDE + 1
    n_blk = -(-(past + n_new) // SLC_BLOCK)
    w_buf = wink_ref.shape[-1]
    n_heads = n_groups * rep
    rows = rep * TQ

    for j in range(pp):
        off = pl.multiple_of((p_step * pp + j) * page, page)
        kcs[pl.ds(off, page), :] = kc_pages[j][0, 0].T
        vcs[pl.ds(off, page), :] = vc_pages[j][0, 0].T
        kst[:, pl.ds(off, page)] = ks_pages[j][0, 0].astype(BF16)
        vst[:, pl.ds(off, page)] = vs_pages[j][0, 0].astype(BF16)

    @pl.when(p_step == pl.num_programs(1) - 1)
    def _():
        kst[:, past:past + page] = ksn_ref[0]
        vst[:, past:past + page] = vsn_ref[0]
        kwt[:, 0:w_buf] = wink_ref[0, 0].astype(BF16)
        vwt[:, 0:w_buf] = winv_ref[0, 0].astype(BF16)
        kwt[:, w_buf:w_buf + LANE] = kwn_ref[0]
        vwt[:, w_buf:w_buf + LANE] = vwn_ref[0]

        tpos = past + jnp.minimum(lax.broadcasted_iota(I32, (TQ, 1), 0), n_new - 1)
        gates = jax.nn.sigmoid(gate_ref[0])

        def compress(rows_ref, wc_ref, pew_ref, w2_ref):
            a = jnp.zeros((n_chunks, 2 * gd), F32)
            for r in range(CMP_STRIDE):
                xr = rows_ref[pl.ds(r, n_chunks, stride=CMP_STRIDE), :].astype(BF16)
                a = a + jnp.dot(xr, wc_ref[r * gd:(r + 1) * gd, :], preferred_element_type=F32)
            out = _compress_finish(a, pew_ref, w2_ref, n_groups, hd)
            return [o.astype(BF16) for o in out]

        kcmp = compress(kcs, wck_ref, pewk_ref, w2k_ref)
        vcmp = compress(vcs, wcv_ref, pewv_ref, w2v_ref)

        c_idx = lax.broadcasted_iota(I32, (TQ, n_chunks), 1)
        c_ok = ((c_idx < n_cmp) & (c_idx * CMP_STRIDE + (CMP_BLOCK - 1) <= tpos))[None]
        cover = _block_cover(n_chunks)
        n_slc = kst.shape[1]
        col = lax.broadcasted_iota(I32, (TQ, n_slc), 1)
        blk_of_col = lax.broadcasted_iota(I32, (LANE, n_slc), 1) // SLC_BLOCK
        expand = jnp.where(lax.broadcasted_iota(I32, (LANE, n_slc), 0) == blk_of_col, 1.0, 0.0).astype(BF16)
        n_win = kwt.shape[1]
        w_c = lax.broadcasted_iota(I32, (TQ, n_win), 1)
        w_pos = past - w_buf + w_c
        w_d = tpos - w_pos
        w_ok = ((w_c < w_buf + n_new) & (w_pos >= 0) & (w_d >= 0) & (w_d < WINDOW))[None]

        for g in range(n_groups):
            q = q_ref[0, g * rows:(g + 1) * rows]
            gs = slice(g * hd, (g + 1) * hd)
            s = lax.dot_general(q, kcmp[g], (((1,), (1,)), ((), ())), preferred_element_type=F32)
            p_cmp = _softmax_rows(jnp.where(c_ok, s.reshape(rep, TQ, n_chunks), -jnp.inf))
            o_cmp = jnp.dot(p_cmp.reshape(rows, n_chunks).astype(BF16), vcmp[g], preferred_element_type=F32)
            imp = _block_importance(jnp.sum(p_cmp, axis=0), cover, tpos)
            sel_blocks = _top_blocks_rows(imp, n_blk, min(N_SLC, n_blk)).astype(BF16)
            hit = jnp.dot(sel_blocks, expand, preferred_element_type=F32)
            sel = (jnp.where(col <= tpos, hit, 0.0) > 0.5)[None]
            _, o_slc = _attend_cols(q, kst[gs, :], vst[gs, :], sel, rep)
            _, o_win = _attend_cols(q, kwt[gs, :], vwt[gs, :], w_ok, rep)
            for r in range(rep):
                h = g * rep + r
                rs = slice(r * TQ, (r + 1) * TQ)
                o_ref[0, :, h * hd:(h + 1) * hd] = (
                    gates[:, h:h + 1] * o_cmp[rs] + gates[:, n_heads + h:n_heads + h + 1] * o_slc[rs]
                    + gates[:, 2 * n_heads + h:2 * n_heads + h + 1] * o_win[rs])


def nsa_sample_attention(q, gate_logits, ks, vs, kw, vw, cmp_k, cmp_v, pools_kc, pools_vc, pools_ks, pools_vs,
                         wins_k, wins_v, layer, page_table):
    b, t, h, d = q.shape
    g = ks.shape[2]
    gd = g * d
    n_pages = page_table.shape[1]
    page = pools_ks.shape[2]
    past = n_pages * page
    pp = min(PAGES_PER_STEP, n_pages)
    assert n_pages % pp == 0 and t <= TQ and (past + t) // CMP_STRIDE * CMP_STRIDE == past
    cw = CMP_STRIDE * gd
    w_buf = wins_k.shape[2]

    wck, pewk, w2k = _compress_weights(*cmp_k, g, d)
    wcv, pewv, w2v = _compress_weights(*cmp_v, g, d)
    kern = functools.partial(_nsa_sample_kernel, past=past, n_new=t, n_groups=g, rep=h // g, pp=pp)
    win_spec = pl.BlockSpec((1, 1, gd, w_buf), lambda bb, p, pt: (layer, bb, 0, 0))
    o = pl.pallas_call(
        kern,
        grid_spec=pltpu.PrefetchScalarGridSpec(
            num_scalar_prefetch=1,
            grid=(b, n_pages // pp),
            in_specs=[_per_batch(h * TQ, d), _per_batch(TQ, 3 * h), _per_batch(gd, page), _per_batch(gd, page),
                      _per_batch(gd, LANE), _per_batch(gd, LANE), win_spec, win_spec,
                      _whole(cw, 2 * gd), _whole(cw, 2 * gd), _whole(1, d), _whole(1, d), _whole(d, d), _whole(d, d)]
                     + _page_specs((gd, page), layer, n_pages, pp) + _page_specs((gd, page), layer, n_pages, pp)
                     + _page_specs((gd, page), layer, n_pages, pp) + _page_specs((gd, page), layer, n_pages, pp),
            out_specs=_per_batch(TQ, h * d),
            scratch_shapes=[pltpu.VMEM((past, gd), F32), pltpu.VMEM((past, gd), F32),
                            pltpu.VMEM((gd, past + page), BF16), pltpu.VMEM((gd, past + page), BF16),
                            pltpu.VMEM((gd, w_buf + LANE), BF16), pltpu.VMEM((gd, w_buf + LANE), BF16)],
        ),
        out_shape=jax.ShapeDtypeStruct((b, TQ, h * d), F32),
        compiler_params=pltpu.CompilerParams(
            dimension_semantics=("parallel", "arbitrary"), vmem_limit_bytes=VMEM_LIMIT),
        name="nsa_sample_attention",
    )(page_table.reshape(-1).astype(I32), _head_rows(q),
      jnp.pad(gate_logits.astype(F32), ((0, 0), (0, TQ - t), (0, 0))),
      _new_cols(ks, page), _new_cols(vs, page), _new_cols(kw, LANE), _new_cols(vw, LANE),
      _cols_view(wins_k), _cols_view(wins_v), wck, wcv, pewk, pewv, w2k, w2v,
      *([_cols_view(pools_kc)] * pp), *([_cols_view(pools_vc)] * pp),
      *([_cols_view(pools_ks)] * pp), *([_cols_view(pools_vs)] * pp))
    return o[:, :t]


def last_rows(a, n):
    t = a.shape[1]
    if t >= n:
        return a[:, t - n:]
    return jnp.pad(a, ((0, 0), (n - t, 0)) + ((0, 0),) * (a.ndim - 2))


def mem_kv(mem, g, w_kv):
    b, m, _ = mem.shape
    k, v = jnp.split(mm(mem, w_kv, g), 2, axis=-1)
    return k.reshape(b, m, X_HEADS, X_HEAD_DIM), v.reshape(b, m, X_HEADS, X_HEAD_DIM)


def cross_attn(x, g, mk, mv, w_q, w_o):
    b, t, _ = x.shape
    q = mm(x, w_q, g).reshape(b, t, X_HEADS, X_HEAD_DIM)
    s = jnp.einsum('bthd,bmhd->bhtm', q, mk).astype(F32) * (X_HEAD_DIM ** -0.5)
    p = jax.nn.softmax(s, axis=-1).astype(x.dtype)
    return mm(jnp.einsum('bhtm,bmhd->bthd', p, mv).reshape(b, t, -1), w_o, res=x)


A_WIDTHS = (A_HEADS * HEAD_DIM, A_KV_HEADS * HEAD_DIM, A_KV_HEADS * HEAD_DIM, IDX_HEADS * IDX_DIM, IDX_DIM, IDX_HEADS)


def dsa_project(x, g, w_in, pos, keys_transposed):
    b, t, d = x.shape
    keys = [(F32, False), (BF16, keys_transposed)]
    segments = [(True, HEAD_DIM ** -0.5, [(BF16, False)]), (True, None, keys), (False, None, [(F32, False), (BF16, False)]),
                (True, None, [(BF16, False)]), (True, None, keys),
                (False, IDX_HEADS ** -0.5 * IDX_DIM ** -0.5, [(F32, False)])]
    return project(x.reshape(b * t, d), g, _split_cols(w_in, A_WIDTHS), segments, jnp.tile(pos, b))


def dsa_prompt(x, g, w_in, w_out):
    b, s, _ = x.shape
    q, k, kt, v, vb, iq, ik, ikt, iw = dsa_project(x, g, w_in, jnp.arange(s), True)
    o = dsa_prompt_attention(q.reshape(b, s, -1), kt, vb.reshape(b, s, -1), iq.reshape(b, s, -1), ikt,
                             iw.reshape(b, s, -1), min(TOPK_MAX, s // 4), A_KV_HEADS, IDX_HEADS)
    state = (k.reshape(b, s, A_KV_HEADS, HEAD_DIM), v.reshape(b, s, A_KV_HEADS, HEAD_DIM), ik.reshape(b, s, IDX_DIM))
    return mm(o, w_out, res=x), state


def dsa_sample(x, g, w_in, w_out, pools_k, pools_v, pools_idx, layer, page_table):
    b, t, _ = x.shape
    past = page_table.shape[1] * pools_k.shape[2]
    q, k, kb, v, vb, iq, ik, ikb, iw = dsa_project(x, g, w_in, past + jnp.arange(t), False)
    o = dsa_sample_attention(q.reshape(b, t, A_HEADS, HEAD_DIM), kb.reshape(b, t, A_KV_HEADS, HEAD_DIM),
                             vb.reshape(b, t, A_KV_HEADS, HEAD_DIM), iq.reshape(b, t, IDX_HEADS, IDX_DIM),
                             ikb.reshape(b, t, IDX_DIM), iw.reshape(b, t, IDX_HEADS), pools_k, pools_v, pools_idx,
                             layer, page_table, min(TOPK_MAX, (past + t) // 4))
    state = (k.reshape(b, t, A_KV_HEADS, HEAD_DIM), v.reshape(b, t, A_KV_HEADS, HEAD_DIM), ik.reshape(b, t, IDX_DIM))
    return mm(o, w_out, res=x), state


def _shift_rows(x, prev, k):
    return jnp.concatenate([prev[prev.shape[0] - k:], x[:x.shape[0] - k]], axis=0)


def _tile_transpose(m, rows_out, cols_out):
    r, c = m.shape
    tile = m
    if c < LANE:
        tile = jnp.concatenate([tile, jnp.zeros((r, LANE - c), m.dtype)], axis=1)
    if r < LANE:
        tile = jnp.concatenate([tile, jnp.zeros((LANE - r, LANE), m.dtype)], axis=0)
    return tile.T[:rows_out, :cols_out]


def _ssd_kernel(z_ref, x_ref, dt_ref, cw_ref, cb_ref, dtb_ref, alog_ref, dskip_ref, buf_ref, h0_ref,
                y_ref, hout_ref, carry_ref, ht_ref, *, n_heads, n_groups, hd, n_state):
    c = pl.program_id(1)
    l = x_ref.shape[0]
    d_inner = n_heads * hd
    rep = n_heads // n_groups

    @pl.when(c == 0)
    def _():
        carry_ref[...] = buf_ref[0]
        for h in range(n_heads):
            ht_ref[h] = _tile_transpose(h0_ref[0, h], n_state, hd)

    x = x_ref[...]
    prev = carry_ref[...]
    conv = cb_ref[...] + x * cw_ref[B_CONV - 1:B_CONV, :]
    for k in range(1, B_CONV):
        conv = conv + _shift_rows(x, prev, k) * cw_ref[B_CONV - 1 - k:B_CONV - k, :]
    carry_ref[...] = x[l - SUBLANE:]
    act = jax.nn.silu(conv)
    gn = n_groups * n_state
    xs, bm, cm = act[:, :d_inner], act[:, d_inner:d_inner + gn], act[:, d_inner + gn:]

    dt = jax.nn.softplus(dt_ref[...] + dtb_ref[...])
    a = -jnp.exp(alog_ref[...])
    acs = dt * a
    k = 1
    while k < l:
        acs = acs + jnp.concatenate([jnp.zeros((k, n_heads), F32), acs[:l - k]], axis=0)
        k *= 2
    acs_t = jnp.concatenate([acs, jnp.zeros((l, LANE - n_heads), F32)], axis=1).T
    last = acs[l - 1:l, :]
    to_end = jnp.exp(last - acs)
    grow = jnp.exp(acs)
    chunk_decay = jnp.exp(last)
    causal = lax.broadcasted_iota(I32, (l, l), 0) >= lax.broadcasted_iota(I32, (l, l), 1)
    z = z_ref[...]
    dskip = dskip_ref[...]

    for g in range(n_groups):
        bm_g = bm[:, g * n_state:(g + 1) * n_state]
        cm_g = cm[:, g * n_state:(g + 1) * n_state].astype(BF16)
        cb = lax.dot_general(cm_g, bm_g.astype(BF16), (((1,), (1,)), ((), ())), preferred_element_type=F32)
        bm_t = bm_g.T.astype(BF16)
        for r in range(rep):
            h = g * rep + r
            hs = slice(h * hd, (h + 1) * hd)
            xs_h = xs[:, hs]
            xr = xs_h * dt[:, h:h + 1]
            seg = acs[:, h:h + 1] - acs_t[h:h + 1, :]
            decay = jnp.where(causal, jnp.exp(jnp.where(causal, seg, 0.0)), 0.0)
            y = jnp.dot((cb * decay).astype(BF16), xr.astype(BF16), preferred_element_type=F32)
            ht = ht_ref[h]
            y = y + jnp.dot(cm_g, ht.astype(BF16), preferred_element_type=F32) * grow[:, h:h + 1]
            upd = jnp.dot(bm_t, (xr * to_end[:, h:h + 1]).astype(BF16), preferred_element_type=F32)
            ht_ref[h] = ht * chunk_decay[:, h:h + 1] + upd
            y = y + xs_h * dskip[:, hs]
            y_ref[:, hs] = y * jax.nn.silu(z[:, hs])

    @pl.when(c == pl.num_programs(1) - 1)
    def _():
        for h in range(n_heads):
            hout_ref[0, h] = _tile_transpose(ht_ref[h], hd, n_state)


def ssd_prompt(z, xbc, dt, conv_w, conv_b, dt_bias, a_log, d_skip, conv_buf, ssm0, n_groups):
    b, n_heads, hd, n_state = ssm0.shape
    m, w = xbc.shape
    s = m // b
    l = min(B_CHUNK, s)
    nc = s // l
    d_inner = n_heads * hd
    buf = jnp.pad(conv_buf.astype(F32), ((0, 0), (SUBLANE - (B_CONV - 1), 0), (0, 0)))
    dskip_full = jnp.repeat(d_skip.astype(F32), hd).reshape(1, d_inner)
    rows = lambda width: pl.BlockSpec((l, width), lambda bb, c: (bb * nc + c, 0))
    state = pl.BlockSpec((1, n_heads, hd, n_state), lambda bb, c: (bb, 0, 0, 0))
    kern = functools.partial(_ssd_kernel, n_heads=n_heads, n_groups=n_groups, hd=hd, n_state=n_state)
    return pl.pallas_call(
        kern,
        grid=(b, nc),
        in_specs=[rows(d_inner), rows(w), rows(n_heads), _resident((B_CONV, w)), _resident((1, w)),
                  _resident((1, n_heads)), _resident((1, n_heads)), _resident((1, d_inner)),
                  pl.BlockSpec((1, SUBLANE, w), lambda bb, c: (bb, 0, 0)), state],
        out_specs=[rows(d_inner), state],
        out_shape=[jax.ShapeDtypeStruct((m, d_inner), F32), jax.ShapeDtypeStruct(ssm0.shape, F32)],
        scratch_shapes=[pltpu.VMEM((SUBLANE, w), F32), pltpu.VMEM((n_heads, n_state, hd), F32)],
        compiler_params=pltpu.CompilerParams(
            dimension_semantics=("parallel", "arbitrary"), vmem_limit_bytes=VMEM_LIMIT),
        name="ssd_prompt",
    )(z, xbc, dt, conv_w.astype(F32), conv_b.reshape(1, w).astype(F32), dt_bias.reshape(1, n_heads).astype(F32),
      a_log.reshape(1, n_heads).astype(F32), dskip_full, buf, ssm0.astype(F32))


def mamba_prompt(x, g, w_in, conv_w, conv_b, dt_bias, a_log, d_skip, g_norm, w_out, conv_buf, ssm0):
    b, s, d = x.shape
    plain = (False, None, [(F32, False)])
    z, xbc, dt = project(x.reshape(b * s, d), g, _split_cols(w_in, (B_D_INNER, B_CONV_DIM, B_HEADS)),
                         [plain, plain, plain], jnp.zeros((b * s,), I32))
    y, ssm = ssd_prompt(z, xbc, dt, conv_w, conv_b, dt_bias, a_log, d_skip, conv_buf, ssm0, B_GROUPS)
    xpad = jnp.concatenate([conv_buf.astype(F32), xbc.reshape(b, s, -1)[:, s - (B_CONV - 1):]], axis=1)
    new_buf = xpad[:, xpad.shape[1] - (B_CONV - 1):]
    return mm(y.reshape(b, s, -1), w_out, g_norm, res=x), new_buf, ssm.astype(ssm0.dtype)


def ssd_chunked(x, dt, a, bm, cm, h0):
    bsz, t, nh, p = x.shape
    g, n = bm.shape[2], bm.shape[3]
    r = nh // g
    l = min(B_CHUNK, t)
    nc = t // l
    xr = (x * dt[..., None]).reshape(bsz, nc, l, g, r, p)
    acs = jnp.cumsum((dt * a).reshape(bsz, nc, l, g, r), axis=2)
    br = bm.reshape(bsz, nc, l, g, n)
    cr = cm.reshape(bsz, nc, l, g, n)
    seg = acs[:, :, :, None] - acs[:, :, None, :]
    causal = jnp.tril(jnp.ones((l, l), bool))[:, :, None, None]
    decay = jnp.where(causal, jnp.exp(jnp.where(causal, seg, 0.0)), 0.0)
    cb = jnp.einsum('bcign,bcjgn->bcijg', cr, br)
    y_intra = jnp.einsum('bcijg,bcijgr,bcjgrp->bcigrp', cb, decay, xr)
    to_end = jnp.exp(acs[:, :, -1:] - acs)
    s_chunk = jnp.einsum('bclgn,bclgr,bclgrp->bcgrpn', br, to_end, xr)
    d_chunk = jnp.exp(acs[:, :, -1])

    def step(hc, inp):
        s_c, d_c = inp
        return hc * d_c[..., None, None] + s_c, hc

    h_fin, h_in = lax.scan(step, h0.reshape(bsz, g, r, p, n),
                           (jnp.moveaxis(s_chunk, 1, 0), jnp.moveaxis(d_chunk, 1, 0)))
    h_in = jnp.moveaxis(h_in, 0, 1)
    y_inter = jnp.einsum('bcign,bcigr,bcgrpn->bcigrp', cr, jnp.exp(acs), h_in)
    return (y_intra + y_inter).reshape(bsz, t, nh, p), h_fin.reshape(bsz, nh, p, n)


def mamba_mixer(x, g, w_in, conv_w, conv_b, dt_bias, a_log, d_skip, g_norm, w_out, conv_buf, ssm0):
    b, t, _ = x.shape
    proj = mm(x, w_in, g)
    z = proj[..., :B_D_INNER]
    xbc = proj[..., B_D_INNER:B_D_INNER + B_CONV_DIM]
    dt = proj[..., B_D_INNER + B_CONV_DIM:]
    xpad = jnp.concatenate([conv_buf.astype(xbc.dtype), xbc], axis=1)
    conv = conv_b + sum(xpad[:, j:j + t] * conv_w[j] for j in range(B_CONV))
    xbc = jax.nn.silu(conv)
    new_buf = xpad[:, t:]
    gn = B_GROUPS * B_STATE
    xs = xbc[..., :B_D_INNER].reshape(b, t, B_HEADS, B_HEADDIM).astype(F32)
    bm = xbc[..., B_D_INNER:B_D_INNER + gn].reshape(b, t, B_GROUPS, B_STATE).astype(F32)
    cm = xbc[..., B_D_INNER + gn:].reshape(b, t, B_GROUPS, B_STATE).astype(F32)
    dt = jax.nn.softplus(dt.astype(F32) + dt_bias.astype(F32))
    a = -jnp.exp(a_log.astype(F32))
    y, ssm = ssd_chunked(xs, dt, a, bm, cm, ssm0.astype(F32))
    y = (y + xs * d_skip.astype(F32)[:, None]).reshape(b, t, B_D_INNER).astype(x.dtype)
    y = y * jax.nn.silu(z)
    return mm(y, w_out, g_norm, res=x), new_buf, ssm.astype(ssm0.dtype)


C_KV_WIDTH = C_KV_HEADS * HEAD_DIM
C_WIDTHS = (C_HEADS * HEAD_DIM,) + (C_KV_WIDTH,) * 6 + (3 * C_HEADS,)


def nsa_project(x, g, w_in, pos, keys_transposed):
    b, t, d = x.shape
    keys = (True, None, [(F32, False), (BF16, keys_transposed)])
    vals = (False, None, [(F32, False), (BF16, False)])
    segments = [(True, HEAD_DIM ** -0.5, [(BF16, False)]), (True, None, [(F32, False)]), (False, None, [(F32, False)]),
                keys, vals, keys, vals, (False, None, [(F32, False)])]
    return project(x.reshape(b * t, d), g, _split_cols(w_in, C_WIDTHS), segments, jnp.tile(pos, b))


def nsa_prompt(x, g, w_in, w_out, cmp_k, cmp_v, w_buf):
    b, s, _ = x.shape
    q, kc, vc, ks, kst, vs, vsb, kw, kwt, vw, vwb, gates = nsa_project(x, g, w_in, jnp.arange(s), True)
    seq = lambda a: a.reshape(b, s, -1)
    kct = compress_rows(seq(kc), cmp_k, C_KV_HEADS, True)
    vcc = compress_rows(seq(vc), cmp_v, C_KV_HEADS, False)
    n_cmp = s // CMP_STRIDE - CMP_BLOCK // CMP_STRIDE + 1
    o = nsa_prompt_attention(seq(q), seq(gates), kct, vcc, n_cmp, kst, seq(vsb), kwt, seq(vwb), C_KV_HEADS)
    heads = lambda a: a.reshape(b, s, C_KV_HEADS, HEAD_DIM)
    state = (heads(kc), heads(vc), heads(ks), heads(vs), last_rows(heads(kw), w_buf), last_rows(heads(vw), w_buf))
    return mm(o, w_out, res=x), state


def nsa_sample(x, g, w_in, w_out, cmp_k, cmp_v, pools_kc, pools_vc, pools_ks, pools_vs, wins_k, wins_v, layer,
               page_table):
    b, t, _ = x.shape
    past = page_table.shape[1] * pools_kc.shape[2]
    q, kc, vc, ks, ksb, vs, vsb, kw, kwb, vw, vwb, gates = nsa_project(x, g, w_in, past + jnp.arange(t), False)
    heads = lambda a: a.reshape(b, t, C_KV_HEADS, HEAD_DIM)
    o = nsa_sample_attention(q.reshape(b, t, C_HEADS, HEAD_DIM), gates.reshape(b, t, -1), heads(ksb), heads(vsb),
                             heads(kwb), heads(vwb), cmp_k, cmp_v, pools_kc, pools_vc, pools_ks, pools_vs, wins_k,
                             wins_v, layer, page_table)
    win_k, win_v = wins_k[layer], wins_v[layer]
    w_buf = win_k.shape[1]
    kwin = jnp.concatenate([win_k, heads(kw)], axis=1)
    vwin = jnp.concatenate([win_v, heads(vw)], axis=1)
    state = (heads(kc), heads(vc), heads(ks), heads(vs), last_rows(kwin, w_buf), last_rows(vwin, w_buf))
    return mm(o, w_out, res=x), state


def kernel(x_prompt, x_sample, cache_a_k, cache_a_v, cache_a_idx, state_b_ssm, state_b_conv, cache_c_cmp_k, cache_c_cmp_v, cache_c_slc_k, cache_c_slc_v, cache_c_win_k, cache_c_win_v, cache_mem_k, cache_mem_v, page_table, mem_prompt, g_ffn1, ffn1_wi, ffn1_wo, g_mix, g_xattn, g_mem, x_w_q, x_w_kv, x_w_o, g_ffn2, ffn2_wi, ffn2_wo, g_final, a_w_in, a_w_out, b_w_in, b_conv_w, b_conv_b, b_dt_bias, b_a_log, b_d_skip, b_g_norm, b_w_out, c_w_in, c_w_out, c_pe_k, c_w1_k, c_w2_k, c_pe_v, c_w1_v, c_w2_v):
    xp, xs = x_prompt, x_sample
    w_buf = cache_c_win_k.shape[2]
    ak_p, av_p, ai_p, bs_p, bc_p = [], [], [], [], []
    cck_p, ccv_p, csk_p, csv_p, cwk_p, cwv_p = [], [], [], [], [], []
    mk_p, mv_p = [], []
    ak_s, av_s, ai_s, bs_s, bc_s = [], [], [], [], []
    cck_s, ccv_s, csk_s, csv_s, cwk_s, cwv_s = [], [], [], [], [], []
    ia = ib = ic = 0
    for i in range(DEPTH):
        xp = ffn_residual(xp, g_ffn1[i], ffn1_wi[i], ffn1_wo[i])
        xs = ffn_residual(xs, g_ffn1[i], ffn1_wi[i], ffn1_wo[i])
        kind = i % N_MIXERS
        if kind == 0:
            op, (k1, v1, i1) = dsa_prompt(xp, g_mix[i], a_w_in[ia], a_w_out[ia])
            os_, (k2, v2, i2) = dsa_sample(xs, g_mix[i], a_w_in[ia], a_w_out[ia], cache_a_k, cache_a_v, cache_a_idx,
                                           ia, page_table)
            ak_p.append(k1); av_p.append(v1); ai_p.append(i1)
            ak_s.append(k2); av_s.append(v2); ai_s.append(i2)
            ia += 1
        elif kind == 1:
            conv0 = jnp.zeros((xp.shape[0], B_CONV - 1, B_CONV_DIM), xp.dtype)
            ssm0 = jnp.zeros((xp.shape[0], B_HEADS, B_HEADDIM, B_STATE), F32)
            wts = (b_w_in[ib], b_conv_w[ib], b_conv_b[ib], b_dt_bias[ib], b_a_log[ib], b_d_skip[ib],
                   b_g_norm[ib], b_w_out[ib])
            op, c1, s1 = mamba_prompt(xp, g_mix[i], *wts, conv0, ssm0)
            os_, c2, s2 = mamba_mixer(xs, g_mix[i], *wts, state_b_conv[ib], state_b_ssm[ib])
            bc_p.append(c1); bs_p.append(s1); bc_s.append(c2); bs_s.append(s2)
            ib += 1
        else:
            cmp_k = (c_pe_k[ic], c_w1_k[ic], c_w2_k[ic])
            cmp_v = (c_pe_v[ic], c_w1_v[ic], c_w2_v[ic])
            op, st1 = nsa_prompt(xp, g_mix[i], c_w_in[ic], c_w_out[ic], cmp_k, cmp_v, w_buf)
            os_, st2 = nsa_sample(xs, g_mix[i], c_w_in[ic], c_w_out[ic], cmp_k, cmp_v, cache_c_cmp_k, cache_c_cmp_v,
                                  cache_c_slc_k, cache_c_slc_v, cache_c_win_k, cache_c_win_v, ic, page_table)
            for lst, arr in zip((cck_p, ccv_p, csk_p, csv_p, cwk_p, cwv_p), st1):
                lst.append(arr)
            for lst, arr in zip((cck_s, ccv_s, csk_s, csv_s, cwk_s, cwv_s), st2):
                lst.append(arr)
            ic += 1
        xp, xs = op, os_
        mk, mv = mem_kv(mem_prompt, g_mem[i], x_w_kv[i])
        mk_p.append(mk); mv_p.append(mv)
        xp = cross_attn(xp, g_xattn[i], mk, mv, x_w_q[i], x_w_o[i])
        xs = cross_attn(xs, g_xattn[i], cache_mem_k[i], cache_mem_v[i], x_w_q[i], x_w_o[i])
        xp = ffn_residual(xp, g_ffn2[i], ffn2_wi[i], ffn2_wo[i])
        xs = ffn_residual(xs, g_ffn2[i], ffn2_wi[i], ffn2_wo[i])
    y_prompt = rms_norm_rows(xp, g_final)
    y_sample = rms_norm_rows(xs, g_final)
    st = jnp.stack
    return (y_prompt, y_sample,
            st(ak_p), st(av_p), st(ai_p), st(bs_p), st(bc_p),
            st(cck_p), st(ccv_p), st(csk_p), st(csv_p), st(cwk_p), st(cwv_p),
            st(mk_p), st(mv_p),
            st(ak_s), st(av_s), st(ai_s), st(bs_s), st(bc_s),
            st(cck_s), st(ccv_s), st(csk_s), st(csv_s), st(cwk_s), st(cwv_s))
```

```python
import functools

import jax
import jax.numpy as jnp
import numpy as np
from jax import lax
from jax.experimental import pallas as pl
from jax.experimental.pallas import tpu as pltpu

F32 = jnp.float32
BF16 = jnp.bfloat16
I32 = jnp.int32
INT_MIN = -2 ** 31
EPS = 1e-6

D_MODEL = 1024
DEPTH = 4
N_MIXERS = 3
HEAD_DIM = 64
ROPE_THETA = 10000.0
Q_BLOCK = 128
A_HEADS = D_MODEL // HEAD_DIM
A_KV_HEADS = 4
IDX_HEADS = 8
IDX_DIM = 64
TOPK_MAX = 256
B_D_INNER = 2 * D_MODEL
B_HEADDIM = 64
B_HEADS = B_D_INNER // B_HEADDIM
B_GROUPS = 4
B_STATE = 128
B_CONV = 4
B_CONV_DIM = B_D_INNER + 2 * B_GROUPS * B_STATE
B_CHUNK = 128
C_HEADS = D_MODEL // HEAD_DIM
C_KV_HEADS = 2
CMP_BLOCK = 32
CMP_STRIDE = 16
SLC_BLOCK = 64
N_SLC = 16
WINDOW = 512
X_HEADS = 4
X_HEAD_DIM = 128
D_FF = 2816

LANE = 128
SUBLANE = 8
VMEM_LIMIT = 48 * 1024 * 1024
MM_COLS = 512
FF_CHUNK = 256
ATTN_COLS = 1024
TQ = SUBLANE
PAGES_PER_STEP = 16


def _pick_tile(n, candidates):
    for c in candidates:
        if n % c == 0:
            return c
    return n


def _resident(shape):
    return pl.BlockSpec(shape, lambda *_: (0,) * len(shape), pipeline_mode=pl.Buffered(1))


def _rms(x, g):
    return x * lax.rsqrt(jnp.mean(x * x, axis=-1, keepdims=True) + EPS) * g


def _nmm_kernel(x_ref, g_ref, w_ref, *rest, norm, residual):
    o_ref = rest[-1]
    x = x_ref[...]
    if norm:
        x = _rms(x, g_ref[...])
    xb = x.astype(BF16)
    n = w_ref.shape[1]
    for c0 in range(0, n, MM_COLS):
        c1 = min(c0 + MM_COLS, n)
        y = jnp.dot(xb, w_ref[:, c0:c1], preferred_element_type=F32)
        o_ref[:, c0:c1] = rest[0][:, c0:c1] + y if residual else y


def norm_matmul(x, w, g=None, res=None):
    m, k = x.shape
    n = w.shape[1]
    assert res is None or (n % LANE == 0 and m % SUBLANE == 0)
    n_pad = -(-n // LANE) * LANE
    wb = w.astype(BF16)
    if n_pad != n:
        wb = jnp.pad(wb, ((0, 0), (0, n_pad - n)))
    m_orig = m
    if m % SUBLANE != 0:
        m = -(-m // 256) * 256
        x = jnp.pad(x, ((0, m - m_orig), (0, 0)))
    tm = _pick_tile(m, (256, 128, 64, 32, 16, 8))
    norm = g is not None
    gg = (g if norm else jnp.ones((k,), F32)).reshape(1, k).astype(F32)
    row_block = lambda width: pl.BlockSpec((tm, width), lambda i: (i, 0))
    extra = [] if res is None else [res]
    out = pl.pallas_call(
        functools.partial(_nmm_kernel, norm=norm, residual=res is not None),
        grid=(m // tm,),
        in_specs=[row_block(k), _resident((1, k)), _resident((k, n_pad))] + [row_block(n_pad) for _ in extra],
        out_specs=row_block(n_pad),
        out_shape=jax.ShapeDtypeStruct((m, n_pad), F32),
        compiler_params=pltpu.CompilerParams(dimension_semantics=("parallel",), vmem_limit_bytes=VMEM_LIMIT),
        name="norm_matmul",
    )(x, gg, wb, *extra)
    if n_pad != n or m != m_orig:
        out = out[:m_orig, :n]
    return out


def mm(x, w, g=None, res=None):
    lead = x.shape[:-1]
    r2 = None if res is None else res.reshape(-1, res.shape[-1])
    return norm_matmul(x.reshape(-1, x.shape[-1]), w, g, r2).reshape(lead + (w.shape[1],))


def _ffn_kernel(x_ref, g_ref, wg_ref, wu_ref, wo_ref, o_ref, act_ref, *, n_chunks):
    xn = _rms(x_ref[...], g_ref[...]).astype(BF16)

    def body(c, _):
        off = pl.multiple_of(c * FF_CHUNK, FF_CHUNK)
        hg = jnp.dot(xn, wg_ref[:, pl.ds(off, FF_CHUNK)], preferred_element_type=F32)
        hu = jnp.dot(xn, wu_ref[:, pl.ds(off, FF_CHUNK)], preferred_element_type=F32)
        act_ref[:, pl.ds(off, FF_CHUNK)] = (jax.nn.silu(hg) * hu).astype(BF16)
        return 0

    lax.fori_loop(0, n_chunks, body, 0)
    for c0 in range(0, o_ref.shape[1], MM_COLS):
        cs = slice(c0, c0 + MM_COLS)
        o_ref[:, cs] = x_ref[:, cs] + 0.5 * jnp.dot(act_ref[...], wo_ref[:, cs], preferred_element_type=F32)


def ffn_residual(x, g, wi, wo):
    lead, d = x.shape[:-1], x.shape[-1]
    x2 = x.reshape(-1, d)
    m = x2.shape[0]
    f = wo.shape[0]
    assert f % FF_CHUNK == 0 and d % MM_COLS == 0
    tm = _pick_tile(m, (512, 256, 128, 64, 32, 16, 8))
    wb = wi.astype(BF16)
    out = pl.pallas_call(
        functools.partial(_ffn_kernel, n_chunks=f // FF_CHUNK),
        grid=(m // tm,),
        in_specs=[pl.BlockSpec((tm, d), lambda i: (i, 0)), _resident((1, d)), _resident((d, f)), _resident((d, f)),
                  _resident((f, d))],
        out_specs=pl.BlockSpec((tm, d), lambda i: (i, 0)),
        out_shape=jax.ShapeDtypeStruct((m, d), F32),
        scratch_shapes=[pltpu.VMEM((tm, f), BF16)],
        compiler_params=pltpu.CompilerParams(dimension_semantics=("parallel",), vmem_limit_bytes=VMEM_LIMIT),
        name="ffn_residual",
    )(x2, g.reshape(1, d).astype(F32), wb[:, :f], wb[:, f:], wo.astype(BF16))
    return out.reshape(lead + (d,))


def _rmsnorm_kernel(x_ref, g_ref, o_ref):
    o_ref[...] = _rms(x_ref[...], g_ref[...])


def rms_norm_rows(x, g):
    lead, d = x.shape[:-1], x.shape[-1]
    x2 = x.reshape(-1, d)
    m = x2.shape[0]
    tm = _pick_tile(m, (512, 256, 128, 64, 32, 16, 8))
    row_block = pl.BlockSpec((tm, d), lambda i: (i, 0))
    out = pl.pallas_call(
        _rmsnorm_kernel, grid=(m // tm,), in_specs=[row_block, _resident((1, d))], out_specs=row_block,
        out_shape=jax.ShapeDtypeStruct((m, d), F32),
        compiler_params=pltpu.CompilerParams(dimension_semantics=("parallel",)), name="rms_norm_rows",
    )(x2, g.reshape(1, d).astype(F32))
    return out.reshape(lead + (d,))


def rope_tables(pos):
    half = HEAD_DIM // 2
    inv = ROPE_THETA ** (-jnp.arange(half, dtype=F32) / half)
    ang = pos.astype(F32)[:, None] * inv[None, :]
    cos, sin = jnp.cos(ang), jnp.sin(ang)
    reps = LANE // HEAD_DIM
    return (jnp.tile(jnp.concatenate([cos, cos], axis=1), (1, reps)),
            jnp.tile(jnp.concatenate([-sin, sin], axis=1), (1, reps)))


def _rope_cols(y, cos, sin):
    w = y.shape[1]
    half = HEAD_DIM // 2
    lane = lax.broadcasted_iota(I32, y.shape, 1)
    partner = jnp.where(lane % HEAD_DIM < half, pltpu.roll(y, w - half, axis=1), pltpu.roll(y, half, axis=1))
    reps = w // LANE
    return y * jnp.tile(cos, (1, reps)) + partner * jnp.tile(sin, (1, reps))


def _proj_kernel(x_ref, g_ref, cos_ref, sin_ref, w_ref, *out_refs, segments):
    xb = _rms(x_ref[...], g_ref[...]).astype(BF16)
    cos, sin = cos_ref[...], sin_ref[...]
    k = 0
    for start, width, rope, scale, outs in segments:
        y = jnp.dot(xb, w_ref[:, start:start + width], preferred_element_type=F32)
        if rope:
            y = _rope_cols(y, cos, sin)
        if scale is not None:
            y = y * scale
        for used, dtype, transposed in outs:
            v = y[:, :used]
            out_refs[k][...] = (v.T if transposed else v).astype(dtype)
            k += 1


def project(x, g, w_cols, segments, pos_rows):
    m, kdim = x.shape
    packed, segs, out_shapes, out_specs = [], [], [], []
    tm = _pick_tile(m, (256, 128, 64, 32, 16, 8))
    start = 0
    for wc, (rope, scale, outs) in zip(w_cols, segments):
        n = wc.shape[1]
        n_pad = -(-n // LANE) * LANE
        packed.append(jnp.pad(wc.astype(BF16), ((0, 0), (0, n_pad - n))))
        segs.append((start, n_pad, rope, scale, tuple((n, dt, tr) for dt, tr in outs)))
        for dt, tr in outs:
            if tr:
                out_shapes.append(jax.ShapeDtypeStruct((n, m), dt))
                out_specs.append(pl.BlockSpec((n, tm), lambda i: (0, i)))
            else:
                out_shapes.append(jax.ShapeDtypeStruct((m, n), dt))
                out_specs.append(pl.BlockSpec((tm, n), lambda i: (i, 0)))
        start += n_pad
    wb = jnp.concatenate(packed, axis=1)
    cos, sin = rope_tables(pos_rows)
    row_block = lambda width: pl.BlockSpec((tm, width), lambda i: (i, 0))
    return pl.pallas_call(
        functools.partial(_proj_kernel, segments=tuple(segs)),
        grid=(m // tm,),
        in_specs=[row_block(kdim), _resident((1, kdim)), row_block(LANE), row_block(LANE), _resident(wb.shape)],
        out_specs=out_specs,
        out_shape=out_shapes,
        compiler_params=pltpu.CompilerParams(dimension_semantics=("parallel",), vmem_limit_bytes=VMEM_LIMIT),
        name="project",
    )(x, g.reshape(1, kdim).astype(F32), cos, sin, wb)


def _split_cols(w, widths):
    cols = np.cumsum((0,) + tuple(widths))
    return [w[:, cols[j]:cols[j + 1]] for j in range(len(widths))]


def _softmax_rows(s):
    m = jnp.max(s, axis=-1, keepdims=True)
    e = jnp.exp(s - jnp.where(m == -jnp.inf, 0.0, m))
    den = jnp.sum(e, axis=-1, keepdims=True)
    return e / jnp.where(den > 0, den, 1.0)


def _flash_step(carry, q, kt_tile, v_tile, sel, rep):
    m, l, acc = carry
    rows, tk = q.shape[0], kt_tile.shape[1]
    s = jnp.dot(q, kt_tile, preferred_element_type=F32)
    s = jnp.where(sel, s.reshape(rep, Q_BLOCK, tk), -jnp.inf).reshape(rows, tk)
    m_new = jnp.maximum(m, jnp.max(s, axis=1, keepdims=True))
    m_safe = jnp.where(m_new == -jnp.inf, 0.0, m_new)
    alpha = jnp.exp(m - m_safe)
    p = jnp.exp(s - m_safe)
    l = l * alpha + jnp.sum(p, axis=1, keepdims=True)
    pv = jnp.dot(p.astype(BF16), v_tile, preferred_element_type=F32)
    return m_new, l, acc * alpha + pv


def _flash_init(rows, hd):
    return (jnp.full((rows, 1), -jnp.inf, F32), jnp.zeros((rows, 1), F32), jnp.zeros((rows, hd), F32))


def _flash_finish(carry):
    _, l, acc = carry
    return acc / jnp.where(l > 0, l, 1.0)


def _stack_heads(x, h0, rep, hd):
    return jnp.concatenate([x[:, (h0 + r) * hd:(h0 + r + 1) * hd] for r in range(rep)], axis=0)


def _sortable_key(score):
    bits = pltpu.bitcast(score, I32)
    key = jnp.where(bits < 0, bits ^ jnp.int32(0x7FFFFFFF), bits)
    return jnp.where(score == 0.0, 0, key)


def _count_lanes(pred_fn, key_ref, nkt, tk):
    rows = key_ref.shape[0]

    def body(kt, cnt):
        keys = key_ref[:, pl.ds(pl.multiple_of(kt * tk, tk), tk)]
        for c in range(tk // LANE):
            hit = pred_fn(keys[:, c * LANE:(c + 1) * LANE], kt * tk + c * LANE)
            cnt = cnt + jnp.where(hit, 1, 0)
        return cnt

    cnt = lax.fori_loop(0, nkt, body, jnp.zeros((rows, LANE), I32))
    return jnp.sum(cnt, axis=1, keepdims=True)


def _top_k_mask_params(key_ref, nkt, tk, topk, n_valid):
    rows = key_ref.shape[0]
    lane_iota = lax.broadcasted_iota(I32, (rows, LANE), 1)

    def bit_body(it, thr):
        cand = thr + jnp.left_shift(jnp.int32(1), 31 - it)
        cnt = _count_lanes(lambda keys, c0: keys >= cand, key_ref, nkt, tk)
        return jnp.where(cnt >= topk, cand, thr)

    thr = lax.fori_loop(0, 32, bit_body, jnp.full((rows, 1), INT_MIN, I32))
    many = n_valid > topk
    thr = jnp.where(many, thr, INT_MIN + 1)
    c_gt = _count_lanes(lambda keys, c0: keys > thr, key_ref, nkt, tk)
    c_eq = _count_lanes(lambda keys, c0: keys == thr, key_ref, nkt, tk)
    need = topk - c_gt
    excess = many & (c_eq > need)
    n_col_bits = max(1, (key_ref.shape[1] - 1).bit_length())

    def tie_search():
        def jbody(it, last):
            cand = last + jnp.left_shift(jnp.int32(1), n_col_bits - 1 - it)
            cnt = _count_lanes(lambda keys, c0: (keys == thr) & (c0 + lane_iota < cand), key_ref, nkt, tk)
            return jnp.where(cnt <= need - 1, cand, last)

        return lax.fori_loop(0, n_col_bits, jbody, jnp.zeros((rows, 1), I32))

    any_excess = jnp.max(jnp.where(excess, 1, 0)) > 0
    last_tie = lax.cond(any_excess, tie_search, lambda: jnp.zeros((rows, 1), I32))
    return thr, jnp.where(excess, last_tie, jnp.int32(2 ** 30))


SEARCH_DIGIT_BITS = 4


def _top_k_mask_params_small(keys, col, topk, n_valid):
    rows = keys.shape[0]
    count = lambda hit: jnp.sum(jnp.where(hit, 1, 0), axis=1, keepdims=True)
    thr = jnp.full((rows, 1), INT_MIN, I32)
    for shift in range(32 - SEARCH_DIGIT_BITS, -1, -SEARCH_DIGIT_BITS):
        digit = jnp.zeros((rows, 1), I32)
        for d in range(1, 2 ** SEARCH_DIGIT_BITS):
            step = np.int64(d << shift).astype(np.int32)
            digit = digit + jnp.where(count(keys >= thr + jnp.int32(step)) >= topk, 1, 0)
        thr = thr + jnp.left_shift(digit, shift)
    many = n_valid > topk
    thr = jnp.where(many, thr, INT_MIN + 1)
    need = topk - count(keys > thr)
    tie = keys == thr
    excess = many & (count(tie) > need)
    n_col_bits = max(1, (keys.shape[1] - 1).bit_length())

    def tie_search():
        last = jnp.zeros((rows, 1), I32)
        for it in range(n_col_bits):
            cand = last + (1 << (n_col_bits - 1 - it))
            last = jnp.where(count(tie & (col < cand)) <= need - 1, cand, last)
        return last

    any_excess = jnp.max(jnp.where(excess, 1, 0)) > 0
    last_tie = lax.cond(any_excess, tie_search, lambda: jnp.zeros((rows, 1), I32))
    return thr, jnp.where(excess, last_tie, jnp.int32(2 ** 30))


def _block_cover(n_cmp_rows):
    c0 = lax.broadcasted_iota(I32, (n_cmp_rows, LANE), 0) * CMP_STRIDE
    s0 = lax.broadcasted_iota(I32, (n_cmp_rows, LANE), 1) * SLC_BLOCK
    return jnp.where(c0 < s0 + SLC_BLOCK, jnp.where(c0 + CMP_BLOCK > s0, 1.0, 0.0), 0.0).astype(BF16)


def _block_importance(p_sum, cover, row_pos):
    p_hi = p_sum.astype(BF16)
    p_lo = (p_sum - p_hi.astype(F32)).astype(BF16)
    imp = jnp.dot(p_hi, cover, preferred_element_type=F32) + jnp.dot(p_lo, cover, preferred_element_type=F32)
    jb = lax.broadcasted_iota(I32, imp.shape, 1)
    cur = row_pos // SLC_BLOCK
    forced = (jb == 0) | (jb == cur) | (jb == cur - 1)
    imp = jnp.where(forced, jnp.inf, imp)
    return jnp.where(jb <= cur, imp, -jnp.inf)


def _dsa_prompt_kernel(iq_ref, iw_ref, ikt_ref, q_ref, kt_ref, v_ref, o_ref, key_ref, *, topk, tk,
                       n_idx_heads, n_groups, rep):
    i = pl.program_id(1)
    t0 = i * Q_BLOCK
    nkt = (t0 + Q_BLOCK + tk - 1) // tk
    row = t0 + lax.broadcasted_iota(I32, (Q_BLOCK, 1), 0)
    hd = kt_ref.shape[0] // n_groups
    di = ikt_ref.shape[0]

    iw = iw_ref[0]
    iq = iq_ref[0]
    iq_heads = [iq[:, h * di:(h + 1) * di] for h in range(n_idx_heads)]

    def score_body(kt, _):
        off = pl.multiple_of(kt * tk, tk)
        acc = jnp.zeros((Q_BLOCK, tk), F32)
        for h in range(n_idx_heads):
            r = jnp.dot(iq_heads[h], ikt_ref[:, pl.ds(off, tk)], preferred_element_type=F32)
            acc = acc + jnp.maximum(r, 0.0) * iw[:, h:h + 1]
        col = off + lax.broadcasted_iota(I32, (Q_BLOCK, tk), 1)
        key_ref[:, pl.ds(off, tk)] = jnp.where(col <= row, _sortable_key(acc), INT_MIN)
        return 0

    lax.fori_loop(0, nkt, score_body, 0)

    thr, last_tie = _top_k_mask_params(key_ref, nkt, tk, topk, row + 1)

    q = q_ref[0]
    q_groups = [_stack_heads(q, g * rep, rep, hd) for g in range(n_groups)]
    rows_g = rep * Q_BLOCK

    def attn_body(kt, carry):
        off = pl.multiple_of(kt * tk, tk)
        keys = key_ref[:, pl.ds(off, tk)]
        col = off + lax.broadcasted_iota(I32, (Q_BLOCK, tk), 1)
        sel = ((keys > thr) | ((keys == thr) & (col <= last_tie)))[None]
        v_t = v_ref[0, pl.ds(off, tk), :]
        return tuple(
            _flash_step(carry[g], q_groups[g], kt_ref[g * hd:(g + 1) * hd, pl.ds(off, tk)],
                        v_t[:, g * hd:(g + 1) * hd], sel, rep)
            for g in range(n_groups))

    fin = lax.fori_loop(0, nkt, attn_body, tuple(_flash_init(rows_g, hd) for _ in range(n_groups)))
    for g in range(n_groups):
        o = _flash_finish(fin[g])
        for r in range(rep):
            h = g * rep + r
            o_ref[0, :, h * hd:(h + 1) * hd] = o[r * Q_BLOCK:(r + 1) * Q_BLOCK]


def dsa_prompt_attention(q, kt, v, iq, ikt, iw, topk, n_groups, n_idx_heads, tk=ATTN_COLS):
    b, s, hd_all = q.shape
    gd, di = kt.shape[0], ikt.shape[0]
    d = gd // n_groups
    tk = min(tk, s)
    kern = functools.partial(_dsa_prompt_kernel, topk=topk, tk=tk, n_idx_heads=n_idx_heads, n_groups=n_groups,
                             rep=hd_all // gd)
    q_block = lambda width: pl.BlockSpec((1, Q_BLOCK, width), lambda bb, i: (bb, i, 0))
    return pl.pallas_call(
        kern,
        grid=(b, s // Q_BLOCK),
        in_specs=[
            q_block(n_idx_heads * di), q_block(n_idx_heads),
            pl.BlockSpec((di, s), lambda bb, i: (0, bb)),
            q_block(hd_all),
            pl.BlockSpec((gd, s), lambda bb, i: (0, bb)),
            pl.BlockSpec((1, s, gd), lambda bb, i: (bb, 0, 0)),
        ],
        out_specs=q_block(hd_all),
        out_shape=jax.ShapeDtypeStruct((b, s, hd_all), F32),
        scratch_shapes=[pltpu.VMEM((Q_BLOCK, s), I32)],
        compiler_params=pltpu.CompilerParams(
            dimension_semantics=("parallel", "arbitrary"), vmem_limit_bytes=VMEM_LIMIT),
        name="dsa_prompt_attention",
    )(iq, iw, ikt, q, kt, v)


def _top_blocks(imp_ref, n_live, n_sel):
    imp_t = imp_ref[...]
    j_iota = lax.broadcasted_iota(I32, imp_t.shape, 0)

    def body(k, rank):
        row_k = imp_ref[pl.ds(k, 1), :]
        earlier = jnp.where(j_iota > k, 1.0, 0.0)
        return rank + jnp.where(row_k > imp_t, 1.0, jnp.where(row_k == imp_t, earlier, 0.0))

    rank = lax.fori_loop(0, n_live, body, jnp.zeros(imp_t.shape, F32))
    return jnp.where(rank < n_sel, 1.0, 0.0)


def _nsa_prompt_kernel(q_ref, gate_ref, kct_ref, vc_ref, kst_ref, vs_ref, kwt_ref, vw_ref, o_ref, imp_ref, *,
                       tk, n_cmp, n_blk, n_groups, rep, win_len):
    i = pl.program_id(1)
    t0 = i * Q_BLOCK
    nkt = (t0 + Q_BLOCK + tk - 1) // tk
    hd = kst_ref.shape[0] // n_groups
    rows = rep * Q_BLOCK
    nc_pad = kct_ref.shape[-1]
    row = t0 + lax.broadcasted_iota(I32, (Q_BLOCK, 1), 0)
    gates = jax.nn.sigmoid(gate_ref[0])
    n_heads = n_groups * rep
    q_all = q_ref[0]

    cover = _block_cover(nc_pad)
    c_idx = lax.broadcasted_iota(I32, (Q_BLOCK, nc_pad), 1)
    c_ok = ((c_idx < n_cmp) & (c_idx * CMP_STRIDE + (CMP_BLOCK - 1) <= row))[None]
    w_start = pl.multiple_of(jnp.maximum(t0 - WINDOW, 0), Q_BLOCK)
    w_col = w_start + lax.broadcasted_iota(I32, (Q_BLOCK, win_len), 1)
    w_d = row - w_col
    w_ok = ((w_d >= 0) & (w_d < WINDOW))[None]

    for g in range(n_groups):
        q = _stack_heads(q_all, g * rep, rep, hd)
        lo, hi = g * hd, (g + 1) * hd

        s = jnp.dot(q, kct_ref[0, lo:hi, :], preferred_element_type=F32)
        s = jnp.where(c_ok, s.reshape(rep, Q_BLOCK, nc_pad), -jnp.inf)
        p = _softmax_rows(s)
        o_cmp = jnp.dot(p.reshape(rows, nc_pad).astype(BF16), vc_ref[0, :, lo:hi], preferred_element_type=F32)

        imp = _block_importance(jnp.sum(p, axis=0), cover, row)
        imp_ref[...] = imp.T
        n_live = jnp.minimum((t0 + Q_BLOCK - 1) // SLC_BLOCK + 1, n_blk)
        sel_blocks = _top_blocks(imp_ref, n_live, min(N_SLC, n_blk)).T.astype(BF16)

        def slc_body(kt, carry):
            off = pl.multiple_of(kt * tk, tk)
            col = off + lax.broadcasted_iota(I32, (Q_BLOCK, tk), 1)
            blk_of_col = (off + lax.broadcasted_iota(I32, (LANE, tk), 1)) // SLC_BLOCK
            expand = jnp.where(lax.broadcasted_iota(I32, (LANE, tk), 0) == blk_of_col, 1.0, 0.0).astype(BF16)
            hit = jnp.dot(sel_blocks, expand, preferred_element_type=F32)
            sel = (jnp.where(col <= row, hit, 0.0) > 0.5)[None]
            return _flash_step(carry, q, kst_ref[lo:hi, pl.ds(off, tk)], vs_ref[0, pl.ds(off, tk), lo:hi], sel, rep)

        o_slc = _flash_finish(lax.fori_loop(0, nkt, slc_body, _flash_init(rows, hd)))

        s = jnp.dot(q, kwt_ref[lo:hi, pl.ds(w_start, win_len)], preferred_element_type=F32)
        s = jnp.where(w_ok, s.reshape(rep, Q_BLOCK, win_len), -jnp.inf)
        p = _softmax_rows(s).reshape(rows, win_len)
        o_win = jnp.dot(p.astype(BF16), vw_ref[0, pl.ds(w_start, win_len), lo:hi], preferred_element_type=F32)

        for r in range(rep):
            h = g * rep + r
            rs = slice(r * Q_BLOCK, (r + 1) * Q_BLOCK)
            o_ref[0, :, h * hd:(h + 1) * hd] = (
                gates[:, h:h + 1] * o_cmp[rs] + gates[:, n_heads + h:n_heads + h + 1] * o_slc[rs]
                + gates[:, 2 * n_heads + h:2 * n_heads + h + 1] * o_win[rs])


def nsa_prompt_attention(q, gate_logits, kct, vc, n_cmp, kst, vs, kwt, vw, n_groups, tk=ATTN_COLS):
    b, s, hd_all = q.shape
    gd, nc_pad = kct.shape[1], kct.shape[2]
    n_blk = -(-s // SLC_BLOCK)
    assert n_blk <= LANE and s % Q_BLOCK == 0
    tk = min(tk, s)
    win_len = min(WINDOW + Q_BLOCK, s)
    kern = functools.partial(_nsa_prompt_kernel, tk=tk, n_cmp=n_cmp, n_blk=n_blk, n_groups=n_groups,
                             rep=hd_all // gd, win_len=win_len)
    per_b = lambda *blk: pl.BlockSpec((1,) + blk, lambda bb, i: (bb,) + (0,) * len(blk))
    keys_t = pl.BlockSpec((gd, s), lambda bb, i: (0, bb))
    q_block = lambda width: pl.BlockSpec((1, Q_BLOCK, width), lambda bb, i: (bb, i, 0))
    return pl.pallas_call(
        kern,
        grid=(b, s // Q_BLOCK),
        in_specs=[q_block(hd_all), q_block(gate_logits.shape[-1]), per_b(gd, nc_pad), per_b(nc_pad, gd),
                  keys_t, per_b(s, gd), keys_t, per_b(s, gd)],
        out_specs=q_block(hd_all),
        out_shape=jax.ShapeDtypeStruct((b, s, hd_all), F32),
        scratch_shapes=[pltpu.VMEM((LANE, Q_BLOCK), F32)],
        compiler_params=pltpu.CompilerParams(
            dimension_semantics=("parallel", "arbitrary"), vmem_limit_bytes=VMEM_LIMIT),
        name="nsa_prompt_attention",
    )(q, gate_logits, kct, vc, kst, vs, kwt, vw)


def _compress_finish(a, pew_ref, w2_ref, n_groups, hd):
    half = n_groups * hd
    first, second = a[:, :half], a[:, half:]
    second = jnp.concatenate([second[1:], jnp.zeros((1, half), F32)], axis=0)
    out = []
    for g in range(n_groups):
        hcol = first[:, g * hd:(g + 1) * hd] + second[:, g * hd:(g + 1) * hd] + pew_ref[...]
        out.append(jnp.dot(jax.nn.silu(hcol).astype(BF16), w2_ref[...], preferred_element_type=F32))
    return out


def _compress_kernel(c_ref, wc_ref, pew_ref, w2_ref, o_ref, *, n_groups, hd, transposed):
    a = jnp.dot(c_ref[0].astype(BF16), wc_ref[...], preferred_element_type=F32)
    res = jnp.concatenate(_compress_finish(a, pew_ref, w2_ref, n_groups, hd), axis=1)
    o_ref[0] = (res.T if transposed else res).astype(BF16)


def _compress_chunk_weights(w1, n_groups, hd):
    wr = w1.reshape(2, CMP_STRIDE, hd, hd)
    eye = jnp.eye(n_groups, dtype=w1.dtype)
    wc = jnp.einsum('hrdo,gk->rgdhko', wr, eye)
    return wc.reshape(CMP_STRIDE * n_groups * hd, 2 * n_groups * hd)


def _compress_weights(pe, w1, w2, n_groups, hd):
    return _compress_chunk_weights(w1, n_groups, hd).astype(BF16), mm(pe.reshape(1, -1), w1), w2.astype(BF16)


def compress_rows(rows, cmp_w, n_groups, transposed):
    b, l, gd = rows.shape
    hd = gd // n_groups
    n_chunks = l // CMP_STRIDE
    cw = CMP_STRIDE * gd
    wc, pew, w2 = _compress_weights(*cmp_w, n_groups, hd)
    out_blk = (gd, n_chunks) if transposed else (n_chunks, gd)
    return pl.pallas_call(
        functools.partial(_compress_kernel, n_groups=n_groups, hd=hd, transposed=transposed),
        grid=(b,),
        in_specs=[pl.BlockSpec((1, n_chunks, cw), lambda bb: (bb, 0, 0)), _resident(wc.shape), _resident(pew.shape),
                  _resident(w2.shape)],
        out_specs=pl.BlockSpec((1,) + out_blk, lambda bb: (bb, 0, 0)),
        out_shape=jax.ShapeDtypeStruct((b,) + out_blk, BF16),
        compiler_params=pltpu.CompilerParams(dimension_semantics=("parallel",), vmem_limit_bytes=VMEM_LIMIT),
        name="compress_rows",
    )(rows.reshape(b, n_chunks, cw), wc, pew, w2)


def _attend_cols(q, kt, vt, sel, rep):
    n = kt.shape[1]
    s = jnp.dot(q, kt, preferred_element_type=F32)
    s = jnp.where(sel, s.reshape(rep, TQ, n), -jnp.inf)
    p = _softmax_rows(s).reshape(rep * TQ, n)
    return p, lax.dot_general(p.astype(BF16), vt, (((1,), (1,)), ((), ())), preferred_element_type=F32)


def _head_rows(x):
    b, t, h, d = x.shape
    x = jnp.pad(x.astype(BF16), ((0, 0), (0, TQ - t), (0, 0), (0, 0)))
    return x.transpose(0, 2, 1, 3).reshape(b, h * TQ, d)


def _new_cols(x, width):
    b, t = x.shape[:2]
    return jnp.pad(x.reshape(b, t, -1).astype(BF16).transpose(0, 2, 1), ((0, 0), (0, 0), (0, width - t)))


def _cols_view(pool):
    lead, rows = pool.shape[:2], pool.shape[2]
    perm = (0, 1) + tuple(range(3, pool.ndim)) + (2,)
    return pool.transpose(perm).reshape(lead + (-1, rows))


def _page_specs(block, layer, n_pages, pp):
    def spec(j):
        return pl.BlockSpec((1, 1) + block,
                            lambda bb, p, pt: (layer, pt[bb * n_pages + p * pp + j]) + (0,) * len(block))
    return [spec(j) for j in range(pp)]


def _per_batch(*blk):
    return pl.BlockSpec((1,) + blk, lambda bb, p, pt: (bb,) + (0,) * len(blk))


def _whole(*blk):
    return pl.BlockSpec(blk, lambda bb, p, pt: (0,) * len(blk))


def _dsa_sample_kernel(pt_ref, iq_ref, iw_ref, q_ref, ikn_ref, kn_ref, vn_ref, *rest, topk, past, n_new,
                       n_idx_heads, n_groups, rep, pp):
    idx_pages = rest[:pp]
    k_pages = rest[pp:2 * pp]
    v_pages = rest[2 * pp:3 * pp]
    o_ref, key_ref, kst, vst = rest[3 * pp:]
    p_step = pl.program_id(1)
    hd = q_ref.shape[-1]
    page = idx_pages[0].shape[-1]
    n_tiles = past // page + 1
    iw = iw_ref[0]
    tpos = past + jnp.minimum(lax.broadcasted_iota(I32, (TQ, 1), 0), n_new - 1)

    def scores(ikt):
        r = jnp.dot(iq_ref[0], ikt, preferred_element_type=F32)
        r = jnp.maximum(r, 0.0).reshape(n_idx_heads, TQ, ikt.shape[1]) * iw
        return jnp.sum(r, axis=0)

    for j in range(pp):
        off = pl.multiple_of((p_step * pp + j) * page, page)
        key_ref[:, pl.ds(off, page)] = _sortable_key(scores(idx_pages[j][0, 0].astype(BF16)))
        kst[:, pl.ds(off, page)] = k_pages[j][0, 0].astype(BF16)
        vst[:, pl.ds(off, page)] = v_pages[j][0, 0].astype(BF16)

    @pl.when(p_step == pl.num_programs(1) - 1)
    def _():
        c = lax.broadcasted_iota(I32, (TQ, page), 1)
        fresh_ok = (c < n_new) & (past + c <= tpos)
        key_ref[:, past:past + page] = jnp.where(fresh_ok, _sortable_key(scores(ikn_ref[0])), INT_MIN)
        kst[:, past:past + page] = kn_ref[0]
        vst[:, past:past + page] = vn_ref[0]
        keys = key_ref[...]
        col = lax.broadcasted_iota(I32, keys.shape, 1)
        thr, last_tie = _top_k_mask_params_small(keys, col, topk, tpos + 1)
        sel = ((keys > thr) | ((keys == thr) & (col <= last_tie)))[None]
        for g in range(n_groups):
            gs = slice(g * hd, (g + 1) * hd)
            _, o = _attend_cols(q_ref[0, g * rep * TQ:(g + 1) * rep * TQ], kst[gs, :], vst[gs, :], sel, rep)
            for r in range(rep):
                h = g * rep + r
                o_ref[0, :, h * hd:(h + 1) * hd] = o[r * TQ:(r + 1) * TQ]


def dsa_sample_attention(q, k, v, iq, ik, iw, pools_k, pools_v, pools_idx, layer, page_table, topk):
    b, t, h, d = q.shape
    g = k.shape[2]
    hi, di = iq.shape[2], iq.shape[3]
    n_pages = page_table.shape[1]
    page = pools_k.shape[2]
    past = n_pages * page
    pp = min(PAGES_PER_STEP, n_pages)
    assert n_pages % pp == 0 and t <= TQ
    iw_h = jnp.pad(iw.astype(F32), ((0, 0), (0, TQ - t), (0, 0))).transpose(0, 2, 1)[..., None]
    kern = functools.partial(_dsa_sample_kernel, topk=topk, past=past, n_new=t, n_idx_heads=hi, n_groups=g,
                             rep=h // g, pp=pp)
    o = pl.pallas_call(
        kern,
        grid_spec=pltpu.PrefetchScalarGridSpec(
            num_scalar_prefetch=1,
            grid=(b, n_pages // pp),
            in_specs=[_per_batch(hi * TQ, di), _per_batch(hi, TQ, 1), _per_batch(h * TQ, d), _per_batch(di, page),
                      _per_batch(g * d, page), _per_batch(g * d, page)]
                     + _page_specs((di, page), layer, n_pages, pp)
                     + _page_specs((g * d, page), layer, n_pages, pp)
                     + _page_specs((g * d, page), layer, n_pages, pp),
            out_specs=_per_batch(TQ, h * d),
            scratch_shapes=[pltpu.VMEM((TQ, past + page), I32), pltpu.VMEM((g * d, past + page), BF16),
                            pltpu.VMEM((g * d, past + page), BF16)],
        ),
        out_shape=jax.ShapeDtypeStruct((b, TQ, h * d), F32),
        compiler_params=pltpu.CompilerParams(
            dimension_semantics=("parallel", "arbitrary"), vmem_limit_bytes=VMEM_LIMIT),
        name="dsa_sample_attention",
    )(page_table.reshape(-1).astype(I32), _head_rows(iq), iw_h, _head_rows(q), _new_cols(ik, page),
      _new_cols(k, page), _new_cols(v, page),
      *([_cols_view(pools_idx)] * pp), *([_cols_view(pools_k)] * pp), *([_cols_view(pools_v)] * pp))
    return o[:, :t]


def _top_blocks_rows(imp, n_blk, n_sel):
    j_iota = lax.broadcasted_iota(I32, imp.shape, 1)
    rank = jnp.zeros(imp.shape, F32)
    for k in range(n_blk):
        col_k = imp[:, k:k + 1]
        earlier = jnp.where(j_iota > k, 1.0, 0.0)
        rank = rank + jnp.where(col_k > imp, 1.0, jnp.where(col_k == imp, earlier, 0.0))
    return jnp.where((rank < n_sel) & (j_iota < n_blk), 1.0, 0.0)


def _nsa_sample_kernel(pt_ref, q_ref, gate_ref, ksn_ref, vsn_ref, kwn_ref, vwn_ref, wink_ref, winv_ref,
                       wck_ref, wcv_ref, pewk_ref, pewv_ref, w2k_ref, w2v_ref, *rest, past, n_new, n_groups, rep, pp):
    kc_pages = rest[:pp]
    vc_pages = rest[pp:2 * pp]
    ks_pages = rest[2 * pp:3 * pp]
    vs_pages = rest[3 * pp:4 * pp]
    o_ref, kcs, vcs, kst, vst, kwt, vwt = rest[4 * pp:]
    p_step = pl.program_id(1)
    hd = q_ref.shape[-1]
    gd = n_groups * hd
    page = ks_pages[0].shape[-1]
    n_chunks = kcs.shape[0] // CMP_STRIDE
    n_cmp = n_chunks - CMP_BLOCK // CMP_STRIDE + 1
    n_blk = -(-(past + n_new) // SLC_BLOCK)
    w_buf = wink_ref.shape[-1]
    n_heads = n_groups * rep
    rows = rep * TQ

    for j in range(pp):
        off = pl.multiple_of((p_step * pp + j) * page, page)
        kcs[pl.ds(off, page), :] = kc_pages[j][0, 0].T
        vcs[pl.ds(off, page), :] = vc_pages[j][0, 0].T
        kst[:, pl.ds(off, page)] = ks_pages[j][0, 0].astype(BF16)
        vst[:, pl.ds(off, page)] = vs_pages[j][0, 0].astype(BF16)

    @pl.when(p_step == pl.num_programs(1) - 1)
    def _():
        kst[:, past:past + page] = ksn_ref[0]
        vst[:, past:past + page] = vsn_ref[0]
        kwt[:, 0:w_buf] = wink_ref[0, 0].astype(BF16)
        vwt[:, 0:w_buf] = winv_ref[0, 0].astype(BF16)
        kwt[:, w_buf:w_buf + LANE] = kwn_ref[0]
        vwt[:, w_buf:w_buf + LANE] = vwn_ref[0]

        tpos = past + jnp.minimum(lax.broadcasted_iota(I32, (TQ, 1), 0), n_new - 1)
        gates = jax.nn.sigmoid(gate_ref[0])

        def compress(rows_ref, wc_ref, pew_ref, w2_ref):
            a = jnp.zeros((n_chunks, 2 * gd), F32)
            for r in range(CMP_STRIDE):
                xr = rows_ref[pl.ds(r, n_chunks, stride=CMP_STRIDE), :].astype(BF16)
                a = a + jnp.dot(xr, wc_ref[r * gd:(r + 1) * gd, :], preferred_element_type=F32)
            out = _compress_finish(a, pew_ref, w2_ref, n_groups, hd)
            return [o.astype(BF16) for o in out]

        kcmp = compress(kcs, wck_ref, pewk_ref, w2k_ref)
        vcmp = compress(vcs, wcv_ref, pewv_ref, w2v_ref)

        c_idx = lax.broadcasted_iota(I32, (TQ, n_chunks), 1)
        c_ok = ((c_idx < n_cmp) & (c_idx * CMP_STRIDE + (CMP_BLOCK - 1) <= tpos))[None]
        cover = _block_cover(n_chunks)
        n_slc = kst.shape[1]
        col = lax.broadcasted_iota(I32, (TQ, n_slc), 1)
        blk_of_col = lax.broadcasted_iota(I32, (LANE, n_slc), 1) // SLC_BLOCK
        expand = jnp.where(lax.broadcasted_iota(I32, (LANE, n_slc), 0) == blk_of_col, 1.0, 0.0).astype(BF16)
        n_win = kwt.shape[1]
        w_c = lax.broadcasted_iota(I32, (TQ, n_win), 1)
        w_pos = past - w_buf + w_c
        w_d = tpos - w_pos
        w_ok = ((w_c < w_buf + n_new) & (w_pos >= 0) & (w_d >= 0) & (w_d < WINDOW))[None]

        for g in range(n_groups):
            q = q_ref[0, g * rows:(g + 1) * rows]
            gs = slice(g * hd, (g + 1) * hd)
            s = lax.dot_general(q, kcmp[g], (((1,), (1,)), ((), ())), preferred_element_type=F32)
            p_cmp = _softmax_rows(jnp.where(c_ok, s.reshape(rep, TQ, n_chunks), -jnp.inf))
            o_cmp = jnp.dot(p_cmp.reshape(rows, n_chunks).astype(BF16), vcmp[g], preferred_element_type=F32)
            imp = _block_importance(jnp.sum(p_cmp, axis=0), cover, tpos)
            sel_blocks = _top_blocks_rows(imp, n_blk, min(N_SLC, n_blk)).astype(BF16)
            hit = jnp.dot(sel_blocks, expand, preferred_element_type=F32)
            sel = (jnp.where(col <= tpos, hit, 0.0) > 0.5)[None]
            _, o_slc = _attend_cols(q, kst[gs, :], vst[gs, :], sel, rep)
            _, o_win = _attend_cols(q, kwt[gs, :], vwt[gs, :], w_ok, rep)
            for r in range(rep):
                h = g * rep + r
                rs = slice(r * TQ, (r + 1) * TQ)
                o_ref[0, :, h * hd:(h + 1) * hd] = (
                    gates[:, h:h + 1] * o_cmp[rs] + gates[:, n_heads + h:n_heads + h + 1] * o_slc[rs]
                    + gates[:, 2 * n_heads + h:2 * n_heads + h + 1] * o_win[rs])


def nsa_sample_attention(q, gate_logits, ks, vs, kw, vw, cmp_k, cmp_v, pools_kc, pools_vc, pools_ks, pools_vs,
                         wins_k, wins_v, layer, page_table):
    b, t, h, d = q.shape
    g = ks.shape[2]
    gd = g * d
    n_pages = page_table.shape[1]
    page = pools_ks.shape[2]
    past = n_pages * page
    pp = min(PAGES_PER_STEP, n_pages)
    assert n_pages % pp == 0 and t <= TQ and (past + t) // CMP_STRIDE * CMP_STRIDE == past
    cw = CMP_STRIDE * gd
    w_buf = wins_k.shape[2]

    wck, pewk, w2k = _compress_weights(*cmp_k, g, d)
    wcv, pewv, w2v = _compress_weights(*cmp_v, g, d)
    kern = functools.partial(_nsa_sample_kernel, past=past, n_new=t, n_groups=g, rep=h // g, pp=pp)
    win_spec = pl.BlockSpec((1, 1, gd, w_buf), lambda bb, p, pt: (layer, bb, 0, 0))
    o = pl.pallas_call(
        kern,
        grid_spec=pltpu.PrefetchScalarGridSpec(
            num_scalar_prefetch=1,
            grid=(b, n_pages // pp),
            in_specs=[_per_batch(h * TQ, d), _per_batch(TQ, 3 * h), _per_batch(gd, page), _per_batch(gd, page),
                      _per_batch(gd, LANE), _per_batch(gd, LANE), win_spec, win_spec,
                      _whole(cw, 2 * gd), _whole(cw, 2 * gd), _whole(1, d), _whole(1, d), _whole(d, d), _whole(d, d)]
                     + _page_specs((gd, page), layer, n_pages, pp) + _page_specs((gd, page), layer, n_pages, pp)
                     + _page_specs((gd, page), layer, n_pages, pp) + _page_specs((gd, page), layer, n_pages, pp),
            out_specs=_per_batch(TQ, h * d),
            scratch_shapes=[pltpu.VMEM((past, gd), F32), pltpu.VMEM((past, gd), F32),
                            pltpu.VMEM((gd, past + page), BF16), pltpu.VMEM((gd, past + page), BF16),
                            pltpu.VMEM((gd, w_buf + LANE), BF16), pltpu.VMEM((gd, w_buf + LANE), BF16)],
        ),
        out_shape=jax.ShapeDtypeStruct((b, TQ, h * d), F32),
        compiler_params=pltpu.CompilerParams(
            dimension_semantics=("parallel", "arbitrary"), vmem_limit_bytes=VMEM_LIMIT),
        name="nsa_sample_attention",
    )(page_table.reshape(-1).astype(I32), _head_rows(q),
      jnp.pad(gate_logits.astype(F32), ((0, 0), (0, TQ - t), (0, 0))),
      _new_cols(ks, page), _new_cols(vs, page), _new_cols(kw, LANE), _new_cols(vw, LANE),
      _cols_view(wins_k), _cols_view(wins_v), wck, wcv, pewk, pewv, w2k, w2v,
      *([_cols_view(pools_kc)] * pp), *([_cols_view(pools_vc)] * pp),
      *([_cols_view(pools_ks)] * pp), *([_cols_view(pools_vs)] * pp))
    return o[:, :t]


def last_rows(a, n):
    t = a.shape[1]
    if t >= n:
        return a[:, t - n:]
    return jnp.pad(a, ((0, 0), (n - t, 0)) + ((0, 0),) * (a.ndim - 2))


def mem_kv(mem, g, w_kv):
    b, m, _ = mem.shape
    k, v = jnp.split(mm(mem, w_kv, g), 2, axis=-1)
    return k.reshape(b, m, X_HEADS, X_HEAD_DIM), v.reshape(b, m, X_HEADS, X_HEAD_DIM)


def cross_attn(x, g, mk, mv, w_q, w_o):
    b, t, _ = x.shape
    q = mm(x, w_q, g).reshape(b, t, X_HEADS, X_HEAD_DIM)
    s = jnp.einsum('bthd,bmhd->bhtm', q, mk).astype(F32) * (X_HEAD_DIM ** -0.5)
    p = jax.nn.softmax(s, axis=-1).astype(x.dtype)
    return mm(jnp.einsum('bhtm,bmhd->bthd', p, mv).reshape(b, t, -1), w_o, res=x)


A_WIDTHS = (A_HEADS * HEAD_DIM, A_KV_HEADS * HEAD_DIM, A_KV_HEADS * HEAD_DIM, IDX_HEADS * IDX_DIM, IDX_DIM, IDX_HEADS)


def dsa_project(x, g, w_in, pos, keys_transposed):
    b, t, d = x.shape
    keys = [(F32, False), (BF16, keys_transposed)]
    segments = [(True, HEAD_DIM ** -0.5, [(BF16, False)]), (True, None, keys), (False, None, [(F32, False), (BF16, False)]),
                (True, None, [(BF16, False)]), (True, None, keys),
                (False, IDX_HEADS ** -0.5 * IDX_DIM ** -0.5, [(F32, False)])]
    return project(x.reshape(b * t, d), g, _split_cols(w_in, A_WIDTHS), segments, jnp.tile(pos, b))


def dsa_prompt(x, g, w_in, w_out):
    b, s, _ = x.shape
    q, k, kt, v, vb, iq, ik, ikt, iw = dsa_project(x, g, w_in, jnp.arange(s), True)
    o = dsa_prompt_attention(q.reshape(b, s, -1), kt, vb.reshape(b, s, -1), iq.reshape(b, s, -1), ikt,
                             iw.reshape(b, s, -1), min(TOPK_MAX, s // 4), A_KV_HEADS, IDX_HEADS)
    state = (k.reshape(b, s, A_KV_HEADS, HEAD_DIM), v.reshape(b, s, A_KV_HEADS, HEAD_DIM), ik.reshape(b, s, IDX_DIM))
    return mm(o, w_out, res=x), state


def dsa_sample(x, g, w_in, w_out, pools_k, pools_v, pools_idx, layer, page_table):
    b, t, _ = x.shape
    past = page_table.shape[1] * pools_k.shape[2]
    q, k, kb, v, vb, iq, ik, ikb, iw = dsa_project(x, g, w_in, past + jnp.arange(t), False)
    o = dsa_sample_attention(q.reshape(b, t, A_HEADS, HEAD_DIM), kb.reshape(b, t, A_KV_HEADS, HEAD_DIM),
                             vb.reshape(b, t, A_KV_HEADS, HEAD_DIM), iq.reshape(b, t, IDX_HEADS, IDX_DIM),
                             ikb.reshape(b, t, IDX_DIM), iw.reshape(b, t, IDX_HEADS), pools_k, pools_v, pools_idx,
                             layer, page_table, min(TOPK_MAX, (past + t) // 4))
    state = (k.reshape(b, t, A_KV_HEADS, HEAD_DIM), v.reshape(b, t, A_KV_HEADS, HEAD_DIM), ik.reshape(b, t, IDX_DIM))
    return mm(o, w_out, res=x), state


def _shift_rows(x, prev, k):
    return jnp.concatenate([prev[prev.shape[0] - k:], x[:x.shape[0] - k]], axis=0)


def _tile_transpose(m, rows_out, cols_out):
    r, c = m.shape
    tile = m
    if c < LANE:
        tile = jnp.concatenate([tile, jnp.zeros((r, LANE - c), m.dtype)], axis=1)
    if r < LANE:
        tile = jnp.concatenate([tile, jnp.zeros((LANE - r, LANE), m.dtype)], axis=0)
    return tile.T[:rows_out, :cols_out]


def _ssd_kernel(z_ref, x_ref, dt_ref, cw_ref, cb_ref, dtb_ref, alog_ref, dskip_ref, buf_ref, h0_ref,
                y_ref, hout_ref, carry_ref, ht_ref, *, n_heads, n_groups, hd, n_state):
    c = pl.program_id(1)
    l = x_ref.shape[0]
    d_inner = n_heads * hd
    rep = n_heads // n_groups

    @pl.when(c == 0)
    def _():
        carry_ref[...] = buf_ref[0]
        for h in range(n_heads):
            ht_ref[h] = _tile_transpose(h0_ref[0, h], n_state, hd)

    x = x_ref[...]
    prev = carry_ref[...]
    conv = cb_ref[...] + x * cw_ref[B_CONV - 1:B_CONV, :]
    for k in range(1, B_CONV):
        conv = conv + _shift_rows(x, prev, k) * cw_ref[B_CONV - 1 - k:B_CONV - k, :]
    carry_ref[...] = x[l - SUBLANE:]
    act = jax.nn.silu(conv)
    gn = n_groups * n_state
    xs, bm, cm = act[:, :d_inner], act[:, d_inner:d_inner + gn], act[:, d_inner + gn:]

    dt = jax.nn.softplus(dt_ref[...] + dtb_ref[...])
    a = -jnp.exp(alog_ref[...])
    acs = dt * a
    k = 1
    while k < l:
        acs = acs + jnp.concatenate([jnp.zeros((k, n_heads), F32), acs[:l - k]], axis=0)
        k *= 2
    acs_t = jnp.concatenate([acs, jnp.zeros((l, LANE - n_heads), F32)], axis=1).T
    last = acs[l - 1:l, :]
    to_end = jnp.exp(last - acs)
    grow = jnp.exp(acs)
    chunk_decay = jnp.exp(last)
    causal = lax.broadcasted_iota(I32, (l, l), 0) >= lax.broadcasted_iota(I32, (l, l), 1)
    z = z_ref[...]
    dskip = dskip_ref[...]

    for g in range(n_groups):
        bm_g = bm[:, g * n_state:(g + 1) * n_state]
        cm_g = cm[:, g * n_state:(g + 1) * n_state].astype(BF16)
        cb = lax.dot_general(cm_g, bm_g.astype(BF16), (((1,), (1,)), ((), ())), preferred_element_type=F32)
        bm_t = bm_g.T.astype(BF16)
        for r in range(rep):
            h = g * rep + r
            hs = slice(h * hd, (h + 1) * hd)
            xs_h = xs[:, hs]
            xr = xs_h * dt[:, h:h + 1]
            seg = acs[:, h:h + 1] - acs_t[h:h + 1, :]
            decay = jnp.where(causal, jnp.exp(jnp.where(causal, seg, 0.0)), 0.0)
            y = jnp.dot((cb * decay).astype(BF16), xr.astype(BF16), preferred_element_type=F32)
            ht = ht_ref[h]
            y = y + jnp.dot(cm_g, ht.astype(BF16), preferred_element_type=F32) * grow[:, h:h + 1]
            upd = jnp.dot(bm_t, (xr * to_end[:, h:h + 1]).astype(BF16), preferred_element_type=F32)
            ht_ref[h] = ht * chunk_decay[:, h:h + 1] + upd
            y = y + xs_h * dskip[:, hs]
            y_ref[:, hs] = y * jax.nn.silu(z[:, hs])

    @pl.when(c == pl.num_programs(1) - 1)
    def _():
        for h in range(n_heads):
            hout_ref[0, h] = _tile_transpose(ht_ref[h], hd, n_state)


def ssd_prompt(z, xbc, dt, conv_w, conv_b, dt_bias, a_log, d_skip, conv_buf, ssm0, n_groups):
    b, n_heads, hd, n_state = ssm0.shape
    m, w = xbc.shape
    s = m // b
    l = min(B_CHUNK, s)
    nc = s // l
    d_inner = n_heads * hd
    buf = jnp.pad(conv_buf.astype(F32), ((0, 0), (SUBLANE - (B_CONV - 1), 0), (0, 0)))
    dskip_full = jnp.repeat(d_skip.astype(F32), hd).reshape(1, d_inner)
    rows = lambda width: pl.BlockSpec((l, width), lambda bb, c: (bb * nc + c, 0))
    state = pl.BlockSpec((1, n_heads, hd, n_state), lambda bb, c: (bb, 0, 0, 0))
    kern = functools.partial(_ssd_kernel, n_heads=n_heads, n_groups=n_groups, hd=hd, n_state=n_state)
    return pl.pallas_call(
        kern,
        grid=(b, nc),
        in_specs=[rows(d_inner), rows(w), rows(n_heads), _resident((B_CONV, w)), _resident((1, w)),
                  _resident((1, n_heads)), _resident((1, n_heads)), _resident((1, d_inner)),
                  pl.BlockSpec((1, SUBLANE, w), lambda bb, c: (bb, 0, 0)), state],
        out_specs=[rows(d_inner), state],
        out_shape=[jax.ShapeDtypeStruct((m, d_inner), F32), jax.ShapeDtypeStruct(ssm0.shape, F32)],
        scratch_shapes=[pltpu.VMEM((SUBLANE, w), F32), pltpu.VMEM((n_heads, n_state, hd), F32)],
        compiler_params=pltpu.CompilerParams(
            dimension_semantics=("parallel", "arbitrary"), vmem_limit_bytes=VMEM_LIMIT),
        name="ssd_prompt",
    )(z, xbc, dt, conv_w.astype(F32), conv_b.reshape(1, w).astype(F32), dt_bias.reshape(1, n_heads).astype(F32),
      a_log.reshape(1, n_heads).astype(F32), dskip_full, buf, ssm0.astype(F32))


def mamba_prompt(x, g, w_in, conv_w, conv_b, dt_bias, a_log, d_skip, g_norm, w_out, conv_buf, ssm0):
    b, s, d = x.shape
    plain = (False, None, [(F32, False)])
    z, xbc, dt = project(x.reshape(b * s, d), g, _split_cols(w_in, (B_D_INNER, B_CONV_DIM, B_HEADS)),
                         [plain, plain, plain], jnp.zeros((b * s,), I32))
    y, ssm = ssd_prompt(z, xbc, dt, conv_w, conv_b, dt_bias, a_log, d_skip, conv_buf, ssm0, B_GROUPS)
    xpad = jnp.concatenate([conv_buf.astype(F32), xbc.reshape(b, s, -1)[:, s - (B_CONV - 1):]], axis=1)
    new_buf = xpad[:, xpad.shape[1] - (B_CONV - 1):]
    return mm(y.reshape(b, s, -1), w_out, g_norm, res=x), new_buf, ssm.astype(ssm0.dtype)


def ssd_chunked(x, dt, a, bm, cm, h0):
    bsz, t, nh, p = x.shape
    g, n = bm.shape[2], bm.shape[3]
    r = nh // g
    l = min(B_CHUNK, t)
    nc = t // l
    xr = (x * dt[..., None]).reshape(bsz, nc, l, g, r, p)
    acs = jnp.cumsum((dt * a).reshape(bsz, nc, l, g, r), axis=2)
    br = bm.reshape(bsz, nc, l, g, n)
    cr = cm.reshape(bsz, nc, l, g, n)
    seg = acs[:, :, :, None] - acs[:, :, None, :]
    causal = jnp.tril(jnp.ones((l, l), bool))[:, :, None, None]
    decay = jnp.where(causal, jnp.exp(jnp.where(causal, seg, 0.0)), 0.0)
    cb = jnp.einsum('bcign,bcjgn->bcijg', cr, br)
    y_intra = jnp.einsum('bcijg,bcijgr,bcjgrp->bcigrp', cb, decay, xr)
    to_end = jnp.exp(acs[:, :, -1:] - acs)
    s_chunk = jnp.einsum('bclgn,bclgr,bclgrp->bcgrpn', br, to_end, xr)
    d_chunk = jnp.exp(acs[:, :, -1])

    def step(hc, inp):
        s_c, d_c = inp
        return hc * d_c[..., None, None] + s_c, hc

    h_fin, h_in = lax.scan(step, h0.reshape(bsz, g, r, p, n),
                           (jnp.moveaxis(s_chunk, 1, 0), jnp.moveaxis(d_chunk, 1, 0)))
    h_in = jnp.moveaxis(h_in, 0, 1)
    y_inter = jnp.einsum('bcign,bcigr,bcgrpn->bcigrp', cr, jnp.exp(acs), h_in)
    return (y_intra + y_inter).reshape(bsz, t, nh, p), h_fin.reshape(bsz, nh, p, n)


def mamba_mixer(x, g, w_in, conv_w, conv_b, dt_bias, a_log, d_skip, g_norm, w_out, conv_buf, ssm0):
    b, t, _ = x.shape
    proj = mm(x, w_in, g)
    z = proj[..., :B_D_INNER]
    xbc = proj[..., B_D_INNER:B_D_INNER + B_CONV_DIM]
    dt = proj[..., B_D_INNER + B_CONV_DIM:]
    xpad = jnp.concatenate([conv_buf.astype(xbc.dtype), xbc], axis=1)
    conv = conv_b + sum(xpad[:, j:j + t] * conv_w[j] for j in range(B_CONV))
    xbc = jax.nn.silu(conv)
    new_buf = xpad[:, t:]
    gn = B_GROUPS * B_STATE
    xs = xbc[..., :B_D_INNER].reshape(b, t, B_HEADS, B_HEADDIM).astype(F32)
    bm = xbc[..., B_D_INNER:B_D_INNER + gn].reshape(b, t, B_GROUPS, B_STATE).astype(F32)
    cm = xbc[..., B_D_INNER + gn:].reshape(b, t, B_GROUPS, B_STATE).astype(F32)
    dt = jax.nn.softplus(dt.astype(F32) + dt_bias.astype(F32))
    a = -jnp.exp(a_log.astype(F32))
    y, ssm = ssd_chunked(xs, dt, a, bm, cm, ssm0.astype(F32))
    y = (y + xs * d_skip.astype(F32)[:, None]).reshape(b, t, B_D_INNER).astype(x.dtype)
    y = y * jax.nn.silu(z)
    return mm(y, w_out, g_norm, res=x), new_buf, ssm.astype(ssm0.dtype)


C_KV_WIDTH = C_KV_HEADS * HEAD_DIM
C_WIDTHS = (C_HEADS * HEAD_DIM,) + (C_KV_WIDTH,) * 6 + (3 * C_HEADS,)


def nsa_project(x, g, w_in, pos, keys_transposed):
    b, t, d = x.shape
    keys = (True, None, [(F32, False), (BF16, keys_transposed)])
    vals = (False, None, [(F32, False), (BF16, False)])
    segments = [(True, HEAD_DIM ** -0.5, [(BF16, False)]), (True, None, [(F32, False)]), (False, None, [(F32, False)]),
                keys, vals, keys, vals, (False, None, [(F32, False)])]
    return project(x.reshape(b * t, d), g, _split_cols(w_in, C_WIDTHS), segments, jnp.tile(pos, b))


def nsa_prompt(x, g, w_in, w_out, cmp_k, cmp_v, w_buf):
    b, s, _ = x.shape
    q, kc, vc, ks, kst, vs, vsb, kw, kwt, vw, vwb, gates = nsa_project(x, g, w_in, jnp.arange(s), True)
    seq = lambda a: a.reshape(b, s, -1)
    kct = compress_rows(seq(kc), cmp_k, C_KV_HEADS, True)
    vcc = compress_rows(seq(vc), cmp_v, C_KV_HEADS, False)
    n_cmp = s // CMP_STRIDE - CMP_BLOCK // CMP_STRIDE + 1
    o = nsa_prompt_attention(seq(q), seq(gates), kct, vcc, n_cmp, kst, seq(vsb), kwt, seq(vwb), C_KV_HEADS)
    heads = lambda a: a.reshape(b, s, C_KV_HEADS, HEAD_DIM)
    state = (heads(kc), heads(vc), heads(ks), heads(vs), last_rows(heads(kw), w_buf), last_rows(heads(vw), w_buf))
    return mm(o, w_out, res=x), state


def nsa_sample(x, g, w_in, w_out, cmp_k, cmp_v, pools_kc, pools_vc, pools_ks, pools_vs, wins_k, wins_v, layer,
               page_table):
    b, t, _ = x.shape
    past = page_table.shape[1] * pools_kc.shape[2]
    q, kc, vc, ks, ksb, vs, vsb, kw, kwb, vw, vwb, gates = nsa_project(x, g, w_in, past + jnp.arange(t), False)
    heads = lambda a: a.reshape(b, t, C_KV_HEADS, HEAD_DIM)
    o = nsa_sample_attention(q.reshape(b, t, C_HEADS, HEAD_DIM), gates.reshape(b, t, -1), heads(ksb), heads(vsb),
                             heads(kwb), heads(vwb), cmp_k, cmp_v, pools_kc, pools_vc, pools_ks, pools_vs, wins_k,
                             wins_v, layer, page_table)
    win_k, win_v = wins_k[layer], wins_v[layer]
    w_buf = win_k.shape[1]
    kwin = jnp.concatenate([win_k, heads(kw)], axis=1)
    vwin = jnp.concatenate([win_v, heads(vw)], axis=1)
    state = (heads(kc), heads(vc), heads(ks), heads(vs), last_rows(kwin, w_buf), last_rows(vwin, w_buf))
    return mm(o, w_out, res=x), state


def kernel(x_prompt, x_sample, cache_a_k, cache_a_v, cache_a_idx, state_b_ssm, state_b_conv, cache_c_cmp_k, cache_c_cmp_v, cache_c_slc_k, cache_c_slc_v, cache_c_win_k, cache_c_win_v, cache_mem_k, cache_mem_v, page_table, mem_prompt, g_ffn1, ffn1_wi, ffn1_wo, g_mix, g_xattn, g_mem, x_w_q, x_w_kv, x_w_o, g_ffn2, ffn2_wi, ffn2_wo, g_final, a_w_in, a_w_out, b_w_in, b_conv_w, b_conv_b, b_dt_bias, b_a_log, b_d_skip, b_g_norm, b_w_out, c_w_in, c_w_out, c_pe_k, c_w1_k, c_w2_k, c_pe_v, c_w1_v, c_w2_v):
    xp, xs = x_prompt, x_sample
    w_buf = cache_c_win_k.shape[2]
    ak_p, av_p, ai_p, bs_p, bc_p = [], [], [], [], []
    cck_p, ccv_p, csk_p, csv_p, cwk_p, cwv_p = [], [], [], [], [], []
    mk_p, mv_p = [], []
    ak_s, av_s, ai_s, bs_s, bc_s = [], [], [], [], []
    cck_s, ccv_s, csk_s, csv_s, cwk_s, cwv_s = [], [], [], [], [], []
    ia = ib = ic = 0
    for i in range(DEPTH):
        xp = ffn_residual(xp, g_ffn1[i], ffn1_wi[i], ffn1_wo[i])
        xs = ffn_residual(xs, g_ffn1[i], ffn1_wi[i], ffn1_wo[i])
        kind = i % N_MIXERS
        if kind == 0:
            op, (k1, v1, i1) = dsa_prompt(xp, g_mix[i], a_w_in[ia], a_w_out[ia])
            os_, (k2, v2, i2) = dsa_sample(xs, g_mix[i], a_w_in[ia], a_w_out[ia], cache_a_k, cache_a_v, cache_a_idx,
                                           ia, page_table)
            ak_p.append(k1); av_p.append(v1); ai_p.append(i1)
            ak_s.append(k2); av_s.append(v2); ai_s.append(i2)
            ia += 1
        elif kind == 1:
            conv0 = jnp.zeros((xp.shape[0], B_CONV - 1, B_CONV_DIM), xp.dtype)
            ssm0 = jnp.zeros((xp.shape[0], B_HEADS, B_HEADDIM, B_STATE), F32)
            wts = (b_w_in[ib], b_conv_w[ib], b_conv_b[ib], b_dt_bias[ib], b_a_log[ib], b_d_skip[ib],
                   b_g_norm[ib], b_w_out[ib])
            op, c1, s1 = mamba_prompt(xp, g_mix[i], *wts, conv0, ssm0)
            os_, c2, s2 = mamba_mixer(xs, g_mix[i], *wts, state_b_conv[ib], state_b_ssm[ib])
            bc_p.append(c1); bs_p.append(s1); bc_s.append(c2); bs_s.append(s2)
            ib += 1
        else:
            cmp_k = (c_pe_k[ic], c_w1_k[ic], c_w2_k[ic])
            cmp_v = (c_pe_v[ic], c_w1_v[ic], c_w2_v[ic])
            op, st1 = nsa_prompt(xp, g_mix[i], c_w_in[ic], c_w_out[ic], cmp_k, cmp_v, w_buf)
            os_, st2 = nsa_sample(xs, g_mix[i], c_w_in[ic], c_w_out[ic], cmp_k, cmp_v, cache_c_cmp_k, cache_c_cmp_v,
                                  cache_c_slc_k, cache_c_slc_v, cache_c_win_k, cache_c_win_v, ic, page_table)
            for lst, arr in zip((cck_p, ccv_p, csk_p, csv_p, cwk_p, cwv_p), st1):
                lst.append(arr)
            for lst, arr in zip((cck_s, ccv_s, csk_s, csv_s, cwk_s, cwv_s), st2):
                lst.append(arr)
            ic += 1
        xp, xs = op, os_
        mk, mv = mem_kv(mem_prompt, g_mem[i], x_w_kv[i])
        mk_p.append(mk); mv_p.append(mv)
        xp = cross_attn(xp, g_xattn[i], mk, mv, x_w_q[i], x_w_o[i])
        xs = cross_attn(xs, g_xattn[i], cache_mem_k[i], cache_mem_v[i], x_w_q[i], x_w_o[i])
        xp = ffn_residual(xp, g_ffn2[i], ffn2_wi[i], ffn2_wo[i])
        xs = ffn_residual(xs, g_ffn2[i], ffn2_wi[i], ffn2_wo[i])
    y_prompt = rms_norm_rows(xp, g_final)
    y_sample = rms_norm_rows(xs, g_final)
    st = jnp.stack
    return (y_prompt, y_sample,
            st(ak_p), st(av_p), st(ai_p), st(bs_p), st(bc_p),
            st(cck_p), st(ccv_p), st(csk_p), st(csv_p), st(cwk_p), st(cwv_p),
            st(mk_p), st(mv_p),
            st(ak_s), st(av_s), st(ai_s), st(bs_s), st(bc_s),
            st(cck_s), st(ccv_s), st(csk_s), st(csv_s), st(cwk_s), st(cwv_s))
```

```python
import functools

import jax
import jax.numpy as jnp
import numpy as np
from jax import lax
from jax.experimental import pallas as pl
from jax.experimental.pallas import tpu as pltpu

F32 = jnp.float32
BF16 = jnp.bfloat16
I32 = jnp.int32
INT_MIN = -2 ** 31
EPS = 1e-6

D_MODEL = 1024
DEPTH = 4
N_MIXERS = 3
HEAD_DIM = 64
ROPE_THETA = 10000.0
Q_BLOCK = 128
A_HEADS = D_MODEL // HEAD_DIM
A_KV_HEADS = 4
IDX_HEADS = 8
IDX_DIM = 64
TOPK_MAX = 256
B_D_INNER = 2 * D_MODEL
B_HEADDIM = 64
B_HEADS = B_D_INNER // B_HEADDIM
B_GROUPS = 4
B_STATE = 128
B_CONV = 4
B_CONV_DIM = B_D_INNER + 2 * B_GROUPS * B_STATE
B_CHUNK = 128
C_HEADS = D_MODEL // HEAD_DIM
C_KV_HEADS = 2
CMP_BLOCK = 32
CMP_STRIDE = 16
SLC_BLOCK = 64
N_SLC = 16
WINDOW = 512
X_HEADS = 4
X_HEAD_DIM = 128
D_FF = 2816

LANE = 128
SUBLANE = 8
VMEM_LIMIT = 48 * 1024 * 1024
MM_COLS = 512
FF_CHUNK = 256
ATTN_COLS = 1024
TQ = SUBLANE
PAGES_PER_STEP = 16


def _pick_tile(n, candidates):
    for c in candidates:
        if n % c == 0:
            return c
    return n


def _resident(shape):
    return pl.BlockSpec(shape, lambda *_: (0,) * len(shape), pipeline_mode=pl.Buffered(1))


def _rms(x, g):
    return x * lax.rsqrt(jnp.mean(x * x, axis=-1, keepdims=True) + EPS) * g


def _nmm_kernel(x_ref, g_ref, w_ref, *rest, norm, residual):
    o_ref = rest[-1]
    x = x_ref[...]
    if norm:
        x = _rms(x, g_ref[...])
    xb = x.astype(BF16)
    n = w_ref.shape[1]
    for c0 in range(0, n, MM_COLS):
        c1 = min(c0 + MM_COLS, n)
        y = jnp.dot(xb, w_ref[:, c0:c1], preferred_element_type=F32)
        o_ref[:, c0:c1] = rest[0][:, c0:c1] + y if residual else y


def norm_matmul(x, w, g=None, res=None):
    m, k = x.shape
    n = w.shape[1]
    assert res is None or (n % LANE == 0 and m % SUBLANE == 0)
    n_pad = -(-n // LANE) * LANE
    wb = w.astype(BF16)
    if n_pad != n:
        wb = jnp.pad(wb, ((0, 0), (0, n_pad - n)))
    m_orig = m
    if m % SUBLANE != 0:
        m = -(-m // 256) * 256
        x = jnp.pad(x, ((0, m - m_orig), (0, 0)))
    tm = _pick_tile(m, (256, 128, 64, 32, 16, 8))
    norm = g is not None
    gg = (g if norm else jnp.ones((k,), F32)).reshape(1, k).astype(F32)
    row_block = lambda width: pl.BlockSpec((tm, width), lambda i: (i, 0))
    extra = [] if res is None else [res]
    out = pl.pallas_call(
        functools.partial(_nmm_kernel, norm=norm, residual=res is not None),
        grid=(m // tm,),
        in_specs=[row_block(k), _resident((1, k)), _resident((k, n_pad))] + [row_block(n_pad) for _ in extra],
        out_specs=row_block(n_pad),
        out_shape=jax.ShapeDtypeStruct((m, n_pad), F32),
        compiler_params=pltpu.CompilerParams(dimension_semantics=("parallel",), vmem_limit_bytes=VMEM_LIMIT),
        name="norm_matmul",
    )(x, gg, wb, *extra)
    if n_pad != n or m != m_orig:
        out = out[:m_orig, :n]
    return out


def mm(x, w, g=None, res=None):
    lead = x.shape[:-1]
    r2 = None if res is None else res.reshape(-1, res.shape[-1])
    return norm_matmul(x.reshape(-1, x.shape[-1]), w, g, r2).reshape(lead + (w.shape[1],))


def _ffn_kernel(x_ref, g_ref, wg_ref, wu_ref, wo_ref, o_ref, act_ref, *, n_chunks):
    xn = _rms(x_ref[...], g_ref[...]).astype(BF16)

    for c in range(n_chunks):
        cs = slice(c * FF_CHUNK, (c + 1) * FF_CHUNK)
        hg = jnp.dot(xn, wg_ref[:, cs], preferred_element_type=F32)
        hu = jnp.dot(xn, wu_ref[:, cs], preferred_element_type=F32)
        act_ref[:, cs] = (jax.nn.silu(hg) * hu).astype(BF16)
    for c0 in range(0, o_ref.shape[1], MM_COLS):
        cs = slice(c0, c0 + MM_COLS)
        o_ref[:, cs] = x_ref[:, cs] + 0.5 * jnp.dot(act_ref[...], wo_ref[:, cs], preferred_element_type=F32)


def ffn_residual(x, g, wi, wo):
    lead, d = x.shape[:-1], x.shape[-1]
    x2 = x.reshape(-1, d)
    m = x2.shape[0]
    f = wo.shape[0]
    assert f % FF_CHUNK == 0 and d % MM_COLS == 0
    tm = _pick_tile(m, (512, 256, 128, 64, 32, 16, 8))
    wb = wi.astype(BF16)
    out = pl.pallas_call(
        functools.partial(_ffn_kernel, n_chunks=f // FF_CHUNK),
        grid=(m // tm,),
        in_specs=[pl.BlockSpec((tm, d), lambda i: (i, 0)), _resident((1, d)), _resident((d, f)), _resident((d, f)),
                  _resident((f, d))],
        out_specs=pl.BlockSpec((tm, d), lambda i: (i, 0)),
        out_shape=jax.ShapeDtypeStruct((m, d), F32),
        scratch_shapes=[pltpu.VMEM((tm, f), BF16)],
        compiler_params=pltpu.CompilerParams(dimension_semantics=("parallel",), vmem_limit_bytes=VMEM_LIMIT),
        name="ffn_residual",
    )(x2, g.reshape(1, d).astype(F32), wb[:, :f], wb[:, f:], wo.astype(BF16))
    return out.reshape(lead + (d,))


def _rmsnorm_kernel(x_ref, g_ref, o_ref):
    o_ref[...] = _rms(x_ref[...], g_ref[...])


def rms_norm_rows(x, g):
    lead, d = x.shape[:-1], x.shape[-1]
    x2 = x.reshape(-1, d)
    m = x2.shape[0]
    tm = _pick_tile(m, (512, 256, 128, 64, 32, 16, 8))
    row_block = pl.BlockSpec((tm, d), lambda i: (i, 0))
    out = pl.pallas_call(
        _rmsnorm_kernel, grid=(m // tm,), in_specs=[row_block, _resident((1, d))], out_specs=row_block,
        out_shape=jax.ShapeDtypeStruct((m, d), F32),
        compiler_params=pltpu.CompilerParams(dimension_semantics=("parallel",)), name="rms_norm_rows",
    )(x2, g.reshape(1, d).astype(F32))
    return out.reshape(lead + (d,))


def rope_tables(pos):
    half = HEAD_DIM // 2
    inv = ROPE_THETA ** (-jnp.arange(half, dtype=F32) / half)
    ang = pos.astype(F32)[:, None] * inv[None, :]
    cos, sin = jnp.cos(ang), jnp.sin(ang)
    reps = LANE // HEAD_DIM
    return (jnp.tile(jnp.concatenate([cos, cos], axis=1), (1, reps)),
            jnp.tile(jnp.concatenate([-sin, sin], axis=1), (1, reps)))


def _rope_cols(y, cos, sin):
    w = y.shape[1]
    half = HEAD_DIM // 2
    lane = lax.broadcasted_iota(I32, y.shape, 1)
    partner = jnp.where(lane % HEAD_DIM < half, pltpu.roll(y, w - half, axis=1), pltpu.roll(y, half, axis=1))
    reps = w // LANE
    return y * jnp.tile(cos, (1, reps)) + partner * jnp.tile(sin, (1, reps))


def _proj_kernel(x_ref, g_ref, cos_ref, sin_ref, w_ref, *out_refs, segments):
    xb = _rms(x_ref[...], g_ref[...]).astype(BF16)
    cos, sin = cos_ref[...], sin_ref[...]
    k = 0
    for start, width, rope, scale, outs in segments:
        y = jnp.dot(xb, w_ref[:, start:start + width], preferred_element_type=F32)
        if rope:
            y = _rope_cols(y, cos, sin)
        if scale is not None:
            y = y * scale
        for used, dtype, transposed in outs:
            v = y[:, :used]
            out_refs[k][...] = (v.T if transposed else v).astype(dtype)
            k += 1


def project(x, g, w_cols, segments, pos_rows):
    m, kdim = x.shape
    packed, segs, out_shapes, out_specs = [], [], [], []
    tm = _pick_tile(m, (256, 128, 64, 32, 16, 8))
    start = 0
    for wc, (rope, scale, outs) in zip(w_cols, segments):
        n = wc.shape[1]
        n_pad = -(-n // LANE) * LANE
        packed.append(jnp.pad(wc.astype(BF16), ((0, 0), (0, n_pad - n))))
        segs.append((start, n_pad, rope, scale, tuple((n, dt, tr) for dt, tr in outs)))
        for dt, tr in outs:
            if tr:
                out_shapes.append(jax.ShapeDtypeStruct((n, m), dt))
                out_specs.append(pl.BlockSpec((n, tm), lambda i: (0, i)))
            else:
                out_shapes.append(jax.ShapeDtypeStruct((m, n), dt))
                out_specs.append(pl.BlockSpec((tm, n), lambda i: (i, 0)))
        start += n_pad
    wb = jnp.concatenate(packed, axis=1)
    cos, sin = rope_tables(pos_rows)
    row_block = lambda width: pl.BlockSpec((tm, width), lambda i: (i, 0))
    return pl.pallas_call(
        functools.partial(_proj_kernel, segments=tuple(segs)),
        grid=(m // tm,),
        in_specs=[row_block(kdim), _resident((1, kdim)), row_block(LANE), row_block(LANE), _resident(wb.shape)],
        out_specs=out_specs,
        out_shape=out_shapes,
        compiler_params=pltpu.CompilerParams(dimension_semantics=("parallel",), vmem_limit_bytes=VMEM_LIMIT),
        name="project",
    )(x, g.reshape(1, kdim).astype(F32), cos, sin, wb)


def _split_cols(w, widths):
    cols = np.cumsum((0,) + tuple(widths))
    return [w[:, cols[j]:cols[j + 1]] for j in range(len(widths))]


def _softmax_rows(s):
    m = jnp.max(s, axis=-1, keepdims=True)
    e = jnp.exp(s - jnp.where(m == -jnp.inf, 0.0, m))
    den = jnp.sum(e, axis=-1, keepdims=True)
    return e / jnp.where(den > 0, den, 1.0)


def _flash_step(carry, q, kt_tile, v_tile, sel, rep):
    m, l, acc = carry
    rows, tk = q.shape[0], kt_tile.shape[1]
    s = jnp.dot(q, kt_tile, preferred_element_type=F32)
    s = jnp.where(sel, s.reshape(rep, Q_BLOCK, tk), -jnp.inf).reshape(rows, tk)
    m_new = jnp.maximum(m, jnp.max(s, axis=1, keepdims=True))
    m_safe = jnp.where(m_new == -jnp.inf, 0.0, m_new)
    alpha = jnp.exp(m - m_safe)
    p = jnp.exp(s - m_safe)
    l = l * alpha + jnp.sum(p, axis=1, keepdims=True)
    pv = jnp.dot(p.astype(BF16), v_tile, preferred_element_type=F32)
    return m_new, l, acc * alpha + pv


def _flash_init(rows, hd):
    return (jnp.full((rows, 1), -jnp.inf, F32), jnp.zeros((rows, 1), F32), jnp.zeros((rows, hd), F32))


def _flash_finish(carry):
    _, l, acc = carry
    return acc / jnp.where(l > 0, l, 1.0)


def _stack_heads(x, h0, rep, hd):
    return jnp.concatenate([x[:, (h0 + r) * hd:(h0 + r + 1) * hd] for r in range(rep)], axis=0)


def _sortable_key(score):
    bits = pltpu.bitcast(score, I32)
    key = jnp.where(bits < 0, bits ^ jnp.int32(0x7FFFFFFF), bits)
    return jnp.where(score == 0.0, 0, key)


def _count_lanes(pred_fn, key_ref, nkt, tk):
    rows = key_ref.shape[0]

    def body(kt, cnt):
        keys = key_ref[:, pl.ds(pl.multiple_of(kt * tk, tk), tk)]
        for c in range(tk // LANE):
            hit = pred_fn(keys[:, c * LANE:(c + 1) * LANE], kt * tk + c * LANE)
            cnt = cnt + jnp.where(hit, 1, 0)
        return cnt

    cnt = lax.fori_loop(0, nkt, body, jnp.zeros((rows, LANE), I32))
    return jnp.sum(cnt, axis=1, keepdims=True)


def _top_k_mask_params(key_ref, nkt, tk, topk, n_valid):
    rows = key_ref.shape[0]
    lane_iota = lax.broadcasted_iota(I32, (rows, LANE), 1)

    def bit_body(it, thr):
        cand = thr + jnp.left_shift(jnp.int32(1), 31 - it)
        cnt = _count_lanes(lambda keys, c0: keys >= cand, key_ref, nkt, tk)
        return jnp.where(cnt >= topk, cand, thr)

    thr = lax.fori_loop(0, 32, bit_body, jnp.full((rows, 1), INT_MIN, I32))
    many = n_valid > topk
    thr = jnp.where(many, thr, INT_MIN + 1)
    c_gt = _count_lanes(lambda keys, c0: keys > thr, key_ref, nkt, tk)
    c_eq = _count_lanes(lambda keys, c0: keys == thr, key_ref, nkt, tk)
    need = topk - c_gt
    excess = many & (c_eq > need)
    n_col_bits = max(1, (key_ref.shape[1] - 1).bit_length())

    def tie_search():
        def jbody(it, last):
            cand = last + jnp.left_shift(jnp.int32(1), n_col_bits - 1 - it)
            cnt = _count_lanes(lambda keys, c0: (keys == thr) & (c0 + lane_iota < cand), key_ref, nkt, tk)
            return jnp.where(cnt <= need - 1, cand, last)

        return lax.fori_loop(0, n_col_bits, jbody, jnp.zeros((rows, 1), I32))

    any_excess = jnp.max(jnp.where(excess, 1, 0)) > 0
    last_tie = lax.cond(any_excess, tie_search, lambda: jnp.zeros((rows, 1), I32))
    return thr, jnp.where(excess, last_tie, jnp.int32(2 ** 30))


SEARCH_DIGIT_BITS = 4


def _top_k_mask_params_small(keys, col, topk, n_valid):
    rows = keys.shape[0]
    count = lambda hit: jnp.sum(jnp.where(hit, 1, 0), axis=1, keepdims=True)
    thr = jnp.full((rows, 1), INT_MIN, I32)
    for shift in range(32 - SEARCH_DIGIT_BITS, -1, -SEARCH_DIGIT_BITS):
        digit = jnp.zeros((rows, 1), I32)
        for d in range(1, 2 ** SEARCH_DIGIT_BITS):
            step = np.int64(d << shift).astype(np.int32)
            digit = digit + jnp.where(count(keys >= thr + jnp.int32(step)) >= topk, 1, 0)
        thr = thr + jnp.left_shift(digit, shift)
    many = n_valid > topk
    thr = jnp.where(many, thr, INT_MIN + 1)
    need = topk - count(keys > thr)
    tie = keys == thr
    excess = many & (count(tie) > need)
    n_col_bits = max(1, (keys.shape[1] - 1).bit_length())

    def tie_search():
        last = jnp.zeros((rows, 1), I32)
        for it in range(n_col_bits):
            cand = last + (1 << (n_col_bits - 1 - it))
            last = jnp.where(count(tie & (col < cand)) <= need - 1, cand, last)
        return last

    any_excess = jnp.max(jnp.where(excess, 1, 0)) > 0
    last_tie = lax.cond(any_excess, tie_search, lambda: jnp.zeros((rows, 1), I32))
    return thr, jnp.where(excess, last_tie, jnp.int32(2 ** 30))


def _block_cover(n_cmp_rows):
    c0 = lax.broadcasted_iota(I32, (n_cmp_rows, LANE), 0) * CMP_STRIDE
    s0 = lax.broadcasted_iota(I32, (n_cmp_rows, LANE), 1) * SLC_BLOCK
    return jnp.where(c0 < s0 + SLC_BLOCK, jnp.where(c0 + CMP_BLOCK > s0, 1.0, 0.0), 0.0).astype(BF16)


def _block_importance(p_sum, cover, row_pos):
    p_hi = p_sum.astype(BF16)
    p_lo = (p_sum - p_hi.astype(F32)).astype(BF16)
    imp = jnp.dot(p_hi, cover, preferred_element_type=F32) + jnp.dot(p_lo, cover, preferred_element_type=F32)
    jb = lax.broadcasted_iota(I32, imp.shape, 1)
    cur = row_pos // SLC_BLOCK
    forced = (jb == 0) | (jb == cur) | (jb == cur - 1)
    imp = jnp.where(forced, jnp.inf, imp)
    return jnp.where(jb <= cur, imp, -jnp.inf)


def _dsa_prompt_kernel(iq_ref, iw_ref, ikt_ref, q_ref, kt_ref, v_ref, o_ref, key_ref, *, topk, tk,
                       n_idx_heads, n_groups, rep):
    i = pl.program_id(1)
    t0 = i * Q_BLOCK
    nkt = (t0 + Q_BLOCK + tk - 1) // tk
    row = t0 + lax.broadcasted_iota(I32, (Q_BLOCK, 1), 0)
    hd = kt_ref.shape[0] // n_groups
    di = ikt_ref.shape[0]

    iw = iw_ref[0]
    iq = iq_ref[0]
    iq_heads = [iq[:, h * di:(h + 1) * di] for h in range(n_idx_heads)]

    def score_body(kt, _):
        off = pl.multiple_of(kt * tk, tk)
        acc = jnp.zeros((Q_BLOCK, tk), F32)
        for h in range(n_idx_heads):
            r = jnp.dot(iq_heads[h], ikt_ref[:, pl.ds(off, tk)], preferred_element_type=F32)
            acc = acc + jnp.maximum(r, 0.0) * iw[:, h:h + 1]
        col = off + lax.broadcasted_iota(I32, (Q_BLOCK, tk), 1)
        key_ref[:, pl.ds(off, tk)] = jnp.where(col <= row, _sortable_key(acc), INT_MIN)
        return 0

    lax.fori_loop(0, nkt, score_body, 0)

    thr, last_tie = _top_k_mask_params(key_ref, nkt, tk, topk, row + 1)

    q = q_ref[0]
    q_groups = [_stack_heads(q, g * rep, rep, hd) for g in range(n_groups)]
    rows_g = rep * Q_BLOCK

    def attn_body(kt, carry):
        off = pl.multiple_of(kt * tk, tk)
        keys = key_ref[:, pl.ds(off, tk)]
        col = off + lax.broadcasted_iota(I32, (Q_BLOCK, tk), 1)
        sel = ((keys > thr) | ((keys == thr) & (col <= last_tie)))[None]
        v_t = v_ref[0, pl.ds(off, tk), :]
        return tuple(
            _flash_step(carry[g], q_groups[g], kt_ref[g * hd:(g + 1) * hd, pl.ds(off, tk)],
                        v_t[:, g * hd:(g + 1) * hd], sel, rep)
            for g in range(n_groups))

    fin = lax.fori_loop(0, nkt, attn_body, tuple(_flash_init(rows_g, hd) for _ in range(n_groups)))
    for g in range(n_groups):
        o = _flash_finish(fin[g])
        for r in range(rep):
            h = g * rep + r
            o_ref[0, :, h * hd:(h + 1) * hd] = o[r * Q_BLOCK:(r + 1) * Q_BLOCK]


def dsa_prompt_attention(q, kt, v, iq, ikt, iw, topk, n_groups, n_idx_heads, tk=ATTN_COLS):
    b, s, hd_all = q.shape
    gd, di = kt.shape[0], ikt.shape[0]
    d = gd // n_groups
    tk = min(tk, s)
    kern = functools.partial(_dsa_prompt_kernel, topk=topk, tk=tk, n_idx_heads=n_idx_heads, n_groups=n_groups,
                             rep=hd_all // gd)
    q_block = lambda width: pl.BlockSpec((1, Q_BLOCK, width), lambda bb, i: (bb, i, 0))
    return pl.pallas_call(
        kern,
        grid=(b, s // Q_BLOCK),
        in_specs=[
            q_block(n_idx_heads * di), q_block(n_idx_heads),
            pl.BlockSpec((di, s), lambda bb, i: (0, bb)),
            q_block(hd_all),
            pl.BlockSpec((gd, s), lambda bb, i: (0, bb)),
            pl.BlockSpec((1, s, gd), lambda bb, i: (bb, 0, 0)),
        ],
        out_specs=q_block(hd_all),
        out_shape=jax.ShapeDtypeStruct((b, s, hd_all), F32),
        scratch_shapes=[pltpu.VMEM((Q_BLOCK, s), I32)],
        compiler_params=pltpu.CompilerParams(
            dimension_semantics=("parallel", "arbitrary"), vmem_limit_bytes=VMEM_LIMIT),
        name="dsa_prompt_attention",
    )(iq, iw, ikt, q, kt, v)


def _top_blocks(imp_ref, n_live, n_sel):
    imp_t = imp_ref[...]
    j_iota = lax.broadcasted_iota(I32, imp_t.shape, 0)

    def body(k, rank):
        row_k = imp_ref[pl.ds(k, 1), :]
        earlier = jnp.where(j_iota > k, 1.0, 0.0)
        return rank + jnp.where(row_k > imp_t, 1.0, jnp.where(row_k == imp_t, earlier, 0.0))

    rank = lax.fori_loop(0, n_live, body, jnp.zeros(imp_t.shape, F32))
    return jnp.where(rank < n_sel, 1.0, 0.0)


def _nsa_prompt_kernel(q_ref, gate_ref, kct_ref, vc_ref, kst_ref, vs_ref, kwt_ref, vw_ref, o_ref, imp_ref, *,
                       tk, n_cmp, n_blk, n_groups, rep, win_len):
    i = pl.program_id(1)
    t0 = i * Q_BLOCK
    nkt = (t0 + Q_BLOCK + tk - 1) // tk
    hd = kst_ref.shape[0] // n_groups
    rows = rep * Q_BLOCK
    nc_pad = kct_ref.shape[-1]
    row = t0 + lax.broadcasted_iota(I32, (Q_BLOCK, 1), 0)
    gates = jax.nn.sigmoid(gate_ref[0])
    n_heads = n_groups * rep
    q_all = q_ref[0]

    cover = _block_cover(nc_pad)
    c_idx = lax.broadcasted_iota(I32, (Q_BLOCK, nc_pad), 1)
    c_ok = ((c_idx < n_cmp) & (c_idx * CMP_STRIDE + (CMP_BLOCK - 1) <= row))[None]
    w_start = pl.multiple_of(jnp.maximum(t0 - WINDOW, 0), Q_BLOCK)
    w_col = w_start + lax.broadcasted_iota(I32, (Q_BLOCK, win_len), 1)
    w_d = row - w_col
    w_ok = ((w_d >= 0) & (w_d < WINDOW))[None]

    for g in range(n_groups):
        q = _stack_heads(q_all, g * rep, rep, hd)
        lo, hi = g * hd, (g + 1) * hd

        s = jnp.dot(q, kct_ref[0, lo:hi, :], preferred_element_type=F32)
        s = jnp.where(c_ok, s.reshape(rep, Q_BLOCK, nc_pad), -jnp.inf)
        p = _softmax_rows(s)
        o_cmp = jnp.dot(p.reshape(rows, nc_pad).astype(BF16), vc_ref[0, :, lo:hi], preferred_element_type=F32)

        imp = _block_importance(jnp.sum(p, axis=0), cover, row)
        imp_ref[...] = imp.T
        n_live = jnp.minimum((t0 + Q_BLOCK - 1) // SLC_BLOCK + 1, n_blk)
        sel_blocks = _top_blocks(imp_ref, n_live, min(N_SLC, n_blk)).T.astype(BF16)

        def slc_body(kt, carry):
            off = pl.multiple_of(kt * tk, tk)
            col = off + lax.broadcasted_iota(I32, (Q_BLOCK, tk), 1)
            blk_of_col = (off + lax.broadcasted_iota(I32, (LANE, tk), 1)) // SLC_BLOCK
            expand = jnp.where(lax.broadcasted_iota(I32, (LANE, tk), 0) == blk_of_col, 1.0, 0.0).astype(BF16)
            hit = jnp.dot(sel_blocks, expand, preferred_element_type=F32)
            sel = (jnp.where(col <= row, hit, 0.0) > 0.5)[None]
            return _flash_step(carry, q, kst_ref[lo:hi, pl.ds(off, tk)], vs_ref[0, pl.ds(off, tk), lo:hi], sel, rep)

        o_slc = _flash_finish(lax.fori_loop(0, nkt, slc_body, _flash_init(rows, hd)))

        s = jnp.dot(q, kwt_ref[lo:hi, pl.ds(w_start, win_len)], preferred_element_type=F32)
        s = jnp.where(w_ok, s.reshape(rep, Q_BLOCK, win_len), -jnp.inf)
        p = _softmax_rows(s).reshape(rows, win_len)
        o_win = jnp.dot(p.astype(BF16), vw_ref[0, pl.ds(w_start, win_len), lo:hi], preferred_element_type=F32)

        for r in range(rep):
            h = g * rep + r
            rs = slice(r * Q_BLOCK, (r + 1) * Q_BLOCK)
            o_ref[0, :, h * hd:(h + 1) * hd] = (
                gates[:, h:h + 1] * o_cmp[rs] + gates[:, n_heads + h:n_heads + h + 1] * o_slc[rs]
                + gates[:, 2 * n_heads + h:2 * n_heads + h + 1] * o_win[rs])


def nsa_prompt_attention(q, gate_logits, kct, vc, n_cmp, kst, vs, kwt, vw, n_groups, tk=ATTN_COLS):
    b, s, hd_all = q.shape
    gd, nc_pad = kct.shape[1], kct.shape[2]
    n_blk = -(-s // SLC_BLOCK)
    assert n_blk <= LANE and s % Q_BLOCK == 0
    tk = min(tk, s)
    win_len = min(WINDOW + Q_BLOCK, s)
    kern = functools.partial(_nsa_prompt_kernel, tk=tk, n_cmp=n_cmp, n_blk=n_blk, n_groups=n_groups,
                             rep=hd_all // gd, win_len=win_len)
    per_b = lambda *blk: pl.BlockSpec((1,) + blk, lambda bb, i: (bb,) + (0,) * len(blk))
    keys_t = pl.BlockSpec((gd, s), lambda bb, i: (0, bb))
    q_block = lambda width: pl.BlockSpec((1, Q_BLOCK, width), lambda bb, i: (bb, i, 0))
    return pl.pallas_call(
        kern,
        grid=(b, s // Q_BLOCK),
        in_specs=[q_block(hd_all), q_block(gate_logits.shape[-1]), per_b(gd, nc_pad), per_b(nc_pad, gd),
                  keys_t, per_b(s, gd), keys_t, per_b(s, gd)],
        out_specs=q_block(hd_all),
        out_shape=jax.ShapeDtypeStruct((b, s, hd_all), F32),
        scratch_shapes=[pltpu.VMEM((LANE, Q_BLOCK), F32)],
        compiler_params=pltpu.CompilerParams(
            dimension_semantics=("parallel", "arbitrary"), vmem_limit_bytes=VMEM_LIMIT),
        name="nsa_prompt_attention",
    )(q, gate_logits, kct, vc, kst, vs, kwt, vw)


def _compress_finish(a, pew_ref, w2_ref, n_groups, hd):
    half = n_groups * hd
    first, second = a[:, :half], a[:, half:]
    second = jnp.concatenate([second[1:], jnp.zeros((1, half), F32)], axis=0)
    out = []
    for g in range(n_groups):
        hcol = first[:, g * hd:(g + 1) * hd] + second[:, g * hd:(g + 1) * hd] + pew_ref[...]
        out.append(jnp.dot(jax.nn.silu(hcol).astype(BF16), w2_ref[...], preferred_element_type=F32))
    return out


def _compress_kernel(c_ref, wc_ref, pew_ref, w2_ref, o_ref, *, n_groups, hd, transposed):
    a = jnp.dot(c_ref[0].astype(BF16), wc_ref[...], preferred_element_type=F32)
    res = jnp.concatenate(_compress_finish(a, pew_ref, w2_ref, n_groups, hd), axis=1)
    o_ref[0] = (res.T if transposed else res).astype(BF16)


def _compress_chunk_weights(w1, n_groups, hd):
    wr = w1.reshape(2, CMP_STRIDE, hd, hd)
    eye = jnp.eye(n_groups, dtype=w1.dtype)
    wc = jnp.einsum('hrdo,gk->rgdhko', wr, eye)
    return wc.reshape(CMP_STRIDE * n_groups * hd, 2 * n_groups * hd)


def _compress_weights(pe, w1, w2, n_groups, hd):
    return _compress_chunk_weights(w1, n_groups, hd).astype(BF16), mm(pe.reshape(1, -1), w1), w2.astype(BF16)


def compress_rows(rows, cmp_w, n_groups, transposed):
    b, l, gd = rows.shape
    hd = gd // n_groups
    n_chunks = l // CMP_STRIDE
    cw = CMP_STRIDE * gd
    wc, pew, w2 = _compress_weights(*cmp_w, n_groups, hd)
    out_blk = (gd, n_chunks) if transposed else (n_chunks, gd)
    return pl.pallas_call(
        functools.partial(_compress_kernel, n_groups=n_groups, hd=hd, transposed=transposed),
        grid=(b,),
        in_specs=[pl.BlockSpec((1, n_chunks, cw), lambda bb: (bb, 0, 0)), _resident(wc.shape), _resident(pew.shape),
                  _resident(w2.shape)],
        out_specs=pl.BlockSpec((1,) + out_blk, lambda bb: (bb, 0, 0)),
        out_shape=jax.ShapeDtypeStruct((b,) + out_blk, BF16),
        compiler_params=pltpu.CompilerParams(dimension_semantics=("parallel",), vmem_limit_bytes=VMEM_LIMIT),
        name="compress_rows",
    )(rows.reshape(b, n_chunks, cw), wc, pew, w2)


def _attend_cols(q, kt, vt, sel, rep):
    n = kt.shape[1]
    s = jnp.dot(q, kt, preferred_element_type=F32)
    s = jnp.where(sel, s.reshape(rep, TQ, n), -jnp.inf)
    p = _softmax_rows(s).reshape(rep * TQ, n)
    return p, lax.dot_general(p.astype(BF16), vt, (((1,), (1,)), ((), ())), preferred_element_type=F32)


def _head_rows(x):
    b, t, h, d = x.shape
    x = jnp.pad(x.astype(BF16), ((0, 0), (0, TQ - t), (0, 0), (0, 0)))
    return x.transpose(0, 2, 1, 3).reshape(b, h * TQ, d)


def _new_cols(x, width):
    b, t = x.shape[:2]
    return jnp.pad(x.reshape(b, t, -1).astype(BF16).transpose(0, 2, 1), ((0, 0), (0, 0), (0, width - t)))


def _cols_view(pool):
    lead, rows = pool.shape[:2], pool.shape[2]
    perm = (0, 1) + tuple(range(3, pool.ndim)) + (2,)
    return pool.transpose(perm).reshape(lead + (-1, rows))


def _page_specs(block, layer, n_pages, pp):
    def spec(j):
        return pl.BlockSpec((1, 1) + block,
                            lambda bb, p, pt: (layer, pt[bb * n_pages + p * pp + j]) + (0,) * len(block))
    return [spec(j) for j in range(pp)]


def _per_batch(*blk):
    return pl.BlockSpec((1,) + blk, lambda bb, p, pt: (bb,) + (0,) * len(blk))


def _whole(*blk):
    return pl.BlockSpec(blk, lambda bb, p, pt: (0,) * len(blk))


def _dsa_sample_kernel(pt_ref, iq_ref, iw_ref, q_ref, ikn_ref, kn_ref, vn_ref, *rest, topk, past, n_new,
                       n_idx_heads, n_groups, rep, pp):
    idx_pages = rest[:pp]
    k_pages = rest[pp:2 * pp]
    v_pages = rest[2 * pp:3 * pp]
    o_ref, key_ref, kst, vst = rest[3 * pp:]
    p_step = pl.program_id(1)
    hd = q_ref.shape[-1]
    page = idx_pages[0].shape[-1]
    n_tiles = past // page + 1
    iw = iw_ref[0]
    tpos = past + jnp.minimum(lax.broadcasted_iota(I32, (TQ, 1), 0), n_new - 1)

    def scores(ikt):
        r = jnp.dot(iq_ref[0], ikt, preferred_element_type=F32)
        r = jnp.maximum(r, 0.0).reshape(n_idx_heads, TQ, ikt.shape[1]) * iw
        return jnp.sum(r, axis=0)

    for j in range(pp):
        off = pl.multiple_of((p_step * pp + j) * page, page)
        key_ref[:, pl.ds(off, page)] = _sortable_key(scores(idx_pages[j][0, 0].astype(BF16)))
        kst[:, pl.ds(off, page)] = k_pages[j][0, 0].astype(BF16)
        vst[:, pl.ds(off, page)] = v_pages[j][0, 0].astype(BF16)

    @pl.when(p_step == pl.num_programs(1) - 1)
    def _():
        c = lax.broadcasted_iota(I32, (TQ, page), 1)
        fresh_ok = (c < n_new) & (past + c <= tpos)
        key_ref[:, past:past + page] = jnp.where(fresh_ok, _sortable_key(scores(ikn_ref[0])), INT_MIN)
        kst[:, past:past + page] = kn_ref[0]
        vst[:, past:past + page] = vn_ref[0]
        keys = key_ref[...]
        col = lax.broadcasted_iota(I32, keys.shape, 1)
        thr, last_tie = _top_k_mask_params_small(keys, col, topk, tpos + 1)
        sel = ((keys > thr) | ((keys == thr) & (col <= last_tie)))[None]
        for g in range(n_groups):
            gs = slice(g * hd, (g + 1) * hd)
            _, o = _attend_cols(q_ref[0, g * rep * TQ:(g + 1) * rep * TQ], kst[gs, :], vst[gs, :], sel, rep)
            for r in range(rep):
                h = g * rep + r
                o_ref[0, :, h * hd:(h + 1) * hd] = o[r * TQ:(r + 1) * TQ]


def dsa_sample_attention(q, k, v, iq, ik, iw, pools_k, pools_v, pools_idx, layer, page_table, topk):
    b, t, h, d = q.shape
    g = k.shape[2]
    hi, di = iq.shape[2], iq.shape[3]
    n_pages = page_table.shape[1]
    page = pools_k.shape[2]
    past = n_pages * page
    pp = min(PAGES_PER_STEP, n_pages)
    assert n_pages % pp == 0 and t <= TQ
    iw_h = jnp.pad(iw.astype(F32), ((0, 0), (0, TQ - t), (0, 0))).transpose(0, 2, 1)[..., None]
    kern = functools.partial(_dsa_sample_kernel, topk=topk, past=past, n_new=t, n_idx_heads=hi, n_groups=g,
                             rep=h // g, pp=pp)
    o = pl.pallas_call(
        kern,
        grid_spec=pltpu.PrefetchScalarGridSpec(
            num_scalar_prefetch=1,
            grid=(b, n_pages // pp),
            in_specs=[_per_batch(hi * TQ, di), _per_batch(hi, TQ, 1), _per_batch(h * TQ, d), _per_batch(di, page),
                      _per_batch(g * d, page), _per_batch(g * d, page)]
                     + _page_specs((di, page), layer, n_pages, pp)
                     + _page_specs((g * d, page), layer, n_pages, pp)
                     + _page_specs((g * d, page), layer, n_pages, pp),
            out_specs=_per_batch(TQ, h * d),
            scratch_shapes=[pltpu.VMEM((TQ, past + page), I32), pltpu.VMEM((g * d, past + page), BF16),
                            pltpu.VMEM((g * d, past + page), BF16)],
        ),
        out_shape=jax.ShapeDtypeStruct((b, TQ, h * d), F32),
        compiler_params=pltpu.CompilerParams(
            dimension_semantics=("parallel", "arbitrary"), vmem_limit_bytes=VMEM_LIMIT),
        name="dsa_sample_attention",
    )(page_table.reshape(-1).astype(I32), _head_rows(iq), iw_h, _head_rows(q), _new_cols(ik, page),
      _new_cols(k, page), _new_cols(v, page),
      *([_cols_view(pools_idx)] * pp), *([_cols_view(pools_k)] * pp), *([_cols_view(pools_v)] * pp))
    return o[:, :t]


def _top_blocks_rows(imp, n_blk, n_sel):
    j_iota = lax.broadcasted_iota(I32, imp.shape, 1)
    rank = jnp.zeros(imp.shape, F32)
    for k in range(n_blk):
        col_k = imp[:, k:k + 1]
        earlier = jnp.where(j_iota > k, 1.0, 0.0)
        rank = rank + jnp.where(col_k > imp, 1.0, jnp.where(col_k == imp, earlier, 0.0))
    return jnp.where((rank < n_sel) & (j_iota < n_blk), 1.0, 0.0)


def _nsa_sample_kernel(pt_ref, q_ref, gate_ref, ksn_ref, vsn_ref, kwn_ref, vwn_ref, wink_ref, winv_ref,
                       wck_ref, wcv_ref, pewk_ref, pewv_ref, w2k_ref, w2v_ref, *rest, past, n_new, n_groups, rep, pp):
    kc_pages = rest[:pp]
    vc_pages = rest[pp:2 * pp]
    ks_pages = rest[2 * pp:3 * pp]
    vs_pages = rest[3 * pp:4 * pp]
    o_ref, kcs, vcs, kst, vst, kwt, vwt = rest[4 * pp:]
    p_step = pl.program_id(1)
    hd = q_ref.shape[-1]
    gd = n_groups * hd
    page = ks_pages[0].shape[-1]
    n_chunks = kcs.shape[0] // CMP_STRIDE
    n_cmp = n_chunks - CMP_BLOCK // CMP_STRIDE + 1
    n_blk = -(-(past + n_new) // SLC_BLOCK)
    w_buf = wink_ref.shape[-1]
    n_heads = n_groups * rep
    rows = rep * TQ

    for j in range(pp):
        off = pl.multiple_of((p_step * pp + j) * page, page)
        kcs[pl.ds(off, page), :] = kc_pages[j][0, 0].T
        vcs[pl.ds(off, page), :] = vc_pages[j][0, 0].T
        kst[:, pl.ds(off, page)] = ks_pages[j][0, 0].astype(BF16)
        vst[:, pl.ds(off, page)] = vs_pages[j][0, 0].astype(BF16)

    @pl.when(p_step == pl.num_programs(1) - 1)
    def _():
        kst[:, past:past + page] = ksn_ref[0]
        vst[:, past:past + page] = vsn_ref[0]
        kwt[:, 0:w_buf] = wink_ref[0, 0].astype(BF16)
        vwt[:, 0:w_buf] = winv_ref[0, 0].astype(BF16)
        kwt[:, w_buf:w_buf + LANE] = kwn_ref[0]
        vwt[:, w_buf:w_buf + LANE] = vwn_ref[0]

        tpos = past + jnp.minimum(lax.broadcasted_iota(I32, (TQ, 1), 0), n_new - 1)
        gates = jax.nn.sigmoid(gate_ref[0])

        def compress(rows_ref, wc_ref, pew_ref, w2_ref):
            a = jnp.zeros((n_chunks, 2 * gd), F32)
            for r in range(CMP_STRIDE):
                xr = rows_ref[pl.ds(r, n_chunks, stride=CMP_STRIDE), :].astype(BF16)
                a = a + jnp.dot(xr, wc_ref[r * gd:(r + 1) * gd, :], preferred_element_type=F32)
            out = _compress_finish(a, pew_ref, w2_ref, n_groups, hd)
            return [o.astype(BF16) for o in out]

        kcmp = compress(kcs, wck_ref, pewk_ref, w2k_ref)
        vcmp = compress(vcs, wcv_ref, pewv_ref, w2v_ref)

        c_idx = lax.broadcasted_iota(I32, (TQ, n_chunks), 1)
        c_ok = ((c_idx < n_cmp) & (c_idx * CMP_STRIDE + (CMP_BLOCK - 1) <= tpos))[None]
        cover = _block_cover(n_chunks)
        n_slc = kst.shape[1]
        col = lax.broadcasted_iota(I32, (TQ, n_slc), 1)
        blk_of_col = lax.broadcasted_iota(I32, (LANE, n_slc), 1) // SLC_BLOCK
        expand = jnp.where(lax.broadcasted_iota(I32, (LANE, n_slc), 0) == blk_of_col, 1.0, 0.0).astype(BF16)
        n_win = kwt.shape[1]
        w_c = lax.broadcasted_iota(I32, (TQ, n_win), 1)
        w_pos = past - w_buf + w_c
        w_d = tpos - w_pos
        w_ok = ((w_c < w_buf + n_new) & (w_pos >= 0) & (w_d >= 0) & (w_d < WINDOW))[None]

        for g in range(n_groups):
            q = q_ref[0, g * rows:(g + 1) * rows]
            gs = slice(g * hd, (g + 1) * hd)
            s = lax.dot_general(q, kcmp[g], (((1,), (1,)), ((), ())), preferred_element_type=F32)
            p_cmp = _softmax_rows(jnp.where(c_ok, s.reshape(rep, TQ, n_chunks), -jnp.inf))
            o_cmp = jnp.dot(p_cmp.reshape(rows, n_chunks).astype(BF16), vcmp[g], preferred_element_type=F32)
            imp = _block_importance(jnp.sum(p_cmp, axis=0), cover, tpos)
            sel_blocks = _top_blocks_rows(imp, n_blk, min(N_SLC, n_blk)).astype(BF16)
            hit = jnp.dot(sel_blocks, expand, preferred_element_type=F32)
            sel = (jnp.where(col <= tpos, hit, 0.0) > 0.5)[None]
            _, o_slc = _attend_cols(q, kst[gs, :], vst[gs, :], sel, rep)
            _, o_win = _attend_cols(q, kwt[gs, :], vwt[gs, :], w_ok, rep)
            for r in range(rep):
                h = g * rep + r
                rs = slice(r * TQ, (r + 1) * TQ)
                o_ref[0, :, h * hd:(h + 1) * hd] = (
                    gates[:, h:h + 1] * o_cmp[rs] + gates[:, n_heads + h:n_heads + h + 1] * o_slc[rs]
                    + gates[:, 2 * n_heads + h:2 * n_heads + h + 1] * o_win[rs])


def nsa_sample_attention(q, gate_logits, ks, vs, kw, vw, cmp_k, cmp_v, pools_kc, pools_vc, pools_ks, pools_vs,
                         wins_k, wins_v, layer, page_table):
    b, t, h, d = q.shape
    g = ks.shape[2]
    gd = g * d
    n_pages = page_table.shape[1]
    page = pools_ks.shape[2]
    past = n_pages * page
    pp = min(PAGES_PER_STEP, n_pages)
    assert n_pages % pp == 0 and t <= TQ and (past + t) // CMP_STRIDE * CMP_STRIDE == past
    cw = CMP_STRIDE * gd
    w_buf = wins_k.shape[2]

    wck, pewk, w2k = _compress_weights(*cmp_k, g, d)
    wcv, pewv, w2v = _compress_weights(*cmp_v, g, d)
    kern = functools.partial(_nsa_sample_kernel, past=past, n_new=t, n_groups=g, rep=h // g, pp=pp)
    win_spec = pl.BlockSpec((1, 1, gd, w_buf), lambda bb, p, pt: (layer, bb, 0, 0))
    o = pl.pallas_call(
        kern,
        grid_spec=pltpu.PrefetchScalarGridSpec(
            num_scalar_prefetch=1,
            grid=(b, n_pages // pp),
            in_specs=[_per_batch(h * TQ, d), _per_batch(TQ, 3 * h), _per_batch(gd, page), _per_batch(gd, page),
                      _per_batch(gd, LANE), _per_batch(gd, LANE), win_spec, win_spec,
                      _whole(cw, 2 * gd), _whole(cw, 2 * gd), _whole(1, d), _whole(1, d), _whole(d, d), _whole(d, d)]
                     + _page_specs((gd, page), layer, n_pages, pp) + _page_specs((gd, page), layer, n_pages, pp)
                     + _page_specs((gd, page), layer, n_pages, pp) + _page_specs((gd, page), layer, n_pages, pp),
            out_specs=_per_batch(TQ, h * d),
            scratch_shapes=[pltpu.VMEM((past, gd), F32), pltpu.VMEM((past, gd), F32),
                            pltpu.VMEM((gd, past + page), BF16), pltpu.VMEM((gd, past + page), BF16),
                            pltpu.VMEM((gd, w_buf + LANE), BF16), pltpu.VMEM((gd, w_buf + LANE), BF16)],
        ),
        out_shape=jax.ShapeDtypeStruct((b, TQ, h * d), F32),
        compiler_params=pltpu.CompilerParams(
            dimension_semantics=("parallel", "arbitrary"), vmem_limit_bytes=VMEM_LIMIT),
        name="nsa_sample_attention",
    )(page_table.reshape(-1).astype(I32), _head_rows(q),
      jnp.pad(gate_logits.astype(F32), ((0, 0), (0, TQ - t), (0, 0))),
      _new_cols(ks, page), _new_cols(vs, page), _new_cols(kw, LANE), _new_cols(vw, LANE),
      _cols_view(wins_k), _cols_view(wins_v), wck, wcv, pewk, pewv, w2k, w2v,
      *([_cols_view(pools_kc)] * pp), *([_cols_view(pools_vc)] * pp),
      *([_cols_view(pools_ks)] * pp), *([_cols_view(pools_vs)] * pp))
    return o[:, :t]


def last_rows(a, n):
    t = a.shape[1]
    if t >= n:
        return a[:, t - n:]
    return jnp.pad(a, ((0, 0), (n - t, 0)) + ((0, 0),) * (a.ndim - 2))


def mem_kv(mem, g, w_kv):
    b, m, _ = mem.shape
    k, v = jnp.split(mm(mem, w_kv, g), 2, axis=-1)
    return k.reshape(b, m, X_HEADS, X_HEAD_DIM), v.reshape(b, m, X_HEADS, X_HEAD_DIM)


def cross_attn(x, g, mk, mv, w_q, w_o):
    b, t, _ = x.shape
    q = mm(x, w_q, g).reshape(b, t, X_HEADS, X_HEAD_DIM)
    s = jnp.einsum('bthd,bmhd->bhtm', q, mk).astype(F32) * (X_HEAD_DIM ** -0.5)
    p = jax.nn.softmax(s, axis=-1).astype(x.dtype)
    return mm(jnp.einsum('bhtm,bmhd->bthd', p, mv).reshape(b, t, -1), w_o, res=x)


A_WIDTHS = (A_HEADS * HEAD_DIM, A_KV_HEADS * HEAD_DIM, A_KV_HEADS * HEAD_DIM, IDX_HEADS * IDX_DIM, IDX_DIM, IDX_HEADS)


def dsa_project(x, g, w_in, pos, keys_transposed):
    b, t, d = x.shape
    keys = [(F32, False), (BF16, keys_transposed)]
    segments = [(True, HEAD_DIM ** -0.5, [(BF16, False)]), (True, None, keys), (False, None, [(F32, False), (BF16, False)]),
                (True, None, [(BF16, False)]), (True, None, keys),
                (False, IDX_HEADS ** -0.5 * IDX_DIM ** -0.5, [(F32, False)])]
    return project(x.reshape(b * t, d), g, _split_cols(w_in, A_WIDTHS), segments, jnp.tile(pos, b))


def dsa_prompt(x, g, w_in, w_out):
    b, s, _ = x.shape
    q, k, kt, v, vb, iq, ik, ikt, iw = dsa_project(x, g, w_in, jnp.arange(s), True)
    o = dsa_prompt_attention(q.reshape(b, s, -1), kt, vb.reshape(b, s, -1), iq.reshape(b, s, -1), ikt,
                             iw.reshape(b, s, -1), min(TOPK_MAX, s // 4), A_KV_HEADS, IDX_HEADS)
    state = (k.reshape(b, s, A_KV_HEADS, HEAD_DIM), v.reshape(b, s, A_KV_HEADS, HEAD_DIM), ik.reshape(b, s, IDX_DIM))
    return mm(o, w_out, res=x), state


def dsa_sample(x, g, w_in, w_out, pools_k, pools_v, pools_idx, layer, page_table):
    b, t, _ = x.shape
    past = page_table.shape[1] * pools_k.shape[2]
    q, k, kb, v, vb, iq, ik, ikb, iw = dsa_project(x, g, w_in, past + jnp.arange(t), False)
    o = dsa_sample_attention(q.reshape(b, t, A_HEADS, HEAD_DIM), kb.reshape(b, t, A_KV_HEADS, HEAD_DIM),
                             vb.reshape(b, t, A_KV_HEADS, HEAD_DIM), iq.reshape(b, t, IDX_HEADS, IDX_DIM),
                             ikb.reshape(b, t, IDX_DIM), iw.reshape(b, t, IDX_HEADS), pools_k, pools_v, pools_idx,
                             layer, page_table, min(TOPK_MAX, (past + t) // 4))
    state = (k.reshape(b, t, A_KV_HEADS, HEAD_DIM), v.reshape(b, t, A_KV_HEADS, HEAD_DIM), ik.reshape(b, t, IDX_DIM))
    return mm(o, w_out, res=x), state


def _shift_rows(x, prev, k):
    return jnp.concatenate([prev[prev.shape[0] - k:], x[:x.shape[0] - k]], axis=0)


def _tile_transpose(m, rows_out, cols_out):
    r, c = m.shape
    tile = m
    if c < LANE:
        tile = jnp.concatenate([tile, jnp.zeros((r, LANE - c), m.dtype)], axis=1)
    if r < LANE:
        tile = jnp.concatenate([tile, jnp.zeros((LANE - r, LANE), m.dtype)], axis=0)
    return tile.T[:rows_out, :cols_out]


def _ssd_kernel(z_ref, x_ref, dt_ref, cw_ref, cb_ref, dtb_ref, alog_ref, dskip_ref, buf_ref, h0_ref,
                y_ref, hout_ref, carry_ref, ht_ref, *, n_heads, n_groups, hd, n_state):
    c = pl.program_id(1)
    l = x_ref.shape[0]
    d_inner = n_heads * hd
    rep = n_heads // n_groups

    @pl.when(c == 0)
    def _():
        carry_ref[...] = buf_ref[0]
        for h in range(n_heads):
            ht_ref[h] = _tile_transpose(h0_ref[0, h], n_state, hd)

    x = x_ref[...]
    prev = carry_ref[...]
    conv = cb_ref[...] + x * cw_ref[B_CONV - 1:B_CONV, :]
    for k in range(1, B_CONV):
        conv = conv + _shift_rows(x, prev, k) * cw_ref[B_CONV - 1 - k:B_CONV - k, :]
    carry_ref[...] = x[l - SUBLANE:]
    act = jax.nn.silu(conv)
    gn = n_groups * n_state
    xs, bm, cm = act[:, :d_inner], act[:, d_inner:d_inner + gn], act[:, d_inner + gn:]

    dt = jax.nn.softplus(dt_ref[...] + dtb_ref[...])
    a = -jnp.exp(alog_ref[...])
    acs = dt * a
    k = 1
    while k < l:
        acs = acs + jnp.concatenate([jnp.zeros((k, n_heads), F32), acs[:l - k]], axis=0)
        k *= 2
    acs_t = jnp.concatenate([acs, jnp.zeros((l, LANE - n_heads), F32)], axis=1).T
    last = acs[l - 1:l, :]
    to_end = jnp.exp(last - acs)
    grow = jnp.exp(acs)
    chunk_decay = jnp.exp(last)
    causal = lax.broadcasted_iota(I32, (l, l), 0) >= lax.broadcasted_iota(I32, (l, l), 1)
    z = z_ref[...]
    dskip = dskip_ref[...]

    for g in range(n_groups):
        bm_g = bm[:, g * n_state:(g + 1) * n_state]
        cm_g = cm[:, g * n_state:(g + 1) * n_state].astype(BF16)
        cb = lax.dot_general(cm_g, bm_g.astype(BF16), (((1,), (1,)), ((), ())), preferred_element_type=F32)
        bm_t = bm_g.T.astype(BF16)
        for r in range(rep):
            h = g * rep + r
            hs = slice(h * hd, (h + 1) * hd)
            xs_h = xs[:, hs]
            xr = xs_h * dt[:, h:h + 1]
            seg = acs[:, h:h + 1] - acs_t[h:h + 1, :]
            decay = jnp.where(causal, jnp.exp(jnp.where(causal, seg, 0.0)), 0.0)
            y = jnp.dot((cb * decay).astype(BF16), xr.astype(BF16), preferred_element_type=F32)
            ht = ht_ref[h]
            y = y + jnp.dot(cm_g, ht.astype(BF16), preferred_element_type=F32) * grow[:, h:h + 1]
            upd = jnp.dot(bm_t, (xr * to_end[:, h:h + 1]).astype(BF16), preferred_element_type=F32)
            ht_ref[h] = ht * chunk_decay[:, h:h + 1] + upd
            y = y + xs_h * dskip[:, hs]
            y_ref[:, hs] = y * jax.nn.silu(z[:, hs])

    @pl.when(c == pl.num_programs(1) - 1)
    def _():
        for h in range(n_heads):
            hout_ref[0, h] = _tile_transpose(ht_ref[h], hd, n_state)


def ssd_prompt(z, xbc, dt, conv_w, conv_b, dt_bias, a_log, d_skip, conv_buf, ssm0, n_groups):
    b, n_heads, hd, n_state = ssm0.shape
    m, w = xbc.shape
    s = m // b
    l = min(B_CHUNK, s)
    nc = s // l
    d_inner = n_heads * hd
    buf = jnp.pad(conv_buf.astype(F32), ((0, 0), (SUBLANE - (B_CONV - 1), 0), (0, 0)))
    dskip_full = jnp.repeat(d_skip.astype(F32), hd).reshape(1, d_inner)
    rows = lambda width: pl.BlockSpec((l, width), lambda bb, c: (bb * nc + c, 0))
    state = pl.BlockSpec((1, n_heads, hd, n_state), lambda bb, c: (bb, 0, 0, 0))
    kern = functools.partial(_ssd_kernel, n_heads=n_heads, n_groups=n_groups, hd=hd, n_state=n_state)
    return pl.pallas_call(
        kern,
        grid=(b, nc),
        in_specs=[rows(d_inner), rows(w), rows(n_heads), _resident((B_CONV, w)), _resident((1, w)),
                  _resident((1, n_heads)), _resident((1, n_heads)), _resident((1, d_inner)),
                  pl.BlockSpec((1, SUBLANE, w), lambda bb, c: (bb, 0, 0)), state],
        out_specs=[rows(d_inner), state],
        out_shape=[jax.ShapeDtypeStruct((m, d_inner), F32), jax.ShapeDtypeStruct(ssm0.shape, F32)],
        scratch_shapes=[pltpu.VMEM((SUBLANE, w), F32), pltpu.VMEM((n_heads, n_state, hd), F32)],
        compiler_params=pltpu.CompilerParams(
            dimension_semantics=("parallel", "arbitrary"), vmem_limit_bytes=VMEM_LIMIT),
        name="ssd_prompt",
    )(z, xbc, dt, conv_w.astype(F32), conv_b.reshape(1, w).astype(F32), dt_bias.reshape(1, n_heads).astype(F32),
      a_log.reshape(1, n_heads).astype(F32), dskip_full, buf, ssm0.astype(F32))


def mamba_prompt(x, g, w_in, conv_w, conv_b, dt_bias, a_log, d_skip, g_norm, w_out, conv_buf, ssm0):
    b, s, d = x.shape
    plain = (False, None, [(F32, False)])
    z, xbc, dt = project(x.reshape(b * s, d), g, _split_cols(w_in, (B_D_INNER, B_CONV_DIM, B_HEADS)),
                         [plain, plain, plain], jnp.zeros((b * s,), I32))
    y, ssm = ssd_prompt(z, xbc, dt, conv_w, conv_b, dt_bias, a_log, d_skip, conv_buf, ssm0, B_GROUPS)
    xpad = jnp.concatenate([conv_buf.astype(F32), xbc.reshape(b, s, -1)[:, s - (B_CONV - 1):]], axis=1)
    new_buf = xpad[:, xpad.shape[1] - (B_CONV - 1):]
    return mm(y.reshape(b, s, -1), w_out, g_norm, res=x), new_buf, ssm.astype(ssm0.dtype)


def ssd_chunked(x, dt, a, bm, cm, h0):
    bsz, t, nh, p = x.shape
    g, n = bm.shape[2], bm.shape[3]
    r = nh // g
    l = min(B_CHUNK, t)
    nc = t // l
    xr = (x * dt[..., None]).reshape(bsz, nc, l, g, r, p)
    acs = jnp.cumsum((dt * a).reshape(bsz, nc, l, g, r), axis=2)
    br = bm.reshape(bsz, nc, l, g, n)
    cr = cm.reshape(bsz, nc, l, g, n)
    seg = acs[:, :, :, None] - acs[:, :, None, :]
    causal = jnp.tril(jnp.ones((l, l), bool))[:, :, None, None]
    decay = jnp.where(causal, jnp.exp(jnp.where(causal, seg, 0.0)), 0.0)
    cb = jnp.einsum('bcign,bcjgn->bcijg', cr, br)
    y_intra = jnp.einsum('bcijg,bcijgr,bcjgrp->bcigrp', cb, decay, xr)
    to_end = jnp.exp(acs[:, :, -1:] - acs)
    s_chunk = jnp.einsum('bclgn,bclgr,bclgrp->bcgrpn', br, to_end, xr)
    d_chunk = jnp.exp(acs[:, :, -1])

    def step(hc, inp):
        s_c, d_c = inp
        return hc * d_c[..., None, None] + s_c, hc

    h_fin, h_in = lax.scan(step, h0.reshape(bsz, g, r, p, n),
                           (jnp.moveaxis(s_chunk, 1, 0), jnp.moveaxis(d_chunk, 1, 0)))
    h_in = jnp.moveaxis(h_in, 0, 1)
    y_inter = jnp.einsum('bcign,bcigr,bcgrpn->bcigrp', cr, jnp.exp(acs), h_in)
    return (y_intra + y_inter).reshape(bsz, t, nh, p), h_fin.reshape(bsz, nh, p, n)


def mamba_mixer(x, g, w_in, conv_w, conv_b, dt_bias, a_log, d_skip, g_norm, w_out, conv_buf, ssm0):
    b, t, _ = x.shape
    proj = mm(x, w_in, g)
    z = proj[..., :B_D_INNER]
    xbc = proj[..., B_D_INNER:B_D_INNER + B_CONV_DIM]
    dt = proj[..., B_D_INNER + B_CONV_DIM:]
    xpad = jnp.concatenate([conv_buf.astype(xbc.dtype), xbc], axis=1)
    conv = conv_b + sum(xpad[:, j:j + t] * conv_w[j] for j in range(B_CONV))
    xbc = jax.nn.silu(conv)
    new_buf = xpad[:, t:]
    gn = B_GROUPS * B_STATE
    xs = xbc[..., :B_D_INNER].reshape(b, t, B_HEADS, B_HEADDIM).astype(F32)
    bm = xbc[..., B_D_INNER:B_D_INNER + gn].reshape(b, t, B_GROUPS, B_STATE).astype(F32)
    cm = xbc[..., B_D_INNER + gn:].reshape(b, t, B_GROUPS, B_STATE).astype(F32)
    dt = jax.nn.softplus(dt.astype(F32) + dt_bias.astype(F32))
    a = -jnp.exp(a_log.astype(F32))
    y, ssm = ssd_chunked(xs, dt, a, bm, cm, ssm0.astype(F32))
    y = (y + xs * d_skip.astype(F32)[:, None]).reshape(b, t, B_D_INNER).astype(x.dtype)
    y = y * jax.nn.silu(z)
    return mm(y, w_out, g_norm, res=x), new_buf, ssm.astype(ssm0.dtype)


C_KV_WIDTH = C_KV_HEADS * HEAD_DIM
C_WIDTHS = (C_HEADS * HEAD_DIM,) + (C_KV_WIDTH,) * 6 + (3 * C_HEADS,)


def nsa_project(x, g, w_in, pos, keys_transposed):
    b, t, d = x.shape
    keys = (True, None, [(F32, False), (BF16, keys_transposed)])
    vals = (False, None, [(F32, False), (BF16, False)])
    segments = [(True, HEAD_DIM ** -0.5, [(BF16, False)]), (True, None, [(F32, False)]), (False, None, [(F32, False)]),
                keys, vals, keys, vals, (False, None, [(F32, False)])]
    return project(x.reshape(b * t, d), g, _split_cols(w_in, C_WIDTHS), segments, jnp.tile(pos, b))


def nsa_prompt(x, g, w_in, w_out, cmp_k, cmp_v, w_buf):
    b, s, _ = x.shape
    q, kc, vc, ks, kst, vs, vsb, kw, kwt, vw, vwb, gates = nsa_project(x, g, w_in, jnp.arange(s), True)
    seq = lambda a: a.reshape(b, s, -1)
    kct = compress_rows(seq(kc), cmp_k, C_KV_HEADS, True)
    vcc = compress_rows(seq(vc), cmp_v, C_KV_HEADS, False)
    n_cmp = s // CMP_STRIDE - CMP_BLOCK // CMP_STRIDE + 1
    o = nsa_prompt_attention(seq(q), seq(gates), kct, vcc, n_cmp, kst, seq(vsb), kwt, seq(vwb), C_KV_HEADS)
    heads = lambda a: a.reshape(b, s, C_KV_HEADS, HEAD_DIM)
    state = (heads(kc), heads(vc), heads(ks), heads(vs), last_rows(heads(kw), w_buf), last_rows(heads(vw), w_buf))
    return mm(o, w_out, res=x), state


def nsa_sample(x, g, w_in, w_out, cmp_k, cmp_v, pools_kc, pools_vc, pools_ks, pools_vs, wins_k, wins_v, layer,
               page_table):
    b, t, _ = x.shape
    past = page_table.shape[1] * pools_kc.shape[2]
    q, kc, vc, ks, ksb, vs, vsb, kw, kwb, vw, vwb, gates = nsa_project(x, g, w_in, past + jnp.arange(t), False)
    heads = lambda a: a.reshape(b, t, C_KV_HEADS, HEAD_DIM)
    o = nsa_sample_attention(q.reshape(b, t, C_HEADS, HEAD_DIM), gates.reshape(b, t, -1), heads(ksb), heads(vsb),
                             heads(kwb), heads(vwb), cmp_k, cmp_v, pools_kc, pools_vc, pools_ks, pools_vs, wins_k,
                             wins_v, layer, page_table)
    win_k, win_v = wins_k[layer], wins_v[layer]
    w_buf = win_k.shape[1]
    kwin = jnp.concatenate([win_k, heads(kw)], axis=1)
    vwin = jnp.concatenate([win_v, heads(vw)], axis=1)
    state = (heads(kc), heads(vc), heads(ks), heads(vs), last_rows(kwin, w_buf), last_rows(vwin, w_buf))
    return mm(o, w_out, res=x), state


def kernel(x_prompt, x_sample, cache_a_k, cache_a_v, cache_a_idx, state_b_ssm, state_b_conv, cache_c_cmp_k, cache_c_cmp_v, cache_c_slc_k, cache_c_slc_v, cache_c_win_k, cache_c_win_v, cache_mem_k, cache_mem_v, page_table, mem_prompt, g_ffn1, ffn1_wi, ffn1_wo, g_mix, g_xattn, g_mem, x_w_q, x_w_kv, x_w_o, g_ffn2, ffn2_wi, ffn2_wo, g_final, a_w_in, a_w_out, b_w_in, b_conv_w, b_conv_b, b_dt_bias, b_a_log, b_d_skip, b_g_norm, b_w_out, c_w_in, c_w_out, c_pe_k, c_w1_k, c_w2_k, c_pe_v, c_w1_v, c_w2_v):
    xp, xs = x_prompt, x_sample
    w_buf = cache_c_win_k.shape[2]
    ak_p, av_p, ai_p, bs_p, bc_p = [], [], [], [], []
    cck_p, ccv_p, csk_p, csv_p, cwk_p, cwv_p = [], [], [], [], [], []
    mk_p, mv_p = [], []
    ak_s, av_s, ai_s, bs_s, bc_s = [], [], [], [], []
    cck_s, ccv_s, csk_s, csv_s, cwk_s, cwv_s = [], [], [], [], [], []
    ia = ib = ic = 0
    for i in range(DEPTH):
        xp = ffn_residual(xp, g_ffn1[i], ffn1_wi[i], ffn1_wo[i])
        xs = ffn_residual(xs, g_ffn1[i], ffn1_wi[i], ffn1_wo[i])
        kind = i % N_MIXERS
        if kind == 0:
            op, (k1, v1, i1) = dsa_prompt(xp, g_mix[i], a_w_in[ia], a_w_out[ia])
            os_, (k2, v2, i2) = dsa_sample(xs, g_mix[i], a_w_in[ia], a_w_out[ia], cache_a_k, cache_a_v, cache_a_idx,
                                           ia, page_table)
            ak_p.append(k1); av_p.append(v1); ai_p.append(i1)
            ak_s.append(k2); av_s.append(v2); ai_s.append(i2)
            ia += 1
        elif kind == 1:
            conv0 = jnp.zeros((xp.shape[0], B_CONV - 1, B_CONV_DIM), xp.dtype)
            ssm0 = jnp.zeros((xp.shape[0], B_HEADS, B_HEADDIM, B_STATE), F32)
            wts = (b_w_in[ib], b_conv_w[ib], b_conv_b[ib], b_dt_bias[ib], b_a_log[ib], b_d_skip[ib],
                   b_g_norm[ib], b_w_out[ib])
            op, c1, s1 = mamba_prompt(xp, g_mix[i], *wts, conv0, ssm0)
            os_, c2, s2 = mamba_mixer(xs, g_mix[i], *wts, state_b_conv[ib], state_b_ssm[ib])
            bc_p.append(c1); bs_p.append(s1); bc_s.append(c2); bs_s.append(s2)
            ib += 1
        else:
            cmp_k = (c_pe_k[ic], c_w1_k[ic], c_w2_k[ic])
            cmp_v = (c_pe_v[ic], c_w1_v[ic], c_w2_v[ic])
            op, st1 = nsa_prompt(xp, g_mix[i], c_w_in[ic], c_w_out[ic], cmp_k, cmp_v, w_buf)
            os_, st2 = nsa_sample(xs, g_mix[i], c_w_in[ic], c_w_out[ic], cmp_k, cmp_v, cache_c_cmp_k, cache_c_cmp_v,
                                  cache_c_slc_k, cache_c_slc_v, cache_c_win_k, cache_c_win_v, ic, page_table)
            for lst, arr in zip((cck_p, ccv_p, csk_p, csv_p, cwk_p, cwv_p), st1):
                lst.append(arr)
            for lst, arr in zip((cck_s, ccv_s, csk_s, csv_s, cwk_s, cwv_s), st2):
                lst.append(arr)
            ic += 1
        xp, xs = op, os_
        mk, mv = mem_kv(mem_prompt, g_mem[i], x_w_kv[i])
        mk_p.append(mk); mv_p.append(mv)
        xp = cross_attn(xp, g_xattn[i], mk, mv, x_w_q[i], x_w_o[i])
        xs = cross_attn(xs, g_xattn[i], cache_mem_k[i], cache_mem_v[i], x_w_q[i], x_w_o[i])
        xp = ffn_residual(xp, g_ffn2[i], ffn2_wi[i], ffn2_wo[i])
        xs = ffn_residual(xs, g_ffn2[i], ffn2_wi[i], ffn2_wo[i])
    y_prompt = rms_norm_rows(xp, g_final)
    y_sample = rms_norm_rows(xs, g_final)
    st = jnp.stack
    return (y_prompt, y_sample,
            st(ak_p), st(av_p), st(ai_p), st(bs_p), st(bc_p),
            st(cck_p), st(ccv_p), st(csk_p), st(csv_p), st(cwk_p), st(cwv_p),
            st(mk_p), st(mv_p),
            st(ak_s), st(av_s), st(ai_s), st(bs_s), st(bc_s),
            st(cck_s), st(ccv_s), st(csk_s), st(csv_s), st(cwk_s), st(cwv_s))
```

```python
import functools

import jax
import jax.numpy as jnp
import numpy as np
from jax import lax
from jax.experimental import pallas as pl
from jax.experimental.pallas import tpu as pltpu

F32 = jnp.float32
BF16 = jnp.bfloat16
I32 = jnp.int32
INT_MIN = -2 ** 31
EPS = 1e-6

D_MODEL = 1024
DEPTH = 4
N_MIXERS = 3
HEAD_DIM = 64
ROPE_THETA = 10000.0
Q_BLOCK = 128
A_HEADS = D_MODEL // HEAD_DIM
A_KV_HEADS = 4
IDX_HEADS = 8
IDX_DIM = 64
TOPK_MAX = 256
B_D_INNER = 2 * D_MODEL
B_HEADDIM = 64
B_HEADS = B_D_INNER // B_HEADDIM
B_GROUPS = 4
B_STATE = 128
B_CONV = 4
B_CONV_DIM = B_D_INNER + 2 * B_GROUPS * B_STATE
B_CHUNK = 128
C_HEADS = D_MODEL // HEAD_DIM
C_KV_HEADS = 2
CMP_BLOCK = 32
CMP_STRIDE = 16
SLC_BLOCK = 64
N_SLC = 16
WINDOW = 512
X_HEADS = 4
X_HEAD_DIM = 128
D_FF = 2816

LANE = 128
SUBLANE = 8
VMEM_LIMIT = 48 * 1024 * 1024
MM_COLS = 512
FF_CHUNK = 256
ATTN_COLS = 1024
TQ = SUBLANE
PAGES_PER_STEP = 16


def _pick_tile(n, candidates):
    for c in candidates:
        if n % c == 0:
            return c
    return n


def _resident(shape):
    return pl.BlockSpec(shape, lambda *_: (0,) * len(shape), pipeline_mode=pl.Buffered(1))


def _rms(x, g):
    return x * lax.rsqrt(jnp.mean(x * x, axis=-1, keepdims=True) + EPS) * g


def _nmm_kernel(x_ref, g_ref, w_ref, *rest, norm, residual):
    o_ref = rest[-1]
    x = x_ref[...]
    if norm:
        x = _rms(x, g_ref[...])
    xb = x.astype(BF16)
    n = w_ref.shape[1]
    for c0 in range(0, n, MM_COLS):
        c1 = min(c0 + MM_COLS, n)
        y = jnp.dot(xb, w_ref[:, c0:c1], preferred_element_type=F32)
        o_ref[:, c0:c1] = rest[0][:, c0:c1] + y if residual else y


def norm_matmul(x, w, g=None, res=None):
    m, k = x.shape
    n = w.shape[1]
    assert res is None or (n % LANE == 0 and m % SUBLANE == 0)
    n_pad = -(-n // LANE) * LANE
    wb = w.astype(BF16)
    if n_pad != n:
        wb = jnp.pad(wb, ((0, 0), (0, n_pad - n)))
    m_orig = m
    if m % SUBLANE != 0:
        m = -(-m // 256) * 256
        x = jnp.pad(x, ((0, m - m_orig), (0, 0)))
    tm = _pick_tile(m, (256, 128, 64, 32, 16, 8))
    norm = g is not None
    gg = (g if norm else jnp.ones((k,), F32)).reshape(1, k).astype(F32)
    row_block = lambda width: pl.BlockSpec((tm, width), lambda i: (i, 0))
    extra = [] if res is None else [res]
    out = pl.pallas_call(
        functools.partial(_nmm_kernel, norm=norm, residual=res is not None),
        grid=(m // tm,),
        in_specs=[row_block(k), _resident((1, k)), _resident((k, n_pad))] + [row_block(n_pad) for _ in extra],
        out_specs=row_block(n_pad),
        out_shape=jax.ShapeDtypeStruct((m, n_pad), F32),
        compiler_params=pltpu.CompilerParams(dimension_semantics=("parallel",), vmem_limit_bytes=VMEM_LIMIT),
        name="norm_matmul",
    )(x, gg, wb, *extra)
    if n_pad != n or m != m_orig:
        out = out[:m_orig, :n]
    return out


def mm(x, w, g=None, res=None):
    lead = x.shape[:-1]
    r2 = None if res is None else res.reshape(-1, res.shape[-1])
    return norm_matmul(x.reshape(-1, x.shape[-1]), w, g, r2).reshape(lead + (w.shape[1],))


def _ffn_kernel(x_ref, g_ref, wg_ref, wu_ref, wo_ref, o_ref, act_ref, *, n_chunks):
    xn = _rms(x_ref[...], g_ref[...]).astype(BF16)

    for c in range(n_chunks):
        cs = slice(c * FF_CHUNK, (c + 1) * FF_CHUNK)
        hg = jnp.dot(xn, wg_ref[:, cs], preferred_element_type=F32)
        hu = jnp.dot(xn, wu_ref[:, cs], preferred_element_type=F32)
        act_ref[:, cs] = (jax.nn.silu(hg) * hu).astype(BF16)
    for c0 in range(0, o_ref.shape[1], MM_COLS):
        cs = slice(c0, c0 + MM_COLS)
        o_ref[:, cs] = x_ref[:, cs] + 0.5 * jnp.dot(act_ref[...], wo_ref[:, cs], preferred_element_type=F32)


def ffn_residual(x, g, wi, wo):
    lead, d = x.shape[:-1], x.shape[-1]
    x2 = x.reshape(-1, d)
    m = x2.shape[0]
    f = wo.shape[0]
    assert f % FF_CHUNK == 0 and d % MM_COLS == 0
    tm = _pick_tile(m, (512, 256, 128, 64, 32, 16, 8))
    wb = wi.astype(BF16)
    out = pl.pallas_call(
        functools.partial(_ffn_kernel, n_chunks=f // FF_CHUNK),
        grid=(m // tm,),
        in_specs=[pl.BlockSpec((tm, d), lambda i: (i, 0)), _resident((1, d)), _resident((d, f)), _resident((d, f)),
                  _resident((f, d))],
        out_specs=pl.BlockSpec((tm, d), lambda i: (i, 0)),
        out_shape=jax.ShapeDtypeStruct((m, d), F32),
        scratch_shapes=[pltpu.VMEM((tm, f), BF16)],
        compiler_params=pltpu.CompilerParams(dimension_semantics=("parallel",), vmem_limit_bytes=VMEM_LIMIT),
        name="ffn_residual",
    )(x2, g.reshape(1, d).astype(F32), wb[:, :f], wb[:, f:], wo.astype(BF16))
    return out.reshape(lead + (d,))


def _rmsnorm_kernel(x_ref, g_ref, o_ref):
    o_ref[...] = _rms(x_ref[...], g_ref[...])


def rms_norm_rows(x, g):
    lead, d = x.shape[:-1], x.shape[-1]
    x2 = x.reshape(-1, d)
    m = x2.shape[0]
    tm = _pick_tile(m, (512, 256, 128, 64, 32, 16, 8))
    row_block = pl.BlockSpec((tm, d), lambda i: (i, 0))
    out = pl.pallas_call(
        _rmsnorm_kernel, grid=(m // tm,), in_specs=[row_block, _resident((1, d))], out_specs=row_block,
        out_shape=jax.ShapeDtypeStruct((m, d), F32),
        compiler_params=pltpu.CompilerParams(dimension_semantics=("parallel",)), name="rms_norm_rows",
    )(x2, g.reshape(1, d).astype(F32))
    return out.reshape(lead + (d,))


def rope_tables(pos):
    half = HEAD_DIM // 2
    inv = ROPE_THETA ** (-jnp.arange(half, dtype=F32) / half)
    ang = pos.astype(F32)[:, None] * inv[None, :]
    cos, sin = jnp.cos(ang), jnp.sin(ang)
    reps = LANE // HEAD_DIM
    return (jnp.tile(jnp.concatenate([cos, cos], axis=1), (1, reps)),
            jnp.tile(jnp.concatenate([-sin, sin], axis=1), (1, reps)))


def _rope_cols(y, cos, sin):
    w = y.shape[1]
    half = HEAD_DIM // 2
    lane = lax.broadcasted_iota(I32, y.shape, 1)
    partner = jnp.where(lane % HEAD_DIM < half, pltpu.roll(y, w - half, axis=1), pltpu.roll(y, half, axis=1))
    reps = w // LANE
    return y * jnp.tile(cos, (1, reps)) + partner * jnp.tile(sin, (1, reps))


def _proj_kernel(x_ref, g_ref, cos_ref, sin_ref, w_ref, *out_refs, segments):
    xb = _rms(x_ref[...], g_ref[...]).astype(BF16)
    cos, sin = cos_ref[...], sin_ref[...]
    k = 0
    for start, width, rope, scale, outs in segments:
        y = jnp.dot(xb, w_ref[:, start:start + width], preferred_element_type=F32)
        if rope:
            y = _rope_cols(y, cos, sin)
        if scale is not None:
            y = y * scale
        for used, dtype, transposed in outs:
            v = y[:, :used]
            out_refs[k][...] = (v.T if transposed else v).astype(dtype)
            k += 1


def project(x, g, w_cols, segments, pos_rows):
    m, kdim = x.shape
    packed, segs, out_shapes, out_specs = [], [], [], []
    tm = _pick_tile(m, (256, 128, 64, 32, 16, 8))
    start = 0
    for wc, (rope, scale, outs) in zip(w_cols, segments):
        n = wc.shape[1]
        n_pad = -(-n // LANE) * LANE
        packed.append(jnp.pad(wc.astype(BF16), ((0, 0), (0, n_pad - n))))
        segs.append((start, n_pad, rope, scale, tuple((n, dt, tr) for dt, tr in outs)))
        for dt, tr in outs:
            if tr:
                out_shapes.append(jax.ShapeDtypeStruct((n, m), dt))
                out_specs.append(pl.BlockSpec((n, tm), lambda i: (0, i)))
            else:
                out_shapes.append(jax.ShapeDtypeStruct((m, n), dt))
                out_specs.append(pl.BlockSpec((tm, n), lambda i: (i, 0)))
        start += n_pad
    wb = jnp.concatenate(packed, axis=1)
    cos, sin = rope_tables(pos_rows)
    row_block = lambda width: pl.BlockSpec((tm, width), lambda i: (i, 0))
    return pl.pallas_call(
        functools.partial(_proj_kernel, segments=tuple(segs)),
        grid=(m // tm,),
        in_specs=[row_block(kdim), _resident((1, kdim)), row_block(LANE), row_block(LANE), _resident(wb.shape)],
        out_specs=out_specs,
        out_shape=out_shapes,
        compiler_params=pltpu.CompilerParams(dimension_semantics=("parallel",), vmem_limit_bytes=VMEM_LIMIT),
        name="project",
    )(x, g.reshape(1, kdim).astype(F32), cos, sin, wb)


def _split_cols(w, widths):
    cols = np.cumsum((0,) + tuple(widths))
    return [w[:, cols[j]:cols[j + 1]] for j in range(len(widths))]


def _softmax_rows(s):
    m = jnp.max(s, axis=-1, keepdims=True)
    e = jnp.exp(s - jnp.where(m == -jnp.inf, 0.0, m))
    den = jnp.sum(e, axis=-1, keepdims=True)
    return e / jnp.where(den > 0, den, 1.0)


def _flash_step(carry, q, kt_tile, v_tile, sel, rep):
    m, l, acc = carry
    rows, tk = q.shape[0], kt_tile.shape[1]
    s = jnp.dot(q, kt_tile, preferred_element_type=F32)
    s = jnp.where(sel, s.reshape(rep, Q_BLOCK, tk), -jnp.inf).reshape(rows, tk)
    m_new = jnp.maximum(m, jnp.max(s, axis=1, keepdims=True))
    m_safe = jnp.where(m_new == -jnp.inf, 0.0, m_new)
    alpha = jnp.exp(m - m_safe)
    p = jnp.exp(s - m_safe)
    l = l * alpha + jnp.sum(p, axis=1, keepdims=True)
    pv = jnp.dot(p.astype(BF16), v_tile, preferred_element_type=F32)
    return m_new, l, acc * alpha + pv


def _flash_init(rows, hd):
    return (jnp.full((rows, 1), -jnp.inf, F32), jnp.zeros((rows, 1), F32), jnp.zeros((rows, hd), F32))


def _flash_finish(carry):
    _, l, acc = carry
    return acc / jnp.where(l > 0, l, 1.0)


def _stack_heads(x, h0, rep, hd):
    return jnp.concatenate([x[:, (h0 + r) * hd:(h0 + r + 1) * hd] for r in range(rep)], axis=0)


def _sortable_key(score):
    bits = pltpu.bitcast(score, I32)
    key = jnp.where(bits < 0, bits ^ jnp.int32(0x7FFFFFFF), bits)
    return jnp.where(score == 0.0, 0, key)


def _count_lanes(pred_fn, key_ref, nkt, tk):
    rows = key_ref.shape[0]

    def body(kt, cnt):
        keys = key_ref[:, pl.ds(pl.multiple_of(kt * tk, tk), tk)]
        for c in range(tk // LANE):
            hit = pred_fn(keys[:, c * LANE:(c + 1) * LANE], kt * tk + c * LANE)
            cnt = cnt + jnp.where(hit, 1, 0)
        return cnt

    cnt = lax.fori_loop(0, nkt, body, jnp.zeros((rows, LANE), I32))
    return jnp.sum(cnt, axis=1, keepdims=True)


def _top_k_mask_params(key_ref, nkt, tk, topk, n_valid):
    rows = key_ref.shape[0]
    lane_iota = lax.broadcasted_iota(I32, (rows, LANE), 1)

    def bit_body(it, thr):
        cand = thr + jnp.left_shift(jnp.int32(1), 31 - it)
        cnt = _count_lanes(lambda keys, c0: keys >= cand, key_ref, nkt, tk)
        return jnp.where(cnt >= topk, cand, thr)

    thr = lax.fori_loop(0, 32, bit_body, jnp.full((rows, 1), INT_MIN, I32))
    many = n_valid > topk
    thr = jnp.where(many, thr, INT_MIN + 1)
    c_gt = _count_lanes(lambda keys, c0: keys > thr, key_ref, nkt, tk)
    c_eq = _count_lanes(lambda keys, c0: keys == thr, key_ref, nkt, tk)
    need = topk - c_gt
    excess = many & (c_eq > need)
    n_col_bits = max(1, (key_ref.shape[1] - 1).bit_length())

    def tie_search():
        def jbody(it, last):
            cand = last + jnp.left_shift(jnp.int32(1), n_col_bits - 1 - it)
            cnt = _count_lanes(lambda keys, c0: (keys == thr) & (c0 + lane_iota < cand), key_ref, nkt, tk)
            return jnp.where(cnt <= need - 1, cand, last)

        return lax.fori_loop(0, n_col_bits, jbody, jnp.zeros((rows, 1), I32))

    any_excess = jnp.max(jnp.where(excess, 1, 0)) > 0
    last_tie = lax.cond(any_excess, tie_search, lambda: jnp.zeros((rows, 1), I32))
    return thr, jnp.where(excess, last_tie, jnp.int32(2 ** 30))


SEARCH_DIGIT_BITS = 4


def _top_k_mask_params_small(keys, col, topk, n_valid):
    rows = keys.shape[0]
    count = lambda hit: jnp.sum(jnp.where(hit, 1, 0), axis=1, keepdims=True)
    thr = jnp.full((rows, 1), INT_MIN, I32)
    for shift in range(32 - SEARCH_DIGIT_BITS, -1, -SEARCH_DIGIT_BITS):
        digit = jnp.zeros((rows, 1), I32)
        for d in range(1, 2 ** SEARCH_DIGIT_BITS):
            step = np.int64(d << shift).astype(np.int32)
            digit = digit + jnp.where(count(keys >= thr + jnp.int32(step)) >= topk, 1, 0)
        thr = thr + jnp.left_shift(digit, shift)
    many = n_valid > topk
    thr = jnp.where(many, thr, INT_MIN + 1)
    need = topk - count(keys > thr)
    tie = keys == thr
    excess = many & (count(tie) > need)
    n_col_bits = max(1, (keys.shape[1] - 1).bit_length())

    def tie_search():
        last = jnp.zeros((rows, 1), I32)
        for it in range(n_col_bits):
            cand = last + (1 << (n_col_bits - 1 - it))
            last = jnp.where(count(tie & (col < cand)) <= need - 1, cand, last)
        return last

    any_excess = jnp.max(jnp.where(excess, 1, 0)) > 0
    last_tie = lax.cond(any_excess, tie_search, lambda: jnp.zeros((rows, 1), I32))
    return thr, jnp.where(excess, last_tie, jnp.int32(2 ** 30))


def _block_cover(n_cmp_rows):
    c0 = lax.broadcasted_iota(I32, (n_cmp_rows, LANE), 0) * CMP_STRIDE
    s0 = lax.broadcasted_iota(I32, (n_cmp_rows, LANE), 1) * SLC_BLOCK
    return jnp.where(c0 < s0 + SLC_BLOCK, jnp.where(c0 + CMP_BLOCK > s0, 1.0, 0.0), 0.0).astype(BF16)


def _block_importance(p_sum, cover, row_pos):
    p_hi = p_sum.astype(BF16)
    p_lo = (p_sum - p_hi.astype(F32)).astype(BF16)
    imp = jnp.dot(p_hi, cover, preferred_element_type=F32) + jnp.dot(p_lo, cover, preferred_element_type=F32)
    jb = lax.broadcasted_iota(I32, imp.shape, 1)
    cur = row_pos // SLC_BLOCK
    forced = (jb == 0) | (jb == cur) | (jb == cur - 1)
    imp = jnp.where(forced, jnp.inf, imp)
    return jnp.where(jb <= cur, imp, -jnp.inf)


def _dsa_prompt_kernel(iq_ref, iw_ref, ikt_ref, q_ref, kt_ref, v_ref, o_ref, key_ref, *, topk, tk,
                       n_idx_heads, n_groups, rep):
    i = pl.program_id(1)
    t0 = i * Q_BLOCK
    nkt = (t0 + Q_BLOCK + tk - 1) // tk
    row = t0 + lax.broadcasted_iota(I32, (Q_BLOCK, 1), 0)
    hd = kt_ref.shape[0] // n_groups
    di = ikt_ref.shape[0]

    iw = iw_ref[0]
    iq = iq_ref[0]
    iq_heads = [iq[:, h * di:(h + 1) * di] for h in range(n_idx_heads)]

    def score_body(kt, _):
        off = pl.multiple_of(kt * tk, tk)
        acc = jnp.zeros((Q_BLOCK, tk), F32)
        for h in range(n_idx_heads):
            r = jnp.dot(iq_heads[h], ikt_ref[:, pl.ds(off, tk)], preferred_element_type=F32)
            acc = acc + jnp.maximum(r, 0.0) * iw[:, h:h + 1]
        col = off + lax.broadcasted_iota(I32, (Q_BLOCK, tk), 1)
        key_ref[:, pl.ds(off, tk)] = jnp.where(col <= row, _sortable_key(acc), INT_MIN)
        return 0

    lax.fori_loop(0, nkt, score_body, 0)

    thr, last_tie = _top_k_mask_params(key_ref, nkt, tk, topk, row + 1)

    q = q_ref[0]
    q_groups = [_stack_heads(q, g * rep, rep, hd) for g in range(n_groups)]
    rows_g = rep * Q_BLOCK

    def attn_body(kt, carry):
        off = pl.multiple_of(kt * tk, tk)
        keys = key_ref[:, pl.ds(off, tk)]
        col = off + lax.broadcasted_iota(I32, (Q_BLOCK, tk), 1)
        sel = ((keys > thr) | ((keys == thr) & (col <= last_tie)))[None]
        v_t = v_ref[0, pl.ds(off, tk), :]
        return tuple(
            _flash_step(carry[g], q_groups[g], kt_ref[g * hd:(g + 1) * hd, pl.ds(off, tk)],
                        v_t[:, g * hd:(g + 1) * hd], sel, rep)
            for g in range(n_groups))

    fin = lax.fori_loop(0, nkt, attn_body, tuple(_flash_init(rows_g, hd) for _ in range(n_groups)))
    for g in range(n_groups):
        o = _flash_finish(fin[g])
        for r in range(rep):
            h = g * rep + r
            o_ref[0, :, h * hd:(h + 1) * hd] = o[r * Q_BLOCK:(r + 1) * Q_BLOCK]


def dsa_prompt_attention(q, kt, v, iq, ikt, iw, topk, n_groups, n_idx_heads, tk=ATTN_COLS):
    b, s, hd_all = q.shape
    gd, di = kt.shape[0], ikt.shape[0]
    d = gd // n_groups
    tk = min(tk, s)
    kern = functools.partial(_dsa_prompt_kernel, topk=topk, tk=tk, n_idx_heads=n_idx_heads, n_groups=n_groups,
                             rep=hd_all // gd)
    q_block = lambda width: pl.BlockSpec((1, Q_BLOCK, width), lambda bb, i: (bb, i, 0))
    return pl.pallas_call(
        kern,
        grid=(b, s // Q_BLOCK),
        in_specs=[
            q_block(n_idx_heads * di), q_block(n_idx_heads),
            pl.BlockSpec((di, s), lambda bb, i: (0, bb)),
            q_block(hd_all),
            pl.BlockSpec((gd, s), lambda bb, i: (0, bb)),
            pl.BlockSpec((1, s, gd), lambda bb, i: (bb, 0, 0)),
        ],
        out_specs=q_block(hd_all),
        out_shape=jax.ShapeDtypeStruct((b, s, hd_all), F32),
        scratch_shapes=[pltpu.VMEM((Q_BLOCK, s), I32)],
        compiler_params=pltpu.CompilerParams(
            dimension_semantics=("parallel", "arbitrary"), vmem_limit_bytes=VMEM_LIMIT),
        name="dsa_prompt_attention",
    )(iq, iw, ikt, q, kt, v)


def _top_blocks(imp_ref, n_live, n_sel):
    imp_t = imp_ref[...]
    j_iota = lax.broadcasted_iota(I32, imp_t.shape, 0)

    def body(k, rank):
        row_k = imp_ref[pl.ds(k, 1), :]
        earlier = jnp.where(j_iota > k, 1.0, 0.0)
        return rank + jnp.where(row_k > imp_t, 1.0, jnp.where(row_k == imp_t, earlier, 0.0))

    rank = lax.fori_loop(0, n_live, body, jnp.zeros(imp_t.shape, F32))
    return jnp.where(rank < n_sel, 1.0, 0.0)


def _nsa_prompt_kernel(q_ref, gate_ref, kct_ref, vc_ref, kst_ref, vs_ref, kwt_ref, vw_ref, o_ref, imp_ref, *,
                       tk, n_cmp, n_blk, n_groups, rep, win_len):
    i = pl.program_id(1)
    t0 = i * Q_BLOCK
    nkt = (t0 + Q_BLOCK + tk - 1) // tk
    hd = kst_ref.shape[0] // n_groups
    rows = rep * Q_BLOCK
    nc_pad = kct_ref.shape[-1]
    row = t0 + lax.broadcasted_iota(I32, (Q_BLOCK, 1), 0)
    gates = jax.nn.sigmoid(gate_ref[0])
    n_heads = n_groups * rep
    q_all = q_ref[0]

    cover = _block_cover(nc_pad)
    c_idx = lax.broadcasted_iota(I32, (Q_BLOCK, nc_pad), 1)
    c_ok = ((c_idx < n_cmp) & (c_idx * CMP_STRIDE + (CMP_BLOCK - 1) <= row))[None]
    w_start = pl.multiple_of(jnp.maximum(t0 - WINDOW, 0), Q_BLOCK)
    w_col = w_start + lax.broadcasted_iota(I32, (Q_BLOCK, win_len), 1)
    w_d = row - w_col
    w_ok = ((w_d >= 0) & (w_d < WINDOW))[None]

    for g in range(n_groups):
        q = _stack_heads(q_all, g * rep, rep, hd)
        lo, hi = g * hd, (g + 1) * hd

        s = jnp.dot(q, kct_ref[0, lo:hi, :], preferred_element_type=F32)
        s = jnp.where(c_ok, s.reshape(rep, Q_BLOCK, nc_pad), -jnp.inf)
        p = _softmax_rows(s)
        o_cmp = jnp.dot(p.reshape(rows, nc_pad).astype(BF16), vc_ref[0, :, lo:hi], preferred_element_type=F32)

        imp = _block_importance(jnp.sum(p, axis=0), cover, row)
        imp_ref[...] = imp.T
        n_live = jnp.minimum((t0 + Q_BLOCK - 1) // SLC_BLOCK + 1, n_blk)
        sel_blocks = _top_blocks(imp_ref, n_live, min(N_SLC, n_blk)).T.astype(BF16)

        def slc_body(kt, carry):
            off = pl.multiple_of(kt * tk, tk)
            col = off + lax.broadcasted_iota(I32, (Q_BLOCK, tk), 1)
            blk_of_col = (off + lax.broadcasted_iota(I32, (LANE, tk), 1)) // SLC_BLOCK
            expand = jnp.where(lax.broadcasted_iota(I32, (LANE, tk), 0) == blk_of_col, 1.0, 0.0).astype(BF16)
            hit = jnp.dot(sel_blocks, expand, preferred_element_type=F32)
            sel = (jnp.where(col <= row, hit, 0.0) > 0.5)[None]
            return _flash_step(carry, q, kst_ref[lo:hi, pl.ds(off, tk)], vs_ref[0, pl.ds(off, tk), lo:hi], sel, rep)

        o_slc = _flash_finish(lax.fori_loop(0, nkt, slc_body, _flash_init(rows, hd)))

        s = jnp.dot(q, kwt_ref[lo:hi, pl.ds(w_start, win_len)], preferred_element_type=F32)
        s = jnp.where(w_ok, s.reshape(rep, Q_BLOCK, win_len), -jnp.inf)
        p = _softmax_rows(s).reshape(rows, win_len)
        o_win = jnp.dot(p.astype(BF16), vw_ref[0, pl.ds(w_start, win_len), lo:hi], preferred_element_type=F32)

        for r in range(rep):
            h = g * rep + r
            rs = slice(r * Q_BLOCK, (r + 1) * Q_BLOCK)
            o_ref[0, :, h * hd:(h + 1) * hd] = (
                gates[:, h:h + 1] * o_cmp[rs] + gates[:, n_heads + h:n_heads + h + 1] * o_slc[rs]
                + gates[:, 2 * n_heads + h:2 * n_heads + h + 1] * o_win[rs])


def nsa_prompt_attention(q, gate_logits, kct, vc, n_cmp, kst, vs, kwt, vw, n_groups, tk=ATTN_COLS):
    b, s, hd_all = q.shape
    gd, nc_pad = kct.shape[1], kct.shape[2]
    n_blk = -(-s // SLC_BLOCK)
    assert n_blk <= LANE and s % Q_BLOCK == 0
    tk = min(tk, s)
    win_len = min(WINDOW + Q_BLOCK, s)
    kern = functools.partial(_nsa_prompt_kernel, tk=tk, n_cmp=n_cmp, n_blk=n_blk, n_groups=n_groups,
                             rep=hd_all // gd, win_len=win_len)
    per_b = lambda *blk: pl.BlockSpec((1,) + blk, lambda bb, i: (bb,) + (0,) * len(blk))
    keys_t = pl.BlockSpec((gd, s), lambda bb, i: (0, bb))
    q_block = lambda width: pl.BlockSpec((1, Q_BLOCK, width), lambda bb, i: (bb, i, 0))
    return pl.pallas_call(
        kern,
        grid=(b, s // Q_BLOCK),
        in_specs=[q_block(hd_all), q_block(gate_logits.shape[-1]), per_b(gd, nc_pad), per_b(nc_pad, gd),
                  keys_t, per_b(s, gd), keys_t, per_b(s, gd)],
        out_specs=q_block(hd_all),
        out_shape=jax.ShapeDtypeStruct((b, s, hd_all), F32),
        scratch_shapes=[pltpu.VMEM((LANE, Q_BLOCK), F32)],
        compiler_params=pltpu.CompilerParams(
            dimension_semantics=("parallel", "arbitrary"), vmem_limit_bytes=VMEM_LIMIT),
        name="nsa_prompt_attention",
    )(q, gate_logits, kct, vc, kst, vs, kwt, vw)


def _compress_finish(a, pew_ref, w2_ref, n_groups, hd):
    half = n_groups * hd
    first, second = a[:, :half], a[:, half:]
    second = jnp.concatenate([second[1:], jnp.zeros((1, half), F32)], axis=0)
    out = []
    for g in range(n_groups):
        hcol = first[:, g * hd:(g + 1) * hd] + second[:, g * hd:(g + 1) * hd] + pew_ref[...]
        out.append(jnp.dot(jax.nn.silu(hcol).astype(BF16), w2_ref[...], preferred_element_type=F32))
    return out


def _compress_kernel(c_ref, wc_ref, pew_ref, w2_ref, o_ref, *, n_groups, hd, transposed):
    a = jnp.dot(c_ref[0].astype(BF16), wc_ref[...], preferred_element_type=F32)
    res = jnp.concatenate(_compress_finish(a, pew_ref, w2_ref, n_groups, hd), axis=1)
    o_ref[0] = (res.T if transposed else res).astype(BF16)


def _compress_chunk_weights(w1, n_groups, hd):
    wr = w1.reshape(2, CMP_STRIDE, hd, hd)
    eye = jnp.eye(n_groups, dtype=w1.dtype)
    wc = jnp.einsum('hrdo,gk->rgdhko', wr, eye)
    return wc.reshape(CMP_STRIDE * n_groups * hd, 2 * n_groups * hd)


def _compress_weights(pe, w1, w2, n_groups, hd):
    return _compress_chunk_weights(w1, n_groups, hd).astype(BF16), mm(pe.reshape(1, -1), w1), w2.astype(BF16)


def compress_rows(rows, cmp_w, n_groups, transposed):
    b, l, gd = rows.shape
    hd = gd // n_groups
    n_chunks = l // CMP_STRIDE
    cw = CMP_STRIDE * gd
    wc, pew, w2 = _compress_weights(*cmp_w, n_groups, hd)
    out_blk = (gd, n_chunks) if transposed else (n_chunks, gd)
    return pl.pallas_call(
        functools.partial(_compress_kernel, n_groups=n_groups, hd=hd, transposed=transposed),
        grid=(b,),
        in_specs=[pl.BlockSpec((1, n_chunks, cw), lambda bb: (bb, 0, 0)), _resident(wc.shape), _resident(pew.shape),
                  _resident(w2.shape)],
        out_specs=pl.BlockSpec((1,) + out_blk, lambda bb: (bb, 0, 0)),
        out_shape=jax.ShapeDtypeStruct((b,) + out_blk, BF16),
        compiler_params=pltpu.CompilerParams(dimension_semantics=("parallel",), vmem_limit_bytes=VMEM_LIMIT),
        name="compress_rows",
    )(rows.reshape(b, n_chunks, cw), wc, pew, w2)


def _attend_cols(q, kt, vt, sel, rep):
    n = kt.shape[1]
    s = jnp.dot(q, kt, preferred_element_type=F32)
    s = jnp.where(sel, s.reshape(rep, TQ, n), -jnp.inf)
    p = _softmax_rows(s).reshape(rep * TQ, n)
    return p, lax.dot_general(p.astype(BF16), vt, (((1,), (1,)), ((), ())), preferred_element_type=F32)


def _head_rows(x):
    b, t, h, d = x.shape
    x = jnp.pad(x.astype(BF16), ((0, 0), (0, TQ - t), (0, 0), (0, 0)))
    return x.transpose(0, 2, 1, 3).reshape(b, h * TQ, d)


def _new_cols(x, width):
    b, t = x.shape[:2]
    return jnp.pad(x.reshape(b, t, -1).astype(BF16).transpose(0, 2, 1), ((0, 0), (0, 0), (0, width - t)))


def _cols_view(pool):
    lead, rows = pool.shape[:2], pool.shape[2]
    perm = (0, 1) + tuple(range(3, pool.ndim)) + (2,)
    return pool.transpose(perm).reshape(lead + (-1, rows))


def _page_specs(block, layer, n_pages, pp):
    def spec(j):
        return pl.BlockSpec((1, 1) + block,
                            lambda bb, p, pt: (layer, pt[bb * n_pages + p * pp + j]) + (0,) * len(block))
    return [spec(j) for j in range(pp)]


def _per_batch(*blk):
    return pl.BlockSpec((1,) + blk, lambda bb, p, pt: (bb,) + (0,) * len(blk))


def _whole(*blk):
    return pl.BlockSpec(blk, lambda bb, p, pt: (0,) * len(blk))


def _dsa_sample_kernel(pt_ref, iq_ref, iw_ref, q_ref, ikn_ref, kn_ref, vn_ref, *rest, topk, past, n_new,
                       n_idx_heads, n_groups, rep, pp):
    idx_pages = rest[:pp]
    k_pages = rest[pp:2 * pp]
    v_pages = rest[2 * pp:3 * pp]
    o_ref, key_ref, kst, vst = rest[3 * pp:]
    p_step = pl.program_id(1)
    hd = q_ref.shape[-1]
    page = idx_pages[0].shape[-1]
    n_tiles = past // page + 1
    iw = iw_ref[0]
    tpos = past + jnp.minimum(lax.broadcasted_iota(I32, (TQ, 1), 0), n_new - 1)

    def scores(ikt):
        r = jnp.dot(iq_ref[0], ikt, preferred_element_type=F32)
        r = jnp.maximum(r, 0.0).reshape(n_idx_heads, TQ, ikt.shape[1]) * iw
        return jnp.sum(r, axis=0)

    for j in range(pp):
        off = pl.multiple_of((p_step * pp + j) * page, page)
        key_ref[:, pl.ds(off, page)] = _sortable_key(scores(idx_pages[j][0, 0].astype(BF16)))
        kst[:, pl.ds(off, page)] = k_pages[j][0, 0].astype(BF16)
        vst[:, pl.ds(off, page)] = v_pages[j][0, 0].astype(BF16)

    @pl.when(p_step == pl.num_programs(1) - 1)
    def _():
        c = lax.broadcasted_iota(I32, (TQ, page), 1)
        fresh_ok = (c < n_new) & (past + c <= tpos)
        key_ref[:, past:past + page] = jnp.where(fresh_ok, _sortable_key(scores(ikn_ref[0])), INT_MIN)
        kst[:, past:past + page] = kn_ref[0]
        vst[:, past:past + page] = vn_ref[0]
        keys = key_ref[...]
        col = lax.broadcasted_iota(I32, keys.shape, 1)
        thr, last_tie = _top_k_mask_params_small(keys, col, topk, tpos + 1)
        sel = ((keys > thr) | ((keys == thr) & (col <= last_tie)))[None]
        for g in range(n_groups):
            gs = slice(g * hd, (g + 1) * hd)
            _, o = _attend_cols(q_ref[0, g * rep * TQ:(g + 1) * rep * TQ], kst[gs, :], vst[gs, :], sel, rep)
            for r in range(rep):
                h = g * rep + r
                o_ref[0, :, h * hd:(h + 1) * hd] = o[r * TQ:(r + 1) * TQ]


def dsa_sample_attention(q, k, v, iq, ik, iw, pools_k, pools_v, pools_idx, layer, page_table, topk):
    b, t, h, d = q.shape
    g = k.shape[2]
    hi, di = iq.shape[2], iq.shape[3]
    n_pages = page_table.shape[1]
    page = pools_k.shape[2]
    past = n_pages * page
    pp = min(PAGES_PER_STEP, n_pages)
    assert n_pages % pp == 0 and t <= TQ
    iw_h = jnp.pad(iw.astype(F32), ((0, 0), (0, TQ - t), (0, 0))).transpose(0, 2, 1)[..., None]
    kern = functools.partial(_dsa_sample_kernel, topk=topk, past=past, n_new=t, n_idx_heads=hi, n_groups=g,
                             rep=h // g, pp=pp)
    o = pl.pallas_call(
        kern,
        grid_spec=pltpu.PrefetchScalarGridSpec(
            num_scalar_prefetch=1,
            grid=(b, n_pages // pp),
            in_specs=[_per_batch(hi * TQ, di), _per_batch(hi, TQ, 1), _per_batch(h * TQ, d), _per_batch(di, page),
                      _per_batch(g * d, page), _per_batch(g * d, page)]
                     + _page_specs((di, page), layer, n_pages, pp)
                     + _page_specs((g * d, page), layer, n_pages, pp)
                     + _page_specs((g * d, page), layer, n_pages, pp),
            out_specs=_per_batch(TQ, h * d),
            scratch_shapes=[pltpu.VMEM((TQ, past + page), I32), pltpu.VMEM((g * d, past + page), BF16),
                            pltpu.VMEM((g * d, past + page), BF16)],
        ),
        out_shape=jax.ShapeDtypeStruct((b, TQ, h * d), F32),
        compiler_params=pltpu.CompilerParams(
            dimension_semantics=("parallel", "arbitrary"), vmem_limit_bytes=VMEM_LIMIT),
        name="dsa_sample_attention",
    )(page_table.reshape(-1).astype(I32), _head_rows(iq), iw_h, _head_rows(q), _new_cols(ik, page),
      _new_cols(k, page), _new_cols(v, page),
      *([_cols_view(pools_idx)] * pp), *([_cols_view(pools_k)] * pp), *([_cols_view(pools_v)] * pp))
    return o[:, :t]


def _top_blocks_rows(imp, n_blk, n_sel):
    j_iota = lax.broadcasted_iota(I32, imp.shape, 1)
    rank = jnp.zeros(imp.shape, F32)
    for k in range(n_blk):
        col_k = imp[:, k:k + 1]
        earlier = jnp.where(j_iota > k, 1.0, 0.0)
        rank = rank + jnp.where(col_k > imp, 1.0, jnp.where(col_k == imp, earlier, 0.0))
    return jnp.where((rank < n_sel) & (j_iota < n_blk), 1.0, 0.0)


def _nsa_sample_kernel(pt_ref, q_ref, gate_ref, ksn_ref, vsn_ref, kwn_ref, vwn_ref, wink_ref, winv_ref,
                       wck_ref, wcv_ref, pewk_ref, pewv_ref, w2k_ref, w2v_ref, *rest, past, n_new, n_groups, rep, pp):
    kc_pages = rest[:pp]
    vc_pages = rest[pp:2 * pp]
    ks_pages = rest[2 * pp:3 * pp]
    vs_pages = rest[3 * pp:4 * pp]
    o_ref, kcs, vcs, kst, vst, kwt, vwt = rest[4 * pp:]
    p_step = pl.program_id(1)
    hd = q_ref.shape[-1]
    gd = n_groups * hd
    page = ks_pages[0].shape[-1]
    n_chunks = kcs.shape[0] // CMP_STRIDE
    n_cmp = n_chunks - CMP_BLOCK // CMP_STRIDE + 1
    n_blk = -(-(past + n_new) // SLC_BLOCK)
    w_buf = wink_ref.shape[-1]
    n_heads = n_groups * rep
    rows = rep * TQ

    for j in range(pp):
        off = pl.multiple_of((p_step * pp + j) * page, page)
        kcs[pl.ds(off, page), :] = kc_pages[j][0, 0].T
        vcs[pl.ds(off, page), :] = vc_pages[j][0, 0].T
        kst[:, pl.ds(off, page)] = ks_pages[j][0, 0].astype(BF16)
        vst[:, pl.ds(off, page)] = vs_pages[j][0, 0].astype(BF16)

    @pl.when(p_step == pl.num_programs(1) - 1)
    def _():
        kst[:, past:past + page] = ksn_ref[0]
        vst[:, past:past + page] = vsn_ref[0]
        kwt[:, 0:w_buf] = wink_ref[0, 0].astype(BF16)
        vwt[:, 0:w_buf] = winv_ref[0, 0].astype(BF16)
        kwt[:, w_buf:w_buf + LANE] = kwn_ref[0]
        vwt[:, w_buf:w_buf + LANE] = vwn_ref[0]

        tpos = past + jnp.minimum(lax.broadcasted_iota(I32, (TQ, 1), 0), n_new - 1)
        gates = jax.nn.sigmoid(gate_ref[0])

        def compress(rows_ref, wc_ref, pew_ref, w2_ref):
            a = jnp.zeros((n_chunks, 2 * gd), F32)
            for r in range(CMP_STRIDE):
                xr = rows_ref[pl.ds(r, n_chunks, stride=CMP_STRIDE), :].astype(BF16)
                a = a + jnp.dot(xr, wc_ref[r * gd:(r + 1) * gd, :], preferred_element_type=F32)
            out = _compress_finish(a, pew_ref, w2_ref, n_groups, hd)
            return [o.astype(BF16) for o in out]

        kcmp = compress(kcs, wck_ref, pewk_ref, w2k_ref)
        vcmp = compress(vcs, wcv_ref, pewv_ref, w2v_ref)

        c_idx = lax.broadcasted_iota(I32, (TQ, n_chunks), 1)
        c_ok = ((c_idx < n_cmp) & (c_idx * CMP_STRIDE + (CMP_BLOCK - 1) <= tpos))[None]
        cover = _block_cover(n_chunks)
        n_slc = kst.shape[1]
        col = lax.broadcasted_iota(I32, (TQ, n_slc), 1)
        blk_of_col = lax.broadcasted_iota(I32, (LANE, n_slc), 1) // SLC_BLOCK
        expand = jnp.where(lax.broadcasted_iota(I32, (LANE, n_slc), 0) == blk_of_col, 1.0, 0.0).astype(BF16)
        n_win = kwt.shape[1]
        w_c = lax.broadcasted_iota(I32, (TQ, n_win), 1)
        w_pos = past - w_buf + w_c
        w_d = tpos - w_pos
        w_ok = ((w_c < w_buf + n_new) & (w_pos >= 0) & (w_d >= 0) & (w_d < WINDOW))[None]

        for g in range(n_groups):
            q = q_ref[0, g * rows:(g + 1) * rows]
            gs = slice(g * hd, (g + 1) * hd)
            s = lax.dot_general(q, kcmp[g], (((1,), (1,)), ((), ())), preferred_element_type=F32)
            p_cmp = _softmax_rows(jnp.where(c_ok, s.reshape(rep, TQ, n_chunks), -jnp.inf))
            o_cmp = jnp.dot(p_cmp.reshape(rows, n_chunks).astype(BF16), vcmp[g], preferred_element_type=F32)
            imp = _block_importance(jnp.sum(p_cmp, axis=0), cover, tpos)
            sel_blocks = _top_blocks_rows(imp, n_blk, min(N_SLC, n_blk)).astype(BF16)
            hit = jnp.dot(sel_blocks, expand, preferred_element_type=F32)
            sel = (jnp.where(col <= tpos, hit, 0.0) > 0.5)[None]
            _, o_slc = _attend_cols(q, kst[gs, :], vst[gs, :], sel, rep)
            _, o_win = _attend_cols(q, kwt[gs, :], vwt[gs, :], w_ok, rep)
            for r in range(rep):
                h = g * rep + r
                rs = slice(r * TQ, (r + 1) * TQ)
                o_ref[0, :, h * hd:(h + 1) * hd] = (
                    gates[:, h:h + 1] * o_cmp[rs] + gates[:, n_heads + h:n_heads + h + 1] * o_slc[rs]
                    + gates[:, 2 * n_heads + h:2 * n_heads + h + 1] * o_win[rs])


def nsa_sample_attention(q, gate_logits, ks, vs, kw, vw, cmp_k, cmp_v, pools_kc, pools_vc, pools_ks, pools_vs,
                         wins_k, wins_v, layer, page_table):
    b, t, h, d = q.shape
    g = ks.shape[2]
    gd = g * d
    n_pages = page_table.shape[1]
    page = pools_ks.shape[2]
    past = n_pages * page
    pp = min(PAGES_PER_STEP, n_pages)
    assert n_pages % pp == 0 and t <= TQ and (past + t) // CMP_STRIDE * CMP_STRIDE == past
    cw = CMP_STRIDE * gd
    w_buf = wins_k.shape[2]

    wck, pewk, w2k = _compress_weights(*cmp_k, g, d)
    wcv, pewv, w2v = _compress_weights(*cmp_v, g, d)
    kern = functools.partial(_nsa_sample_kernel, past=past, n_new=t, n_groups=g, rep=h // g, pp=pp)
    win_spec = pl.BlockSpec((1, 1, gd, w_buf), lambda bb, p, pt: (layer, bb, 0, 0))
    o = pl.pallas_call(
        kern,
        grid_spec=pltpu.PrefetchScalarGridSpec(
            num_scalar_prefetch=1,
            grid=(b, n_pages // pp),
            in_specs=[_per_batch(h * TQ, d), _per_batch(TQ, 3 * h), _per_batch(gd, page), _per_batch(gd, page),
                      _per_batch(gd, LANE), _per_batch(gd, LANE), win_spec, win_spec,
                      _whole(cw, 2 * gd), _whole(cw, 2 * gd), _whole(1, d), _whole(1, d), _whole(d, d), _whole(d, d)]
                     + _page_specs((gd, page), layer, n_pages, pp) + _page_specs((gd, page), layer, n_pages, pp)
                     + _page_specs((gd, page), layer, n_pages, pp) + _page_specs((gd, page), layer, n_pages, pp),
            out_specs=_per_batch(TQ, h * d),
            scratch_shapes=[pltpu.VMEM((past, gd), F32), pltpu.VMEM((past, gd), F32),
                            pltpu.VMEM((gd, past + page), BF16), pltpu.VMEM((gd, past + page), BF16),
                            pltpu.VMEM((gd, w_buf + LANE), BF16), pltpu.VMEM((gd, w_buf + LANE), BF16)],
        ),
        out_shape=jax.ShapeDtypeStruct((b, TQ, h * d), F32),
        compiler_params=pltpu.CompilerParams(
            dimension_semantics=("parallel", "arbitrary"), vmem_limit_bytes=VMEM_LIMIT),
        name="nsa_sample_attention",
    )(page_table.reshape(-1).astype(I32), _head_rows(q),
      jnp.pad(gate_logits.astype(F32), ((0, 0), (0, TQ - t), (0, 0))),
      _new_cols(ks, page), _new_cols(vs, page), _new_cols(kw, LANE), _new_cols(vw, LANE),
      _cols_view(wins_k), _cols_view(wins_v), wck, wcv, pewk, pewv, w2k, w2v,
      *([_cols_view(pools_kc)] * pp), *([_cols_view(pools_vc)] * pp),
      *([_cols_view(pools_ks)] * pp), *([_cols_view(pools_vs)] * pp))
    return o[:, :t]


def last_rows(a, n):
    t = a.shape[1]
    if t >= n:
        return a[:, t - n:]
    return jnp.pad(a, ((0, 0), (n - t, 0)) + ((0, 0),) * (a.ndim - 2))


def mem_kv(mem, g, w_kv):
    b, m, _ = mem.shape
    k, v = jnp.split(mm(mem, w_kv, g), 2, axis=-1)
    return k.reshape(b, m, X_HEADS, X_HEAD_DIM), v.reshape(b, m, X_HEADS, X_HEAD_DIM)


def _xattn_kernel(x_ref, g_ref, wq_ref, mk_ref, mv_ref, wo_ref, o_ref, *, n_heads, hd):
    xn = _rms(x_ref[...], g_ref[...]).astype(BF16)
    q = jnp.dot(xn, wq_ref[...], preferred_element_type=F32).astype(BF16)
    mk = mk_ref[0].astype(BF16)
    mv = mv_ref[0].astype(BF16)
    outs = []
    for h in range(n_heads):
        hs = slice(h * hd, (h + 1) * hd)
        s = lax.dot_general(q[:, hs], mk[:, hs], (((1,), (1,)), ((), ())), preferred_element_type=F32) * (hd ** -0.5)
        e = jnp.exp(s - jnp.max(s, axis=-1, keepdims=True))
        p = e / jnp.sum(e, axis=-1, keepdims=True)
        outs.append(jnp.dot(p.astype(BF16), mv[:, hs], preferred_element_type=F32).astype(BF16))
    a = jnp.concatenate(outs, axis=1)
    for c0 in range(0, o_ref.shape[1], MM_COLS):
        cs = slice(c0, c0 + MM_COLS)
        o_ref[:, cs] = x_ref[:, cs] + jnp.dot(a, wo_ref[:, cs], preferred_element_type=F32)


def cross_attn_rows(x, g, mk, mv, w_q, w_o):
    b, t, d = x.shape
    mlen, n_heads, hd = mk.shape[1:]
    hw = n_heads * hd
    tm = _pick_tile(t, (512, 256, 128, 64, 32, 16, 8))
    steps = t // tm
    rows = pl.BlockSpec((tm, d), lambda i: (i, 0))
    mem = pl.BlockSpec((1, mlen, hw), lambda i: (i // steps, 0, 0))
    out = pl.pallas_call(
        functools.partial(_xattn_kernel, n_heads=n_heads, hd=hd),
        grid=(b * steps,),
        in_specs=[rows, _resident((1, d)), _resident((d, hw)), mem, mem, _resident((hw, d))],
        out_specs=rows,
        out_shape=jax.ShapeDtypeStruct((b * t, d), F32),
        compiler_params=pltpu.CompilerParams(dimension_semantics=("parallel",), vmem_limit_bytes=VMEM_LIMIT),
        name="cross_attn_rows",
    )(x.reshape(b * t, d), g.reshape(1, d).astype(F32), w_q.astype(BF16), mk.reshape(b, mlen, hw),
      mv.reshape(b, mlen, hw), w_o.astype(BF16))
    return out.reshape(b, t, d)


def cross_attn(x, g, mk, mv, w_q, w_o):
    b, t, _ = x.shape
    if t % SUBLANE == 0:
        return cross_attn_rows(x, g, mk, mv, w_q, w_o)
    q = mm(x, w_q, g).reshape(b, t, X_HEADS, X_HEAD_DIM)
    s = jnp.einsum('bthd,bmhd->bhtm', q, mk).astype(F32) * (X_HEAD_DIM ** -0.5)
    p = jax.nn.softmax(s, axis=-1).astype(x.dtype)
    return mm(jnp.einsum('bhtm,bmhd->bthd', p, mv).reshape(b, t, -1), w_o, res=x)


A_WIDTHS = (A_HEADS * HEAD_DIM, A_KV_HEADS * HEAD_DIM, A_KV_HEADS * HEAD_DIM, IDX_HEADS * IDX_DIM, IDX_DIM, IDX_HEADS)


def dsa_project(x, g, w_in, pos, keys_transposed):
    b, t, d = x.shape
    keys = [(F32, False), (BF16, keys_transposed)]
    segments = [(True, HEAD_DIM ** -0.5, [(BF16, False)]), (True, None, keys), (False, None, [(F32, False), (BF16, False)]),
                (True, None, [(BF16, False)]), (True, None, keys),
                (False, IDX_HEADS ** -0.5 * IDX_DIM ** -0.5, [(F32, False)])]
    return project(x.reshape(b * t, d), g, _split_cols(w_in, A_WIDTHS), segments, jnp.tile(pos, b))


def dsa_prompt(x, g, w_in, w_out):
    b, s, _ = x.shape
    q, k, kt, v, vb, iq, ik, ikt, iw = dsa_project(x, g, w_in, jnp.arange(s), True)
    o = dsa_prompt_attention(q.reshape(b, s, -1), kt, vb.reshape(b, s, -1), iq.reshape(b, s, -1), ikt,
                             iw.reshape(b, s, -1), min(TOPK_MAX, s // 4), A_KV_HEADS, IDX_HEADS)
    state = (k.reshape(b, s, A_KV_HEADS, HEAD_DIM), v.reshape(b, s, A_KV_HEADS, HEAD_DIM), ik.reshape(b, s, IDX_DIM))
    return mm(o, w_out, res=x), state


def dsa_sample(x, g, w_in, w_out, pools_k, pools_v, pools_idx, layer, page_table):
    b, t, _ = x.shape
    past = page_table.shape[1] * pools_k.shape[2]
    q, k, kb, v, vb, iq, ik, ikb, iw = dsa_project(x, g, w_in, past + jnp.arange(t), False)
    o = dsa_sample_attention(q.reshape(b, t, A_HEADS, HEAD_DIM), kb.reshape(b, t, A_KV_HEADS, HEAD_DIM),
                             vb.reshape(b, t, A_KV_HEADS, HEAD_DIM), iq.reshape(b, t, IDX_HEADS, IDX_DIM),
                             ikb.reshape(b, t, IDX_DIM), iw.reshape(b, t, IDX_HEADS), pools_k, pools_v, pools_idx,
                             layer, page_table, min(TOPK_MAX, (past + t) // 4))
    state = (k.reshape(b, t, A_KV_HEADS, HEAD_DIM), v.reshape(b, t, A_KV_HEADS, HEAD_DIM), ik.reshape(b, t, IDX_DIM))
    return mm(o, w_out, res=x), state


def _shift_rows(x, prev, k):
    return jnp.concatenate([prev[prev.shape[0] - k:], x[:x.shape[0] - k]], axis=0)


def _tile_transpose(m, rows_out, cols_out):
    r, c = m.shape
    tile = m
    if c < LANE:
        tile = jnp.concatenate([tile, jnp.zeros((r, LANE - c), m.dtype)], axis=1)
    if r < LANE:
        tile = jnp.concatenate([tile, jnp.zeros((LANE - r, LANE), m.dtype)], axis=0)
    return tile.T[:rows_out, :cols_out]


def _ssd_kernel(z_ref, x_ref, dt_ref, cw_ref, cb_ref, dtb_ref, alog_ref, dskip_ref, buf_ref, h0_ref,
                y_ref, hout_ref, carry_ref, ht_ref, *, n_heads, n_groups, hd, n_state):
    c = pl.program_id(1)
    l = x_ref.shape[0]
    d_inner = n_heads * hd
    rep = n_heads // n_groups

    @pl.when(c == 0)
    def _():
        carry_ref[...] = buf_ref[0]
        for h in range(n_heads):
            ht_ref[h] = _tile_transpose(h0_ref[0, h], n_state, hd)

    x = x_ref[...]
    prev = carry_ref[...]
    conv = cb_ref[...] + x * cw_ref[B_CONV - 1:B_CONV, :]
    for k in range(1, B_CONV):
        conv = conv + _shift_rows(x, prev, k) * cw_ref[B_CONV - 1 - k:B_CONV - k, :]
    carry_ref[...] = x[l - SUBLANE:]
    act = jax.nn.silu(conv)
    gn = n_groups * n_state
    xs, bm, cm = act[:, :d_inner], act[:, d_inner:d_inner + gn], act[:, d_inner + gn:]

    dt = jax.nn.softplus(dt_ref[...] + dtb_ref[...])
    a = -jnp.exp(alog_ref[...])
    acs = dt * a
    k = 1
    while k < l:
        acs = acs + jnp.concatenate([jnp.zeros((k, n_heads), F32), acs[:l - k]], axis=0)
        k *= 2
    acs_t = jnp.concatenate([acs, jnp.zeros((l, LANE - n_heads), F32)], axis=1).T
    last = acs[l - 1:l, :]
    to_end = jnp.exp(last - acs)
    grow = jnp.exp(acs)
    chunk_decay = jnp.exp(last)
    causal = lax.broadcasted_iota(I32, (l, l), 0) >= lax.broadcasted_iota(I32, (l, l), 1)
    z = z_ref[...]
    dskip = dskip_ref[...]

    for g in range(n_groups):
        bm_g = bm[:, g * n_state:(g + 1) * n_state]
        cm_g = cm[:, g * n_state:(g + 1) * n_state].astype(BF16)
        cb = lax.dot_general(cm_g, bm_g.astype(BF16), (((1,), (1,)), ((), ())), preferred_element_type=F32)
        bm_t = bm_g.T.astype(BF16)
        for r in range(rep):
            h = g * rep + r
            hs = slice(h * hd, (h + 1) * hd)
            xs_h = xs[:, hs]
            xr = xs_h * dt[:, h:h + 1]
            seg = acs[:, h:h + 1] - acs_t[h:h + 1, :]
            decay = jnp.where(causal, jnp.exp(jnp.where(causal, seg, 0.0)), 0.0)
            y = jnp.dot((cb * decay).astype(BF16), xr.astype(BF16), preferred_element_type=F32)
            ht = ht_ref[h]
            y = y + jnp.dot(cm_g, ht.astype(BF16), preferred_element_type=F32) * grow[:, h:h + 1]
            upd = jnp.dot(bm_t, (xr * to_end[:, h:h + 1]).astype(BF16), preferred_element_type=F32)
            ht_ref[h] = ht * chunk_decay[:, h:h + 1] + upd
            y = y + xs_h * dskip[:, hs]
            y_ref[:, hs] = y * jax.nn.silu(z[:, hs])

    @pl.when(c == pl.num_programs(1) - 1)
    def _():
        for h in range(n_heads):
            hout_ref[0, h] = _tile_transpose(ht_ref[h], hd, n_state)


def ssd_prompt(z, xbc, dt, conv_w, conv_b, dt_bias, a_log, d_skip, conv_buf, ssm0, n_groups):
    b, n_heads, hd, n_state = ssm0.shape
    m, w = xbc.shape
    s = m // b
    l = min(B_CHUNK, s)
    nc = s // l
    d_inner = n_heads * hd
    buf = jnp.pad(conv_buf.astype(F32), ((0, 0), (SUBLANE - (B_CONV - 1), 0), (0, 0)))
    dskip_full = jnp.repeat(d_skip.astype(F32), hd).reshape(1, d_inner)
    rows = lambda width: pl.BlockSpec((l, width), lambda bb, c: (bb * nc + c, 0))
    state = pl.BlockSpec((1, n_heads, hd, n_state), lambda bb, c: (bb, 0, 0, 0))
    kern = functools.partial(_ssd_kernel, n_heads=n_heads, n_groups=n_groups, hd=hd, n_state=n_state)
    return pl.pallas_call(
        kern,
        grid=(b, nc),
        in_specs=[rows(d_inner), rows(w), rows(n_heads), _resident((B_CONV, w)), _resident((1, w)),
                  _resident((1, n_heads)), _resident((1, n_heads)), _resident((1, d_inner)),
                  pl.BlockSpec((1, SUBLANE, w), lambda bb, c: (bb, 0, 0)), state],
        out_specs=[rows(d_inner), state],
        out_shape=[jax.ShapeDtypeStruct((m, d_inner), F32), jax.ShapeDtypeStruct(ssm0.shape, F32)],
        scratch_shapes=[pltpu.VMEM((SUBLANE, w), F32), pltpu.VMEM((n_heads, n_state, hd), F32)],
        compiler_params=pltpu.CompilerParams(
            dimension_semantics=("parallel", "arbitrary"), vmem_limit_bytes=VMEM_LIMIT),
        name="ssd_prompt",
    )(z, xbc, dt, conv_w.astype(F32), conv_b.reshape(1, w).astype(F32), dt_bias.reshape(1, n_heads).astype(F32),
      a_log.reshape(1, n_heads).astype(F32), dskip_full, buf, ssm0.astype(F32))


def mamba_prompt(x, g, w_in, conv_w, conv_b, dt_bias, a_log, d_skip, g_norm, w_out, conv_buf, ssm0):
    b, s, d = x.shape
    plain = (False, None, [(F32, False)])
    z, xbc, dt = project(x.reshape(b * s, d), g, _split_cols(w_in, (B_D_INNER, B_CONV_DIM, B_HEADS)),
                         [plain, plain, plain], jnp.zeros((b * s,), I32))
    y, ssm = ssd_prompt(z, xbc, dt, conv_w, conv_b, dt_bias, a_log, d_skip, conv_buf, ssm0, B_GROUPS)
    xpad = jnp.concatenate([conv_buf.astype(F32), xbc.reshape(b, s, -1)[:, s - (B_CONV - 1):]], axis=1)
    new_buf = xpad[:, xpad.shape[1] - (B_CONV - 1):]
    return mm(y.reshape(b, s, -1), w_out, g_norm, res=x), new_buf, ssm.astype(ssm0.dtype)


def ssd_chunked(x, dt, a, bm, cm, h0):
    bsz, t, nh, p = x.shape
    g, n = bm.shape[2], bm.shape[3]
    r = nh // g
    l = min(B_CHUNK, t)
    nc = t // l
    xr = (x * dt[..., None]).reshape(bsz, nc, l, g, r, p)
    acs = jnp.cumsum((dt * a).reshape(bsz, nc, l, g, r), axis=2)
    br = bm.reshape(bsz, nc, l, g, n)
    cr = cm.reshape(bsz, nc, l, g, n)
    seg = acs[:, :, :, None] - acs[:, :, None, :]
    causal = jnp.tril(jnp.ones((l, l), bool))[:, :, None, None]
    decay = jnp.where(causal, jnp.exp(jnp.where(causal, seg, 0.0)), 0.0)
    cb = jnp.einsum('bcign,bcjgn->bcijg', cr, br)
    y_intra = jnp.einsum('bcijg,bcijgr,bcjgrp->bcigrp', cb, decay, xr)
    to_end = jnp.exp(acs[:, :, -1:] - acs)
    s_chunk = jnp.einsum('bclgn,bclgr,bclgrp->bcgrpn', br, to_end, xr)
    d_chunk = jnp.exp(acs[:, :, -1])

    def step(hc, inp):
        s_c, d_c = inp
        return hc * d_c[..., None, None] + s_c, hc

    h_fin, h_in = lax.scan(step, h0.reshape(bsz, g, r, p, n),
                           (jnp.moveaxis(s_chunk, 1, 0), jnp.moveaxis(d_chunk, 1, 0)))
    h_in = jnp.moveaxis(h_in, 0, 1)
    y_inter = jnp.einsum('bcign,bcigr,bcgrpn->bcigrp', cr, jnp.exp(acs), h_in)
    return (y_intra + y_inter).reshape(bsz, t, nh, p), h_fin.reshape(bsz, nh, p, n)


def mamba_mixer(x, g, w_in, conv_w, conv_b, dt_bias, a_log, d_skip, g_norm, w_out, conv_buf, ssm0):
    b, t, _ = x.shape
    proj = mm(x, w_in, g)
    z = proj[..., :B_D_INNER]
    xbc = proj[..., B_D_INNER:B_D_INNER + B_CONV_DIM]
    dt = proj[..., B_D_INNER + B_CONV_DIM:]
    xpad = jnp.concatenate([conv_buf.astype(xbc.dtype), xbc], axis=1)
    conv = conv_b + sum(xpad[:, j:j + t] * conv_w[j] for j in range(B_CONV))
    xbc = jax.nn.silu(conv)
    new_buf = xpad[:, t:]
    gn = B_GROUPS * B_STATE
    xs = xbc[..., :B_D_INNER].reshape(b, t, B_HEADS, B_HEADDIM).astype(F32)
    bm = xbc[..., B_D_INNER:B_D_INNER + gn].reshape(b, t, B_GROUPS, B_STATE).astype(F32)
    cm = xbc[..., B_D_INNER + gn:].reshape(b, t, B_GROUPS, B_STATE).astype(F32)
    dt = jax.nn.softplus(dt.astype(F32) + dt_bias.astype(F32))
    a = -jnp.exp(a_log.astype(F32))
    y, ssm = ssd_chunked(xs, dt, a, bm, cm, ssm0.astype(F32))
    y = (y + xs * d_skip.astype(F32)[:, None]).reshape(b, t, B_D_INNER).astype(x.dtype)
    y = y * jax.nn.silu(z)
    return mm(y, w_out, g_norm, res=x), new_buf, ssm.astype(ssm0.dtype)


C_KV_WIDTH = C_KV_HEADS * HEAD_DIM
C_WIDTHS = (C_HEADS * HEAD_DIM,) + (C_KV_WIDTH,) * 6 + (3 * C_HEADS,)


def nsa_project(x, g, w_in, pos, keys_transposed):
    b, t, d = x.shape
    keys = (True, None, [(F32, False), (BF16, keys_transposed)])
    vals = (False, None, [(F32, False), (BF16, False)])
    segments = [(True, HEAD_DIM ** -0.5, [(BF16, False)]), (True, None, [(F32, False)]), (False, None, [(F32, False)]),
                keys, vals, keys, vals, (False, None, [(F32, False)])]
    return project(x.reshape(b * t, d), g, _split_cols(w_in, C_WIDTHS), segments, jnp.tile(pos, b))


def nsa_prompt(x, g, w_in, w_out, cmp_k, cmp_v, w_buf):
    b, s, _ = x.shape
    q, kc, vc, ks, kst, vs, vsb, kw, kwt, vw, vwb, gates = nsa_project(x, g, w_in, jnp.arange(s), True)
    seq = lambda a: a.reshape(b, s, -1)
    kct = compress_rows(seq(kc), cmp_k, C_KV_HEADS, True)
    vcc = compress_rows(seq(vc), cmp_v, C_KV_HEADS, False)
    n_cmp = s // CMP_STRIDE - CMP_BLOCK // CMP_STRIDE + 1
    o = nsa_prompt_attention(seq(q), seq(gates), kct, vcc, n_cmp, kst, seq(vsb), kwt, seq(vwb), C_KV_HEADS)
    heads = lambda a: a.reshape(b, s, C_KV_HEADS, HEAD_DIM)
    state = (heads(kc), heads(vc), heads(ks), heads(vs), last_rows(heads(kw), w_buf), last_rows(heads(vw), w_buf))
    return mm(o, w_out, res=x), state


def nsa_sample(x, g, w_in, w_out, cmp_k, cmp_v, pools_kc, pools_vc, pools_ks, pools_vs, wins_k, wins_v, layer,
               page_table):
    b, t, _ = x.shape
    past = page_table.shape[1] * pools_kc.shape[2]
    q, kc, vc, ks, ksb, vs, vsb, kw, kwb, vw, vwb, gates = nsa_project(x, g, w_in, past + jnp.arange(t), False)
    heads = lambda a: a.reshape(b, t, C_KV_HEADS, HEAD_DIM)
    o = nsa_sample_attention(q.reshape(b, t, C_HEADS, HEAD_DIM), gates.reshape(b, t, -1), heads(ksb), heads(vsb),
                             heads(kwb), heads(vwb), cmp_k, cmp_v, pools_kc, pools_vc, pools_ks, pools_vs, wins_k,
                             wins_v, layer, page_table)
    win_k, win_v = wins_k[layer], wins_v[layer]
    w_buf = win_k.shape[1]
    kwin = jnp.concatenate([win_k, heads(kw)], axis=1)
    vwin = jnp.concatenate([win_v, heads(vw)], axis=1)
    state = (heads(kc), heads(vc), heads(ks), heads(vs), last_rows(kwin, w_buf), last_rows(vwin, w_buf))
    return mm(o, w_out, res=x), state


def kernel(x_prompt, x_sample, cache_a_k, cache_a_v, cache_a_idx, state_b_ssm, state_b_conv, cache_c_cmp_k, cache_c_cmp_v, cache_c_slc_k, cache_c_slc_v, cache_c_win_k, cache_c_win_v, cache_mem_k, cache_mem_v, page_table, mem_prompt, g_ffn1, ffn1_wi, ffn1_wo, g_mix, g_xattn, g_mem, x_w_q, x_w_kv, x_w_o, g_ffn2, ffn2_wi, ffn2_wo, g_final, a_w_in, a_w_out, b_w_in, b_conv_w, b_conv_b, b_dt_bias, b_a_log, b_d_skip, b_g_norm, b_w_out, c_w_in, c_w_out, c_pe_k, c_w1_k, c_w2_k, c_pe_v, c_w1_v, c_w2_v):
    xp, xs = x_prompt, x_sample
    w_buf = cache_c_win_k.shape[2]
    ak_p, av_p, ai_p, bs_p, bc_p = [], [], [], [], []
    cck_p, ccv_p, csk_p, csv_p, cwk_p, cwv_p = [], [], [], [], [], []
    mk_p, mv_p = [], []
    ak_s, av_s, ai_s, bs_s, bc_s = [], [], [], [], []
    cck_s, ccv_s, csk_s, csv_s, cwk_s, cwv_s = [], [], [], [], [], []
    ia = ib = ic = 0
    for i in range(DEPTH):
        xp = ffn_residual(xp, g_ffn1[i], ffn1_wi[i], ffn1_wo[i])
        xs = ffn_residual(xs, g_ffn1[i], ffn1_wi[i], ffn1_wo[i])
        kind = i % N_MIXERS
        if kind == 0:
            op, (k1, v1, i1) = dsa_prompt(xp, g_mix[i], a_w_in[ia], a_w_out[ia])
            os_, (k2, v2, i2) = dsa_sample(xs, g_mix[i], a_w_in[ia], a_w_out[ia], cache_a_k, cache_a_v, cache_a_idx,
                                           ia, page_table)
            ak_p.append(k1); av_p.append(v1); ai_p.append(i1)
            ak_s.append(k2); av_s.append(v2); ai_s.append(i2)
            ia += 1
        elif kind == 1:
            conv0 = jnp.zeros((xp.shape[0], B_CONV - 1, B_CONV_DIM), xp.dtype)
            ssm0 = jnp.zeros((xp.shape[0], B_HEADS, B_HEADDIM, B_STATE), F32)
            wts = (b_w_in[ib], b_conv_w[ib], b_conv_b[ib], b_dt_bias[ib], b_a_log[ib], b_d_skip[ib],
                   b_g_norm[ib], b_w_out[ib])
            op, c1, s1 = mamba_prompt(xp, g_mix[i], *wts, conv0, ssm0)
            os_, c2, s2 = mamba_mixer(xs, g_mix[i], *wts, state_b_conv[ib], state_b_ssm[ib])
            bc_p.append(c1); bs_p.append(s1); bc_s.append(c2); bs_s.append(s2)
            ib += 1
        else:
            cmp_k = (c_pe_k[ic], c_w1_k[ic], c_w2_k[ic])
            cmp_v = (c_pe_v[ic], c_w1_v[ic], c_w2_v[ic])
            op, st1 = nsa_prompt(xp, g_mix[i], c_w_in[ic], c_w_out[ic], cmp_k, cmp_v, w_buf)
            os_, st2 = nsa_sample(xs, g_mix[i], c_w_in[ic], c_w_out[ic], cmp_k, cmp_v, cache_c_cmp_k, cache_c_cmp_v,
                                  cache_c_slc_k, cache_c_slc_v, cache_c_win_k, cache_c_win_v, ic, page_table)
            for lst, arr in zip((cck_p, ccv_p, csk_p, csv_p, cwk_p, cwv_p), st1):
                lst.append(arr)
            for lst, arr in zip((cck_s, ccv_s, csk_s, csv_s, cwk_s, cwv_s), st2):
                lst.append(arr)
            ic += 1
        xp, xs = op, os_
        mk, mv = mem_kv(mem_prompt, g_mem[i], x_w_kv[i])
        mk_p.append(mk); mv_p.append(mv)
        xp = cross_attn(xp, g_xattn[i], mk, mv, x_w_q[i], x_w_o[i])
        xs = cross_attn(xs, g_xattn[i], cache_mem_k[i], cache_mem_v[i], x_w_q[i], x_w_o[i])
        xp = ffn_residual(xp, g_ffn2[i], ffn2_wi[i], ffn2_wo[i])
        xs = ffn_residual(xs, g_ffn2[i], ffn2_wi[i], ffn2_wo[i])
    y_prompt = rms_norm_rows(xp, g_final)
    y_sample = rms_norm_rows(xs, g_final)
    st = jnp.stack
    return (y_prompt, y_sample,
            st(ak_p), st(av_p), st(ai_p), st(bs_p), st(bc_p),
            st(cck_p), st(ccv_p), st(csk_p), st(csv_p), st(cwk_p), st(cwv_p),
            st(mk_p), st(mv_p),
            st(ak_s), st(av_s), st(ai_s), st(bs_s), st(bc_s),
            st(cck_s), st(ccv_s), st(csk_s), st(csv_s), st(cwk_s), st(cwv_s))
```

```python
import functools

import jax
import jax.numpy as jnp
import numpy as np
from jax import lax
from jax.experimental import pallas as pl
from jax.experimental.pallas import tpu as pltpu

F32 = jnp.float32
BF16 = jnp.bfloat16
I32 = jnp.int32
INT_MIN = -2 ** 31
EPS = 1e-6

D_MODEL = 1024
DEPTH = 4
N_MIXERS = 3
HEAD_DIM = 64
ROPE_THETA = 10000.0
Q_BLOCK = 128
A_HEADS = D_MODEL // HEAD_DIM
A_KV_HEADS = 4
IDX_HEADS = 8
IDX_DIM = 64
TOPK_MAX = 256
B_D_INNER = 2 * D_MODEL
B_HEADDIM = 64
B_HEADS = B_D_INNER // B_HEADDIM
B_GROUPS = 4
B_STATE = 128
B_CONV = 4
B_CONV_DIM = B_D_INNER + 2 * B_GROUPS * B_STATE
B_CHUNK = 128
C_HEADS = D_MODEL // HEAD_DIM
C_KV_HEADS = 2
CMP_BLOCK = 32
CMP_STRIDE = 16
SLC_BLOCK = 64
N_SLC = 16
WINDOW = 512
X_HEADS = 4
X_HEAD_DIM = 128
D_FF = 2816

LANE = 128
SUBLANE = 8
VMEM_LIMIT = 48 * 1024 * 1024
MM_COLS = 512
FF_CHUNK = 256
ATTN_COLS = 1024
TQ = SUBLANE
PAGES_PER_STEP = 16


def _pick_tile(n, candidates):
    for c in candidates:
        if n % c == 0:
            return c
    return n


def _resident(shape):
    return pl.BlockSpec(shape, lambda *_: (0,) * len(shape), pipeline_mode=pl.Buffered(1))


def _rms(x, g):
    return x * lax.rsqrt(jnp.mean(x * x, axis=-1, keepdims=True) + EPS) * g


def _nmm_kernel(x_ref, g_ref, w_ref, *rest, norm, residual):
    o_ref = rest[-1]
    x = x_ref[...]
    if norm:
        x = _rms(x, g_ref[...])
    xb = x.astype(BF16)
    n = w_ref.shape[1]
    for c0 in range(0, n, MM_COLS):
        c1 = min(c0 + MM_COLS, n)
        y = jnp.dot(xb, w_ref[:, c0:c1], preferred_element_type=F32)
        o_ref[:, c0:c1] = rest[0][:, c0:c1] + y if residual else y


def norm_matmul(x, w, g=None, res=None):
    m, k = x.shape
    n = w.shape[1]
    assert res is None or (n % LANE == 0 and m % SUBLANE == 0)
    n_pad = -(-n // LANE) * LANE
    wb = w.astype(BF16)
    if n_pad != n:
        wb = jnp.pad(wb, ((0, 0), (0, n_pad - n)))
    m_orig = m
    if m % SUBLANE != 0:
        m = -(-m // 256) * 256
        x = jnp.pad(x, ((0, m - m_orig), (0, 0)))
    tm = _pick_tile(m, (256, 128, 64, 32, 16, 8))
    norm = g is not None
    gg = (g if norm else jnp.ones((k,), F32)).reshape(1, k).astype(F32)
    row_block = lambda width: pl.BlockSpec((tm, width), lambda i: (i, 0))
    extra = [] if res is None else [res]
    out = pl.pallas_call(
        functools.partial(_nmm_kernel, norm=norm, residual=res is not None),
        grid=(m // tm,),
        in_specs=[row_block(k), _resident((1, k)), _resident((k, n_pad))] + [row_block(n_pad) for _ in extra],
        out_specs=row_block(n_pad),
        out_shape=jax.ShapeDtypeStruct((m, n_pad), F32),
        compiler_params=pltpu.CompilerParams(dimension_semantics=("parallel",), vmem_limit_bytes=VMEM_LIMIT),
        name="norm_matmul",
    )(x, gg, wb, *extra)
    if n_pad != n or m != m_orig:
        out = out[:m_orig, :n]
    return out


def mm(x, w, g=None, res=None):
    lead = x.shape[:-1]
    r2 = None if res is None else res.reshape(-1, res.shape[-1])
    return norm_matmul(x.reshape(-1, x.shape[-1]), w, g, r2).reshape(lead + (w.shape[1],))


def _ffn_kernel(x_ref, g_ref, wg_ref, wu_ref, wo_ref, o_ref, act_ref, *, n_chunks):
    xn = _rms(x_ref[...], g_ref[...]).astype(BF16)

    for c in range(n_chunks):
        cs = slice(c * FF_CHUNK, (c + 1) * FF_CHUNK)
        hg = jnp.dot(xn, wg_ref[:, cs], preferred_element_type=F32)
        hu = jnp.dot(xn, wu_ref[:, cs], preferred_element_type=F32)
        act_ref[:, cs] = (jax.nn.silu(hg) * hu).astype(BF16)
    for c0 in range(0, o_ref.shape[1], MM_COLS):
        cs = slice(c0, c0 + MM_COLS)
        o_ref[:, cs] = x_ref[:, cs] + 0.5 * jnp.dot(act_ref[...], wo_ref[:, cs], preferred_element_type=F32)


def ffn_residual(x, g, wi, wo):
    lead, d = x.shape[:-1], x.shape[-1]
    x2 = x.reshape(-1, d)
    m = x2.shape[0]
    f = wo.shape[0]
    assert f % FF_CHUNK == 0 and d % MM_COLS == 0
    tm = _pick_tile(m, (512, 256, 128, 64, 32, 16, 8))
    wb = wi.astype(BF16)
    out = pl.pallas_call(
        functools.partial(_ffn_kernel, n_chunks=f // FF_CHUNK),
        grid=(m // tm,),
        in_specs=[pl.BlockSpec((tm, d), lambda i: (i, 0)), _resident((1, d)), _resident((d, f)), _resident((d, f)),
                  _resident((f, d))],
        out_specs=pl.BlockSpec((tm, d), lambda i: (i, 0)),
        out_shape=jax.ShapeDtypeStruct((m, d), F32),
        scratch_shapes=[pltpu.VMEM((tm, f), BF16)],
        compiler_params=pltpu.CompilerParams(dimension_semantics=("parallel",), vmem_limit_bytes=VMEM_LIMIT),
        name="ffn_residual",
    )(x2, g.reshape(1, d).astype(F32), wb[:, :f], wb[:, f:], wo.astype(BF16))
    return out.reshape(lead + (d,))


def _rmsnorm_kernel(x_ref, g_ref, o_ref):
    o_ref[...] = _rms(x_ref[...], g_ref[...])


def rms_norm_rows(x, g):
    lead, d = x.shape[:-1], x.shape[-1]
    x2 = x.reshape(-1, d)
    m = x2.shape[0]
    tm = _pick_tile(m, (512, 256, 128, 64, 32, 16, 8))
    row_block = pl.BlockSpec((tm, d), lambda i: (i, 0))
    out = pl.pallas_call(
        _rmsnorm_kernel, grid=(m // tm,), in_specs=[row_block, _resident((1, d))], out_specs=row_block,
        out_shape=jax.ShapeDtypeStruct((m, d), F32),
        compiler_params=pltpu.CompilerParams(dimension_semantics=("parallel",)), name="rms_norm_rows",
    )(x2, g.reshape(1, d).astype(F32))
    return out.reshape(lead + (d,))


def rope_tables(pos):
    half = HEAD_DIM // 2
    inv = ROPE_THETA ** (-jnp.arange(half, dtype=F32) / half)
    ang = pos.astype(F32)[:, None] * inv[None, :]
    cos, sin = jnp.cos(ang), jnp.sin(ang)
    reps = LANE // HEAD_DIM
    return (jnp.tile(jnp.concatenate([cos, cos], axis=1), (1, reps)),
            jnp.tile(jnp.concatenate([-sin, sin], axis=1), (1, reps)))


def _rope_cols(y, cos, sin):
    w = y.shape[1]
    half = HEAD_DIM // 2
    lane = lax.broadcasted_iota(I32, y.shape, 1)
    partner = jnp.where(lane % HEAD_DIM < half, pltpu.roll(y, w - half, axis=1), pltpu.roll(y, half, axis=1))
    reps = w // LANE
    return y * jnp.tile(cos, (1, reps)) + partner * jnp.tile(sin, (1, reps))


def _proj_kernel(x_ref, g_ref, cos_ref, sin_ref, w_ref, *out_refs, segments):
    xb = _rms(x_ref[...], g_ref[...]).astype(BF16)
    cos, sin = cos_ref[...], sin_ref[...]
    k = 0
    for start, width, rope, scale, outs in segments:
        y = jnp.dot(xb, w_ref[:, start:start + width], preferred_element_type=F32)
        if rope:
            y = _rope_cols(y, cos, sin)
        if scale is not None:
            y = y * scale
        for used, dtype, transposed in outs:
            v = y[:, :used]
            out_refs[k][...] = (v.T if transposed else v).astype(dtype)
            k += 1


def project(x, g, w_cols, segments, pos_rows):
    m, kdim = x.shape
    packed, segs, out_shapes, out_specs = [], [], [], []
    tm = _pick_tile(m, (256, 128, 64, 32, 16, 8))
    start = 0
    for wc, (rope, scale, outs) in zip(w_cols, segments):
        n = wc.shape[1]
        n_pad = -(-n // LANE) * LANE
        packed.append(jnp.pad(wc.astype(BF16), ((0, 0), (0, n_pad - n))))
        segs.append((start, n_pad, rope, scale, tuple((n, dt, tr) for dt, tr in outs)))
        for dt, tr in outs:
            if tr:
                out_shapes.append(jax.ShapeDtypeStruct((n, m), dt))
                out_specs.append(pl.BlockSpec((n, tm), lambda i: (0, i)))
            else:
                out_shapes.append(jax.ShapeDtypeStruct((m, n), dt))
                out_specs.append(pl.BlockSpec((tm, n), lambda i: (i, 0)))
        start += n_pad
    wb = jnp.concatenate(packed, axis=1)
    cos, sin = rope_tables(pos_rows)
    row_block = lambda width: pl.BlockSpec((tm, width), lambda i: (i, 0))
    return pl.pallas_call(
        functools.partial(_proj_kernel, segments=tuple(segs)),
        grid=(m // tm,),
        in_specs=[row_block(kdim), _resident((1, kdim)), row_block(LANE), row_block(LANE), _resident(wb.shape)],
        out_specs=out_specs,
        out_shape=out_shapes,
        compiler_params=pltpu.CompilerParams(dimension_semantics=("parallel",), vmem_limit_bytes=VMEM_LIMIT),
        name="project",
    )(x, g.reshape(1, kdim).astype(F32), cos, sin, wb)


def _split_cols(w, widths):
    cols = np.cumsum((0,) + tuple(widths))
    return [w[:, cols[j]:cols[j + 1]] for j in range(len(widths))]


def _softmax_rows(s):
    m = jnp.max(s, axis=-1, keepdims=True)
    e = jnp.exp(s - jnp.where(m == -jnp.inf, 0.0, m))
    den = jnp.sum(e, axis=-1, keepdims=True)
    return e / jnp.where(den > 0, den, 1.0)


def _flash_step(carry, q, kt_tile, v_tile, sel, rep):
    m, l, acc = carry
    rows, tk = q.shape[0], kt_tile.shape[1]
    s = jnp.dot(q, kt_tile, preferred_element_type=F32)
    s = jnp.where(sel, s.reshape(rep, Q_BLOCK, tk), -jnp.inf).reshape(rows, tk)
    m_new = jnp.maximum(m, jnp.max(s, axis=1, keepdims=True))
    m_safe = jnp.where(m_new == -jnp.inf, 0.0, m_new)
    alpha = jnp.exp(m - m_safe)
    p = jnp.exp(s - m_safe)
    l = l * alpha + jnp.sum(p, axis=1, keepdims=True)
    pv = jnp.dot(p.astype(BF16), v_tile, preferred_element_type=F32)
    return m_new, l, acc * alpha + pv


def _flash_init(rows, hd):
    return (jnp.full((rows, 1), -jnp.inf, F32), jnp.zeros((rows, 1), F32), jnp.zeros((rows, hd), F32))


def _flash_finish(carry):
    _, l, acc = carry
    return acc / jnp.where(l > 0, l, 1.0)


def _stack_heads(x, h0, rep, hd):
    return jnp.concatenate([x[:, (h0 + r) * hd:(h0 + r + 1) * hd] for r in range(rep)], axis=0)


def _sortable_key(score):
    bits = pltpu.bitcast(score, I32)
    key = jnp.where(bits < 0, bits ^ jnp.int32(0x7FFFFFFF), bits)
    return jnp.where(score == 0.0, 0, key)


def _tile_plan(limit, tk):
    half = tk // 2
    n_half = (limit + half - 1) // half
    return n_half // 2, n_half % 2 == 1


def _over_tiles(plan, tk, step, carry):
    n_wide, has_half = plan
    half = tk // 2
    carry = lax.fori_loop(0, n_wide, lambda kt, c: step(c, pl.multiple_of(kt * tk, tk), tk), carry)
    return lax.cond(has_half, lambda c: step(c, pl.multiple_of(n_wide * tk, half), half), lambda c: c, carry)


def _count_lanes(pred_fn, key_ref, plan, tk):
    rows = key_ref.shape[0]

    def step(cnt, off, width):
        keys = key_ref[:, pl.ds(off, width)]
        for c in range(width // LANE):
            hit = pred_fn(keys[:, c * LANE:(c + 1) * LANE], off + c * LANE)
            cnt = cnt + jnp.where(hit, 1, 0)
        return cnt

    cnt = _over_tiles(plan, tk, step, jnp.zeros((rows, LANE), I32))
    return jnp.sum(cnt, axis=1, keepdims=True)


def _top_k_mask_params(key_ref, plan, tk, topk, n_valid):
    rows = key_ref.shape[0]
    lane_iota = lax.broadcasted_iota(I32, (rows, LANE), 1)

    def bit_body(it, thr):
        cand = thr + jnp.left_shift(jnp.int32(1), 31 - it)
        cnt = _count_lanes(lambda keys, c0: keys >= cand, key_ref, plan, tk)
        return jnp.where(cnt >= topk, cand, thr)

    thr = lax.fori_loop(0, 32, bit_body, jnp.full((rows, 1), INT_MIN, I32))
    many = n_valid > topk
    thr = jnp.where(many, thr, INT_MIN + 1)
    c_gt = _count_lanes(lambda keys, c0: keys > thr, key_ref, plan, tk)
    c_eq = _count_lanes(lambda keys, c0: keys == thr, key_ref, plan, tk)
    need = topk - c_gt
    excess = many & (c_eq > need)
    n_col_bits = max(1, (key_ref.shape[1] - 1).bit_length())

    def tie_search():
        def jbody(it, last):
            cand = last + jnp.left_shift(jnp.int32(1), n_col_bits - 1 - it)
            cnt = _count_lanes(lambda keys, c0: (keys == thr) & (c0 + lane_iota < cand), key_ref, plan, tk)
            return jnp.where(cnt <= need - 1, cand, last)

        return lax.fori_loop(0, n_col_bits, jbody, jnp.zeros((rows, 1), I32))

    any_excess = jnp.max(jnp.where(excess, 1, 0)) > 0
    last_tie = lax.cond(any_excess, tie_search, lambda: jnp.zeros((rows, 1), I32))
    return thr, jnp.where(excess, last_tie, jnp.int32(2 ** 30))


SEARCH_DIGIT_BITS = 4


def _top_k_mask_params_small(keys, col, topk, n_valid):
    rows = keys.shape[0]
    count = lambda hit: jnp.sum(jnp.where(hit, 1, 0), axis=1, keepdims=True)
    thr = jnp.full((rows, 1), INT_MIN, I32)
    for shift in range(32 - SEARCH_DIGIT_BITS, -1, -SEARCH_DIGIT_BITS):
        digit = jnp.zeros((rows, 1), I32)
        for d in range(1, 2 ** SEARCH_DIGIT_BITS):
            step = np.int64(d << shift).astype(np.int32)
            digit = digit + jnp.where(count(keys >= thr + jnp.int32(step)) >= topk, 1, 0)
        thr = thr + jnp.left_shift(digit, shift)
    many = n_valid > topk
    thr = jnp.where(many, thr, INT_MIN + 1)
    need = topk - count(keys > thr)
    tie = keys == thr
    excess = many & (count(tie) > need)
    n_col_bits = max(1, (keys.shape[1] - 1).bit_length())

    def tie_search():
        last = jnp.zeros((rows, 1), I32)
        for it in range(n_col_bits):
            cand = last + (1 << (n_col_bits - 1 - it))
            last = jnp.where(count(tie & (col < cand)) <= need - 1, cand, last)
        return last

    any_excess = jnp.max(jnp.where(excess, 1, 0)) > 0
    last_tie = lax.cond(any_excess, tie_search, lambda: jnp.zeros((rows, 1), I32))
    return thr, jnp.where(excess, last_tie, jnp.int32(2 ** 30))


def _block_cover(n_cmp_rows):
    c0 = lax.broadcasted_iota(I32, (n_cmp_rows, LANE), 0) * CMP_STRIDE
    s0 = lax.broadcasted_iota(I32, (n_cmp_rows, LANE), 1) * SLC_BLOCK
    return jnp.where(c0 < s0 + SLC_BLOCK, jnp.where(c0 + CMP_BLOCK > s0, 1.0, 0.0), 0.0).astype(BF16)


def _block_importance(p_sum, cover, row_pos):
    p_hi = p_sum.astype(BF16)
    p_lo = (p_sum - p_hi.astype(F32)).astype(BF16)
    imp = jnp.dot(p_hi, cover, preferred_element_type=F32) + jnp.dot(p_lo, cover, preferred_element_type=F32)
    jb = lax.broadcasted_iota(I32, imp.shape, 1)
    cur = row_pos // SLC_BLOCK
    forced = (jb == 0) | (jb == cur) | (jb == cur - 1)
    imp = jnp.where(forced, jnp.inf, imp)
    return jnp.where(jb <= cur, imp, -jnp.inf)


def _dsa_prompt_kernel(iq_ref, iw_ref, ikt_ref, q_ref, kt_ref, v_ref, o_ref, key_ref, *, topk, tk,
                       n_idx_heads, n_groups, rep):
    i = pl.program_id(1)
    t0 = i * Q_BLOCK
    plan = _tile_plan(t0 + Q_BLOCK, tk)
    row = t0 + lax.broadcasted_iota(I32, (Q_BLOCK, 1), 0)
    hd = kt_ref.shape[0] // n_groups
    di = ikt_ref.shape[0]

    iw = iw_ref[0]
    iq = iq_ref[0]
    iq_heads = [iq[:, h * di:(h + 1) * di] for h in range(n_idx_heads)]

    def score_step(_, off, width):
        acc = jnp.zeros((Q_BLOCK, width), F32)
        for h in range(n_idx_heads):
            r = jnp.dot(iq_heads[h], ikt_ref[:, pl.ds(off, width)], preferred_element_type=F32)
            acc = acc + jnp.maximum(r, 0.0) * iw[:, h:h + 1]
        col = off + lax.broadcasted_iota(I32, (Q_BLOCK, width), 1)
        key_ref[:, pl.ds(off, width)] = jnp.where(col <= row, _sortable_key(acc), INT_MIN)
        return 0

    _over_tiles(plan, tk, score_step, 0)

    thr, last_tie = _top_k_mask_params(key_ref, plan, tk, topk, row + 1)

    q = q_ref[0]
    q_groups = [_stack_heads(q, g * rep, rep, hd) for g in range(n_groups)]
    rows_g = rep * Q_BLOCK

    def attn_step(carry, off, width):
        keys = key_ref[:, pl.ds(off, width)]
        col = off + lax.broadcasted_iota(I32, (Q_BLOCK, width), 1)
        sel = ((keys > thr) | ((keys == thr) & (col <= last_tie)))[None]
        v_t = v_ref[0, pl.ds(off, width), :]
        return tuple(
            _flash_step(carry[g], q_groups[g], kt_ref[g * hd:(g + 1) * hd, pl.ds(off, width)],
                        v_t[:, g * hd:(g + 1) * hd], sel, rep)
            for g in range(n_groups))

    fin = _over_tiles(plan, tk, attn_step, tuple(_flash_init(rows_g, hd) for _ in range(n_groups)))
    for g in range(n_groups):
        o = _flash_finish(fin[g])
        for r in range(rep):
            h = g * rep + r
            o_ref[0, :, h * hd:(h + 1) * hd] = o[r * Q_BLOCK:(r + 1) * Q_BLOCK]


def dsa_prompt_attention(q, kt, v, iq, ikt, iw, topk, n_groups, n_idx_heads, tk=ATTN_COLS):
    b, s, hd_all = q.shape
    gd, di = kt.shape[0], ikt.shape[0]
    d = gd // n_groups
    tk = min(tk, s)
    kern = functools.partial(_dsa_prompt_kernel, topk=topk, tk=tk, n_idx_heads=n_idx_heads, n_groups=n_groups,
                             rep=hd_all // gd)
    q_block = lambda width: pl.BlockSpec((1, Q_BLOCK, width), lambda bb, i: (bb, i, 0))
    return pl.pallas_call(
        kern,
        grid=(b, s // Q_BLOCK),
        in_specs=[
            q_block(n_idx_heads * di), q_block(n_idx_heads),
            pl.BlockSpec((di, s), lambda bb, i: (0, bb)),
            q_block(hd_all),
            pl.BlockSpec((gd, s), lambda bb, i: (0, bb)),
            pl.BlockSpec((1, s, gd), lambda bb, i: (bb, 0, 0)),
        ],
        out_specs=q_block(hd_all),
        out_shape=jax.ShapeDtypeStruct((b, s, hd_all), F32),
        scratch_shapes=[pltpu.VMEM((Q_BLOCK, s), I32)],
        compiler_params=pltpu.CompilerParams(
            dimension_semantics=("parallel", "arbitrary"), vmem_limit_bytes=VMEM_LIMIT),
        name="dsa_prompt_attention",
    )(iq, iw, ikt, q, kt, v)


def _top_blocks(imp_ref, n_live, n_sel):
    imp_t = imp_ref[...]
    j_iota = lax.broadcasted_iota(I32, imp_t.shape, 0)

    def body(k, rank):
        row_k = imp_ref[pl.ds(k, 1), :]
        earlier = jnp.where(j_iota > k, 1.0, 0.0)
        return rank + jnp.where(row_k > imp_t, 1.0, jnp.where(row_k == imp_t, earlier, 0.0))

    rank = lax.fori_loop(0, n_live, body, jnp.zeros(imp_t.shape, F32))
    return jnp.where(rank < n_sel, 1.0, 0.0)


def _nsa_prompt_kernel(q_ref, gate_ref, kct_ref, vc_ref, kst_ref, vs_ref, kwt_ref, vw_ref, o_ref, imp_ref, *,
                       tk, n_cmp, n_blk, n_groups, rep, win_len):
    i = pl.program_id(1)
    t0 = i * Q_BLOCK
    plan = _tile_plan(t0 + Q_BLOCK, tk)
    hd = kst_ref.shape[0] // n_groups
    rows = rep * Q_BLOCK
    nc_pad = kct_ref.shape[-1]
    row = t0 + lax.broadcasted_iota(I32, (Q_BLOCK, 1), 0)
    gates = jax.nn.sigmoid(gate_ref[0])
    n_heads = n_groups * rep
    q_all = q_ref[0]

    cover = _block_cover(nc_pad)
    c_idx = lax.broadcasted_iota(I32, (Q_BLOCK, nc_pad), 1)
    c_ok = ((c_idx < n_cmp) & (c_idx * CMP_STRIDE + (CMP_BLOCK - 1) <= row))[None]
    w_start = pl.multiple_of(jnp.maximum(t0 - WINDOW, 0), Q_BLOCK)
    w_col = w_start + lax.broadcasted_iota(I32, (Q_BLOCK, win_len), 1)
    w_d = row - w_col
    w_ok = ((w_d >= 0) & (w_d < WINDOW))[None]

    for g in range(n_groups):
        q = _stack_heads(q_all, g * rep, rep, hd)
        lo, hi = g * hd, (g + 1) * hd

        s = jnp.dot(q, kct_ref[0, lo:hi, :], preferred_element_type=F32)
        s = jnp.where(c_ok, s.reshape(rep, Q_BLOCK, nc_pad), -jnp.inf)
        p = _softmax_rows(s)
        o_cmp = jnp.dot(p.reshape(rows, nc_pad).astype(BF16), vc_ref[0, :, lo:hi], preferred_element_type=F32)

        imp = _block_importance(jnp.sum(p, axis=0), cover, row)
        imp_ref[...] = imp.T
        n_live = jnp.minimum((t0 + Q_BLOCK - 1) // SLC_BLOCK + 1, n_blk)
        sel_blocks = _top_blocks(imp_ref, n_live, min(N_SLC, n_blk)).T.astype(BF16)

        def slc_step(carry, off, width):
            col = off + lax.broadcasted_iota(I32, (Q_BLOCK, width), 1)
            blk_of_col = (off + lax.broadcasted_iota(I32, (LANE, width), 1)) // SLC_BLOCK
            expand = jnp.where(lax.broadcasted_iota(I32, (LANE, width), 0) == blk_of_col, 1.0, 0.0).astype(BF16)
            hit = jnp.dot(sel_blocks, expand, preferred_element_type=F32)
            sel = (jnp.where(col <= row, hit, 0.0) > 0.5)[None]
            return _flash_step(carry, q, kst_ref[lo:hi, pl.ds(off, width)], vs_ref[0, pl.ds(off, width), lo:hi], sel,
                               rep)

        o_slc = _flash_finish(_over_tiles(plan, tk, slc_step, _flash_init(rows, hd)))

        s = jnp.dot(q, kwt_ref[lo:hi, pl.ds(w_start, win_len)], preferred_element_type=F32)
        s = jnp.where(w_ok, s.reshape(rep, Q_BLOCK, win_len), -jnp.inf)
        p = _softmax_rows(s).reshape(rows, win_len)
        o_win = jnp.dot(p.astype(BF16), vw_ref[0, pl.ds(w_start, win_len), lo:hi], preferred_element_type=F32)

        for r in range(rep):
            h = g * rep + r
            rs = slice(r * Q_BLOCK, (r + 1) * Q_BLOCK)
            o_ref[0, :, h * hd:(h + 1) * hd] = (
                gates[:, h:h + 1] * o_cmp[rs] + gates[:, n_heads + h:n_heads + h + 1] * o_slc[rs]
                + gates[:, 2 * n_heads + h:2 * n_heads + h + 1] * o_win[rs])


def nsa_prompt_attention(q, gate_logits, kct, vc, n_cmp, kst, vs, kwt, vw, n_groups, tk=ATTN_COLS):
    b, s, hd_all = q.shape
    gd, nc_pad = kct.shape[1], kct.shape[2]
    n_blk = -(-s // SLC_BLOCK)
    assert n_blk <= LANE and s % Q_BLOCK == 0
    tk = min(tk, s)
    win_len = min(WINDOW + Q_BLOCK, s)
    kern = functools.partial(_nsa_prompt_kernel, tk=tk, n_cmp=n_cmp, n_blk=n_blk, n_groups=n_groups,
                             rep=hd_all // gd, win_len=win_len)
    per_b = lambda *blk: pl.BlockSpec((1,) + blk, lambda bb, i: (bb,) + (0,) * len(blk))
    keys_t = pl.BlockSpec((gd, s), lambda bb, i: (0, bb))
    q_block = lambda width: pl.BlockSpec((1, Q_BLOCK, width), lambda bb, i: (bb, i, 0))
    return pl.pallas_call(
        kern,
        grid=(b, s // Q_BLOCK),
        in_specs=[q_block(hd_all), q_block(gate_logits.shape[-1]), per_b(gd, nc_pad), per_b(nc_pad, gd),
                  keys_t, per_b(s, gd), keys_t, per_b(s, gd)],
        out_specs=q_block(hd_all),
        out_shape=jax.ShapeDtypeStruct((b, s, hd_all), F32),
        scratch_shapes=[pltpu.VMEM((LANE, Q_BLOCK), F32)],
        compiler_params=pltpu.CompilerParams(
            dimension_semantics=("parallel", "arbitrary"), vmem_limit_bytes=VMEM_LIMIT),
        name="nsa_prompt_attention",
    )(q, gate_logits, kct, vc, kst, vs, kwt, vw)


def _compress_finish(a, pew_ref, w2_ref, n_groups, hd):
    half = n_groups * hd
    first, second = a[:, :half], a[:, half:]
    second = jnp.concatenate([second[1:], jnp.zeros((1, half), F32)], axis=0)
    out = []
    for g in range(n_groups):
        hcol = first[:, g * hd:(g + 1) * hd] + second[:, g * hd:(g + 1) * hd] + pew_ref[...]
        out.append(jnp.dot(jax.nn.silu(hcol).astype(BF16), w2_ref[...], preferred_element_type=F32))
    return out


def _compress_kernel(c_ref, wc_ref, pew_ref, w2_ref, o_ref, *, n_groups, hd, transposed):
    a = jnp.dot(c_ref[0].astype(BF16), wc_ref[...], preferred_element_type=F32)
    res = jnp.concatenate(_compress_finish(a, pew_ref, w2_ref, n_groups, hd), axis=1)
    o_ref[0] = (res.T if transposed else res).astype(BF16)


def _compress_chunk_weights(w1, n_groups, hd):
    wr = w1.reshape(2, CMP_STRIDE, hd, hd)
    eye = jnp.eye(n_groups, dtype=w1.dtype)
    wc = jnp.einsum('hrdo,gk->rgdhko', wr, eye)
    return wc.reshape(CMP_STRIDE * n_groups * hd, 2 * n_groups * hd)


def _compress_weights(pe, w1, w2, n_groups, hd):
    return _compress_chunk_weights(w1, n_groups, hd).astype(BF16), mm(pe.reshape(1, -1), w1), w2.astype(BF16)


def compress_rows(rows, cmp_w, n_groups, transposed):
    b, l, gd = rows.shape
    hd = gd // n_groups
    n_chunks = l // CMP_STRIDE
    cw = CMP_STRIDE * gd
    wc, pew, w2 = _compress_weights(*cmp_w, n_groups, hd)
    out_blk = (gd, n_chunks) if transposed else (n_chunks, gd)
    return pl.pallas_call(
        functools.partial(_compress_kernel, n_groups=n_groups, hd=hd, transposed=transposed),
        grid=(b,),
        in_specs=[pl.BlockSpec((1, n_chunks, cw), lambda bb: (bb, 0, 0)), _resident(wc.shape), _resident(pew.shape),
                  _resident(w2.shape)],
        out_specs=pl.BlockSpec((1,) + out_blk, lambda bb: (bb, 0, 0)),
        out_shape=jax.ShapeDtypeStruct((b,) + out_blk, BF16),
        compiler_params=pltpu.CompilerParams(dimension_semantics=("parallel",), vmem_limit_bytes=VMEM_LIMIT),
        name="compress_rows",
    )(rows.reshape(b, n_chunks, cw), wc, pew, w2)


def _attend_cols(q, kt, vt, sel, rep):
    n = kt.shape[1]
    s = jnp.dot(q, kt, preferred_element_type=F32)
    s = jnp.where(sel, s.reshape(rep, TQ, n), -jnp.inf)
    p = _softmax_rows(s).reshape(rep * TQ, n)
    return p, lax.dot_general(p.astype(BF16), vt, (((1,), (1,)), ((), ())), preferred_element_type=F32)


def _head_rows(x):
    b, t, h, d = x.shape
    x = jnp.pad(x.astype(BF16), ((0, 0), (0, TQ - t), (0, 0), (0, 0)))
    return x.transpose(0, 2, 1, 3).reshape(b, h * TQ, d)


def _new_cols(x, width):
    b, t = x.shape[:2]
    return jnp.pad(x.reshape(b, t, -1).astype(BF16).transpose(0, 2, 1), ((0, 0), (0, 0), (0, width - t)))


def _cols_view(pool):
    lead, rows = pool.shape[:2], pool.shape[2]
    perm = (0, 1) + tuple(range(3, pool.ndim)) + (2,)
    return pool.transpose(perm).reshape(lead + (-1, rows))


def _page_specs(block, layer, n_pages, pp):
    def spec(j):
        return pl.BlockSpec((1, 1) + block,
                            lambda bb, p, pt: (layer, pt[bb * n_pages + p * pp + j]) + (0,) * len(block))
    return [spec(j) for j in range(pp)]


def _per_batch(*blk):
    return pl.BlockSpec((1,) + blk, lambda bb, p, pt: (bb,) + (0,) * len(blk))


def _whole(*blk):
    return pl.BlockSpec(blk, lambda bb, p, pt: (0,) * len(blk))


def _dsa_sample_kernel(pt_ref, iq_ref, iw_ref, q_ref, ikn_ref, kn_ref, vn_ref, *rest, topk, past, n_new,
                       n_idx_heads, n_groups, rep, pp):
    idx_pages = rest[:pp]
    k_pages = rest[pp:2 * pp]
    v_pages = rest[2 * pp:3 * pp]
    o_ref, key_ref, kst, vst = rest[3 * pp:]
    p_step = pl.program_id(1)
    hd = q_ref.shape[-1]
    page = idx_pages[0].shape[-1]
    n_tiles = past // page + 1
    iw = iw_ref[0]
    tpos = past + jnp.minimum(lax.broadcasted_iota(I32, (TQ, 1), 0), n_new - 1)

    def scores(ikt):
        r = jnp.dot(iq_ref[0], ikt, preferred_element_type=F32)
        r = jnp.maximum(r, 0.0).reshape(n_idx_heads, TQ, ikt.shape[1]) * iw
        return jnp.sum(r, axis=0)

    for j in range(pp):
        off = pl.multiple_of((p_step * pp + j) * page, page)
        key_ref[:, pl.ds(off, page)] = _sortable_key(scores(idx_pages[j][0, 0].astype(BF16)))
        kst[:, pl.ds(off, page)] = k_pages[j][0, 0].astype(BF16)
        vst[:, pl.ds(off, page)] = v_pages[j][0, 0].astype(BF16)

    @pl.when(p_step == pl.num_programs(1) - 1)
    def _():
        c = lax.broadcasted_iota(I32, (TQ, page), 1)
        fresh_ok = (c < n_new) & (past + c <= tpos)
        key_ref[:, past:past + page] = jnp.where(fresh_ok, _sortable_key(scores(ikn_ref[0])), INT_MIN)
        kst[:, past:past + page] = kn_ref[0]
        vst[:, past:past + page] = vn_ref[0]
        keys = key_ref[...]
        col = lax.broadcasted_iota(I32, keys.shape, 1)
        thr, last_tie = _top_k_mask_params_small(keys, col, topk, tpos + 1)
        sel = ((keys > thr) | ((keys == thr) & (col <= last_tie)))[None]
        for g in range(n_groups):
            gs = slice(g * hd, (g + 1) * hd)
            _, o = _attend_cols(q_ref[0, g * rep * TQ:(g + 1) * rep * TQ], kst[gs, :], vst[gs, :], sel, rep)
            for r in range(rep):
                h = g * rep + r
                o_ref[0, :, h * hd:(h + 1) * hd] = o[r * TQ:(r + 1) * TQ]


def dsa_sample_attention(q, k, v, iq, ik, iw, pools_k, pools_v, pools_idx, layer, page_table, topk):
    b, t, h, d = q.shape
    g = k.shape[2]
    hi, di = iq.shape[2], iq.shape[3]
    n_pages = page_table.shape[1]
    page = pools_k.shape[2]
    past = n_pages * page
    pp = min(PAGES_PER_STEP, n_pages)
    assert n_pages % pp == 0 and t <= TQ
    iw_h = jnp.pad(iw.astype(F32), ((0, 0), (0, TQ - t), (0, 0))).transpose(0, 2, 1)[..., None]
    kern = functools.partial(_dsa_sample_kernel, topk=topk, past=past, n_new=t, n_idx_heads=hi, n_groups=g,
                             rep=h // g, pp=pp)
    o = pl.pallas_call(
        kern,
        grid_spec=pltpu.PrefetchScalarGridSpec(
            num_scalar_prefetch=1,
            grid=(b, n_pages // pp),
            in_specs=[_per_batch(hi * TQ, di), _per_batch(hi, TQ, 1), _per_batch(h * TQ, d), _per_batch(di, page),
                      _per_batch(g * d, page), _per_batch(g * d, page)]
                     + _page_specs((di, page), layer, n_pages, pp)
                     + _page_specs((g * d, page), layer, n_pages, pp)
                     + _page_specs((g * d, page), layer, n_pages, pp),
            out_specs=_per_batch(TQ, h * d),
            scratch_shapes=[pltpu.VMEM((TQ, past + page), I32), pltpu.VMEM((g * d, past + page), BF16),
                            pltpu.VMEM((g * d, past + page), BF16)],
        ),
        out_shape=jax.ShapeDtypeStruct((b, TQ, h * d), F32),
        compiler_params=pltpu.CompilerParams(
            dimension_semantics=("parallel", "arbitrary"), vmem_limit_bytes=VMEM_LIMIT),
        name="dsa_sample_attention",
    )(page_table.reshape(-1).astype(I32), _head_rows(iq), iw_h, _head_rows(q), _new_cols(ik, page),
      _new_cols(k, page), _new_cols(v, page),
      *([_cols_view(pools_idx)] * pp), *([_cols_view(pools_k)] * pp), *([_cols_view(pools_v)] * pp))
    return o[:, :t]


def _top_blocks_rows(imp, n_blk, n_sel):
    j_iota = lax.broadcasted_iota(I32, imp.shape, 1)
    rank = jnp.zeros(imp.shape, F32)
    for k in range(n_blk):
        col_k = imp[:, k:k + 1]
        earlier = jnp.where(j_iota > k, 1.0, 0.0)
        rank = rank + jnp.where(col_k > imp, 1.0, jnp.where(col_k == imp, earlier, 0.0))
    return jnp.where((rank < n_sel) & (j_iota < n_blk), 1.0, 0.0)


def _nsa_sample_kernel(pt_ref, q_ref, gate_ref, ksn_ref, vsn_ref, kwn_ref, vwn_ref, wink_ref, winv_ref,
                       wck_ref, wcv_ref, pewk_ref, pewv_ref, w2k_ref, w2v_ref, *rest, past, n_new, n_groups, rep, pp):
    kc_pages = rest[:pp]
    vc_pages = rest[pp:2 * pp]
    ks_pages = rest[2 * pp:3 * pp]
    vs_pages = rest[3 * pp:4 * pp]
    o_ref, kcs, vcs, kst, vst, kwt, vwt = rest[4 * pp:]
    p_step = pl.program_id(1)
    hd = q_ref.shape[-1]
    gd = n_groups * hd
    page = ks_pages[0].shape[-1]
    n_chunks = kcs.shape[0] // CMP_STRIDE
    n_cmp = n_chunks - CMP_BLOCK // CMP_STRIDE + 1
    n_blk = -(-(past + n_new) // SLC_BLOCK)
    w_buf = wink_ref.shape[-1]
    n_heads = n_groups * rep
    rows = rep * TQ

    for j in range(pp):
        off = pl.multiple_of((p_step * pp + j) * page, page)
        kcs[pl.ds(off, page), :] = kc_pages[j][0, 0].T
        vcs[pl.ds(off, page), :] = vc_pages[j][0, 0].T
        kst[:, pl.ds(off, page)] = ks_pages[j][0, 0].astype(BF16)
        vst[:, pl.ds(off, page)] = vs_pages[j][0, 0].astype(BF16)

    @pl.when(p_step == pl.num_programs(1) - 1)
    def _():
        kst[:, past:past + page] = ksn_ref[0]
        vst[:, past:past + page] = vsn_ref[0]
        kwt[:, 0:w_buf] = wink_ref[0, 0].astype(BF16)
        vwt[:, 0:w_buf] = winv_ref[0, 0].astype(BF16)
        kwt[:, w_buf:w_buf + LANE] = kwn_ref[0]
        vwt[:, w_buf:w_buf + LANE] = vwn_ref[0]

        tpos = past + jnp.minimum(lax.broadcasted_iota(I32, (TQ, 1), 0), n_new - 1)
        gates = jax.nn.sigmoid(gate_ref[0])

        def compress(rows_ref, wc_ref, pew_ref, w2_ref):
            a = jnp.zeros((n_chunks, 2 * gd), F32)
            for r in range(CMP_STRIDE):
                xr = rows_ref[pl.ds(r, n_chunks, stride=CMP_STRIDE), :].astype(BF16)
                a = a + jnp.dot(xr, wc_ref[r * gd:(r + 1) * gd, :], preferred_element_type=F32)
            out = _compress_finish(a, pew_ref, w2_ref, n_groups, hd)
            return [o.astype(BF16) for o in out]

        kcmp = compress(kcs, wck_ref, pewk_ref, w2k_ref)
        vcmp = compress(vcs, wcv_ref, pewv_ref, w2v_ref)

        c_idx = lax.broadcasted_iota(I32, (TQ, n_chunks), 1)
        c_ok = ((c_idx < n_cmp) & (c_idx * CMP_STRIDE + (CMP_BLOCK - 1) <= tpos))[None]
        cover = _block_cover(n_chunks)
        n_slc = kst.shape[1]
        col = lax.broadcasted_iota(I32, (TQ, n_slc), 1)
        blk_of_col = lax.broadcasted_iota(I32, (LANE, n_slc), 1) // SLC_BLOCK
        expand = jnp.where(lax.broadcasted_iota(I32, (LANE, n_slc), 0) == blk_of_col, 1.0, 0.0).astype(BF16)
        n_win = kwt.shape[1]
        w_c = lax.broadcasted_iota(I32, (TQ, n_win), 1)
        w_pos = past - w_buf + w_c
        w_d = tpos - w_pos
        w_ok = ((w_c < w_buf + n_new) & (w_pos >= 0) & (w_d >= 0) & (w_d < WINDOW))[None]

        for g in range(n_groups):
            q = q_ref[0, g * rows:(g + 1) * rows]
            gs = slice(g * hd, (g + 1) * hd)
            s = lax.dot_general(q, kcmp[g], (((1,), (1,)), ((), ())), preferred_element_type=F32)
            p_cmp = _softmax_rows(jnp.where(c_ok, s.reshape(rep, TQ, n_chunks), -jnp.inf))
            o_cmp = jnp.dot(p_cmp.reshape(rows, n_chunks).astype(BF16), vcmp[g], preferred_element_type=F32)
            imp = _block_importance(jnp.sum(p_cmp, axis=0), cover, tpos)
            sel_blocks = _top_blocks_rows(imp, n_blk, min(N_SLC, n_blk)).astype(BF16)
            hit = jnp.dot(sel_blocks, expand, preferred_element_type=F32)
            sel = (jnp.where(col <= tpos, hit, 0.0) > 0.5)[None]
            _, o_slc = _attend_cols(q, kst[gs, :], vst[gs, :], sel, rep)
            _, o_win = _attend_cols(q, kwt[gs, :], vwt[gs, :], w_ok, rep)
            for r in range(rep):
                h = g * rep + r
                rs = slice(r * TQ, (r + 1) * TQ)
                o_ref[0, :, h * hd:(h + 1) * hd] = (
                    gates[:, h:h + 1] * o_cmp[rs] + gates[:, n_heads + h:n_heads + h + 1] * o_slc[rs]
                    + gates[:, 2 * n_heads + h:2 * n_heads + h + 1] * o_win[rs])


def nsa_sample_attention(q, gate_logits, ks, vs, kw, vw, cmp_k, cmp_v, pools_kc, pools_vc, pools_ks, pools_vs,
                         wins_k, wins_v, layer, page_table):
    b, t, h, d = q.shape
    g = ks.shape[2]
    gd = g * d
    n_pages = page_table.shape[1]
    page = pools_ks.shape[2]
    past = n_pages * page
    pp = min(PAGES_PER_STEP, n_pages)
    assert n_pages % pp == 0 and t <= TQ and (past + t) // CMP_STRIDE * CMP_STRIDE == past
    cw = CMP_STRIDE * gd
    w_buf = wins_k.shape[2]

    wck, pewk, w2k = _compress_weights(*cmp_k, g, d)
    wcv, pewv, w2v = _compress_weights(*cmp_v, g, d)
    kern = functools.partial(_nsa_sample_kernel, past=past, n_new=t, n_groups=g, rep=h // g, pp=pp)
    win_spec = pl.BlockSpec((1, 1, gd, w_buf), lambda bb, p, pt: (layer, bb, 0, 0))
    o = pl.pallas_call(
        kern,
        grid_spec=pltpu.PrefetchScalarGridSpec(
            num_scalar_prefetch=1,
            grid=(b, n_pages // pp),
            in_specs=[_per_batch(h * TQ, d), _per_batch(TQ, 3 * h), _per_batch(gd, page), _per_batch(gd, page),
                      _per_batch(gd, LANE), _per_batch(gd, LANE), win_spec, win_spec,
                      _whole(cw, 2 * gd), _whole(cw, 2 * gd), _whole(1, d), _whole(1, d), _whole(d, d), _whole(d, d)]
                     + _page_specs((gd, page), layer, n_pages, pp) + _page_specs((gd, page), layer, n_pages, pp)
                     + _page_specs((gd, page), layer, n_pages, pp) + _page_specs((gd, page), layer, n_pages, pp),
            out_specs=_per_batch(TQ, h * d),
            scratch_shapes=[pltpu.VMEM((past, gd), F32), pltpu.VMEM((past, gd), F32),
                            pltpu.VMEM((gd, past + page), BF16), pltpu.VMEM((gd, past + page), BF16),
                            pltpu.VMEM((gd, w_buf + LANE), BF16), pltpu.VMEM((gd, w_buf + LANE), BF16)],
        ),
        out_shape=jax.ShapeDtypeStruct((b, TQ, h * d), F32),
        compiler_params=pltpu.CompilerParams(
            dimension_semantics=("parallel", "arbitrary"), vmem_limit_bytes=VMEM_LIMIT),
        name="nsa_sample_attention",
    )(page_table.reshape(-1).astype(I32), _head_rows(q),
      jnp.pad(gate_logits.astype(F32), ((0, 0), (0, TQ - t), (0, 0))),
      _new_cols(ks, page), _new_cols(vs, page), _new_cols(kw, LANE), _new_cols(vw, LANE),
      _cols_view(wins_k), _cols_view(wins_v), wck, wcv, pewk, pewv, w2k, w2v,
      *([_cols_view(pools_kc)] * pp), *([_cols_view(pools_vc)] * pp),
      *([_cols_view(pools_ks)] * pp), *([_cols_view(pools_vs)] * pp))
    return o[:, :t]


def last_rows(a, n):
    t = a.shape[1]
    if t >= n:
        return a[:, t - n:]
    return jnp.pad(a, ((0, 0), (n - t, 0)) + ((0, 0),) * (a.ndim - 2))


def mem_kv(mem, g, w_kv):
    b, m, _ = mem.shape
    k, v = jnp.split(mm(mem, w_kv, g), 2, axis=-1)
    return k.reshape(b, m, X_HEADS, X_HEAD_DIM), v.reshape(b, m, X_HEADS, X_HEAD_DIM)


def _xattn_kernel(x_ref, g_ref, wq_ref, mk_ref, mv_ref, wo_ref, o_ref, *, n_heads, hd):
    xn = _rms(x_ref[...], g_ref[...]).astype(BF16)
    q = jnp.dot(xn, wq_ref[...], preferred_element_type=F32).astype(BF16)
    mk = mk_ref[0].astype(BF16)
    mv = mv_ref[0].astype(BF16)
    outs = []
    for h in range(n_heads):
        hs = slice(h * hd, (h + 1) * hd)
        s = lax.dot_general(q[:, hs], mk[:, hs], (((1,), (1,)), ((), ())), preferred_element_type=F32) * (hd ** -0.5)
        e = jnp.exp(s - jnp.max(s, axis=-1, keepdims=True))
        p = e / jnp.sum(e, axis=-1, keepdims=True)
        outs.append(jnp.dot(p.astype(BF16), mv[:, hs], preferred_element_type=F32).astype(BF16))
    a = jnp.concatenate(outs, axis=1)
    for c0 in range(0, o_ref.shape[1], MM_COLS):
        cs = slice(c0, c0 + MM_COLS)
        o_ref[:, cs] = x_ref[:, cs] + jnp.dot(a, wo_ref[:, cs], preferred_element_type=F32)


def cross_attn_rows(x, g, mk, mv, w_q, w_o):
    b, t, d = x.shape
    mlen, n_heads, hd = mk.shape[1:]
    hw = n_heads * hd
    tm = _pick_tile(t, (512, 256, 128, 64, 32, 16, 8))
    steps = t // tm
    rows = pl.BlockSpec((tm, d), lambda i: (i, 0))
    mem = pl.BlockSpec((1, mlen, hw), lambda i: (i // steps, 0, 0))
    out = pl.pallas_call(
        functools.partial(_xattn_kernel, n_heads=n_heads, hd=hd),
        grid=(b * steps,),
        in_specs=[rows, _resident((1, d)), _resident((d, hw)), mem, mem, _resident((hw, d))],
        out_specs=rows,
        out_shape=jax.ShapeDtypeStruct((b * t, d), F32),
        compiler_params=pltpu.CompilerParams(dimension_semantics=("parallel",), vmem_limit_bytes=VMEM_LIMIT),
        name="cross_attn_rows",
    )(x.reshape(b * t, d), g.reshape(1, d).astype(F32), w_q.astype(BF16), mk.reshape(b, mlen, hw),
      mv.reshape(b, mlen, hw), w_o.astype(BF16))
    return out.reshape(b, t, d)


def cross_attn(x, g, mk, mv, w_q, w_o):
    b, t, _ = x.shape
    if t % SUBLANE == 0:
        return cross_attn_rows(x, g, mk, mv, w_q, w_o)
    q = mm(x, w_q, g).reshape(b, t, X_HEADS, X_HEAD_DIM)
    s = jnp.einsum('bthd,bmhd->bhtm', q, mk).astype(F32) * (X_HEAD_DIM ** -0.5)
    p = jax.nn.softmax(s, axis=-1).astype(x.dtype)
    return mm(jnp.einsum('bhtm,bmhd->bthd', p, mv).reshape(b, t, -1), w_o, res=x)


A_WIDTHS = (A_HEADS * HEAD_DIM, A_KV_HEADS * HEAD_DIM, A_KV_HEADS * HEAD_DIM, IDX_HEADS * IDX_DIM, IDX_DIM, IDX_HEADS)


def dsa_project(x, g, w_in, pos, keys_transposed):
    b, t, d = x.shape
    keys = [(F32, False), (BF16, keys_transposed)]
    segments = [(True, HEAD_DIM ** -0.5, [(BF16, False)]), (True, None, keys), (False, None, [(F32, False), (BF16, False)]),
                (True, None, [(BF16, False)]), (True, None, keys),
                (False, IDX_HEADS ** -0.5 * IDX_DIM ** -0.5, [(F32, False)])]
    return project(x.reshape(b * t, d), g, _split_cols(w_in, A_WIDTHS), segments, jnp.tile(pos, b))


def dsa_prompt(x, g, w_in, w_out):
    b, s, _ = x.shape
    q, k, kt, v, vb, iq, ik, ikt, iw = dsa_project(x, g, w_in, jnp.arange(s), True)
    o = dsa_prompt_attention(q.reshape(b, s, -1), kt, vb.reshape(b, s, -1), iq.reshape(b, s, -1), ikt,
                             iw.reshape(b, s, -1), min(TOPK_MAX, s // 4), A_KV_HEADS, IDX_HEADS)
    state = (k.reshape(b, s, A_KV_HEADS, HEAD_DIM), v.reshape(b, s, A_KV_HEADS, HEAD_DIM), ik.reshape(b, s, IDX_DIM))
    return mm(o, w_out, res=x), state


def dsa_sample(x, g, w_in, w_out, pools_k, pools_v, pools_idx, layer, page_table):
    b, t, _ = x.shape
    past = page_table.shape[1] * pools_k.shape[2]
    q, k, kb, v, vb, iq, ik, ikb, iw = dsa_project(x, g, w_in, past + jnp.arange(t), False)
    o = dsa_sample_attention(q.reshape(b, t, A_HEADS, HEAD_DIM), kb.reshape(b, t, A_KV_HEADS, HEAD_DIM),
                             vb.reshape(b, t, A_KV_HEADS, HEAD_DIM), iq.reshape(b, t, IDX_HEADS, IDX_DIM),
                             ikb.reshape(b, t, IDX_DIM), iw.reshape(b, t, IDX_HEADS), pools_k, pools_v, pools_idx,
                             layer, page_table, min(TOPK_MAX, (past + t) // 4))
    state = (k.reshape(b, t, A_KV_HEADS, HEAD_DIM), v.reshape(b, t, A_KV_HEADS, HEAD_DIM), ik.reshape(b, t, IDX_DIM))
    return mm(o, w_out, res=x), state


def _shift_rows(x, prev, k):
    return jnp.concatenate([prev[prev.shape[0] - k:], x[:x.shape[0] - k]], axis=0)


def _tile_transpose(m, rows_out, cols_out):
    r, c = m.shape
    tile = m
    if c < LANE:
        tile = jnp.concatenate([tile, jnp.zeros((r, LANE - c), m.dtype)], axis=1)
    if r < LANE:
        tile = jnp.concatenate([tile, jnp.zeros((LANE - r, LANE), m.dtype)], axis=0)
    return tile.T[:rows_out, :cols_out]


def _ssd_kernel(z_ref, x_ref, dt_ref, cw_ref, cb_ref, dtb_ref, alog_ref, dskip_ref, buf_ref, h0_ref,
                y_ref, hout_ref, carry_ref, ht_ref, *, n_heads, n_groups, hd, n_state):
    c = pl.program_id(1)
    l = x_ref.shape[0]
    d_inner = n_heads * hd
    rep = n_heads // n_groups

    @pl.when(c == 0)
    def _():
        carry_ref[...] = buf_ref[0]
        for h in range(n_heads):
            ht_ref[h] = _tile_transpose(h0_ref[0, h], n_state, hd)

    x = x_ref[...]
    prev = carry_ref[...]
    conv = cb_ref[...] + x * cw_ref[B_CONV - 1:B_CONV, :]
    for k in range(1, B_CONV):
        conv = conv + _shift_rows(x, prev, k) * cw_ref[B_CONV - 1 - k:B_CONV - k, :]
    carry_ref[...] = x[l - SUBLANE:]
    act = jax.nn.silu(conv)
    gn = n_groups * n_state
    xs, bm, cm = act[:, :d_inner], act[:, d_inner:d_inner + gn], act[:, d_inner + gn:]

    dt = jax.nn.softplus(dt_ref[...] + dtb_ref[...])
    a = -jnp.exp(alog_ref[...])
    acs = dt * a
    k = 1
    while k < l:
        acs = acs + jnp.concatenate([jnp.zeros((k, n_heads), F32), acs[:l - k]], axis=0)
        k *= 2
    acs_t = jnp.concatenate([acs, jnp.zeros((l, LANE - n_heads), F32)], axis=1).T
    last = acs[l - 1:l, :]
    to_end = jnp.exp(last - acs)
    grow = jnp.exp(acs)
    chunk_decay = jnp.exp(last)
    causal = lax.broadcasted_iota(I32, (l, l), 0) >= lax.broadcasted_iota(I32, (l, l), 1)
    z = z_ref[...]
    dskip = dskip_ref[...]

    for g in range(n_groups):
        bm_g = bm[:, g * n_state:(g + 1) * n_state]
        cm_g = cm[:, g * n_state:(g + 1) * n_state].astype(BF16)
        cb = lax.dot_general(cm_g, bm_g.astype(BF16), (((1,), (1,)), ((), ())), preferred_element_type=F32)
        bm_t = bm_g.T.astype(BF16)
        for r in range(rep):
            h = g * rep + r
            hs = slice(h * hd, (h + 1) * hd)
            xs_h = xs[:, hs]
            xr = xs_h * dt[:, h:h + 1]
            seg = acs[:, h:h + 1] - acs_t[h:h + 1, :]
            decay = jnp.where(causal, jnp.exp(jnp.where(causal, seg, 0.0)), 0.0)
            y = jnp.dot((cb * decay).astype(BF16), xr.astype(BF16), preferred_element_type=F32)
            ht = ht_ref[h]
            y = y + jnp.dot(cm_g, ht.astype(BF16), preferred_element_type=F32) * grow[:, h:h + 1]
            upd = jnp.dot(bm_t, (xr * to_end[:, h:h + 1]).astype(BF16), preferred_element_type=F32)
            ht_ref[h] = ht * chunk_decay[:, h:h + 1] + upd
            y = y + xs_h * dskip[:, hs]
            y_ref[:, hs] = y * jax.nn.silu(z[:, hs])

    @pl.when(c == pl.num_programs(1) - 1)
    def _():
        for h in range(n_heads):
            hout_ref[0, h] = _tile_transpose(ht_ref[h], hd, n_state)


def ssd_prompt(z, xbc, dt, conv_w, conv_b, dt_bias, a_log, d_skip, conv_buf, ssm0, n_groups):
    b, n_heads, hd, n_state = ssm0.shape
    m, w = xbc.shape
    s = m // b
    l = min(B_CHUNK, s)
    nc = s // l
    d_inner = n_heads * hd
    buf = jnp.pad(conv_buf.astype(F32), ((0, 0), (SUBLANE - (B_CONV - 1), 0), (0, 0)))
    dskip_full = jnp.repeat(d_skip.astype(F32), hd).reshape(1, d_inner)
    rows = lambda width: pl.BlockSpec((l, width), lambda bb, c: (bb * nc + c, 0))
    state = pl.BlockSpec((1, n_heads, hd, n_state), lambda bb, c: (bb, 0, 0, 0))
    kern = functools.partial(_ssd_kernel, n_heads=n_heads, n_groups=n_groups, hd=hd, n_state=n_state)
    return pl.pallas_call(
        kern,
        grid=(b, nc),
        in_specs=[rows(d_inner), rows(w), rows(n_heads), _resident((B_CONV, w)), _resident((1, w)),
                  _resident((1, n_heads)), _resident((1, n_heads)), _resident((1, d_inner)),
                  pl.BlockSpec((1, SUBLANE, w), lambda bb, c: (bb, 0, 0)), state],
        out_specs=[rows(d_inner), state],
        out_shape=[jax.ShapeDtypeStruct((m, d_inner), F32), jax.ShapeDtypeStruct(ssm0.shape, F32)],
        scratch_shapes=[pltpu.VMEM((SUBLANE, w), F32), pltpu.VMEM((n_heads, n_state, hd), F32)],
        compiler_params=pltpu.CompilerParams(
            dimension_semantics=("parallel", "arbitrary"), vmem_limit_bytes=VMEM_LIMIT),
        name="ssd_prompt",
    )(z, xbc, dt, conv_w.astype(F32), conv_b.reshape(1, w).astype(F32), dt_bias.reshape(1, n_heads).astype(F32),
      a_log.reshape(1, n_heads).astype(F32), dskip_full, buf, ssm0.astype(F32))


def mamba_prompt(x, g, w_in, conv_w, conv_b, dt_bias, a_log, d_skip, g_norm, w_out, conv_buf, ssm0):
    b, s, d = x.shape
    plain = (False, None, [(F32, False)])
    z, xbc, dt = project(x.reshape(b * s, d), g, _split_cols(w_in, (B_D_INNER, B_CONV_DIM, B_HEADS)),
                         [plain, plain, plain], jnp.zeros((b * s,), I32))
    y, ssm = ssd_prompt(z, xbc, dt, conv_w, conv_b, dt_bias, a_log, d_skip, conv_buf, ssm0, B_GROUPS)
    xpad = jnp.concatenate([conv_buf.astype(F32), xbc.reshape(b, s, -1)[:, s - (B_CONV - 1):]], axis=1)
    new_buf = xpad[:, xpad.shape[1] - (B_CONV - 1):]
    return mm(y.reshape(b, s, -1), w_out, g_norm, res=x), new_buf, ssm.astype(ssm0.dtype)


def ssd_chunked(x, dt, a, bm, cm, h0):
    bsz, t, nh, p = x.shape
    g, n = bm.shape[2], bm.shape[3]
    r = nh // g
    l = min(B_CHUNK, t)
    nc = t // l
    xr = (x * dt[..., None]).reshape(bsz, nc, l, g, r, p)
    acs = jnp.cumsum((dt * a).reshape(bsz, nc, l, g, r), axis=2)
    br = bm.reshape(bsz, nc, l, g, n)
    cr = cm.reshape(bsz, nc, l, g, n)
    seg = acs[:, :, :, None] - acs[:, :, None, :]
    causal = jnp.tril(jnp.ones((l, l), bool))[:, :, None, None]
    decay = jnp.where(causal, jnp.exp(jnp.where(causal, seg, 0.0)), 0.0)
    cb = jnp.einsum('bcign,bcjgn->bcijg', cr, br)
    y_intra = jnp.einsum('bcijg,bcijgr,bcjgrp->bcigrp', cb, decay, xr)
    to_end = jnp.exp(acs[:, :, -1:] - acs)
    s_chunk = jnp.einsum('bclgn,bclgr,bclgrp->bcgrpn', br, to_end, xr)
    d_chunk = jnp.exp(acs[:, :, -1])

    def step(hc, inp):
        s_c, d_c = inp
        return hc * d_c[..., None, None] + s_c, hc

    h_fin, h_in = lax.scan(step, h0.reshape(bsz, g, r, p, n),
                           (jnp.moveaxis(s_chunk, 1, 0), jnp.moveaxis(d_chunk, 1, 0)))
    h_in = jnp.moveaxis(h_in, 0, 1)
    y_inter = jnp.einsum('bcign,bcigr,bcgrpn->bcigrp', cr, jnp.exp(acs), h_in)
    return (y_intra + y_inter).reshape(bsz, t, nh, p), h_fin.reshape(bsz, nh, p, n)


def mamba_mixer(x, g, w_in, conv_w, conv_b, dt_bias, a_log, d_skip, g_norm, w_out, conv_buf, ssm0):
    b, t, _ = x.shape
    proj = mm(x, w_in, g)
    z = proj[..., :B_D_INNER]
    xbc = proj[..., B_D_INNER:B_D_INNER + B_CONV_DIM]
    dt = proj[..., B_D_INNER + B_CONV_DIM:]
    xpad = jnp.concatenate([conv_buf.astype(xbc.dtype), xbc], axis=1)
    conv = conv_b + sum(xpad[:, j:j + t] * conv_w[j] for j in range(B_CONV))
    xbc = jax.nn.silu(conv)
    new_buf = xpad[:, t:]
    gn = B_GROUPS * B_STATE
    xs = xbc[..., :B_D_INNER].reshape(b, t, B_HEADS, B_HEADDIM).astype(F32)
    bm = xbc[..., B_D_INNER:B_D_INNER + gn].reshape(b, t, B_GROUPS, B_STATE).astype(F32)
    cm = xbc[..., B_D_INNER + gn:].reshape(b, t, B_GROUPS, B_STATE).astype(F32)
    dt = jax.nn.softplus(dt.astype(F32) + dt_bias.astype(F32))
    a = -jnp.exp(a_log.astype(F32))
    y, ssm = ssd_chunked(xs, dt, a, bm, cm, ssm0.astype(F32))
    y = (y + xs * d_skip.astype(F32)[:, None]).reshape(b, t, B_D_INNER).astype(x.dtype)
    y = y * jax.nn.silu(z)
    return mm(y, w_out, g_norm, res=x), new_buf, ssm.astype(ssm0.dtype)


C_KV_WIDTH = C_KV_HEADS * HEAD_DIM
C_WIDTHS = (C_HEADS * HEAD_DIM,) + (C_KV_WIDTH,) * 6 + (3 * C_HEADS,)


def nsa_project(x, g, w_in, pos, keys_transposed):
    b, t, d = x.shape
    keys = (True, None, [(F32, False), (BF16, keys_transposed)])
    vals = (False, None, [(F32, False), (BF16, False)])
    segments = [(True, HEAD_DIM ** -0.5, [(BF16, False)]), (True, None, [(F32, False)]), (False, None, [(F32, False)]),
                keys, vals, keys, vals, (False, None, [(F32, False)])]
    return project(x.reshape(b * t, d), g, _split_cols(w_in, C_WIDTHS), segments, jnp.tile(pos, b))


def nsa_prompt(x, g, w_in, w_out, cmp_k, cmp_v, w_buf):
    b, s, _ = x.shape
    q, kc, vc, ks, kst, vs, vsb, kw, kwt, vw, vwb, gates = nsa_project(x, g, w_in, jnp.arange(s), True)
    seq = lambda a: a.reshape(b, s, -1)
    kct = compress_rows(seq(kc), cmp_k, C_KV_HEADS, True)
    vcc = compress_rows(seq(vc), cmp_v, C_KV_HEADS, False)
    n_cmp = s // CMP_STRIDE - CMP_BLOCK // CMP_STRIDE + 1
    o = nsa_prompt_attention(seq(q), seq(gates), kct, vcc, n_cmp, kst, seq(vsb), kwt, seq(vwb), C_KV_HEADS)
    heads = lambda a: a.reshape(b, s, C_KV_HEADS, HEAD_DIM)
    state = (heads(kc), heads(vc), heads(ks), heads(vs), last_rows(heads(kw), w_buf), last_rows(heads(vw), w_buf))
    return mm(o, w_out, res=x), state


def nsa_sample(x, g, w_in, w_out, cmp_k, cmp_v, pools_kc, pools_vc, pools_ks, pools_vs, wins_k, wins_v, layer,
               page_table):
    b, t, _ = x.shape
    past = page_table.shape[1] * pools_kc.shape[2]
    q, kc, vc, ks, ksb, vs, vsb, kw, kwb, vw, vwb, gates = nsa_project(x, g, w_in, past + jnp.arange(t), False)
    heads = lambda a: a.reshape(b, t, C_KV_HEADS, HEAD_DIM)
    o = nsa_sample_attention(q.reshape(b, t, C_HEADS, HEAD_DIM), gates.reshape(b, t, -1), heads(ksb), heads(vsb),
                             heads(kwb), heads(vwb), cmp_k, cmp_v, pools_kc, pools_vc, pools_ks, pools_vs, wins_k,
                             wins_v, layer, page_table)
    win_k, win_v = wins_k[layer], wins_v[layer]
    w_buf = win_k.shape[1]
    kwin = jnp.concatenate([win_k, heads(kw)], axis=1)
    vwin = jnp.concatenate([win_v, heads(vw)], axis=1)
    state = (heads(kc), heads(vc), heads(ks), heads(vs), last_rows(kwin, w_buf), last_rows(vwin, w_buf))
    return mm(o, w_out, res=x), state


def kernel(x_prompt, x_sample, cache_a_k, cache_a_v, cache_a_idx, state_b_ssm, state_b_conv, cache_c_cmp_k, cache_c_cmp_v, cache_c_slc_k, cache_c_slc_v, cache_c_win_k, cache_c_win_v, cache_mem_k, cache_mem_v, page_table, mem_prompt, g_ffn1, ffn1_wi, ffn1_wo, g_mix, g_xattn, g_mem, x_w_q, x_w_kv, x_w_o, g_ffn2, ffn2_wi, ffn2_wo, g_final, a_w_in, a_w_out, b_w_in, b_conv_w, b_conv_b, b_dt_bias, b_a_log, b_d_skip, b_g_norm, b_w_out, c_w_in, c_w_out, c_pe_k, c_w1_k, c_w2_k, c_pe_v, c_w1_v, c_w2_v):
    xp, xs = x_prompt, x_sample
    w_buf = cache_c_win_k.shape[2]
    ak_p, av_p, ai_p, bs_p, bc_p = [], [], [], [], []
    cck_p, ccv_p, csk_p, csv_p, cwk_p, cwv_p = [], [], [], [], [], []
    mk_p, mv_p = [], []
    ak_s, av_s, ai_s, bs_s, bc_s = [], [], [], [], []
    cck_s, ccv_s, csk_s, csv_s, cwk_s, cwv_s = [], [], [], [], [], []
    ia = ib = ic = 0
    for i in range(DEPTH):
        xp = ffn_residual(xp, g_ffn1[i], ffn1_wi[i], ffn1_wo[i])
        xs = ffn_residual(xs, g_ffn1[i], ffn1_wi[i], ffn1_wo[i])
        kind = i % N_MIXERS
        if kind == 0:
            op, (k1, v1, i1) = dsa_prompt(xp, g_mix[i], a_w_in[ia], a_w_out[ia])
            os_, (k2, v2, i2) = dsa_sample(xs, g_mix[i], a_w_in[ia], a_w_out[ia], cache_a_k, cache_a_v, cache_a_idx,
                                           ia, page_table)
            ak_p.append(k1); av_p.append(v1); ai_p.append(i1)
            ak_s.append(k2); av_s.append(v2); ai_s.append(i2)
            ia += 1
        elif kind == 1:
            conv0 = jnp.zeros((xp.shape[0], B_CONV - 1, B_CONV_DIM), xp.dtype)
            ssm0 = jnp.zeros((xp.shape[0], B_HEADS, B_HEADDIM, B_STATE), F32)
            wts = (b_w_in[ib], b_conv_w[ib], b_conv_b[ib], b_dt_bias[ib], b_a_log[ib], b_d_skip[ib],
                   b_g_norm[ib], b_w_out[ib])
            op, c1, s1 = mamba_prompt(xp, g_mix[i], *wts, conv0, ssm0)
            os_, c2, s2 = mamba_mixer(xs, g_mix[i], *wts, state_b_conv[ib], state_b_ssm[ib])
            bc_p.append(c1); bs_p.append(s1); bc_s.append(c2); bs_s.append(s2)
            ib += 1
        else:
            cmp_k = (c_pe_k[ic], c_w1_k[ic], c_w2_k[ic])
            cmp_v = (c_pe_v[ic], c_w1_v[ic], c_w2_v[ic])
            op, st1 = nsa_prompt(xp, g_mix[i], c_w_in[ic], c_w_out[ic], cmp_k, cmp_v, w_buf)
            os_, st2 = nsa_sample(xs, g_mix[i], c_w_in[ic], c_w_out[ic], cmp_k, cmp_v, cache_c_cmp_k, cache_c_cmp_v,
                                  cache_c_slc_k, cache_c_slc_v, cache_c_win_k, cache_c_win_v, ic, page_table)
            for lst, arr in zip((cck_p, ccv_p, csk_p, csv_p, cwk_p, cwv_p), st1):
                lst.append(arr)
            for lst, arr in zip((cck_s, ccv_s, csk_s, csv_s, cwk_s, cwv_s), st2):
                lst.append(arr)
            ic += 1
        xp, xs = op, os_
        mk, mv = mem_kv(mem_prompt, g_mem[i], x_w_kv[i])
        mk_p.append(mk); mv_p.append(mv)
        xp = cross_attn(xp, g_xattn[i], mk, mv, x_w_q[i], x_w_o[i])
        xs = cross_attn(xs, g_xattn[i], cache_mem_k[i], cache_mem_v[i], x_w_q[i], x_w_o[i])
        xp = ffn_residual(xp, g_ffn2[i], ffn2_wi[i], ffn2_wo[i])
        xs = ffn_residual(xs, g_ffn2[i], ffn2_wi[i], ffn2_wo[i])
    y_prompt = rms_norm_rows(xp, g_final)
    y_sample = rms_norm_rows(xs, g_final)
    st = jnp.stack
    return (y_prompt, y_sample,
            st(ak_p), st(av_p), st(ai_p), st(bs_p), st(bc_p),
            st(cck_p), st(ccv_p), st(csk_p), st(csv_p), st(cwk_p), st(cwv_p),
            st(mk_p), st(mv_p),
            st(ak_s), st(av_s), st(ai_s), st(bs_s), st(bc_s),
            st(cck_s), st(ccv_s), st(csk_s), st(csv_s), st(cwk_s), st(cwv_s))
```

```python
import functools

import jax
import jax.numpy as jnp
import numpy as np
from jax import lax
from jax.experimental import pallas as pl
from jax.experimental.pallas import tpu as pltpu

F32 = jnp.float32
BF16 = jnp.bfloat16
I32 = jnp.int32
INT_MIN = -2 ** 31
EPS = 1e-6

D_MODEL = 1024
DEPTH = 4
N_MIXERS = 3
HEAD_DIM = 64
ROPE_THETA = 10000.0
Q_BLOCK = 128
A_HEADS = D_MODEL // HEAD_DIM
A_KV_HEADS = 4
IDX_HEADS = 8
IDX_DIM = 64
TOPK_MAX = 256
B_D_INNER = 2 * D_MODEL
B_HEADDIM = 64
B_HEADS = B_D_INNER // B_HEADDIM
B_GROUPS = 4
B_STATE = 128
B_CONV = 4
B_CONV_DIM = B_D_INNER + 2 * B_GROUPS * B_STATE
B_CHUNK = 128
C_HEADS = D_MODEL // HEAD_DIM
C_KV_HEADS = 2
CMP_BLOCK = 32
CMP_STRIDE = 16
SLC_BLOCK = 64
N_SLC = 16
WINDOW = 512
X_HEADS = 4
X_HEAD_DIM = 128
D_FF = 2816

LANE = 128
SUBLANE = 8
VMEM_LIMIT = 48 * 1024 * 1024
MM_COLS = 512
FF_CHUNK = 256
ATTN_COLS = 1024
TQ = SUBLANE
PAGES_PER_STEP = 16


def _pick_tile(n, candidates):
    for c in candidates:
        if n % c == 0:
            return c
    return n


def _resident(shape):
    return pl.BlockSpec(shape, lambda *_: (0,) * len(shape), pipeline_mode=pl.Buffered(1))


def _rms(x, g):
    return x * lax.rsqrt(jnp.mean(x * x, axis=-1, keepdims=True) + EPS) * g


def _nmm_kernel(x_ref, g_ref, w_ref, *rest, norm, residual):
    o_ref = rest[-1]
    x = x_ref[...]
    if norm:
        x = _rms(x, g_ref[...])
    xb = x.astype(BF16)
    n = w_ref.shape[1]
    for c0 in range(0, n, MM_COLS):
        c1 = min(c0 + MM_COLS, n)
        y = jnp.dot(xb, w_ref[:, c0:c1], preferred_element_type=F32)
        o_ref[:, c0:c1] = rest[0][:, c0:c1] + y if residual else y


def norm_matmul(x, w, g=None, res=None):
    m, k = x.shape
    n = w.shape[1]
    assert res is None or (n % LANE == 0 and m % SUBLANE == 0)
    n_pad = -(-n // LANE) * LANE
    wb = w.astype(BF16)
    if n_pad != n:
        wb = jnp.pad(wb, ((0, 0), (0, n_pad - n)))
    m_orig = m
    if m % SUBLANE != 0:
        m = -(-m // 256) * 256
        x = jnp.pad(x, ((0, m - m_orig), (0, 0)))
    tm = _pick_tile(m, (256, 128, 64, 32, 16, 8))
    norm = g is not None
    gg = (g if norm else jnp.ones((k,), F32)).reshape(1, k).astype(F32)
    row_block = lambda width: pl.BlockSpec((tm, width), lambda i: (i, 0))
    extra = [] if res is None else [res]
    out = pl.pallas_call(
        functools.partial(_nmm_kernel, norm=norm, residual=res is not None),
        grid=(m // tm,),
        in_specs=[row_block(k), _resident((1, k)), _resident((k, n_pad))] + [row_block(n_pad) for _ in extra],
        out_specs=row_block(n_pad),
        out_shape=jax.ShapeDtypeStruct((m, n_pad), F32),
        compiler_params=pltpu.CompilerParams(dimension_semantics=("parallel",), vmem_limit_bytes=VMEM_LIMIT),
        name="norm_matmul",
    )(x, gg, wb, *extra)
    if n_pad != n or m != m_orig:
        out = out[:m_orig, :n]
    return out


def mm(x, w, g=None, res=None):
    lead = x.shape[:-1]
    r2 = None if res is None else res.reshape(-1, res.shape[-1])
    return norm_matmul(x.reshape(-1, x.shape[-1]), w, g, r2).reshape(lead + (w.shape[1],))


def _ffn_kernel(x_ref, g_ref, wg_ref, wu_ref, wo_ref, o_ref, act_ref, *, n_chunks):
    xn = _rms(x_ref[...], g_ref[...]).astype(BF16)

    for c in range(n_chunks):
        cs = slice(c * FF_CHUNK, (c + 1) * FF_CHUNK)
        hg = jnp.dot(xn, wg_ref[:, cs], preferred_element_type=F32)
        hu = jnp.dot(xn, wu_ref[:, cs], preferred_element_type=F32)
        act_ref[:, cs] = (jax.nn.silu(hg) * hu).astype(BF16)
    for c0 in range(0, o_ref.shape[1], MM_COLS):
        cs = slice(c0, c0 + MM_COLS)
        o_ref[:, cs] = x_ref[:, cs] + 0.5 * jnp.dot(act_ref[...], wo_ref[:, cs], preferred_element_type=F32)


def ffn_residual(x, g, wi, wo):
    lead, d = x.shape[:-1], x.shape[-1]
    x2 = x.reshape(-1, d)
    m = x2.shape[0]
    f = wo.shape[0]
    assert f % FF_CHUNK == 0 and d % MM_COLS == 0
    tm = _pick_tile(m, (512, 256, 128, 64, 32, 16, 8))
    wb = wi.astype(BF16)
    out = pl.pallas_call(
        functools.partial(_ffn_kernel, n_chunks=f // FF_CHUNK),
        grid=(m // tm,),
        in_specs=[pl.BlockSpec((tm, d), lambda i: (i, 0)), _resident((1, d)), _resident((d, f)), _resident((d, f)),
                  _resident((f, d))],
        out_specs=pl.BlockSpec((tm, d), lambda i: (i, 0)),
        out_shape=jax.ShapeDtypeStruct((m, d), F32),
        scratch_shapes=[pltpu.VMEM((tm, f), BF16)],
        compiler_params=pltpu.CompilerParams(dimension_semantics=("parallel",), vmem_limit_bytes=VMEM_LIMIT),
        name="ffn_residual",
    )(x2, g.reshape(1, d).astype(F32), wb[:, :f], wb[:, f:], wo.astype(BF16))
    return out.reshape(lead + (d,))


def _rmsnorm_kernel(x_ref, g_ref, o_ref):
    o_ref[...] = _rms(x_ref[...], g_ref[...])


def rms_norm_rows(x, g):
    lead, d = x.shape[:-1], x.shape[-1]
    x2 = x.reshape(-1, d)
    m = x2.shape[0]
    tm = _pick_tile(m, (512, 256, 128, 64, 32, 16, 8))
    row_block = pl.BlockSpec((tm, d), lambda i: (i, 0))
    out = pl.pallas_call(
        _rmsnorm_kernel, grid=(m // tm,), in_specs=[row_block, _resident((1, d))], out_specs=row_block,
        out_shape=jax.ShapeDtypeStruct((m, d), F32),
        compiler_params=pltpu.CompilerParams(dimension_semantics=("parallel",)), name="rms_norm_rows",
    )(x2, g.reshape(1, d).astype(F32))
    return out.reshape(lead + (d,))


def rope_tables(pos):
    half = HEAD_DIM // 2
    inv = ROPE_THETA ** (-jnp.arange(half, dtype=F32) / half)
    ang = pos.astype(F32)[:, None] * inv[None, :]
    cos, sin = jnp.cos(ang), jnp.sin(ang)
    reps = LANE // HEAD_DIM
    return (jnp.tile(jnp.concatenate([cos, cos], axis=1), (1, reps)),
            jnp.tile(jnp.concatenate([-sin, sin], axis=1), (1, reps)))


def _rope_cols(y, cos, sin):
    w = y.shape[1]
    half = HEAD_DIM // 2
    lane = lax.broadcasted_iota(I32, y.shape, 1)
    partner = jnp.where(lane % HEAD_DIM < half, pltpu.roll(y, w - half, axis=1), pltpu.roll(y, half, axis=1))
    reps = w // LANE
    return y * jnp.tile(cos, (1, reps)) + partner * jnp.tile(sin, (1, reps))


def _proj_kernel(x_ref, g_ref, cos_ref, sin_ref, w_ref, *out_refs, segments):
    xb = _rms(x_ref[...], g_ref[...]).astype(BF16)
    cos, sin = cos_ref[...], sin_ref[...]
    k = 0
    for start, width, rope, scale, outs in segments:
        y = jnp.dot(xb, w_ref[:, start:start + width], preferred_element_type=F32)
        if rope:
            y = _rope_cols(y, cos, sin)
        if scale is not None:
            y = y * scale
        for used, dtype, transposed in outs:
            v = y[:, :used]
            out_refs[k][...] = (v.T if transposed else v).astype(dtype)
            k += 1


def project(x, g, w_cols, segments, pos_rows):
    m, kdim = x.shape
    packed, segs, out_shapes, out_specs = [], [], [], []
    tm = _pick_tile(m, (256, 128, 64, 32, 16, 8))
    start = 0
    for wc, (rope, scale, outs) in zip(w_cols, segments):
        n = wc.shape[1]
        n_pad = -(-n // LANE) * LANE
        packed.append(jnp.pad(wc.astype(BF16), ((0, 0), (0, n_pad - n))))
        segs.append((start, n_pad, rope, scale, tuple((n, dt, tr) for dt, tr in outs)))
        for dt, tr in outs:
            if tr:
                out_shapes.append(jax.ShapeDtypeStruct((n, m), dt))
                out_specs.append(pl.BlockSpec((n, tm), lambda i: (0, i)))
            else:
                out_shapes.append(jax.ShapeDtypeStruct((m, n), dt))
                out_specs.append(pl.BlockSpec((tm, n), lambda i: (i, 0)))
        start += n_pad
    wb = jnp.concatenate(packed, axis=1)
    cos, sin = rope_tables(pos_rows)
    row_block = lambda width: pl.BlockSpec((tm, width), lambda i: (i, 0))
    return pl.pallas_call(
        functools.partial(_proj_kernel, segments=tuple(segs)),
        grid=(m // tm,),
        in_specs=[row_block(kdim), _resident((1, kdim)), row_block(LANE), row_block(LANE), _resident(wb.shape)],
        out_specs=out_specs,
        out_shape=out_shapes,
        compiler_params=pltpu.CompilerParams(dimension_semantics=("parallel",), vmem_limit_bytes=VMEM_LIMIT),
        name="project",
    )(x, g.reshape(1, kdim).astype(F32), cos, sin, wb)


def _split_cols(w, widths):
    cols = np.cumsum((0,) + tuple(widths))
    return [w[:, cols[j]:cols[j + 1]] for j in range(len(widths))]


def _softmax_rows(s):
    m = jnp.max(s, axis=-1, keepdims=True)
    e = jnp.exp(s - jnp.where(m == -jnp.inf, 0.0, m))
    den = jnp.sum(e, axis=-1, keepdims=True)
    return e / jnp.where(den > 0, den, 1.0)


def _flash_step(carry, q, kt_tile, v_tile, sel, rep):
    m, l, acc = carry
    rows, tk = q.shape[0], kt_tile.shape[1]
    s = jnp.dot(q, kt_tile, preferred_element_type=F32)
    s = jnp.where(sel, s.reshape(rep, Q_BLOCK, tk), -jnp.inf).reshape(rows, tk)
    m_new = jnp.maximum(m, jnp.max(s, axis=1, keepdims=True))
    m_safe = jnp.where(m_new == -jnp.inf, 0.0, m_new)
    alpha = jnp.exp(m - m_safe)
    p = jnp.exp(s - m_safe)
    l = l * alpha + jnp.sum(p, axis=1, keepdims=True)
    pv = jnp.dot(p.astype(BF16), v_tile, preferred_element_type=F32)
    return m_new, l, acc * alpha + pv


def _flash_init(rows, hd):
    return (jnp.full((rows, 1), -jnp.inf, F32), jnp.zeros((rows, 1), F32), jnp.zeros((rows, hd), F32))


def _flash_finish(carry):
    _, l, acc = carry
    return acc / jnp.where(l > 0, l, 1.0)


def _stack_heads(x, h0, rep, hd):
    return jnp.concatenate([x[:, (h0 + r) * hd:(h0 + r + 1) * hd] for r in range(rep)], axis=0)


def _sortable_key(score):
    bits = pltpu.bitcast(score, I32)
    key = jnp.where(bits < 0, bits ^ jnp.int32(0x7FFFFFFF), bits)
    return jnp.where(score == 0.0, 0, key)


def _tile_plan(limit, tk):
    half = tk // 2
    n_half = (limit + half - 1) // half
    return n_half // 2, n_half % 2 == 1


def _over_tiles(plan, tk, step, carry):
    n_wide, has_half = plan
    half = tk // 2
    carry = lax.fori_loop(0, n_wide, lambda kt, c: step(c, pl.multiple_of(kt * tk, tk), tk), carry)
    return lax.cond(has_half, lambda c: step(c, pl.multiple_of(n_wide * tk, half), half), lambda c: c, carry)


def _count_lanes(pred_fn, key_ref, plan, tk):
    rows = key_ref.shape[0]

    def step(cnt, off, width):
        keys = key_ref[:, pl.ds(off, width)]
        for c in range(width // LANE):
            hit = pred_fn(keys[:, c * LANE:(c + 1) * LANE], off + c * LANE)
            cnt = cnt + jnp.where(hit, 1, 0)
        return cnt

    cnt = _over_tiles(plan, tk, step, jnp.zeros((rows, LANE), I32))
    return jnp.sum(cnt, axis=1, keepdims=True)


def _top_k_mask_params(key_ref, plan, tk, topk, n_valid):
    rows = key_ref.shape[0]
    lane_iota = lax.broadcasted_iota(I32, (rows, LANE), 1)

    def bit_body(it, thr):
        cand = thr + jnp.left_shift(jnp.int32(1), 31 - it)
        cnt = _count_lanes(lambda keys, c0: keys >= cand, key_ref, plan, tk)
        return jnp.where(cnt >= topk, cand, thr)

    thr = lax.fori_loop(0, 32, bit_body, jnp.full((rows, 1), INT_MIN, I32))
    many = n_valid > topk
    thr = jnp.where(many, thr, INT_MIN + 1)
    c_gt = _count_lanes(lambda keys, c0: keys > thr, key_ref, plan, tk)
    c_eq = _count_lanes(lambda keys, c0: keys == thr, key_ref, plan, tk)
    need = topk - c_gt
    excess = many & (c_eq > need)
    n_col_bits = max(1, (key_ref.shape[1] - 1).bit_length())

    def tie_search():
        def jbody(it, last):
            cand = last + jnp.left_shift(jnp.int32(1), n_col_bits - 1 - it)
            cnt = _count_lanes(lambda keys, c0: (keys == thr) & (c0 + lane_iota < cand), key_ref, plan, tk)
            return jnp.where(cnt <= need - 1, cand, last)

        return lax.fori_loop(0, n_col_bits, jbody, jnp.zeros((rows, 1), I32))

    any_excess = jnp.max(jnp.where(excess, 1, 0)) > 0
    last_tie = lax.cond(any_excess, tie_search, lambda: jnp.zeros((rows, 1), I32))
    return thr, jnp.where(excess, last_tie, jnp.int32(2 ** 30))


SEARCH_DIGIT_BITS = 4


def _top_k_mask_params_small(keys, col, topk, n_valid):
    rows = keys.shape[0]
    count = lambda hit: jnp.sum(jnp.where(hit, 1, 0), axis=1, keepdims=True)
    thr = jnp.full((rows, 1), INT_MIN, I32)
    for shift in range(32 - SEARCH_DIGIT_BITS, -1, -SEARCH_DIGIT_BITS):
        digit = jnp.zeros((rows, 1), I32)
        for d in range(1, 2 ** SEARCH_DIGIT_BITS):
            step = np.int64(d << shift).astype(np.int32)
            digit = digit + jnp.where(count(keys >= thr + jnp.int32(step)) >= topk, 1, 0)
        thr = thr + jnp.left_shift(digit, shift)
    many = n_valid > topk
    thr = jnp.where(many, thr, INT_MIN + 1)
    need = topk - count(keys > thr)
    tie = keys == thr
    excess = many & (count(tie) > need)
    n_col_bits = max(1, (keys.shape[1] - 1).bit_length())

    def tie_search():
        last = jnp.zeros((rows, 1), I32)
        for it in range(n_col_bits):
            cand = last + (1 << (n_col_bits - 1 - it))
            last = jnp.where(count(tie & (col < cand)) <= need - 1, cand, last)
        return last

    any_excess = jnp.max(jnp.where(excess, 1, 0)) > 0
    last_tie = lax.cond(any_excess, tie_search, lambda: jnp.zeros((rows, 1), I32))
    return thr, jnp.where(excess, last_tie, jnp.int32(2 ** 30))


def _block_cover(n_cmp_rows):
    c0 = lax.broadcasted_iota(I32, (n_cmp_rows, LANE), 0) * CMP_STRIDE
    s0 = lax.broadcasted_iota(I32, (n_cmp_rows, LANE), 1) * SLC_BLOCK
    return jnp.where(c0 < s0 + SLC_BLOCK, jnp.where(c0 + CMP_BLOCK > s0, 1.0, 0.0), 0.0).astype(BF16)


def _block_importance(p_sum, cover, row_pos):
    p_hi = p_sum.astype(BF16)
    p_lo = (p_sum - p_hi.astype(F32)).astype(BF16)
    imp = jnp.dot(p_hi, cover, preferred_element_type=F32) + jnp.dot(p_lo, cover, preferred_element_type=F32)
    jb = lax.broadcasted_iota(I32, imp.shape, 1)
    cur = row_pos // SLC_BLOCK
    forced = (jb == 0) | (jb == cur) | (jb == cur - 1)
    imp = jnp.where(forced, jnp.inf, imp)
    return jnp.where(jb <= cur, imp, -jnp.inf)


def _dsa_prompt_kernel(iq_ref, iw_ref, ikt_ref, q_ref, kt_ref, v_ref, o_ref, key_ref, *, topk, tk,
                       n_idx_heads, n_groups, rep):
    i = pl.program_id(1)
    t0 = i * Q_BLOCK
    plan = _tile_plan(t0 + Q_BLOCK, tk)
    row = t0 + lax.broadcasted_iota(I32, (Q_BLOCK, 1), 0)
    hd = kt_ref.shape[0] // n_groups
    di = ikt_ref.shape[0]

    iw = iw_ref[0]
    iq = iq_ref[0]
    iq_heads = [iq[:, h * di:(h + 1) * di] for h in range(n_idx_heads)]

    def score_step(_, off, width):
        acc = jnp.zeros((Q_BLOCK, width), F32)
        for h in range(n_idx_heads):
            r = jnp.dot(iq_heads[h], ikt_ref[:, pl.ds(off, width)], preferred_element_type=F32)
            acc = acc + jnp.maximum(r, 0.0) * iw[:, h:h + 1]
        col = off + lax.broadcasted_iota(I32, (Q_BLOCK, width), 1)
        key_ref[:, pl.ds(off, width)] = jnp.where(col <= row, _sortable_key(acc), INT_MIN)
        return 0

    _over_tiles(plan, tk, score_step, 0)

    thr, last_tie = _top_k_mask_params(key_ref, plan, tk, topk, row + 1)

    q = q_ref[0]
    q_groups = [_stack_heads(q, g * rep, rep, hd) for g in range(n_groups)]
    rows_g = rep * Q_BLOCK

    def attn_step(carry, off, width):
        keys = key_ref[:, pl.ds(off, width)]
        col = off + lax.broadcasted_iota(I32, (Q_BLOCK, width), 1)
        sel = ((keys > thr) | ((keys == thr) & (col <= last_tie)))[None]
        v_t = v_ref[0, pl.ds(off, width), :]
        return tuple(
            _flash_step(carry[g], q_groups[g], kt_ref[g * hd:(g + 1) * hd, pl.ds(off, width)],
                        v_t[:, g * hd:(g + 1) * hd], sel, rep)
            for g in range(n_groups))

    fin = _over_tiles(plan, tk, attn_step, tuple(_flash_init(rows_g, hd) for _ in range(n_groups)))
    for g in range(n_groups):
        o = _flash_finish(fin[g])
        for r in range(rep):
            h = g * rep + r
            o_ref[0, :, h * hd:(h + 1) * hd] = o[r * Q_BLOCK:(r + 1) * Q_BLOCK]


def dsa_prompt_attention(q, kt, v, iq, ikt, iw, topk, n_groups, n_idx_heads, tk=ATTN_COLS):
    b, s, hd_all = q.shape
    gd, di = kt.shape[0], ikt.shape[0]
    d = gd // n_groups
    tk = min(tk, s)
    kern = functools.partial(_dsa_prompt_kernel, topk=topk, tk=tk, n_idx_heads=n_idx_heads, n_groups=n_groups,
                             rep=hd_all // gd)
    q_block = lambda width: pl.BlockSpec((1, Q_BLOCK, width), lambda bb, i: (bb, i, 0))
    return pl.pallas_call(
        kern,
        grid=(b, s // Q_BLOCK),
        in_specs=[
            q_block(n_idx_heads * di), q_block(n_idx_heads),
            pl.BlockSpec((di, s), lambda bb, i: (0, bb)),
            q_block(hd_all),
            pl.BlockSpec((gd, s), lambda bb, i: (0, bb)),
            pl.BlockSpec((1, s, gd), lambda bb, i: (bb, 0, 0)),
        ],
        out_specs=q_block(hd_all),
        out_shape=jax.ShapeDtypeStruct((b, s, hd_all), F32),
        scratch_shapes=[pltpu.VMEM((Q_BLOCK, s), I32)],
        compiler_params=pltpu.CompilerParams(
            dimension_semantics=("parallel", "arbitrary"), vmem_limit_bytes=VMEM_LIMIT),
        name="dsa_prompt_attention",
    )(iq, iw, ikt, q, kt, v)


def _top_blocks(imp_ref, n_live, n_sel):
    imp_t = imp_ref[...]
    j_iota = lax.broadcasted_iota(I32, imp_t.shape, 0)

    def body(k, rank):
        row_k = imp_ref[pl.ds(k, 1), :]
        earlier = jnp.where(j_iota > k, 1.0, 0.0)
        return rank + jnp.where(row_k > imp_t, 1.0, jnp.where(row_k == imp_t, earlier, 0.0))

    rank = lax.fori_loop(0, n_live, body, jnp.zeros(imp_t.shape, F32))
    return jnp.where(rank < n_sel, 1.0, 0.0)


def _nsa_prompt_kernel(q_ref, gate_ref, kct_ref, vc_ref, kst_ref, vs_ref, kwt_ref, vw_ref, o_ref, imp_ref, *,
                       tk, n_cmp, n_blk, n_groups, rep, win_len):
    i = pl.program_id(1)
    t0 = i * Q_BLOCK
    plan = _tile_plan(t0 + Q_BLOCK, tk)
    hd = kst_ref.shape[0] // n_groups
    rows = rep * Q_BLOCK
    nc_pad = kct_ref.shape[-1]
    row = t0 + lax.broadcasted_iota(I32, (Q_BLOCK, 1), 0)
    gates = jax.nn.sigmoid(gate_ref[0])
    n_heads = n_groups * rep
    q_all = q_ref[0]

    cover = _block_cover(nc_pad)
    c_idx = lax.broadcasted_iota(I32, (Q_BLOCK, nc_pad), 1)
    c_ok = ((c_idx < n_cmp) & (c_idx * CMP_STRIDE + (CMP_BLOCK - 1) <= row))[None]
    w_start = pl.multiple_of(jnp.maximum(t0 - WINDOW, 0), Q_BLOCK)
    w_col = w_start + lax.broadcasted_iota(I32, (Q_BLOCK, win_len), 1)
    w_d = row - w_col
    w_ok = ((w_d >= 0) & (w_d < WINDOW))[None]

    for g in range(n_groups):
        q = _stack_heads(q_all, g * rep, rep, hd)
        lo, hi = g * hd, (g + 1) * hd

        s = jnp.dot(q, kct_ref[0, lo:hi, :], preferred_element_type=F32)
        s = jnp.where(c_ok, s.reshape(rep, Q_BLOCK, nc_pad), -jnp.inf)
        p = _softmax_rows(s)
        o_cmp = jnp.dot(p.reshape(rows, nc_pad).astype(BF16), vc_ref[0, :, lo:hi], preferred_element_type=F32)

        imp = _block_importance(jnp.sum(p, axis=0), cover, row)
        imp_ref[...] = imp.T
        n_live = jnp.minimum((t0 + Q_BLOCK - 1) // SLC_BLOCK + 1, n_blk)
        sel_blocks = _top_blocks(imp_ref, n_live, min(N_SLC, n_blk)).T.astype(BF16)

        def slc_step(carry, off, width):
            col = off + lax.broadcasted_iota(I32, (Q_BLOCK, width), 1)
            blk_of_col = (off + lax.broadcasted_iota(I32, (LANE, width), 1)) // SLC_BLOCK
            expand = jnp.where(lax.broadcasted_iota(I32, (LANE, width), 0) == blk_of_col, 1.0, 0.0).astype(BF16)
            hit = jnp.dot(sel_blocks, expand, preferred_element_type=F32)
            sel = (jnp.where(col <= row, hit, 0.0) > 0.5)[None]
            return _flash_step(carry, q, kst_ref[lo:hi, pl.ds(off, width)], vs_ref[0, pl.ds(off, width), lo:hi], sel,
                               rep)

        o_slc = _flash_finish(_over_tiles(plan, tk, slc_step, _flash_init(rows, hd)))

        s = jnp.dot(q, kwt_ref[lo:hi, pl.ds(w_start, win_len)], preferred_element_type=F32)
        s = jnp.where(w_ok, s.reshape(rep, Q_BLOCK, win_len), -jnp.inf)
        p = _softmax_rows(s).reshape(rows, win_len)
        o_win = jnp.dot(p.astype(BF16), vw_ref[0, pl.ds(w_start, win_len), lo:hi], preferred_element_type=F32)

        for r in range(rep):
            h = g * rep + r
            rs = slice(r * Q_BLOCK, (r + 1) * Q_BLOCK)
            o_ref[0, :, h * hd:(h + 1) * hd] = (
                gates[:, h:h + 1] * o_cmp[rs] + gates[:, n_heads + h:n_heads + h + 1] * o_slc[rs]
                + gates[:, 2 * n_heads + h:2 * n_heads + h + 1] * o_win[rs])


def nsa_prompt_attention(q, gate_logits, kct, vc, n_cmp, kst, vs, kwt, vw, n_groups, tk=ATTN_COLS):
    b, s, hd_all = q.shape
    gd, nc_pad = kct.shape[1], kct.shape[2]
    n_blk = -(-s // SLC_BLOCK)
    assert n_blk <= LANE and s % Q_BLOCK == 0
    tk = min(tk, s)
    win_len = min(WINDOW + Q_BLOCK, s)
    kern = functools.partial(_nsa_prompt_kernel, tk=tk, n_cmp=n_cmp, n_blk=n_blk, n_groups=n_groups,
                             rep=hd_all // gd, win_len=win_len)
    per_b = lambda *blk: pl.BlockSpec((1,) + blk, lambda bb, i: (bb,) + (0,) * len(blk))
    keys_t = pl.BlockSpec((gd, s), lambda bb, i: (0, bb))
    q_block = lambda width: pl.BlockSpec((1, Q_BLOCK, width), lambda bb, i: (bb, i, 0))
    return pl.pallas_call(
        kern,
        grid=(b, s // Q_BLOCK),
        in_specs=[q_block(hd_all), q_block(gate_logits.shape[-1]), per_b(gd, nc_pad), per_b(nc_pad, gd),
                  keys_t, per_b(s, gd), keys_t, per_b(s, gd)],
        out_specs=q_block(hd_all),
        out_shape=jax.ShapeDtypeStruct((b, s, hd_all), F32),
        scratch_shapes=[pltpu.VMEM((LANE, Q_BLOCK), F32)],
        compiler_params=pltpu.CompilerParams(
            dimension_semantics=("parallel", "arbitrary"), vmem_limit_bytes=VMEM_LIMIT),
        name="nsa_prompt_attention",
    )(q, gate_logits, kct, vc, kst, vs, kwt, vw)


def _compress_finish(a, pew_ref, w2_ref, n_groups, hd):
    half = n_groups * hd
    first, second = a[:, :half], a[:, half:]
    second = jnp.concatenate([second[1:], jnp.zeros((1, half), F32)], axis=0)
    out = []
    for g in range(n_groups):
        hcol = first[:, g * hd:(g + 1) * hd] + second[:, g * hd:(g + 1) * hd] + pew_ref[...]
        out.append(jnp.dot(jax.nn.silu(hcol).astype(BF16), w2_ref[...], preferred_element_type=F32))
    return out


def _compress_kernel(c_ref, wc_ref, pew_ref, w2_ref, o_ref, *, n_groups, hd, transposed):
    a = jnp.dot(c_ref[0].astype(BF16), wc_ref[...], preferred_element_type=F32)
    res = jnp.concatenate(_compress_finish(a, pew_ref, w2_ref, n_groups, hd), axis=1)
    o_ref[0] = (res.T if transposed else res).astype(BF16)


def _compress_chunk_weights(w1, n_groups, hd):
    wr = w1.reshape(2, CMP_STRIDE, hd, hd)
    eye = jnp.eye(n_groups, dtype=w1.dtype)
    wc = jnp.einsum('hrdo,gk->rgdhko', wr, eye)
    return wc.reshape(CMP_STRIDE * n_groups * hd, 2 * n_groups * hd)


def _compress_weights(pe, w1, w2, n_groups, hd):
    return _compress_chunk_weights(w1, n_groups, hd).astype(BF16), mm(pe.reshape(1, -1), w1), w2.astype(BF16)


def compress_rows(rows, cmp_w, n_groups, transposed):
    b, l, gd = rows.shape
    hd = gd // n_groups
    n_chunks = l // CMP_STRIDE
    cw = CMP_STRIDE * gd
    wc, pew, w2 = _compress_weights(*cmp_w, n_groups, hd)
    out_blk = (gd, n_chunks) if transposed else (n_chunks, gd)
    return pl.pallas_call(
        functools.partial(_compress_kernel, n_groups=n_groups, hd=hd, transposed=transposed),
        grid=(b,),
        in_specs=[pl.BlockSpec((1, n_chunks, cw), lambda bb: (bb, 0, 0)), _resident(wc.shape), _resident(pew.shape),
                  _resident(w2.shape)],
        out_specs=pl.BlockSpec((1,) + out_blk, lambda bb: (bb, 0, 0)),
        out_shape=jax.ShapeDtypeStruct((b,) + out_blk, BF16),
        compiler_params=pltpu.CompilerParams(dimension_semantics=("parallel",), vmem_limit_bytes=VMEM_LIMIT),
        name="compress_rows",
    )(rows.reshape(b, n_chunks, cw), wc, pew, w2)


def _attend_cols(q, kt, vt, sel, rep):
    n = kt.shape[1]
    s = jnp.dot(q, kt, preferred_element_type=F32)
    s = jnp.where(sel, s.reshape(rep, TQ, n), -jnp.inf)
    p = _softmax_rows(s).reshape(rep * TQ, n)
    return p, lax.dot_general(p.astype(BF16), vt, (((1,), (1,)), ((), ())), preferred_element_type=F32)


def _head_rows(x):
    b, t, h, d = x.shape
    x = jnp.pad(x.astype(BF16), ((0, 0), (0, TQ - t), (0, 0), (0, 0)))
    return x.transpose(0, 2, 1, 3).reshape(b, h * TQ, d)


def _new_cols(x, width):
    b, t = x.shape[:2]
    return jnp.pad(x.reshape(b, t, -1).astype(BF16).transpose(0, 2, 1), ((0, 0), (0, 0), (0, width - t)))


def _cols_view(pool):
    lead, rows = pool.shape[:2], pool.shape[2]
    perm = (0, 1) + tuple(range(3, pool.ndim)) + (2,)
    return pool.transpose(perm).reshape(lead + (-1, rows))


def _page_specs(block, layer, n_pages, pp):
    def spec(j):
        return pl.BlockSpec((1, 1) + block,
                            lambda bb, p, pt: (layer, pt[bb * n_pages + p * pp + j]) + (0,) * len(block))
    return [spec(j) for j in range(pp)]


def _per_batch(*blk):
    return pl.BlockSpec((1,) + blk, lambda bb, p, pt: (bb,) + (0,) * len(blk))


def _whole(*blk):
    return pl.BlockSpec(blk, lambda bb, p, pt: (0,) * len(blk))


def _dsa_sample_kernel(pt_ref, iq_ref, iw_ref, q_ref, ikn_ref, kn_ref, vn_ref, *rest, topk, past, n_new,
                       n_idx_heads, n_groups, rep, pp):
    idx_pages = rest[:pp]
    k_pages = rest[pp:2 * pp]
    v_pages = rest[2 * pp:3 * pp]
    o_ref, key_ref, kst, vst = rest[3 * pp:]
    p_step = pl.program_id(1)
    hd = q_ref.shape[-1]
    page = idx_pages[0].shape[-1]
    n_tiles = past // page + 1
    iw = iw_ref[0]
    tpos = past + jnp.minimum(lax.broadcasted_iota(I32, (TQ, 1), 0), n_new - 1)

    def scores(ikt):
        r = jnp.dot(iq_ref[0], ikt, preferred_element_type=F32)
        r = jnp.maximum(r, 0.0).reshape(n_idx_heads, TQ, ikt.shape[1]) * iw
        return jnp.sum(r, axis=0)

    for j in range(pp):
        off = pl.multiple_of((p_step * pp + j) * page, page)
        key_ref[:, pl.ds(off, page)] = _sortable_key(scores(idx_pages[j][0, 0].astype(BF16)))
        kst[:, pl.ds(off, page)] = k_pages[j][0, 0].astype(BF16)
        vst[:, pl.ds(off, page)] = v_pages[j][0, 0].astype(BF16)

    @pl.when(p_step == pl.num_programs(1) - 1)
    def _():
        c = lax.broadcasted_iota(I32, (TQ, page), 1)
        fresh_ok = (c < n_new) & (past + c <= tpos)
        key_ref[:, past:past + page] = jnp.where(fresh_ok, _sortable_key(scores(ikn_ref[0])), INT_MIN)
        kst[:, past:past + page] = kn_ref[0]
        vst[:, past:past + page] = vn_ref[0]
        keys = key_ref[...]
        col = lax.broadcasted_iota(I32, keys.shape, 1)
        thr, last_tie = _top_k_mask_params_small(keys, col, topk, tpos + 1)
        sel = ((keys > thr) | ((keys == thr) & (col <= last_tie)))[None]
        for g in range(n_groups):
            gs = slice(g * hd, (g + 1) * hd)
            _, o = _attend_cols(q_ref[0, g * rep * TQ:(g + 1) * rep * TQ], kst[gs, :], vst[gs, :], sel, rep)
            for r in range(rep):
                h = g * rep + r
                o_ref[0, :, h * hd:(h + 1) * hd] = o[r * TQ:(r + 1) * TQ]


def dsa_sample_attention(q, k, v, iq, ik, iw, pools_k, pools_v, pools_idx, layer, page_table, topk):
    b, t, h, d = q.shape
    g = k.shape[2]
    hi, di = iq.shape[2], iq.shape[3]
    n_pages = page_table.shape[1]
    page = pools_k.shape[2]
    past = n_pages * page
    pp = min(PAGES_PER_STEP, n_pages)
    assert n_pages % pp == 0 and t <= TQ
    iw_h = jnp.pad(iw.astype(F32), ((0, 0), (0, TQ - t), (0, 0))).transpose(0, 2, 1)[..., None]
    kern = functools.partial(_dsa_sample_kernel, topk=topk, past=past, n_new=t, n_idx_heads=hi, n_groups=g,
                             rep=h // g, pp=pp)
    o = pl.pallas_call(
        kern,
        grid_spec=pltpu.PrefetchScalarGridSpec(
            num_scalar_prefetch=1,
            grid=(b, n_pages // pp),
            in_specs=[_per_batch(hi * TQ, di), _per_batch(hi, TQ, 1), _per_batch(h * TQ, d), _per_batch(di, page),
                      _per_batch(g * d, page), _per_batch(g * d, page)]
                     + _page_specs((di, page), layer, n_pages, pp)
                     + _page_specs((g * d, page), layer, n_pages, pp)
                     + _page_specs((g * d, page), layer, n_pages, pp),
            out_specs=_per_batch(TQ, h * d),
            scratch_shapes=[pltpu.VMEM((TQ, past + page), I32), pltpu.VMEM((g * d, past + page), BF16),
                            pltpu.VMEM((g * d, past + page), BF16)],
        ),
        out_shape=jax.ShapeDtypeStruct((b, TQ, h * d), F32),
        compiler_params=pltpu.CompilerParams(
            dimension_semantics=("parallel", "arbitrary"), vmem_limit_bytes=VMEM_LIMIT),
        name="dsa_sample_attention",
    )(page_table.reshape(-1).astype(I32), _head_rows(iq), iw_h, _head_rows(q), _new_cols(ik, page),
      _new_cols(k, page), _new_cols(v, page),
      *([_cols_view(pools_idx)] * pp), *([_cols_view(pools_k)] * pp), *([_cols_view(pools_v)] * pp))
    return o[:, :t]


def _top_blocks_rows(imp, n_blk, n_sel):
    j_iota = lax.broadcasted_iota(I32, imp.shape, 1)
    rank = jnp.zeros(imp.shape, F32)
    for k in range(n_blk):
        col_k = imp[:, k:k + 1]
        earlier = jnp.where(j_iota > k, 1.0, 0.0)
        rank = rank + jnp.where(col_k > imp, 1.0, jnp.where(col_k == imp, earlier, 0.0))
    return jnp.where((rank < n_sel) & (j_iota < n_blk), 1.0, 0.0)


def _nsa_sample_kernel(pt_ref, q_ref, gate_ref, ksn_ref, vsn_ref, kwn_ref, vwn_ref, wink_ref, winv_ref,
                       wck_ref, wcv_ref, pewk_ref, pewv_ref, w2k_ref, w2v_ref, *rest, past, n_new, n_groups, rep, pp):
    kc_pages = rest[:pp]
    vc_pages = rest[pp:2 * pp]
    ks_pages = rest[2 * pp:3 * pp]
    vs_pages = rest[3 * pp:4 * pp]
    o_ref, kcs, vcs, kst, vst, kwt, vwt = rest[4 * pp:]
    p_step = pl.program_id(1)
    hd = q_ref.shape[-1]
    gd = n_groups * hd
    page = ks_pages[0].shape[-1]
    n_chunks = kcs.shape[0] // CMP_STRIDE
    n_cmp = n_chunks - CMP_BLOCK // CMP_STRIDE + 1
    n_blk = -(-(past + n_new) // SLC_BLOCK)
    w_buf = wink_ref.shape[-1]
    n_heads = n_groups * rep
    rows = rep * TQ

    for j in range(pp):
        off = pl.multiple_of((p_step * pp + j) * page, page)
        kcs[pl.ds(off, page), :] = kc_pages[j][0, 0].T
        vcs[pl.ds(off, page), :] = vc_pages[j][0, 0].T
        kst[:, pl.ds(off, page)] = ks_pages[j][0, 0].astype(BF16)
        vst[:, pl.ds(off, page)] = vs_pages[j][0, 0].astype(BF16)

    @pl.when(p_step == pl.num_programs(1) - 1)
    def _():
        kst[:, past:past + page] = ksn_ref[0]
        vst[:, past:past + page] = vsn_ref[0]
        kwt[:, 0:w_buf] = wink_ref[0, 0].astype(BF16)
        vwt[:, 0:w_buf] = winv_ref[0, 0].astype(BF16)
        kwt[:, w_buf:w_buf + LANE] = kwn_ref[0]
        vwt[:, w_buf:w_buf + LANE] = vwn_ref[0]

        tpos = past + jnp.minimum(lax.broadcasted_iota(I32, (TQ, 1), 0), n_new - 1)
        gates = jax.nn.sigmoid(gate_ref[0])

        def compress(rows_ref, wc_ref, pew_ref, w2_ref):
            a = jnp.zeros((n_chunks, 2 * gd), F32)
            for r in range(CMP_STRIDE):
                xr = rows_ref[pl.ds(r, n_chunks, stride=CMP_STRIDE), :].astype(BF16)
                a = a + jnp.dot(xr, wc_ref[r * gd:(r + 1) * gd, :], preferred_element_type=F32)
            out = _compress_finish(a, pew_ref, w2_ref, n_groups, hd)
            return [o.astype(BF16) for o in out]

        kcmp = compress(kcs, wck_ref, pewk_ref, w2k_ref)
        vcmp = compress(vcs, wcv_ref, pewv_ref, w2v_ref)

        c_idx = lax.broadcasted_iota(I32, (TQ, n_chunks), 1)
        c_ok = ((c_idx < n_cmp) & (c_idx * CMP_STRIDE + (CMP_BLOCK - 1) <= tpos))[None]
        cover = _block_cover(n_chunks)
        n_slc = kst.shape[1]
        col = lax.broadcasted_iota(I32, (TQ, n_slc), 1)
        blk_of_col = lax.broadcasted_iota(I32, (LANE, n_slc), 1) // SLC_BLOCK
        expand = jnp.where(lax.broadcasted_iota(I32, (LANE, n_slc), 0) == blk_of_col, 1.0, 0.0).astype(BF16)
        n_win = kwt.shape[1]
        w_c = lax.broadcasted_iota(I32, (TQ, n_win), 1)
        w_pos = past - w_buf + w_c
        w_d = tpos - w_pos
        w_ok = ((w_c < w_buf + n_new) & (w_pos >= 0) & (w_d >= 0) & (w_d < WINDOW))[None]

        for g in range(n_groups):
            q = q_ref[0, g * rows:(g + 1) * rows]
            gs = slice(g * hd, (g + 1) * hd)
            s = lax.dot_general(q, kcmp[g], (((1,), (1,)), ((), ())), preferred_element_type=F32)
            p_cmp = _softmax_rows(jnp.where(c_ok, s.reshape(rep, TQ, n_chunks), -jnp.inf))
            o_cmp = jnp.dot(p_cmp.reshape(rows, n_chunks).astype(BF16), vcmp[g], preferred_element_type=F32)
            imp = _block_importance(jnp.sum(p_cmp, axis=0), cover, tpos)
            sel_blocks = _top_blocks_rows(imp, n_blk, min(N_SLC, n_blk)).astype(BF16)
            hit = jnp.dot(sel_blocks, expand, preferred_element_type=F32)
            sel = (jnp.where(col <= tpos, hit, 0.0) > 0.5)[None]
            _, o_slc = _attend_cols(q, kst[gs, :], vst[gs, :], sel, rep)
            _, o_win = _attend_cols(q, kwt[gs, :], vwt[gs, :], w_ok, rep)
            for r in range(rep):
                h = g * rep + r
                rs = slice(r * TQ, (r + 1) * TQ)
                o_ref[0, :, h * hd:(h + 1) * hd] = (
                    gates[:, h:h + 1] * o_cmp[rs] + gates[:, n_heads + h:n_heads + h + 1] * o_slc[rs]
                    + gates[:, 2 * n_heads + h:2 * n_heads + h + 1] * o_win[rs])


def nsa_sample_attention(q, gate_logits, ks, vs, kw, vw, cmp_k, cmp_v, pools_kc, pools_vc, pools_ks, pools_vs,
                         wins_k, wins_v, layer, page_table):
    b, t, h, d = q.shape
    g = ks.shape[2]
    gd = g * d
    n_pages = page_table.shape[1]
    page = pools_ks.shape[2]
    past = n_pages * page
    pp = min(PAGES_PER_STEP, n_pages)
    assert n_pages % pp == 0 and t <= TQ and (past + t) // CMP_STRIDE * CMP_STRIDE == past
    cw = CMP_STRIDE * gd
    w_buf = wins_k.shape[2]

    wck, pewk, w2k = _compress_weights(*cmp_k, g, d)
    wcv, pewv, w2v = _compress_weights(*cmp_v, g, d)
    kern = functools.partial(_nsa_sample_kernel, past=past, n_new=t, n_groups=g, rep=h // g, pp=pp)
    win_spec = pl.BlockSpec((1, 1, gd, w_buf), lambda bb, p, pt: (layer, bb, 0, 0))
    o = pl.pallas_call(
        kern,
        grid_spec=pltpu.PrefetchScalarGridSpec(
            num_scalar_prefetch=1,
            grid=(b, n_pages // pp),
            in_specs=[_per_batch(h * TQ, d), _per_batch(TQ, 3 * h), _per_batch(gd, page), _per_batch(gd, page),
                      _per_batch(gd, LANE), _per_batch(gd, LANE), win_spec, win_spec,
                      _whole(cw, 2 * gd), _whole(cw, 2 * gd), _whole(1, d), _whole(1, d), _whole(d, d), _whole(d, d)]
                     + _page_specs((gd, page), layer, n_pages, pp) + _page_specs((gd, page), layer, n_pages, pp)
                     + _page_specs((gd, page), layer, n_pages, pp) + _page_specs((gd, page), layer, n_pages, pp),
            out_specs=_per_batch(TQ, h * d),
            scratch_shapes=[pltpu.VMEM((past, gd), F32), pltpu.VMEM((past, gd), F32),
                            pltpu.VMEM((gd, past + page), BF16), pltpu.VMEM((gd, past + page), BF16),
                            pltpu.VMEM((gd, w_buf + LANE), BF16), pltpu.VMEM((gd, w_buf + LANE), BF16)],
        ),
        out_shape=jax.ShapeDtypeStruct((b, TQ, h * d), F32),
        compiler_params=pltpu.CompilerParams(
            dimension_semantics=("parallel", "arbitrary"), vmem_limit_bytes=VMEM_LIMIT),
        name="nsa_sample_attention",
    )(page_table.reshape(-1).astype(I32), _head_rows(q),
      jnp.pad(gate_logits.astype(F32), ((0, 0), (0, TQ - t), (0, 0))),
      _new_cols(ks, page), _new_cols(vs, page), _new_cols(kw, LANE), _new_cols(vw, LANE),
      _cols_view(wins_k), _cols_view(wins_v), wck, wcv, pewk, pewv, w2k, w2v,
      *([_cols_view(pools_kc)] * pp), *([_cols_view(pools_vc)] * pp),
      *([_cols_view(pools_ks)] * pp), *([_cols_view(pools_vs)] * pp))
    return o[:, :t]


def last_rows(a, n):
    t = a.shape[1]
    if t >= n:
        return a[:, t - n:]
    return jnp.pad(a, ((0, 0), (n - t, 0)) + ((0, 0),) * (a.ndim - 2))


def mem_kv(mem, g, w_kv):
    b, m, _ = mem.shape
    k, v = jnp.split(mm(mem, w_kv, g), 2, axis=-1)
    return k.reshape(b, m, X_HEADS, X_HEAD_DIM), v.reshape(b, m, X_HEADS, X_HEAD_DIM)


def _xattn_kernel(x_ref, g_ref, wq_ref, mk_ref, mv_ref, wo_ref, o_ref, *, n_heads, hd):
    xn = _rms(x_ref[...], g_ref[...]).astype(BF16)
    q = jnp.dot(xn, wq_ref[...], preferred_element_type=F32).astype(BF16)
    mk = mk_ref[0].astype(BF16)
    mv = mv_ref[0].astype(BF16)
    outs = []
    for h in range(n_heads):
        hs = slice(h * hd, (h + 1) * hd)
        s = lax.dot_general(q[:, hs], mk[:, hs], (((1,), (1,)), ((), ())), preferred_element_type=F32) * (hd ** -0.5)
        e = jnp.exp(s - jnp.max(s, axis=-1, keepdims=True))
        p = e / jnp.sum(e, axis=-1, keepdims=True)
        outs.append(jnp.dot(p.astype(BF16), mv[:, hs], preferred_element_type=F32).astype(BF16))
    a = jnp.concatenate(outs, axis=1)
    for c0 in range(0, o_ref.shape[1], MM_COLS):
        cs = slice(c0, c0 + MM_COLS)
        o_ref[:, cs] = x_ref[:, cs] + jnp.dot(a, wo_ref[:, cs], preferred_element_type=F32)


def cross_attn_rows(x, g, mk, mv, w_q, w_o):
    b, t, d = x.shape
    mlen, n_heads, hd = mk.shape[1:]
    hw = n_heads * hd
    tm = _pick_tile(t, (512, 256, 128, 64, 32, 16, 8))
    steps = t // tm
    rows = pl.BlockSpec((tm, d), lambda i: (i, 0))
    mem = pl.BlockSpec((1, mlen, hw), lambda i: (i // steps, 0, 0))
    out = pl.pallas_call(
        functools.partial(_xattn_kernel, n_heads=n_heads, hd=hd),
        grid=(b * steps,),
        in_specs=[rows, _resident((1, d)), _resident((d, hw)), mem, mem, _resident((hw, d))],
        out_specs=rows,
        out_shape=jax.ShapeDtypeStruct((b * t, d), F32),
        compiler_params=pltpu.CompilerParams(dimension_semantics=("parallel",), vmem_limit_bytes=VMEM_LIMIT),
        name="cross_attn_rows",
    )(x.reshape(b * t, d), g.reshape(1, d).astype(F32), w_q.astype(BF16), mk.reshape(b, mlen, hw),
      mv.reshape(b, mlen, hw), w_o.astype(BF16))
    return out.reshape(b, t, d)


def cross_attn(x, g, mk, mv, w_q, w_o):
    b, t, _ = x.shape
    if t % SUBLANE == 0:
        return cross_attn_rows(x, g, mk, mv, w_q, w_o)
    q = mm(x, w_q, g).reshape(b, t, X_HEADS, X_HEAD_DIM)
    s = jnp.einsum('bthd,bmhd->bhtm', q, mk).astype(F32) * (X_HEAD_DIM ** -0.5)
    p = jax.nn.softmax(s, axis=-1).astype(x.dtype)
    return mm(jnp.einsum('bhtm,bmhd->bthd', p, mv).reshape(b, t, -1), w_o, res=x)


A_WIDTHS = (A_HEADS * HEAD_DIM, A_KV_HEADS * HEAD_DIM, A_KV_HEADS * HEAD_DIM, IDX_HEADS * IDX_DIM, IDX_DIM, IDX_HEADS)


def dsa_project(x, g, w_in, pos, keys_transposed):
    b, t, d = x.shape
    keys = [(F32, False), (BF16, keys_transposed)]
    segments = [(True, HEAD_DIM ** -0.5, [(BF16, False)]), (True, None, keys), (False, None, [(F32, False), (BF16, False)]),
                (True, None, [(BF16, False)]), (True, None, keys),
                (False, IDX_HEADS ** -0.5 * IDX_DIM ** -0.5, [(F32, False)])]
    return project(x.reshape(b * t, d), g, _split_cols(w_in, A_WIDTHS), segments, jnp.tile(pos, b))


def dsa_prompt(x, g, w_in, w_out):
    b, s, _ = x.shape
    q, k, kt, v, vb, iq, ik, ikt, iw = dsa_project(x, g, w_in, jnp.arange(s), True)
    o = dsa_prompt_attention(q.reshape(b, s, -1), kt, vb.reshape(b, s, -1), iq.reshape(b, s, -1), ikt,
                             iw.reshape(b, s, -1), min(TOPK_MAX, s // 4), A_KV_HEADS, IDX_HEADS)
    state = (k.reshape(b, s, A_KV_HEADS, HEAD_DIM), v.reshape(b, s, A_KV_HEADS, HEAD_DIM), ik.reshape(b, s, IDX_DIM))
    return mm(o, w_out, res=x), state


def dsa_sample(x, g, w_in, w_out, pools_k, pools_v, pools_idx, layer, page_table):
    b, t, _ = x.shape
    past = page_table.shape[1] * pools_k.shape[2]
    q, k, kb, v, vb, iq, ik, ikb, iw = dsa_project(x, g, w_in, past + jnp.arange(t), False)
    o = dsa_sample_attention(q.reshape(b, t, A_HEADS, HEAD_DIM), kb.reshape(b, t, A_KV_HEADS, HEAD_DIM),
                             vb.reshape(b, t, A_KV_HEADS, HEAD_DIM), iq.reshape(b, t, IDX_HEADS, IDX_DIM),
                             ikb.reshape(b, t, IDX_DIM), iw.reshape(b, t, IDX_HEADS), pools_k, pools_v, pools_idx,
                             layer, page_table, min(TOPK_MAX, (past + t) // 4))
    state = (k.reshape(b, t, A_KV_HEADS, HEAD_DIM), v.reshape(b, t, A_KV_HEADS, HEAD_DIM), ik.reshape(b, t, IDX_DIM))
    return mm(o, w_out, res=x), state


def _shift_rows(x, prev, k):
    return jnp.concatenate([prev[prev.shape[0] - k:], x[:x.shape[0] - k]], axis=0)


def _tile_transpose(m, rows_out, cols_out):
    r, c = m.shape
    tile = m
    if c < LANE:
        tile = jnp.concatenate([tile, jnp.zeros((r, LANE - c), m.dtype)], axis=1)
    if r < LANE:
        tile = jnp.concatenate([tile, jnp.zeros((LANE - r, LANE), m.dtype)], axis=0)
    return tile.T[:rows_out, :cols_out]


def _ssd_kernel(z_ref, x_ref, dt_ref, cw_ref, cb_ref, dtb_ref, alog_ref, dskip_ref, buf_ref, h0_ref,
                y_ref, hout_ref, carry_ref, ht_ref, *, n_heads, n_groups, hd, n_state):
    c = pl.program_id(1)
    l = x_ref.shape[0]
    d_inner = n_heads * hd
    rep = n_heads // n_groups
    n_pairs = n_heads // 2

    @pl.when(c == 0)
    def _():
        carry_ref[...] = buf_ref[0]
        for p in range(n_pairs):
            ht_ref[p] = jnp.concatenate([_tile_transpose(h0_ref[0, 2 * p], n_state, hd),
                                         _tile_transpose(h0_ref[0, 2 * p + 1], n_state, hd)], axis=1)

    x = x_ref[...]
    prev = carry_ref[...]
    conv = cb_ref[...] + x * cw_ref[B_CONV - 1:B_CONV, :]
    for k in range(1, B_CONV):
        conv = conv + _shift_rows(x, prev, k) * cw_ref[B_CONV - 1 - k:B_CONV - k, :]
    carry_ref[...] = x[l - SUBLANE:]
    act = jax.nn.silu(conv)
    gn = n_groups * n_state
    xs, bm, cm = act[:, :d_inner], act[:, d_inner:d_inner + gn], act[:, d_inner + gn:]

    dt = jax.nn.softplus(dt_ref[...] + dtb_ref[...])
    a = -jnp.exp(alog_ref[...])
    acs = dt * a
    k = 1
    while k < l:
        acs = acs + jnp.concatenate([jnp.zeros((k, n_heads), F32), acs[:l - k]], axis=0)
        k *= 2
    acs_t = jnp.concatenate([acs, jnp.zeros((l, LANE - n_heads), F32)], axis=1).T

    spread = jnp.where(lax.broadcasted_iota(I32, (n_heads, d_inner), 1) // hd
                       == lax.broadcasted_iota(I32, (n_heads, d_inner), 0), 1.0, 0.0).astype(BF16)

    def per_column(v):
        out = jnp.zeros((l, d_inner), F32)
        for _ in range(3):
            part = v.astype(BF16)
            out = out + jnp.dot(part, spread, preferred_element_type=F32)
            v = v - part.astype(F32)
        return out

    dt_cols = per_column(dt)
    acs_cols = per_column(acs)
    last_cols = acs_cols[l - 1:l, :]
    xr_all = xs * dt_cols
    xw_all = (xr_all * jnp.exp(last_cols - acs_cols)).astype(BF16)
    grow_cols = jnp.exp(acs_cols)
    decay_cols = jnp.exp(last_cols)
    y_skip = xs * dskip_ref[...]
    gate = jax.nn.silu(z_ref[...])
    causal = lax.broadcasted_iota(I32, (l, l), 0) >= lax.broadcasted_iota(I32, (l, l), 1)
    first_head = lax.broadcasted_iota(I32, (l, LANE), 1) < hd

    for g in range(n_groups):
        bm_g = bm[:, g * n_state:(g + 1) * n_state]
        cm_g = cm[:, g * n_state:(g + 1) * n_state].astype(BF16)
        cb = lax.dot_general(cm_g, bm_g.astype(BF16), (((1,), (1,)), ((), ())), preferred_element_type=F32)
        bm_t = bm_g.T.astype(BF16)
        for p in range(g * rep // 2, (g + 1) * rep // 2):
            ps = slice(p * LANE, (p + 1) * LANE)
            xr = xr_all[:, ps]
            lhs = []
            for h in (2 * p, 2 * p + 1):
                seg = acs[:, h:h + 1] - acs_t[h:h + 1, :]
                lhs.append(cb * jnp.where(causal, jnp.exp(jnp.where(causal, seg, 0.0)), 0.0))
            rhs = jnp.concatenate([jnp.where(first_head, xr, 0.0), jnp.where(first_head, 0.0, xr)], axis=0)
            y = jnp.dot(jnp.concatenate(lhs, axis=1).astype(BF16), rhs.astype(BF16), preferred_element_type=F32)
            ht = ht_ref[p]
            y = y + jnp.dot(cm_g, ht.astype(BF16), preferred_element_type=F32) * grow_cols[:, ps]
            ht_ref[p] = ht * decay_cols[:, ps] + jnp.dot(bm_t, xw_all[:, ps], preferred_element_type=F32)
            y_ref[:, ps] = (y + y_skip[:, ps]) * gate[:, ps]

    @pl.when(c == pl.num_programs(1) - 1)
    def _():
        for p in range(n_pairs):
            ht = ht_ref[p]
            hout_ref[0, 2 * p] = _tile_transpose(ht[:, :hd], hd, n_state)
            hout_ref[0, 2 * p + 1] = _tile_transpose(ht[:, hd:], hd, n_state)


def ssd_prompt(z, xbc, dt, conv_w, conv_b, dt_bias, a_log, d_skip, conv_buf, ssm0, n_groups):
    b, n_heads, hd, n_state = ssm0.shape
    m, w = xbc.shape
    s = m // b
    l = min(B_CHUNK, s)
    nc = s // l
    d_inner = n_heads * hd
    buf = jnp.pad(conv_buf.astype(F32), ((0, 0), (SUBLANE - (B_CONV - 1), 0), (0, 0)))
    dskip_full = jnp.repeat(d_skip.astype(F32), hd).reshape(1, d_inner)
    rows = lambda width: pl.BlockSpec((l, width), lambda bb, c: (bb * nc + c, 0))
    state = pl.BlockSpec((1, n_heads, hd, n_state), lambda bb, c: (bb, 0, 0, 0))
    kern = functools.partial(_ssd_kernel, n_heads=n_heads, n_groups=n_groups, hd=hd, n_state=n_state)
    return pl.pallas_call(
        kern,
        grid=(b, nc),
        in_specs=[rows(d_inner), rows(w), rows(n_heads), _resident((B_CONV, w)), _resident((1, w)),
                  _resident((1, n_heads)), _resident((1, n_heads)), _resident((1, d_inner)),
                  pl.BlockSpec((1, SUBLANE, w), lambda bb, c: (bb, 0, 0)), state],
        out_specs=[rows(d_inner), state],
        out_shape=[jax.ShapeDtypeStruct((m, d_inner), F32), jax.ShapeDtypeStruct(ssm0.shape, F32)],
        scratch_shapes=[pltpu.VMEM((SUBLANE, w), F32), pltpu.VMEM((n_heads // 2, n_state, 2 * hd), F32)],
        compiler_params=pltpu.CompilerParams(
            dimension_semantics=("parallel", "arbitrary"), vmem_limit_bytes=VMEM_LIMIT),
        name="ssd_prompt",
    )(z, xbc, dt, conv_w.astype(F32), conv_b.reshape(1, w).astype(F32), dt_bias.reshape(1, n_heads).astype(F32),
      a_log.reshape(1, n_heads).astype(F32), dskip_full, buf, ssm0.astype(F32))


def mamba_prompt(x, g, w_in, conv_w, conv_b, dt_bias, a_log, d_skip, g_norm, w_out, conv_buf, ssm0):
    b, s, d = x.shape
    plain = (False, None, [(F32, False)])
    z, xbc, dt = project(x.reshape(b * s, d), g, _split_cols(w_in, (B_D_INNER, B_CONV_DIM, B_HEADS)),
                         [plain, plain, plain], jnp.zeros((b * s,), I32))
    y, ssm = ssd_prompt(z, xbc, dt, conv_w, conv_b, dt_bias, a_log, d_skip, conv_buf, ssm0, B_GROUPS)
    xpad = jnp.concatenate([conv_buf.astype(F32), xbc.reshape(b, s, -1)[:, s - (B_CONV - 1):]], axis=1)
    new_buf = xpad[:, xpad.shape[1] - (B_CONV - 1):]
    return mm(y.reshape(b, s, -1), w_out, g_norm, res=x), new_buf, ssm.astype(ssm0.dtype)


def ssd_chunked(x, dt, a, bm, cm, h0):
    bsz, t, nh, p = x.shape
    g, n = bm.shape[2], bm.shape[3]
    r = nh // g
    l = min(B_CHUNK, t)
    nc = t // l
    xr = (x * dt[..., None]).reshape(bsz, nc, l, g, r, p)
    acs = jnp.cumsum((dt * a).reshape(bsz, nc, l, g, r), axis=2)
    br = bm.reshape(bsz, nc, l, g, n)
    cr = cm.reshape(bsz, nc, l, g, n)
    seg = acs[:, :, :, None] - acs[:, :, None, :]
    causal = jnp.tril(jnp.ones((l, l), bool))[:, :, None, None]
    decay = jnp.where(causal, jnp.exp(jnp.where(causal, seg, 0.0)), 0.0)
    cb = jnp.einsum('bcign,bcjgn->bcijg', cr, br)
    y_intra = jnp.einsum('bcijg,bcijgr,bcjgrp->bcigrp', cb, decay, xr)
    to_end = jnp.exp(acs[:, :, -1:] - acs)
    s_chunk = jnp.einsum('bclgn,bclgr,bclgrp->bcgrpn', br, to_end, xr)
    d_chunk = jnp.exp(acs[:, :, -1])

    def step(hc, inp):
        s_c, d_c = inp
        return hc * d_c[..., None, None] + s_c, hc

    h_fin, h_in = lax.scan(step, h0.reshape(bsz, g, r, p, n),
                           (jnp.moveaxis(s_chunk, 1, 0), jnp.moveaxis(d_chunk, 1, 0)))
    h_in = jnp.moveaxis(h_in, 0, 1)
    y_inter = jnp.einsum('bcign,bcigr,bcgrpn->bcigrp', cr, jnp.exp(acs), h_in)
    return (y_intra + y_inter).reshape(bsz, t, nh, p), h_fin.reshape(bsz, nh, p, n)


def mamba_mixer(x, g, w_in, conv_w, conv_b, dt_bias, a_log, d_skip, g_norm, w_out, conv_buf, ssm0):
    b, t, _ = x.shape
    proj = mm(x, w_in, g)
    z = proj[..., :B_D_INNER]
    xbc = proj[..., B_D_INNER:B_D_INNER + B_CONV_DIM]
    dt = proj[..., B_D_INNER + B_CONV_DIM:]
    xpad = jnp.concatenate([conv_buf.astype(xbc.dtype), xbc], axis=1)
    conv = conv_b + sum(xpad[:, j:j + t] * conv_w[j] for j in range(B_CONV))
    xbc = jax.nn.silu(conv)
    new_buf = xpad[:, t:]
    gn = B_GROUPS * B_STATE
    xs = xbc[..., :B_D_INNER].reshape(b, t, B_HEADS, B_HEADDIM).astype(F32)
    bm = xbc[..., B_D_INNER:B_D_INNER + gn].reshape(b, t, B_GROUPS, B_STATE).astype(F32)
    cm = xbc[..., B_D_INNER + gn:].reshape(b, t, B_GROUPS, B_STATE).astype(F32)
    dt = jax.nn.softplus(dt.astype(F32) + dt_bias.astype(F32))
    a = -jnp.exp(a_log.astype(F32))
    y, ssm = ssd_chunked(xs, dt, a, bm, cm, ssm0.astype(F32))
    y = (y + xs * d_skip.astype(F32)[:, None]).reshape(b, t, B_D_INNER).astype(x.dtype)
    y = y * jax.nn.silu(z)
    return mm(y, w_out, g_norm, res=x), new_buf, ssm.astype(ssm0.dtype)


C_KV_WIDTH = C_KV_HEADS * HEAD_DIM
C_WIDTHS = (C_HEADS * HEAD_DIM,) + (C_KV_WIDTH,) * 6 + (3 * C_HEADS,)


def nsa_project(x, g, w_in, pos, keys_transposed):
    b, t, d = x.shape
    keys = (True, None, [(F32, False), (BF16, keys_transposed)])
    vals = (False, None, [(F32, False), (BF16, False)])
    segments = [(True, HEAD_DIM ** -0.5, [(BF16, False)]), (True, None, [(F32, False)]), (False, None, [(F32, False)]),
                keys, vals, keys, vals, (False, None, [(F32, False)])]
    return project(x.reshape(b * t, d), g, _split_cols(w_in, C_WIDTHS), segments, jnp.tile(pos, b))


def nsa_prompt(x, g, w_in, w_out, cmp_k, cmp_v, w_buf):
    b, s, _ = x.shape
    q, kc, vc, ks, kst, vs, vsb, kw, kwt, vw, vwb, gates = nsa_project(x, g, w_in, jnp.arange(s), True)
    seq = lambda a: a.reshape(b, s, -1)
    kct = compress_rows(seq(kc), cmp_k, C_KV_HEADS, True)
    vcc = compress_rows(seq(vc), cmp_v, C_KV_HEADS, False)
    n_cmp = s // CMP_STRIDE - CMP_BLOCK // CMP_STRIDE + 1
    o = nsa_prompt_attention(seq(q), seq(gates), kct, vcc, n_cmp, kst, seq(vsb), kwt, seq(vwb), C_KV_HEADS)
    heads = lambda a: a.reshape(b, s, C_KV_HEADS, HEAD_DIM)
    state = (heads(kc), heads(vc), heads(ks), heads(vs), last_rows(heads(kw), w_buf), last_rows(heads(vw), w_buf))
    return mm(o, w_out, res=x), state


def nsa_sample(x, g, w_in, w_out, cmp_k, cmp_v, pools_kc, pools_vc, pools_ks, pools_vs, wins_k, wins_v, layer,
               page_table):
    b, t, _ = x.shape
    past = page_table.shape[1] * pools_kc.shape[2]
    q, kc, vc, ks, ksb, vs, vsb, kw, kwb, vw, vwb, gates = nsa_project(x, g, w_in, past + jnp.arange(t), False)
    heads = lambda a: a.reshape(b, t, C_KV_HEADS, HEAD_DIM)
    o = nsa_sample_attention(q.reshape(b, t, C_HEADS, HEAD_DIM), gates.reshape(b, t, -1), heads(ksb), heads(vsb),
                             heads(kwb), heads(vwb), cmp_k, cmp_v, pools_kc, pools_vc, pools_ks, pools_vs, wins_k,
                             wins_v, layer, page_table)
    win_k, win_v = wins_k[layer], wins_v[layer]
    w_buf = win_k.shape[1]
    kwin = jnp.concatenate([win_k, heads(kw)], axis=1)
    vwin = jnp.concatenate([win_v, heads(vw)], axis=1)
    state = (heads(kc), heads(vc), heads(ks), heads(vs), last_rows(kwin, w_buf), last_rows(vwin, w_buf))
    return mm(o, w_out, res=x), state


def kernel(x_prompt, x_sample, cache_a_k, cache_a_v, cache_a_idx, state_b_ssm, state_b_conv, cache_c_cmp_k, cache_c_cmp_v, cache_c_slc_k, cache_c_slc_v, cache_c_win_k, cache_c_win_v, cache_mem_k, cache_mem_v, page_table, mem_prompt, g_ffn1, ffn1_wi, ffn1_wo, g_mix, g_xattn, g_mem, x_w_q, x_w_kv, x_w_o, g_ffn2, ffn2_wi, ffn2_wo, g_final, a_w_in, a_w_out, b_w_in, b_conv_w, b_conv_b, b_dt_bias, b_a_log, b_d_skip, b_g_norm, b_w_out, c_w_in, c_w_out, c_pe_k, c_w1_k, c_w2_k, c_pe_v, c_w1_v, c_w2_v):
    xp, xs = x_prompt, x_sample
    w_buf = cache_c_win_k.shape[2]
    ak_p, av_p, ai_p, bs_p, bc_p = [], [], [], [], []
    cck_p, ccv_p, csk_p, csv_p, cwk_p, cwv_p = [], [], [], [], [], []
    mk_p, mv_p = [], []
    ak_s, av_s, ai_s, bs_s, bc_s = [], [], [], [], []
    cck_s, ccv_s, csk_s, csv_s, cwk_s, cwv_s = [], [], [], [], [], []
    ia = ib = ic = 0
    for i in range(DEPTH):
        xp = ffn_residual(xp, g_ffn1[i], ffn1_wi[i], ffn1_wo[i])
        xs = ffn_residual(xs, g_ffn1[i], ffn1_wi[i], ffn1_wo[i])
        kind = i % N_MIXERS
        if kind == 0:
            op, (k1, v1, i1) = dsa_prompt(xp, g_mix[i], a_w_in[ia], a_w_out[ia])
            os_, (k2, v2, i2) = dsa_sample(xs, g_mix[i], a_w_in[ia], a_w_out[ia], cache_a_k, cache_a_v, cache_a_idx,
                                           ia, page_table)
            ak_p.append(k1); av_p.append(v1); ai_p.append(i1)
            ak_s.append(k2); av_s.append(v2); ai_s.append(i2)
            ia += 1
        elif kind == 1:
            conv0 = jnp.zeros((xp.shape[0], B_CONV - 1, B_CONV_DIM), xp.dtype)
            ssm0 = jnp.zeros((xp.shape[0], B_HEADS, B_HEADDIM, B_STATE), F32)
            wts = (b_w_in[ib], b_conv_w[ib], b_conv_b[ib], b_dt_bias[ib], b_a_log[ib], b_d_skip[ib],
                   b_g_norm[ib], b_w_out[ib])
            op, c1, s1 = mamba_prompt(xp, g_mix[i], *wts, conv0, ssm0)
            os_, c2, s2 = mamba_mixer(xs, g_mix[i], *wts, state_b_conv[ib], state_b_ssm[ib])
            bc_p.append(c1); bs_p.append(s1); bc_s.append(c2); bs_s.append(s2)
            ib += 1
        else:
            cmp_k = (c_pe_k[ic], c_w1_k[ic], c_w2_k[ic])
            cmp_v = (c_pe_v[ic], c_w1_v[ic], c_w2_v[ic])
            op, st1 = nsa_prompt(xp, g_mix[i], c_w_in[ic], c_w_out[ic], cmp_k, cmp_v, w_buf)
            os_, st2 = nsa_sample(xs, g_mix[i], c_w_in[ic], c_w_out[ic], cmp_k, cmp_v, cache_c_cmp_k, cache_c_cmp_v,
                                  cache_c_slc_k, cache_c_slc_v, cache_c_win_k, cache_c_win_v, ic, page_table)
            for lst, arr in zip((cck_p, ccv_p, csk_p, csv_p, cwk_p, cwv_p), st1):
                lst.append(arr)
            for lst, arr in zip((cck_s, ccv_s, csk_s, csv_s, cwk_s, cwv_s), st2):
                lst.append(arr)
            ic += 1
        xp, xs = op, os_
        mk, mv = mem_kv(mem_prompt, g_mem[i], x_w_kv[i])
        mk_p.append(mk); mv_p.append(mv)
        xp = cross_attn(xp, g_xattn[i], mk, mv, x_w_q[i], x_w_o[i])
        xs = cross_attn(xs, g_xattn[i], cache_mem_k[i], cache_mem_v[i], x_w_q[i], x_w_o[i])
        xp = ffn_residual(xp, g_ffn2[i], ffn2_wi[i], ffn2_wo[i])
        xs = ffn_residual(xs, g_ffn2[i], ffn2_wi[i], ffn2_wo[i])
    y_prompt = rms_norm_rows(xp, g_final)
    y_sample = rms_norm_rows(xs, g_final)
    st = jnp.stack
    return (y_prompt, y_sample,
            st(ak_p), st(av_p), st(ai_p), st(bs_p), st(bc_p),
            st(cck_p), st(ccv_p), st(csk_p), st(csv_p), st(cwk_p), st(cwv_p),
            st(mk_p), st(mv_p),
            st(ak_s), st(av_s), st(ai_s), st(bs_s), st(bc_s),
            st(cck_s), st(ccv_s), st(csk_s), st(csv_s), st(cwk_s), st(cwv_s))
```

```python
import functools

import jax
import jax.numpy as jnp
import numpy as np
from jax import lax
from jax.experimental import pallas as pl
from jax.experimental.pallas import tpu as pltpu

F32 = jnp.float32
BF16 = jnp.bfloat16
I32 = jnp.int32
INT_MIN = -2 ** 31
EPS = 1e-6

D_MODEL = 1024
DEPTH = 4
N_MIXERS = 3
HEAD_DIM = 64
ROPE_THETA = 10000.0
Q_BLOCK = 128
A_HEADS = D_MODEL // HEAD_DIM
A_KV_HEADS = 4
IDX_HEADS = 8
IDX_DIM = 64
TOPK_MAX = 256
B_D_INNER = 2 * D_MODEL
B_HEADDIM = 64
B_HEADS = B_D_INNER // B_HEADDIM
B_GROUPS = 4
B_STATE = 128
B_CONV = 4
B_CONV_DIM = B_D_INNER + 2 * B_GROUPS * B_STATE
B_CHUNK = 128
C_HEADS = D_MODEL // HEAD_DIM
C_KV_HEADS = 2
CMP_BLOCK = 32
CMP_STRIDE = 16
SLC_BLOCK = 64
N_SLC = 16
WINDOW = 512
X_HEADS = 4
X_HEAD_DIM = 128
D_FF = 2816

LANE = 128
SUBLANE = 8
VMEM_LIMIT = 48 * 1024 * 1024
MM_COLS = 512
FF_CHUNK = 256
ATTN_COLS = 1024
TQ = SUBLANE
PAGES_PER_STEP = 16
DECODE_ROWS = 2


def _pick_tile(n, candidates):
    for c in candidates:
        if n % c == 0:
            return c
    return n


def _resident(shape):
    return pl.BlockSpec(shape, lambda *_: (0,) * len(shape), pipeline_mode=pl.Buffered(1))


def _rms(x, g):
    return x * lax.rsqrt(jnp.mean(x * x, axis=-1, keepdims=True) + EPS) * g


def _nmm_kernel(x_ref, g_ref, w_ref, *rest, norm, residual):
    o_ref = rest[-1]
    x = x_ref[...]
    if norm:
        x = _rms(x, g_ref[...])
    xb = x.astype(BF16)
    n = w_ref.shape[1]
    for c0 in range(0, n, MM_COLS):
        c1 = min(c0 + MM_COLS, n)
        y = jnp.dot(xb, w_ref[:, c0:c1], preferred_element_type=F32)
        o_ref[:, c0:c1] = rest[0][:, c0:c1] + y if residual else y


def norm_matmul(x, w, g=None, res=None):
    m, k = x.shape
    n = w.shape[1]
    assert res is None or (n % LANE == 0 and m % SUBLANE == 0)
    n_pad = -(-n // LANE) * LANE
    wb = w.astype(BF16)
    if n_pad != n:
        wb = jnp.pad(wb, ((0, 0), (0, n_pad - n)))
    m_orig = m
    if m % SUBLANE != 0:
        m = -(-m // 256) * 256
        x = jnp.pad(x, ((0, m - m_orig), (0, 0)))
    tm = _pick_tile(m, (256, 128, 64, 32, 16, 8))
    norm = g is not None
    gg = (g if norm else jnp.ones((k,), F32)).reshape(1, k).astype(F32)
    row_block = lambda width: pl.BlockSpec((tm, width), lambda i: (i, 0))
    extra = [] if res is None else [res]
    out = pl.pallas_call(
        functools.partial(_nmm_kernel, norm=norm, residual=res is not None),
        grid=(m // tm,),
        in_specs=[row_block(k), _resident((1, k)), _resident((k, n_pad))] + [row_block(n_pad) for _ in extra],
        out_specs=row_block(n_pad),
        out_shape=jax.ShapeDtypeStruct((m, n_pad), F32),
        compiler_params=pltpu.CompilerParams(dimension_semantics=("parallel",), vmem_limit_bytes=VMEM_LIMIT),
        name="norm_matmul",
    )(x, gg, wb, *extra)
    if n_pad != n or m != m_orig:
        out = out[:m_orig, :n]
    return out


def mm(x, w, g=None, res=None):
    lead = x.shape[:-1]
    r2 = None if res is None else res.reshape(-1, res.shape[-1])
    return norm_matmul(x.reshape(-1, x.shape[-1]), w, g, r2).reshape(lead + (w.shape[1],))


def _ffn_kernel(x_ref, g_ref, wg_ref, wu_ref, wo_ref, o_ref, act_ref, *, n_chunks):
    xn = _rms(x_ref[...], g_ref[...]).astype(BF16)

    for c in range(n_chunks):
        cs = slice(c * FF_CHUNK, (c + 1) * FF_CHUNK)
        hg = jnp.dot(xn, wg_ref[:, cs], preferred_element_type=F32)
        hu = jnp.dot(xn, wu_ref[:, cs], preferred_element_type=F32)
        act_ref[:, cs] = (jax.nn.silu(hg) * hu).astype(BF16)
    for c0 in range(0, o_ref.shape[1], MM_COLS):
        cs = slice(c0, c0 + MM_COLS)
        o_ref[:, cs] = x_ref[:, cs] + 0.5 * jnp.dot(act_ref[...], wo_ref[:, cs], preferred_element_type=F32)


def ffn_residual(x, g, wi, wo):
    lead, d = x.shape[:-1], x.shape[-1]
    x2 = x.reshape(-1, d)
    m = x2.shape[0]
    f = wo.shape[0]
    assert f % FF_CHUNK == 0 and d % MM_COLS == 0
    tm = _pick_tile(m, (512, 256, 128, 64, 32, 16, 8))
    wb = wi.astype(BF16)
    out = pl.pallas_call(
        functools.partial(_ffn_kernel, n_chunks=f // FF_CHUNK),
        grid=(m // tm,),
        in_specs=[pl.BlockSpec((tm, d), lambda i: (i, 0)), _resident((1, d)), _resident((d, f)), _resident((d, f)),
                  _resident((f, d))],
        out_specs=pl.BlockSpec((tm, d), lambda i: (i, 0)),
        out_shape=jax.ShapeDtypeStruct((m, d), F32),
        scratch_shapes=[pltpu.VMEM((tm, f), BF16)],
        compiler_params=pltpu.CompilerParams(dimension_semantics=("parallel",), vmem_limit_bytes=VMEM_LIMIT),
        name="ffn_residual",
    )(x2, g.reshape(1, d).astype(F32), wb[:, :f], wb[:, f:], wo.astype(BF16))
    return out.reshape(lead + (d,))


def _rmsnorm_kernel(x_ref, g_ref, o_ref):
    o_ref[...] = _rms(x_ref[...], g_ref[...])


def rms_norm_rows(x, g):
    lead, d = x.shape[:-1], x.shape[-1]
    x2 = x.reshape(-1, d)
    m = x2.shape[0]
    tm = _pick_tile(m, (512, 256, 128, 64, 32, 16, 8))
    row_block = pl.BlockSpec((tm, d), lambda i: (i, 0))
    out = pl.pallas_call(
        _rmsnorm_kernel, grid=(m // tm,), in_specs=[row_block, _resident((1, d))], out_specs=row_block,
        out_shape=jax.ShapeDtypeStruct((m, d), F32),
        compiler_params=pltpu.CompilerParams(dimension_semantics=("parallel",)), name="rms_norm_rows",
    )(x2, g.reshape(1, d).astype(F32))
    return out.reshape(lead + (d,))


def rope_tables(pos):
    half = HEAD_DIM // 2
    inv = ROPE_THETA ** (-jnp.arange(half, dtype=F32) / half)
    ang = pos.astype(F32)[:, None] * inv[None, :]
    cos, sin = jnp.cos(ang), jnp.sin(ang)
    reps = LANE // HEAD_DIM
    return (jnp.tile(jnp.concatenate([cos, cos], axis=1), (1, reps)),
            jnp.tile(jnp.concatenate([-sin, sin], axis=1), (1, reps)))


def _rope_cols(y, cos, sin):
    w = y.shape[1]
    half = HEAD_DIM // 2
    lane = lax.broadcasted_iota(I32, y.shape, 1)
    partner = jnp.where(lane % HEAD_DIM < half, pltpu.roll(y, w - half, axis=1), pltpu.roll(y, half, axis=1))
    reps = w // LANE
    return y * jnp.tile(cos, (1, reps)) + partner * jnp.tile(sin, (1, reps))


def _proj_kernel(x_ref, g_ref, cos_ref, sin_ref, w_ref, *out_refs, segments):
    xb = _rms(x_ref[...], g_ref[...]).astype(BF16)
    cos, sin = cos_ref[...], sin_ref[...]
    k = 0
    for start, width, rope, scale, outs in segments:
        y = jnp.dot(xb, w_ref[:, start:start + width], preferred_element_type=F32)
        if rope:
            y = _rope_cols(y, cos, sin)
        if scale is not None:
            y = y * scale
        for used, dtype, transposed in outs:
            v = y[:, :used]
            out_refs[k][...] = (v.T if transposed else v).astype(dtype)
            k += 1


def project(x, g, w_cols, segments, pos_rows):
    m, kdim = x.shape
    packed, segs, out_shapes, out_specs = [], [], [], []
    tm = _pick_tile(m, (256, 128, 64, 32, 16, 8))
    start = 0
    for wc, (rope, scale, outs) in zip(w_cols, segments):
        n = wc.shape[1]
        n_pad = -(-n // LANE) * LANE
        packed.append(jnp.pad(wc.astype(BF16), ((0, 0), (0, n_pad - n))))
        segs.append((start, n_pad, rope, scale, tuple((n, dt, tr) for dt, tr in outs)))
        for dt, tr in outs:
            if tr:
                out_shapes.append(jax.ShapeDtypeStruct((n, m), dt))
                out_specs.append(pl.BlockSpec((n, tm), lambda i: (0, i)))
            else:
                out_shapes.append(jax.ShapeDtypeStruct((m, n), dt))
                out_specs.append(pl.BlockSpec((tm, n), lambda i: (i, 0)))
        start += n_pad
    wb = jnp.concatenate(packed, axis=1)
    cos, sin = rope_tables(pos_rows)
    row_block = lambda width: pl.BlockSpec((tm, width), lambda i: (i, 0))
    return pl.pallas_call(
        functools.partial(_proj_kernel, segments=tuple(segs)),
        grid=(m // tm,),
        in_specs=[row_block(kdim), _resident((1, kdim)), row_block(LANE), row_block(LANE), _resident(wb.shape)],
        out_specs=out_specs,
        out_shape=out_shapes,
        compiler_params=pltpu.CompilerParams(dimension_semantics=("parallel",), vmem_limit_bytes=VMEM_LIMIT),
        name="project",
    )(x, g.reshape(1, kdim).astype(F32), cos, sin, wb)


def _split_cols(w, widths):
    cols = np.cumsum((0,) + tuple(widths))
    return [w[:, cols[j]:cols[j + 1]] for j in range(len(widths))]


def _softmax_rows(s):
    m = jnp.max(s, axis=-1, keepdims=True)
    e = jnp.exp(s - jnp.where(m == -jnp.inf, 0.0, m))
    den = jnp.sum(e, axis=-1, keepdims=True)
    return e / jnp.where(den > 0, den, 1.0)


def _flash_step(carry, q, kt_tile, v_tile, sel, rep):
    m, l, acc = carry
    rows, tk = q.shape[0], kt_tile.shape[1]
    s = jnp.dot(q, kt_tile, preferred_element_type=F32)
    s = jnp.where(sel, s.reshape(rep, Q_BLOCK, tk), -jnp.inf).reshape(rows, tk)
    m_new = jnp.maximum(m, jnp.max(s, axis=1, keepdims=True))
    m_safe = jnp.where(m_new == -jnp.inf, 0.0, m_new)
    alpha = jnp.exp(m - m_safe)
    p = jnp.exp(s - m_safe)
    l = l * alpha + jnp.sum(p, axis=1, keepdims=True)
    pv = jnp.dot(p.astype(BF16), v_tile, preferred_element_type=F32)
    return m_new, l, acc * alpha + pv


def _flash_init(rows, hd):
    return (jnp.full((rows, 1), -jnp.inf, F32), jnp.zeros((rows, 1), F32), jnp.zeros((rows, hd), F32))


def _flash_finish(carry):
    _, l, acc = carry
    return acc / jnp.where(l > 0, l, 1.0)


def _stack_heads(x, h0, rep, hd):
    return jnp.concatenate([x[:, (h0 + r) * hd:(h0 + r + 1) * hd] for r in range(rep)], axis=0)


def _sortable_key(score):
    bits = pltpu.bitcast(score, I32)
    key = jnp.where(bits < 0, bits ^ jnp.int32(0x7FFFFFFF), bits)
    return jnp.where(score == 0.0, 0, key)


def _tile_plan(limit, tk):
    half = tk // 2
    n_half = (limit + half - 1) // half
    return n_half // 2, n_half % 2 == 1


def _over_tiles(plan, tk, step, carry):
    n_wide, has_half = plan
    half = tk // 2
    carry = lax.fori_loop(0, n_wide, lambda kt, c: step(c, pl.multiple_of(kt * tk, tk), tk), carry)
    return lax.cond(has_half, lambda c: step(c, pl.multiple_of(n_wide * tk, half), half), lambda c: c, carry)


def _count_lanes(pred_fn, key_ref, plan, tk):
    rows = key_ref.shape[0]

    def step(cnt, off, width):
        keys = key_ref[:, pl.ds(off, width)]
        for c in range(width // LANE):
            hit = pred_fn(keys[:, c * LANE:(c + 1) * LANE], off + c * LANE)
            cnt = cnt + jnp.where(hit, 1, 0)
        return cnt

    cnt = _over_tiles(plan, tk, step, jnp.zeros((rows, LANE), I32))
    return jnp.sum(cnt, axis=1, keepdims=True)


def _top_k_mask_params(key_ref, plan, tk, topk, n_valid):
    rows = key_ref.shape[0]
    lane_iota = lax.broadcasted_iota(I32, (rows, LANE), 1)

    def bit_body(it, thr):
        cand = thr + jnp.left_shift(jnp.int32(1), 31 - it)
        cnt = _count_lanes(lambda keys, c0: keys >= cand, key_ref, plan, tk)
        return jnp.where(cnt >= topk, cand, thr)

    thr = lax.fori_loop(0, 32, bit_body, jnp.full((rows, 1), INT_MIN, I32))
    many = n_valid > topk
    thr = jnp.where(many, thr, INT_MIN + 1)
    c_gt = _count_lanes(lambda keys, c0: keys > thr, key_ref, plan, tk)
    c_eq = _count_lanes(lambda keys, c0: keys == thr, key_ref, plan, tk)
    need = topk - c_gt
    excess = many & (c_eq > need)
    n_col_bits = max(1, (key_ref.shape[1] - 1).bit_length())

    def tie_search():
        def jbody(it, last):
            cand = last + jnp.left_shift(jnp.int32(1), n_col_bits - 1 - it)
            cnt = _count_lanes(lambda keys, c0: (keys == thr) & (c0 + lane_iota < cand), key_ref, plan, tk)
            return jnp.where(cnt <= need - 1, cand, last)

        return lax.fori_loop(0, n_col_bits, jbody, jnp.zeros((rows, 1), I32))

    any_excess = jnp.max(jnp.where(excess, 1, 0)) > 0
    last_tie = lax.cond(any_excess, tie_search, lambda: jnp.zeros((rows, 1), I32))
    return thr, jnp.where(excess, last_tie, jnp.int32(2 ** 30))


SEARCH_DIGIT_BITS = 4


def _top_k_mask_params_small(keys, col, topk, n_valid):
    rows = keys.shape[0]
    count = lambda hit: jnp.sum(jnp.where(hit, 1, 0), axis=1, keepdims=True)
    thr = jnp.full((rows, 1), INT_MIN, I32)
    for shift in range(32 - SEARCH_DIGIT_BITS, -1, -SEARCH_DIGIT_BITS):
        digit = jnp.zeros((rows, 1), I32)
        for d in range(1, 2 ** SEARCH_DIGIT_BITS):
            step = np.int64(d << shift).astype(np.int32)
            digit = digit + jnp.where(count(keys >= thr + jnp.int32(step)) >= topk, 1, 0)
        thr = thr + jnp.left_shift(digit, shift)
    many = n_valid > topk
    thr = jnp.where(many, thr, INT_MIN + 1)
    need = topk - count(keys > thr)
    tie = keys == thr
    excess = many & (count(tie) > need)
    n_col_bits = max(1, (keys.shape[1] - 1).bit_length())

    def tie_search():
        last = jnp.zeros((rows, 1), I32)
        for it in range(n_col_bits):
            cand = last + (1 << (n_col_bits - 1 - it))
            last = jnp.where(count(tie & (col < cand)) <= need - 1, cand, last)
        return last

    any_excess = jnp.max(jnp.where(excess, 1, 0)) > 0
    last_tie = lax.cond(any_excess, tie_search, lambda: jnp.zeros((rows, 1), I32))
    return thr, jnp.where(excess, last_tie, jnp.int32(2 ** 30))


def _block_cover(n_cmp_rows):
    c0 = lax.broadcasted_iota(I32, (n_cmp_rows, LANE), 0) * CMP_STRIDE
    s0 = lax.broadcasted_iota(I32, (n_cmp_rows, LANE), 1) * SLC_BLOCK
    return jnp.where(c0 < s0 + SLC_BLOCK, jnp.where(c0 + CMP_BLOCK > s0, 1.0, 0.0), 0.0).astype(BF16)


def _block_importance(p_sum, cover, row_pos):
    p_hi = p_sum.astype(BF16)
    p_lo = (p_sum - p_hi.astype(F32)).astype(BF16)
    imp = jnp.dot(p_hi, cover, preferred_element_type=F32) + jnp.dot(p_lo, cover, preferred_element_type=F32)
    jb = lax.broadcasted_iota(I32, imp.shape, 1)
    cur = row_pos // SLC_BLOCK
    forced = (jb == 0) | (jb == cur) | (jb == cur - 1)
    imp = jnp.where(forced, jnp.inf, imp)
    return jnp.where(jb <= cur, imp, -jnp.inf)


def _dsa_prompt_kernel(iq_ref, iw_ref, ikt_ref, q_ref, kt_ref, v_ref, o_ref, key_ref, *, topk, tk,
                       n_idx_heads, n_groups, rep):
    i = pl.program_id(1)
    t0 = i * Q_BLOCK
    plan = _tile_plan(t0 + Q_BLOCK, tk)
    row = t0 + lax.broadcasted_iota(I32, (Q_BLOCK, 1), 0)
    hd = kt_ref.shape[0] // n_groups
    di = ikt_ref.shape[0]

    iw = iw_ref[0]
    iq = iq_ref[0]
    iq_heads = [iq[:, h * di:(h + 1) * di] for h in range(n_idx_heads)]

    def score_step(_, off, width):
        acc = jnp.zeros((Q_BLOCK, width), F32)
        for h in range(n_idx_heads):
            r = jnp.dot(iq_heads[h], ikt_ref[:, pl.ds(off, width)], preferred_element_type=F32)
            acc = acc + jnp.maximum(r, 0.0) * iw[:, h:h + 1]
        col = off + lax.broadcasted_iota(I32, (Q_BLOCK, width), 1)
        key_ref[:, pl.ds(off, width)] = jnp.where(col <= row, _sortable_key(acc), INT_MIN)
        return 0

    _over_tiles(plan, tk, score_step, 0)

    thr, last_tie = _top_k_mask_params(key_ref, plan, tk, topk, row + 1)

    q = q_ref[0]
    q_groups = [_stack_heads(q, g * rep, rep, hd) for g in range(n_groups)]
    rows_g = rep * Q_BLOCK

    def attn_step(carry, off, width):
        keys = key_ref[:, pl.ds(off, width)]
        col = off + lax.broadcasted_iota(I32, (Q_BLOCK, width), 1)
        sel = ((keys > thr) | ((keys == thr) & (col <= last_tie)))[None]
        v_t = v_ref[0, pl.ds(off, width), :]
        return tuple(
            _flash_step(carry[g], q_groups[g], kt_ref[g * hd:(g + 1) * hd, pl.ds(off, width)],
                        v_t[:, g * hd:(g + 1) * hd], sel, rep)
            for g in range(n_groups))

    fin = _over_tiles(plan, tk, attn_step, tuple(_flash_init(rows_g, hd) for _ in range(n_groups)))
    for g in range(n_groups):
        o = _flash_finish(fin[g])
        for r in range(rep):
            h = g * rep + r
            o_ref[0, :, h * hd:(h + 1) * hd] = o[r * Q_BLOCK:(r + 1) * Q_BLOCK]


def dsa_prompt_attention(q, kt, v, iq, ikt, iw, topk, n_groups, n_idx_heads, tk=ATTN_COLS):
    b, s, hd_all = q.shape
    gd, di = kt.shape[0], ikt.shape[0]
    d = gd // n_groups
    tk = min(tk, s)
    kern = functools.partial(_dsa_prompt_kernel, topk=topk, tk=tk, n_idx_heads=n_idx_heads, n_groups=n_groups,
                             rep=hd_all // gd)
    q_block = lambda width: pl.BlockSpec((1, Q_BLOCK, width), lambda bb, i: (bb, i, 0))
    return pl.pallas_call(
        kern,
        grid=(b, s // Q_BLOCK),
        in_specs=[
            q_block(n_idx_heads * di), q_block(n_idx_heads),
            pl.BlockSpec((di, s), lambda bb, i: (0, bb)),
            q_block(hd_all),
            pl.BlockSpec((gd, s), lambda bb, i: (0, bb)),
            pl.BlockSpec((1, s, gd), lambda bb, i: (bb, 0, 0)),
        ],
        out_specs=q_block(hd_all),
        out_shape=jax.ShapeDtypeStruct((b, s, hd_all), F32),
        scratch_shapes=[pltpu.VMEM((Q_BLOCK, s), I32)],
        compiler_params=pltpu.CompilerParams(
            dimension_semantics=("parallel", "arbitrary"), vmem_limit_bytes=VMEM_LIMIT),
        name="dsa_prompt_attention",
    )(iq, iw, ikt, q, kt, v)


def _top_blocks(imp_ref, n_live, n_sel):
    imp_t = imp_ref[...]
    j_iota = lax.broadcasted_iota(I32, imp_t.shape, 0)

    def body(k, rank):
        row_k = imp_ref[pl.ds(k, 1), :]
        earlier = jnp.where(j_iota > k, 1.0, 0.0)
        return rank + jnp.where(row_k > imp_t, 1.0, jnp.where(row_k == imp_t, earlier, 0.0))

    rank = lax.fori_loop(0, n_live, body, jnp.zeros(imp_t.shape, F32))
    return jnp.where(rank < n_sel, 1.0, 0.0)


def _nsa_prompt_kernel(q_ref, gate_ref, kct_ref, vc_ref, kst_ref, vs_ref, kwt_ref, vw_ref, o_ref, imp_ref, *,
                       tk, n_cmp, n_blk, n_groups, rep, win_len):
    i = pl.program_id(1)
    t0 = i * Q_BLOCK
    plan = _tile_plan(t0 + Q_BLOCK, tk)
    hd = kst_ref.shape[0] // n_groups
    rows = rep * Q_BLOCK
    nc_pad = kct_ref.shape[-1]
    row = t0 + lax.broadcasted_iota(I32, (Q_BLOCK, 1), 0)
    gates = jax.nn.sigmoid(gate_ref[0])
    n_heads = n_groups * rep
    q_all = q_ref[0]

    cover = _block_cover(nc_pad)
    c_idx = lax.broadcasted_iota(I32, (Q_BLOCK, nc_pad), 1)
    c_ok = ((c_idx < n_cmp) & (c_idx * CMP_STRIDE + (CMP_BLOCK - 1) <= row))[None]
    w_start = pl.multiple_of(jnp.maximum(t0 - WINDOW, 0), Q_BLOCK)
    w_col = w_start + lax.broadcasted_iota(I32, (Q_BLOCK, win_len), 1)
    w_d = row - w_col
    w_ok = ((w_d >= 0) & (w_d < WINDOW))[None]

    for g in range(n_groups):
        q = _stack_heads(q_all, g * rep, rep, hd)
        lo, hi = g * hd, (g + 1) * hd

        s = jnp.dot(q, kct_ref[0, lo:hi, :], preferred_element_type=F32)
        s = jnp.where(c_ok, s.reshape(rep, Q_BLOCK, nc_pad), -jnp.inf)
        p = _softmax_rows(s)
        o_cmp = jnp.dot(p.reshape(rows, nc_pad).astype(BF16), vc_ref[0, :, lo:hi], preferred_element_type=F32)

        imp = _block_importance(jnp.sum(p, axis=0), cover, row)
        imp_ref[...] = imp.T
        n_live = jnp.minimum((t0 + Q_BLOCK - 1) // SLC_BLOCK + 1, n_blk)
        sel_blocks = _top_blocks(imp_ref, n_live, min(N_SLC, n_blk)).T.astype(BF16)

        def slc_step(carry, off, width):
            col = off + lax.broadcasted_iota(I32, (Q_BLOCK, width), 1)
            blk_of_col = (off + lax.broadcasted_iota(I32, (LANE, width), 1)) // SLC_BLOCK
            expand = jnp.where(lax.broadcasted_iota(I32, (LANE, width), 0) == blk_of_col, 1.0, 0.0).astype(BF16)
            hit = jnp.dot(sel_blocks, expand, preferred_element_type=F32)
            sel = (jnp.where(col <= row, hit, 0.0) > 0.5)[None]
            return _flash_step(carry, q, kst_ref[lo:hi, pl.ds(off, width)], vs_ref[0, pl.ds(off, width), lo:hi], sel,
                               rep)

        o_slc = _flash_finish(_over_tiles(plan, tk, slc_step, _flash_init(rows, hd)))

        s = jnp.dot(q, kwt_ref[lo:hi, pl.ds(w_start, win_len)], preferred_element_type=F32)
        s = jnp.where(w_ok, s.reshape(rep, Q_BLOCK, win_len), -jnp.inf)
        p = _softmax_rows(s).reshape(rows, win_len)
        o_win = jnp.dot(p.astype(BF16), vw_ref[0, pl.ds(w_start, win_len), lo:hi], preferred_element_type=F32)

        for r in range(rep):
            h = g * rep + r
            rs = slice(r * Q_BLOCK, (r + 1) * Q_BLOCK)
            o_ref[0, :, h * hd:(h + 1) * hd] = (
                gates[:, h:h + 1] * o_cmp[rs] + gates[:, n_heads + h:n_heads + h + 1] * o_slc[rs]
                + gates[:, 2 * n_heads + h:2 * n_heads + h + 1] * o_win[rs])


def nsa_prompt_attention(q, gate_logits, kct, vc, n_cmp, kst, vs, kwt, vw, n_groups, tk=ATTN_COLS):
    b, s, hd_all = q.shape
    gd, nc_pad = kct.shape[1], kct.shape[2]
    n_blk = -(-s // SLC_BLOCK)
    assert n_blk <= LANE and s % Q_BLOCK == 0
    tk = min(tk, s)
    win_len = min(WINDOW + Q_BLOCK, s)
    kern = functools.partial(_nsa_prompt_kernel, tk=tk, n_cmp=n_cmp, n_blk=n_blk, n_groups=n_groups,
                             rep=hd_all // gd, win_len=win_len)
    per_b = lambda *blk: pl.BlockSpec((1,) + blk, lambda bb, i: (bb,) + (0,) * len(blk))
    keys_t = pl.BlockSpec((gd, s), lambda bb, i: (0, bb))
    q_block = lambda width: pl.BlockSpec((1, Q_BLOCK, width), lambda bb, i: (bb, i, 0))
    return pl.pallas_call(
        kern,
        grid=(b, s // Q_BLOCK),
        in_specs=[q_block(hd_all), q_block(gate_logits.shape[-1]), per_b(gd, nc_pad), per_b(nc_pad, gd),
                  keys_t, per_b(s, gd), keys_t, per_b(s, gd)],
        out_specs=q_block(hd_all),
        out_shape=jax.ShapeDtypeStruct((b, s, hd_all), F32),
        scratch_shapes=[pltpu.VMEM((LANE, Q_BLOCK), F32)],
        compiler_params=pltpu.CompilerParams(
            dimension_semantics=("parallel", "arbitrary"), vmem_limit_bytes=VMEM_LIMIT),
        name="nsa_prompt_attention",
    )(q, gate_logits, kct, vc, kst, vs, kwt, vw)


def _compress_finish(a, pew_ref, w2_ref, n_groups, hd):
    half = n_groups * hd
    first, second = a[:, :half], a[:, half:]
    second = jnp.concatenate([second[1:], jnp.zeros((1, half), F32)], axis=0)
    out = []
    for g in range(n_groups):
        hcol = first[:, g * hd:(g + 1) * hd] + second[:, g * hd:(g + 1) * hd] + pew_ref[...]
        out.append(jnp.dot(jax.nn.silu(hcol).astype(BF16), w2_ref[...], preferred_element_type=F32))
    return out


def _compress_kernel(c_ref, wc_ref, pew_ref, w2_ref, o_ref, *, n_groups, hd, transposed):
    a = jnp.dot(c_ref[0].astype(BF16), wc_ref[...], preferred_element_type=F32)
    res = jnp.concatenate(_compress_finish(a, pew_ref, w2_ref, n_groups, hd), axis=1)
    o_ref[0] = (res.T if transposed else res).astype(BF16)


def _compress_chunk_weights(w1, n_groups, hd):
    wr = w1.reshape(2, CMP_STRIDE, hd, hd)
    eye = jnp.eye(n_groups, dtype=w1.dtype)
    wc = jnp.einsum('hrdo,gk->rgdhko', wr, eye)
    return wc.reshape(CMP_STRIDE * n_groups * hd, 2 * n_groups * hd)


def _compress_weights(pe, w1, w2, n_groups, hd):
    return _compress_chunk_weights(w1, n_groups, hd).astype(BF16), mm(pe.reshape(1, -1), w1), w2.astype(BF16)


def compress_rows(rows, cmp_w, n_groups, transposed):
    b, l, gd = rows.shape
    hd = gd // n_groups
    n_chunks = l // CMP_STRIDE
    cw = CMP_STRIDE * gd
    wc, pew, w2 = _compress_weights(*cmp_w, n_groups, hd)
    out_blk = (gd, n_chunks) if transposed else (n_chunks, gd)
    return pl.pallas_call(
        functools.partial(_compress_kernel, n_groups=n_groups, hd=hd, transposed=transposed),
        grid=(b,),
        in_specs=[pl.BlockSpec((1, n_chunks, cw), lambda bb: (bb, 0, 0)), _resident(wc.shape), _resident(pew.shape),
                  _resident(w2.shape)],
        out_specs=pl.BlockSpec((1,) + out_blk, lambda bb: (bb, 0, 0)),
        out_shape=jax.ShapeDtypeStruct((b,) + out_blk, BF16),
        compiler_params=pltpu.CompilerParams(dimension_semantics=("parallel",), vmem_limit_bytes=VMEM_LIMIT),
        name="compress_rows",
    )(rows.reshape(b, n_chunks, cw), wc, pew, w2)


def _attend_cols(q, kt, vt, sel, rep):
    n = kt.shape[1]
    s = jnp.dot(q, kt, preferred_element_type=F32)
    s = jnp.where(sel, s.reshape(rep, TQ, n), -jnp.inf)
    p = _softmax_rows(s).reshape(rep * TQ, n)
    return p, lax.dot_general(p.astype(BF16), vt, (((1,), (1,)), ((), ())), preferred_element_type=F32)


def _head_rows(x):
    b, t, h, d = x.shape
    x = jnp.pad(x.astype(BF16), ((0, 0), (0, TQ - t), (0, 0), (0, 0)))
    return x.transpose(0, 2, 1, 3).reshape(b, h * TQ, d)


def _new_cols(x, width):
    b, t = x.shape[:2]
    return jnp.pad(x.reshape(b, t, -1).astype(BF16).transpose(0, 2, 1), ((0, 0), (0, 0), (0, width - t)))


def _cols_view(pool):
    lead, rows = pool.shape[:2], pool.shape[2]
    perm = (0, 1) + tuple(range(3, pool.ndim)) + (2,)
    return pool.transpose(perm).reshape(lead + (-1, rows))


def _page_specs(block, layer, n_pages, pp):
    def spec(e, j):
        return pl.BlockSpec(
            (1, 1) + block,
            lambda bb, p, pt: (layer, pt[(bb * DECODE_ROWS + e) * n_pages + p * pp + j]) + (0,) * len(block))
    return [spec(e, j) for e in range(DECODE_ROWS) for j in range(pp)]


def _per_batch(*blk):
    return pl.BlockSpec((DECODE_ROWS,) + blk, lambda bb, p, pt: (bb,) + (0,) * len(blk))


def _whole(*blk):
    return pl.BlockSpec(blk, lambda bb, p, pt: (0,) * len(blk))


def _dsa_sample_kernel(pt_ref, iq_ref, iw_ref, q_ref, ikn_ref, kn_ref, vn_ref, *rest, topk, past, n_new,
                       n_idx_heads, n_groups, rep, pp):
    n_in = DECODE_ROWS * pp
    idx_pages, k_pages, v_pages = rest[:n_in], rest[n_in:2 * n_in], rest[2 * n_in:3 * n_in]
    o_ref, key_ref, kst, vst = rest[3 * n_in:]
    p_step = pl.program_id(1)
    hd = q_ref.shape[-1]
    page = idx_pages[0].shape[-1]
    tpos = past + jnp.minimum(lax.broadcasted_iota(I32, (TQ, 1), 0), n_new - 1)

    def scores(e, ikt):
        r = jnp.dot(iq_ref[e], ikt, preferred_element_type=F32)
        r = jnp.maximum(r, 0.0).reshape(n_idx_heads, TQ, ikt.shape[1]) * iw_ref[e]
        return jnp.sum(r, axis=0)

    for e in range(DECODE_ROWS):
        for j in range(pp):
            off = pl.multiple_of((p_step * pp + j) * page, page)
            key_ref[e, :, pl.ds(off, page)] = _sortable_key(scores(e, idx_pages[e * pp + j][0, 0].astype(BF16)))
            kst[e, :, pl.ds(off, page)] = k_pages[e * pp + j][0, 0].astype(BF16)
            vst[e, :, pl.ds(off, page)] = v_pages[e * pp + j][0, 0].astype(BF16)

    @pl.when(p_step == pl.num_programs(1) - 1)
    def _():
        for e in range(DECODE_ROWS):
            c = lax.broadcasted_iota(I32, (TQ, page), 1)
            fresh_ok = (c < n_new) & (past + c <= tpos)
            key_ref[e, :, past:past + page] = jnp.where(fresh_ok, _sortable_key(scores(e, ikn_ref[e])), INT_MIN)
            kst[e, :, past:past + page] = kn_ref[e]
            vst[e, :, past:past + page] = vn_ref[e]
            keys = key_ref[e]
            col = lax.broadcasted_iota(I32, keys.shape, 1)
            thr, last_tie = _top_k_mask_params_small(keys, col, topk, tpos + 1)
            sel = ((keys > thr) | ((keys == thr) & (col <= last_tie)))[None]
            for g in range(n_groups):
                gs = slice(g * hd, (g + 1) * hd)
                _, o = _attend_cols(q_ref[e, g * rep * TQ:(g + 1) * rep * TQ], kst[e, gs, :], vst[e, gs, :], sel, rep)
                for r in range(rep):
                    h = g * rep + r
                    o_ref[e, :, h * hd:(h + 1) * hd] = o[r * TQ:(r + 1) * TQ]


def dsa_sample_attention(q, k, v, iq, ik, iw, pools_k, pools_v, pools_idx, layer, page_table, topk):
    b, t, h, d = q.shape
    g = k.shape[2]
    hi, di = iq.shape[2], iq.shape[3]
    n_pages = page_table.shape[1]
    page = pools_k.shape[2]
    past = n_pages * page
    pp = min(PAGES_PER_STEP, n_pages)
    assert n_pages % pp == 0 and t <= TQ and b % DECODE_ROWS == 0
    n_in = DECODE_ROWS * pp
    iw_h = jnp.pad(iw.astype(F32), ((0, 0), (0, TQ - t), (0, 0))).transpose(0, 2, 1)[..., None]
    kern = functools.partial(_dsa_sample_kernel, topk=topk, past=past, n_new=t, n_idx_heads=hi, n_groups=g,
                             rep=h // g, pp=pp)
    o = pl.pallas_call(
        kern,
        grid_spec=pltpu.PrefetchScalarGridSpec(
            num_scalar_prefetch=1,
            grid=(b // DECODE_ROWS, n_pages // pp),
            in_specs=[_per_batch(hi * TQ, di), _per_batch(hi, TQ, 1), _per_batch(h * TQ, d), _per_batch(di, page),
                      _per_batch(g * d, page), _per_batch(g * d, page)]
                     + _page_specs((di, page), layer, n_pages, pp)
                     + _page_specs((g * d, page), layer, n_pages, pp)
                     + _page_specs((g * d, page), layer, n_pages, pp),
            out_specs=_per_batch(TQ, h * d),
            scratch_shapes=[pltpu.VMEM((DECODE_ROWS, TQ, past + page), I32),
                            pltpu.VMEM((DECODE_ROWS, g * d, past + page), BF16),
                            pltpu.VMEM((DECODE_ROWS, g * d, past + page), BF16)],
        ),
        out_shape=jax.ShapeDtypeStruct((b, TQ, h * d), F32),
        compiler_params=pltpu.CompilerParams(
            dimension_semantics=("parallel", "arbitrary"), vmem_limit_bytes=VMEM_LIMIT),
        name="dsa_sample_attention",
    )(page_table.reshape(-1).astype(I32), _head_rows(iq), iw_h, _head_rows(q), _new_cols(ik, page),
      _new_cols(k, page), _new_cols(v, page),
      *([_cols_view(pools_idx)] * n_in), *([_cols_view(pools_k)] * n_in), *([_cols_view(pools_v)] * n_in))
    return o[:, :t]


def _top_blocks_rows(imp, n_blk, n_sel):
    j_iota = lax.broadcasted_iota(I32, imp.shape, 1)
    rank = jnp.zeros(imp.shape, F32)
    for k in range(n_blk):
        col_k = imp[:, k:k + 1]
        earlier = jnp.where(j_iota > k, 1.0, 0.0)
        rank = rank + jnp.where(col_k > imp, 1.0, jnp.where(col_k == imp, earlier, 0.0))
    return jnp.where((rank < n_sel) & (j_iota < n_blk), 1.0, 0.0)


def _nsa_sample_kernel(pt_ref, q_ref, gate_ref, ksn_ref, vsn_ref, kwn_ref, vwn_ref, wink_ref, winv_ref,
                       wck_ref, wcv_ref, pewk_ref, pewv_ref, w2k_ref, w2v_ref, *rest, past, n_new, n_groups, rep, pp):
    n_in = DECODE_ROWS * pp
    kc_pages, vc_pages, ks_pages, vs_pages = (rest[k * n_in:(k + 1) * n_in] for k in range(4))
    o_ref, kcs, vcs, kst, vst, kwt, vwt = rest[4 * n_in:]
    p_step = pl.program_id(1)
    hd = q_ref.shape[-1]
    gd = n_groups * hd
    page = ks_pages[0].shape[-1]
    n_chunks = kcs.shape[1] // CMP_STRIDE
    n_cmp = n_chunks - CMP_BLOCK // CMP_STRIDE + 1
    n_blk = -(-(past + n_new) // SLC_BLOCK)
    w_buf = wink_ref.shape[-1]
    n_heads = n_groups * rep
    rows = rep * TQ

    for e in range(DECODE_ROWS):
        for j in range(pp):
            off = pl.multiple_of((p_step * pp + j) * page, page)
            kcs[e, pl.ds(off, page), :] = kc_pages[e * pp + j][0, 0].T
            vcs[e, pl.ds(off, page), :] = vc_pages[e * pp + j][0, 0].T
            kst[e, :, pl.ds(off, page)] = ks_pages[e * pp + j][0, 0].astype(BF16)
            vst[e, :, pl.ds(off, page)] = vs_pages[e * pp + j][0, 0].astype(BF16)

    def attend_row(e):
        kst[e, :, past:past + page] = ksn_ref[e]
        vst[e, :, past:past + page] = vsn_ref[e]
        kwt[e, :, 0:w_buf] = wink_ref[0, e].astype(BF16)
        vwt[e, :, 0:w_buf] = winv_ref[0, e].astype(BF16)
        kwt[e, :, w_buf:w_buf + LANE] = kwn_ref[e]
        vwt[e, :, w_buf:w_buf + LANE] = vwn_ref[e]

        tpos = past + jnp.minimum(lax.broadcasted_iota(I32, (TQ, 1), 0), n_new - 1)
        gates = jax.nn.sigmoid(gate_ref[e])

        def compress(rows_ref, wc_ref, pew_ref, w2_ref):
            a = jnp.zeros((n_chunks, 2 * gd), F32)
            for r in range(CMP_STRIDE):
                xr = rows_ref[e, pl.ds(r, n_chunks, stride=CMP_STRIDE), :].astype(BF16)
                a = a + jnp.dot(xr, wc_ref[r * gd:(r + 1) * gd, :], preferred_element_type=F32)
            out = _compress_finish(a, pew_ref, w2_ref, n_groups, hd)
            return [o.astype(BF16) for o in out]

        kcmp = compress(kcs, wck_ref, pewk_ref, w2k_ref)
        vcmp = compress(vcs, wcv_ref, pewv_ref, w2v_ref)

        c_idx = lax.broadcasted_iota(I32, (TQ, n_chunks), 1)
        c_ok = ((c_idx < n_cmp) & (c_idx * CMP_STRIDE + (CMP_BLOCK - 1) <= tpos))[None]
        cover = _block_cover(n_chunks)
        n_slc = kst.shape[2]
        col = lax.broadcasted_iota(I32, (TQ, n_slc), 1)
        blk_of_col = lax.broadcasted_iota(I32, (LANE, n_slc), 1) // SLC_BLOCK
        expand = jnp.where(lax.broadcasted_iota(I32, (LANE, n_slc), 0) == blk_of_col, 1.0, 0.0).astype(BF16)
        n_win = kwt.shape[2]
        w_c = lax.broadcasted_iota(I32, (TQ, n_win), 1)
        w_pos = past - w_buf + w_c
        w_d = tpos - w_pos
        w_ok = ((w_c < w_buf + n_new) & (w_pos >= 0) & (w_d >= 0) & (w_d < WINDOW))[None]

        for g in range(n_groups):
            q = q_ref[e, g * rows:(g + 1) * rows]
            gs = slice(g * hd, (g + 1) * hd)
            s = lax.dot_general(q, kcmp[g], (((1,), (1,)), ((), ())), preferred_element_type=F32)
            p_cmp = _softmax_rows(jnp.where(c_ok, s.reshape(rep, TQ, n_chunks), -jnp.inf))
            o_cmp = jnp.dot(p_cmp.reshape(rows, n_chunks).astype(BF16), vcmp[g], preferred_element_type=F32)
            imp = _block_importance(jnp.sum(p_cmp, axis=0), cover, tpos)
            sel_blocks = _top_blocks_rows(imp, n_blk, min(N_SLC, n_blk)).astype(BF16)
            hit = jnp.dot(sel_blocks, expand, preferred_element_type=F32)
            sel = (jnp.where(col <= tpos, hit, 0.0) > 0.5)[None]
            _, o_slc = _attend_cols(q, kst[e, gs, :], vst[e, gs, :], sel, rep)
            _, o_win = _attend_cols(q, kwt[e, gs, :], vwt[e, gs, :], w_ok, rep)
            for r in range(rep):
                h = g * rep + r
                rs = slice(r * TQ, (r + 1) * TQ)
                o_ref[e, :, h * hd:(h + 1) * hd] = (
                    gates[:, h:h + 1] * o_cmp[rs] + gates[:, n_heads + h:n_heads + h + 1] * o_slc[rs]
                    + gates[:, 2 * n_heads + h:2 * n_heads + h + 1] * o_win[rs])

    @pl.when(p_step == pl.num_programs(1) - 1)
    def _():
        for e in range(DECODE_ROWS):
            attend_row(e)


def nsa_sample_attention(q, gate_logits, ks, vs, kw, vw, cmp_k, cmp_v, pools_kc, pools_vc, pools_ks, pools_vs,
                         wins_k, wins_v, layer, page_table):
    b, t, h, d = q.shape
    g = ks.shape[2]
    gd = g * d
    n_pages = page_table.shape[1]
    page = pools_ks.shape[2]
    past = n_pages * page
    pp = min(PAGES_PER_STEP, n_pages)
    assert n_pages % pp == 0 and t <= TQ and (past + t) // CMP_STRIDE * CMP_STRIDE == past
    assert b % DECODE_ROWS == 0
    n_in = DECODE_ROWS * pp
    rows = DECODE_ROWS
    cw = CMP_STRIDE * gd
    w_buf = wins_k.shape[2]

    wck, pewk, w2k = _compress_weights(*cmp_k, g, d)
    wcv, pewv, w2v = _compress_weights(*cmp_v, g, d)
    kern = functools.partial(_nsa_sample_kernel, past=past, n_new=t, n_groups=g, rep=h // g, pp=pp)
    win_spec = pl.BlockSpec((1, rows, gd, w_buf), lambda bb, p, pt: (layer, bb, 0, 0))
    o = pl.pallas_call(
        kern,
        grid_spec=pltpu.PrefetchScalarGridSpec(
            num_scalar_prefetch=1,
            grid=(b // rows, n_pages // pp),
            in_specs=[_per_batch(h * TQ, d), _per_batch(TQ, 3 * h), _per_batch(gd, page), _per_batch(gd, page),
                      _per_batch(gd, LANE), _per_batch(gd, LANE), win_spec, win_spec,
                      _whole(cw, 2 * gd), _whole(cw, 2 * gd), _whole(1, d), _whole(1, d), _whole(d, d), _whole(d, d)]
                     + _page_specs((gd, page), layer, n_pages, pp) + _page_specs((gd, page), layer, n_pages, pp)
                     + _page_specs((gd, page), layer, n_pages, pp) + _page_specs((gd, page), layer, n_pages, pp),
            out_specs=_per_batch(TQ, h * d),
            scratch_shapes=[pltpu.VMEM((rows, past, gd), F32), pltpu.VMEM((rows, past, gd), F32),
                            pltpu.VMEM((rows, gd, past + page), BF16), pltpu.VMEM((rows, gd, past + page), BF16),
                            pltpu.VMEM((rows, gd, w_buf + LANE), BF16), pltpu.VMEM((rows, gd, w_buf + LANE), BF16)],
        ),
        out_shape=jax.ShapeDtypeStruct((b, TQ, h * d), F32),
        compiler_params=pltpu.CompilerParams(
            dimension_semantics=("parallel", "arbitrary"), vmem_limit_bytes=VMEM_LIMIT),
        name="nsa_sample_attention",
    )(page_table.reshape(-1).astype(I32), _head_rows(q),
      jnp.pad(gate_logits.astype(F32), ((0, 0), (0, TQ - t), (0, 0))),
      _new_cols(ks, page), _new_cols(vs, page), _new_cols(kw, LANE), _new_cols(vw, LANE),
      _cols_view(wins_k), _cols_view(wins_v), wck, wcv, pewk, pewv, w2k, w2v,
      *([_cols_view(pools_kc)] * n_in), *([_cols_view(pools_vc)] * n_in),
      *([_cols_view(pools_ks)] * n_in), *([_cols_view(pools_vs)] * n_in))
    return o[:, :t]


def last_rows(a, n):
    t = a.shape[1]
    if t >= n:
        return a[:, t - n:]
    return jnp.pad(a, ((0, 0), (n - t, 0)) + ((0, 0),) * (a.ndim - 2))


def mem_kv(mem, g, w_kv):
    b, m, _ = mem.shape
    k, v = jnp.split(mm(mem, w_kv, g), 2, axis=-1)
    return k.reshape(b, m, X_HEADS, X_HEAD_DIM), v.reshape(b, m, X_HEADS, X_HEAD_DIM)


def _xattn_kernel(x_ref, g_ref, wq_ref, mk_ref, mv_ref, wo_ref, o_ref, *, n_heads, hd):
    xn = _rms(x_ref[...], g_ref[...]).astype(BF16)
    q = jnp.dot(xn, wq_ref[...], preferred_element_type=F32).astype(BF16)
    mk = mk_ref[0].astype(BF16)
    mv = mv_ref[0].astype(BF16)
    outs = []
    for h in range(n_heads):
        hs = slice(h * hd, (h + 1) * hd)
        s = lax.dot_general(q[:, hs], mk[:, hs], (((1,), (1,)), ((), ())), preferred_element_type=F32) * (hd ** -0.5)
        e = jnp.exp(s - jnp.max(s, axis=-1, keepdims=True))
        p = e / jnp.sum(e, axis=-1, keepdims=True)
        outs.append(jnp.dot(p.astype(BF16), mv[:, hs], preferred_element_type=F32).astype(BF16))
    a = jnp.concatenate(outs, axis=1)
    for c0 in range(0, o_ref.shape[1], MM_COLS):
        cs = slice(c0, c0 + MM_COLS)
        o_ref[:, cs] = x_ref[:, cs] + jnp.dot(a, wo_ref[:, cs], preferred_element_type=F32)


def cross_attn_rows(x, g, mk, mv, w_q, w_o):
    b, t, d = x.shape
    mlen, n_heads, hd = mk.shape[1:]
    hw = n_heads * hd
    tm = _pick_tile(t, (512, 256, 128, 64, 32, 16, 8))
    steps = t // tm
    rows = pl.BlockSpec((tm, d), lambda i: (i, 0))
    mem = pl.BlockSpec((1, mlen, hw), lambda i: (i // steps, 0, 0))
    out = pl.pallas_call(
        functools.partial(_xattn_kernel, n_heads=n_heads, hd=hd),
        grid=(b * steps,),
        in_specs=[rows, _resident((1, d)), _resident((d, hw)), mem, mem, _resident((hw, d))],
        out_specs=rows,
        out_shape=jax.ShapeDtypeStruct((b * t, d), F32),
        compiler_params=pltpu.CompilerParams(dimension_semantics=("parallel",), vmem_limit_bytes=VMEM_LIMIT),
        name="cross_attn_rows",
    )(x.reshape(b * t, d), g.reshape(1, d).astype(F32), w_q.astype(BF16), mk.reshape(b, mlen, hw),
      mv.reshape(b, mlen, hw), w_o.astype(BF16))
    return out.reshape(b, t, d)


def cross_attn(x, g, mk, mv, w_q, w_o):
    b, t, _ = x.shape
    if t % SUBLANE == 0:
        return cross_attn_rows(x, g, mk, mv, w_q, w_o)
    q = mm(x, w_q, g).reshape(b, t, X_HEADS, X_HEAD_DIM)
    s = jnp.einsum('bthd,bmhd->bhtm', q, mk).astype(F32) * (X_HEAD_DIM ** -0.5)
    p = jax.nn.softmax(s, axis=-1).astype(x.dtype)
    return mm(jnp.einsum('bhtm,bmhd->bthd', p, mv).reshape(b, t, -1), w_o, res=x)


A_WIDTHS = (A_HEADS * HEAD_DIM, A_KV_HEADS * HEAD_DIM, A_KV_HEADS * HEAD_DIM, IDX_HEADS * IDX_DIM, IDX_DIM, IDX_HEADS)


def dsa_project(x, g, w_in, pos, keys_transposed):
    b, t, d = x.shape
    keys = [(F32, False), (BF16, keys_transposed)]
    segments = [(True, HEAD_DIM ** -0.5, [(BF16, False)]), (True, None, keys), (False, None, [(F32, False), (BF16, False)]),
                (True, None, [(BF16, False)]), (True, None, keys),
                (False, IDX_HEADS ** -0.5 * IDX_DIM ** -0.5, [(F32, False)])]
    return project(x.reshape(b * t, d), g, _split_cols(w_in, A_WIDTHS), segments, jnp.tile(pos, b))


def dsa_prompt(x, g, w_in, w_out):
    b, s, _ = x.shape
    q, k, kt, v, vb, iq, ik, ikt, iw = dsa_project(x, g, w_in, jnp.arange(s), True)
    o = dsa_prompt_attention(q.reshape(b, s, -1), kt, vb.reshape(b, s, -1), iq.reshape(b, s, -1), ikt,
                             iw.reshape(b, s, -1), min(TOPK_MAX, s // 4), A_KV_HEADS, IDX_HEADS)
    state = (k.reshape(b, s, A_KV_HEADS, HEAD_DIM), v.reshape(b, s, A_KV_HEADS, HEAD_DIM), ik.reshape(b, s, IDX_DIM))
    return mm(o, w_out, res=x), state


def dsa_sample(x, g, w_in, w_out, pools_k, pools_v, pools_idx, layer, page_table):
    b, t, _ = x.shape
    past = page_table.shape[1] * pools_k.shape[2]
    q, k, kb, v, vb, iq, ik, ikb, iw = dsa_project(x, g, w_in, past + jnp.arange(t), False)
    o = dsa_sample_attention(q.reshape(b, t, A_HEADS, HEAD_DIM), kb.reshape(b, t, A_KV_HEADS, HEAD_DIM),
                             vb.reshape(b, t, A_KV_HEADS, HEAD_DIM), iq.reshape(b, t, IDX_HEADS, IDX_DIM),
                             ikb.reshape(b, t, IDX_DIM), iw.reshape(b, t, IDX_HEADS), pools_k, pools_v, pools_idx,
                             layer, page_table, min(TOPK_MAX, (past + t) // 4))
    state = (k.reshape(b, t, A_KV_HEADS, HEAD_DIM), v.reshape(b, t, A_KV_HEADS, HEAD_DIM), ik.reshape(b, t, IDX_DIM))
    return mm(o, w_out, res=x), state


def _shift_rows(x, prev, k):
    return jnp.concatenate([prev[prev.shape[0] - k:], x[:x.shape[0] - k]], axis=0)


def _tile_transpose(m, rows_out, cols_out):
    r, c = m.shape
    tile = m
    if c < LANE:
        tile = jnp.concatenate([tile, jnp.zeros((r, LANE - c), m.dtype)], axis=1)
    if r < LANE:
        tile = jnp.concatenate([tile, jnp.zeros((LANE - r, LANE), m.dtype)], axis=0)
    return tile.T[:rows_out, :cols_out]


def _ssd_kernel(z_ref, x_ref, dt_ref, cw_ref, cb_ref, dtb_ref, alog_ref, dskip_ref, buf_ref, h0_ref,
                y_ref, hout_ref, carry_ref, ht_ref, *, n_heads, n_groups, hd, n_state):
    c = pl.program_id(1)
    l = x_ref.shape[0]
    d_inner = n_heads * hd
    rep = n_heads // n_groups
    n_pairs = n_heads // 2

    @pl.when(c == 0)
    def _():
        carry_ref[...] = buf_ref[0]
        for p in range(n_pairs):
            ht_ref[p] = jnp.concatenate([_tile_transpose(h0_ref[0, 2 * p], n_state, hd),
                                         _tile_transpose(h0_ref[0, 2 * p + 1], n_state, hd)], axis=1)

    x = x_ref[...]
    prev = carry_ref[...]
    conv = cb_ref[...] + x * cw_ref[B_CONV - 1:B_CONV, :]
    for k in range(1, B_CONV):
        conv = conv + _shift_rows(x, prev, k) * cw_ref[B_CONV - 1 - k:B_CONV - k, :]
    carry_ref[...] = x[l - SUBLANE:]
    act = jax.nn.silu(conv)
    gn = n_groups * n_state
    xs, bm, cm = act[:, :d_inner], act[:, d_inner:d_inner + gn], act[:, d_inner + gn:]

    dt = jax.nn.softplus(dt_ref[...] + dtb_ref[...])
    a = -jnp.exp(alog_ref[...])
    acs = dt * a
    k = 1
    while k < l:
        acs = acs + jnp.concatenate([jnp.zeros((k, n_heads), F32), acs[:l - k]], axis=0)
        k *= 2
    acs_t = jnp.concatenate([acs, jnp.zeros((l, LANE - n_heads), F32)], axis=1).T

    spread = jnp.where(lax.broadcasted_iota(I32, (n_heads, d_inner), 1) // hd
                       == lax.broadcasted_iota(I32, (n_heads, d_inner), 0), 1.0, 0.0).astype(BF16)

    def per_column(v):
        out = jnp.zeros((l, d_inner), F32)
        for _ in range(3):
            part = v.astype(BF16)
            out = out + jnp.dot(part, spread, preferred_element_type=F32)
            v = v - part.astype(F32)
        return out

    dt_cols = per_column(dt)
    acs_cols = per_column(acs)
    last_cols = acs_cols[l - 1:l, :]
    xr_all = xs * dt_cols
    xw_all = (xr_all * jnp.exp(last_cols - acs_cols)).astype(BF16)
    grow_cols = jnp.exp(acs_cols)
    decay_cols = jnp.exp(last_cols)
    y_skip = xs * dskip_ref[...]
    gate = jax.nn.silu(z_ref[...])
    causal = lax.broadcasted_iota(I32, (l, l), 0) >= lax.broadcasted_iota(I32, (l, l), 1)
    first_head = lax.broadcasted_iota(I32, (l, LANE), 1) < hd

    for g in range(n_groups):
        bm_g = bm[:, g * n_state:(g + 1) * n_state]
        cm_g = cm[:, g * n_state:(g + 1) * n_state].astype(BF16)
        cb = lax.dot_general(cm_g, bm_g.astype(BF16), (((1,), (1,)), ((), ())), preferred_element_type=F32)
        bm_t = bm_g.T.astype(BF16)
        for p in range(g * rep // 2, (g + 1) * rep // 2):
            ps = slice(p * LANE, (p + 1) * LANE)
            xr = xr_all[:, ps]
            lhs = []
            for h in (2 * p, 2 * p + 1):
                seg = acs[:, h:h + 1] - acs_t[h:h + 1, :]
                lhs.append(cb * jnp.where(causal, jnp.exp(jnp.where(causal, seg, 0.0)), 0.0))
            rhs = jnp.concatenate([jnp.where(first_head, xr, 0.0), jnp.where(first_head, 0.0, xr)], axis=0)
            y = jnp.dot(jnp.concatenate(lhs, axis=1).astype(BF16), rhs.astype(BF16), preferred_element_type=F32)
            ht = ht_ref[p]
            y = y + jnp.dot(cm_g, ht.astype(BF16), preferred_element_type=F32) * grow_cols[:, ps]
            ht_ref[p] = ht * decay_cols[:, ps] + jnp.dot(bm_t, xw_all[:, ps], preferred_element_type=F32)
            y_ref[:, ps] = (y + y_skip[:, ps]) * gate[:, ps]

    @pl.when(c == pl.num_programs(1) - 1)
    def _():
        for p in range(n_pairs):
            ht = ht_ref[p]
            hout_ref[0, 2 * p] = _tile_transpose(ht[:, :hd], hd, n_state)
            hout_ref[0, 2 * p + 1] = _tile_transpose(ht[:, hd:], hd, n_state)


def ssd_prompt(z, xbc, dt, conv_w, conv_b, dt_bias, a_log, d_skip, conv_buf, ssm0, n_groups):
    b, n_heads, hd, n_state = ssm0.shape
    m, w = xbc.shape
    s = m // b
    l = min(B_CHUNK, s)
    nc = s // l
    d_inner = n_heads * hd
    buf = jnp.pad(conv_buf.astype(F32), ((0, 0), (SUBLANE - (B_CONV - 1), 0), (0, 0)))
    dskip_full = jnp.repeat(d_skip.astype(F32), hd).reshape(1, d_inner)
    rows = lambda width: pl.BlockSpec((l, width), lambda bb, c: (bb * nc + c, 0))
    state = pl.BlockSpec((1, n_heads, hd, n_state), lambda bb, c: (bb, 0, 0, 0))
    kern = functools.partial(_ssd_kernel, n_heads=n_heads, n_groups=n_groups, hd=hd, n_state=n_state)
    return pl.pallas_call(
        kern,
        grid=(b, nc),
        in_specs=[rows(d_inner), rows(w), rows(n_heads), _resident((B_CONV, w)), _resident((1, w)),
                  _resident((1, n_heads)), _resident((1, n_heads)), _resident((1, d_inner)),
                  pl.BlockSpec((1, SUBLANE, w), lambda bb, c: (bb, 0, 0)), state],
        out_specs=[rows(d_inner), state],
        out_shape=[jax.ShapeDtypeStruct((m, d_inner), F32), jax.ShapeDtypeStruct(ssm0.shape, F32)],
        scratch_shapes=[pltpu.VMEM((SUBLANE, w), F32), pltpu.VMEM((n_heads // 2, n_state, 2 * hd), F32)],
        compiler_params=pltpu.CompilerParams(
            dimension_semantics=("parallel", "arbitrary"), vmem_limit_bytes=VMEM_LIMIT),
        name="ssd_prompt",
    )(z, xbc, dt, conv_w.astype(F32), conv_b.reshape(1, w).astype(F32), dt_bias.reshape(1, n_heads).astype(F32),
      a_log.reshape(1, n_heads).astype(F32), dskip_full, buf, ssm0.astype(F32))


def mamba_prompt(x, g, w_in, conv_w, conv_b, dt_bias, a_log, d_skip, g_norm, w_out, conv_buf, ssm0):
    b, s, d = x.shape
    plain = (False, None, [(F32, False)])
    z, xbc, dt = project(x.reshape(b * s, d), g, _split_cols(w_in, (B_D_INNER, B_CONV_DIM, B_HEADS)),
                         [plain, plain, plain], jnp.zeros((b * s,), I32))
    y, ssm = ssd_prompt(z, xbc, dt, conv_w, conv_b, dt_bias, a_log, d_skip, conv_buf, ssm0, B_GROUPS)
    xpad = jnp.concatenate([conv_buf.astype(F32), xbc.reshape(b, s, -1)[:, s - (B_CONV - 1):]], axis=1)
    new_buf = xpad[:, xpad.shape[1] - (B_CONV - 1):]
    return mm(y.reshape(b, s, -1), w_out, g_norm, res=x), new_buf, ssm.astype(ssm0.dtype)


def ssd_chunked(x, dt, a, bm, cm, h0):
    bsz, t, nh, p = x.shape
    g, n = bm.shape[2], bm.shape[3]
    r = nh // g
    l = min(B_CHUNK, t)
    nc = t // l
    xr = (x * dt[..., None]).reshape(bsz, nc, l, g, r, p)
    acs = jnp.cumsum((dt * a).reshape(bsz, nc, l, g, r), axis=2)
    br = bm.reshape(bsz, nc, l, g, n)
    cr = cm.reshape(bsz, nc, l, g, n)
    seg = acs[:, :, :, None] - acs[:, :, None, :]
    causal = jnp.tril(jnp.ones((l, l), bool))[:, :, None, None]
    decay = jnp.where(causal, jnp.exp(jnp.where(causal, seg, 0.0)), 0.0)
    cb = jnp.einsum('bcign,bcjgn->bcijg', cr, br)
    y_intra = jnp.einsum('bcijg,bcijgr,bcjgrp->bcigrp', cb, decay, xr)
    to_end = jnp.exp(acs[:, :, -1:] - acs)
    s_chunk = jnp.einsum('bclgn,bclgr,bclgrp->bcgrpn', br, to_end, xr)
    d_chunk = jnp.exp(acs[:, :, -1])

    def step(hc, inp):
        s_c, d_c = inp
        return hc * d_c[..., None, None] + s_c, hc

    h_fin, h_in = lax.scan(step, h0.reshape(bsz, g, r, p, n),
                           (jnp.moveaxis(s_chunk, 1, 0), jnp.moveaxis(d_chunk, 1, 0)))
    h_in = jnp.moveaxis(h_in, 0, 1)
    y_inter = jnp.einsum('bcign,bcigr,bcgrpn->bcigrp', cr, jnp.exp(acs), h_in)
    return (y_intra + y_inter).reshape(bsz, t, nh, p), h_fin.reshape(bsz, nh, p, n)


def mamba_mixer(x, g, w_in, conv_w, conv_b, dt_bias, a_log, d_skip, g_norm, w_out, conv_buf, ssm0):
    b, t, _ = x.shape
    proj = mm(x, w_in, g)
    z = proj[..., :B_D_INNER]
    xbc = proj[..., B_D_INNER:B_D_INNER + B_CONV_DIM]
    dt = proj[..., B_D_INNER + B_CONV_DIM:]
    xpad = jnp.concatenate([conv_buf.astype(xbc.dtype), xbc], axis=1)
    conv = conv_b + sum(xpad[:, j:j + t] * conv_w[j] for j in range(B_CONV))
    xbc = jax.nn.silu(conv)
    new_buf = xpad[:, t:]
    gn = B_GROUPS * B_STATE
    xs = xbc[..., :B_D_INNER].reshape(b, t, B_HEADS, B_HEADDIM).astype(F32)
    bm = xbc[..., B_D_INNER:B_D_INNER + gn].reshape(b, t, B_GROUPS, B_STATE).astype(F32)
    cm = xbc[..., B_D_INNER + gn:].reshape(b, t, B_GROUPS, B_STATE).astype(F32)
    dt = jax.nn.softplus(dt.astype(F32) + dt_bias.astype(F32))
    a = -jnp.exp(a_log.astype(F32))
    y, ssm = ssd_chunked(xs, dt, a, bm, cm, ssm0.astype(F32))
    y = (y + xs * d_skip.astype(F32)[:, None]).reshape(b, t, B_D_INNER).astype(x.dtype)
    y = y * jax.nn.silu(z)
    return mm(y, w_out, g_norm, res=x), new_buf, ssm.astype(ssm0.dtype)


C_KV_WIDTH = C_KV_HEADS * HEAD_DIM
C_WIDTHS = (C_HEADS * HEAD_DIM,) + (C_KV_WIDTH,) * 6 + (3 * C_HEADS,)


def nsa_project(x, g, w_in, pos, keys_transposed):
    b, t, d = x.shape
    keys = (True, None, [(F32, False), (BF16, keys_transposed)])
    vals = (False, None, [(F32, False), (BF16, False)])
    segments = [(True, HEAD_DIM ** -0.5, [(BF16, False)]), (True, None, [(F32, False)]), (False, None, [(F32, False)]),
                keys, vals, keys, vals, (False, None, [(F32, False)])]
    return project(x.reshape(b * t, d), g, _split_cols(w_in, C_WIDTHS), segments, jnp.tile(pos, b))


def nsa_prompt(x, g, w_in, w_out, cmp_k, cmp_v, w_buf):
    b, s, _ = x.shape
    q, kc, vc, ks, kst, vs, vsb, kw, kwt, vw, vwb, gates = nsa_project(x, g, w_in, jnp.arange(s), True)
    seq = lambda a: a.reshape(b, s, -1)
    kct = compress_rows(seq(kc), cmp_k, C_KV_HEADS, True)
    vcc = compress_rows(seq(vc), cmp_v, C_KV_HEADS, False)
    n_cmp = s // CMP_STRIDE - CMP_BLOCK // CMP_STRIDE + 1
    o = nsa_prompt_attention(seq(q), seq(gates), kct, vcc, n_cmp, kst, seq(vsb), kwt, seq(vwb), C_KV_HEADS)
    heads = lambda a: a.reshape(b, s, C_KV_HEADS, HEAD_DIM)
    state = (heads(kc), heads(vc), heads(ks), heads(vs), last_rows(heads(kw), w_buf), last_rows(heads(vw), w_buf))
    return mm(o, w_out, res=x), state


def nsa_sample(x, g, w_in, w_out, cmp_k, cmp_v, pools_kc, pools_vc, pools_ks, pools_vs, wins_k, wins_v, layer,
               page_table):
    b, t, _ = x.shape
    past = page_table.shape[1] * pools_kc.shape[2]
    q, kc, vc, ks, ksb, vs, vsb, kw, kwb, vw, vwb, gates = nsa_project(x, g, w_in, past + jnp.arange(t), False)
    heads = lambda a: a.reshape(b, t, C_KV_HEADS, HEAD_DIM)
    o = nsa_sample_attention(q.reshape(b, t, C_HEADS, HEAD_DIM), gates.reshape(b, t, -1), heads(ksb), heads(vsb),
                             heads(kwb), heads(vwb), cmp_k, cmp_v, pools_kc, pools_vc, pools_ks, pools_vs, wins_k,
                             wins_v, layer, page_table)
    win_k, win_v = wins_k[layer], wins_v[layer]
    w_buf = win_k.shape[1]
    kwin = jnp.concatenate([win_k, heads(kw)], axis=1)
    vwin = jnp.concatenate([win_v, heads(vw)], axis=1)
    state = (heads(kc), heads(vc), heads(ks), heads(vs), last_rows(kwin, w_buf), last_rows(vwin, w_buf))
    return mm(o, w_out, res=x), state


def kernel(x_prompt, x_sample, cache_a_k, cache_a_v, cache_a_idx, state_b_ssm, state_b_conv, cache_c_cmp_k, cache_c_cmp_v, cache_c_slc_k, cache_c_slc_v, cache_c_win_k, cache_c_win_v, cache_mem_k, cache_mem_v, page_table, mem_prompt, g_ffn1, ffn1_wi, ffn1_wo, g_mix, g_xattn, g_mem, x_w_q, x_w_kv, x_w_o, g_ffn2, ffn2_wi, ffn2_wo, g_final, a_w_in, a_w_out, b_w_in, b_conv_w, b_conv_b, b_dt_bias, b_a_log, b_d_skip, b_g_norm, b_w_out, c_w_in, c_w_out, c_pe_k, c_w1_k, c_w2_k, c_pe_v, c_w1_v, c_w2_v):
    xp, xs = x_prompt, x_sample
    w_buf = cache_c_win_k.shape[2]
    ak_p, av_p, ai_p, bs_p, bc_p = [], [], [], [], []
    cck_p, ccv_p, csk_p, csv_p, cwk_p, cwv_p = [], [], [], [], [], []
    mk_p, mv_p = [], []
    ak_s, av_s, ai_s, bs_s, bc_s = [], [], [], [], []
    cck_s, ccv_s, csk_s, csv_s, cwk_s, cwv_s = [], [], [], [], [], []
    ia = ib = ic = 0
    for i in range(DEPTH):
        xp = ffn_residual(xp, g_ffn1[i], ffn1_wi[i], ffn1_wo[i])
        xs = ffn_residual(xs, g_ffn1[i], ffn1_wi[i], ffn1_wo[i])
        kind = i % N_MIXERS
        if kind == 0:
            op, (k1, v1, i1) = dsa_prompt(xp, g_mix[i], a_w_in[ia], a_w_out[ia])
            os_, (k2, v2, i2) = dsa_sample(xs, g_mix[i], a_w_in[ia], a_w_out[ia], cache_a_k, cache_a_v, cache_a_idx,
                                           ia, page_table)
            ak_p.append(k1); av_p.append(v1); ai_p.append(i1)
            ak_s.append(k2); av_s.append(v2); ai_s.append(i2)
            ia += 1
        elif kind == 1:
            conv0 = jnp.zeros((xp.shape[0], B_CONV - 1, B_CONV_DIM), xp.dtype)
            ssm0 = jnp.zeros((xp.shape[0], B_HEADS, B_HEADDIM, B_STATE), F32)
            wts = (b_w_in[ib], b_conv_w[ib], b_conv_b[ib], b_dt_bias[ib], b_a_log[ib], b_d_skip[ib],
                   b_g_norm[ib], b_w_out[ib])
            op, c1, s1 = mamba_prompt(xp, g_mix[i], *wts, conv0, ssm0)
            os_, c2, s2 = mamba_mixer(xs, g_mix[i], *wts, state_b_conv[ib], state_b_ssm[ib])
            bc_p.append(c1); bs_p.append(s1); bc_s.append(c2); bs_s.append(s2)
            ib += 1
        else:
            cmp_k = (c_pe_k[ic], c_w1_k[ic], c_w2_k[ic])
            cmp_v = (c_pe_v[ic], c_w1_v[ic], c_w2_v[ic])
            op, st1 = nsa_prompt(xp, g_mix[i], c_w_in[ic], c_w_out[ic], cmp_k, cmp_v, w_buf)
            os_, st2 = nsa_sample(xs, g_mix[i], c_w_in[ic], c_w_out[ic], cmp_k, cmp_v, cache_c_cmp_k, cache_c_cmp_v,
                                  cache_c_slc_k, cache_c_slc_v, cache_c_win_k, cache_c_win_v, ic, page_table)
            for lst, arr in zip((cck_p, ccv_p, csk_p, csv_p, cwk_p, cwv_p), st1):
                lst.append(arr)
            for lst, arr in zip((cck_s, ccv_s, csk_s, csv_s, cwk_s, cwv_s), st2):
                lst.append(arr)
            ic += 1
        xp, xs = op, os_
        mk, mv = mem_kv(mem_prompt, g_mem[i], x_w_kv[i])
        mk_p.append(mk); mv_p.append(mv)
        xp = cross_attn(xp, g_xattn[i], mk, mv, x_w_q[i], x_w_o[i])
        xs = cross_attn(xs, g_xattn[i], cache_mem_k[i], cache_mem_v[i], x_w_q[i], x_w_o[i])
        xp = ffn_residual(xp, g_ffn2[i], ffn2_wi[i], ffn2_wo[i])
        xs = ffn_residual(xs, g_ffn2[i], ffn2_wi[i], ffn2_wo[i])
    y_prompt = rms_norm_rows(xp, g_final)
    y_sample = rms_norm_rows(xs, g_final)
    st = jnp.stack
    return (y_prompt, y_sample,
            st(ak_p), st(av_p), st(ai_p), st(bs_p), st(bc_p),
            st(cck_p), st(ccv_p), st(csk_p), st(csv_p), st(cwk_p), st(cwv_p),
            st(mk_p), st(mv_p),
            st(ak_s), st(av_s), st(ai_s), st(bs_s), st(bc_s),
            st(cck_s), st(ccv_s), st(csk_s), st(csv_s), st(cwk_s), st(cwv_s))
```

```python
import functools

import jax
import jax.numpy as jnp
import numpy as np
from jax import lax
from jax.experimental import pallas as pl
from jax.experimental.pallas import tpu as pltpu

F32 = jnp.float32
BF16 = jnp.bfloat16
I32 = jnp.int32
INT_MIN = -2 ** 31
EPS = 1e-6

D_MODEL = 1024
DEPTH = 4
N_MIXERS = 3
HEAD_DIM = 64
ROPE_THETA = 10000.0
Q_BLOCK = 128
A_HEADS = D_MODEL // HEAD_DIM
A_KV_HEADS = 4
IDX_HEADS = 8
IDX_DIM = 64
TOPK_MAX = 256
B_D_INNER = 2 * D_MODEL
B_HEADDIM = 64
B_HEADS = B_D_INNER // B_HEADDIM
B_GROUPS = 4
B_STATE = 128
B_CONV = 4
B_CONV_DIM = B_D_INNER + 2 * B_GROUPS * B_STATE
B_CHUNK = 128
C_HEADS = D_MODEL // HEAD_DIM
C_KV_HEADS = 2
CMP_BLOCK = 32
CMP_STRIDE = 16
SLC_BLOCK = 64
N_SLC = 16
WINDOW = 512
X_HEADS = 4
X_HEAD_DIM = 128
D_FF = 2816

LANE = 128
SUBLANE = 8
VMEM_LIMIT = 48 * 1024 * 1024
MM_COLS = 512
FF_CHUNK = 256
ATTN_COLS = 1024
TQ = SUBLANE
PAGES_PER_STEP = 16


def _pick_tile(n, candidates):
    for c in candidates:
        if n % c == 0:
            return c
    return n


def _resident(shape):
    return pl.BlockSpec(shape, lambda *_: (0,) * len(shape), pipeline_mode=pl.Buffered(1))


def _rms(x, g):
    return x * lax.rsqrt(jnp.mean(x * x, axis=-1, keepdims=True) + EPS) * g


def _nmm_kernel(x_ref, g_ref, w_ref, *rest, norm, residual):
    o_ref = rest[-1]
    x = x_ref[...]
    if norm:
        x = _rms(x, g_ref[...])
    xb = x.astype(BF16)
    n = w_ref.shape[1]
    for c0 in range(0, n, MM_COLS):
        c1 = min(c0 + MM_COLS, n)
        y = jnp.dot(xb, w_ref[:, c0:c1], preferred_element_type=F32)
        o_ref[:, c0:c1] = rest[0][:, c0:c1] + y if residual else y


def norm_matmul(x, w, g=None, res=None):
    m, k = x.shape
    n = w.shape[1]
    assert res is None or (n % LANE == 0 and m % SUBLANE == 0)
    n_pad = -(-n // LANE) * LANE
    wb = w.astype(BF16)
    if n_pad != n:
        wb = jnp.pad(wb, ((0, 0), (0, n_pad - n)))
    m_orig = m
    if m % SUBLANE != 0:
        m = -(-m // 256) * 256
        x = jnp.pad(x, ((0, m - m_orig), (0, 0)))
    tm = _pick_tile(m, (512, 256, 128, 64, 32, 16, 8))
    norm = g is not None
    gg = (g if norm else jnp.ones((k,), F32)).reshape(1, k).astype(F32)
    row_block = lambda width: pl.BlockSpec((tm, width), lambda i: (i, 0))
    extra = [] if res is None else [res]
    out = pl.pallas_call(
        functools.partial(_nmm_kernel, norm=norm, residual=res is not None),
        grid=(m // tm,),
        in_specs=[row_block(k), _resident((1, k)), _resident((k, n_pad))] + [row_block(n_pad) for _ in extra],
        out_specs=row_block(n_pad),
        out_shape=jax.ShapeDtypeStruct((m, n_pad), F32),
        compiler_params=pltpu.CompilerParams(dimension_semantics=("parallel",), vmem_limit_bytes=VMEM_LIMIT),
        name="norm_matmul",
    )(x, gg, wb, *extra)
    if n_pad != n or m != m_orig:
        out = out[:m_orig, :n]
    return out


def mm(x, w, g=None, res=None):
    lead = x.shape[:-1]
    r2 = None if res is None else res.reshape(-1, res.shape[-1])
    return norm_matmul(x.reshape(-1, x.shape[-1]), w, g, r2).reshape(lead + (w.shape[1],))


def _ffn_kernel(x_ref, g_ref, wg_ref, wu_ref, wo_ref, o_ref, act_ref, *, n_chunks):
    xn = _rms(x_ref[...], g_ref[...]).astype(BF16)

    for c in range(n_chunks):
        cs = slice(c * FF_CHUNK, (c + 1) * FF_CHUNK)
        hg = jnp.dot(xn, wg_ref[:, cs], preferred_element_type=F32)
        hu = jnp.dot(xn, wu_ref[:, cs], preferred_element_type=F32)
        act_ref[:, cs] = (jax.nn.silu(hg) * hu).astype(BF16)
    for c0 in range(0, o_ref.shape[1], MM_COLS):
        cs = slice(c0, c0 + MM_COLS)
        o_ref[:, cs] = x_ref[:, cs] + 0.5 * jnp.dot(act_ref[...], wo_ref[:, cs], preferred_element_type=F32)


def ffn_residual(x, g, wi, wo):
    lead, d = x.shape[:-1], x.shape[-1]
    x2 = x.reshape(-1, d)
    m = x2.shape[0]
    f = wo.shape[0]
    assert f % FF_CHUNK == 0 and d % MM_COLS == 0
    tm = _pick_tile(m, (512, 256, 128, 64, 32, 16, 8))
    wb = wi.astype(BF16)
    out = pl.pallas_call(
        functools.partial(_ffn_kernel, n_chunks=f // FF_CHUNK),
        grid=(m // tm,),
        in_specs=[pl.BlockSpec((tm, d), lambda i: (i, 0)), _resident((1, d)), _resident((d, f)), _resident((d, f)),
                  _resident((f, d))],
        out_specs=pl.BlockSpec((tm, d), lambda i: (i, 0)),
        out_shape=jax.ShapeDtypeStruct((m, d), F32),
        scratch_shapes=[pltpu.VMEM((tm, f), BF16)],
        compiler_params=pltpu.CompilerParams(dimension_semantics=("parallel",), vmem_limit_bytes=VMEM_LIMIT),
        name="ffn_residual",
    )(x2, g.reshape(1, d).astype(F32), wb[:, :f], wb[:, f:], wo.astype(BF16))
    return out.reshape(lead + (d,))


def _rmsnorm_kernel(x_ref, g_ref, o_ref):
    o_ref[...] = _rms(x_ref[...], g_ref[...])


def rms_norm_rows(x, g):
    lead, d = x.shape[:-1], x.shape[-1]
    x2 = x.reshape(-1, d)
    m = x2.shape[0]
    tm = _pick_tile(m, (512, 256, 128, 64, 32, 16, 8))
    row_block = pl.BlockSpec((tm, d), lambda i: (i, 0))
    out = pl.pallas_call(
        _rmsnorm_kernel, grid=(m // tm,), in_specs=[row_block, _resident((1, d))], out_specs=row_block,
        out_shape=jax.ShapeDtypeStruct((m, d), F32),
        compiler_params=pltpu.CompilerParams(dimension_semantics=("parallel",)), name="rms_norm_rows",
    )(x2, g.reshape(1, d).astype(F32))
    return out.reshape(lead + (d,))


def rope_tables(pos):
    half = HEAD_DIM // 2
    inv = ROPE_THETA ** (-jnp.arange(half, dtype=F32) / half)
    ang = pos.astype(F32)[:, None] * inv[None, :]
    cos, sin = jnp.cos(ang), jnp.sin(ang)
    reps = LANE // HEAD_DIM
    return (jnp.tile(jnp.concatenate([cos, cos], axis=1), (1, reps)),
            jnp.tile(jnp.concatenate([-sin, sin], axis=1), (1, reps)))


def _rope_cols(y, cos, sin):
    w = y.shape[1]
    half = HEAD_DIM // 2
    lane = lax.broadcasted_iota(I32, y.shape, 1)
    partner = jnp.where(lane % HEAD_DIM < half, pltpu.roll(y, w - half, axis=1), pltpu.roll(y, half, axis=1))
    reps = w // LANE
    return y * jnp.tile(cos, (1, reps)) + partner * jnp.tile(sin, (1, reps))


def _proj_kernel(x_ref, g_ref, cos_ref, sin_ref, w_ref, *out_refs, segments):
    xb = _rms(x_ref[...], g_ref[...]).astype(BF16)
    cos, sin = cos_ref[...], sin_ref[...]
    k = 0
    for start, width, rope, scale, outs in segments:
        y = jnp.dot(xb, w_ref[:, start:start + width], preferred_element_type=F32)
        if rope:
            y = _rope_cols(y, cos, sin)
        if scale is not None:
            y = y * scale
        for used, dtype, transposed in outs:
            v = y[:, :used]
            out_refs[k][...] = (v.T if transposed else v).astype(dtype)
            k += 1


def project(x, g, w_cols, segments, pos_rows):
    m, kdim = x.shape
    packed, segs, out_shapes, out_specs = [], [], [], []
    tm = _pick_tile(m, (512, 256, 128, 64, 32, 16, 8))
    start = 0
    for wc, (rope, scale, outs) in zip(w_cols, segments):
        n = wc.shape[1]
        n_pad = -(-n // LANE) * LANE
        packed.append(jnp.pad(wc.astype(BF16), ((0, 0), (0, n_pad - n))))
        segs.append((start, n_pad, rope, scale, tuple((n, dt, tr) for dt, tr in outs)))
        for dt, tr in outs:
            if tr:
                out_shapes.append(jax.ShapeDtypeStruct((n, m), dt))
                out_specs.append(pl.BlockSpec((n, tm), lambda i: (0, i)))
            else:
                out_shapes.append(jax.ShapeDtypeStruct((m, n), dt))
                out_specs.append(pl.BlockSpec((tm, n), lambda i: (i, 0)))
        start += n_pad
    wb = jnp.concatenate(packed, axis=1)
    cos, sin = rope_tables(pos_rows)
    row_block = lambda width: pl.BlockSpec((tm, width), lambda i: (i, 0))
    return pl.pallas_call(
        functools.partial(_proj_kernel, segments=tuple(segs)),
        grid=(m // tm,),
        in_specs=[row_block(kdim), _resident((1, kdim)), row_block(LANE), row_block(LANE), _resident(wb.shape)],
        out_specs=out_specs,
        out_shape=out_shapes,
        compiler_params=pltpu.CompilerParams(dimension_semantics=("parallel",), vmem_limit_bytes=VMEM_LIMIT),
        name="project",
    )(x, g.reshape(1, kdim).astype(F32), cos, sin, wb)


def _split_cols(w, widths):
    cols = np.cumsum((0,) + tuple(widths))
    return [w[:, cols[j]:cols[j + 1]] for j in range(len(widths))]


def _softmax_rows(s):
    m = jnp.max(s, axis=-1, keepdims=True)
    e = jnp.exp(s - jnp.where(m == -jnp.inf, 0.0, m))
    den = jnp.sum(e, axis=-1, keepdims=True)
    return e / jnp.where(den > 0, den, 1.0)


def _flash_step(carry, q, kt_tile, v_tile, sel, rep):
    m, l, acc = carry
    rows, tk = q.shape[0], kt_tile.shape[1]
    s = jnp.dot(q, kt_tile, preferred_element_type=F32)
    s = jnp.where(sel, s.reshape(rep, Q_BLOCK, tk), -jnp.inf).reshape(rows, tk)
    m_new = jnp.maximum(m, jnp.max(s, axis=1, keepdims=True))
    m_safe = jnp.where(m_new == -jnp.inf, 0.0, m_new)
    alpha = jnp.exp(m - m_safe)
    p = jnp.exp(s - m_safe)
    l = l * alpha + jnp.sum(p, axis=1, keepdims=True)
    pv = jnp.dot(p.astype(BF16), v_tile, preferred_element_type=F32)
    return m_new, l, acc * alpha + pv


def _flash_init(rows, hd):
    return (jnp.full((rows, 1), -jnp.inf, F32), jnp.zeros((rows, 1), F32), jnp.zeros((rows, hd), F32))


def _flash_finish(carry):
    _, l, acc = carry
    return acc / jnp.where(l > 0, l, 1.0)


def _stack_heads(x, h0, rep, hd):
    return jnp.concatenate([x[:, (h0 + r) * hd:(h0 + r + 1) * hd] for r in range(rep)], axis=0)


def _sortable_key(score):
    bits = pltpu.bitcast(score, I32)
    key = jnp.where(bits < 0, bits ^ jnp.int32(0x7FFFFFFF), bits)
    return jnp.where(score == 0.0, 0, key)


def _tile_plan(limit, tk):
    half = tk // 2
    n_half = (limit + half - 1) // half
    return n_half // 2, n_half % 2 == 1


def _over_tiles(plan, tk, step, carry):
    n_wide, has_half = plan
    half = tk // 2
    carry = lax.fori_loop(0, n_wide, lambda kt, c: step(c, pl.multiple_of(kt * tk, tk), tk), carry)
    return lax.cond(has_half, lambda c: step(c, pl.multiple_of(n_wide * tk, half), half), lambda c: c, carry)


def _count_lanes(pred_fn, key_ref, plan, tk):
    rows = key_ref.shape[0]

    def step(cnt, off, width):
        keys = key_ref[:, pl.ds(off, width)]
        for c in range(width // LANE):
            hit = pred_fn(keys[:, c * LANE:(c + 1) * LANE], off + c * LANE)
            cnt = cnt + jnp.where(hit, 1, 0)
        return cnt

    cnt = _over_tiles(plan, tk, step, jnp.zeros((rows, LANE), I32))
    return jnp.sum(cnt, axis=1, keepdims=True)


def _top_k_mask_params(key_ref, plan, tk, topk, n_valid):
    rows = key_ref.shape[0]
    lane_iota = lax.broadcasted_iota(I32, (rows, LANE), 1)

    def bit_body(it, thr):
        cand = thr + jnp.left_shift(jnp.int32(1), 31 - it)
        cnt = _count_lanes(lambda keys, c0: keys >= cand, key_ref, plan, tk)
        return jnp.where(cnt >= topk, cand, thr)

    thr = lax.fori_loop(0, 32, bit_body, jnp.full((rows, 1), INT_MIN, I32))
    many = n_valid > topk
    thr = jnp.where(many, thr, INT_MIN + 1)
    c_gt = _count_lanes(lambda keys, c0: keys > thr, key_ref, plan, tk)
    c_eq = _count_lanes(lambda keys, c0: keys == thr, key_ref, plan, tk)
    need = topk - c_gt
    excess = many & (c_eq > need)
    n_col_bits = max(1, (key_ref.shape[1] - 1).bit_length())

    def tie_search():
        def jbody(it, last):
            cand = last + jnp.left_shift(jnp.int32(1), n_col_bits - 1 - it)
            cnt = _count_lanes(lambda keys, c0: (keys == thr) & (c0 + lane_iota < cand), key_ref, plan, tk)
            return jnp.where(cnt <= need - 1, cand, last)

        return lax.fori_loop(0, n_col_bits, jbody, jnp.zeros((rows, 1), I32))

    any_excess = jnp.max(jnp.where(excess, 1, 0)) > 0
    last_tie = lax.cond(any_excess, tie_search, lambda: jnp.zeros((rows, 1), I32))
    return thr, jnp.where(excess, last_tie, jnp.int32(2 ** 30))


SEARCH_DIGIT_BITS = 4


def _top_k_mask_params_small(keys, col, topk, n_valid):
    rows = keys.shape[0]
    count = lambda hit: jnp.sum(jnp.where(hit, 1, 0), axis=1, keepdims=True)
    thr = jnp.full((rows, 1), INT_MIN, I32)
    for shift in range(32 - SEARCH_DIGIT_BITS, -1, -SEARCH_DIGIT_BITS):
        digit = jnp.zeros((rows, 1), I32)
        for d in range(1, 2 ** SEARCH_DIGIT_BITS):
            step = np.int64(d << shift).astype(np.int32)
            digit = digit + jnp.where(count(keys >= thr + jnp.int32(step)) >= topk, 1, 0)
        thr = thr + jnp.left_shift(digit, shift)
    many = n_valid > topk
    thr = jnp.where(many, thr, INT_MIN + 1)
    need = topk - count(keys > thr)
    tie = keys == thr
    excess = many & (count(tie) > need)
    n_col_bits = max(1, (keys.shape[1] - 1).bit_length())

    def tie_search():
        last = jnp.zeros((rows, 1), I32)
        for it in range(n_col_bits):
            cand = last + (1 << (n_col_bits - 1 - it))
            last = jnp.where(count(tie & (col < cand)) <= need - 1, cand, last)
        return last

    any_excess = jnp.max(jnp.where(excess, 1, 0)) > 0
    last_tie = lax.cond(any_excess, tie_search, lambda: jnp.zeros((rows, 1), I32))
    return thr, jnp.where(excess, last_tie, jnp.int32(2 ** 30))


def _block_cover(n_cmp_rows):
    c0 = lax.broadcasted_iota(I32, (n_cmp_rows, LANE), 0) * CMP_STRIDE
    s0 = lax.broadcasted_iota(I32, (n_cmp_rows, LANE), 1) * SLC_BLOCK
    return jnp.where(c0 < s0 + SLC_BLOCK, jnp.where(c0 + CMP_BLOCK > s0, 1.0, 0.0), 0.0).astype(BF16)


def _block_importance(p_sum, cover, row_pos):
    p_hi = p_sum.astype(BF16)
    p_lo = (p_sum - p_hi.astype(F32)).astype(BF16)
    imp = jnp.dot(p_hi, cover, preferred_element_type=F32) + jnp.dot(p_lo, cover, preferred_element_type=F32)
    jb = lax.broadcasted_iota(I32, imp.shape, 1)
    cur = row_pos // SLC_BLOCK
    forced = (jb == 0) | (jb == cur) | (jb == cur - 1)
    imp = jnp.where(forced, jnp.inf, imp)
    return jnp.where(jb <= cur, imp, -jnp.inf)


def _dsa_prompt_kernel(iq_ref, iw_ref, ikt_ref, q_ref, kt_ref, v_ref, o_ref, key_ref, *, topk, tk,
                       n_idx_heads, n_groups, rep):
    i = pl.program_id(1)
    t0 = i * Q_BLOCK
    plan = _tile_plan(t0 + Q_BLOCK, tk)
    row = t0 + lax.broadcasted_iota(I32, (Q_BLOCK, 1), 0)
    hd = kt_ref.shape[0] // n_groups
    di = ikt_ref.shape[0]

    iw = iw_ref[0]
    iq = iq_ref[0]
    iq_heads = [iq[:, h * di:(h + 1) * di] for h in range(n_idx_heads)]

    def score_step(_, off, width):
        acc = jnp.zeros((Q_BLOCK, width), F32)
        for h in range(n_idx_heads):
            r = jnp.dot(iq_heads[h], ikt_ref[:, pl.ds(off, width)], preferred_element_type=F32)
            acc = acc + jnp.maximum(r, 0.0) * iw[:, h:h + 1]
        col = off + lax.broadcasted_iota(I32, (Q_BLOCK, width), 1)
        key_ref[:, pl.ds(off, width)] = jnp.where(col <= row, _sortable_key(acc), INT_MIN)
        return 0

    _over_tiles(plan, tk, score_step, 0)

    thr, last_tie = _top_k_mask_params(key_ref, plan, tk, topk, row + 1)

    q = q_ref[0]
    q_groups = [_stack_heads(q, g * rep, rep, hd) for g in range(n_groups)]
    rows_g = rep * Q_BLOCK

    def attn_step(carry, off, width):
        keys = key_ref[:, pl.ds(off, width)]
        col = off + lax.broadcasted_iota(I32, (Q_BLOCK, width), 1)
        sel = ((keys > thr) | ((keys == thr) & (col <= last_tie)))[None]
        v_t = v_ref[0, pl.ds(off, width), :]
        return tuple(
            _flash_step(carry[g], q_groups[g], kt_ref[g * hd:(g + 1) * hd, pl.ds(off, width)],
                        v_t[:, g * hd:(g + 1) * hd], sel, rep)
            for g in range(n_groups))

    fin = _over_tiles(plan, tk, attn_step, tuple(_flash_init(rows_g, hd) for _ in range(n_groups)))
    for g in range(n_groups):
        o = _flash_finish(fin[g])
        for r in range(rep):
            h = g * rep + r
            o_ref[0, :, h * hd:(h + 1) * hd] = o[r * Q_BLOCK:(r + 1) * Q_BLOCK]


def dsa_prompt_attention(q, kt, v, iq, ikt, iw, topk, n_groups, n_idx_heads, tk=ATTN_COLS):
    b, s, hd_all = q.shape
    gd, di = kt.shape[0], ikt.shape[0]
    d = gd // n_groups
    tk = min(tk, s)
    kern = functools.partial(_dsa_prompt_kernel, topk=topk, tk=tk, n_idx_heads=n_idx_heads, n_groups=n_groups,
                             rep=hd_all // gd)
    q_block = lambda width: pl.BlockSpec((1, Q_BLOCK, width), lambda bb, i: (bb, i, 0))
    return pl.pallas_call(
        kern,
        grid=(b, s // Q_BLOCK),
        in_specs=[
            q_block(n_idx_heads * di), q_block(n_idx_heads),
            pl.BlockSpec((di, s), lambda bb, i: (0, bb)),
            q_block(hd_all),
            pl.BlockSpec((gd, s), lambda bb, i: (0, bb)),
            pl.BlockSpec((1, s, gd), lambda bb, i: (bb, 0, 0)),
        ],
        out_specs=q_block(hd_all),
        out_shape=jax.ShapeDtypeStruct((b, s, hd_all), F32),
        scratch_shapes=[pltpu.VMEM((Q_BLOCK, s), I32)],
        compiler_params=pltpu.CompilerParams(
            dimension_semantics=("parallel", "arbitrary"), vmem_limit_bytes=VMEM_LIMIT),
        name="dsa_prompt_attention",
    )(iq, iw, ikt, q, kt, v)


def _top_blocks(imp_ref, n_live, n_sel):
    imp_t = imp_ref[...]
    j_iota = lax.broadcasted_iota(I32, imp_t.shape, 0)

    def body(k, rank):
        row_k = imp_ref[pl.ds(k, 1), :]
        earlier = jnp.where(j_iota > k, 1.0, 0.0)
        return rank + jnp.where(row_k > imp_t, 1.0, jnp.where(row_k == imp_t, earlier, 0.0))

    rank = lax.fori_loop(0, n_live, body, jnp.zeros(imp_t.shape, F32))
    return jnp.where(rank < n_sel, 1.0, 0.0)


def _nsa_prompt_kernel(q_ref, gate_ref, kct_ref, vc_ref, kst_ref, vs_ref, kwt_ref, vw_ref, o_ref, imp_ref, *,
                       tk, n_cmp, n_blk, n_groups, rep, win_len):
    i = pl.program_id(1)
    t0 = i * Q_BLOCK
    plan = _tile_plan(t0 + Q_BLOCK, tk)
    hd = kst_ref.shape[0] // n_groups
    rows = rep * Q_BLOCK
    nc_pad = kct_ref.shape[-1]
    row = t0 + lax.broadcasted_iota(I32, (Q_BLOCK, 1), 0)
    gates = jax.nn.sigmoid(gate_ref[0])
    n_heads = n_groups * rep
    q_all = q_ref[0]

    cover = _block_cover(nc_pad)
    c_idx = lax.broadcasted_iota(I32, (Q_BLOCK, nc_pad), 1)
    c_ok = ((c_idx < n_cmp) & (c_idx * CMP_STRIDE + (CMP_BLOCK - 1) <= row))[None]
    w_start = pl.multiple_of(jnp.maximum(t0 - WINDOW, 0), Q_BLOCK)
    w_col = w_start + lax.broadcasted_iota(I32, (Q_BLOCK, win_len), 1)
    w_d = row - w_col
    w_ok = ((w_d >= 0) & (w_d < WINDOW))[None]

    for g in range(n_groups):
        q = _stack_heads(q_all, g * rep, rep, hd)
        lo, hi = g * hd, (g + 1) * hd

        s = jnp.dot(q, kct_ref[0, lo:hi, :], preferred_element_type=F32)
        s = jnp.where(c_ok, s.reshape(rep, Q_BLOCK, nc_pad), -jnp.inf)
        p = _softmax_rows(s)
        o_cmp = jnp.dot(p.reshape(rows, nc_pad).astype(BF16), vc_ref[0, :, lo:hi], preferred_element_type=F32)

        imp = _block_importance(jnp.sum(p, axis=0), cover, row)
        imp_ref[...] = imp.T
        n_live = jnp.minimum((t0 + Q_BLOCK - 1) // SLC_BLOCK + 1, n_blk)
        sel_blocks = _top_blocks(imp_ref, n_live, min(N_SLC, n_blk)).T.astype(BF16)

        def slc_step(carry, off, width):
            col = off + lax.broadcasted_iota(I32, (Q_BLOCK, width), 1)
            blk_of_col = (off + lax.broadcasted_iota(I32, (LANE, width), 1)) // SLC_BLOCK
            expand = jnp.where(lax.broadcasted_iota(I32, (LANE, width), 0) == blk_of_col, 1.0, 0.0).astype(BF16)
            hit = jnp.dot(sel_blocks, expand, preferred_element_type=F32)
            sel = (jnp.where(col <= row, hit, 0.0) > 0.5)[None]
            return _flash_step(carry, q, kst_ref[lo:hi, pl.ds(off, width)], vs_ref[0, pl.ds(off, width), lo:hi], sel,
                               rep)

        o_slc = _flash_finish(_over_tiles(plan, tk, slc_step, _flash_init(rows, hd)))

        s = jnp.dot(q, kwt_ref[lo:hi, pl.ds(w_start, win_len)], preferred_element_type=F32)
        s = jnp.where(w_ok, s.reshape(rep, Q_BLOCK, win_len), -jnp.inf)
        p = _softmax_rows(s).reshape(rows, win_len)
        o_win = jnp.dot(p.astype(BF16), vw_ref[0, pl.ds(w_start, win_len), lo:hi], preferred_element_type=F32)

        for r in range(rep):
            h = g * rep + r
            rs = slice(r * Q_BLOCK, (r + 1) * Q_BLOCK)
            o_ref[0, :, h * hd:(h + 1) * hd] = (
                gates[:, h:h + 1] * o_cmp[rs] + gates[:, n_heads + h:n_heads + h + 1] * o_slc[rs]
                + gates[:, 2 * n_heads + h:2 * n_heads + h + 1] * o_win[rs])


def nsa_prompt_attention(q, gate_logits, kct, vc, n_cmp, kst, vs, kwt, vw, n_groups, tk=ATTN_COLS):
    b, s, hd_all = q.shape
    gd, nc_pad = kct.shape[1], kct.shape[2]
    n_blk = -(-s // SLC_BLOCK)
    assert n_blk <= LANE and s % Q_BLOCK == 0
    tk = min(tk, s)
    win_len = min(WINDOW + Q_BLOCK, s)
    kern = functools.partial(_nsa_prompt_kernel, tk=tk, n_cmp=n_cmp, n_blk=n_blk, n_groups=n_groups,
                             rep=hd_all // gd, win_len=win_len)
    per_b = lambda *blk: pl.BlockSpec((1,) + blk, lambda bb, i: (bb,) + (0,) * len(blk))
    keys_t = pl.BlockSpec((gd, s), lambda bb, i: (0, bb))
    q_block = lambda width: pl.BlockSpec((1, Q_BLOCK, width), lambda bb, i: (bb, i, 0))
    return pl.pallas_call(
        kern,
        grid=(b, s // Q_BLOCK),
        in_specs=[q_block(hd_all), q_block(gate_logits.shape[-1]), per_b(gd, nc_pad), per_b(nc_pad, gd),
                  keys_t, per_b(s, gd), keys_t, per_b(s, gd)],
        out_specs=q_block(hd_all),
        out_shape=jax.ShapeDtypeStruct((b, s, hd_all), F32),
        scratch_shapes=[pltpu.VMEM((LANE, Q_BLOCK), F32)],
        compiler_params=pltpu.CompilerParams(
            dimension_semantics=("parallel", "arbitrary"), vmem_limit_bytes=VMEM_LIMIT),
        name="nsa_prompt_attention",
    )(q, gate_logits, kct, vc, kst, vs, kwt, vw)


def _compress_finish(a, pew_ref, w2_ref, n_groups, hd):
    half = n_groups * hd
    first, second = a[:, :half], a[:, half:]
    second = jnp.concatenate([second[1:], jnp.zeros((1, half), F32)], axis=0)
    out = []
    for g in range(n_groups):
        hcol = first[:, g * hd:(g + 1) * hd] + second[:, g * hd:(g + 1) * hd] + pew_ref[...]
        out.append(jnp.dot(jax.nn.silu(hcol).astype(BF16), w2_ref[...], preferred_element_type=F32))
    return out


def _compress_kernel(c_ref, wc_ref, pew_ref, w2_ref, o_ref, *, n_groups, hd, transposed):
    a = jnp.dot(c_ref[0].astype(BF16), wc_ref[...], preferred_element_type=F32)
    res = jnp.concatenate(_compress_finish(a, pew_ref, w2_ref, n_groups, hd), axis=1)
    o_ref[0] = (res.T if transposed else res).astype(BF16)


def _compress_chunk_weights(w1, n_groups, hd):
    wr = w1.reshape(2, CMP_STRIDE, hd, hd)
    eye = jnp.eye(n_groups, dtype=w1.dtype)
    wc = jnp.einsum('hrdo,gk->rgdhko', wr, eye)
    return wc.reshape(CMP_STRIDE * n_groups * hd, 2 * n_groups * hd)


def _compress_weights(pe, w1, w2, n_groups, hd):
    return _compress_chunk_weights(w1, n_groups, hd).astype(BF16), mm(pe.reshape(1, -1), w1), w2.astype(BF16)


def compress_rows(rows, cmp_w, n_groups, transposed):
    b, l, gd = rows.shape
    hd = gd // n_groups
    n_chunks = l // CMP_STRIDE
    cw = CMP_STRIDE * gd
    wc, pew, w2 = _compress_weights(*cmp_w, n_groups, hd)
    out_blk = (gd, n_chunks) if transposed else (n_chunks, gd)
    return pl.pallas_call(
        functools.partial(_compress_kernel, n_groups=n_groups, hd=hd, transposed=transposed),
        grid=(b,),
        in_specs=[pl.BlockSpec((1, n_chunks, cw), lambda bb: (bb, 0, 0)), _resident(wc.shape), _resident(pew.shape),
                  _resident(w2.shape)],
        out_specs=pl.BlockSpec((1,) + out_blk, lambda bb: (bb, 0, 0)),
        out_shape=jax.ShapeDtypeStruct((b,) + out_blk, BF16),
        compiler_params=pltpu.CompilerParams(dimension_semantics=("parallel",), vmem_limit_bytes=VMEM_LIMIT),
        name="compress_rows",
    )(rows.reshape(b, n_chunks, cw), wc, pew, w2)


def _attend_cols(q, kt, vt, sel, rep):
    n = kt.shape[1]
    s = jnp.dot(q, kt, preferred_element_type=F32)
    s = jnp.where(sel, s.reshape(rep, TQ, n), -jnp.inf)
    p = _softmax_rows(s).reshape(rep * TQ, n)
    return p, lax.dot_general(p.astype(BF16), vt, (((1,), (1,)), ((), ())), preferred_element_type=F32)


def _head_rows(x):
    b, t, h, d = x.shape
    x = jnp.pad(x.astype(BF16), ((0, 0), (0, TQ - t), (0, 0), (0, 0)))
    return x.transpose(0, 2, 1, 3).reshape(b, h * TQ, d)


def _new_cols(x, width):
    b, t = x.shape[:2]
    return jnp.pad(x.reshape(b, t, -1).astype(BF16).transpose(0, 2, 1), ((0, 0), (0, 0), (0, width - t)))


def _cols_view(pool):
    lead, rows = pool.shape[:2], pool.shape[2]
    perm = (0, 1) + tuple(range(3, pool.ndim)) + (2,)
    return pool.transpose(perm).reshape(lead + (-1, rows))


def _page_specs(block, layer, n_pages, pp):
    def spec(j):
        return pl.BlockSpec((1, 1) + block,
                            lambda bb, p, pt: (layer, pt[bb * n_pages + p * pp + j]) + (0,) * len(block))
    return [spec(j) for j in range(pp)]


def _per_batch(*blk):
    return pl.BlockSpec((1,) + blk, lambda bb, p, pt: (bb,) + (0,) * len(blk))


def _whole(*blk):
    return pl.BlockSpec(blk, lambda bb, p, pt: (0,) * len(blk))


def _dsa_sample_kernel(pt_ref, iq_ref, iw_ref, q_ref, ikn_ref, kn_ref, vn_ref, *rest, topk, past, n_new,
                       n_idx_heads, n_groups, rep, pp):
    idx_pages = rest[:pp]
    k_pages = rest[pp:2 * pp]
    v_pages = rest[2 * pp:3 * pp]
    o_ref, key_ref, kst, vst = rest[3 * pp:]
    p_step = pl.program_id(1)
    hd = q_ref.shape[-1]
    page = idx_pages[0].shape[-1]
    n_tiles = past // page + 1
    iw = iw_ref[0]
    tpos = past + jnp.minimum(lax.broadcasted_iota(I32, (TQ, 1), 0), n_new - 1)

    def scores(ikt):
        r = jnp.dot(iq_ref[0], ikt, preferred_element_type=F32)
        r = jnp.maximum(r, 0.0).reshape(n_idx_heads, TQ, ikt.shape[1]) * iw
        return jnp.sum(r, axis=0)

    for j in range(pp):
        off = pl.multiple_of((p_step * pp + j) * page, page)
        key_ref[:, pl.ds(off, page)] = _sortable_key(scores(idx_pages[j][0, 0].astype(BF16)))
        kst[:, pl.ds(off, page)] = k_pages[j][0, 0].astype(BF16)
        vst[:, pl.ds(off, page)] = v_pages[j][0, 0].astype(BF16)

    @pl.when(p_step == pl.num_programs(1) - 1)
    def _():
        c = lax.broadcasted_iota(I32, (TQ, page), 1)
        fresh_ok = (c < n_new) & (past + c <= tpos)
        key_ref[:, past:past + page] = jnp.where(fresh_ok, _sortable_key(scores(ikn_ref[0])), INT_MIN)
        kst[:, past:past + page] = kn_ref[0]
        vst[:, past:past + page] = vn_ref[0]
        keys = key_ref[...]
        col = lax.broadcasted_iota(I32, keys.shape, 1)
        thr, last_tie = _top_k_mask_params_small(keys, col, topk, tpos + 1)
        sel = ((keys > thr) | ((keys == thr) & (col <= last_tie)))[None]
        for g in range(n_groups):
            gs = slice(g * hd, (g + 1) * hd)
            _, o = _attend_cols(q_ref[0, g * rep * TQ:(g + 1) * rep * TQ], kst[gs, :], vst[gs, :], sel, rep)
            for r in range(rep):
                h = g * rep + r
                o_ref[0, :, h * hd:(h + 1) * hd] = o[r * TQ:(r + 1) * TQ]


def dsa_sample_attention(q, k, v, iq, ik, iw, pools_k, pools_v, pools_idx, layer, page_table, topk):
    b, t, h, d = q.shape
    g = k.shape[2]
    hi, di = iq.shape[2], iq.shape[3]
    n_pages = page_table.shape[1]
    page = pools_k.shape[2]
    past = n_pages * page
    pp = min(PAGES_PER_STEP, n_pages)
    assert n_pages % pp == 0 and t <= TQ
    iw_h = jnp.pad(iw.astype(F32), ((0, 0), (0, TQ - t), (0, 0))).transpose(0, 2, 1)[..., None]
    kern = functools.partial(_dsa_sample_kernel, topk=topk, past=past, n_new=t, n_idx_heads=hi, n_groups=g,
                             rep=h // g, pp=pp)
    o = pl.pallas_call(
        kern,
        grid_spec=pltpu.PrefetchScalarGridSpec(
            num_scalar_prefetch=1,
            grid=(b, n_pages // pp),
            in_specs=[_per_batch(hi * TQ, di), _per_batch(hi, TQ, 1), _per_batch(h * TQ, d), _per_batch(di, page),
                      _per_batch(g * d, page), _per_batch(g * d, page)]
                     + _page_specs((di, page), layer, n_pages, pp)
                     + _page_specs((g * d, page), layer, n_pages, pp)
                     + _page_specs((g * d, page), layer, n_pages, pp),
            out_specs=_per_batch(TQ, h * d),
            scratch_shapes=[pltpu.VMEM((TQ, past + page), I32), pltpu.VMEM((g * d, past + page), BF16),
                            pltpu.VMEM((g * d, past + page), BF16)],
        ),
        out_shape=jax.ShapeDtypeStruct((b, TQ, h * d), F32),
        compiler_params=pltpu.CompilerParams(
            dimension_semantics=("parallel", "arbitrary"), vmem_limit_bytes=VMEM_LIMIT),
        name="dsa_sample_attention",
    )(page_table.reshape(-1).astype(I32), _head_rows(iq), iw_h, _head_rows(q), _new_cols(ik, page),
      _new_cols(k, page), _new_cols(v, page),
      *([_cols_view(pools_idx)] * pp), *([_cols_view(pools_k)] * pp), *([_cols_view(pools_v)] * pp))
    return o[:, :t]


def _top_blocks_rows(imp, n_blk, n_sel):
    j_iota = lax.broadcasted_iota(I32, imp.shape, 1)
    rank = jnp.zeros(imp.shape, F32)
    for k in range(n_blk):
        col_k = imp[:, k:k + 1]
        earlier = jnp.where(j_iota > k, 1.0, 0.0)
        rank = rank + jnp.where(col_k > imp, 1.0, jnp.where(col_k == imp, earlier, 0.0))
    return jnp.where((rank < n_sel) & (j_iota < n_blk), 1.0, 0.0)


def _nsa_sample_kernel(pt_ref, q_ref, gate_ref, ksn_ref, vsn_ref, kwn_ref, vwn_ref, wink_ref, winv_ref,
                       wck_ref, wcv_ref, pewk_ref, pewv_ref, w2k_ref, w2v_ref, *rest, past, n_new, n_groups, rep, pp):
    kc_pages = rest[:pp]
    vc_pages = rest[pp:2 * pp]
    ks_pages = rest[2 * pp:3 * pp]
    vs_pages = rest[3 * pp:4 * pp]
    o_ref, kcs, vcs, kst, vst, kwt, vwt = rest[4 * pp:]
    p_step = pl.program_id(1)
    hd = q_ref.shape[-1]
    gd = n_groups * hd
    page = ks_pages[0].shape[-1]
    n_chunks = kcs.shape[0] // CMP_STRIDE
    n_cmp = n_chunks - CMP_BLOCK // CMP_STRIDE + 1
    n_blk = -(-(past + n_new) // SLC_BLOCK)
    w_buf = wink_ref.shape[-1]
    n_heads = n_groups * rep
    rows = rep * TQ

    for j in range(pp):
        off = pl.multiple_of((p_step * pp + j) * page, page)
        kcs[pl.ds(off, page), :] = kc_pages[j][0, 0].T
        vcs[pl.ds(off, page), :] = vc_pages[j][0, 0].T
        kst[:, pl.ds(off, page)] = ks_pages[j][0, 0].astype(BF16)
        vst[:, pl.ds(off, page)] = vs_pages[j][0, 0].astype(BF16)

    @pl.when(p_step == pl.num_programs(1) - 1)
    def _():
        kst[:, past:past + page] = ksn_ref[0]
        vst[:, past:past + page] = vsn_ref[0]
        kwt[:, 0:w_buf] = wink_ref[0, 0].astype(BF16)
        vwt[:, 0:w_buf] = winv_ref[0, 0].astype(BF16)
        kwt[:, w_buf:w_buf + LANE] = kwn_ref[0]
        vwt[:, w_buf:w_buf + LANE] = vwn_ref[0]

        tpos = past + jnp.minimum(lax.broadcasted_iota(I32, (TQ, 1), 0), n_new - 1)
        gates = jax.nn.sigmoid(gate_ref[0])

        def compress(rows_ref, wc_ref, pew_ref, w2_ref):
            a = jnp.zeros((n_chunks, 2 * gd), F32)
            for r in range(CMP_STRIDE):
                xr = rows_ref[pl.ds(r, n_chunks, stride=CMP_STRIDE), :].astype(BF16)
                a = a + jnp.dot(xr, wc_ref[r * gd:(r + 1) * gd, :], preferred_element_type=F32)
            out = _compress_finish(a, pew_ref, w2_ref, n_groups, hd)
            return [o.astype(BF16) for o in out]

        kcmp = compress(kcs, wck_ref, pewk_ref, w2k_ref)
        vcmp = compress(vcs, wcv_ref, pewv_ref, w2v_ref)

        c_idx = lax.broadcasted_iota(I32, (TQ, n_chunks), 1)
        c_ok = ((c_idx < n_cmp) & (c_idx * CMP_STRIDE + (CMP_BLOCK - 1) <= tpos))[None]
        cover = _block_cover(n_chunks)
        n_slc = kst.shape[1]
        col = lax.broadcasted_iota(I32, (TQ, n_slc), 1)
        blk_of_col = lax.broadcasted_iota(I32, (LANE, n_slc), 1) // SLC_BLOCK
        expand = jnp.where(lax.broadcasted_iota(I32, (LANE, n_slc), 0) == blk_of_col, 1.0, 0.0).astype(BF16)
        n_win = kwt.shape[1]
        w_c = lax.broadcasted_iota(I32, (TQ, n_win), 1)
        w_pos = past - w_buf + w_c
        w_d = tpos - w_pos
        w_ok = ((w_c < w_buf + n_new) & (w_pos >= 0) & (w_d >= 0) & (w_d < WINDOW))[None]

        for g in range(n_groups):
            q = q_ref[0, g * rows:(g + 1) * rows]
            gs = slice(g * hd, (g + 1) * hd)
            s = lax.dot_general(q, kcmp[g], (((1,), (1,)), ((), ())), preferred_element_type=F32)
            p_cmp = _softmax_rows(jnp.where(c_ok, s.reshape(rep, TQ, n_chunks), -jnp.inf))
            o_cmp = jnp.dot(p_cmp.reshape(rows, n_chunks).astype(BF16), vcmp[g], preferred_element_type=F32)
            imp = _block_importance(jnp.sum(p_cmp, axis=0), cover, tpos)
            sel_blocks = _top_blocks_rows(imp, n_blk, min(N_SLC, n_blk)).astype(BF16)
            hit = jnp.dot(sel_blocks, expand, preferred_element_type=F32)
            sel = (jnp.where(col <= tpos, hit, 0.0) > 0.5)[None]
            _, o_slc = _attend_cols(q, kst[gs, :], vst[gs, :], sel, rep)
            _, o_win = _attend_cols(q, kwt[gs, :], vwt[gs, :], w_ok, rep)
            for r in range(rep):
                h = g * rep + r
                rs = slice(r * TQ, (r + 1) * TQ)
                o_ref[0, :, h * hd:(h + 1) * hd] = (
                    gates[:, h:h + 1] * o_cmp[rs] + gates[:, n_heads + h:n_heads + h + 1] * o_slc[rs]
                    + gates[:, 2 * n_heads + h:2 * n_heads + h + 1] * o_win[rs])


def nsa_sample_attention(q, gate_logits, ks, vs, kw, vw, cmp_k, cmp_v, pools_kc, pools_vc, pools_ks, pools_vs,
                         wins_k, wins_v, layer, page_table):
    b, t, h, d = q.shape
    g = ks.shape[2]
    gd = g * d
    n_pages = page_table.shape[1]
    page = pools_ks.shape[2]
    past = n_pages * page
    pp = min(PAGES_PER_STEP, n_pages)
    assert n_pages % pp == 0 and t <= TQ and (past + t) // CMP_STRIDE * CMP_STRIDE == past
    cw = CMP_STRIDE * gd
    w_buf = wins_k.shape[2]

    wck, pewk, w2k = _compress_weights(*cmp_k, g, d)
    wcv, pewv, w2v = _compress_weights(*cmp_v, g, d)
    kern = functools.partial(_nsa_sample_kernel, past=past, n_new=t, n_groups=g, rep=h // g, pp=pp)
    win_spec = pl.BlockSpec((1, 1, gd, w_buf), lambda bb, p, pt: (layer, bb, 0, 0))
    o = pl.pallas_call(
        kern,
        grid_spec=pltpu.PrefetchScalarGridSpec(
            num_scalar_prefetch=1,
            grid=(b, n_pages // pp),
            in_specs=[_per_batch(h * TQ, d), _per_batch(TQ, 3 * h), _per_batch(gd, page), _per_batch(gd, page),
                      _per_batch(gd, LANE), _per_batch(gd, LANE), win_spec, win_spec,
                      _whole(cw, 2 * gd), _whole(cw, 2 * gd), _whole(1, d), _whole(1, d), _whole(d, d), _whole(d, d)]
                     + _page_specs((gd, page), layer, n_pages, pp) + _page_specs((gd, page), layer, n_pages, pp)
                     + _page_specs((gd, page), layer, n_pages, pp) + _page_specs((gd, page), layer, n_pages, pp),
            out_specs=_per_batch(TQ, h * d),
            scratch_shapes=[pltpu.VMEM((past, gd), F32), pltpu.VMEM((past, gd), F32),
                            pltpu.VMEM((gd, past + page), BF16), pltpu.VMEM((gd, past + page), BF16),
                            pltpu.VMEM((gd, w_buf + LANE), BF16), pltpu.VMEM((gd, w_buf + LANE), BF16)],
        ),
        out_shape=jax.ShapeDtypeStruct((b, TQ, h * d), F32),
        compiler_params=pltpu.CompilerParams(
            dimension_semantics=("parallel", "arbitrary"), vmem_limit_bytes=VMEM_LIMIT),
        name="nsa_sample_attention",
    )(page_table.reshape(-1).astype(I32), _head_rows(q),
      jnp.pad(gate_logits.astype(F32), ((0, 0), (0, TQ - t), (0, 0))),
      _new_cols(ks, page), _new_cols(vs, page), _new_cols(kw, LANE), _new_cols(vw, LANE),
      _cols_view(wins_k), _cols_view(wins_v), wck, wcv, pewk, pewv, w2k, w2v,
      *([_cols_view(pools_kc)] * pp), *([_cols_view(pools_vc)] * pp),
      *([_cols_view(pools_ks)] * pp), *([_cols_view(pools_vs)] * pp))
    return o[:, :t]


def last_rows(a, n):
    t = a.shape[1]
    if t >= n:
        return a[:, t - n:]
    return jnp.pad(a, ((0, 0), (n - t, 0)) + ((0, 0),) * (a.ndim - 2))


def mem_kv(mem, g, w_kv):
    b, m, _ = mem.shape
    k, v = jnp.split(mm(mem, w_kv, g), 2, axis=-1)
    return k.reshape(b, m, X_HEADS, X_HEAD_DIM), v.reshape(b, m, X_HEADS, X_HEAD_DIM)


def _xattn_kernel(x_ref, g_ref, wq_ref, mk_ref, mv_ref, wo_ref, o_ref, *, n_heads, hd):
    xn = _rms(x_ref[...], g_ref[...]).astype(BF16)
    q = jnp.dot(xn, wq_ref[...], preferred_element_type=F32).astype(BF16)
    mk = mk_ref[0].astype(BF16)
    mv = mv_ref[0].astype(BF16)
    outs = []
    for h in range(n_heads):
        hs = slice(h * hd, (h + 1) * hd)
        s = lax.dot_general(q[:, hs], mk[:, hs], (((1,), (1,)), ((), ())), preferred_element_type=F32) * (hd ** -0.5)
        e = jnp.exp(s - jnp.max(s, axis=-1, keepdims=True))
        p = e / jnp.sum(e, axis=-1, keepdims=True)
        outs.append(jnp.dot(p.astype(BF16), mv[:, hs], preferred_element_type=F32).astype(BF16))
    a = jnp.concatenate(outs, axis=1)
    for c0 in range(0, o_ref.shape[1], MM_COLS):
        cs = slice(c0, c0 + MM_COLS)
        o_ref[:, cs] = x_ref[:, cs] + jnp.dot(a, wo_ref[:, cs], preferred_element_type=F32)


def cross_attn_rows(x, g, mk, mv, w_q, w_o):
    b, t, d = x.shape
    mlen, n_heads, hd = mk.shape[1:]
    hw = n_heads * hd
    tm = _pick_tile(t, (512, 256, 128, 64, 32, 16, 8))
    steps = t // tm
    rows = pl.BlockSpec((tm, d), lambda i: (i, 0))
    mem = pl.BlockSpec((1, mlen, hw), lambda i: (i // steps, 0, 0))
    out = pl.pallas_call(
        functools.partial(_xattn_kernel, n_heads=n_heads, hd=hd),
        grid=(b * steps,),
        in_specs=[rows, _resident((1, d)), _resident((d, hw)), mem, mem, _resident((hw, d))],
        out_specs=rows,
        out_shape=jax.ShapeDtypeStruct((b * t, d), F32),
        compiler_params=pltpu.CompilerParams(dimension_semantics=("parallel",), vmem_limit_bytes=VMEM_LIMIT),
        name="cross_attn_rows",
    )(x.reshape(b * t, d), g.reshape(1, d).astype(F32), w_q.astype(BF16), mk.reshape(b, mlen, hw),
      mv.reshape(b, mlen, hw), w_o.astype(BF16))
    return out.reshape(b, t, d)


def cross_attn(x, g, mk, mv, w_q, w_o):
    b, t, _ = x.shape
    if t % SUBLANE == 0:
        return cross_attn_rows(x, g, mk, mv, w_q, w_o)
    q = mm(x, w_q, g).reshape(b, t, X_HEADS, X_HEAD_DIM)
    s = jnp.einsum('bthd,bmhd->bhtm', q, mk).astype(F32) * (X_HEAD_DIM ** -0.5)
    p = jax.nn.softmax(s, axis=-1).astype(x.dtype)
    return mm(jnp.einsum('bhtm,bmhd->bthd', p, mv).reshape(b, t, -1), w_o, res=x)


A_WIDTHS = (A_HEADS * HEAD_DIM, A_KV_HEADS * HEAD_DIM, A_KV_HEADS * HEAD_DIM, IDX_HEADS * IDX_DIM, IDX_DIM, IDX_HEADS)


def dsa_project(x, g, w_in, pos, keys_transposed):
    b, t, d = x.shape
    keys = [(F32, False), (BF16, keys_transposed)]
    segments = [(True, HEAD_DIM ** -0.5, [(BF16, False)]), (True, None, keys), (False, None, [(F32, False), (BF16, False)]),
                (True, None, [(BF16, False)]), (True, None, keys),
                (False, IDX_HEADS ** -0.5 * IDX_DIM ** -0.5, [(F32, False)])]
    return project(x.reshape(b * t, d), g, _split_cols(w_in, A_WIDTHS), segments, jnp.tile(pos, b))


def dsa_prompt(x, g, w_in, w_out):
    b, s, _ = x.shape
    q, k, kt, v, vb, iq, ik, ikt, iw = dsa_project(x, g, w_in, jnp.arange(s), True)
    o = dsa_prompt_attention(q.reshape(b, s, -1), kt, vb.reshape(b, s, -1), iq.reshape(b, s, -1), ikt,
                             iw.reshape(b, s, -1), min(TOPK_MAX, s // 4), A_KV_HEADS, IDX_HEADS)
    state = (k.reshape(b, s, A_KV_HEADS, HEAD_DIM), v.reshape(b, s, A_KV_HEADS, HEAD_DIM), ik.reshape(b, s, IDX_DIM))
    return mm(o, w_out, res=x), state


def dsa_sample(x, g, w_in, w_out, pools_k, pools_v, pools_idx, layer, page_table):
    b, t, _ = x.shape
    past = page_table.shape[1] * pools_k.shape[2]
    q, k, kb, v, vb, iq, ik, ikb, iw = dsa_project(x, g, w_in, past + jnp.arange(t), False)
    o = dsa_sample_attention(q.reshape(b, t, A_HEADS, HEAD_DIM), kb.reshape(b, t, A_KV_HEADS, HEAD_DIM),
                             vb.reshape(b, t, A_KV_HEADS, HEAD_DIM), iq.reshape(b, t, IDX_HEADS, IDX_DIM),
                             ikb.reshape(b, t, IDX_DIM), iw.reshape(b, t, IDX_HEADS), pools_k, pools_v, pools_idx,
                             layer, page_table, min(TOPK_MAX, (past + t) // 4))
    state = (k.reshape(b, t, A_KV_HEADS, HEAD_DIM), v.reshape(b, t, A_KV_HEADS, HEAD_DIM), ik.reshape(b, t, IDX_DIM))
    return mm(o, w_out, res=x), state


def _shift_rows(x, prev, k):
    return jnp.concatenate([prev[prev.shape[0] - k:], x[:x.shape[0] - k]], axis=0)


def _tile_transpose(m, rows_out, cols_out):
    r, c = m.shape
    tile = m
    if c < LANE:
        tile = jnp.concatenate([tile, jnp.zeros((r, LANE - c), m.dtype)], axis=1)
    if r < LANE:
        tile = jnp.concatenate([tile, jnp.zeros((LANE - r, LANE), m.dtype)], axis=0)
    return tile.T[:rows_out, :cols_out]


def _ssd_kernel(z_ref, x_ref, dt_ref, cw_ref, cb_ref, dtb_ref, alog_ref, dskip_ref, buf_ref, h0_ref,
                y_ref, hout_ref, carry_ref, ht_ref, *, n_heads, n_groups, hd, n_state):
    c = pl.program_id(1)
    l = x_ref.shape[0]
    d_inner = n_heads * hd
    rep = n_heads // n_groups
    n_pairs = n_heads // 2

    @pl.when(c == 0)
    def _():
        carry_ref[...] = buf_ref[0]
        for p in range(n_pairs):
            ht_ref[p] = jnp.concatenate([_tile_transpose(h0_ref[0, 2 * p], n_state, hd),
                                         _tile_transpose(h0_ref[0, 2 * p + 1], n_state, hd)], axis=1)

    x = x_ref[...]
    prev = carry_ref[...]
    conv = cb_ref[...] + x * cw_ref[B_CONV - 1:B_CONV, :]
    for k in range(1, B_CONV):
        conv = conv + _shift_rows(x, prev, k) * cw_ref[B_CONV - 1 - k:B_CONV - k, :]
    carry_ref[...] = x[l - SUBLANE:]
    act = jax.nn.silu(conv)
    gn = n_groups * n_state
    xs, bm, cm = act[:, :d_inner], act[:, d_inner:d_inner + gn], act[:, d_inner + gn:]

    dt = jax.nn.softplus(dt_ref[...] + dtb_ref[...])
    a = -jnp.exp(alog_ref[...])
    acs = dt * a
    k = 1
    while k < l:
        acs = acs + jnp.concatenate([jnp.zeros((k, n_heads), F32), acs[:l - k]], axis=0)
        k *= 2
    acs_t = jnp.concatenate([acs, jnp.zeros((l, LANE - n_heads), F32)], axis=1).T

    spread = jnp.where(lax.broadcasted_iota(I32, (n_heads, d_inner), 1) // hd
                       == lax.broadcasted_iota(I32, (n_heads, d_inner), 0), 1.0, 0.0).astype(BF16)

    def per_column(v):
        out = jnp.zeros((l, d_inner), F32)
        for _ in range(3):
            part = v.astype(BF16)
            out = out + jnp.dot(part, spread, preferred_element_type=F32)
            v = v - part.astype(F32)
        return out

    dt_cols = per_column(dt)
    acs_cols = per_column(acs)
    last_cols = acs_cols[l - 1:l, :]
    xr_all = xs * dt_cols
    xw_all = (xr_all * jnp.exp(last_cols - acs_cols)).astype(BF16)
    grow_cols = jnp.exp(acs_cols)
    decay_cols = jnp.exp(last_cols)
    y_skip = xs * dskip_ref[...]
    gate = jax.nn.silu(z_ref[...])
    causal = lax.broadcasted_iota(I32, (l, l), 0) >= lax.broadcasted_iota(I32, (l, l), 1)
    first_head = lax.broadcasted_iota(I32, (l, LANE), 1) < hd

    for g in range(n_groups):
        bm_g = bm[:, g * n_state:(g + 1) * n_state]
        cm_g = cm[:, g * n_state:(g + 1) * n_state].astype(BF16)
        cb = lax.dot_general(cm_g, bm_g.astype(BF16), (((1,), (1,)), ((), ())), preferred_element_type=F32)
        bm_t = bm_g.T.astype(BF16)
        for p in range(g * rep // 2, (g + 1) * rep // 2):
            ps = slice(p * LANE, (p + 1) * LANE)
            xr = xr_all[:, ps]
            lhs = []
            for h in (2 * p, 2 * p + 1):
                seg = acs[:, h:h + 1] - acs_t[h:h + 1, :]
                lhs.append(cb * jnp.where(causal, jnp.exp(jnp.where(causal, seg, 0.0)), 0.0))
            rhs = jnp.concatenate([jnp.where(first_head, xr, 0.0), jnp.where(first_head, 0.0, xr)], axis=0)
            y = jnp.dot(jnp.concatenate(lhs, axis=1).astype(BF16), rhs.astype(BF16), preferred_element_type=F32)
            ht = ht_ref[p]
            y = y + jnp.dot(cm_g, ht.astype(BF16), preferred_element_type=F32) * grow_cols[:, ps]
            ht_ref[p] = ht * decay_cols[:, ps] + jnp.dot(bm_t, xw_all[:, ps], preferred_element_type=F32)
            y_ref[:, ps] = (y + y_skip[:, ps]) * gate[:, ps]

    @pl.when(c == pl.num_programs(1) - 1)
    def _():
        for p in range(n_pairs):
            ht = ht_ref[p]
            hout_ref[0, 2 * p] = _tile_transpose(ht[:, :hd], hd, n_state)
            hout_ref[0, 2 * p + 1] = _tile_transpose(ht[:, hd:], hd, n_state)


def ssd_prompt(z, xbc, dt, conv_w, conv_b, dt_bias, a_log, d_skip, conv_buf, ssm0, n_groups):
    b, n_heads, hd, n_state = ssm0.shape
    m, w = xbc.shape
    s = m // b
    l = min(B_CHUNK, s)
    nc = s // l
    d_inner = n_heads * hd
    buf = jnp.pad(conv_buf.astype(F32), ((0, 0), (SUBLANE - (B_CONV - 1), 0), (0, 0)))
    dskip_full = jnp.repeat(d_skip.astype(F32), hd).reshape(1, d_inner)
    rows = lambda width: pl.BlockSpec((l, width), lambda bb, c: (bb * nc + c, 0))
    state = pl.BlockSpec((1, n_heads, hd, n_state), lambda bb, c: (bb, 0, 0, 0))
    kern = functools.partial(_ssd_kernel, n_heads=n_heads, n_groups=n_groups, hd=hd, n_state=n_state)
    return pl.pallas_call(
        kern,
        grid=(b, nc),
        in_specs=[rows(d_inner), rows(w), rows(n_heads), _resident((B_CONV, w)), _resident((1, w)),
                  _resident((1, n_heads)), _resident((1, n_heads)), _resident((1, d_inner)),
                  pl.BlockSpec((1, SUBLANE, w), lambda bb, c: (bb, 0, 0)), state],
        out_specs=[rows(d_inner), state],
        out_shape=[jax.ShapeDtypeStruct((m, d_inner), F32), jax.ShapeDtypeStruct(ssm0.shape, F32)],
        scratch_shapes=[pltpu.VMEM((SUBLANE, w), F32), pltpu.VMEM((n_heads // 2, n_state, 2 * hd), F32)],
        compiler_params=pltpu.CompilerParams(
            dimension_semantics=("parallel", "arbitrary"), vmem_limit_bytes=VMEM_LIMIT),
        name="ssd_prompt",
    )(z, xbc, dt, conv_w.astype(F32), conv_b.reshape(1, w).astype(F32), dt_bias.reshape(1, n_heads).astype(F32),
      a_log.reshape(1, n_heads).astype(F32), dskip_full, buf, ssm0.astype(F32))


def mamba_prompt(x, g, w_in, conv_w, conv_b, dt_bias, a_log, d_skip, g_norm, w_out, conv_buf, ssm0):
    b, s, d = x.shape
    plain = (False, None, [(F32, False)])
    z, xbc, dt = project(x.reshape(b * s, d), g, _split_cols(w_in, (B_D_INNER, B_CONV_DIM, B_HEADS)),
                         [plain, plain, plain], jnp.zeros((b * s,), I32))
    y, ssm = ssd_prompt(z, xbc, dt, conv_w, conv_b, dt_bias, a_log, d_skip, conv_buf, ssm0, B_GROUPS)
    xpad = jnp.concatenate([conv_buf.astype(F32), xbc.reshape(b, s, -1)[:, s - (B_CONV - 1):]], axis=1)
    new_buf = xpad[:, xpad.shape[1] - (B_CONV - 1):]
    return mm(y.reshape(b, s, -1), w_out, g_norm, res=x), new_buf, ssm.astype(ssm0.dtype)


def ssd_chunked(x, dt, a, bm, cm, h0):
    bsz, t, nh, p = x.shape
    g, n = bm.shape[2], bm.shape[3]
    r = nh // g
    l = min(B_CHUNK, t)
    nc = t // l
    xr = (x * dt[..., None]).reshape(bsz, nc, l, g, r, p)
    acs = jnp.cumsum((dt * a).reshape(bsz, nc, l, g, r), axis=2)
    br = bm.reshape(bsz, nc, l, g, n)
    cr = cm.reshape(bsz, nc, l, g, n)
    seg = acs[:, :, :, None] - acs[:, :, None, :]
    causal = jnp.tril(jnp.ones((l, l), bool))[:, :, None, None]
    decay = jnp.where(causal, jnp.exp(jnp.where(causal, seg, 0.0)), 0.0)
    cb = jnp.einsum('bcign,bcjgn->bcijg', cr, br)
    y_intra = jnp.einsum('bcijg,bcijgr,bcjgrp->bcigrp', cb, decay, xr)
    to_end = jnp.exp(acs[:, :, -1:] - acs)
    s_chunk = jnp.einsum('bclgn,bclgr,bclgrp->bcgrpn', br, to_end, xr)
    d_chunk = jnp.exp(acs[:, :, -1])

    def step(hc, inp):
        s_c, d_c = inp
        return hc * d_c[..., None, None] + s_c, hc

    h_fin, h_in = lax.scan(step, h0.reshape(bsz, g, r, p, n),
                           (jnp.moveaxis(s_chunk, 1, 0), jnp.moveaxis(d_chunk, 1, 0)))
    h_in = jnp.moveaxis(h_in, 0, 1)
    y_inter = jnp.einsum('bcign,bcigr,bcgrpn->bcigrp', cr, jnp.exp(acs), h_in)
    return (y_intra + y_inter).reshape(bsz, t, nh, p), h_fin.reshape(bsz, nh, p, n)


def mamba_mixer(x, g, w_in, conv_w, conv_b, dt_bias, a_log, d_skip, g_norm, w_out, conv_buf, ssm0):
    b, t, _ = x.shape
    proj = mm(x, w_in, g)
    z = proj[..., :B_D_INNER]
    xbc = proj[..., B_D_INNER:B_D_INNER + B_CONV_DIM]
    dt = proj[..., B_D_INNER + B_CONV_DIM:]
    xpad = jnp.concatenate([conv_buf.astype(xbc.dtype), xbc], axis=1)
    conv = conv_b + sum(xpad[:, j:j + t] * conv_w[j] for j in range(B_CONV))
    xbc = jax.nn.silu(conv)
    new_buf = xpad[:, t:]
    gn = B_GROUPS * B_STATE
    xs = xbc[..., :B_D_INNER].reshape(b, t, B_HEADS, B_HEADDIM).astype(F32)
    bm = xbc[..., B_D_INNER:B_D_INNER + gn].reshape(b, t, B_GROUPS, B_STATE).astype(F32)
    cm = xbc[..., B_D_INNER + gn:].reshape(b, t, B_GROUPS, B_STATE).astype(F32)
    dt = jax.nn.softplus(dt.astype(F32) + dt_bias.astype(F32))
    a = -jnp.exp(a_log.astype(F32))
    y, ssm = ssd_chunked(xs, dt, a, bm, cm, ssm0.astype(F32))
    y = (y + xs * d_skip.astype(F32)[:, None]).reshape(b, t, B_D_INNER).astype(x.dtype)
    y = y * jax.nn.silu(z)
    return mm(y, w_out, g_norm, res=x), new_buf, ssm.astype(ssm0.dtype)


C_KV_WIDTH = C_KV_HEADS * HEAD_DIM
C_WIDTHS = (C_HEADS * HEAD_DIM,) + (C_KV_WIDTH,) * 6 + (3 * C_HEADS,)


def nsa_project(x, g, w_in, pos, keys_transposed):
    b, t, d = x.shape
    keys = (True, None, [(F32, False), (BF16, keys_transposed)])
    vals = (False, None, [(F32, False), (BF16, False)])
    segments = [(True, HEAD_DIM ** -0.5, [(BF16, False)]), (True, None, [(F32, False)]), (False, None, [(F32, False)]),
                keys, vals, keys, vals, (False, None, [(F32, False)])]
    return project(x.reshape(b * t, d), g, _split_cols(w_in, C_WIDTHS), segments, jnp.tile(pos, b))


def nsa_prompt(x, g, w_in, w_out, cmp_k, cmp_v, w_buf):
    b, s, _ = x.shape
    q, kc, vc, ks, kst, vs, vsb, kw, kwt, vw, vwb, gates = nsa_project(x, g, w_in, jnp.arange(s), True)
    seq = lambda a: a.reshape(b, s, -1)
    kct = compress_rows(seq(kc), cmp_k, C_KV_HEADS, True)
    vcc = compress_rows(seq(vc), cmp_v, C_KV_HEADS, False)
    n_cmp = s // CMP_STRIDE - CMP_BLOCK // CMP_STRIDE + 1
    o = nsa_prompt_attention(seq(q), seq(gates), kct, vcc, n_cmp, kst, seq(vsb), kwt, seq(vwb), C_KV_HEADS)
    heads = lambda a: a.reshape(b, s, C_KV_HEADS, HEAD_DIM)
    state = (heads(kc), heads(vc), heads(ks), heads(vs), last_rows(heads(kw), w_buf), last_rows(heads(vw), w_buf))
    return mm(o, w_out, res=x), state


def nsa_sample(x, g, w_in, w_out, cmp_k, cmp_v, pools_kc, pools_vc, pools_ks, pools_vs, wins_k, wins_v, layer,
               page_table):
    b, t, _ = x.shape
    past = page_table.shape[1] * pools_kc.shape[2]
    q, kc, vc, ks, ksb, vs, vsb, kw, kwb, vw, vwb, gates = nsa_project(x, g, w_in, past + jnp.arange(t), False)
    heads = lambda a: a.reshape(b, t, C_KV_HEADS, HEAD_DIM)
    o = nsa_sample_attention(q.reshape(b, t, C_HEADS, HEAD_DIM), gates.reshape(b, t, -1), heads(ksb), heads(vsb),
                             heads(kwb), heads(vwb), cmp_k, cmp_v, pools_kc, pools_vc, pools_ks, pools_vs, wins_k,
                             wins_v, layer, page_table)
    win_k, win_v = wins_k[layer], wins_v[layer]
    w_buf = win_k.shape[1]
    kwin = jnp.concatenate([win_k, heads(kw)], axis=1)
    vwin = jnp.concatenate([win_v, heads(vw)], axis=1)
    state = (heads(kc), heads(vc), heads(ks), heads(vs), last_rows(kwin, w_buf), last_rows(vwin, w_buf))
    return mm(o, w_out, res=x), state


def kernel(x_prompt, x_sample, cache_a_k, cache_a_v, cache_a_idx, state_b_ssm, state_b_conv, cache_c_cmp_k, cache_c_cmp_v, cache_c_slc_k, cache_c_slc_v, cache_c_win_k, cache_c_win_v, cache_mem_k, cache_mem_v, page_table, mem_prompt, g_ffn1, ffn1_wi, ffn1_wo, g_mix, g_xattn, g_mem, x_w_q, x_w_kv, x_w_o, g_ffn2, ffn2_wi, ffn2_wo, g_final, a_w_in, a_w_out, b_w_in, b_conv_w, b_conv_b, b_dt_bias, b_a_log, b_d_skip, b_g_norm, b_w_out, c_w_in, c_w_out, c_pe_k, c_w1_k, c_w2_k, c_pe_v, c_w1_v, c_w2_v):
    xp, xs = x_prompt, x_sample
    w_buf = cache_c_win_k.shape[2]
    ak_p, av_p, ai_p, bs_p, bc_p = [], [], [], [], []
    cck_p, ccv_p, csk_p, csv_p, cwk_p, cwv_p = [], [], [], [], [], []
    mk_p, mv_p = [], []
    ak_s, av_s, ai_s, bs_s, bc_s = [], [], [], [], []
    cck_s, ccv_s, csk_s, csv_s, cwk_s, cwv_s = [], [], [], [], [], []
    ia = ib = ic = 0
    for i in range(DEPTH):
        xp = ffn_residual(xp, g_ffn1[i], ffn1_wi[i], ffn1_wo[i])
        xs = ffn_residual(xs, g_ffn1[i], ffn1_wi[i], ffn1_wo[i])
        kind = i % N_MIXERS
        if kind == 0:
            op, (k1, v1, i1) = dsa_prompt(xp, g_mix[i], a_w_in[ia], a_w_out[ia])
            os_, (k2, v2, i2) = dsa_sample(xs, g_mix[i], a_w_in[ia], a_w_out[ia], cache_a_k, cache_a_v, cache_a_idx,
                                           ia, page_table)
            ak_p.append(k1); av_p.append(v1); ai_p.append(i1)
            ak_s.append(k2); av_s.append(v2); ai_s.append(i2)
            ia += 1
        elif kind == 1:
            conv0 = jnp.zeros((xp.shape[0], B_CONV - 1, B_CONV_DIM), xp.dtype)
            ssm0 = jnp.zeros((xp.shape[0], B_HEADS, B_HEADDIM, B_STATE), F32)
            wts = (b_w_in[ib], b_conv_w[ib], b_conv_b[ib], b_dt_bias[ib], b_a_log[ib], b_d_skip[ib],
                   b_g_norm[ib], b_w_out[ib])
            op, c1, s1 = mamba_prompt(xp, g_mix[i], *wts, conv0, ssm0)
            os_, c2, s2 = mamba_mixer(xs, g_mix[i], *wts, state_b_conv[ib], state_b_ssm[ib])
            bc_p.append(c1); bs_p.append(s1); bc_s.append(c2); bs_s.append(s2)
            ib += 1
        else:
            cmp_k = (c_pe_k[ic], c_w1_k[ic], c_w2_k[ic])
            cmp_v = (c_pe_v[ic], c_w1_v[ic], c_w2_v[ic])
            op, st1 = nsa_prompt(xp, g_mix[i], c_w_in[ic], c_w_out[ic], cmp_k, cmp_v, w_buf)
            os_, st2 = nsa_sample(xs, g_mix[i], c_w_in[ic], c_w_out[ic], cmp_k, cmp_v, cache_c_cmp_k, cache_c_cmp_v,
                                  cache_c_slc_k, cache_c_slc_v, cache_c_win_k, cache_c_win_v, ic, page_table)
            for lst, arr in zip((cck_p, ccv_p, csk_p, csv_p, cwk_p, cwv_p), st1):
                lst.append(arr)
            for lst, arr in zip((cck_s, ccv_s, csk_s, csv_s, cwk_s, cwv_s), st2):
                lst.append(arr)
            ic += 1
        xp, xs = op, os_
        mk, mv = mem_kv(mem_prompt, g_mem[i], x_w_kv[i])
        mk_p.append(mk); mv_p.append(mv)
        xp = cross_attn(xp, g_xattn[i], mk, mv, x_w_q[i], x_w_o[i])
        xs = cross_attn(xs, g_xattn[i], cache_mem_k[i], cache_mem_v[i], x_w_q[i], x_w_o[i])
        xp = ffn_residual(xp, g_ffn2[i], ffn2_wi[i], ffn2_wo[i])
        xs = ffn_residual(xs, g_ffn2[i], ffn2_wi[i], ffn2_wo[i])
    y_prompt = rms_norm_rows(xp, g_final)
    y_sample = rms_norm_rows(xs, g_final)
    st = jnp.stack
    return (y_prompt, y_sample,
            st(ak_p), st(av_p), st(ai_p), st(bs_p), st(bc_p),
            st(cck_p), st(ccv_p), st(csk_p), st(csv_p), st(cwk_p), st(cwv_p),
            st(mk_p), st(mv_p),
            st(ak_s), st(av_s), st(ai_s), st(bs_s), st(bc_s),
            st(cck_s), st(ccv_s), st(csk_s), st(csv_s), st(cwk_s), st(cwv_s))
```

```python
import functools

import jax
import jax.numpy as jnp
import numpy as np
from jax import lax
from jax.experimental import pallas as pl
from jax.experimental.pallas import tpu as pltpu

F32 = jnp.float32
BF16 = jnp.bfloat16
I32 = jnp.int32
INT_MIN = -2 ** 31
EPS = 1e-6

D_MODEL = 1024
DEPTH = 4
N_MIXERS = 3
HEAD_DIM = 64
ROPE_THETA = 10000.0
Q_BLOCK = 128
A_HEADS = D_MODEL // HEAD_DIM
A_KV_HEADS = 4
IDX_HEADS = 8
IDX_DIM = 64
TOPK_MAX = 256
B_D_INNER = 2 * D_MODEL
B_HEADDIM = 64
B_HEADS = B_D_INNER // B_HEADDIM
B_GROUPS = 4
B_STATE = 128
B_CONV = 4
B_CONV_DIM = B_D_INNER + 2 * B_GROUPS * B_STATE
B_CHUNK = 128
C_HEADS = D_MODEL // HEAD_DIM
C_KV_HEADS = 2
CMP_BLOCK = 32
CMP_STRIDE = 16
SLC_BLOCK = 64
N_SLC = 16
WINDOW = 512
X_HEADS = 4
X_HEAD_DIM = 128
D_FF = 2816

LANE = 128
SUBLANE = 8
VMEM_LIMIT = 48 * 1024 * 1024
MM_COLS = 512
FF_CHUNK = 256
ATTN_COLS = 2048
TQ = SUBLANE
PAGES_PER_STEP = 16


def _pick_tile(n, candidates):
    for c in candidates:
        if n % c == 0:
            return c
    return n


def _resident(shape):
    return pl.BlockSpec(shape, lambda *_: (0,) * len(shape), pipeline_mode=pl.Buffered(1))


def _rms(x, g):
    return x * lax.rsqrt(jnp.mean(x * x, axis=-1, keepdims=True) + EPS) * g


def _nmm_kernel(x_ref, g_ref, w_ref, *rest, norm, residual):
    o_ref = rest[-1]
    x = x_ref[...]
    if norm:
        x = _rms(x, g_ref[...])
    xb = x.astype(BF16)
    n = w_ref.shape[1]
    for c0 in range(0, n, MM_COLS):
        c1 = min(c0 + MM_COLS, n)
        y = jnp.dot(xb, w_ref[:, c0:c1], preferred_element_type=F32)
        o_ref[:, c0:c1] = rest[0][:, c0:c1] + y if residual else y


def norm_matmul(x, w, g=None, res=None):
    m, k = x.shape
    n = w.shape[1]
    assert res is None or (n % LANE == 0 and m % SUBLANE == 0)
    n_pad = -(-n // LANE) * LANE
    wb = w.astype(BF16)
    if n_pad != n:
        wb = jnp.pad(wb, ((0, 0), (0, n_pad - n)))
    m_orig = m
    if m % SUBLANE != 0:
        m = -(-m // 256) * 256
        x = jnp.pad(x, ((0, m - m_orig), (0, 0)))
    tm = _pick_tile(m, (512, 256, 128, 64, 32, 16, 8))
    norm = g is not None
    gg = (g if norm else jnp.ones((k,), F32)).reshape(1, k).astype(F32)
    row_block = lambda width: pl.BlockSpec((tm, width), lambda i: (i, 0))
    extra = [] if res is None else [res]
    out = pl.pallas_call(
        functools.partial(_nmm_kernel, norm=norm, residual=res is not None),
        grid=(m // tm,),
        in_specs=[row_block(k), _resident((1, k)), _resident((k, n_pad))] + [row_block(n_pad) for _ in extra],
        out_specs=row_block(n_pad),
        out_shape=jax.ShapeDtypeStruct((m, n_pad), F32),
        compiler_params=pltpu.CompilerParams(dimension_semantics=("parallel",), vmem_limit_bytes=VMEM_LIMIT),
        name="norm_matmul",
    )(x, gg, wb, *extra)
    if n_pad != n or m != m_orig:
        out = out[:m_orig, :n]
    return out


def mm(x, w, g=None, res=None):
    lead = x.shape[:-1]
    r2 = None if res is None else res.reshape(-1, res.shape[-1])
    return norm_matmul(x.reshape(-1, x.shape[-1]), w, g, r2).reshape(lead + (w.shape[1],))


def _ffn_kernel(x_ref, g_ref, wg_ref, wu_ref, wo_ref, o_ref, act_ref, *, n_chunks):
    xn = _rms(x_ref[...], g_ref[...]).astype(BF16)

    for c in range(n_chunks):
        cs = slice(c * FF_CHUNK, (c + 1) * FF_CHUNK)
        hg = jnp.dot(xn, wg_ref[:, cs], preferred_element_type=F32)
        hu = jnp.dot(xn, wu_ref[:, cs], preferred_element_type=F32)
        act_ref[:, cs] = (jax.nn.silu(hg) * hu).astype(BF16)
    for c0 in range(0, o_ref.shape[1], MM_COLS):
        cs = slice(c0, c0 + MM_COLS)
        o_ref[:, cs] = x_ref[:, cs] + 0.5 * jnp.dot(act_ref[...], wo_ref[:, cs], preferred_element_type=F32)


def ffn_residual(x, g, wi, wo):
    lead, d = x.shape[:-1], x.shape[-1]
    x2 = x.reshape(-1, d)
    m = x2.shape[0]
    f = wo.shape[0]
    assert f % FF_CHUNK == 0 and d % MM_COLS == 0
    tm = _pick_tile(m, (512, 256, 128, 64, 32, 16, 8))
    wb = wi.astype(BF16)
    out = pl.pallas_call(
        functools.partial(_ffn_kernel, n_chunks=f // FF_CHUNK),
        grid=(m // tm,),
        in_specs=[pl.BlockSpec((tm, d), lambda i: (i, 0)), _resident((1, d)), _resident((d, f)), _resident((d, f)),
                  _resident((f, d))],
        out_specs=pl.BlockSpec((tm, d), lambda i: (i, 0)),
        out_shape=jax.ShapeDtypeStruct((m, d), F32),
        scratch_shapes=[pltpu.VMEM((tm, f), BF16)],
        compiler_params=pltpu.CompilerParams(dimension_semantics=("parallel",), vmem_limit_bytes=VMEM_LIMIT),
        name="ffn_residual",
    )(x2, g.reshape(1, d).astype(F32), wb[:, :f], wb[:, f:], wo.astype(BF16))
    return out.reshape(lead + (d,))


def _rmsnorm_kernel(x_ref, g_ref, o_ref):
    o_ref[...] = _rms(x_ref[...], g_ref[...])


def rms_norm_rows(x, g):
    lead, d = x.shape[:-1], x.shape[-1]
    x2 = x.reshape(-1, d)
    m = x2.shape[0]
    tm = _pick_tile(m, (512, 256, 128, 64, 32, 16, 8))
    row_block = pl.BlockSpec((tm, d), lambda i: (i, 0))
    out = pl.pallas_call(
        _rmsnorm_kernel, grid=(m // tm,), in_specs=[row_block, _resident((1, d))], out_specs=row_block,
        out_shape=jax.ShapeDtypeStruct((m, d), F32),
        compiler_params=pltpu.CompilerParams(dimension_semantics=("parallel",)), name="rms_norm_rows",
    )(x2, g.reshape(1, d).astype(F32))
    return out.reshape(lead + (d,))


def rope_tables(pos):
    half = HEAD_DIM // 2
    inv = ROPE_THETA ** (-jnp.arange(half, dtype=F32) / half)
    ang = pos.astype(F32)[:, None] * inv[None, :]
    cos, sin = jnp.cos(ang), jnp.sin(ang)
    reps = LANE // HEAD_DIM
    return (jnp.tile(jnp.concatenate([cos, cos], axis=1), (1, reps)),
            jnp.tile(jnp.concatenate([-sin, sin], axis=1), (1, reps)))


def _rope_cols(y, cos, sin):
    w = y.shape[1]
    half = HEAD_DIM // 2
    lane = lax.broadcasted_iota(I32, y.shape, 1)
    partner = jnp.where(lane % HEAD_DIM < half, pltpu.roll(y, w - half, axis=1), pltpu.roll(y, half, axis=1))
    reps = w // LANE
    return y * jnp.tile(cos, (1, reps)) + partner * jnp.tile(sin, (1, reps))


def _proj_kernel(x_ref, g_ref, cos_ref, sin_ref, w_ref, *out_refs, segments):
    xb = _rms(x_ref[...], g_ref[...]).astype(BF16)
    cos, sin = cos_ref[...], sin_ref[...]
    k = 0
    for start, width, rope, scale, outs in segments:
        y = jnp.dot(xb, w_ref[:, start:start + width], preferred_element_type=F32)
        if rope:
            y = _rope_cols(y, cos, sin)
        if scale is not None:
            y = y * scale
        for used, dtype, transposed in outs:
            v = y[:, :used]
            out_refs[k][...] = (v.T if transposed else v).astype(dtype)
            k += 1


def project(x, g, w_cols, segments, pos_rows):
    m, kdim = x.shape
    packed, segs, out_shapes, out_specs = [], [], [], []
    tm = _pick_tile(m, (512, 256, 128, 64, 32, 16, 8))
    start = 0
    for wc, (rope, scale, outs) in zip(w_cols, segments):
        n = wc.shape[1]
        n_pad = -(-n // LANE) * LANE
        packed.append(jnp.pad(wc.astype(BF16), ((0, 0), (0, n_pad - n))))
        segs.append((start, n_pad, rope, scale, tuple((n, dt, tr) for dt, tr in outs)))
        for dt, tr in outs:
            if tr:
                out_shapes.append(jax.ShapeDtypeStruct((n, m), dt))
                out_specs.append(pl.BlockSpec((n, tm), lambda i: (0, i)))
            else:
                out_shapes.append(jax.ShapeDtypeStruct((m, n), dt))
                out_specs.append(pl.BlockSpec((tm, n), lambda i: (i, 0)))
        start += n_pad
    wb = jnp.concatenate(packed, axis=1)
    cos, sin = rope_tables(pos_rows)
    row_block = lambda width: pl.BlockSpec((tm, width), lambda i: (i, 0))
    return pl.pallas_call(
        functools.partial(_proj_kernel, segments=tuple(segs)),
        grid=(m // tm,),
        in_specs=[row_block(kdim), _resident((1, kdim)), row_block(LANE), row_block(LANE), _resident(wb.shape)],
        out_specs=out_specs,
        out_shape=out_shapes,
        compiler_params=pltpu.CompilerParams(dimension_semantics=("parallel",), vmem_limit_bytes=VMEM_LIMIT),
        name="project",
    )(x, g.reshape(1, kdim).astype(F32), cos, sin, wb)


def _split_cols(w, widths):
    cols = np.cumsum((0,) + tuple(widths))
    return [w[:, cols[j]:cols[j + 1]] for j in range(len(widths))]


def _softmax_rows(s):
    m = jnp.max(s, axis=-1, keepdims=True)
    e = jnp.exp(s - jnp.where(m == -jnp.inf, 0.0, m))
    den = jnp.sum(e, axis=-1, keepdims=True)
    return e / jnp.where(den > 0, den, 1.0)


def _flash_step(carry, q, kt_tile, v_tile, sel, rep):
    m, l, acc = carry
    rows, tk = q.shape[0], kt_tile.shape[1]
    s = jnp.dot(q, kt_tile, preferred_element_type=F32)
    s = jnp.where(sel, s.reshape(rep, Q_BLOCK, tk), -jnp.inf).reshape(rows, tk)
    m_new = jnp.maximum(m, jnp.max(s, axis=1, keepdims=True))
    m_safe = jnp.where(m_new == -jnp.inf, 0.0, m_new)
    alpha = jnp.exp(m - m_safe)
    p = jnp.exp(s - m_safe)
    l = l * alpha + jnp.sum(p, axis=1, keepdims=True)
    pv = jnp.dot(p.astype(BF16), v_tile, preferred_element_type=F32)
    return m_new, l, acc * alpha + pv


def _flash_init(rows, hd):
    return (jnp.full((rows, 1), -jnp.inf, F32), jnp.zeros((rows, 1), F32), jnp.zeros((rows, hd), F32))


def _flash_finish(carry):
    _, l, acc = carry
    return acc / jnp.where(l > 0, l, 1.0)


def _stack_heads(x, h0, rep, hd):
    return jnp.concatenate([x[:, (h0 + r) * hd:(h0 + r + 1) * hd] for r in range(rep)], axis=0)


def _sortable_key(score):
    bits = pltpu.bitcast(score, I32)
    key = jnp.where(bits < 0, bits ^ jnp.int32(0x7FFFFFFF), bits)
    return jnp.where(score == 0.0, 0, key)


def _tile_plan(limit, tk):
    half = tk // 2
    n_half = (limit + half - 1) // half
    return n_half // 2, n_half % 2 == 1


def _over_tiles(plan, tk, step, carry):
    n_wide, has_half = plan
    half = tk // 2
    carry = lax.fori_loop(0, n_wide, lambda kt, c: step(c, pl.multiple_of(kt * tk, tk), tk), carry)
    return lax.cond(has_half, lambda c: step(c, pl.multiple_of(n_wide * tk, half), half), lambda c: c, carry)


def _count_lanes(pred_fn, key_ref, plan, tk):
    rows = key_ref.shape[0]

    def step(cnt, off, width):
        keys = key_ref[:, pl.ds(off, width)]
        for c in range(width // LANE):
            hit = pred_fn(keys[:, c * LANE:(c + 1) * LANE], off + c * LANE)
            cnt = cnt + jnp.where(hit, 1, 0)
        return cnt

    cnt = _over_tiles(plan, tk, step, jnp.zeros((rows, LANE), I32))
    return jnp.sum(cnt, axis=1, keepdims=True)


def _top_k_mask_params(key_ref, plan, tk, topk, n_valid):
    rows = key_ref.shape[0]
    lane_iota = lax.broadcasted_iota(I32, (rows, LANE), 1)

    def bit_body(it, thr):
        cand = thr + jnp.left_shift(jnp.int32(1), 31 - it)
        cnt = _count_lanes(lambda keys, c0: keys >= cand, key_ref, plan, tk)
        return jnp.where(cnt >= topk, cand, thr)

    thr = lax.fori_loop(0, 32, bit_body, jnp.full((rows, 1), INT_MIN, I32))
    many = n_valid > topk
    thr = jnp.where(many, thr, INT_MIN + 1)
    c_gt = _count_lanes(lambda keys, c0: keys > thr, key_ref, plan, tk)
    c_eq = _count_lanes(lambda keys, c0: keys == thr, key_ref, plan, tk)
    need = topk - c_gt
    excess = many & (c_eq > need)
    n_col_bits = max(1, (key_ref.shape[1] - 1).bit_length())

    def tie_search():
        def jbody(it, last):
            cand = last + jnp.left_shift(jnp.int32(1), n_col_bits - 1 - it)
            cnt = _count_lanes(lambda keys, c0: (keys == thr) & (c0 + lane_iota < cand), key_ref, plan, tk)
            return jnp.where(cnt <= need - 1, cand, last)

        return lax.fori_loop(0, n_col_bits, jbody, jnp.zeros((rows, 1), I32))

    any_excess = jnp.max(jnp.where(excess, 1, 0)) > 0
    last_tie = lax.cond(any_excess, tie_search, lambda: jnp.zeros((rows, 1), I32))
    return thr, jnp.where(excess, last_tie, jnp.int32(2 ** 30))


SEARCH_DIGIT_BITS = 4


def _top_k_mask_params_small(keys, col, topk, n_valid):
    rows = keys.shape[0]
    count = lambda hit: jnp.sum(jnp.where(hit, 1, 0), axis=1, keepdims=True)
    thr = jnp.full((rows, 1), INT_MIN, I32)
    for shift in range(32 - SEARCH_DIGIT_BITS, -1, -SEARCH_DIGIT_BITS):
        digit = jnp.zeros((rows, 1), I32)
        for d in range(1, 2 ** SEARCH_DIGIT_BITS):
            step = np.int64(d << shift).astype(np.int32)
            digit = digit + jnp.where(count(keys >= thr + jnp.int32(step)) >= topk, 1, 0)
        thr = thr + jnp.left_shift(digit, shift)
    many = n_valid > topk
    thr = jnp.where(many, thr, INT_MIN + 1)
    need = topk - count(keys > thr)
    tie = keys == thr
    excess = many & (count(tie) > need)
    n_col_bits = max(1, (keys.shape[1] - 1).bit_length())

    def tie_search():
        last = jnp.zeros((rows, 1), I32)
        for it in range(n_col_bits):
            cand = last + (1 << (n_col_bits - 1 - it))
            last = jnp.where(count(tie & (col < cand)) <= need - 1, cand, last)
        return last

    any_excess = jnp.max(jnp.where(excess, 1, 0)) > 0
    last_tie = lax.cond(any_excess, tie_search, lambda: jnp.zeros((rows, 1), I32))
    return thr, jnp.where(excess, last_tie, jnp.int32(2 ** 30))


def _block_cover(n_cmp_rows):
    c0 = lax.broadcasted_iota(I32, (n_cmp_rows, LANE), 0) * CMP_STRIDE
    s0 = lax.broadcasted_iota(I32, (n_cmp_rows, LANE), 1) * SLC_BLOCK
    return jnp.where(c0 < s0 + SLC_BLOCK, jnp.where(c0 + CMP_BLOCK > s0, 1.0, 0.0), 0.0).astype(BF16)


def _block_importance(p_sum, cover, row_pos):
    p_hi = p_sum.astype(BF16)
    p_lo = (p_sum - p_hi.astype(F32)).astype(BF16)
    imp = jnp.dot(p_hi, cover, preferred_element_type=F32) + jnp.dot(p_lo, cover, preferred_element_type=F32)
    jb = lax.broadcasted_iota(I32, imp.shape, 1)
    cur = row_pos // SLC_BLOCK
    forced = (jb == 0) | (jb == cur) | (jb == cur - 1)
    imp = jnp.where(forced, jnp.inf, imp)
    return jnp.where(jb <= cur, imp, -jnp.inf)


def _dsa_prompt_kernel(iq_ref, iw_ref, ikt_ref, q_ref, kt_ref, v_ref, o_ref, key_ref, *, topk, tk,
                       n_idx_heads, n_groups, rep):
    i = pl.program_id(1)
    t0 = i * Q_BLOCK
    plan = _tile_plan(t0 + Q_BLOCK, tk)
    row = t0 + lax.broadcasted_iota(I32, (Q_BLOCK, 1), 0)
    hd = kt_ref.shape[0] // n_groups
    di = ikt_ref.shape[0]

    iw = iw_ref[0]
    iq = iq_ref[0]
    iq_heads = [iq[:, h * di:(h + 1) * di] for h in range(n_idx_heads)]

    def score_step(_, off, width):
        acc = jnp.zeros((Q_BLOCK, width), F32)
        for h in range(n_idx_heads):
            r = jnp.dot(iq_heads[h], ikt_ref[:, pl.ds(off, width)], preferred_element_type=F32)
            acc = acc + jnp.maximum(r, 0.0) * iw[:, h:h + 1]
        col = off + lax.broadcasted_iota(I32, (Q_BLOCK, width), 1)
        key_ref[:, pl.ds(off, width)] = jnp.where(col <= row, _sortable_key(acc), INT_MIN)
        return 0

    _over_tiles(plan, tk, score_step, 0)

    thr, last_tie = _top_k_mask_params(key_ref, plan, tk, topk, row + 1)

    q = q_ref[0]
    q_groups = [_stack_heads(q, g * rep, rep, hd) for g in range(n_groups)]
    rows_g = rep * Q_BLOCK

    def attn_step(carry, off, width):
        keys = key_ref[:, pl.ds(off, width)]
        col = off + lax.broadcasted_iota(I32, (Q_BLOCK, width), 1)
        sel = ((keys > thr) | ((keys == thr) & (col <= last_tie)))[None]
        v_t = v_ref[0, pl.ds(off, width), :]
        return tuple(
            _flash_step(carry[g], q_groups[g], kt_ref[g * hd:(g + 1) * hd, pl.ds(off, width)],
                        v_t[:, g * hd:(g + 1) * hd], sel, rep)
            for g in range(n_groups))

    fin = _over_tiles(plan, tk, attn_step, tuple(_flash_init(rows_g, hd) for _ in range(n_groups)))
    for g in range(n_groups):
        o = _flash_finish(fin[g])
        for r in range(rep):
            h = g * rep + r
            o_ref[0, :, h * hd:(h + 1) * hd] = o[r * Q_BLOCK:(r + 1) * Q_BLOCK]


def dsa_prompt_attention(q, kt, v, iq, ikt, iw, topk, n_groups, n_idx_heads, tk=ATTN_COLS):
    b, s, hd_all = q.shape
    gd, di = kt.shape[0], ikt.shape[0]
    d = gd // n_groups
    tk = min(tk, s)
    kern = functools.partial(_dsa_prompt_kernel, topk=topk, tk=tk, n_idx_heads=n_idx_heads, n_groups=n_groups,
                             rep=hd_all // gd)
    q_block = lambda width: pl.BlockSpec((1, Q_BLOCK, width), lambda bb, i: (bb, i, 0))
    return pl.pallas_call(
        kern,
        grid=(b, s // Q_BLOCK),
        in_specs=[
            q_block(n_idx_heads * di), q_block(n_idx_heads),
            pl.BlockSpec((di, s), lambda bb, i: (0, bb)),
            q_block(hd_all),
            pl.BlockSpec((gd, s), lambda bb, i: (0, bb)),
            pl.BlockSpec((1, s, gd), lambda bb, i: (bb, 0, 0)),
        ],
        out_specs=q_block(hd_all),
        out_shape=jax.ShapeDtypeStruct((b, s, hd_all), F32),
        scratch_shapes=[pltpu.VMEM((Q_BLOCK, s), I32)],
        compiler_params=pltpu.CompilerParams(
            dimension_semantics=("parallel", "arbitrary"), vmem_limit_bytes=VMEM_LIMIT),
        name="dsa_prompt_attention",
    )(iq, iw, ikt, q, kt, v)


def _top_blocks(imp_ref, n_live, n_sel):
    imp_t = imp_ref[...]
    j_iota = lax.broadcasted_iota(I32, imp_t.shape, 0)

    def body(k, rank):
        row_k = imp_ref[pl.ds(k, 1), :]
        earlier = jnp.where(j_iota > k, 1.0, 0.0)
        return rank + jnp.where(row_k > imp_t, 1.0, jnp.where(row_k == imp_t, earlier, 0.0))

    rank = lax.fori_loop(0, n_live, body, jnp.zeros(imp_t.shape, F32))
    return jnp.where(rank < n_sel, 1.0, 0.0)


def _nsa_prompt_kernel(q_ref, gate_ref, kct_ref, vc_ref, kst_ref, vs_ref, kwt_ref, vw_ref, o_ref, imp_ref, *,
                       tk, n_cmp, n_blk, n_groups, rep, win_len):
    i = pl.program_id(1)
    t0 = i * Q_BLOCK
    plan = _tile_plan(t0 + Q_BLOCK, tk)
    hd = kst_ref.shape[0] // n_groups
    rows = rep * Q_BLOCK
    nc_pad = kct_ref.shape[-1]
    row = t0 + lax.broadcasted_iota(I32, (Q_BLOCK, 1), 0)
    gates = jax.nn.sigmoid(gate_ref[0])
    n_heads = n_groups * rep
    q_all = q_ref[0]

    cover = _block_cover(nc_pad)
    c_idx = lax.broadcasted_iota(I32, (Q_BLOCK, nc_pad), 1)
    c_ok = ((c_idx < n_cmp) & (c_idx * CMP_STRIDE + (CMP_BLOCK - 1) <= row))[None]
    w_start = pl.multiple_of(jnp.maximum(t0 - WINDOW, 0), Q_BLOCK)
    w_col = w_start + lax.broadcasted_iota(I32, (Q_BLOCK, win_len), 1)
    w_d = row - w_col
    w_ok = ((w_d >= 0) & (w_d < WINDOW))[None]

    for g in range(n_groups):
        q = _stack_heads(q_all, g * rep, rep, hd)
        lo, hi = g * hd, (g + 1) * hd

        s = jnp.dot(q, kct_ref[0, lo:hi, :], preferred_element_type=F32)
        s = jnp.where(c_ok, s.reshape(rep, Q_BLOCK, nc_pad), -jnp.inf)
        p = _softmax_rows(s)
        o_cmp = jnp.dot(p.reshape(rows, nc_pad).astype(BF16), vc_ref[0, :, lo:hi], preferred_element_type=F32)

        imp = _block_importance(jnp.sum(p, axis=0), cover, row)
        imp_ref[...] = imp.T
        n_live = jnp.minimum((t0 + Q_BLOCK - 1) // SLC_BLOCK + 1, n_blk)
        sel_blocks = _top_blocks(imp_ref, n_live, min(N_SLC, n_blk)).T.astype(BF16)

        def slc_step(carry, off, width):
            col = off + lax.broadcasted_iota(I32, (Q_BLOCK, width), 1)
            blk_of_col = (off + lax.broadcasted_iota(I32, (LANE, width), 1)) // SLC_BLOCK
            expand = jnp.where(lax.broadcasted_iota(I32, (LANE, width), 0) == blk_of_col, 1.0, 0.0).astype(BF16)
            hit = jnp.dot(sel_blocks, expand, preferred_element_type=F32)
            sel = (jnp.where(col <= row, hit, 0.0) > 0.5)[None]
            return _flash_step(carry, q, kst_ref[lo:hi, pl.ds(off, width)], vs_ref[0, pl.ds(off, width), lo:hi], sel,
                               rep)

        o_slc = _flash_finish(_over_tiles(plan, tk, slc_step, _flash_init(rows, hd)))

        s = jnp.dot(q, kwt_ref[lo:hi, pl.ds(w_start, win_len)], preferred_element_type=F32)
        s = jnp.where(w_ok, s.reshape(rep, Q_BLOCK, win_len), -jnp.inf)
        p = _softmax_rows(s).reshape(rows, win_len)
        o_win = jnp.dot(p.astype(BF16), vw_ref[0, pl.ds(w_start, win_len), lo:hi], preferred_element_type=F32)

        for r in range(rep):
            h = g * rep + r
            rs = slice(r * Q_BLOCK, (r + 1) * Q_BLOCK)
            o_ref[0, :, h * hd:(h + 1) * hd] = (
                gates[:, h:h + 1] * o_cmp[rs] + gates[:, n_heads + h:n_heads + h + 1] * o_slc[rs]
                + gates[:, 2 * n_heads + h:2 * n_heads + h + 1] * o_win[rs])


def nsa_prompt_attention(q, gate_logits, kct, vc, n_cmp, kst, vs, kwt, vw, n_groups, tk=ATTN_COLS):
    b, s, hd_all = q.shape
    gd, nc_pad = kct.shape[1], kct.shape[2]
    n_blk = -(-s // SLC_BLOCK)
    assert n_blk <= LANE and s % Q_BLOCK == 0
    tk = min(tk, s)
    win_len = min(WINDOW + Q_BLOCK, s)
    kern = functools.partial(_nsa_prompt_kernel, tk=tk, n_cmp=n_cmp, n_blk=n_blk, n_groups=n_groups,
                             rep=hd_all // gd, win_len=win_len)
    per_b = lambda *blk: pl.BlockSpec((1,) + blk, lambda bb, i: (bb,) + (0,) * len(blk))
    keys_t = pl.BlockSpec((gd, s), lambda bb, i: (0, bb))
    q_block = lambda width: pl.BlockSpec((1, Q_BLOCK, width), lambda bb, i: (bb, i, 0))
    return pl.pallas_call(
        kern,
        grid=(b, s // Q_BLOCK),
        in_specs=[q_block(hd_all), q_block(gate_logits.shape[-1]), per_b(gd, nc_pad), per_b(nc_pad, gd),
                  keys_t, per_b(s, gd), keys_t, per_b(s, gd)],
        out_specs=q_block(hd_all),
        out_shape=jax.ShapeDtypeStruct((b, s, hd_all), F32),
        scratch_shapes=[pltpu.VMEM((LANE, Q_BLOCK), F32)],
        compiler_params=pltpu.CompilerParams(
            dimension_semantics=("parallel", "arbitrary"), vmem_limit_bytes=VMEM_LIMIT),
        name="nsa_prompt_attention",
    )(q, gate_logits, kct, vc, kst, vs, kwt, vw)


def _compress_finish(a, pew_ref, w2_ref, n_groups, hd):
    half = n_groups * hd
    first, second = a[:, :half], a[:, half:]
    second = jnp.concatenate([second[1:], jnp.zeros((1, half), F32)], axis=0)
    out = []
    for g in range(n_groups):
        hcol = first[:, g * hd:(g + 1) * hd] + second[:, g * hd:(g + 1) * hd] + pew_ref[...]
        out.append(jnp.dot(jax.nn.silu(hcol).astype(BF16), w2_ref[...], preferred_element_type=F32))
    return out


def _compress_kernel(c_ref, wc_ref, pew_ref, w2_ref, o_ref, *, n_groups, hd, transposed):
    a = jnp.dot(c_ref[0].astype(BF16), wc_ref[...], preferred_element_type=F32)
    res = jnp.concatenate(_compress_finish(a, pew_ref, w2_ref, n_groups, hd), axis=1)
    o_ref[0] = (res.T if transposed else res).astype(BF16)


def _compress_chunk_weights(w1, n_groups, hd):
    wr = w1.reshape(2, CMP_STRIDE, hd, hd)
    eye = jnp.eye(n_groups, dtype=w1.dtype)
    wc = jnp.einsum('hrdo,gk->rgdhko', wr, eye)
    return wc.reshape(CMP_STRIDE * n_groups * hd, 2 * n_groups * hd)


def _compress_weights(pe, w1, w2, n_groups, hd):
    return _compress_chunk_weights(w1, n_groups, hd).astype(BF16), mm(pe.reshape(1, -1), w1), w2.astype(BF16)


def compress_rows(rows, cmp_w, n_groups, transposed):
    b, l, gd = rows.shape
    hd = gd // n_groups
    n_chunks = l // CMP_STRIDE
    cw = CMP_STRIDE * gd
    wc, pew, w2 = _compress_weights(*cmp_w, n_groups, hd)
    out_blk = (gd, n_chunks) if transposed else (n_chunks, gd)
    return pl.pallas_call(
        functools.partial(_compress_kernel, n_groups=n_groups, hd=hd, transposed=transposed),
        grid=(b,),
        in_specs=[pl.BlockSpec((1, n_chunks, cw), lambda bb: (bb, 0, 0)), _resident(wc.shape), _resident(pew.shape),
                  _resident(w2.shape)],
        out_specs=pl.BlockSpec((1,) + out_blk, lambda bb: (bb, 0, 0)),
        out_shape=jax.ShapeDtypeStruct((b,) + out_blk, BF16),
        compiler_params=pltpu.CompilerParams(dimension_semantics=("parallel",), vmem_limit_bytes=VMEM_LIMIT),
        name="compress_rows",
    )(rows.reshape(b, n_chunks, cw), wc, pew, w2)


def _attend_cols(q, kt, vt, sel, rep):
    n = kt.shape[1]
    s = jnp.dot(q, kt, preferred_element_type=F32)
    s = jnp.where(sel, s.reshape(rep, TQ, n), -jnp.inf)
    p = _softmax_rows(s).reshape(rep * TQ, n)
    return p, lax.dot_general(p.astype(BF16), vt, (((1,), (1,)), ((), ())), preferred_element_type=F32)


def _head_rows(x):
    b, t, h, d = x.shape
    x = jnp.pad(x.astype(BF16), ((0, 0), (0, TQ - t), (0, 0), (0, 0)))
    return x.transpose(0, 2, 1, 3).reshape(b, h * TQ, d)


def _new_cols(x, width):
    b, t = x.shape[:2]
    return jnp.pad(x.reshape(b, t, -1).astype(BF16).transpose(0, 2, 1), ((0, 0), (0, 0), (0, width - t)))


def _cols_view(pool):
    lead, rows = pool.shape[:2], pool.shape[2]
    perm = (0, 1) + tuple(range(3, pool.ndim)) + (2,)
    return pool.transpose(perm).reshape(lead + (-1, rows))


def _page_specs(block, layer, n_pages, pp):
    def spec(j):
        return pl.BlockSpec((1, 1) + block,
                            lambda bb, p, pt: (layer, pt[bb * n_pages + p * pp + j]) + (0,) * len(block))
    return [spec(j) for j in range(pp)]


def _per_batch(*blk):
    return pl.BlockSpec((1,) + blk, lambda bb, p, pt: (bb,) + (0,) * len(blk))


def _whole(*blk):
    return pl.BlockSpec(blk, lambda bb, p, pt: (0,) * len(blk))


def _dsa_sample_kernel(pt_ref, iq_ref, iw_ref, q_ref, ikn_ref, kn_ref, vn_ref, *rest, topk, past, n_new,
                       n_idx_heads, n_groups, rep, pp):
    idx_pages = rest[:pp]
    k_pages = rest[pp:2 * pp]
    v_pages = rest[2 * pp:3 * pp]
    o_ref, key_ref, kst, vst = rest[3 * pp:]
    p_step = pl.program_id(1)
    hd = q_ref.shape[-1]
    page = idx_pages[0].shape[-1]
    n_tiles = past // page + 1
    iw = iw_ref[0]
    tpos = past + jnp.minimum(lax.broadcasted_iota(I32, (TQ, 1), 0), n_new - 1)

    def scores(ikt):
        r = jnp.dot(iq_ref[0], ikt, preferred_element_type=F32)
        r = jnp.maximum(r, 0.0).reshape(n_idx_heads, TQ, ikt.shape[1]) * iw
        return jnp.sum(r, axis=0)

    for j in range(pp):
        off = pl.multiple_of((p_step * pp + j) * page, page)
        key_ref[:, pl.ds(off, page)] = _sortable_key(scores(idx_pages[j][0, 0].astype(BF16)))
        kst[:, pl.ds(off, page)] = k_pages[j][0, 0].astype(BF16)
        vst[:, pl.ds(off, page)] = v_pages[j][0, 0].astype(BF16)

    @pl.when(p_step == pl.num_programs(1) - 1)
    def _():
        c = lax.broadcasted_iota(I32, (TQ, page), 1)
        fresh_ok = (c < n_new) & (past + c <= tpos)
        key_ref[:, past:past + page] = jnp.where(fresh_ok, _sortable_key(scores(ikn_ref[0])), INT_MIN)
        kst[:, past:past + page] = kn_ref[0]
        vst[:, past:past + page] = vn_ref[0]
        keys = key_ref[...]
        col = lax.broadcasted_iota(I32, keys.shape, 1)
        thr, last_tie = _top_k_mask_params_small(keys, col, topk, tpos + 1)
        sel = ((keys > thr) | ((keys == thr) & (col <= last_tie)))[None]
        for g in range(n_groups):
            gs = slice(g * hd, (g + 1) * hd)
            _, o = _attend_cols(q_ref[0, g * rep * TQ:(g + 1) * rep * TQ], kst[gs, :], vst[gs, :], sel, rep)
            for r in range(rep):
                h = g * rep + r
                o_ref[0, :, h * hd:(h + 1) * hd] = o[r * TQ:(r + 1) * TQ]


def dsa_sample_attention(q, k, v, iq, ik, iw, pools_k, pools_v, pools_idx, layer, page_table, topk):
    b, t, h, d = q.shape
    g = k.shape[2]
    hi, di = iq.shape[2], iq.shape[3]
    n_pages = page_table.shape[1]
    page = pools_k.shape[2]
    past = n_pages * page
    pp = min(PAGES_PER_STEP, n_pages)
    assert n_pages % pp == 0 and t <= TQ
    iw_h = jnp.pad(iw.astype(F32), ((0, 0), (0, TQ - t), (0, 0))).transpose(0, 2, 1)[..., None]
    kern = functools.partial(_dsa_sample_kernel, topk=topk, past=past, n_new=t, n_idx_heads=hi, n_groups=g,
                             rep=h // g, pp=pp)
    o = pl.pallas_call(
        kern,
        grid_spec=pltpu.PrefetchScalarGridSpec(
            num_scalar_prefetch=1,
            grid=(b, n_pages // pp),
            in_specs=[_per_batch(hi * TQ, di), _per_batch(hi, TQ, 1), _per_batch(h * TQ, d), _per_batch(di, page),
                      _per_batch(g * d, page), _per_batch(g * d, page)]
                     + _page_specs((di, page), layer, n_pages, pp)
                     + _page_specs((g * d, page), layer, n_pages, pp)
                     + _page_specs((g * d, page), layer, n_pages, pp),
            out_specs=_per_batch(TQ, h * d),
            scratch_shapes=[pltpu.VMEM((TQ, past + page), I32), pltpu.VMEM((g * d, past + page), BF16),
                            pltpu.VMEM((g * d, past + page), BF16)],
        ),
        out_shape=jax.ShapeDtypeStruct((b, TQ, h * d), F32),
        compiler_params=pltpu.CompilerParams(
            dimension_semantics=("parallel", "arbitrary"), vmem_limit_bytes=VMEM_LIMIT),
        name="dsa_sample_attention",
    )(page_table.reshape(-1).astype(I32), _head_rows(iq), iw_h, _head_rows(q), _new_cols(ik, page),
      _new_cols(k, page), _new_cols(v, page),
      *([_cols_view(pools_idx)] * pp), *([_cols_view(pools_k)] * pp), *([_cols_view(pools_v)] * pp))
    return o[:, :t]


def _top_blocks_rows(imp, n_blk, n_sel):
    j_iota = lax.broadcasted_iota(I32, imp.shape, 1)
    rank = jnp.zeros(imp.shape, F32)
    for k in range(n_blk):
        col_k = imp[:, k:k + 1]
        earlier = jnp.where(j_iota > k, 1.0, 0.0)
        rank = rank + jnp.where(col_k > imp, 1.0, jnp.where(col_k == imp, earlier, 0.0))
    return jnp.where((rank < n_sel) & (j_iota < n_blk), 1.0, 0.0)


def _nsa_sample_kernel(pt_ref, q_ref, gate_ref, ksn_ref, vsn_ref, kwn_ref, vwn_ref, wink_ref, winv_ref,
                       wck_ref, wcv_ref, pewk_ref, pewv_ref, w2k_ref, w2v_ref, *rest, past, n_new, n_groups, rep, pp):
    kc_pages = rest[:pp]
    vc_pages = rest[pp:2 * pp]
    ks_pages = rest[2 * pp:3 * pp]
    vs_pages = rest[3 * pp:4 * pp]
    o_ref, kcs, vcs, kst, vst, kwt, vwt = rest[4 * pp:]
    p_step = pl.program_id(1)
    hd = q_ref.shape[-1]
    gd = n_groups * hd
    page = ks_pages[0].shape[-1]
    n_chunks = kcs.shape[0] // CMP_STRIDE
    n_cmp = n_chunks - CMP_BLOCK // CMP_STRIDE + 1
    n_blk = -(-(past + n_new) // SLC_BLOCK)
    w_buf = wink_ref.shape[-1]
    n_heads = n_groups * rep
    rows = rep * TQ

    for j in range(pp):
        off = pl.multiple_of((p_step * pp + j) * page, page)
        kcs[pl.ds(off, page), :] = kc_pages[j][0, 0].T
        vcs[pl.ds(off, page), :] = vc_pages[j][0, 0].T
        kst[:, pl.ds(off, page)] = ks_pages[j][0, 0].astype(BF16)
        vst[:, pl.ds(off, page)] = vs_pages[j][0, 0].astype(BF16)

    @pl.when(p_step == pl.num_programs(1) - 1)
    def _():
        kst[:, past:past + page] = ksn_ref[0]
        vst[:, past:past + page] = vsn_ref[0]
        kwt[:, 0:w_buf] = wink_ref[0, 0].astype(BF16)
        vwt[:, 0:w_buf] = winv_ref[0, 0].astype(BF16)
        kwt[:, w_buf:w_buf + LANE] = kwn_ref[0]
        vwt[:, w_buf:w_buf + LANE] = vwn_ref[0]

        tpos = past + jnp.minimum(lax.broadcasted_iota(I32, (TQ, 1), 0), n_new - 1)
        gates = jax.nn.sigmoid(gate_ref[0])

        def compress(rows_ref, wc_ref, pew_ref, w2_ref):
            a = jnp.zeros((n_chunks, 2 * gd), F32)
            for r in range(CMP_STRIDE):
                xr = rows_ref[pl.ds(r, n_chunks, stride=CMP_STRIDE), :].astype(BF16)
                a = a + jnp.dot(xr, wc_ref[r * gd:(r + 1) * gd, :], preferred_element_type=F32)
            out = _compress_finish(a, pew_ref, w2_ref, n_groups, hd)
            return [o.astype(BF16) for o in out]

        kcmp = compress(kcs, wck_ref, pewk_ref, w2k_ref)
        vcmp = compress(vcs, wcv_ref, pewv_ref, w2v_ref)

        c_idx = lax.broadcasted_iota(I32, (TQ, n_chunks), 1)
        c_ok = ((c_idx < n_cmp) & (c_idx * CMP_STRIDE + (CMP_BLOCK - 1) <= tpos))[None]
        cover = _block_cover(n_chunks)
        n_slc = kst.shape[1]
        col = lax.broadcasted_iota(I32, (TQ, n_slc), 1)
        blk_of_col = lax.broadcasted_iota(I32, (LANE, n_slc), 1) // SLC_BLOCK
        expand = jnp.where(lax.broadcasted_iota(I32, (LANE, n_slc), 0) == blk_of_col, 1.0, 0.0).astype(BF16)
        n_win = kwt.shape[1]
        w_c = lax.broadcasted_iota(I32, (TQ, n_win), 1)
        w_pos = past - w_buf + w_c
        w_d = tpos - w_pos
        w_ok = ((w_c < w_buf + n_new) & (w_pos >= 0) & (w_d >= 0) & (w_d < WINDOW))[None]

        for g in range(n_groups):
            q = q_ref[0, g * rows:(g + 1) * rows]
            gs = slice(g * hd, (g + 1) * hd)
            s = lax.dot_general(q, kcmp[g], (((1,), (1,)), ((), ())), preferred_element_type=F32)
            p_cmp = _softmax_rows(jnp.where(c_ok, s.reshape(rep, TQ, n_chunks), -jnp.inf))
            o_cmp = jnp.dot(p_cmp.reshape(rows, n_chunks).astype(BF16), vcmp[g], preferred_element_type=F32)
            imp = _block_importance(jnp.sum(p_cmp, axis=0), cover, tpos)
            sel_blocks = _top_blocks_rows(imp, n_blk, min(N_SLC, n_blk)).astype(BF16)
            hit = jnp.dot(sel_blocks, expand, preferred_element_type=F32)
            sel = (jnp.where(col <= tpos, hit, 0.0) > 0.5)[None]
            _, o_slc = _attend_cols(q, kst[gs, :], vst[gs, :], sel, rep)
            _, o_win = _attend_cols(q, kwt[gs, :], vwt[gs, :], w_ok, rep)
            for r in range(rep):
                h = g * rep + r
                rs = slice(r * TQ, (r + 1) * TQ)
                o_ref[0, :, h * hd:(h + 1) * hd] = (
                    gates[:, h:h + 1] * o_cmp[rs] + gates[:, n_heads + h:n_heads + h + 1] * o_slc[rs]
                    + gates[:, 2 * n_heads + h:2 * n_heads + h + 1] * o_win[rs])


def nsa_sample_attention(q, gate_logits, ks, vs, kw, vw, cmp_k, cmp_v, pools_kc, pools_vc, pools_ks, pools_vs,
                         wins_k, wins_v, layer, page_table):
    b, t, h, d = q.shape
    g = ks.shape[2]
    gd = g * d
    n_pages = page_table.shape[1]
    page = pools_ks.shape[2]
    past = n_pages * page
    pp = min(PAGES_PER_STEP, n_pages)
    assert n_pages % pp == 0 and t <= TQ and (past + t) // CMP_STRIDE * CMP_STRIDE == past
    cw = CMP_STRIDE * gd
    w_buf = wins_k.shape[2]

    wck, pewk, w2k = _compress_weights(*cmp_k, g, d)
    wcv, pewv, w2v = _compress_weights(*cmp_v, g, d)
    kern = functools.partial(_nsa_sample_kernel, past=past, n_new=t, n_groups=g, rep=h // g, pp=pp)
    win_spec = pl.BlockSpec((1, 1, gd, w_buf), lambda bb, p, pt: (layer, bb, 0, 0))
    o = pl.pallas_call(
        kern,
        grid_spec=pltpu.PrefetchScalarGridSpec(
            num_scalar_prefetch=1,
            grid=(b, n_pages // pp),
            in_specs=[_per_batch(h * TQ, d), _per_batch(TQ, 3 * h), _per_batch(gd, page), _per_batch(gd, page),
                      _per_batch(gd, LANE), _per_batch(gd, LANE), win_spec, win_spec,
                      _whole(cw, 2 * gd), _whole(cw, 2 * gd), _whole(1, d), _whole(1, d), _whole(d, d), _whole(d, d)]
                     + _page_specs((gd, page), layer, n_pages, pp) + _page_specs((gd, page), layer, n_pages, pp)
                     + _page_specs((gd, page), layer, n_pages, pp) + _page_specs((gd, page), layer, n_pages, pp),
            out_specs=_per_batch(TQ, h * d),
            scratch_shapes=[pltpu.VMEM((past, gd), F32), pltpu.VMEM((past, gd), F32),
                            pltpu.VMEM((gd, past + page), BF16), pltpu.VMEM((gd, past + page), BF16),
                            pltpu.VMEM((gd, w_buf + LANE), BF16), pltpu.VMEM((gd, w_buf + LANE), BF16)],
        ),
        out_shape=jax.ShapeDtypeStruct((b, TQ, h * d), F32),
        compiler_params=pltpu.CompilerParams(
            dimension_semantics=("parallel", "arbitrary"), vmem_limit_bytes=VMEM_LIMIT),
        name="nsa_sample_attention",
    )(page_table.reshape(-1).astype(I32), _head_rows(q),
      jnp.pad(gate_logits.astype(F32), ((0, 0), (0, TQ - t), (0, 0))),
      _new_cols(ks, page), _new_cols(vs, page), _new_cols(kw, LANE), _new_cols(vw, LANE),
      _cols_view(wins_k), _cols_view(wins_v), wck, wcv, pewk, pewv, w2k, w2v,
      *([_cols_view(pools_kc)] * pp), *([_cols_view(pools_vc)] * pp),
      *([_cols_view(pools_ks)] * pp), *([_cols_view(pools_vs)] * pp))
    return o[:, :t]


def last_rows(a, n):
    t = a.shape[1]
    if t >= n:
        return a[:, t - n:]
    return jnp.pad(a, ((0, 0), (n - t, 0)) + ((0, 0),) * (a.ndim - 2))


def mem_kv(mem, g, w_kv):
    b, m, _ = mem.shape
    k, v = jnp.split(mm(mem, w_kv, g), 2, axis=-1)
    return k.reshape(b, m, X_HEADS, X_HEAD_DIM), v.reshape(b, m, X_HEADS, X_HEAD_DIM)


def _xattn_kernel(x_ref, g_ref, wq_ref, mk_ref, mv_ref, wo_ref, o_ref, *, n_heads, hd):
    xn = _rms(x_ref[...], g_ref[...]).astype(BF16)
    q = jnp.dot(xn, wq_ref[...], preferred_element_type=F32).astype(BF16)
    mk = mk_ref[0].astype(BF16)
    mv = mv_ref[0].astype(BF16)
    outs = []
    for h in range(n_heads):
        hs = slice(h * hd, (h + 1) * hd)
        s = lax.dot_general(q[:, hs], mk[:, hs], (((1,), (1,)), ((), ())), preferred_element_type=F32) * (hd ** -0.5)
        e = jnp.exp(s - jnp.max(s, axis=-1, keepdims=True))
        p = e / jnp.sum(e, axis=-1, keepdims=True)
        outs.append(jnp.dot(p.astype(BF16), mv[:, hs], preferred_element_type=F32).astype(BF16))
    a = jnp.concatenate(outs, axis=1)
    for c0 in range(0, o_ref.shape[1], MM_COLS):
        cs = slice(c0, c0 + MM_COLS)
        o_ref[:, cs] = x_ref[:, cs] + jnp.dot(a, wo_ref[:, cs], preferred_element_type=F32)


def cross_attn_rows(x, g, mk, mv, w_q, w_o):
    b, t, d = x.shape
    mlen, n_heads, hd = mk.shape[1:]
    hw = n_heads * hd
    tm = _pick_tile(t, (512, 256, 128, 64, 32, 16, 8))
    steps = t // tm
    rows = pl.BlockSpec((tm, d), lambda i: (i, 0))
    mem = pl.BlockSpec((1, mlen, hw), lambda i: (i // steps, 0, 0))
    out = pl.pallas_call(
        functools.partial(_xattn_kernel, n_heads=n_heads, hd=hd),
        grid=(b * steps,),
        in_specs=[rows, _resident((1, d)), _resident((d, hw)), mem, mem, _resident((hw, d))],
        out_specs=rows,
        out_shape=jax.ShapeDtypeStruct((b * t, d), F32),
        compiler_params=pltpu.CompilerParams(dimension_semantics=("parallel",), vmem_limit_bytes=VMEM_LIMIT),
        name="cross_attn_rows",
    )(x.reshape(b * t, d), g.reshape(1, d).astype(F32), w_q.astype(BF16), mk.reshape(b, mlen, hw),
      mv.reshape(b, mlen, hw), w_o.astype(BF16))
    return out.reshape(b, t, d)


def cross_attn(x, g, mk, mv, w_q, w_o):
    b, t, _ = x.shape
    if t % SUBLANE == 0:
        return cross_attn_rows(x, g, mk, mv, w_q, w_o)
    q = mm(x, w_q, g).reshape(b, t, X_HEADS, X_HEAD_DIM)
    s = jnp.einsum('bthd,bmhd->bhtm', q, mk).astype(F32) * (X_HEAD_DIM ** -0.5)
    p = jax.nn.softmax(s, axis=-1).astype(x.dtype)
    return mm(jnp.einsum('bhtm,bmhd->bthd', p, mv).reshape(b, t, -1), w_o, res=x)


A_WIDTHS = (A_HEADS * HEAD_DIM, A_KV_HEADS * HEAD_DIM, A_KV_HEADS * HEAD_DIM, IDX_HEADS * IDX_DIM, IDX_DIM, IDX_HEADS)


def dsa_project(x, g, w_in, pos, keys_transposed):
    b, t, d = x.shape
    keys = [(F32, False), (BF16, keys_transposed)]
    segments = [(True, HEAD_DIM ** -0.5, [(BF16, False)]), (True, None, keys), (False, None, [(F32, False), (BF16, False)]),
                (True, None, [(BF16, False)]), (True, None, keys),
                (False, IDX_HEADS ** -0.5 * IDX_DIM ** -0.5, [(F32, False)])]
    return project(x.reshape(b * t, d), g, _split_cols(w_in, A_WIDTHS), segments, jnp.tile(pos, b))


def dsa_prompt(x, g, w_in, w_out):
    b, s, _ = x.shape
    q, k, kt, v, vb, iq, ik, ikt, iw = dsa_project(x, g, w_in, jnp.arange(s), True)
    o = dsa_prompt_attention(q.reshape(b, s, -1), kt, vb.reshape(b, s, -1), iq.reshape(b, s, -1), ikt,
                             iw.reshape(b, s, -1), min(TOPK_MAX, s // 4), A_KV_HEADS, IDX_HEADS)
    state = (k.reshape(b, s, A_KV_HEADS, HEAD_DIM), v.reshape(b, s, A_KV_HEADS, HEAD_DIM), ik.reshape(b, s, IDX_DIM))
    return mm(o, w_out, res=x), state


def dsa_sample(x, g, w_in, w_out, pools_k, pools_v, pools_idx, layer, page_table):
    b, t, _ = x.shape
    past = page_table.shape[1] * pools_k.shape[2]
    q, k, kb, v, vb, iq, ik, ikb, iw = dsa_project(x, g, w_in, past + jnp.arange(t), False)
    o = dsa_sample_attention(q.reshape(b, t, A_HEADS, HEAD_DIM), kb.reshape(b, t, A_KV_HEADS, HEAD_DIM),
                             vb.reshape(b, t, A_KV_HEADS, HEAD_DIM), iq.reshape(b, t, IDX_HEADS, IDX_DIM),
                             ikb.reshape(b, t, IDX_DIM), iw.reshape(b, t, IDX_HEADS), pools_k, pools_v, pools_idx,
                             layer, page_table, min(TOPK_MAX, (past + t) // 4))
    state = (k.reshape(b, t, A_KV_HEADS, HEAD_DIM), v.reshape(b, t, A_KV_HEADS, HEAD_DIM), ik.reshape(b, t, IDX_DIM))
    return mm(o, w_out, res=x), state


def _shift_rows(x, prev, k):
    return jnp.concatenate([prev[prev.shape[0] - k:], x[:x.shape[0] - k]], axis=0)


def _tile_transpose(m, rows_out, cols_out):
    r, c = m.shape
    tile = m
    if c < LANE:
        tile = jnp.concatenate([tile, jnp.zeros((r, LANE - c), m.dtype)], axis=1)
    if r < LANE:
        tile = jnp.concatenate([tile, jnp.zeros((LANE - r, LANE), m.dtype)], axis=0)
    return tile.T[:rows_out, :cols_out]


def _ssd_kernel(z_ref, x_ref, dt_ref, cw_ref, cb_ref, dtb_ref, alog_ref, dskip_ref, buf_ref, h0_ref,
                y_ref, hout_ref, carry_ref, ht_ref, *, n_heads, n_groups, hd, n_state):
    c = pl.program_id(1)
    l = x_ref.shape[0]
    d_inner = n_heads * hd
    rep = n_heads // n_groups
    n_pairs = n_heads // 2

    @pl.when(c == 0)
    def _():
        carry_ref[...] = buf_ref[0]
        for p in range(n_pairs):
            ht_ref[p] = jnp.concatenate([_tile_transpose(h0_ref[0, 2 * p], n_state, hd),
                                         _tile_transpose(h0_ref[0, 2 * p + 1], n_state, hd)], axis=1)

    x = x_ref[...]
    prev = carry_ref[...]
    conv = cb_ref[...] + x * cw_ref[B_CONV - 1:B_CONV, :]
    for k in range(1, B_CONV):
        conv = conv + _shift_rows(x, prev, k) * cw_ref[B_CONV - 1 - k:B_CONV - k, :]
    carry_ref[...] = x[l - SUBLANE:]
    act = jax.nn.silu(conv)
    gn = n_groups * n_state
    xs, bm, cm = act[:, :d_inner], act[:, d_inner:d_inner + gn], act[:, d_inner + gn:]

    dt = jax.nn.softplus(dt_ref[...] + dtb_ref[...])
    a = -jnp.exp(alog_ref[...])
    acs = dt * a
    k = 1
    while k < l:
        acs = acs + jnp.concatenate([jnp.zeros((k, n_heads), F32), acs[:l - k]], axis=0)
        k *= 2
    acs_t = jnp.concatenate([acs, jnp.zeros((l, LANE - n_heads), F32)], axis=1).T

    spread = jnp.where(lax.broadcasted_iota(I32, (n_heads, d_inner), 1) // hd
                       == lax.broadcasted_iota(I32, (n_heads, d_inner), 0), 1.0, 0.0).astype(BF16)

    def per_column(v):
        out = jnp.zeros((l, d_inner), F32)
        for _ in range(3):
            part = v.astype(BF16)
            out = out + jnp.dot(part, spread, preferred_element_type=F32)
            v = v - part.astype(F32)
        return out

    dt_cols = per_column(dt)
    acs_cols = per_column(acs)
    last_cols = acs_cols[l - 1:l, :]
    xr_all = xs * dt_cols
    xw_all = (xr_all * jnp.exp(last_cols - acs_cols)).astype(BF16)
    grow_cols = jnp.exp(acs_cols)
    decay_cols = jnp.exp(last_cols)
    y_skip = xs * dskip_ref[...]
    gate = jax.nn.silu(z_ref[...])
    causal = lax.broadcasted_iota(I32, (l, l), 0) >= lax.broadcasted_iota(I32, (l, l), 1)
    first_head = lax.broadcasted_iota(I32, (l, LANE), 1) < hd

    for g in range(n_groups):
        bm_g = bm[:, g * n_state:(g + 1) * n_state]
        cm_g = cm[:, g * n_state:(g + 1) * n_state].astype(BF16)
        cb = lax.dot_general(cm_g, bm_g.astype(BF16), (((1,), (1,)), ((), ())), preferred_element_type=F32)
        bm_t = bm_g.T.astype(BF16)
        for p in range(g * rep // 2, (g + 1) * rep // 2):
            ps = slice(p * LANE, (p + 1) * LANE)
            xr = xr_all[:, ps]
            lhs = []
            for h in (2 * p, 2 * p + 1):
                seg = acs[:, h:h + 1] - acs_t[h:h + 1, :]
                lhs.append(cb * jnp.where(causal, jnp.exp(jnp.where(causal, seg, 0.0)), 0.0))
            rhs = jnp.concatenate([jnp.where(first_head, xr, 0.0), jnp.where(first_head, 0.0, xr)], axis=0)
            y = jnp.dot(jnp.concatenate(lhs, axis=1).astype(BF16), rhs.astype(BF16), preferred_element_type=F32)
            ht = ht_ref[p]
            y = y + jnp.dot(cm_g, ht.astype(BF16), preferred_element_type=F32) * grow_cols[:, ps]
            ht_ref[p] = ht * decay_cols[:, ps] + jnp.dot(bm_t, xw_all[:, ps], preferred_element_type=F32)
            y_ref[:, ps] = (y + y_skip[:, ps]) * gate[:, ps]

    @pl.when(c == pl.num_programs(1) - 1)
    def _():
        for p in range(n_pairs):
            ht = ht_ref[p]
            hout_ref[0, 2 * p] = _tile_transpose(ht[:, :hd], hd, n_state)
            hout_ref[0, 2 * p + 1] = _tile_transpose(ht[:, hd:], hd, n_state)


def ssd_prompt(z, xbc, dt, conv_w, conv_b, dt_bias, a_log, d_skip, conv_buf, ssm0, n_groups):
    b, n_heads, hd, n_state = ssm0.shape
    m, w = xbc.shape
    s = m // b
    l = min(B_CHUNK, s)
    nc = s // l
    d_inner = n_heads * hd
    buf = jnp.pad(conv_buf.astype(F32), ((0, 0), (SUBLANE - (B_CONV - 1), 0), (0, 0)))
    dskip_full = jnp.repeat(d_skip.astype(F32), hd).reshape(1, d_inner)
    rows = lambda width: pl.BlockSpec((l, width), lambda bb, c: (bb * nc + c, 0))
    state = pl.BlockSpec((1, n_heads, hd, n_state), lambda bb, c: (bb, 0, 0, 0))
    kern = functools.partial(_ssd_kernel, n_heads=n_heads, n_groups=n_groups, hd=hd, n_state=n_state)
    return pl.pallas_call(
        kern,
        grid=(b, nc),
        in_specs=[rows(d_inner), rows(w), rows(n_heads), _resident((B_CONV, w)), _resident((1, w)),
                  _resident((1, n_heads)), _resident((1, n_heads)), _resident((1, d_inner)),
                  pl.BlockSpec((1, SUBLANE, w), lambda bb, c: (bb, 0, 0)), state],
        out_specs=[rows(d_inner), state],
        out_shape=[jax.ShapeDtypeStruct((m, d_inner), F32), jax.ShapeDtypeStruct(ssm0.shape, F32)],
        scratch_shapes=[pltpu.VMEM((SUBLANE, w), F32), pltpu.VMEM((n_heads // 2, n_state, 2 * hd), F32)],
        compiler_params=pltpu.CompilerParams(
            dimension_semantics=("parallel", "arbitrary"), vmem_limit_bytes=VMEM_LIMIT),
        name="ssd_prompt",
    )(z, xbc, dt, conv_w.astype(F32), conv_b.reshape(1, w).astype(F32), dt_bias.reshape(1, n_heads).astype(F32),
      a_log.reshape(1, n_heads).astype(F32), dskip_full, buf, ssm0.astype(F32))


def mamba_prompt(x, g, w_in, conv_w, conv_b, dt_bias, a_log, d_skip, g_norm, w_out, conv_buf, ssm0):
    b, s, d = x.shape
    plain = (False, None, [(F32, False)])
    z, xbc, dt = project(x.reshape(b * s, d), g, _split_cols(w_in, (B_D_INNER, B_CONV_DIM, B_HEADS)),
                         [plain, plain, plain], jnp.zeros((b * s,), I32))
    y, ssm = ssd_prompt(z, xbc, dt, conv_w, conv_b, dt_bias, a_log, d_skip, conv_buf, ssm0, B_GROUPS)
    xpad = jnp.concatenate([conv_buf.astype(F32), xbc.reshape(b, s, -1)[:, s - (B_CONV - 1):]], axis=1)
    new_buf = xpad[:, xpad.shape[1] - (B_CONV - 1):]
    return mm(y.reshape(b, s, -1), w_out, g_norm, res=x), new_buf, ssm.astype(ssm0.dtype)


def ssd_chunked(x, dt, a, bm, cm, h0):
    bsz, t, nh, p = x.shape
    g, n = bm.shape[2], bm.shape[3]
    r = nh // g
    l = min(B_CHUNK, t)
    nc = t // l
    xr = (x * dt[..., None]).reshape(bsz, nc, l, g, r, p)
    acs = jnp.cumsum((dt * a).reshape(bsz, nc, l, g, r), axis=2)
    br = bm.reshape(bsz, nc, l, g, n)
    cr = cm.reshape(bsz, nc, l, g, n)
    seg = acs[:, :, :, None] - acs[:, :, None, :]
    causal = jnp.tril(jnp.ones((l, l), bool))[:, :, None, None]
    decay = jnp.where(causal, jnp.exp(jnp.where(causal, seg, 0.0)), 0.0)
    cb = jnp.einsum('bcign,bcjgn->bcijg', cr, br)
    y_intra = jnp.einsum('bcijg,bcijgr,bcjgrp->bcigrp', cb, decay, xr)
    to_end = jnp.exp(acs[:, :, -1:] - acs)
    s_chunk = jnp.einsum('bclgn,bclgr,bclgrp->bcgrpn', br, to_end, xr)
    d_chunk = jnp.exp(acs[:, :, -1])

    def step(hc, inp):
        s_c, d_c = inp
        return hc * d_c[..., None, None] + s_c, hc

    h_fin, h_in = lax.scan(step, h0.reshape(bsz, g, r, p, n),
                           (jnp.moveaxis(s_chunk, 1, 0), jnp.moveaxis(d_chunk, 1, 0)))
    h_in = jnp.moveaxis(h_in, 0, 1)
    y_inter = jnp.einsum('bcign,bcigr,bcgrpn->bcigrp', cr, jnp.exp(acs), h_in)
    return (y_intra + y_inter).reshape(bsz, t, nh, p), h_fin.reshape(bsz, nh, p, n)


def mamba_mixer(x, g, w_in, conv_w, conv_b, dt_bias, a_log, d_skip, g_norm, w_out, conv_buf, ssm0):
    b, t, _ = x.shape
    proj = mm(x, w_in, g)
    z = proj[..., :B_D_INNER]
    xbc = proj[..., B_D_INNER:B_D_INNER + B_CONV_DIM]
    dt = proj[..., B_D_INNER + B_CONV_DIM:]
    xpad = jnp.concatenate([conv_buf.astype(xbc.dtype), xbc], axis=1)
    conv = conv_b + sum(xpad[:, j:j + t] * conv_w[j] for j in range(B_CONV))
    xbc = jax.nn.silu(conv)
    new_buf = xpad[:, t:]
    gn = B_GROUPS * B_STATE
    xs = xbc[..., :B_D_INNER].reshape(b, t, B_HEADS, B_HEADDIM).astype(F32)
    bm = xbc[..., B_D_INNER:B_D_INNER + gn].reshape(b, t, B_GROUPS, B_STATE).astype(F32)
    cm = xbc[..., B_D_INNER + gn:].reshape(b, t, B_GROUPS, B_STATE).astype(F32)
    dt = jax.nn.softplus(dt.astype(F32) + dt_bias.astype(F32))
    a = -jnp.exp(a_log.astype(F32))
    y, ssm = ssd_chunked(xs, dt, a, bm, cm, ssm0.astype(F32))
    y = (y + xs * d_skip.astype(F32)[:, None]).reshape(b, t, B_D_INNER).astype(x.dtype)
    y = y * jax.nn.silu(z)
    return mm(y, w_out, g_norm, res=x), new_buf, ssm.astype(ssm0.dtype)


C_KV_WIDTH = C_KV_HEADS * HEAD_DIM
C_WIDTHS = (C_HEADS * HEAD_DIM,) + (C_KV_WIDTH,) * 6 + (3 * C_HEADS,)


def nsa_project(x, g, w_in, pos, keys_transposed):
    b, t, d = x.shape
    keys = (True, None, [(F32, False), (BF16, keys_transposed)])
    vals = (False, None, [(F32, False), (BF16, False)])
    segments = [(True, HEAD_DIM ** -0.5, [(BF16, False)]), (True, None, [(F32, False)]), (False, None, [(F32, False)]),
                keys, vals, keys, vals, (False, None, [(F32, False)])]
    return project(x.reshape(b * t, d), g, _split_cols(w_in, C_WIDTHS), segments, jnp.tile(pos, b))


def nsa_prompt(x, g, w_in, w_out, cmp_k, cmp_v, w_buf):
    b, s, _ = x.shape
    q, kc, vc, ks, kst, vs, vsb, kw, kwt, vw, vwb, gates = nsa_project(x, g, w_in, jnp.arange(s), True)
    seq = lambda a: a.reshape(b, s, -1)
    kct = compress_rows(seq(kc), cmp_k, C_KV_HEADS, True)
    vcc = compress_rows(seq(vc), cmp_v, C_KV_HEADS, False)
    n_cmp = s // CMP_STRIDE - CMP_BLOCK // CMP_STRIDE + 1
    o = nsa_prompt_attention(seq(q), seq(gates), kct, vcc, n_cmp, kst, seq(vsb), kwt, seq(vwb), C_KV_HEADS)
    heads = lambda a: a.reshape(b, s, C_KV_HEADS, HEAD_DIM)
    state = (heads(kc), heads(vc), heads(ks), heads(vs), last_rows(heads(kw), w_buf), last_rows(heads(vw), w_buf))
    return mm(o, w_out, res=x), state


def nsa_sample(x, g, w_in, w_out, cmp_k, cmp_v, pools_kc, pools_vc, pools_ks, pools_vs, wins_k, wins_v, layer,
               page_table):
    b, t, _ = x.shape
    past = page_table.shape[1] * pools_kc.shape[2]
    q, kc, vc, ks, ksb, vs, vsb, kw, kwb, vw, vwb, gates = nsa_project(x, g, w_in, past + jnp.arange(t), False)
    heads = lambda a: a.reshape(b, t, C_KV_HEADS, HEAD_DIM)
    o = nsa_sample_attention(q.reshape(b, t, C_HEADS, HEAD_DIM), gates.reshape(b, t, -1), heads(ksb), heads(vsb),
                             heads(kwb), heads(vwb), cmp_k, cmp_v, pools_kc, pools_vc, pools_ks, pools_vs, wins_k,
                             wins_v, layer, page_table)
    win_k, win_v = wins_k[layer], wins_v[layer]
    w_buf = win_k.shape[1]
    kwin = jnp.concatenate([win_k, heads(kw)], axis=1)
    vwin = jnp.concatenate([win_v, heads(vw)], axis=1)
    state = (heads(kc), heads(vc), heads(ks), heads(vs), last_rows(kwin, w_buf), last_rows(vwin, w_buf))
    return mm(o, w_out, res=x), state


def kernel(x_prompt, x_sample, cache_a_k, cache_a_v, cache_a_idx, state_b_ssm, state_b_conv, cache_c_cmp_k, cache_c_cmp_v, cache_c_slc_k, cache_c_slc_v, cache_c_win_k, cache_c_win_v, cache_mem_k, cache_mem_v, page_table, mem_prompt, g_ffn1, ffn1_wi, ffn1_wo, g_mix, g_xattn, g_mem, x_w_q, x_w_kv, x_w_o, g_ffn2, ffn2_wi, ffn2_wo, g_final, a_w_in, a_w_out, b_w_in, b_conv_w, b_conv_b, b_dt_bias, b_a_log, b_d_skip, b_g_norm, b_w_out, c_w_in, c_w_out, c_pe_k, c_w1_k, c_w2_k, c_pe_v, c_w1_v, c_w2_v):
    xp, xs = x_prompt, x_sample
    w_buf = cache_c_win_k.shape[2]
    ak_p, av_p, ai_p, bs_p, bc_p = [], [], [], [], []
    cck_p, ccv_p, csk_p, csv_p, cwk_p, cwv_p = [], [], [], [], [], []
    mk_p, mv_p = [], []
    ak_s, av_s, ai_s, bs_s, bc_s = [], [], [], [], []
    cck_s, ccv_s, csk_s, csv_s, cwk_s, cwv_s = [], [], [], [], [], []
    ia = ib = ic = 0
    for i in range(DEPTH):
        xp = ffn_residual(xp, g_ffn1[i], ffn1_wi[i], ffn1_wo[i])
        xs = ffn_residual(xs, g_ffn1[i], ffn1_wi[i], ffn1_wo[i])
        kind = i % N_MIXERS
        if kind == 0:
            op, (k1, v1, i1) = dsa_prompt(xp, g_mix[i], a_w_in[ia], a_w_out[ia])
            os_, (k2, v2, i2) = dsa_sample(xs, g_mix[i], a_w_in[ia], a_w_out[ia], cache_a_k, cache_a_v, cache_a_idx,
                                           ia, page_table)
            ak_p.append(k1); av_p.append(v1); ai_p.append(i1)
            ak_s.append(k2); av_s.append(v2); ai_s.append(i2)
            ia += 1
        elif kind == 1:
            conv0 = jnp.zeros((xp.shape[0], B_CONV - 1, B_CONV_DIM), xp.dtype)
            ssm0 = jnp.zeros((xp.shape[0], B_HEADS, B_HEADDIM, B_STATE), F32)
            wts = (b_w_in[ib], b_conv_w[ib], b_conv_b[ib], b_dt_bias[ib], b_a_log[ib], b_d_skip[ib],
                   b_g_norm[ib], b_w_out[ib])
            op, c1, s1 = mamba_prompt(xp, g_mix[i], *wts, conv0, ssm0)
            os_, c2, s2 = mamba_mixer(xs, g_mix[i], *wts, state_b_conv[ib], state_b_ssm[ib])
            bc_p.append(c1); bs_p.append(s1); bc_s.append(c2); bs_s.append(s2)
            ib += 1
        else:
            cmp_k = (c_pe_k[ic], c_w1_k[ic], c_w2_k[ic])
            cmp_v = (c_pe_v[ic], c_w1_v[ic], c_w2_v[ic])
            op, st1 = nsa_prompt(xp, g_mix[i], c_w_in[ic], c_w_out[ic], cmp_k, cmp_v, w_buf)
            os_, st2 = nsa_sample(xs, g_mix[i], c_w_in[ic], c_w_out[ic], cmp_k, cmp_v, cache_c_cmp_k, cache_c_cmp_v,
                                  cache_c_slc_k, cache_c_slc_v, cache_c_win_k, cache_c_win_v, ic, page_table)
            for lst, arr in zip((cck_p, ccv_p, csk_p, csv_p, cwk_p, cwv_p), st1):
                lst.append(arr)
            for lst, arr in zip((cck_s, ccv_s, csk_s, csv_s, cwk_s, cwv_s), st2):
                lst.append(arr)
            ic += 1
        xp, xs = op, os_
        mk, mv = mem_kv(mem_prompt, g_mem[i], x_w_kv[i])
        mk_p.append(mk); mv_p.append(mv)
        xp = cross_attn(xp, g_xattn[i], mk, mv, x_w_q[i], x_w_o[i])
        xs = cross_attn(xs, g_xattn[i], cache_mem_k[i], cache_mem_v[i], x_w_q[i], x_w_o[i])
        xp = ffn_residual(xp, g_ffn2[i], ffn2_wi[i], ffn2_wo[i])
        xs = ffn_residual(xs, g_ffn2[i], ffn2_wi[i], ffn2_wo[i])
    y_prompt = rms_norm_rows(xp, g_final)
    y_sample = rms_norm_rows(xs, g_final)
    st = jnp.stack
    return (y_prompt, y_sample,
            st(ak_p), st(av_p), st(ai_p), st(bs_p), st(bc_p),
            st(cck_p), st(ccv_p), st(csk_p), st(csv_p), st(cwk_p), st(cwv_p),
            st(mk_p), st(mv_p),
            st(ak_s), st(av_s), st(ai_s), st(bs_s), st(bc_s),
            st(cck_s), st(ccv_s), st(csk_s), st(csv_s), st(cwk_s), st(cwv_s))
```
